```python
import math
import jax, jax.numpy as jnp
from jax import lax
import numpy as np

D_MODEL = 2048
BATCH = 8
SEQ = 2048
DEPTH = 2

EPS = 1e-6
N_BRANCH = 3
POOL_WINDOWS = (2, 4, 8, 16)
POOL_WIDTH = D_MODEL // 4
POOL_GROUP = POOL_WIDTH // len(POOL_WINDOWS)
SB_HEAD_DIM = 128
SB_WIDTH = 3 * D_MODEL // 8
SB_HEADS = SB_WIDTH // SB_HEAD_DIM
SB_BLOCK = 128
GDN_HEAD_DIM = 128
GDN_WIDTH = 3 * D_MODEL // 8
GDN_HEADS = GDN_WIDTH // GDN_HEAD_DIM
GDN_CONV = 4
GDN_CHUNK = 64
D_FF = 4 * D_MODEL
IN_SIZES = (POOL_WIDTH, 3 * SB_WIDTH, 3 * GDN_WIDTH, GDN_WIDTH, GDN_HEADS, GDN_HEADS, N_BRANCH * D_MODEL)
N_IN = sum(IN_SIZES)

kernel_name = 'hybrid_pool_stickbreak_gdn_block'


def rms_norm(x, gain):
    x32 = x.astype(jnp.float32)
    y = x32 * lax.rsqrt(jnp.mean(x32 * x32, axis=-1, keepdims=True) + EPS)
    return (y * gain.astype(jnp.float32)).astype(x.dtype)


def l2_normalize(x):
    return x * lax.rsqrt(jnp.sum(x * x, axis=-1, keepdims=True) + EPS)


def pool_mixer(p, w_group, scale):
    b, s, _ = p.shape
    p32 = p.astype(jnp.float32)
    csum = jnp.cumsum(p32, axis=1)
    n_seen = jnp.arange(1, s + 1, dtype=jnp.float32)[None, :, None]
    groups = []
    for g, w in enumerate(POOL_WINDOWS):
        sl = slice(g * POOL_GROUP, (g + 1) * POOL_GROUP)
        cg = csum[..., sl]
        lagged = jnp.pad(cg[:, :s - w], ((0, 0), (w, 0), (0, 0)))
        groups.append((cg - lagged) / jnp.minimum(n_seen, w) - p32[..., sl])
    d = jnp.stack(groups, axis=2)
    y = jnp.einsum('bsgc,gcd->bsgd', d, w_group.astype(jnp.float32)).reshape(b, s, POOL_WIDTH)
    return (y * scale.astype(jnp.float32)).astype(p.dtype)


def stick_breaking_attention(q, k, v):
    b, s, h, dh = q.shape
    q32 = q.astype(jnp.float32) * (dh ** -0.5)
    k32 = k.astype(jnp.float32)
    v32 = v.astype(jnp.float32)
    outs = []
    for start in range(0, s, SB_BLOCK):
        end = start + SB_BLOCK
        z = jnp.einsum('bqhd,bkhd->bhqk', q32[:, start:end], k32[:, :end])
        mask = jnp.arange(end)[None, :] < jnp.arange(start, end)[:, None]
        log_stay = jnp.where(mask, jax.nn.log_sigmoid(-z), 0.0)
        log_later = lax.cumsum(log_stay, axis=3, reverse=True) - log_stay
        a = jnp.where(mask, jnp.exp(jax.nn.log_sigmoid(z) + log_later), 0.0)
        outs.append(jnp.einsum('bhqk,bkhd->bqhd', a, v32[:, :end]))
    return jnp.concatenate(outs, axis=1).astype(q.dtype)


def short_causal_conv(x, w):
    k, c = w.shape
    y = lax.conv_general_dilated(x, w[:, None, :], window_strides=(1,), padding=[(k - 1, 0)],
                                 dimension_numbers=('NWC', 'WIO', 'NWC'), feature_group_count=c)
    return jax.nn.silu(y)


def to_chunks(t):
    b, s, h = t.shape[:3]
    t = t.reshape((b, s // GDN_CHUNK, GDN_CHUNK, h) + t.shape[3:])
    return jnp.moveaxis(t, 3, 1)


def gated_delta_rule(q, k, v, log_alpha, beta):
    b, s, h, dk = q.shape
    dv = v.shape[-1]
    qc, kc, vc = to_chunks(q), to_chunks(k), to_chunks(v)
    g = jnp.cumsum(to_chunks(log_alpha), axis=-1)
    bc = to_chunks(beta)
    incl = jnp.tril(jnp.ones((GDN_CHUNK, GDN_CHUNK), dtype=bool))
    strict = jnp.tril(jnp.ones((GDN_CHUNK, GDN_CHUNK), dtype=bool), k=-1)
    diff = g[..., :, None] - g[..., None, :]
    gamma = jnp.where(incl, jnp.exp(jnp.where(incl, diff, 0.0)), 0.0)
    kk = jnp.einsum('bhncd,bhnmd->bhncm', kc, kc)
    lower = jnp.where(strict, bc[..., :, None] * kk * gamma, 0.0)
    unit_lower = lower + jnp.eye(GDN_CHUNK, dtype=lower.dtype)
    rhs = jnp.concatenate([vc * bc[..., None], kc * (bc * jnp.exp(g))[..., None]], axis=-1)
    sol = lax.linalg.triangular_solve(unit_lower, rhs, left_side=True, lower=True, unit_diagonal=True)
    u, w = sol[..., :dv], sol[..., dv:]
    qk = jnp.einsum('bhncd,bhnmd->bhncm', qc, kc) * gamma
    q_dec = qc * jnp.exp(g)[..., None]
    k_dec = kc * jnp.exp(g[..., -1:] - g)[..., None]
    chunk_decay = jnp.exp(g[..., -1])

    def step(state, inp):
        u_n, w_n, qk_n, qd_n, kd_n, dec_n = inp
        v_new = u_n - jnp.einsum('bhck,bhkv->bhcv', w_n, state)
        o_n = jnp.einsum('bhck,bhkv->bhcv', qd_n, state) + jnp.einsum('bhcm,bhmv->bhcv', qk_n, v_new)
        state = state * dec_n[..., None, None] + jnp.einsum('bhck,bhcv->bhkv', kd_n, v_new)
        return state, o_n

    xs = tuple(jnp.moveaxis(t, 2, 0) for t in (u, w, qk, q_dec, k_dec, chunk_decay))
    state0 = jnp.zeros((b, h, dk, dv), jnp.float32)
    _, o = lax.scan(step, state0, xs)
    return o.transpose(1, 0, 3, 2, 4).reshape(b, s, h, dv)


def gdn_mixer(qkv, z, a, bg, conv_w, a_log, dt_bias, norm_gain):
    b, s, _ = qkv.shape
    qkv = short_causal_conv(qkv, conv_w).astype(jnp.float32)
    q, k, v = jnp.split(qkv, 3, axis=-1)
    q = l2_normalize(q.reshape(b, s, GDN_HEADS, GDN_HEAD_DIM)) * (GDN_HEAD_DIM ** -0.5)
    k = l2_normalize(k.reshape(b, s, GDN_HEADS, GDN_HEAD_DIM))
    v = v.reshape(b, s, GDN_HEADS, GDN_HEAD_DIM)
    log_alpha = -jnp.exp(a_log.astype(jnp.float32)) * jax.nn.softplus(a.astype(jnp.float32) + dt_bias.astype(jnp.float32))
    beta = jax.nn.sigmoid(bg.astype(jnp.float32))
    o = gated_delta_rule(q, k, v, log_alpha, beta)
    zh = z.astype(jnp.float32).reshape(b, s, GDN_HEADS, GDN_HEAD_DIM)
    o = rms_norm(o, norm_gain) * jax.nn.silu(zh)
    return o.reshape(b, s, GDN_WIDTH).astype(z.dtype)


def hybrid_mixer(u, w_in, pool_w, pool_scale, gdn_conv, gdn_a_log, gdn_dt_bias, gdn_norm,
                 w_pool_up, w_sb_up, w_gdn_up, w_out):
    b, s, _ = u.shape
    proj = u @ w_in
    offsets = [int(o) for o in np.cumsum(IN_SIZES)[:-1]]
    p, sb_qkv, gdn_qkv, gdn_z, gdn_a, gdn_b, gates = jnp.split(proj, offsets, axis=-1)
    y_pool = pool_mixer(p, pool_w, pool_scale)
    sq, sk, sv = (t.reshape(b, s, SB_HEADS, SB_HEAD_DIM) for t in jnp.split(sb_qkv, 3, axis=-1))
    y_sb = stick_breaking_attention(sq, sk, sv).reshape(b, s, SB_WIDTH)
    y_gdn = gdn_mixer(gdn_qkv, gdn_z, gdn_a, gdn_b, gdn_conv, gdn_a_log, gdn_dt_bias, gdn_norm)
    g_pool, g_sb, g_gdn = jnp.split(jax.nn.sigmoid(gates), N_BRANCH, axis=-1)
    merged = g_pool * (y_pool @ w_pool_up) + g_sb * (y_sb @ w_sb_up) + g_gdn * (y_gdn @ w_gdn_up)
    return merged @ w_out


def squared_relu_mlp(u, w_ff1, w_ff2):
    h = jax.nn.relu(u @ w_ff1)
    return (h * h) @ w_ff2


def _fwd_setup_inputs(seed: int = 0) -> dict:
    key = jax.random.key(seed)
    ks = jax.random.split(key, 20)
    f32 = jnp.float32

    def nrm(k, shape, fan_in):
        return jax.random.normal(k, shape, f32) * (fan_in ** -0.5)

    def gain(k, shape):
        return 1.0 + 0.02 * jax.random.normal(k, shape, f32)

    dt = jnp.exp(jax.random.uniform(ks[7], (DEPTH, GDN_HEADS), f32, math.log(1e-3), math.log(1e-1)))
    return {
        'x': jax.random.normal(ks[0], (BATCH, SEQ, D_MODEL), f32),
        'attn_norm': gain(ks[1], (DEPTH, D_MODEL)),
        'w_in': nrm(ks[2], (DEPTH, D_MODEL, N_IN), D_MODEL),
        'pool_w': nrm(ks[3], (DEPTH, len(POOL_WINDOWS), POOL_GROUP, POOL_GROUP), POOL_GROUP),
        'pool_scale': gain(ks[4], (DEPTH, POOL_WIDTH)),
        'gdn_conv': nrm(ks[5], (DEPTH, GDN_CONV, 3 * GDN_WIDTH), GDN_CONV),
        'gdn_a_log': jnp.log(jax.random.uniform(ks[6], (DEPTH, GDN_HEADS), f32, 1.0, 16.0)),
        'gdn_dt_bias': dt + jnp.log(-jnp.expm1(-dt)),
        'gdn_norm': gain(ks[8], (DEPTH, GDN_HEAD_DIM)),
        'w_pool_up': nrm(ks[9], (DEPTH, POOL_WIDTH, D_MODEL), POOL_WIDTH),
        'w_sb_up': nrm(ks[10], (DEPTH, SB_WIDTH, D_MODEL), SB_WIDTH),
        'w_gdn_up': nrm(ks[11], (DEPTH, GDN_WIDTH, D_MODEL), GDN_WIDTH),
        'w_out': nrm(ks[12], (DEPTH, D_MODEL, D_MODEL), D_MODEL),
        'mlp_norm': gain(ks[13], (DEPTH, D_MODEL)),
        'w_ff1': nrm(ks[14], (DEPTH, D_MODEL, D_FF), D_MODEL),
        'w_ff2': nrm(ks[15], (DEPTH, D_FF, D_MODEL), D_FF),
        'final_norm': gain(ks[16], (D_MODEL,)),
    }


def _fwd_reference(x, attn_norm, w_in, pool_w, pool_scale, gdn_conv, gdn_a_log, gdn_dt_bias, gdn_norm,
              w_pool_up, w_sb_up, w_gdn_up, w_out, mlp_norm, w_ff1, w_ff2, final_norm):
    for l in range(DEPTH):
        u = rms_norm(x, attn_norm[l])
        x = x + hybrid_mixer(u, w_in[l], pool_w[l], pool_scale[l], gdn_conv[l], gdn_a_log[l],
                             gdn_dt_bias[l], gdn_norm[l], w_pool_up[l], w_sb_up[l], w_gdn_up[l], w_out[l])
        x = x + squared_relu_mlp(rms_norm(x, mlp_norm[l]), w_ff1[l], w_ff2[l])
    return rms_norm(x, final_norm)


import jax as _jax
import jax.numpy as _jnp

TWIN_FORMAT = 'train_step'
FWD_PARAMS = ['x', 'attn_norm', 'w_in', 'pool_w', 'pool_scale', 'gdn_conv', 'gdn_a_log', 'gdn_dt_bias', 'gdn_norm', 'w_pool_up', 'w_sb_up', 'w_gdn_up', 'w_out', 'mlp_norm', 'w_ff1', 'w_ff2', 'final_norm']
TWIN_WEIGHTS = ['attn_norm', 'w_in', 'pool_w', 'pool_scale', 'gdn_conv', 'gdn_a_log', 'gdn_dt_bias', 'gdn_norm', 'w_pool_up', 'w_sb_up', 'w_gdn_up', 'w_out', 'mlp_norm', 'w_ff1', 'w_ff2', 'final_norm']
TWIN_DIFF_INPUT = 'x'
TWIN_INPUTS = ['x', 'attn_norm', 'w_in', 'pool_w', 'pool_scale', 'gdn_conv', 'gdn_a_log', 'gdn_dt_bias', 'gdn_norm', 'w_pool_up', 'w_sb_up', 'w_gdn_up', 'w_out', 'mlp_norm', 'w_ff1', 'w_ff2', 'final_norm', 'loss_target', 'm_attn_norm', 'm_w_in', 'm_pool_w', 'm_pool_scale', 'm_gdn_conv', 'm_gdn_a_log', 'm_gdn_dt_bias', 'm_gdn_norm', 'm_w_pool_up', 'm_w_sb_up', 'm_w_gdn_up', 'm_w_out', 'm_mlp_norm', 'm_w_ff1', 'm_w_ff2', 'm_final_norm', 'v_attn_norm', 'v_w_in', 'v_pool_w', 'v_pool_scale', 'v_gdn_conv', 'v_gdn_a_log', 'v_gdn_dt_bias', 'v_gdn_norm', 'v_w_pool_up', 'v_w_sb_up', 'v_w_gdn_up', 'v_w_out', 'v_mlp_norm', 'v_w_ff1', 'v_w_ff2', 'v_final_norm']
TWIN_OUTPUTS = ['loss', 'grad_x', 'grad_attn_norm', 'grad_w_in', 'grad_pool_w', 'grad_pool_scale', 'grad_gdn_conv', 'grad_gdn_a_log', 'grad_gdn_dt_bias', 'grad_gdn_norm', 'grad_w_pool_up', 'grad_w_sb_up', 'grad_w_gdn_up', 'grad_w_out', 'grad_mlp_norm', 'grad_w_ff1', 'grad_w_ff2', 'grad_final_norm', 'delta_attn_norm', 'delta_w_in', 'delta_pool_w', 'delta_pool_scale', 'delta_gdn_conv', 'delta_gdn_a_log', 'delta_gdn_dt_bias', 'delta_gdn_norm', 'delta_w_pool_up', 'delta_w_sb_up', 'delta_w_gdn_up', 'delta_w_out', 'delta_mlp_norm', 'delta_w_ff1', 'delta_w_ff2', 'delta_final_norm', 'new_m_attn_norm', 'new_m_w_in', 'new_m_pool_w', 'new_m_pool_scale', 'new_m_gdn_conv', 'new_m_gdn_a_log', 'new_m_gdn_dt_bias', 'new_m_gdn_norm', 'new_m_w_pool_up', 'new_m_w_sb_up', 'new_m_w_gdn_up', 'new_m_w_out', 'new_m_mlp_norm', 'new_m_w_ff1', 'new_m_w_ff2', 'new_m_final_norm', 'new_v_attn_norm', 'new_v_w_in', 'new_v_pool_w', 'new_v_pool_scale', 'new_v_gdn_conv', 'new_v_gdn_a_log', 'new_v_gdn_dt_bias', 'new_v_gdn_norm', 'new_v_w_pool_up', 'new_v_w_sb_up', 'new_v_w_gdn_up', 'new_v_w_out', 'new_v_mlp_norm', 'new_v_w_ff1', 'new_v_w_ff2', 'new_v_final_norm']
TWIN_LEAF_KINDS = {'loss': 'loss', 'grad_x': 'grad_x', 'grad_attn_norm': 'grad_w', 'grad_w_in': 'grad_w', 'grad_pool_w': 'grad_w', 'grad_pool_scale': 'grad_w', 'grad_gdn_conv': 'grad_w', 'grad_gdn_a_log': 'grad_w', 'grad_gdn_dt_bias': 'grad_w', 'grad_gdn_norm': 'grad_w', 'grad_w_pool_up': 'grad_w', 'grad_w_sb_up': 'grad_w', 'grad_w_gdn_up': 'grad_w', 'grad_w_out': 'grad_w', 'grad_mlp_norm': 'grad_w', 'grad_w_ff1': 'grad_w', 'grad_w_ff2': 'grad_w', 'grad_final_norm': 'grad_w', 'delta_attn_norm': 'delta_w', 'delta_w_in': 'delta_w', 'delta_pool_w': 'delta_w', 'delta_pool_scale': 'delta_w', 'delta_gdn_conv': 'delta_w', 'delta_gdn_a_log': 'delta_w', 'delta_gdn_dt_bias': 'delta_w', 'delta_gdn_norm': 'delta_w', 'delta_w_pool_up': 'delta_w', 'delta_w_sb_up': 'delta_w', 'delta_w_gdn_up': 'delta_w', 'delta_w_out': 'delta_w', 'delta_mlp_norm': 'delta_w', 'delta_w_ff1': 'delta_w', 'delta_w_ff2': 'delta_w', 'delta_final_norm': 'delta_w', 'new_m_attn_norm': 'new_m', 'new_m_w_in': 'new_m', 'new_m_pool_w': 'new_m', 'new_m_pool_scale': 'new_m', 'new_m_gdn_conv': 'new_m', 'new_m_gdn_a_log': 'new_m', 'new_m_gdn_dt_bias': 'new_m', 'new_m_gdn_norm': 'new_m', 'new_m_w_pool_up': 'new_m', 'new_m_w_sb_up': 'new_m', 'new_m_w_gdn_up': 'new_m', 'new_m_w_out': 'new_m', 'new_m_mlp_norm': 'new_m', 'new_m_w_ff1': 'new_m', 'new_m_w_ff2': 'new_m', 'new_m_final_norm': 'new_m', 'new_v_attn_norm': 'new_v', 'new_v_w_in': 'new_v', 'new_v_pool_w': 'new_v', 'new_v_pool_scale': 'new_v', 'new_v_gdn_conv': 'new_v', 'new_v_gdn_a_log': 'new_v', 'new_v_gdn_dt_bias': 'new_v', 'new_v_gdn_norm': 'new_v', 'new_v_w_pool_up': 'new_v', 'new_v_w_sb_up': 'new_v', 'new_v_w_gdn_up': 'new_v', 'new_v_w_out': 'new_v', 'new_v_mlp_norm': 'new_v', 'new_v_w_ff1': 'new_v', 'new_v_w_ff2': 'new_v', 'new_v_final_norm': 'new_v'}


def _forward(args):
    return _fwd_reference(*[args[k] for k in FWD_PARAMS])


def _output_shape():
    out = _jax.eval_shape(lambda: _forward(_fwd_setup_inputs(0)))
    return out.shape, out.dtype

N_MICROBATCH = 1
ADAM_LR = 0.001
ADAM_B1 = 0.9
ADAM_B2 = 0.999
ADAM_EPS = 1e-08
ADAM_WD = 0.01
ADAM_STEP = 10
PER_EXAMPLE_BATCH_AXIS = {'x': 0, 'loss_target': 0}
SHARED_INPUTS = []
_WEIGHT_DTYPES = {'attn_norm': _jnp.float32, 'w_in': _jnp.float32, 'pool_w': _jnp.float32, 'pool_scale': _jnp.float32, 'gdn_conv': _jnp.float32, 'gdn_a_log': _jnp.float32, 'gdn_dt_bias': _jnp.float32, 'gdn_norm': _jnp.float32, 'w_pool_up': _jnp.float32, 'w_sb_up': _jnp.float32, 'w_gdn_up': _jnp.float32, 'w_out': _jnp.float32, 'mlp_norm': _jnp.float32, 'w_ff1': _jnp.float32, 'w_ff2': _jnp.float32, 'final_norm': _jnp.float32}
MOMENT_SCALE = {'attn_norm': 4.468592e-02, 'w_in': 1.817930e-02, 'pool_w': 4.725559e-02, 'pool_scale': 4.859260e-02, 'gdn_conv': 2.068776e-02, 'gdn_a_log': 1.605215e-01, 'gdn_dt_bias': 1.601903e-01, 'gdn_norm': 7.854549e-02, 'w_pool_up': 2.359943e-02, 'w_sb_up': 1.831758e-02, 'w_gdn_up': 1.626801e-02, 'w_out': 3.407532e-02, 'mlp_norm': 4.992775e-02, 'w_ff1': 2.468161e-02, 'w_ff2': 4.706445e-02, 'final_norm': 8.127525e+00}


def _to_microbatches(a, axis):
    t = _jnp.moveaxis(a, axis, 0)
    t = t.reshape((N_MICROBATCH, t.shape[0] // N_MICROBATCH) + t.shape[1:])
    return _jnp.moveaxis(t, 1, axis + 1)


def setup_inputs(seed: int = 0) -> dict:
    inp = _fwd_setup_inputs(seed)
    key = _jax.random.fold_in(_jax.random.key(seed), 7919)
    shape, _ = _output_shape()
    out = dict(inp)
    out["loss_target"] = _jax.random.normal(_jax.random.fold_in(key, 0), shape, _jnp.float32)
    for i, name in enumerate(TWIN_WEIGHTS):
        w = inp[name].astype(_jnp.float32)
        if MOMENT_SCALE is None:
            s = _jnp.sqrt(_jnp.mean(_jnp.square(w)) + 1e-30)
        else:
            s = MOMENT_SCALE[name]
        km, kv = _jax.random.split(_jax.random.fold_in(key, i + 1))
        out[name] = w
        out["m_" + name] = s * _jax.random.normal(km, w.shape, _jnp.float32)
        out["v_" + name] = (s * s) * _jax.random.uniform(kv, w.shape, _jnp.float32, 0.5, 1.5)
    if N_MICROBATCH > 1:
        for name, axis in PER_EXAMPLE_BATCH_AXIS.items():
            out[name] = _to_microbatches(out[name], axis)
    return {'x': out['x'], 'attn_norm': out['attn_norm'], 'w_in': out['w_in'], 'pool_w': out['pool_w'], 'pool_scale': out['pool_scale'], 'gdn_conv': out['gdn_conv'], 'gdn_a_log': out['gdn_a_log'], 'gdn_dt_bias': out['gdn_dt_bias'], 'gdn_norm': out['gdn_norm'], 'w_pool_up': out['w_pool_up'], 'w_sb_up': out['w_sb_up'], 'w_gdn_up': out['w_gdn_up'], 'w_out': out['w_out'], 'mlp_norm': out['mlp_norm'], 'w_ff1': out['w_ff1'], 'w_ff2': out['w_ff2'], 'final_norm': out['final_norm'], 'loss_target': out['loss_target'], 'm_attn_norm': out['m_attn_norm'], 'm_w_in': out['m_w_in'], 'm_pool_w': out['m_pool_w'], 'm_pool_scale': out['m_pool_scale'], 'm_gdn_conv': out['m_gdn_conv'], 'm_gdn_a_log': out['m_gdn_a_log'], 'm_gdn_dt_bias': out['m_gdn_dt_bias'], 'm_gdn_norm': out['m_gdn_norm'], 'm_w_pool_up': out['m_w_pool_up'], 'm_w_sb_up': out['m_w_sb_up'], 'm_w_gdn_up': out['m_w_gdn_up'], 'm_w_out': out['m_w_out'], 'm_mlp_norm': out['m_mlp_norm'], 'm_w_ff1': out['m_w_ff1'], 'm_w_ff2': out['m_w_ff2'], 'm_final_norm': out['m_final_norm'], 'v_attn_norm': out['v_attn_norm'], 'v_w_in': out['v_w_in'], 'v_pool_w': out['v_pool_w'], 'v_pool_scale': out['v_pool_scale'], 'v_gdn_conv': out['v_gdn_conv'], 'v_gdn_a_log': out['v_gdn_a_log'], 'v_gdn_dt_bias': out['v_gdn_dt_bias'], 'v_gdn_norm': out['v_gdn_norm'], 'v_w_pool_up': out['v_w_pool_up'], 'v_w_sb_up': out['v_w_sb_up'], 'v_w_gdn_up': out['v_w_gdn_up'], 'v_w_out': out['v_w_out'], 'v_mlp_norm': out['v_mlp_norm'], 'v_w_ff1': out['v_w_ff1'], 'v_w_ff2': out['v_w_ff2'], 'v_final_norm': out['v_final_norm']}


def _loss(weights, diff, rest, loss_target):
    with _jax.named_scope("forward"):
        args = {**rest, TWIN_DIFF_INPUT: diff, **{k: w.astype(_WEIGHT_DTYPES[k]) for k, w in weights.items()}}
        y = _forward(args)
    with _jax.named_scope("loss_head"):
        err = _jnp.square(y.astype(_jnp.float32) - loss_target)
        return 0.5 * _jnp.sum(_jnp.mean(err, axis=-1)) if err.ndim else 0.5 * err


def _adamw(w, g, m, v):
    m = ADAM_B1 * m + (1.0 - ADAM_B1) * g
    v = ADAM_B2 * v + (1.0 - ADAM_B2) * _jnp.square(g)
    m_hat = m / (1.0 - ADAM_B1 ** ADAM_STEP)
    v_hat = v / (1.0 - ADAM_B2 ** ADAM_STEP)
    delta = -ADAM_LR * (m_hat / (_jnp.sqrt(v_hat) + ADAM_EPS) + ADAM_WD * w)
    return delta, m, v


def reference(x, attn_norm, w_in, pool_w, pool_scale, gdn_conv, gdn_a_log, gdn_dt_bias, gdn_norm, w_pool_up, w_sb_up, w_gdn_up, w_out, mlp_norm, w_ff1, w_ff2, final_norm, loss_target, m_attn_norm, m_w_in, m_pool_w, m_pool_scale, m_gdn_conv, m_gdn_a_log, m_gdn_dt_bias, m_gdn_norm, m_w_pool_up, m_w_sb_up, m_w_gdn_up, m_w_out, m_mlp_norm, m_w_ff1, m_w_ff2, m_final_norm, v_attn_norm, v_w_in, v_pool_w, v_pool_scale, v_gdn_conv, v_gdn_a_log, v_gdn_dt_bias, v_gdn_norm, v_w_pool_up, v_w_sb_up, v_w_gdn_up, v_w_out, v_mlp_norm, v_w_ff1, v_w_ff2, v_final_norm):
    given = dict(x=x, attn_norm=attn_norm, w_in=w_in, pool_w=pool_w, pool_scale=pool_scale, gdn_conv=gdn_conv, gdn_a_log=gdn_a_log, gdn_dt_bias=gdn_dt_bias, gdn_norm=gdn_norm, w_pool_up=w_pool_up, w_sb_up=w_sb_up, w_gdn_up=w_gdn_up, w_out=w_out, mlp_norm=mlp_norm, w_ff1=w_ff1, w_ff2=w_ff2, final_norm=final_norm, loss_target=loss_target, m_attn_norm=m_attn_norm, m_w_in=m_w_in, m_pool_w=m_pool_w, m_pool_scale=m_pool_scale, m_gdn_conv=m_gdn_conv, m_gdn_a_log=m_gdn_a_log, m_gdn_dt_bias=m_gdn_dt_bias, m_gdn_norm=m_gdn_norm, m_w_pool_up=m_w_pool_up, m_w_sb_up=m_w_sb_up, m_w_gdn_up=m_w_gdn_up, m_w_out=m_w_out, m_mlp_norm=m_mlp_norm, m_w_ff1=m_w_ff1, m_w_ff2=m_w_ff2, m_final_norm=m_final_norm, v_attn_norm=v_attn_norm, v_w_in=v_w_in, v_pool_w=v_pool_w, v_pool_scale=v_pool_scale, v_gdn_conv=v_gdn_conv, v_gdn_a_log=v_gdn_a_log, v_gdn_dt_bias=v_gdn_dt_bias, v_gdn_norm=v_gdn_norm, v_w_pool_up=v_w_pool_up, v_w_sb_up=v_w_sb_up, v_w_gdn_up=v_w_gdn_up, v_w_out=v_w_out, v_mlp_norm=v_mlp_norm, v_w_ff1=v_w_ff1, v_w_ff2=v_w_ff2, v_final_norm=v_final_norm)
    weights = {n: given[n] for n in TWIN_WEIGHTS}
    shared = {n: given[n] for n in SHARED_INPUTS}
    per_example = {n: given[n] for n in ['x']}
    grad_fn = _jax.value_and_grad(_loss, argnums=(0, 1))

    def one_microbatch(ex, loss_target):
        ex = dict(ex)
        diff = ex.pop(TWIN_DIFF_INPUT)
        return grad_fn(weights, diff, {**shared, **ex}, loss_target)

    if N_MICROBATCH == 1:
        loss, (grad_w, grad_x) = one_microbatch(per_example, given["loss_target"])
    else:
        def body(carry, xs):
            loss_sum, grad_sum = carry
            l_k, (gw_k, gx_k) = one_microbatch(xs[0], xs[1])
            with _jax.named_scope("update"):
                return (loss_sum + l_k, _jax.tree.map(_jnp.add, grad_sum, gw_k)), gx_k

        init = (_jnp.zeros((), _jnp.float32), _jax.tree.map(_jnp.zeros_like, weights))
        (loss, grad_w), grad_x = _jax.lax.scan(body, init, (per_example, given["loss_target"]))
    with _jax.named_scope("update"):
        delta_w, new_m, new_v = {}, {}, {}
        for n in TWIN_WEIGHTS:
            delta_w[n], new_m[n], new_v[n] = _adamw(weights[n], grad_w[n], given["m_" + n], given["v_" + n])
    return (loss, grad_x, *[grad_w[n] for n in TWIN_WEIGHTS], *[delta_w[n] for n in TWIN_WEIGHTS],
            *[new_m[n] for n in TWIN_WEIGHTS], *[new_v[n] for n in TWIN_WEIGHTS])
```

```python
import functools
import math

import jax
import jax.numpy as jnp
from jax import lax
from jax.experimental import pallas as pl
from jax.experimental.pallas import tpu as pltpu

f32, bf16 = jnp.float32, jnp.bfloat16

D = 2048
NDEV = 8
NL = 2
HD = 128
NH = 6
WH = NH * HD
W_POOL = 512
EPS = 1e-6
N_IN = 12044
NP = 12288
OFF_G, OFF_SB, OFF_GQ, OFF_Z, OFF_P, OFF_AB = 0, 6144, 8448, 10752, 11520, 12032
POOL_WINDOWS = (2, 4, 8, 16)
CH = 128
TQ = 256
VMEM_LIMIT = 56 * 1024 * 1024
ADAM_LR, ADAM_B1, ADAM_B2, ADAM_EPS, ADAM_WD, ADAM_STEP = 0.001, 0.9, 0.999, 1e-08, 0.01, 10
MESH = pl.DeviceIdType.MESH


def _cp(sem=None):
    return pltpu.CompilerParams(dimension_semantics=sem, vmem_limit_bytes=VMEM_LIMIT)


def _sds(shape, dtype):
    return jax.ShapeDtypeStruct(tuple(shape), dtype)


def matmul(a, b, *, name, ta=False, tb=False, out_dtype=f32, tm=1024, tn=1024, tk=2048, epilogue=None, extras=()):
    m, k = (a.shape[1], a.shape[0]) if ta else a.shape
    n = b.shape[0] if tb else b.shape[1]
    assert k == (b.shape[1] if tb else b.shape[0]) and a.dtype == bf16 and b.dtype == bf16
    tm, tn, tk = min(tm, m), min(tn, n), min(tk, k)
    assert m % tm == 0 and n % tn == 0 and k % tk == 0, (m, n, k, tm, tn, tk)
    nk = k // tk
    a_spec = pl.BlockSpec((tk, tm), lambda i, j, q: (q, i)) if ta else pl.BlockSpec((tm, tk), lambda i, j, q: (i, q))
    b_spec = pl.BlockSpec((tn, tk), lambda i, j, q: (j, q)) if tb else pl.BlockSpec((tk, tn), lambda i, j, q: (q, j))
    e_specs = [pl.BlockSpec((tm, tn), lambda i, j, q: (i, j)) for _ in extras]
    dn = (((0 if ta else 1,), (1 if tb else 0,)), ((), ()))
    ne = len(extras)

    def body(a_ref, b_ref, *rest):
        e_refs, o_ref = rest[:ne], rest[ne]
        part = lax.dot_general(a_ref[...], b_ref[...], dn, preferred_element_type=f32)

        def finish(acc):
            if epilogue is not None:
                acc = epilogue(acc, *[e[...] for e in e_refs])
            o_ref[...] = acc.astype(out_dtype)

        if nk == 1:
            finish(part)
        else:
            acc_ref = rest[ne + 1]
            q = pl.program_id(2)

            @pl.when(q == 0)
            def _():
                acc_ref[...] = part

            @pl.when(jnp.logical_and(q > 0, q < nk - 1))
            def _():
                acc_ref[...] += part

            @pl.when(q == nk - 1)
            def _():
                finish(acc_ref[...] + part)

    return pl.pallas_call(
        body, name=name, grid=(m // tm, n // tn, nk),
        in_specs=[a_spec, b_spec] + e_specs, out_specs=pl.BlockSpec((tm, tn), lambda i, j, q: (i, j)),
        out_shape=_sds((m, n), out_dtype),
        scratch_shapes=[pltpu.VMEM((tm, tn), f32)] if nk > 1 else [],
        compiler_params=_cp(("parallel", "parallel", "arbitrary")),
    )(a, b, *extras)


def rowwise(name, fn, rows, params, outs, sums=(), tr=256):
    s = rows[0][0].shape[0]
    tr = min(tr, s)
    nin, nout = len(rows) + len(params), len(outs)
    in_specs = [pl.BlockSpec((tr, w), functools.partial(lambda i, c: (i, c), c=c)) for (_, w, c) in rows]
    in_specs += [pl.BlockSpec(p.shape, lambda i: (0, 0)) for p in params]
    out_specs = [pl.BlockSpec((tr, w), lambda i: (i, 0)) for (w, _) in outs]
    out_specs += [pl.BlockSpec(sh, lambda i: (0, 0)) for sh in sums]
    out_shape = [_sds((s, w), dt) for (w, dt) in outs] + [_sds(sh, f32) for sh in sums]

    def body(*refs):
        res = fn(*[r[...] for r in refs[:nin]])
        for r, v in zip(refs[nin:nin + nout], res[:nout]):
            r[...] = v.astype(r.dtype)
        i = pl.program_id(0)
        for r, v in zip(refs[nin + nout:], res[nout:]):
            @pl.when(i == 0)
            def _(r=r, v=v):
                r[...] = v

            @pl.when(i > 0)
            def _(r=r, v=v):
                r[...] += v

    res = pl.pallas_call(
        body, name=name, grid=(s // tr,), in_specs=in_specs, out_specs=out_specs, out_shape=out_shape,
        compiler_params=_cp(("arbitrary",)),
    )(*[r[0] for r in rows], *params)
    return res


def _rms(x, g):
    return x * lax.rsqrt(jnp.mean(x * x, axis=-1, keepdims=True) + EPS) * g


def rms_fwd(x, g, name):
    return rowwise(name, lambda xb, gb: (_rms(xb, gb),), [(x, D, 0)], [g], [(D, bf16)])[0]


def rms_bwd(x, du, dres, g, name):
    def fn(xb, dub, drb, gb):
        _, vjp = jax.vjp(_rms, xb, gb)
        dx, dg = vjp(dub.astype(f32))
        return drb + dx, drb + dx, dg

    return rowwise(name, fn, [(x, D, 0), (du, D, 0), (dres, D, 0)], [g], [(D, f32), (D, bf16)], [(1, D)])


def _merge(gates, up_p, up_s, up_g):
    sg = jax.nn.sigmoid(gates)
    return sg[:, :D] * up_p + sg[:, D:2 * D] * up_s + sg[:, 2 * D:] * up_g


def merge_fwd(proj, ups, name):
    return rowwise(name, lambda g, a, b, c: (_merge(g, a, b, c),),
                   [(proj, 3 * D, 0)] + [(u, D, 0) for u in ups], [], [(D, bf16)], tr=128)[0]


def merge_bwd(proj, ups, dmerged, name):
    def fn(g, a, b, c, dm):
        _, vjp = jax.vjp(_merge, g, a, b, c)
        return vjp(dm.astype(f32))

    return rowwise(name, fn, [(proj, 3 * D, 0)] + [(u, D, 0) for u in ups] + [(dmerged, D, 0)], [],
                   [(3 * D, bf16), (D, bf16), (D, bf16), (D, bf16)], tr=128)


def _gdn_out(o, z, g):
    ys = []
    for h in range(NH):
        sl = slice(h * HD, (h + 1) * HD)
        ys.append(_rms(o[:, sl], g) * jax.nn.silu(z[:, sl]))
    return jnp.concatenate(ys, axis=1)


def gdn_out_fwd(o, proj, g, name):
    return rowwise(name, lambda ob, zb, gb: (_gdn_out(ob, zb, gb),), [(o, WH, 0), (proj, WH, OFF_Z // WH)], [g],
                   [(WH, bf16)])[0]


def gdn_out_bwd(o, proj, g, dy, name):
    def fn(ob, zb, dyb, gb):
        _, vjp = jax.vjp(_gdn_out, ob, zb, gb)
        return vjp(dyb.astype(f32))

    return rowwise(name, fn, [(o, WH, 0), (proj, WH, OFF_Z // WH), (dy, WH, 0)], [g], [(WH, f32), (WH, bf16)],
                   [(1, HD)])


def loss_head(x, target, g, name):
    def loss_fn(xb, gb, tb):
        err = _rms(xb, gb) - tb
        return (0.5 / D) * jnp.sum(jnp.sum(err * err, axis=1, keepdims=True), axis=0, keepdims=True)

    def fn(xb, tb, gb):
        val, vjp = jax.vjp(functools.partial(loss_fn, tb=tb), xb, gb)
        dx, dg = vjp(jnp.ones((1, 1), f32))
        return dx, dx, jnp.broadcast_to(val, (1, HD)), dg

    return rowwise(name, fn, [(x, D, 0), (target, D, 0)], [g], [(D, f32), (D, bf16)], [(1, HD), (1, D)])


PB = 256


def _split(v):
    hi = v.astype(bf16)
    return hi, (v - hi.astype(f32)).astype(bf16)


def _band_dot(make_band, v, s, forward):
    hi, lo = _split(v)
    nb = s // PB
    outs = []
    for r in range(nb):
        lo_r = max(r - 1, 0) if forward else r
        hi_r = r + 1 if forward else min(r + 2, nb)
        band = make_band(r * PB, lo_r * PB, (hi_r - lo_r) * PB)
        sl = slice(lo_r * PB, hi_r * PB)
        outs.append(jnp.dot(band, hi[sl], preferred_element_type=f32) + jnp.dot(band, lo[sl], preferred_element_type=f32))
    return jnp.concatenate(outs, axis=0)


def _pool_common(p, win, s):
    def band(row0, col0, ncol):
        t = row0 + lax.broadcasted_iota(jnp.int32, (PB, ncol), 0)
        u = col0 + lax.broadcasted_iota(jnp.int32, (PB, ncol), 1)
        return jnp.logical_and(u <= t, t < u + win).astype(bf16)

    def band_t(row0, col0, ncol):
        u = row0 + lax.broadcasted_iota(jnp.int32, (PB, ncol), 0)
        t = col0 + lax.broadcasted_iota(jnp.int32, (PB, ncol), 1)
        return jnp.logical_and(u <= t, t < u + win).astype(bf16)

    t = lax.broadcasted_iota(jnp.int32, (s, 1), 0)
    inv_n = 1.0 / jnp.minimum(t + 1, win).astype(f32)
    d = _band_dot(band, p, s, True) * inv_n - p
    return d, inv_n, band_t


def pool_fwd(proj, pool_w, pool_scale, name):
    s = proj.shape[0]

    def body(p_ref, w_ref, sc_ref, y_ref):
        win = jnp.left_shift(2, pl.program_id(0))
        d, _, _ = _pool_common(p_ref[...], win, s)
        y = jnp.dot(d.astype(bf16), w_ref[...].astype(bf16), preferred_element_type=f32) * sc_ref[...]
        y_ref[...] = y.astype(bf16)

    return pl.pallas_call(
        body, name=name, grid=(4,),
        in_specs=[pl.BlockSpec((s, HD), lambda g: (0, OFF_P // HD + g)), pl.BlockSpec((None, HD, HD), lambda g: (g, 0, 0)),
                  pl.BlockSpec((1, HD), lambda g: (0, g))],
        out_specs=pl.BlockSpec((s, HD), lambda g: (0, g)), out_shape=_sds((s, W_POOL), bf16),
        compiler_params=_cp(("arbitrary",)),
    )(proj, pool_w, pool_scale)


def pool_bwd(proj, pool_w, pool_scale, dy, name):
    s = proj.shape[0]

    def body(p_ref, w_ref, sc_ref, dy_ref, dp_ref, dw_ref, dsc_ref):
        win = jnp.left_shift(2, pl.program_id(0))
        d, inv_n, band_t = _pool_common(p_ref[...], win, s)
        w = w_ref[...].astype(bf16)
        dyf = dy_ref[...].astype(f32)
        dsc_ref[...] = jnp.sum(dyf * jnp.dot(d.astype(bf16), w, preferred_element_type=f32), axis=0, keepdims=True)
        dys = (dyf * sc_ref[...]).astype(bf16)
        dd = lax.dot_general(dys, w, (((1,), (1,)), ((), ())), preferred_element_type=f32)
        dw_ref[...] = lax.dot_general(d.astype(bf16), dys, (((0,), (0,)), ((), ())), preferred_element_type=f32)
        dp_ref[...] = (_band_dot(band_t, dd * inv_n, s, False) - dd).astype(bf16)

    return pl.pallas_call(
        body, name=name, grid=(4,),
        in_specs=[pl.BlockSpec((s, HD), lambda g: (0, OFF_P // HD + g)), pl.BlockSpec((None, HD, HD), lambda g: (g, 0, 0)),
                  pl.BlockSpec((1, HD), lambda g: (0, g)), pl.BlockSpec((s, HD), lambda g: (0, g))],
        out_specs=[pl.BlockSpec((s, HD), lambda g: (0, g)), pl.BlockSpec((None, HD, HD), lambda g: (g, 0, 0)),
                   pl.BlockSpec((1, HD), lambda g: (0, g))],
        out_shape=[_sds((s, W_POOL), bf16), _sds((4, HD, HD), f32), _sds((1, W_POOL), f32)],
        compiler_params=_cp(("arbitrary",)),
    )(proj, pool_w, pool_scale, dy)


def _sb_tile(q, k_ref, kb, qi, carry):
    k = k_ref[pl.ds(pl.multiple_of(kb * TQ, TQ), TQ), :].astype(bf16)
    z = lax.dot_general(q, k, (((1,), (1,)), ((), ())), preferred_element_type=f32)
    row = qi * TQ + lax.broadcasted_iota(jnp.int32, (TQ, TQ), 0)
    col = kb * TQ + lax.broadcasted_iota(jnp.int32, (TQ, TQ), 1)
    mask = col < row
    ls = jnp.where(mask, jax.nn.log_sigmoid(-z), 0.0)
    j = lax.broadcasted_iota(jnp.int32, (TQ, TQ), 0)
    u = lax.broadcasted_iota(jnp.int32, (TQ, TQ), 1)
    later = (j > u).astype(bf16)
    hi, lo = _split(ls)
    lw = jnp.dot(hi, later, preferred_element_type=f32) + jnp.dot(lo, later, preferred_element_type=f32)
    a = jnp.where(mask, jnp.exp(ls + z + lw + carry), 0.0)
    return z, mask, ls, a


def sb_fwd(proj, name):
    s = proj.shape[0]
    nq = s // TQ
    scale = HD ** -0.5

    def body(q_ref, k_ref, v_ref, y_ref):
        qi = pl.program_id(1)
        q = (q_ref[...] * scale).astype(bf16)

        def step(j, c):
            acc, carry = c
            kb = qi - j
            _, _, ls, a = _sb_tile(q, k_ref, kb, qi, carry)
            v = v_ref[pl.ds(pl.multiple_of(kb * TQ, TQ), TQ), :].astype(bf16)
            acc = acc + jnp.dot(a.astype(bf16), v, preferred_element_type=f32)
            return acc, carry + jnp.sum(ls, axis=1, keepdims=True)

        acc, _ = lax.fori_loop(0, qi + 1, step, (jnp.zeros((TQ, HD), f32), jnp.zeros((TQ, 1), f32)))
        y_ref[...] = acc.astype(bf16)

    c0 = OFF_SB // HD
    return pl.pallas_call(
        body, name=name, grid=(NH, nq),
        in_specs=[pl.BlockSpec((TQ, HD), lambda h, i: (i, c0 + h)), pl.BlockSpec((s, HD), lambda h, i: (0, c0 + NH + h)),
                  pl.BlockSpec((s, HD), lambda h, i: (0, c0 + 2 * NH + h))],
        out_specs=pl.BlockSpec((TQ, HD), lambda h, i: (i, h)), out_shape=_sds((s, WH), bf16),
        compiler_params=_cp(("arbitrary", "arbitrary")),
    )(proj, proj, proj)


def sb_bwd(proj, dy, name):
    s = proj.shape[0]
    nq = s // TQ
    scale = HD ** -0.5

    def body(q_ref, k_ref, v_ref, do_ref, dq_ref, dk_ref, dv_ref, e_scr, z_scr, dk_acc, dv_acc):
        qi = pl.program_id(1)
        q = (q_ref[...] * scale).astype(bf16)
        do = do_ref[...]

        @pl.when(qi == 0)
        def _():
            dk_acc[...] = jnp.zeros_like(dk_acc)
            dv_acc[...] = jnp.zeros_like(dv_acc)

        def sweep_back(j, carry):
            kb = qi - j
            rows = pl.ds(pl.multiple_of(kb * TQ, TQ), TQ)
            z, _, ls, a = _sb_tile(q, k_ref, kb, qi, carry)
            v = v_ref[rows, :].astype(bf16)
            da = lax.dot_general(do, v, (((1,), (1,)), ((), ())), preferred_element_type=f32)
            e_scr[kb] = da * a
            z_scr[kb] = z
            dv_acc[rows, :] += lax.dot_general(a.astype(bf16), do, (((0,), (0,)), ((), ())), preferred_element_type=f32)
            return carry + jnp.sum(ls, axis=1, keepdims=True)

        lax.fori_loop(0, qi + 1, sweep_back, jnp.zeros((TQ, 1), f32))

        def sweep_fwd(kb, c):
            dq, carry = c
            rows = pl.ds(pl.multiple_of(kb * TQ, TQ), TQ)
            e, z = e_scr[kb], z_scr[kb]
            row = qi * TQ + lax.broadcasted_iota(jnp.int32, (TQ, TQ), 0)
            col = kb * TQ + lax.broadcasted_iota(jnp.int32, (TQ, TQ), 1)
            j = lax.broadcasted_iota(jnp.int32, (TQ, TQ), 0)
            u = lax.broadcasted_iota(jnp.int32, (TQ, TQ), 1)
            earlier = (j < u).astype(bf16)
            hi, lo = _split(e)
            ew = jnp.dot(hi, earlier, preferred_element_type=f32) + jnp.dot(lo, earlier, preferred_element_type=f32)
            sig = jax.nn.sigmoid(z)
            dz = jnp.where(col < row, e * (1.0 - sig) - (ew + carry) * sig, 0.0).astype(bf16)
            k = k_ref[rows, :].astype(bf16)
            dq = dq + jnp.dot(dz, k, preferred_element_type=f32)
            dk_acc[rows, :] += lax.dot_general(dz, q, (((0,), (0,)), ((), ())), preferred_element_type=f32)
            return dq, carry + jnp.sum(e, axis=1, keepdims=True)

        dq, _ = lax.fori_loop(0, qi + 1, sweep_fwd, (jnp.zeros((TQ, HD), f32), jnp.zeros((TQ, 1), f32)))
        dq_ref[...] = (dq * scale).astype(bf16)

        @pl.when(qi == nq - 1)
        def _():
            dk_ref[...] = dk_acc[...].astype(bf16)
            dv_ref[...] = dv_acc[...].astype(bf16)

    c0 = OFF_SB // HD
    return pl.pallas_call(
        body, name=name, grid=(NH, nq),
        in_specs=[pl.BlockSpec((TQ, HD), lambda h, i: (i, c0 + h)), pl.BlockSpec((s, HD), lambda h, i: (0, c0 + NH + h)),
                  pl.BlockSpec((s, HD), lambda h, i: (0, c0 + 2 * NH + h)), pl.BlockSpec((TQ, HD), lambda h, i: (i, h))],
        out_specs=[pl.BlockSpec((TQ, HD), lambda h, i: (i, h)), pl.BlockSpec((s, HD), lambda h, i: (0, h)),
                   pl.BlockSpec((s, HD), lambda h, i: (0, h))],
        out_shape=[_sds((s, WH), bf16)] * 3,
        scratch_shapes=[pltpu.VMEM((nq, TQ, TQ), f32), pltpu.VMEM((nq, TQ, TQ), f32), pltpu.VMEM((s, HD), f32),
                        pltpu.VMEM((s, HD), f32)],
        compiler_params=_cp(("arbitrary", "arbitrary")),
    )(proj, proj, proj, dy)


CB = 256


def _shift_down(v, k, s):
    if k == 0:
        return v
    row = lax.broadcasted_iota(jnp.int32, v.shape, 0)
    return jnp.where(row < k, 0.0, pltpu.roll(v, k, axis=0))


def _shift_up(v, k, s):
    if k == 0:
        return v
    row = lax.broadcasted_iota(jnp.int32, v.shape, 0)
    return jnp.where(row >= s - k, 0.0, pltpu.roll(v, s - k, axis=0))


def conv_fwd(proj, w, name):
    s = proj.shape[0]

    def body(x_ref, w_ref, y_ref):
        x, wv = x_ref[...], w_ref[...]
        y = sum(wv[3 - k:4 - k, :] * _shift_down(x, k, s) for k in range(4))
        y_ref[...] = jax.nn.silu(y)

    return pl.pallas_call(
        body, name=name, grid=(3 * WH // CB,),
        in_specs=[pl.BlockSpec((s, CB), lambda j: (0, OFF_GQ // CB + j)), pl.BlockSpec((4, CB), lambda j: (0, j))],
        out_specs=pl.BlockSpec((s, CB), lambda j: (0, j)), out_shape=_sds((s, 3 * WH), f32),
        compiler_params=_cp(("parallel",)),
    )(proj, w)


def conv_bwd(proj, w, dc, name):
    s = proj.shape[0]

    def body(x_ref, w_ref, dc_ref, dx_ref, dw_ref):
        x, wv = x_ref[...], w_ref[...]
        xs = [_shift_down(x, k, s) for k in range(4)]
        y = sum(wv[3 - k:4 - k, :] * xs[k] for k in range(4))
        sig = jax.nn.sigmoid(y)
        dy = dc_ref[...] * (sig * (1.0 + y * (1.0 - sig)))
        dx_ref[...] = sum(wv[3 - k:4 - k, :] * _shift_up(dy, k, s) for k in range(4)).astype(bf16)
        dw_ref[...] = jnp.concatenate([jnp.sum(dy * xs[3 - i], axis=0, keepdims=True) for i in range(4)], axis=0)

    return pl.pallas_call(
        body, name=name, grid=(3 * WH // CB,),
        in_specs=[pl.BlockSpec((s, CB), lambda j: (0, OFF_GQ // CB + j)), pl.BlockSpec((4, CB), lambda j: (0, j)),
                  pl.BlockSpec((s, CB), lambda j: (0, j))],
        out_specs=[pl.BlockSpec((s, CB), lambda j: (0, j)), pl.BlockSpec((4, CB), lambda j: (0, j))],
        out_shape=[_sds((s, 3 * WH), bf16), _sds((4, 3 * WH), f32)],
        compiler_params=_cp(("parallel",)),
    )(proj, w, dc)


HI = lax.Precision.HIGHEST


def _lane_pick(v, h):
    lane = lax.broadcasted_iota(jnp.int32, v.shape, v.ndim - 1)
    return jnp.sum(jnp.where(lane == h, v, 0.0), axis=-1, keepdims=True)


def _l2n(v):
    return v * lax.rsqrt(jnp.sum(v * v, axis=-1, keepdims=True) + EPS)


def _dot_nt(a, b):
    return lax.dot_general(a, b, (((1,), (1,)), ((), ())), preferred_element_type=f32)


def _gdn_chunk(cq, ck, cv, ab, alog, dtb, h):
    ones = jnp.ones((CH, HD), f32)
    q = _l2n(cq) * (HD ** -0.5)
    k = _l2n(ck)
    la = -jnp.exp(_lane_pick(alog, h)) * jax.nn.softplus(_lane_pick(ab, h) + _lane_pick(dtb, h))
    beta = jax.nn.sigmoid(_lane_pick(ab, NH + h)) * ones
    i = lax.broadcasted_iota(jnp.int32, (CH, CH), 0)
    j = lax.broadcasted_iota(jnp.int32, (CH, CH), 1)
    incl, strict = j <= i, j < i
    g = jnp.dot(incl.astype(f32), la * ones, precision=HI, preferred_element_type=f32)
    diff = g - g.T
    gamma = jnp.where(incl, jnp.exp(jnp.where(incl, diff, 0.0)), 0.0)
    kb = k.astype(bf16)
    lower = jnp.where(strict, beta * _dot_nt(kb, kb) * gamma, 0.0)
    eye = (i == j).astype(f32)
    inv = eye - lower
    pw = jnp.dot(lower, lower, precision=HI, preferred_element_type=f32)
    for m in range(1, int(math.log2(CH))):
        inv = inv + jnp.dot(inv, pw, precision=HI, preferred_element_type=f32)
        if m < int(math.log2(CH)) - 1:
            pw = jnp.dot(pw, pw, precision=HI, preferred_element_type=f32)
    eg = jnp.exp(g)
    u = jnp.dot(inv, cv * beta, precision=HI, preferred_element_type=f32)
    w = jnp.dot(inv, k * (beta * eg), precision=HI, preferred_element_type=f32)
    qk = _dot_nt(q.astype(bf16), kb) * gamma
    g_last = g[CH - 1:CH, :]
    return u, w, q * eg, k * jnp.exp(g_last - g), qk, jnp.exp(g_last)


def gdn_a_fwd(c, proj, alog, dtb, name):
    s = c.shape[0]
    nc = s // CH

    def body(c_ref, ab_ref, al_ref, dt_ref, u_ref, w_ref, qd_ref, kd_ref, qk_ref, dec_ref):
        for h in range(NH):
            sl = slice(h * HD, (h + 1) * HD)
            cq, ck, cv = (c_ref[:, t * WH + h * HD:t * WH + (h + 1) * HD] for t in range(3))
            res = _gdn_chunk(cq, ck, cv, ab_ref[...], al_ref[...], dt_ref[...], h)
            for r, v in zip((u_ref, w_ref, qd_ref, kd_ref, qk_ref), res[:5]):
                r[:, sl] = v
            dec_ref[:, sl] = jnp.broadcast_to(res[5], (8, HD))

    row = pl.BlockSpec((CH, WH), lambda n: (n, 0))
    par = pl.BlockSpec((1, HD), lambda n: (0, 0))
    return pl.pallas_call(
        body, name=name, grid=(nc,),
        in_specs=[pl.BlockSpec((CH, 3 * WH), lambda n: (n, 0)), pl.BlockSpec((CH, HD), lambda n: (n, OFF_AB // HD)), par, par],
        out_specs=[row] * 5 + [pl.BlockSpec((None, 8, WH), lambda n: (n, 0, 0))],
        out_shape=[_sds((s, WH), f32)] * 5 + [_sds((nc, 8, WH), f32)],
        compiler_params=_cp(("parallel",)),
    )(c, proj, alog, dtb)


def gdn_a_bwd(c, proj, alog, dtb, cots, ddec, name):
    s = c.shape[0]
    nc = s // CH

    def body(c_ref, ab_ref, al_ref, dt_ref, du_ref, dw_ref, dqd_ref, dkd_ref, dqk_ref, ddec_ref,
             dc_ref, dab_ref, dal_ref, ddt_ref):
        n = pl.program_id(0)
        dab, dal, ddt = jnp.zeros((CH, HD), f32), jnp.zeros((1, HD), f32), jnp.zeros((1, HD), f32)
        for h in range(NH):
            sl = slice(h * HD, (h + 1) * HD)
            cq, ck, cv = (c_ref[:, t * WH + h * HD:t * WH + (h + 1) * HD] for t in range(3))
            _, vjp = jax.vjp(functools.partial(_gdn_chunk, h=h), cq, ck, cv, ab_ref[...], al_ref[...], dt_ref[...])
            lane = lax.broadcasted_iota(jnp.int32, (1, HD), 1)
            dd = jnp.where(lane == 0, ddec_ref[0:1, sl], 0.0)
            dcs = vjp((du_ref[:, sl], dw_ref[:, sl], dqd_ref[:, sl], dkd_ref[:, sl], dqk_ref[:, sl], dd))
            for t in range(3):
                dc_ref[:, t * WH + h * HD:t * WH + (h + 1) * HD] = dcs[t]
            dab, dal, ddt = dab + dcs[3], dal + dcs[4], ddt + dcs[5]
        dab_ref[...] = dab.astype(bf16)

        @pl.when(n == 0)
        def _():
            dal_ref[...] = dal
            ddt_ref[...] = ddt

        @pl.when(n > 0)
        def _():
            dal_ref[...] += dal
            ddt_ref[...] += ddt

    row = pl.BlockSpec((CH, WH), lambda n: (n, 0))
    wide = pl.BlockSpec((CH, 3 * WH), lambda n: (n, 0))
    par = pl.BlockSpec((1, HD), lambda n: (0, 0))
    return pl.pallas_call(
        body, name=name, grid=(nc,),
        in_specs=[wide, pl.BlockSpec((CH, HD), lambda n: (n, OFF_AB // HD)), par, par] + [row] * 5
        + [pl.BlockSpec((None, 8, WH), lambda n: (n, 0, 0))],
        out_specs=[wide, pl.BlockSpec((CH, HD), lambda n: (n, 0)), par, par],
        out_shape=[_sds((s, 3 * WH), f32), _sds((s, HD), bf16), _sds((1, HD), f32), _sds((1, HD), f32)],
        compiler_params=_cp(("arbitrary",)),
    )(c, proj, alog, dtb, *cots, ddec)


def gdn_b_fwd(u, w, qd, kd, qk, dec, name):
    s = u.shape[0]
    nc = s // CH

    def body(u_ref, w_ref, qd_ref, kd_ref, qk_ref, dec_ref, o_ref, st_ref, state):
        n = pl.program_id(0)

        @pl.when(n == 0)
        def _():
            state[...] = jnp.zeros_like(state)

        for h in range(NH):
            sl = slice(h * HD, (h + 1) * HD)
            st = state[sl, :]
            st_ref[sl, :] = st
            sb = st.astype(bf16)
            vn = u_ref[:, sl] - jnp.dot(w_ref[:, sl].astype(bf16), sb, preferred_element_type=f32)
            vb = vn.astype(bf16)
            o_ref[:, sl] = (jnp.dot(qd_ref[:, sl].astype(bf16), sb, preferred_element_type=f32)
                            + jnp.dot(qk_ref[:, sl].astype(bf16), vb, preferred_element_type=f32))
            state[sl, :] = st * dec_ref[0:1, sl] + lax.dot_general(
                kd_ref[:, sl].astype(bf16), vb, (((0,), (0,)), ((), ())), preferred_element_type=f32)

    row = pl.BlockSpec((CH, WH), lambda n: (n, 0))
    return pl.pallas_call(
        body, name=name, grid=(nc,),
        in_specs=[row] * 5 + [pl.BlockSpec((None, 8, WH), lambda n: (n, 0, 0))],
        out_specs=[row, pl.BlockSpec((None, WH, HD), lambda n: (n, 0, 0))],
        out_shape=[_sds((s, WH), f32), _sds((nc, WH, HD), f32)],
        scratch_shapes=[pltpu.VMEM((WH, HD), f32)],
        compiler_params=_cp(("arbitrary",)),
    )(u, w, qd, kd, qk, dec)


def gdn_b_bwd(u, w, qd, kd, qk, dec, states, do, name):
    s = u.shape[0]
    nc = s // CH

    def body(u_ref, w_ref, qd_ref, kd_ref, qk_ref, dec_ref, st_ref, do_ref,
             du_ref, dw_ref, dqd_ref, dkd_ref, dqk_ref, ddec_ref, dstate):
        n = pl.program_id(0)

        @pl.when(n == 0)
        def _():
            dstate[...] = jnp.zeros_like(dstate)

        for h in range(NH):
            sl = slice(h * HD, (h + 1) * HD)
            st, ds = st_ref[sl, :], dstate[sl, :]
            sb, dsb = st.astype(bf16), ds.astype(bf16)
            wb, qdb, kdb, qkb = (r[:, sl].astype(bf16) for r in (w_ref, qd_ref, kd_ref, qk_ref))
            dob = do_ref[:, sl].astype(bf16)
            vn = u_ref[:, sl] - jnp.dot(wb, sb, preferred_element_type=f32)
            vb = vn.astype(bf16)
            dvn = (lax.dot_general(qkb, dob, (((0,), (0,)), ((), ())), preferred_element_type=f32)
                   + jnp.dot(kdb, dsb, preferred_element_type=f32))
            dvb = dvn.astype(bf16)
            du_ref[:, sl] = dvn
            dw_ref[:, sl] = -_dot_nt(dvb, sb)
            dqd_ref[:, sl] = _dot_nt(dob, sb)
            dkd_ref[:, sl] = _dot_nt(vb, dsb)
            dqk_ref[:, sl] = _dot_nt(dob, vb)
            tot = jnp.sum(jnp.sum(ds * st, axis=1, keepdims=True), axis=0, keepdims=True)
            ddec_ref[:, sl] = jnp.broadcast_to(tot, (8, HD))
            dstate[sl, :] = (ds * dec_ref[0:1, sl]
                             + lax.dot_general(qdb, dob, (((0,), (0,)), ((), ())), preferred_element_type=f32)
                             - lax.dot_general(wb, dvb, (((0,), (0,)), ((), ())), preferred_element_type=f32))

    row = pl.BlockSpec((CH, WH), lambda n: (nc - 1 - n, 0))
    small = pl.BlockSpec((None, 8, WH), lambda n: (nc - 1 - n, 0, 0))
    return pl.pallas_call(
        body, name=name, grid=(nc,),
        in_specs=[row] * 5 + [small, pl.BlockSpec((None, WH, HD), lambda n: (nc - 1 - n, 0, 0)), row],
        out_specs=[row] * 5 + [small],
        out_shape=[_sds((s, WH), f32)] * 5 + [_sds((nc, 8, WH), f32)],
        scratch_shapes=[pltpu.VMEM((WH, HD), f32)],
        compiler_params=_cp(("arbitrary",)),
    )(u, w, qd, kd, qk, dec, states, do)


ANY = pl.BlockSpec(memory_space=pl.ANY)


def _dev_index(p):
    return 4 * p[0] + 2 * p[1] + p[2]


def _shard_of(ref, axis, size, idx):
    return ref.at[pl.ds(idx * size, size), :] if axis == 0 else ref.at[:, pl.ds(idx * size, size)]


def all_gather(xs, axes, name):
    n = len(xs)
    fulls = [tuple(d * (NDEV if a == ax else 1) for a, d in enumerate(x.shape)) for x, ax in zip(xs, axes)]

    def body(*refs):
        x_refs, o_refs = refs[:n], refs[n:2 * n]
        send, recv, loc = refs[2 * n:]
        x, y, c = lax.axis_index("x"), lax.axis_index("y"), lax.axis_index("c")
        me, sib = (x, y, c), (x, y, 1 - c)
        chips = [(1 - x, y), (x, 1 - y), (1 - x, 1 - y)]

        def part(t, p):
            return _shard_of(o_refs[t], axes[t], xs[t].shape[axes[t]], _dev_index(p))

        def copy(t, k, block, to, src=None):
            return pltpu.make_async_remote_copy(
                src_ref=part(t, block) if src is None else src, dst_ref=part(t, block),
                send_sem=send.at[t, k], recv_sem=recv.at[t, k], device_id=to, device_id_type=MESH)

        mine = [pltpu.make_async_copy(x_refs[t], part(t, me), loc.at[t]) for t in range(n)]
        for cp in mine:
            cp.start()
        first = []
        for t in range(n):
            first.append(copy(t, 0, me, sib, src=x_refs[t]))
            first += [copy(t, 1 + j, me, (*chip, c), src=x_refs[t]) for j, chip in enumerate(chips)]
        for cp in first:
            cp.start()
        passed = []
        for t in range(n):
            for j, chip in enumerate(chips):
                copy(t, 1 + j, (*chip, c), me).wait_recv()
                passed.append(copy(t, 4 + j, (*chip, c), sib))
                passed[-1].start()
        for t in range(n):
            copy(t, 0, sib, me).wait_recv()
            for j, chip in enumerate(chips):
                copy(t, 4 + j, (*chip, 1 - c), me).wait_recv()
        for cp in first + passed:
            cp.wait_send()
        for cp in mine:
            cp.wait()

    return pl.pallas_call(
        body, name=name, in_specs=[ANY] * n, out_specs=[ANY] * n,
        out_shape=[_sds(f, x.dtype) for f, x in zip(fulls, xs)],
        scratch_shapes=[pltpu.SemaphoreType.DMA((n, 7)), pltpu.SemaphoreType.DMA((n, 7)), pltpu.SemaphoreType.DMA((n,))],
        compiler_params=pltpu.CompilerParams(has_side_effects=True),
    )(*xs)


def exchange(gs, axes, name):
    n = len(gs)
    shards = [tuple(d // (NDEV if a == ax else 1) for a, d in enumerate(g.shape)) for g, ax in zip(gs, axes)]

    def body(*refs):
        g_refs, r_refs = refs[:n], refs[n:2 * n]
        send, recv, loc = refs[2 * n:]
        x, y, c = lax.axis_index("x"), lax.axis_index("y"), lax.axis_index("c")
        me = _dev_index((x, y, c))
        peers = []
        for r in range(1, NDEV):
            peers.append((1 - x if r & 4 else x, 1 - y if r & 2 else y, 1 - c if r & 1 else c))

        def src(t, idx):
            return _shard_of(g_refs[t], axes[t], shards[t][axes[t]], idx)

        mine = [pltpu.make_async_copy(src(t, me), r_refs[t].at[me], loc.at[t]) for t in range(n)]
        for cp in mine:
            cp.start()
        sends = []
        for t in range(n):
            for k, p in enumerate(peers):
                sends.append(pltpu.make_async_remote_copy(
                    src_ref=src(t, _dev_index(p)), dst_ref=r_refs[t].at[me], send_sem=send.at[t, k], recv_sem=recv.at[t, k],
                    device_id=p, device_id_type=MESH))
                sends[-1].start()
        for t in range(n):
            for k, p in enumerate(peers):
                pltpu.make_async_remote_copy(
                    src_ref=src(t, me), dst_ref=r_refs[t].at[_dev_index(p)], send_sem=send.at[t, k], recv_sem=recv.at[t, k],
                    device_id=p, device_id_type=MESH).wait_recv()
        for cp in sends:
            cp.wait_send()
        for cp in mine:
            cp.wait()

    return pl.pallas_call(
        body, name=name, in_specs=[ANY] * n, out_specs=[ANY] * n,
        out_shape=[_sds((NDEV,) + sh, g.dtype) for sh, g in zip(shards, gs)],
        scratch_shapes=[pltpu.SemaphoreType.DMA((n, 7)), pltpu.SemaphoreType.DMA((n, 7)), pltpu.SemaphoreType.DMA((n,))],
        compiler_params=pltpu.CompilerParams(has_side_effects=True),
    )(*gs)


def _adamw(w, g, m, v):
    m = ADAM_B1 * m + (1.0 - ADAM_B1) * g
    v = ADAM_B2 * v + (1.0 - ADAM_B2) * jnp.square(g)
    m_hat = m / (1.0 - ADAM_B1 ** ADAM_STEP)
    v_hat = v / (1.0 - ADAM_B2 ** ADAM_STEP)
    return -ADAM_LR * (m_hat / (jnp.sqrt(v_hat) + ADAM_EPS) + ADAM_WD * w), m, v


def _sum8(r_ref):
    g = r_ref[0].astype(f32)
    for j in range(1, NDEV):
        g = g + r_ref[j].astype(f32)
    return g


def _row_tile(rows, cols):
    tr = min(rows, max(8, 1 << int(math.log2((1 << 18) / cols))))
    assert rows % tr == 0, (rows, cols)
    return tr


def sum_partials(r0, r1, name):
    _, rows, cols = r0.shape
    tr = _row_tile(rows, cols)
    nb = rows // tr

    def body(r0_ref, r1_ref, g_ref):
        layer = pl.program_id(0)

        @pl.when(layer == 0)
        def _():
            g_ref[...] = _sum8(r0_ref)

        @pl.when(layer == 1)
        def _():
            g_ref[...] = _sum8(r1_ref)

    return pl.pallas_call(
        body, name=name, grid=(NL, nb),
        in_specs=[pl.BlockSpec((NDEV, tr, cols), lambda l, i: (0, jnp.where(l == 0, i, nb - 1), 0)),
                  pl.BlockSpec((NDEV, tr, cols), lambda l, i: (0, jnp.where(l == 1, i, 0), 0))],
        out_specs=pl.BlockSpec((None, tr, cols), lambda l, i: (l, i, 0)), out_shape=_sds((NL, rows, cols), f32),
        compiler_params=_cp(("arbitrary", "arbitrary")),
    )(r0, r1)


def adamw(w, m, v, name, g=None, r0=None, r1=None):
    _, rows, cols = w.shape
    tr = _row_tile(rows, cols)
    nb = rows // tr
    blk = pl.BlockSpec((None, tr, cols), lambda l, i: (l, i, 0))
    summed = g is None

    def body(*refs):
        w_ref, m_ref, v_ref = refs[:3]
        outs = refs[-4:] if summed else refs[-3:]
        if summed:
            layer = pl.program_id(0)

            @pl.when(layer == 0)
            def _():
                outs[0][...] = _sum8(refs[3])

            @pl.when(layer == 1)
            def _():
                outs[0][...] = _sum8(refs[4])

            grad = outs[0][...]
        else:
            grad = refs[3][...]
        delta, nm, nv = _adamw(w_ref[...], grad, m_ref[...], v_ref[...])
        outs[-3][...], outs[-2][...], outs[-1][...] = delta, nm, nv

    if summed:
        extra = [r0, r1]
        extra_specs = [pl.BlockSpec((NDEV, tr, cols), lambda l, i: (0, jnp.where(l == 0, i, nb - 1), 0)),
                       pl.BlockSpec((NDEV, tr, cols), lambda l, i: (0, jnp.where(l == 1, i, 0), 0))]
    else:
        extra, extra_specs = [g], [blk]
    nout = 4 if summed else 3
    res = pl.pallas_call(
        body, name=name, grid=(NL, nb), in_specs=[blk] * 3 + extra_specs, out_specs=[blk] * nout,
        out_shape=[_sds(w.shape, f32)] * nout, compiler_params=_cp(("arbitrary", "arbitrary")),
    )(w, m, v, *extra)
    return tuple(res) if summed else (g,) + tuple(res)


def small_adamw(parts, w, m, v, name):
    def body(p_ref, w_ref, m_ref, v_ref, g_ref, d_ref, nm_ref, nv_ref):
        g = _sum8(p_ref)
        g_ref[...] = g
        d_ref[...], nm_ref[...], nv_ref[...] = _adamw(w_ref[...], g, m_ref[...], v_ref[...])

    return pl.pallas_call(body, name=name, out_shape=[_sds(w.shape, f32)] * 4, compiler_params=_cp())(parts, w, m, v)


def _pack_cols(w):
    pad = jnp.zeros((w.shape[0], NP - OFF_AB - 12), w.dtype)
    return jnp.concatenate([w[:, 5900:12044], w[:, 512:2816], w[:, 2816:5120], w[:, 5120:5888], w[:, 0:512],
                            w[:, 5888:5900], pad], axis=1)


def _unpack_cols(g):
    return jnp.concatenate([g[..., OFF_P:OFF_P + 512], g[..., OFF_SB:OFF_SB + 2304], g[..., OFF_GQ:OFF_GQ + 2304],
                            g[..., OFF_Z:OFF_Z + 768], g[..., OFF_AB:OFF_AB + 12], g[..., OFF_G:OFF_G + 6144]], axis=-1)


def _lanes(v):
    flat = v.reshape(-1)
    n = -(-flat.shape[0] // HD) * HD
    return jnp.pad(flat, (0, n - flat.shape[0])).reshape(n // HD, HD)


def _layer_fwd(x, p, l):
    nm = lambda s: f"{s}_l{l}"
    u = rms_fwd(x, p["attn_norm"], nm("rms1"))
    proj = matmul(u, p["w_in"], name=nm("inproj"))
    y_pool = pool_fwd(proj, p["pool_w"], p["pool_scale"], nm("pool"))
    y_sb = sb_fwd(proj, nm("sb"))
    c = conv_fwd(proj, p["conv"], nm("conv"))
    ga = gdn_a_fwd(c, proj, p["alog"], p["dtb"], nm("gdna"))
    o, states = gdn_b_fwd(*ga, nm("gdnb"))
    y_gdn = gdn_out_fwd(o, proj, p["gdn_norm"], nm("gdno"))
    ups = [matmul(y, p[k], name=nm(k)) for y, k in ((y_pool, "w_pool_up"), (y_sb, "w_sb_up"), (y_gdn, "w_gdn_up"))]
    merged = merge_fwd(proj, ups, nm("merge"))
    x1 = matmul(merged, p["w_out"], name=nm("outproj"), epilogue=lambda acc, r: acc + r, extras=(x,))
    u2 = rms_fwd(x1, p["mlp_norm"], nm("rms2"))
    h2 = matmul(u2, p["w_ff1"], name=nm("ff1"), out_dtype=bf16, epilogue=lambda acc: jnp.square(jnp.maximum(acc, 0.0)))
    x2 = matmul(h2, p["w_ff2"], name=nm("ff2"), epilogue=lambda acc, r: acc + r, extras=(x1,))
    saved = dict(x=x, u=u, proj=proj, y_pool=y_pool, y_sb=y_sb, c=c, ga=ga, o=o, states=states, y_gdn=y_gdn, ups=ups,
                 merged=merged, x1=x1, u2=u2, h2=h2)
    return x2, saved


def _layer_bwd(dx2, dx2b, sv, p, l):
    nm = lambda s: f"{s}_l{l}"
    s = dx2.shape[0]
    dh = matmul(dx2b, p["w_ff2"], name=nm("d_ff2_x"), tb=True, out_dtype=bf16,
                epilogue=lambda acc, h2: acc * (2.0 * jnp.sqrt(h2.astype(f32))), extras=(sv["h2"],))
    g_ff2 = matmul(sv["h2"], dx2b, name=nm("d_ff2_w"), ta=True, out_dtype=bf16)
    du2 = matmul(dh, p["w_ff1"], name=nm("d_ff1_x"), tb=True)
    g_ff1 = matmul(sv["u2"], dh, name=nm("d_ff1_w"), ta=True, out_dtype=bf16)
    dx1, dx1b, d_mlp_norm = rms_bwd(sv["x1"], du2, dx2, p["mlp_norm"], nm("d_rms2"))
    dmerged = matmul(dx1b, p["w_out"], name=nm("d_out_x"), tb=True, out_dtype=bf16)
    g_out = matmul(sv["merged"], dx1b, name=nm("d_out_w"), ta=True, out_dtype=bf16)
    dgates, *dups = merge_bwd(sv["proj"], sv["ups"], dmerged, nm("d_merge"))
    dys, g_ups = [], []
    for dup, y, k in zip(dups, (sv["y_pool"], sv["y_sb"], sv["y_gdn"]), ("w_pool_up", "w_sb_up", "w_gdn_up")):
        dys.append(matmul(dup, p[k], name=nm("d_" + k + "_x"), tb=True, out_dtype=bf16))
        g_ups.append(matmul(y, dup, name=nm("d_" + k + "_w"), ta=True, out_dtype=bf16))
    do, dz, d_gdn_norm = gdn_out_bwd(sv["o"], sv["proj"], p["gdn_norm"], dys[2], nm("d_gdno"))
    cots = gdn_b_bwd(*sv["ga"], sv["states"], do, nm("d_gdnb"))
    dc, dab, d_alog, d_dtb = gdn_a_bwd(sv["c"], sv["proj"], p["alog"], p["dtb"], cots[:5], cots[5], nm("d_gdna"))
    dgq, d_conv = conv_bwd(sv["proj"], p["conv"], dc, nm("d_conv"))
    dq, dk, dv = sb_bwd(sv["proj"], dys[1], nm("d_sb"))
    dp, d_pool_w, d_pool_scale = pool_bwd(sv["proj"], p["pool_w"], p["pool_scale"], dys[0], nm("d_pool"))
    dproj = jnp.concatenate([dgates, dq, dk, dv, dgq, dz, dp, dab, jnp.zeros((s, NP - OFF_AB - HD), bf16)], axis=1)
    du = matmul(dproj, p["w_in"], name=nm("d_in_x"), tb=True)
    g_in = matmul(sv["u"], dproj, name=nm("d_in_w"), ta=True, out_dtype=bf16)
    dx, dxb, d_attn_norm = rms_bwd(sv["x"], du, dx1, p["attn_norm"], nm("d_rms1"))
    big = [g_in, g_ups[0], g_ups[1], g_ups[2], g_out, g_ff1, g_ff2]
    small = [d_attn_norm, d_pool_w, d_pool_scale, d_conv, d_alog, d_dtb, d_gdn_norm, d_mlp_norm]
    return dx, dxb, big, small


BIG_AXES = (0, 1, 1, 1, 0, 1, 0)


def kernel(x, attn_norm, w_in, pool_w, pool_scale, gdn_conv, gdn_a_log, gdn_dt_bias, gdn_norm, w_pool_up, w_sb_up, w_gdn_up, w_out, mlp_norm, w_ff1, w_ff2, final_norm, loss_target, m_attn_norm, m_w_in, m_pool_w, m_pool_scale, m_gdn_conv, m_gdn_a_log, m_gdn_dt_bias, m_gdn_norm, m_w_pool_up, m_w_sb_up, m_w_gdn_up, m_w_out, m_mlp_norm, m_w_ff1, m_w_ff2, m_final_norm, v_attn_norm, v_w_in, v_pool_w, v_pool_scale, v_gdn_conv, v_gdn_a_log, v_gdn_dt_bias, v_gdn_norm, v_w_pool_up, v_w_sb_up, v_w_gdn_up, v_w_out, v_mlp_norm, v_w_ff1, v_w_ff2, v_final_norm):
    s = x.shape[1]
    me = _dev_index((lax.axis_index("x"), lax.axis_index("y"), lax.axis_index("c")))
    ncv = gdn_conv.shape[2]

    shards = []
    for l in range(NL):
        shards += [_pack_cols(w_in[l]).astype(bf16), w_pool_up[l].astype(bf16), w_sb_up[l].astype(bf16),
                   w_gdn_up[l].astype(bf16), w_out[l].astype(bf16), w_ff1[l].astype(bf16), w_ff2[l].astype(bf16)]
    shards.append(gdn_conv.reshape(NL * 4, ncv))
    full = all_gather(shards, BIG_AXES * NL + (0,), "gather_weights")
    conv_full = full[-1].reshape(NDEV, NL, 4, ncv).transpose(1, 2, 0, 3).reshape(NL, 4, NDEV * ncv)
    params = []
    for l in range(NL):
        p = dict(zip(("w_in", "w_pool_up", "w_sb_up", "w_gdn_up", "w_out", "w_ff1", "w_ff2"), full[7 * l:7 * l + 7]))
        p.update(attn_norm=attn_norm[l][None], mlp_norm=mlp_norm[l][None], pool_w=pool_w[l], pool_scale=pool_scale[l][None],
                 conv=conv_full[l], alog=_lanes(gdn_a_log[l]), dtb=_lanes(gdn_dt_bias[l]), gdn_norm=gdn_norm[l][None])
        params.append(p)

    h = x[0]
    saved = []
    for l in range(NL):
        h, sv = _layer_fwd(h, params[l], l)
        saved.append(sv)
    dh, dhb, loss_row, d_final = loss_head(h, loss_target[0], final_norm[None], "loss_head")
    bigs, smalls = [None] * NL, [None] * NL
    for l in reversed(range(NL)):
        dh, dhb, bigs[l], smalls[l] = _layer_bwd(dh, dhb, saved[l], params[l], l)

    recv = exchange(bigs[0] + bigs[1], BIG_AXES * NL, "exchange_grads")
    small_rows = [_lanes(t) for l in range(NL) for t in smalls[l]] + [_lanes(d_final), loss_row]
    n_small = sum(t.shape[0] for t in small_rows)
    pad_rows = -n_small % 8
    packed = jnp.concatenate(small_rows + [jnp.zeros((pad_rows, HD), f32)], axis=0)
    parts = all_gather([packed], (0,), "gather_small")[0].reshape(NDEV, n_small + pad_rows, HD)

    def pack_small(tree):
        rows = []
        for l in range(NL):
            rows += [_lanes(tree["attn_norm"][l]), _lanes(tree["pool_w"][l]), _lanes(tree["pool_scale"][l]),
                     jnp.zeros((4 * NDEV * ncv // HD, HD), f32), _lanes(tree["gdn_a_log"][l]), _lanes(tree["gdn_dt_bias"][l]),
                     _lanes(tree["gdn_norm"][l]), _lanes(tree["mlp_norm"][l])]
        rows += [_lanes(tree["final_norm"]), jnp.zeros((1 + pad_rows, HD), f32)]
        return jnp.concatenate(rows, axis=0)

    names = ("attn_norm", "pool_w", "pool_scale", "gdn_a_log", "gdn_dt_bias", "gdn_norm", "mlp_norm", "final_norm")
    w_small = pack_small(dict(zip(names, (attn_norm, pool_w, pool_scale, gdn_a_log, gdn_dt_bias, gdn_norm, mlp_norm, final_norm))))
    m_small = pack_small(dict(zip(names, (m_attn_norm, m_pool_w, m_pool_scale, m_gdn_a_log, m_gdn_dt_bias, m_gdn_norm, m_mlp_norm, m_final_norm))))
    v_small = pack_small(dict(zip(names, (v_attn_norm, v_pool_w, v_pool_scale, v_gdn_a_log, v_gdn_dt_bias, v_gdn_norm, v_mlp_norm, v_final_norm))))
    small_out = small_adamw(parts, w_small, m_small, v_small, "adamw_small")

    def unpack_small(buf):
        out, conv_g, r = {}, [], 0
        layer_items = (("attn_norm", (D,)), ("pool_w", (4, HD, HD)), ("pool_scale", (W_POOL,)), ("conv", (4, NDEV * ncv)),
                       ("gdn_a_log", (NH,)), ("gdn_dt_bias", (NH,)), ("gdn_norm", (HD,)), ("mlp_norm", (D,)))
        per_layer = {k: [] for k, _ in layer_items}
        for l in range(NL):
            for k, shape in layer_items:
                size = math.prod(shape)
                nrow = -(-size // HD)
                per_layer[k].append(buf[r:r + nrow].reshape(-1)[:size].reshape(shape))
                r += nrow
        for k, _ in layer_items:
            out[k] = jnp.stack(per_layer[k])
        out["final_norm"] = buf[r:r + D // HD].reshape(D)
        out["loss"] = buf[r + D // HD, 0]
        return out

    sm = [unpack_small(b) for b in small_out]
    loss = sm[0]["loss"]
    g_conv = lax.dynamic_slice_in_dim(sm[0]["conv"], me * ncv, ncv, axis=2)
    conv_out = adamw(gdn_conv, m_gdn_conv, v_gdn_conv, "adamw_conv", g=g_conv)

    big_out = {}
    big_names = ("w_in", "w_pool_up", "w_sb_up", "w_gdn_up", "w_out", "w_ff1", "w_ff2")
    big_w = dict(zip(big_names, ((w_in, m_w_in, v_w_in), (w_pool_up, m_w_pool_up, v_w_pool_up), (w_sb_up, m_w_sb_up, v_w_sb_up),
                                 (w_gdn_up, m_w_gdn_up, v_w_gdn_up), (w_out, m_w_out, v_w_out), (w_ff1, m_w_ff1, v_w_ff1),
                                 (w_ff2, m_w_ff2, v_w_ff2))))
    for t, k in enumerate(big_names):
        w, m, v = big_w[k]
        r0, r1 = recv[t], recv[7 + t]
        if k == "w_in":
            g = _unpack_cols(sum_partials(r0, r1, "sum_w_in"))
            big_out[k] = adamw(w, m, v, "adamw_" + k, g=g)
        else:
            big_out[k] = adamw(w, m, v, "adamw_" + k, r0=r0, r1=r1)

    def leaf(i, k):
        if k in big_out:
            return big_out[k][i]
        if k == "gdn_conv":
            return conv_out[i]
        return sm[i][k]

    order = ("attn_norm", "w_in", "pool_w", "pool_scale", "gdn_conv", "gdn_a_log", "gdn_dt_bias", "gdn_norm", "w_pool_up",
             "w_sb_up", "w_gdn_up", "w_out", "mlp_norm", "w_ff1", "w_ff2", "final_norm")
    return (loss, dh[None]) + tuple(leaf(i, k) for i in range(4) for k in order)
```

```python
import functools
import math

import jax
import jax.numpy as jnp
from jax import lax
from jax.experimental import pallas as pl
from jax.experimental.pallas import tpu as pltpu
from jax.experimental.pallas import tpu_sc as plsc

f32, bf16 = jnp.float32, jnp.bfloat16

D = 2048
NDEV = 8
NL = 2
HD = 128
NH = 6
WH = NH * HD
W_POOL = 512
EPS = 1e-6
N_IN = 12044
NP = 12288
OFF_G, OFF_SB, OFF_GQ, OFF_Z, OFF_P, OFF_AB = 0, 6144, 8448, 10752, 11520, 12032
POOL_WINDOWS = (2, 4, 8, 16)
CH = 128
TQ = 256
VMEM_LIMIT = 56 * 1024 * 1024
ADAM_LR, ADAM_B1, ADAM_B2, ADAM_EPS, ADAM_WD, ADAM_STEP = 0.001, 0.9, 0.999, 1e-08, 0.01, 10
MESH = pl.DeviceIdType.MESH


def _cp(sem=None):
    return pltpu.CompilerParams(dimension_semantics=sem, vmem_limit_bytes=VMEM_LIMIT)


def _sds(shape, dtype):
    return jax.ShapeDtypeStruct(tuple(shape), dtype)


def matmul(a, b, *, name, ta=False, tb=False, out_dtype=f32, tm=1024, tn=1024, tk=2048, epilogue=None, extras=()):
    m, k = (a.shape[1], a.shape[0]) if ta else a.shape
    n = b.shape[0] if tb else b.shape[1]
    assert k == (b.shape[1] if tb else b.shape[0]) and a.dtype == bf16 and b.dtype == bf16
    tm, tn, tk = min(tm, m), min(tn, n), min(tk, k)
    assert m % tm == 0 and n % tn == 0 and k % tk == 0, (m, n, k, tm, tn, tk)
    nk = k // tk
    a_spec = pl.BlockSpec((tk, tm), lambda i, j, q: (q, i)) if ta else pl.BlockSpec((tm, tk), lambda i, j, q: (i, q))
    b_spec = pl.BlockSpec((tn, tk), lambda i, j, q: (j, q)) if tb else pl.BlockSpec((tk, tn), lambda i, j, q: (q, j))
    e_specs = [pl.BlockSpec((tm, tn), lambda i, j, q: (i, j)) for _ in extras]
    dn = (((0 if ta else 1,), (1 if tb else 0,)), ((), ()))
    ne = len(extras)

    def body(a_ref, b_ref, *rest):
        e_refs, o_ref = rest[:ne], rest[ne]
        part = lax.dot_general(a_ref[...], b_ref[...], dn, preferred_element_type=f32)

        def finish(acc):
            if epilogue is not None:
                acc = epilogue(acc, *[e[...] for e in e_refs])
            o_ref[...] = acc.astype(out_dtype)

        if nk == 1:
            finish(part)
        else:
            acc_ref = rest[ne + 1]
            q = pl.program_id(2)

            @pl.when(q == 0)
            def _():
                acc_ref[...] = part

            @pl.when(jnp.logical_and(q > 0, q < nk - 1))
            def _():
                acc_ref[...] += part

            @pl.when(q == nk - 1)
            def _():
                finish(acc_ref[...] + part)

    return pl.pallas_call(
        body, name=name, grid=(m // tm, n // tn, nk),
        in_specs=[a_spec, b_spec] + e_specs, out_specs=pl.BlockSpec((tm, tn), lambda i, j, q: (i, j)),
        out_shape=_sds((m, n), out_dtype),
        scratch_shapes=[pltpu.VMEM((tm, tn), f32)] if nk > 1 else [],
        compiler_params=_cp(("parallel", "parallel", "arbitrary")),
    )(a, b, *extras)


def rowwise(name, fn, rows, params, outs, sums=(), tr=256):
    s = rows[0][0].shape[0]
    tr = min(tr, s)
    nin, nout = len(rows) + len(params), len(outs)
    in_specs = [pl.BlockSpec((tr, w), functools.partial(lambda i, c: (i, c), c=c)) for (_, w, c) in rows]
    in_specs += [pl.BlockSpec(p.shape, lambda i: (0, 0)) for p in params]
    out_specs = [pl.BlockSpec((tr, w), lambda i: (i, 0)) for (w, _) in outs]
    out_specs += [pl.BlockSpec(sh, lambda i: (0, 0)) for sh in sums]
    out_shape = [_sds((s, w), dt) for (w, dt) in outs] + [_sds(sh, f32) for sh in sums]

    def body(*refs):
        res = fn(*[r[...] for r in refs[:nin]])
        for r, v in zip(refs[nin:nin + nout], res[:nout]):
            r[...] = v.astype(r.dtype)
        i = pl.program_id(0)
        for r, v in zip(refs[nin + nout:], res[nout:]):
            @pl.when(i == 0)
            def _(r=r, v=v):
                r[...] = v

            @pl.when(i > 0)
            def _(r=r, v=v):
                r[...] += v

    res = pl.pallas_call(
        body, name=name, grid=(s // tr,), in_specs=in_specs, out_specs=out_specs, out_shape=out_shape,
        compiler_params=_cp(("arbitrary",)),
    )(*[r[0] for r in rows], *params)
    return res


def _rms(x, g):
    return x * lax.rsqrt(jnp.mean(x * x, axis=-1, keepdims=True) + EPS) * g


def rms_fwd(x, g, name):
    return rowwise(name, lambda xb, gb: (_rms(xb, gb),), [(x, D, 0)], [g], [(D, bf16)])[0]


def rms_bwd(x, du, dres, g, name):
    def fn(xb, dub, drb, gb):
        _, vjp = jax.vjp(_rms, xb, gb)
        dx, dg = vjp(dub.astype(f32))
        return drb + dx, drb + dx, dg

    return rowwise(name, fn, [(x, D, 0), (du, D, 0), (dres, D, 0)], [g], [(D, f32), (D, bf16)], [(1, D)])


def _merge(gates, up_p, up_s, up_g):
    sg = jax.nn.sigmoid(gates)
    return sg[:, :D] * up_p + sg[:, D:2 * D] * up_s + sg[:, 2 * D:] * up_g


def merge_fwd(proj, ups, name):
    return rowwise(name, lambda g, a, b, c: (_merge(g, a, b, c),),
                   [(proj, 3 * D, 0)] + [(u, D, 0) for u in ups], [], [(D, bf16)], tr=128)[0]


def merge_bwd(proj, ups, dmerged, name):
    def fn(g, a, b, c, dm):
        _, vjp = jax.vjp(_merge, g, a, b, c)
        return vjp(dm.astype(f32))

    return rowwise(name, fn, [(proj, 3 * D, 0)] + [(u, D, 0) for u in ups] + [(dmerged, D, 0)], [],
                   [(3 * D, bf16), (D, bf16), (D, bf16), (D, bf16)], tr=128)


def _gdn_out(o, z, g):
    ys = []
    for h in range(NH):
        sl = slice(h * HD, (h + 1) * HD)
        ys.append(_rms(o[:, sl], g) * jax.nn.silu(z[:, sl]))
    return jnp.concatenate(ys, axis=1)


def gdn_out_fwd(o, proj, g, name):
    return rowwise(name, lambda ob, zb, gb: (_gdn_out(ob, zb, gb),), [(o, WH, 0), (proj, WH, OFF_Z // WH)], [g],
                   [(WH, bf16)])[0]


def gdn_out_bwd(o, proj, g, dy, name):
    def fn(ob, zb, dyb, gb):
        _, vjp = jax.vjp(_gdn_out, ob, zb, gb)
        return vjp(dyb.astype(f32))

    return rowwise(name, fn, [(o, WH, 0), (proj, WH, OFF_Z // WH), (dy, WH, 0)], [g], [(WH, f32), (WH, bf16)],
                   [(1, HD)])


def loss_head(x, target, g, name):
    def loss_fn(xb, gb, tb):
        err = _rms(xb, gb) - tb
        return (0.5 / D) * jnp.sum(jnp.sum(err * err, axis=1, keepdims=True), axis=0, keepdims=True)

    def fn(xb, tb, gb):
        val, vjp = jax.vjp(functools.partial(loss_fn, tb=tb), xb, gb)
        dx, dg = vjp(jnp.ones((1, 1), f32))
        return dx, dx, jnp.broadcast_to(val, (1, HD)), dg

    return rowwise(name, fn, [(x, D, 0), (target, D, 0)], [g], [(D, f32), (D, bf16)], [(1, HD), (1, D)])


PB = 256


def _split(v):
    hi = v.astype(bf16)
    return hi, (v - hi.astype(f32)).astype(bf16)


def _band_dot(make_band, v, s, forward):
    hi, lo = _split(v)
    nb = s // PB
    outs = []
    for r in range(nb):
        lo_r = max(r - 1, 0) if forward else r
        hi_r = r + 1 if forward else min(r + 2, nb)
        band = make_band(r * PB, lo_r * PB, (hi_r - lo_r) * PB)
        sl = slice(lo_r * PB, hi_r * PB)
        outs.append(jnp.dot(band, hi[sl], preferred_element_type=f32) + jnp.dot(band, lo[sl], preferred_element_type=f32))
    return jnp.concatenate(outs, axis=0)


def _pool_common(p, win, s):
    def band(row0, col0, ncol):
        t = row0 + lax.broadcasted_iota(jnp.int32, (PB, ncol), 0)
        u = col0 + lax.broadcasted_iota(jnp.int32, (PB, ncol), 1)
        return jnp.logical_and(u <= t, t < u + win).astype(bf16)

    def band_t(row0, col0, ncol):
        u = row0 + lax.broadcasted_iota(jnp.int32, (PB, ncol), 0)
        t = col0 + lax.broadcasted_iota(jnp.int32, (PB, ncol), 1)
        return jnp.logical_and(u <= t, t < u + win).astype(bf16)

    t = lax.broadcasted_iota(jnp.int32, (s, 1), 0)
    inv_n = 1.0 / jnp.minimum(t + 1, win).astype(f32)
    d = _band_dot(band, p, s, True) * inv_n - p
    return d, inv_n, band_t


def pool_fwd(proj, pool_w, pool_scale, name):
    s = proj.shape[0]

    def body(p_ref, w_ref, sc_ref, y_ref):
        win = jnp.left_shift(2, pl.program_id(0))
        d, _, _ = _pool_common(p_ref[...], win, s)
        y = jnp.dot(d.astype(bf16), w_ref[...].astype(bf16), preferred_element_type=f32) * sc_ref[...]
        y_ref[...] = y.astype(bf16)

    return pl.pallas_call(
        body, name=name, grid=(4,),
        in_specs=[pl.BlockSpec((s, HD), lambda g: (0, OFF_P // HD + g)), pl.BlockSpec((None, HD, HD), lambda g: (g, 0, 0)),
                  pl.BlockSpec((1, HD), lambda g: (0, g))],
        out_specs=pl.BlockSpec((s, HD), lambda g: (0, g)), out_shape=_sds((s, W_POOL), bf16),
        compiler_params=_cp(("arbitrary",)),
    )(proj, pool_w, pool_scale)


def pool_bwd(proj, pool_w, pool_scale, dy, name):
    s = proj.shape[0]

    def body(p_ref, w_ref, sc_ref, dy_ref, dp_ref, dw_ref, dsc_ref):
        win = jnp.left_shift(2, pl.program_id(0))
        d, inv_n, band_t = _pool_common(p_ref[...], win, s)
        w = w_ref[...].astype(bf16)
        dyf = dy_ref[...].astype(f32)
        dsc_ref[...] = jnp.sum(dyf * jnp.dot(d.astype(bf16), w, preferred_element_type=f32), axis=0, keepdims=True)
        dys = (dyf * sc_ref[...]).astype(bf16)
        dd = lax.dot_general(dys, w, (((1,), (1,)), ((), ())), preferred_element_type=f32)
        dw_ref[...] = lax.dot_general(d.astype(bf16), dys, (((0,), (0,)), ((), ())), preferred_element_type=f32)
        dp_ref[...] = (_band_dot(band_t, dd * inv_n, s, False) - dd).astype(bf16)

    return pl.pallas_call(
        body, name=name, grid=(4,),
        in_specs=[pl.BlockSpec((s, HD), lambda g: (0, OFF_P // HD + g)), pl.BlockSpec((None, HD, HD), lambda g: (g, 0, 0)),
                  pl.BlockSpec((1, HD), lambda g: (0, g)), pl.BlockSpec((s, HD), lambda g: (0, g))],
        out_specs=[pl.BlockSpec((s, HD), lambda g: (0, g)), pl.BlockSpec((None, HD, HD), lambda g: (g, 0, 0)),
                   pl.BlockSpec((1, HD), lambda g: (0, g))],
        out_shape=[_sds((s, W_POOL), bf16), _sds((4, HD, HD), f32), _sds((1, W_POOL), f32)],
        compiler_params=_cp(("arbitrary",)),
    )(proj, pool_w, pool_scale, dy)


def _sb_tile(q, k_ref, kb, qi, carry):
    k = k_ref[pl.ds(pl.multiple_of(kb * TQ, TQ), TQ), :].astype(bf16)
    z = lax.dot_general(q, k, (((1,), (1,)), ((), ())), preferred_element_type=f32)
    row = qi * TQ + lax.broadcasted_iota(jnp.int32, (TQ, TQ), 0)
    col = kb * TQ + lax.broadcasted_iota(jnp.int32, (TQ, TQ), 1)
    mask = col < row
    ls = jnp.where(mask, jax.nn.log_sigmoid(-z), 0.0)
    j = lax.broadcasted_iota(jnp.int32, (TQ, TQ), 0)
    u = lax.broadcasted_iota(jnp.int32, (TQ, TQ), 1)
    later = (j > u).astype(bf16)
    hi, lo = _split(ls)
    lw = jnp.dot(hi, later, preferred_element_type=f32) + jnp.dot(lo, later, preferred_element_type=f32)
    a = jnp.where(mask, jnp.exp(ls + z + lw + carry), 0.0)
    return z, mask, ls, a


def sb_fwd(proj, name):
    s = proj.shape[0]
    nq = s // TQ
    scale = HD ** -0.5

    def body(q_ref, k_ref, v_ref, y_ref):
        qi = pl.program_id(1)
        q = (q_ref[...] * scale).astype(bf16)

        def step(j, c):
            acc, carry = c
            kb = qi - j
            _, _, ls, a = _sb_tile(q, k_ref, kb, qi, carry)
            v = v_ref[pl.ds(pl.multiple_of(kb * TQ, TQ), TQ), :].astype(bf16)
            acc = acc + jnp.dot(a.astype(bf16), v, preferred_element_type=f32)
            return acc, carry + jnp.sum(ls, axis=1, keepdims=True)

        acc, _ = lax.fori_loop(0, qi + 1, step, (jnp.zeros((TQ, HD), f32), jnp.zeros((TQ, 1), f32)))
        y_ref[...] = acc.astype(bf16)

    c0 = OFF_SB // HD
    return pl.pallas_call(
        body, name=name, grid=(NH, nq),
        in_specs=[pl.BlockSpec((TQ, HD), lambda h, i: (i, c0 + h)), pl.BlockSpec((s, HD), lambda h, i: (0, c0 + NH + h)),
                  pl.BlockSpec((s, HD), lambda h, i: (0, c0 + 2 * NH + h))],
        out_specs=pl.BlockSpec((TQ, HD), lambda h, i: (i, h)), out_shape=_sds((s, WH), bf16),
        compiler_params=_cp(("arbitrary", "arbitrary")),
    )(proj, proj, proj)


def sb_bwd(proj, dy, name):
    s = proj.shape[0]
    nq = s // TQ
    scale = HD ** -0.5

    def body(q_ref, k_ref, v_ref, do_ref, dq_ref, dk_ref, dv_ref, e_scr, z_scr, dk_acc, dv_acc):
        qi = pl.program_id(1)
        q = (q_ref[...] * scale).astype(bf16)
        do = do_ref[...]

        @pl.when(qi == 0)
        def _():
            dk_acc[...] = jnp.zeros_like(dk_acc)
            dv_acc[...] = jnp.zeros_like(dv_acc)

        def sweep_back(j, carry):
            kb = qi - j
            rows = pl.ds(pl.multiple_of(kb * TQ, TQ), TQ)
            z, _, ls, a = _sb_tile(q, k_ref, kb, qi, carry)
            v = v_ref[rows, :].astype(bf16)
            da = lax.dot_general(do, v, (((1,), (1,)), ((), ())), preferred_element_type=f32)
            e_scr[kb] = da * a
            z_scr[kb] = z
            dv_acc[rows, :] += lax.dot_general(a.astype(bf16), do, (((0,), (0,)), ((), ())), preferred_element_type=f32)
            return carry + jnp.sum(ls, axis=1, keepdims=True)

        lax.fori_loop(0, qi + 1, sweep_back, jnp.zeros((TQ, 1), f32))

        def sweep_fwd(kb, c):
            dq, carry = c
            rows = pl.ds(pl.multiple_of(kb * TQ, TQ), TQ)
            e, z = e_scr[kb], z_scr[kb]
            row = qi * TQ + lax.broadcasted_iota(jnp.int32, (TQ, TQ), 0)
            col = kb * TQ + lax.broadcasted_iota(jnp.int32, (TQ, TQ), 1)
            j = lax.broadcasted_iota(jnp.int32, (TQ, TQ), 0)
            u = lax.broadcasted_iota(jnp.int32, (TQ, TQ), 1)
            earlier = (j < u).astype(bf16)
            hi, lo = _split(e)
            ew = jnp.dot(hi, earlier, preferred_element_type=f32) + jnp.dot(lo, earlier, preferred_element_type=f32)
            sig = jax.nn.sigmoid(z)
            dz = jnp.where(col < row, e * (1.0 - sig) - (ew + carry) * sig, 0.0).astype(bf16)
            k = k_ref[rows, :].astype(bf16)
            dq = dq + jnp.dot(dz, k, preferred_element_type=f32)
            dk_acc[rows, :] += lax.dot_general(dz, q, (((0,), (0,)), ((), ())), preferred_element_type=f32)
            return dq, carry + jnp.sum(e, axis=1, keepdims=True)

        dq, _ = lax.fori_loop(0, qi + 1, sweep_fwd, (jnp.zeros((TQ, HD), f32), jnp.zeros((TQ, 1), f32)))
        dq_ref[...] = (dq * scale).astype(bf16)

        @pl.when(qi == nq - 1)
        def _():
            dk_ref[...] = dk_acc[...].astype(bf16)
            dv_ref[...] = dv_acc[...].astype(bf16)

    c0 = OFF_SB // HD
    return pl.pallas_call(
        body, name=name, grid=(NH, nq),
        in_specs=[pl.BlockSpec((TQ, HD), lambda h, i: (i, c0 + h)), pl.BlockSpec((s, HD), lambda h, i: (0, c0 + NH + h)),
                  pl.BlockSpec((s, HD), lambda h, i: (0, c0 + 2 * NH + h)), pl.BlockSpec((TQ, HD), lambda h, i: (i, h))],
        out_specs=[pl.BlockSpec((TQ, HD), lambda h, i: (i, h)), pl.BlockSpec((s, HD), lambda h, i: (0, h)),
                   pl.BlockSpec((s, HD), lambda h, i: (0, h))],
        out_shape=[_sds((s, WH), bf16)] * 3,
        scratch_shapes=[pltpu.VMEM((nq, TQ, TQ), f32), pltpu.VMEM((nq, TQ, TQ), f32), pltpu.VMEM((s, HD), f32),
                        pltpu.VMEM((s, HD), f32)],
        compiler_params=_cp(("arbitrary", "arbitrary")),
    )(proj, proj, proj, dy)


CB = 256


def _shift_down(v, k, s):
    if k == 0:
        return v
    row = lax.broadcasted_iota(jnp.int32, v.shape, 0)
    return jnp.where(row < k, 0.0, pltpu.roll(v, k, axis=0))


def _shift_up(v, k, s):
    if k == 0:
        return v
    row = lax.broadcasted_iota(jnp.int32, v.shape, 0)
    return jnp.where(row >= s - k, 0.0, pltpu.roll(v, s - k, axis=0))


def conv_fwd(proj, w, name):
    s = proj.shape[0]

    def body(x_ref, w_ref, y_ref):
        x, wv = x_ref[...], w_ref[...]
        y = sum(wv[3 - k:4 - k, :] * _shift_down(x, k, s) for k in range(4))
        y_ref[...] = jax.nn.silu(y)

    return pl.pallas_call(
        body, name=name, grid=(3 * WH // CB,),
        in_specs=[pl.BlockSpec((s, CB), lambda j: (0, OFF_GQ // CB + j)), pl.BlockSpec((4, CB), lambda j: (0, j))],
        out_specs=pl.BlockSpec((s, CB), lambda j: (0, j)), out_shape=_sds((s, 3 * WH), f32),
        compiler_params=_cp(("parallel",)),
    )(proj, w)


def conv_bwd(proj, w, dc, name):
    s = proj.shape[0]

    def body(x_ref, w_ref, dc_ref, dx_ref, dw_ref):
        x, wv = x_ref[...], w_ref[...]
        xs = [_shift_down(x, k, s) for k in range(4)]
        y = sum(wv[3 - k:4 - k, :] * xs[k] for k in range(4))
        sig = jax.nn.sigmoid(y)
        dy = dc_ref[...] * (sig * (1.0 + y * (1.0 - sig)))
        dx_ref[...] = sum(wv[3 - k:4 - k, :] * _shift_up(dy, k, s) for k in range(4)).astype(bf16)
        dw_ref[...] = jnp.concatenate([jnp.sum(dy * xs[3 - i], axis=0, keepdims=True) for i in range(4)], axis=0)

    return pl.pallas_call(
        body, name=name, grid=(3 * WH // CB,),
        in_specs=[pl.BlockSpec((s, CB), lambda j: (0, OFF_GQ // CB + j)), pl.BlockSpec((4, CB), lambda j: (0, j)),
                  pl.BlockSpec((s, CB), lambda j: (0, j))],
        out_specs=[pl.BlockSpec((s, CB), lambda j: (0, j)), pl.BlockSpec((4, CB), lambda j: (0, j))],
        out_shape=[_sds((s, 3 * WH), bf16), _sds((4, 3 * WH), f32)],
        compiler_params=_cp(("parallel",)),
    )(proj, w, dc)


SOLVE_PASSES = 3
NN, NT, TN = (((1,), (0,)), ((), ())), (((1,), (1,)), ((), ())), (((0,), (0,)), ((), ()))


def _pdot_impl(a, b, dn, passes):
    ah, al = _split(a)
    bh, bl = _split(b)
    dot = lambda p, q: lax.dot_general(p, q, dn, preferred_element_type=f32)
    if passes == 1:
        return dot(ah, bh)
    if passes == 2:
        return dot(ah, bh) + dot(ah, bl)
    return dot(ah, bh) + (dot(ah, bl) + dot(al, bh))


@functools.partial(jax.custom_vjp, nondiff_argnums=(2,))
def _pdot(a, b, passes):
    return _pdot_impl(a, b, NN, passes)


def _pdot_fwd(a, b, passes):
    return _pdot_impl(a, b, NN, passes), (a, b)


def _pdot_bwd(passes, res, ct):
    a, b = res
    return _pdot_impl(ct, b, NT, passes), _pdot_impl(a, ct, TN, passes)


_pdot.defvjp(_pdot_fwd, _pdot_bwd)


@functools.partial(jax.custom_vjp, nondiff_argnums=(2,))
def _pdot_nt(a, b, passes):
    return _pdot_impl(a, b, NT, passes)


def _pdot_nt_fwd(a, b, passes):
    return _pdot_impl(a, b, NT, passes), (a, b)


def _pdot_nt_bwd(passes, res, ct):
    a, b = res
    return _pdot_impl(ct, b, NN, passes), _pdot_impl(ct, a, TN, passes)


_pdot_nt.defvjp(_pdot_nt_fwd, _pdot_nt_bwd)


def _lane_pick(v, h):
    lane = lax.broadcasted_iota(jnp.int32, v.shape, v.ndim - 1)
    return jnp.sum(jnp.where(lane == h, v, 0.0), axis=-1, keepdims=True)


def _l2n(v):
    return v * lax.rsqrt(jnp.sum(v * v, axis=-1, keepdims=True) + EPS)


def _dot_nt(a, b):
    return lax.dot_general(a, b, (((1,), (1,)), ((), ())), preferred_element_type=f32)


def _gdn_chunk(cq, ck, cv, ab, alog, dtb, h):
    ones = jnp.ones((CH, HD), f32)
    q = _l2n(cq) * (HD ** -0.5)
    k = _l2n(ck)
    la = -jnp.exp(_lane_pick(alog, h)) * jax.nn.softplus(_lane_pick(ab, h) + _lane_pick(dtb, h))
    beta = jax.nn.sigmoid(_lane_pick(ab, NH + h)) * ones
    i = lax.broadcasted_iota(jnp.int32, (CH, CH), 0)
    j = lax.broadcasted_iota(jnp.int32, (CH, CH), 1)
    incl, strict = j <= i, j < i
    g = _pdot(incl.astype(f32), la * ones, 2)
    diff = g - g.T
    gamma = jnp.where(incl, jnp.exp(jnp.where(incl, diff, 0.0)), 0.0)
    lower = jnp.where(strict, beta * _pdot_nt(k, k, 1) * gamma, 0.0)
    eye = (i == j).astype(f32)
    inv = eye - lower
    pw = _pdot(lower, lower, SOLVE_PASSES)
    for m in range(1, int(math.log2(CH))):
        inv = inv + _pdot(inv, pw, SOLVE_PASSES)
        if m < int(math.log2(CH)) - 1:
            pw = _pdot(pw, pw, SOLVE_PASSES)
    eg = jnp.exp(g)
    u = _pdot(inv, cv * beta, SOLVE_PASSES)
    w = _pdot(inv, k * (beta * eg), SOLVE_PASSES)
    qk = _pdot_nt(q, k, 1) * gamma
    g_last = g[CH - 1:CH, :]
    return u, w, q * eg, k * jnp.exp(g_last - g), qk, jnp.exp(g_last)


def gdn_a_fwd(c, proj, alog, dtb, name):
    s = c.shape[0]
    nc = s // CH

    def body(c_ref, ab_ref, al_ref, dt_ref, u_ref, w_ref, qd_ref, kd_ref, qk_ref, dec_ref):
        for h in range(NH):
            sl = slice(h * HD, (h + 1) * HD)
            cq, ck, cv = (c_ref[:, t * WH + h * HD:t * WH + (h + 1) * HD] for t in range(3))
            res = _gdn_chunk(cq, ck, cv, ab_ref[...], al_ref[...], dt_ref[...], h)
            for r, v in zip((u_ref, w_ref, qd_ref, kd_ref, qk_ref), res[:5]):
                r[:, sl] = v
            dec_ref[:, sl] = jnp.broadcast_to(res[5], (8, HD))

    row = pl.BlockSpec((CH, WH), lambda n: (n, 0))
    par = pl.BlockSpec((1, HD), lambda n: (0, 0))
    return pl.pallas_call(
        body, name=name, grid=(nc,),
        in_specs=[pl.BlockSpec((CH, 3 * WH), lambda n: (n, 0)), pl.BlockSpec((CH, HD), lambda n: (n, OFF_AB // HD)), par, par],
        out_specs=[row] * 5 + [pl.BlockSpec((None, 8, WH), lambda n: (n, 0, 0))],
        out_shape=[_sds((s, WH), f32)] * 5 + [_sds((nc, 8, WH), f32)],
        compiler_params=_cp(("parallel",)),
    )(c, proj, alog, dtb)


def gdn_a_bwd(c, proj, alog, dtb, cots, ddec, name):
    s = c.shape[0]
    nc = s // CH

    def body(c_ref, ab_ref, al_ref, dt_ref, du_ref, dw_ref, dqd_ref, dkd_ref, dqk_ref, ddec_ref,
             dc_ref, dab_ref, dal_ref, ddt_ref):
        n = pl.program_id(0)
        dab, dal, ddt = jnp.zeros((CH, HD), f32), jnp.zeros((1, HD), f32), jnp.zeros((1, HD), f32)
        for h in range(NH):
            sl = slice(h * HD, (h + 1) * HD)
            cq, ck, cv = (c_ref[:, t * WH + h * HD:t * WH + (h + 1) * HD] for t in range(3))
            _, vjp = jax.vjp(functools.partial(_gdn_chunk, h=h), cq, ck, cv, ab_ref[...], al_ref[...], dt_ref[...])
            lane = lax.broadcasted_iota(jnp.int32, (1, HD), 1)
            dd = jnp.where(lane == 0, ddec_ref[0:1, sl], 0.0)
            dcs = vjp((du_ref[:, sl], dw_ref[:, sl], dqd_ref[:, sl], dkd_ref[:, sl], dqk_ref[:, sl], dd))
            for t in range(3):
                dc_ref[:, t * WH + h * HD:t * WH + (h + 1) * HD] = dcs[t]
            dab, dal, ddt = dab + dcs[3], dal + dcs[4], ddt + dcs[5]
        dab_ref[...] = dab.astype(bf16)

        @pl.when(n == 0)
        def _():
            dal_ref[...] = dal
            ddt_ref[...] = ddt

        @pl.when(n > 0)
        def _():
            dal_ref[...] += dal
            ddt_ref[...] += ddt

    row = pl.BlockSpec((CH, WH), lambda n: (n, 0))
    wide = pl.BlockSpec((CH, 3 * WH), lambda n: (n, 0))
    par = pl.BlockSpec((1, HD), lambda n: (0, 0))
    return pl.pallas_call(
        body, name=name, grid=(nc,),
        in_specs=[wide, pl.BlockSpec((CH, HD), lambda n: (n, OFF_AB // HD)), par, par] + [row] * 5
        + [pl.BlockSpec((None, 8, WH), lambda n: (n, 0, 0))],
        out_specs=[wide, pl.BlockSpec((CH, HD), lambda n: (n, 0)), par, par],
        out_shape=[_sds((s, 3 * WH), f32), _sds((s, HD), bf16), _sds((1, HD), f32), _sds((1, HD), f32)],
        compiler_params=_cp(("arbitrary",)),
    )(c, proj, alog, dtb, *cots, ddec)


def gdn_b_fwd(u, w, qd, kd, qk, dec, name):
    s = u.shape[0]
    nc = s // CH

    def body(u_ref, w_ref, qd_ref, kd_ref, qk_ref, dec_ref, o_ref, st_ref, state):
        n = pl.program_id(0)

        @pl.when(n == 0)
        def _():
            state[...] = jnp.zeros_like(state)

        for h in range(NH):
            sl = slice(h * HD, (h + 1) * HD)
            st = state[sl, :]
            st_ref[sl, :] = st
            sb = st.astype(bf16)
            vn = u_ref[:, sl] - jnp.dot(w_ref[:, sl].astype(bf16), sb, preferred_element_type=f32)
            vb = vn.astype(bf16)
            o_ref[:, sl] = (jnp.dot(qd_ref[:, sl].astype(bf16), sb, preferred_element_type=f32)
                            + jnp.dot(qk_ref[:, sl].astype(bf16), vb, preferred_element_type=f32))
            state[sl, :] = st * dec_ref[0:1, sl] + lax.dot_general(
                kd_ref[:, sl].astype(bf16), vb, (((0,), (0,)), ((), ())), preferred_element_type=f32)

    row = pl.BlockSpec((CH, WH), lambda n: (n, 0))
    return pl.pallas_call(
        body, name=name, grid=(nc,),
        in_specs=[row] * 5 + [pl.BlockSpec((None, 8, WH), lambda n: (n, 0, 0))],
        out_specs=[row, pl.BlockSpec((None, WH, HD), lambda n: (n, 0, 0))],
        out_shape=[_sds((s, WH), f32), _sds((nc, WH, HD), f32)],
        scratch_shapes=[pltpu.VMEM((WH, HD), f32)],
        compiler_params=_cp(("arbitrary",)),
    )(u, w, qd, kd, qk, dec)


def gdn_b_bwd(u, w, qd, kd, qk, dec, states, do, name):
    s = u.shape[0]
    nc = s // CH

    def body(u_ref, w_ref, qd_ref, kd_ref, qk_ref, dec_ref, st_ref, do_ref,
             du_ref, dw_ref, dqd_ref, dkd_ref, dqk_ref, ddec_ref, dstate):
        n = pl.program_id(0)

        @pl.when(n == 0)
        def _():
            dstate[...] = jnp.zeros_like(dstate)

        for h in range(NH):
            sl = slice(h * HD, (h + 1) * HD)
            st, ds = st_ref[sl, :], dstate[sl, :]
            sb, dsb = st.astype(bf16), ds.astype(bf16)
            wb, qdb, kdb, qkb = (r[:, sl].astype(bf16) for r in (w_ref, qd_ref, kd_ref, qk_ref))
            dob = do_ref[:, sl].astype(bf16)
            vn = u_ref[:, sl] - jnp.dot(wb, sb, preferred_element_type=f32)
            vb = vn.astype(bf16)
            dvn = (lax.dot_general(qkb, dob, (((0,), (0,)), ((), ())), preferred_element_type=f32)
                   + jnp.dot(kdb, dsb, preferred_element_type=f32))
            dvb = dvn.astype(bf16)
            du_ref[:, sl] = dvn
            dw_ref[:, sl] = -_dot_nt(dvb, sb)
            dqd_ref[:, sl] = _dot_nt(dob, sb)
            dkd_ref[:, sl] = _dot_nt(vb, dsb)
            dqk_ref[:, sl] = _dot_nt(dob, vb)
            tot = jnp.sum(jnp.sum(ds * st, axis=1, keepdims=True), axis=0, keepdims=True)
            ddec_ref[:, sl] = jnp.broadcast_to(tot, (8, HD))
            dstate[sl, :] = (ds * dec_ref[0:1, sl]
                             + lax.dot_general(qdb, dob, (((0,), (0,)), ((), ())), preferred_element_type=f32)
                             - lax.dot_general(wb, dvb, (((0,), (0,)), ((), ())), preferred_element_type=f32))

    row = pl.BlockSpec((CH, WH), lambda n: (nc - 1 - n, 0))
    small = pl.BlockSpec((None, 8, WH), lambda n: (nc - 1 - n, 0, 0))
    return pl.pallas_call(
        body, name=name, grid=(nc,),
        in_specs=[row] * 5 + [small, pl.BlockSpec((None, WH, HD), lambda n: (nc - 1 - n, 0, 0)), row],
        out_specs=[row] * 5 + [small],
        out_shape=[_sds((s, WH), f32)] * 5 + [_sds((nc, 8, WH), f32)],
        scratch_shapes=[pltpu.VMEM((WH, HD), f32)],
        compiler_params=_cp(("arbitrary",)),
    )(u, w, qd, kd, qk, dec, states, do)


ANY = pl.BlockSpec(memory_space=pl.ANY)


def _dev_index(p):
    return 4 * p[0] + 2 * p[1] + p[2]


def _shard_of(ref, axis, size, idx):
    return ref.at[pl.ds(idx * size, size), :] if axis == 0 else ref.at[:, pl.ds(idx * size, size)]


def _launch(body, ins, out_sds, name, sequencer_id):
    n = len(ins)
    sems = [pltpu.SemaphoreType.DMA((n, 7)), pltpu.SemaphoreType.DMA((n, 7)), pltpu.SemaphoreType.DMA((n,))]
    if sequencer_id is None:
        return pl.pallas_call(
            lambda *refs: body(refs[:n], refs[n:2 * n], *refs[2 * n:]), name=name, in_specs=[ANY] * n, out_specs=[ANY] * n,
            out_shape=out_sds, scratch_shapes=sems, compiler_params=pltpu.CompilerParams(has_side_effects=True),
        )(*ins)
    in_refs = [jax.new_ref(a, memory_space=pltpu.MemorySpace.HBM) for a in ins]
    out_refs = [jax.empty_ref(sd, memory_space=pltpu.MemorySpace.HBM) for sd in out_sds]

    @pl.kernel(mesh=plsc.ScalarSubcoreMesh(axis_name="sequencer", num_cores=1), name=name, scratch_types=sems,
               compiler_params=pltpu.CompilerParams(collective_id=sequencer_id))
    def launch(send, recv, loc):
        x, y, c = lax.axis_index("x"), lax.axis_index("y"), lax.axis_index("c")
        barrier = pltpu.get_barrier_semaphore()
        for r in range(1, NDEV):
            peer = (1 - x if r & 4 else x, 1 - y if r & 2 else y, 1 - c if r & 1 else c)
            pl.semaphore_signal(barrier, inc=1, device_id=peer, device_id_type=MESH)
        pl.semaphore_wait(barrier, NDEV - 1)
        body(in_refs, out_refs, send, recv, loc)

    launch()
    return [r[...] for r in out_refs]


def all_gather(xs, axes, name, sequencer_id=None):
    n = len(xs)
    fulls = [tuple(d * (NDEV if a == ax else 1) for a, d in enumerate(x.shape)) for x, ax in zip(xs, axes)]

    def body(x_refs, o_refs, send, recv, loc):
        x, y, c = lax.axis_index("x"), lax.axis_index("y"), lax.axis_index("c")
        me, sib = (x, y, c), (x, y, 1 - c)
        chips = [(1 - x, y), (x, 1 - y), (1 - x, 1 - y)]

        def part(t, p):
            return _shard_of(o_refs[t], axes[t], xs[t].shape[axes[t]], _dev_index(p))

        def copy(t, k, block, to, src=None):
            return pltpu.make_async_remote_copy(
                src_ref=part(t, block) if src is None else src, dst_ref=part(t, block),
                send_sem=send.at[t, k], recv_sem=recv.at[t, k], device_id=to, device_id_type=MESH)

        mine = [pltpu.make_async_copy(x_refs[t], part(t, me), loc.at[t]) for t in range(n)]
        for cp in mine:
            cp.start()
        first = []
        for t in range(n):
            first.append(copy(t, 0, me, sib, src=x_refs[t]))
            first += [copy(t, 1 + j, me, (*chip, c), src=x_refs[t]) for j, chip in enumerate(chips)]
        for cp in first:
            cp.start()
        passed = []
        for t in range(n):
            for j, chip in enumerate(chips):
                copy(t, 1 + j, (*chip, c), me).wait_recv()
                passed.append(copy(t, 4 + j, (*chip, c), sib))
                passed[-1].start()
        for t in range(n):
            copy(t, 0, sib, me).wait_recv()
            for j, chip in enumerate(chips):
                copy(t, 4 + j, (*chip, 1 - c), me).wait_recv()
        for cp in first + passed:
            cp.wait_send()
        for cp in mine:
            cp.wait()

    return _launch(body, xs, [_sds(f, x.dtype) for f, x in zip(fulls, xs)], name, sequencer_id)


def exchange(gs, axes, name, sequencer_id=None):
    n = len(gs)
    shards = [tuple(d // (NDEV if a == ax else 1) for a, d in enumerate(g.shape)) for g, ax in zip(gs, axes)]

    def body(g_refs, r_refs, send, recv, loc):
        x, y, c = lax.axis_index("x"), lax.axis_index("y"), lax.axis_index("c")
        me = _dev_index((x, y, c))
        peers = []
        for r in range(1, NDEV):
            peers.append((1 - x if r & 4 else x, 1 - y if r & 2 else y, 1 - c if r & 1 else c))

        def src(t, idx):
            return _shard_of(g_refs[t], axes[t], shards[t][axes[t]], idx)

        mine = [pltpu.make_async_copy(src(t, me), r_refs[t].at[me], loc.at[t]) for t in range(n)]
        for cp in mine:
            cp.start()
        sends = []
        for t in range(n):
            for k, p in enumerate(peers):
                sends.append(pltpu.make_async_remote_copy(
                    src_ref=src(t, _dev_index(p)), dst_ref=r_refs[t].at[me], send_sem=send.at[t, k], recv_sem=recv.at[t, k],
                    device_id=p, device_id_type=MESH))
                sends[-1].start()
        for t in range(n):
            for k, p in enumerate(peers):
                pltpu.make_async_remote_copy(
                    src_ref=src(t, me), dst_ref=r_refs[t].at[_dev_index(p)], send_sem=send.at[t, k], recv_sem=recv.at[t, k],
                    device_id=p, device_id_type=MESH).wait_recv()
        for cp in sends:
            cp.wait_send()
        for cp in mine:
            cp.wait()

    return _launch(body, gs, [_sds((NDEV,) + sh, g.dtype) for sh, g in zip(shards, gs)], name, sequencer_id)


def _adamw(w, g, m, v):
    m = ADAM_B1 * m + (1.0 - ADAM_B1) * g
    v = ADAM_B2 * v + (1.0 - ADAM_B2) * jnp.square(g)
    m_hat = m / (1.0 - ADAM_B1 ** ADAM_STEP)
    v_hat = v / (1.0 - ADAM_B2 ** ADAM_STEP)
    return -ADAM_LR * (m_hat / (jnp.sqrt(v_hat) + ADAM_EPS) + ADAM_WD * w), m, v


def _sum8(r_ref):
    g = r_ref[0].astype(f32)
    for j in range(1, NDEV):
        g = g + r_ref[j].astype(f32)
    return g


def _row_tile(rows, cols):
    tr = min(rows, max(8, 1 << int(math.log2((1 << 18) / cols))))
    assert rows % tr == 0, (rows, cols)
    return tr


def sum_partials(r0, r1, name):
    _, rows, cols = r0.shape
    tr = _row_tile(rows, cols)
    nb = rows // tr

    def body(r0_ref, r1_ref, g_ref):
        layer = pl.program_id(0)

        @pl.when(layer == 0)
        def _():
            g_ref[...] = _sum8(r0_ref)

        @pl.when(layer == 1)
        def _():
            g_ref[...] = _sum8(r1_ref)

    return pl.pallas_call(
        body, name=name, grid=(NL, nb),
        in_specs=[pl.BlockSpec((NDEV, tr, cols), lambda l, i: (0, jnp.where(l == 0, i, nb - 1), 0)),
                  pl.BlockSpec((NDEV, tr, cols), lambda l, i: (0, jnp.where(l == 1, i, 0), 0))],
        out_specs=pl.BlockSpec((None, tr, cols), lambda l, i: (l, i, 0)), out_shape=_sds((NL, rows, cols), f32),
        compiler_params=_cp(("arbitrary", "arbitrary")),
    )(r0, r1)


def adamw(w, m, v, name, g=None, r0=None, r1=None):
    _, rows, cols = w.shape
    tr = _row_tile(rows, cols)
    nb = rows // tr
    blk = pl.BlockSpec((None, tr, cols), lambda l, i: (l, i, 0))
    summed = g is None

    def body(*refs):
        w_ref, m_ref, v_ref = refs[:3]
        outs = refs[-4:] if summed else refs[-3:]
        if summed:
            layer = pl.program_id(0)

            @pl.when(layer == 0)
            def _():
                outs[0][...] = _sum8(refs[3])

            @pl.when(layer == 1)
            def _():
                outs[0][...] = _sum8(refs[4])

            grad = outs[0][...]
        else:
            grad = refs[3][...]
        delta, nm, nv = _adamw(w_ref[...], grad, m_ref[...], v_ref[...])
        outs[-3][...], outs[-2][...], outs[-1][...] = delta, nm, nv

    if summed:
        extra = [r0, r1]
        extra_specs = [pl.BlockSpec((NDEV, tr, cols), lambda l, i: (0, jnp.where(l == 0, i, nb - 1), 0)),
                       pl.BlockSpec((NDEV, tr, cols), lambda l, i: (0, jnp.where(l == 1, i, 0), 0))]
    else:
        extra, extra_specs = [g], [blk]
    nout = 4 if summed else 3
    res = pl.pallas_call(
        body, name=name, grid=(NL, nb), in_specs=[blk] * 3 + extra_specs, out_specs=[blk] * nout,
        out_shape=[_sds(w.shape, f32)] * nout, compiler_params=_cp(("arbitrary", "arbitrary")),
    )(w, m, v, *extra)
    return tuple(res) if summed else (g,) + tuple(res)


def small_adamw(parts, w, m, v, name):
    def body(p_ref, w_ref, m_ref, v_ref, g_ref, d_ref, nm_ref, nv_ref):
        g = _sum8(p_ref)
        g_ref[...] = g
        d_ref[...], nm_ref[...], nv_ref[...] = _adamw(w_ref[...], g, m_ref[...], v_ref[...])

    return pl.pallas_call(body, name=name, out_shape=[_sds(w.shape, f32)] * 4, compiler_params=_cp())(parts, w, m, v)


def _pack_cols(w):
    pad = jnp.zeros((w.shape[0], NP - OFF_AB - 12), w.dtype)
    return jnp.concatenate([w[:, 5900:12044], w[:, 512:2816], w[:, 2816:5120], w[:, 5120:5888], w[:, 0:512],
                            w[:, 5888:5900], pad], axis=1)


def _unpack_cols(g):
    return jnp.concatenate([g[..., OFF_P:OFF_P + 512], g[..., OFF_SB:OFF_SB + 2304], g[..., OFF_GQ:OFF_GQ + 2304],
                            g[..., OFF_Z:OFF_Z + 768], g[..., OFF_AB:OFF_AB + 12], g[..., OFF_G:OFF_G + 6144]], axis=-1)


def _lanes(v):
    flat = v.reshape(-1)
    n = -(-flat.shape[0] // HD) * HD
    return jnp.pad(flat, (0, n - flat.shape[0])).reshape(n // HD, HD)


def _lanes8(v):
    rows = _lanes(v)
    return jnp.pad(rows, ((0, -rows.shape[0] % 8), (0, 0)))


def _layer_fwd(x, p, l):
    nm = lambda s: f"{s}_l{l}"
    u = rms_fwd(x, p["attn_norm"], nm("rms1"))
    proj = matmul(u, p["w_in"], name=nm("inproj"))
    y_pool = pool_fwd(proj, p["pool_w"], p["pool_scale"], nm("pool"))
    y_sb = sb_fwd(proj, nm("sb"))
    c = conv_fwd(proj, p["conv"], nm("conv"))
    ga = gdn_a_fwd(c, proj, p["alog"], p["dtb"], nm("gdna"))
    o, states = gdn_b_fwd(*ga, nm("gdnb"))
    y_gdn = gdn_out_fwd(o, proj, p["gdn_norm"], nm("gdno"))
    ups = [matmul(y, p[k], name=nm(k)) for y, k in ((y_pool, "w_pool_up"), (y_sb, "w_sb_up"), (y_gdn, "w_gdn_up"))]
    merged = merge_fwd(proj, ups, nm("merge"))
    x1 = matmul(merged, p["w_out"], name=nm("outproj"), epilogue=lambda acc, r: acc + r, extras=(x,))
    u2 = rms_fwd(x1, p["mlp_norm"], nm("rms2"))
    h2 = matmul(u2, p["w_ff1"], name=nm("ff1"), out_dtype=bf16, epilogue=lambda acc: jnp.square(jnp.maximum(acc, 0.0)))
    x2 = matmul(h2, p["w_ff2"], name=nm("ff2"), epilogue=lambda acc, r: acc + r, extras=(x1,))
    saved = dict(x=x, u=u, proj=proj, y_pool=y_pool, y_sb=y_sb, c=c, ga=ga, o=o, states=states, y_gdn=y_gdn, ups=ups,
                 merged=merged, x1=x1, u2=u2, h2=h2)
    return x2, saved


def _layer_bwd(dx2, dx2b, sv, p, l):
    nm = lambda s: f"{s}_l{l}"
    s = dx2.shape[0]
    dh = matmul(dx2b, p["w_ff2"], name=nm("d_ff2_x"), tb=True, out_dtype=bf16,
                epilogue=lambda acc, h2: acc * (2.0 * jnp.sqrt(h2.astype(f32))), extras=(sv["h2"],))
    g_ff2 = matmul(sv["h2"], dx2b, name=nm("d_ff2_w"), ta=True, out_dtype=bf16)
    du2 = matmul(dh, p["w_ff1"], name=nm("d_ff1_x"), tb=True)
    g_ff1 = matmul(sv["u2"], dh, name=nm("d_ff1_w"), ta=True, out_dtype=bf16)
    dx1, dx1b, d_mlp_norm = rms_bwd(sv["x1"], du2, dx2, p["mlp_norm"], nm("d_rms2"))
    dmerged = matmul(dx1b, p["w_out"], name=nm("d_out_x"), tb=True, out_dtype=bf16)
    g_out = matmul(sv["merged"], dx1b, name=nm("d_out_w"), ta=True, out_dtype=bf16)
    dgates, *dups = merge_bwd(sv["proj"], sv["ups"], dmerged, nm("d_merge"))
    dys, g_ups = [], []
    for dup, y, k in zip(dups, (sv["y_pool"], sv["y_sb"], sv["y_gdn"]), ("w_pool_up", "w_sb_up", "w_gdn_up")):
        dys.append(matmul(dup, p[k], name=nm("d_" + k + "_x"), tb=True, out_dtype=bf16))
        g_ups.append(matmul(y, dup, name=nm("d_" + k + "_w"), ta=True, out_dtype=bf16))
    do, dz, d_gdn_norm = gdn_out_bwd(sv["o"], sv["proj"], p["gdn_norm"], dys[2], nm("d_gdno"))
    cots = gdn_b_bwd(*sv["ga"], sv["states"], do, nm("d_gdnb"))
    dc, dab, d_alog, d_dtb = gdn_a_bwd(sv["c"], sv["proj"], p["alog"], p["dtb"], cots[:5], cots[5], nm("d_gdna"))
    dgq, d_conv = conv_bwd(sv["proj"], p["conv"], dc, nm("d_conv"))
    dq, dk, dv = sb_bwd(sv["proj"], dys[1], nm("d_sb"))
    dp, d_pool_w, d_pool_scale = pool_bwd(sv["proj"], p["pool_w"], p["pool_scale"], dys[0], nm("d_pool"))
    dproj = jnp.concatenate([dgates, dq, dk, dv, dgq, dz, dp, dab, jnp.zeros((s, NP - OFF_AB - HD), bf16)], axis=1)
    du = matmul(dproj, p["w_in"], name=nm("d_in_x"), tb=True)
    g_in = matmul(sv["u"], dproj, name=nm("d_in_w"), ta=True, out_dtype=bf16)
    dx, dxb, d_attn_norm = rms_bwd(sv["x"], du, dx1, p["attn_norm"], nm("d_rms1"))
    big = [g_in, g_ups[0], g_ups[1], g_ups[2], g_out, g_ff1, g_ff2]
    small = [d_attn_norm, d_pool_w, d_pool_scale, d_conv, d_alog, d_dtb, d_gdn_norm, d_mlp_norm]
    return dx, dxb, big, small


BIG_AXES = (0, 1, 1, 1, 0, 1, 0)


def kernel(x, attn_norm, w_in, pool_w, pool_scale, gdn_conv, gdn_a_log, gdn_dt_bias, gdn_norm, w_pool_up, w_sb_up, w_gdn_up, w_out, mlp_norm, w_ff1, w_ff2, final_norm, loss_target, m_attn_norm, m_w_in, m_pool_w, m_pool_scale, m_gdn_conv, m_gdn_a_log, m_gdn_dt_bias, m_gdn_norm, m_w_pool_up, m_w_sb_up, m_w_gdn_up, m_w_out, m_mlp_norm, m_w_ff1, m_w_ff2, m_final_norm, v_attn_norm, v_w_in, v_pool_w, v_pool_scale, v_gdn_conv, v_gdn_a_log, v_gdn_dt_bias, v_gdn_norm, v_w_pool_up, v_w_sb_up, v_w_gdn_up, v_w_out, v_mlp_norm, v_w_ff1, v_w_ff2, v_final_norm):
    s = x.shape[1]
    me = _dev_index((lax.axis_index("x"), lax.axis_index("y"), lax.axis_index("c")))
    ncv = gdn_conv.shape[2]

    full = []
    for l in range(NL):
        shards = [_pack_cols(w_in[l]).astype(bf16), w_pool_up[l].astype(bf16), w_sb_up[l].astype(bf16),
                  w_gdn_up[l].astype(bf16), w_out[l].astype(bf16), w_ff1[l].astype(bf16), w_ff2[l].astype(bf16)]
        if l == 0:
            shards.append(gdn_conv.reshape(NL * 4, ncv))
        full.append(all_gather(shards, BIG_AXES + (0,) * (len(shards) - 7), f"gather_weights_l{l}", sequencer_id=1 + l))
    conv_full = full[0][7].reshape(NDEV, NL, 4, ncv).transpose(1, 2, 0, 3).reshape(NL, 4, NDEV * ncv)
    params = []
    for l in range(NL):
        p = dict(zip(("w_in", "w_pool_up", "w_sb_up", "w_gdn_up", "w_out", "w_ff1", "w_ff2"), full[l][:7]))
        p.update(attn_norm=attn_norm[l][None], mlp_norm=mlp_norm[l][None], pool_w=pool_w[l], pool_scale=pool_scale[l][None],
                 conv=conv_full[l], alog=_lanes(gdn_a_log[l]), dtb=_lanes(gdn_dt_bias[l]), gdn_norm=gdn_norm[l][None])
        params.append(p)

    h = x[0]
    saved = []
    for l in range(NL):
        h, sv = _layer_fwd(h, params[l], l)
        saved.append(sv)
    dh, dhb, loss_row, d_final = loss_head(h, loss_target[0], final_norm[None], "loss_head")
    recv, smalls = [None] * NL, [None] * NL
    for l in reversed(range(NL)):
        dh, dhb, big, smalls[l] = _layer_bwd(dh, dhb, saved[l], params[l], l)
        recv[l] = exchange(big, BIG_AXES, f"exchange_grads_l{l}", sequencer_id=3 + l)
    recv = recv[0] + recv[1]

    small_rows = [_lanes8(t) for l in range(NL) for t in smalls[l]] + [_lanes8(d_final), _lanes8(loss_row)]
    packed = jnp.concatenate(small_rows, axis=0)
    parts = all_gather([packed], (0,), "gather_small")[0].reshape(NDEV, packed.shape[0], HD)

    def pack_small(tree):
        rows = []
        for l in range(NL):
            rows += [_lanes8(tree["attn_norm"][l]), _lanes8(tree["pool_w"][l]), _lanes8(tree["pool_scale"][l]),
                     jnp.zeros((4 * NDEV * ncv // HD, HD), f32), _lanes8(tree["gdn_a_log"][l]), _lanes8(tree["gdn_dt_bias"][l]),
                     _lanes8(tree["gdn_norm"][l]), _lanes8(tree["mlp_norm"][l])]
        rows += [_lanes8(tree["final_norm"]), jnp.zeros((8, HD), f32)]
        return jnp.concatenate(rows, axis=0)

    names = ("attn_norm", "pool_w", "pool_scale", "gdn_a_log", "gdn_dt_bias", "gdn_norm", "mlp_norm", "final_norm")
    w_small = pack_small(dict(zip(names, (attn_norm, pool_w, pool_scale, gdn_a_log, gdn_dt_bias, gdn_norm, mlp_norm, final_norm))))
    m_small = pack_small(dict(zip(names, (m_attn_norm, m_pool_w, m_pool_scale, m_gdn_a_log, m_gdn_dt_bias, m_gdn_norm, m_mlp_norm, m_final_norm))))
    v_small = pack_small(dict(zip(names, (v_attn_norm, v_pool_w, v_pool_scale, v_gdn_a_log, v_gdn_dt_bias, v_gdn_norm, v_mlp_norm, v_final_norm))))
    small_out = small_adamw(parts, w_small, m_small, v_small, "adamw_small")

    def unpack_small(buf):
        out, conv_g, r = {}, [], 0
        layer_items = (("attn_norm", (D,)), ("pool_w", (4, HD, HD)), ("pool_scale", (W_POOL,)), ("conv", (4, NDEV * ncv)),
                       ("gdn_a_log", (NH,)), ("gdn_dt_bias", (NH,)), ("gdn_norm", (HD,)), ("mlp_norm", (D,)))
        per_layer = {k: [] for k, _ in layer_items}
        for l in range(NL):
            for k, shape in layer_items:
                size = math.prod(shape)
                nrow = -(-size // (8 * HD)) * 8
                per_layer[k].append(buf[r:r + nrow].reshape(-1)[:size].reshape(shape))
                r += nrow
        for k, _ in layer_items:
            out[k] = jnp.stack(per_layer[k])
        out["final_norm"] = buf[r:r + D // HD].reshape(D)
        out["loss"] = buf[r + D // HD, 0]
        return out

    sm = [unpack_small(b) for b in small_out]
    loss = sm[0]["loss"]
    g_conv = lax.dynamic_slice_in_dim(sm[0]["conv"], me * ncv, ncv, axis=2)
    conv_out = adamw(gdn_conv, m_gdn_conv, v_gdn_conv, "adamw_conv", g=g_conv)

    big_out = {}
    big_names = ("w_in", "w_pool_up", "w_sb_up", "w_gdn_up", "w_out", "w_ff1", "w_ff2")
    big_w = dict(zip(big_names, ((w_in, m_w_in, v_w_in), (w_pool_up, m_w_pool_up, v_w_pool_up), (w_sb_up, m_w_sb_up, v_w_sb_up),
                                 (w_gdn_up, m_w_gdn_up, v_w_gdn_up), (w_out, m_w_out, v_w_out), (w_ff1, m_w_ff1, v_w_ff1),
                                 (w_ff2, m_w_ff2, v_w_ff2))))
    for t, k in enumerate(big_names):
        w, m, v = big_w[k]
        r0, r1 = recv[t], recv[7 + t]
        if k == "w_in":
            g = _unpack_cols(sum_partials(r0, r1, "sum_w_in"))
            big_out[k] = adamw(w, m, v, "adamw_" + k, g=g)
        else:
            big_out[k] = adamw(w, m, v, "adamw_" + k, r0=r0, r1=r1)

    def leaf(i, k):
        if k in big_out:
            return big_out[k][i]
        if k == "gdn_conv":
            return conv_out[i]
        return sm[i][k]

    order = ("attn_norm", "w_in", "pool_w", "pool_scale", "gdn_conv", "gdn_a_log", "gdn_dt_bias", "gdn_norm", "w_pool_up",
             "w_sb_up", "w_gdn_up", "w_out", "mlp_norm", "w_ff1", "w_ff2", "final_norm")
    return (loss, dh[None]) + tuple(leaf(i, k) for i in range(4) for k in order)
```

```python
import functools
import math

import jax
import jax.numpy as jnp
from jax import lax
from jax.experimental import pallas as pl
from jax.experimental.pallas import tpu as pltpu
from jax.experimental.pallas import tpu_sc as plsc

f32, bf16 = jnp.float32, jnp.bfloat16

D = 2048
NDEV = 8
NL = 2
HD = 128
NH = 6
WH = NH * HD
W_POOL = 512
EPS = 1e-6
N_IN = 12044
NP = 12288
OFF_G, OFF_SB, OFF_GQ, OFF_Z, OFF_P, OFF_AB = 0, 6144, 8448, 10752, 11520, 12032
POOL_WINDOWS = (2, 4, 8, 16)
CH = 128
TQ = 256
VMEM_LIMIT = 56 * 1024 * 1024
ADAM_LR, ADAM_B1, ADAM_B2, ADAM_EPS, ADAM_WD, ADAM_STEP = 0.001, 0.9, 0.999, 1e-08, 0.01, 10
MESH = pl.DeviceIdType.MESH


def _cp(sem=None):
    return pltpu.CompilerParams(dimension_semantics=sem, vmem_limit_bytes=VMEM_LIMIT)


def _sds(shape, dtype):
    return jax.ShapeDtypeStruct(tuple(shape), dtype)


def matmul(a, b, *, name, ta=False, tb=False, out_dtype=f32, tm=1024, tn=1024, tk=2048, epilogue=None, extras=()):
    m, k = (a.shape[1], a.shape[0]) if ta else a.shape
    n = b.shape[0] if tb else b.shape[1]
    assert k == (b.shape[1] if tb else b.shape[0]) and a.dtype == bf16 and b.dtype == bf16
    tm, tn, tk = min(tm, m), min(tn, n), min(tk, k)
    assert m % tm == 0 and n % tn == 0 and k % tk == 0, (m, n, k, tm, tn, tk)
    nk = k // tk
    a_spec = pl.BlockSpec((tk, tm), lambda i, j, q: (q, i)) if ta else pl.BlockSpec((tm, tk), lambda i, j, q: (i, q))
    b_spec = pl.BlockSpec((tn, tk), lambda i, j, q: (j, q)) if tb else pl.BlockSpec((tk, tn), lambda i, j, q: (q, j))
    e_specs = [pl.BlockSpec((tm, tn), lambda i, j, q: (i, j)) for _ in extras]
    dn = (((0 if ta else 1,), (1 if tb else 0,)), ((), ()))
    ne = len(extras)

    def body(a_ref, b_ref, *rest):
        e_refs, o_ref = rest[:ne], rest[ne]
        part = lax.dot_general(a_ref[...], b_ref[...], dn, preferred_element_type=f32)

        def finish(acc):
            if epilogue is not None:
                acc = epilogue(acc, *[e[...] for e in e_refs])
            o_ref[...] = acc.astype(out_dtype)

        if nk == 1:
            finish(part)
        else:
            acc_ref = rest[ne + 1]
            q = pl.program_id(2)

            @pl.when(q == 0)
            def _():
                acc_ref[...] = part

            @pl.when(jnp.logical_and(q > 0, q < nk - 1))
            def _():
                acc_ref[...] += part

            @pl.when(q == nk - 1)
            def _():
                finish(acc_ref[...] + part)

    return pl.pallas_call(
        body, name=name, grid=(m // tm, n // tn, nk),
        in_specs=[a_spec, b_spec] + e_specs, out_specs=pl.BlockSpec((tm, tn), lambda i, j, q: (i, j)),
        out_shape=_sds((m, n), out_dtype),
        scratch_shapes=[pltpu.VMEM((tm, tn), f32)] if nk > 1 else [],
        compiler_params=_cp(("parallel", "parallel", "arbitrary")),
    )(a, b, *extras)


def rowwise(name, fn, rows, params, outs, sums=(), tr=256):
    s = rows[0][0].shape[0]
    tr = min(tr, s)
    nin, nout = len(rows) + len(params), len(outs)
    in_specs = [pl.BlockSpec((tr, w), functools.partial(lambda i, c: (i, c), c=c)) for (_, w, c) in rows]
    in_specs += [pl.BlockSpec(p.shape, lambda i: (0, 0)) for p in params]
    out_specs = [pl.BlockSpec((tr, w), lambda i: (i, 0)) for (w, _) in outs]
    out_specs += [pl.BlockSpec(sh, lambda i: (0, 0)) for sh in sums]
    out_shape = [_sds((s, w), dt) for (w, dt) in outs] + [_sds(sh, f32) for sh in sums]

    def body(*refs):
        res = fn(*[r[...] for r in refs[:nin]])
        for r, v in zip(refs[nin:nin + nout], res[:nout]):
            r[...] = v.astype(r.dtype)
        i = pl.program_id(0)
        for r, v in zip(refs[nin + nout:], res[nout:]):
            @pl.when(i == 0)
            def _(r=r, v=v):
                r[...] = v

            @pl.when(i > 0)
            def _(r=r, v=v):
                r[...] += v

    res = pl.pallas_call(
        body, name=name, grid=(s // tr,), in_specs=in_specs, out_specs=out_specs, out_shape=out_shape,
        compiler_params=_cp(("arbitrary",)),
    )(*[r[0] for r in rows], *params)
    return res


def _rms(x, g):
    return x * lax.rsqrt(jnp.mean(x * x, axis=-1, keepdims=True) + EPS) * g


def rms_fwd(x, g, name):
    return rowwise(name, lambda xb, gb: (_rms(xb, gb),), [(x, D, 0)], [g], [(D, bf16)])[0]


def rms_bwd(x, du, dres, g, name):
    def fn(xb, dub, drb, gb):
        _, vjp = jax.vjp(_rms, xb, gb)
        dx, dg = vjp(dub.astype(f32))
        return drb + dx, drb + dx, dg

    return rowwise(name, fn, [(x, D, 0), (du, D, 0), (dres, D, 0)], [g], [(D, f32), (D, bf16)], [(1, D)])


def _merge(gates, up_p, up_s, up_g):
    sg = jax.nn.sigmoid(gates)
    return sg[:, :D] * up_p + sg[:, D:2 * D] * up_s + sg[:, 2 * D:] * up_g


def merge_fwd(proj, ups, name):
    return rowwise(name, lambda g, a, b, c: (_merge(g, a, b, c),),
                   [(proj, 3 * D, 0)] + [(u, D, 0) for u in ups], [], [(D, bf16)], tr=128)[0]


def merge_bwd(proj, ups, dmerged, name):
    def fn(g, a, b, c, dm):
        _, vjp = jax.vjp(_merge, g, a, b, c)
        return vjp(dm.astype(f32))

    return rowwise(name, fn, [(proj, 3 * D, 0)] + [(u, D, 0) for u in ups] + [(dmerged, D, 0)], [],
                   [(3 * D, bf16), (D, bf16), (D, bf16), (D, bf16)], tr=128)


def _gdn_out(o, z, g):
    ys = []
    for h in range(NH):
        sl = slice(h * HD, (h + 1) * HD)
        ys.append(_rms(o[:, sl], g) * jax.nn.silu(z[:, sl]))
    return jnp.concatenate(ys, axis=1)


def gdn_out_fwd(o, proj, g, name):
    return rowwise(name, lambda ob, zb, gb: (_gdn_out(ob, zb, gb),), [(o, WH, 0), (proj, WH, OFF_Z // WH)], [g],
                   [(WH, bf16)])[0]


def gdn_out_bwd(o, proj, g, dy, name):
    def fn(ob, zb, dyb, gb):
        _, vjp = jax.vjp(_gdn_out, ob, zb, gb)
        return vjp(dyb.astype(f32))

    return rowwise(name, fn, [(o, WH, 0), (proj, WH, OFF_Z // WH), (dy, WH, 0)], [g], [(WH, f32), (WH, bf16)],
                   [(1, HD)])


def loss_head(x, target, g, name):
    def loss_fn(xb, gb, tb):
        err = _rms(xb, gb) - tb
        return (0.5 / D) * jnp.sum(jnp.sum(err * err, axis=1, keepdims=True), axis=0, keepdims=True)

    def fn(xb, tb, gb):
        val, vjp = jax.vjp(functools.partial(loss_fn, tb=tb), xb, gb)
        dx, dg = vjp(jnp.ones((1, 1), f32))
        return dx, dx, jnp.broadcast_to(val, (1, HD)), dg

    return rowwise(name, fn, [(x, D, 0), (target, D, 0)], [g], [(D, f32), (D, bf16)], [(1, HD), (1, D)])


PB = 256


def _split(v):
    hi = v.astype(bf16)
    return hi, (v - hi.astype(f32)).astype(bf16)


def _band_dot(make_band, v, s, forward):
    hi, lo = _split(v)
    nb = s // PB
    outs = []
    for r in range(nb):
        lo_r = max(r - 1, 0) if forward else r
        hi_r = r + 1 if forward else min(r + 2, nb)
        band = make_band(r * PB, lo_r * PB, (hi_r - lo_r) * PB)
        sl = slice(lo_r * PB, hi_r * PB)
        outs.append(jnp.dot(band, hi[sl], preferred_element_type=f32) + jnp.dot(band, lo[sl], preferred_element_type=f32))
    return jnp.concatenate(outs, axis=0)


def _pool_common(p, win, s):
    def band(row0, col0, ncol):
        t = row0 + lax.broadcasted_iota(jnp.int32, (PB, ncol), 0)
        u = col0 + lax.broadcasted_iota(jnp.int32, (PB, ncol), 1)
        return jnp.logical_and(u <= t, t < u + win).astype(bf16)

    def band_t(row0, col0, ncol):
        u = row0 + lax.broadcasted_iota(jnp.int32, (PB, ncol), 0)
        t = col0 + lax.broadcasted_iota(jnp.int32, (PB, ncol), 1)
        return jnp.logical_and(u <= t, t < u + win).astype(bf16)

    t = lax.broadcasted_iota(jnp.int32, (s, 1), 0)
    inv_n = 1.0 / jnp.minimum(t + 1, win).astype(f32)
    d = _band_dot(band, p, s, True) * inv_n - p
    return d, inv_n, band_t


def pool_fwd(proj, pool_w, pool_scale, name):
    s = proj.shape[0]

    def body(p_ref, w_ref, sc_ref, y_ref):
        win = jnp.left_shift(2, pl.program_id(0))
        d, _, _ = _pool_common(p_ref[...], win, s)
        y = jnp.dot(d.astype(bf16), w_ref[...].astype(bf16), preferred_element_type=f32) * sc_ref[...]
        y_ref[...] = y.astype(bf16)

    return pl.pallas_call(
        body, name=name, grid=(4,),
        in_specs=[pl.BlockSpec((s, HD), lambda g: (0, OFF_P // HD + g)), pl.BlockSpec((None, HD, HD), lambda g: (g, 0, 0)),
                  pl.BlockSpec((1, HD), lambda g: (0, g))],
        out_specs=pl.BlockSpec((s, HD), lambda g: (0, g)), out_shape=_sds((s, W_POOL), bf16),
        compiler_params=_cp(("arbitrary",)),
    )(proj, pool_w, pool_scale)


def pool_bwd(proj, pool_w, pool_scale, dy, name):
    s = proj.shape[0]

    def body(p_ref, w_ref, sc_ref, dy_ref, dp_ref, dw_ref, dsc_ref):
        win = jnp.left_shift(2, pl.program_id(0))
        d, inv_n, band_t = _pool_common(p_ref[...], win, s)
        w = w_ref[...].astype(bf16)
        dyf = dy_ref[...].astype(f32)
        dsc_ref[...] = jnp.sum(dyf * jnp.dot(d.astype(bf16), w, preferred_element_type=f32), axis=0, keepdims=True)
        dys = (dyf * sc_ref[...]).astype(bf16)
        dd = lax.dot_general(dys, w, (((1,), (1,)), ((), ())), preferred_element_type=f32)
        dw_ref[...] = lax.dot_general(d.astype(bf16), dys, (((0,), (0,)), ((), ())), preferred_element_type=f32)
        dp_ref[...] = (_band_dot(band_t, dd * inv_n, s, False) - dd).astype(bf16)

    return pl.pallas_call(
        body, name=name, grid=(4,),
        in_specs=[pl.BlockSpec((s, HD), lambda g: (0, OFF_P // HD + g)), pl.BlockSpec((None, HD, HD), lambda g: (g, 0, 0)),
                  pl.BlockSpec((1, HD), lambda g: (0, g)), pl.BlockSpec((s, HD), lambda g: (0, g))],
        out_specs=[pl.BlockSpec((s, HD), lambda g: (0, g)), pl.BlockSpec((None, HD, HD), lambda g: (g, 0, 0)),
                   pl.BlockSpec((1, HD), lambda g: (0, g))],
        out_shape=[_sds((s, W_POOL), bf16), _sds((4, HD, HD), f32), _sds((1, W_POOL), f32)],
        compiler_params=_cp(("arbitrary",)),
    )(proj, pool_w, pool_scale, dy)


def _sb_tile(q, k_ref, kb, qi, carry):
    k = k_ref[pl.ds(pl.multiple_of(kb * TQ, TQ), TQ), :].astype(bf16)
    z = lax.dot_general(q, k, (((1,), (1,)), ((), ())), preferred_element_type=f32)
    row = qi * TQ + lax.broadcasted_iota(jnp.int32, (TQ, TQ), 0)
    col = kb * TQ + lax.broadcasted_iota(jnp.int32, (TQ, TQ), 1)
    mask = col < row
    ls = jnp.where(mask, jax.nn.log_sigmoid(-z), 0.0)
    j = lax.broadcasted_iota(jnp.int32, (TQ, TQ), 0)
    u = lax.broadcasted_iota(jnp.int32, (TQ, TQ), 1)
    later = (j > u).astype(bf16)
    hi, lo = _split(ls)
    lw = jnp.dot(hi, later, preferred_element_type=f32) + jnp.dot(lo, later, preferred_element_type=f32)
    a = jnp.where(mask, jnp.exp(ls + z + lw + carry), 0.0)
    return z, mask, ls, a


def sb_fwd(proj, name):
    s = proj.shape[0]
    nq = s // TQ
    scale = HD ** -0.5

    def body(q_ref, k_ref, v_ref, y_ref):
        qi = pl.program_id(1)
        q = (q_ref[...] * scale).astype(bf16)

        def step(j, c):
            acc, carry = c
            kb = qi - j
            _, _, ls, a = _sb_tile(q, k_ref, kb, qi, carry)
            v = v_ref[pl.ds(pl.multiple_of(kb * TQ, TQ), TQ), :].astype(bf16)
            acc = acc + jnp.dot(a.astype(bf16), v, preferred_element_type=f32)
            return acc, carry + jnp.sum(ls, axis=1, keepdims=True)

        acc, _ = lax.fori_loop(0, qi + 1, step, (jnp.zeros((TQ, HD), f32), jnp.zeros((TQ, 1), f32)))
        y_ref[...] = acc.astype(bf16)

    c0 = OFF_SB // HD
    return pl.pallas_call(
        body, name=name, grid=(NH, nq),
        in_specs=[pl.BlockSpec((TQ, HD), lambda h, i: (i, c0 + h)), pl.BlockSpec((s, HD), lambda h, i: (0, c0 + NH + h)),
                  pl.BlockSpec((s, HD), lambda h, i: (0, c0 + 2 * NH + h))],
        out_specs=pl.BlockSpec((TQ, HD), lambda h, i: (i, h)), out_shape=_sds((s, WH), bf16),
        compiler_params=_cp(("arbitrary", "arbitrary")),
    )(proj, proj, proj)


def sb_bwd(proj, dy, name):
    s = proj.shape[0]
    nq = s // TQ
    scale = HD ** -0.5

    def body(q_ref, k_ref, v_ref, do_ref, dq_ref, dk_ref, dv_ref, e_scr, z_scr, dk_acc, dv_acc):
        qi = pl.program_id(1)
        q = (q_ref[...] * scale).astype(bf16)
        do = do_ref[...]

        @pl.when(qi == 0)
        def _():
            dk_acc[...] = jnp.zeros_like(dk_acc)
            dv_acc[...] = jnp.zeros_like(dv_acc)

        def sweep_back(j, carry):
            kb = qi - j
            rows = pl.ds(pl.multiple_of(kb * TQ, TQ), TQ)
            z, _, ls, a = _sb_tile(q, k_ref, kb, qi, carry)
            v = v_ref[rows, :].astype(bf16)
            da = lax.dot_general(do, v, (((1,), (1,)), ((), ())), preferred_element_type=f32)
            e_scr[kb] = da * a
            z_scr[kb] = z
            dv_acc[rows, :] += lax.dot_general(a.astype(bf16), do, (((0,), (0,)), ((), ())), preferred_element_type=f32)
            return carry + jnp.sum(ls, axis=1, keepdims=True)

        lax.fori_loop(0, qi + 1, sweep_back, jnp.zeros((TQ, 1), f32))

        def sweep_fwd(kb, c):
            dq, carry = c
            rows = pl.ds(pl.multiple_of(kb * TQ, TQ), TQ)
            e, z = e_scr[kb], z_scr[kb]
            row = qi * TQ + lax.broadcasted_iota(jnp.int32, (TQ, TQ), 0)
            col = kb * TQ + lax.broadcasted_iota(jnp.int32, (TQ, TQ), 1)
            j = lax.broadcasted_iota(jnp.int32, (TQ, TQ), 0)
            u = lax.broadcasted_iota(jnp.int32, (TQ, TQ), 1)
            earlier = (j < u).astype(bf16)
            hi, lo = _split(e)
            ew = jnp.dot(hi, earlier, preferred_element_type=f32) + jnp.dot(lo, earlier, preferred_element_type=f32)
            sig = jax.nn.sigmoid(z)
            dz = jnp.where(col < row, e * (1.0 - sig) - (ew + carry) * sig, 0.0).astype(bf16)
            k = k_ref[rows, :].astype(bf16)
            dq = dq + jnp.dot(dz, k, preferred_element_type=f32)
            dk_acc[rows, :] += lax.dot_general(dz, q, (((0,), (0,)), ((), ())), preferred_element_type=f32)
            return dq, carry + jnp.sum(e, axis=1, keepdims=True)

        dq, _ = lax.fori_loop(0, qi + 1, sweep_fwd, (jnp.zeros((TQ, HD), f32), jnp.zeros((TQ, 1), f32)))
        dq_ref[...] = (dq * scale).astype(bf16)

        @pl.when(qi == nq - 1)
        def _():
            dk_ref[...] = dk_acc[...].astype(bf16)
            dv_ref[...] = dv_acc[...].astype(bf16)

    c0 = OFF_SB // HD
    return pl.pallas_call(
        body, name=name, grid=(NH, nq),
        in_specs=[pl.BlockSpec((TQ, HD), lambda h, i: (i, c0 + h)), pl.BlockSpec((s, HD), lambda h, i: (0, c0 + NH + h)),
                  pl.BlockSpec((s, HD), lambda h, i: (0, c0 + 2 * NH + h)), pl.BlockSpec((TQ, HD), lambda h, i: (i, h))],
        out_specs=[pl.BlockSpec((TQ, HD), lambda h, i: (i, h)), pl.BlockSpec((s, HD), lambda h, i: (0, h)),
                   pl.BlockSpec((s, HD), lambda h, i: (0, h))],
        out_shape=[_sds((s, WH), bf16)] * 3,
        scratch_shapes=[pltpu.VMEM((nq, TQ, TQ), f32), pltpu.VMEM((nq, TQ, TQ), f32), pltpu.VMEM((s, HD), f32),
                        pltpu.VMEM((s, HD), f32)],
        compiler_params=_cp(("arbitrary", "arbitrary")),
    )(proj, proj, proj, dy)


CB = 256


def _shift_down(v, k, s):
    if k == 0:
        return v
    row = lax.broadcasted_iota(jnp.int32, v.shape, 0)
    return jnp.where(row < k, 0.0, pltpu.roll(v, k, axis=0))


def _shift_up(v, k, s):
    if k == 0:
        return v
    row = lax.broadcasted_iota(jnp.int32, v.shape, 0)
    return jnp.where(row >= s - k, 0.0, pltpu.roll(v, s - k, axis=0))


def conv_fwd(proj, w, name):
    s = proj.shape[0]

    def body(x_ref, w_ref, y_ref):
        x, wv = x_ref[...], w_ref[...]
        y = sum(wv[3 - k:4 - k, :] * _shift_down(x, k, s) for k in range(4))
        y_ref[...] = jax.nn.silu(y)

    return pl.pallas_call(
        body, name=name, grid=(3 * WH // CB,),
        in_specs=[pl.BlockSpec((s, CB), lambda j: (0, OFF_GQ // CB + j)), pl.BlockSpec((4, CB), lambda j: (0, j))],
        out_specs=pl.BlockSpec((s, CB), lambda j: (0, j)), out_shape=_sds((s, 3 * WH), f32),
        compiler_params=_cp(("parallel",)),
    )(proj, w)


def conv_bwd(proj, w, dc, name):
    s = proj.shape[0]

    def body(x_ref, w_ref, dc_ref, dx_ref, dw_ref):
        x, wv = x_ref[...], w_ref[...]
        xs = [_shift_down(x, k, s) for k in range(4)]
        y = sum(wv[3 - k:4 - k, :] * xs[k] for k in range(4))
        sig = jax.nn.sigmoid(y)
        dy = dc_ref[...] * (sig * (1.0 + y * (1.0 - sig)))
        dx_ref[...] = sum(wv[3 - k:4 - k, :] * _shift_up(dy, k, s) for k in range(4)).astype(bf16)
        dw_ref[...] = jnp.concatenate([jnp.sum(dy * xs[3 - i], axis=0, keepdims=True) for i in range(4)], axis=0)

    return pl.pallas_call(
        body, name=name, grid=(3 * WH // CB,),
        in_specs=[pl.BlockSpec((s, CB), lambda j: (0, OFF_GQ // CB + j)), pl.BlockSpec((4, CB), lambda j: (0, j)),
                  pl.BlockSpec((s, CB), lambda j: (0, j))],
        out_specs=[pl.BlockSpec((s, CB), lambda j: (0, j)), pl.BlockSpec((4, CB), lambda j: (0, j))],
        out_shape=[_sds((s, 3 * WH), bf16), _sds((4, 3 * WH), f32)],
        compiler_params=_cp(("parallel",)),
    )(proj, w, dc)


SOLVE_PASSES = 3
NN, NT, TN = (((1,), (0,)), ((), ())), (((1,), (1,)), ((), ())), (((0,), (0,)), ((), ()))


def _pdot_impl(a, b, dn, passes):
    ah, al = _split(a)
    bh, bl = _split(b)
    dot = lambda p, q: lax.dot_general(p, q, dn, preferred_element_type=f32)
    if passes == 1:
        return dot(ah, bh)
    if passes == 2:
        return dot(ah, bh) + dot(ah, bl)
    return dot(ah, bh) + (dot(ah, bl) + dot(al, bh))


@functools.partial(jax.custom_vjp, nondiff_argnums=(2,))
def _pdot(a, b, passes):
    return _pdot_impl(a, b, NN, passes)


def _pdot_fwd(a, b, passes):
    return _pdot_impl(a, b, NN, passes), (a, b)


def _pdot_bwd(passes, res, ct):
    a, b = res
    return _pdot_impl(ct, b, NT, passes), _pdot_impl(a, ct, TN, passes)


_pdot.defvjp(_pdot_fwd, _pdot_bwd)


@functools.partial(jax.custom_vjp, nondiff_argnums=(2,))
def _pdot_nt(a, b, passes):
    return _pdot_impl(a, b, NT, passes)


def _pdot_nt_fwd(a, b, passes):
    return _pdot_impl(a, b, NT, passes), (a, b)


def _pdot_nt_bwd(passes, res, ct):
    a, b = res
    return _pdot_impl(ct, b, NN, passes), _pdot_impl(ct, a, TN, passes)


_pdot_nt.defvjp(_pdot_nt_fwd, _pdot_nt_bwd)


def _lane_pick(v, h):
    lane = lax.broadcasted_iota(jnp.int32, v.shape, v.ndim - 1)
    return jnp.sum(jnp.where(lane == h, v, 0.0), axis=-1, keepdims=True)


def _l2n(v):
    return v * lax.rsqrt(jnp.sum(v * v, axis=-1, keepdims=True) + EPS)


def _dot_nt(a, b):
    return lax.dot_general(a, b, (((1,), (1,)), ((), ())), preferred_element_type=f32)


def _gdn_chunk(cq, ck, cv, ab, alog, dtb, h):
    ones = jnp.ones((CH, HD), f32)
    q = _l2n(cq) * (HD ** -0.5)
    k = _l2n(ck)
    la = -jnp.exp(_lane_pick(alog, h)) * jax.nn.softplus(_lane_pick(ab, h) + _lane_pick(dtb, h))
    beta = jax.nn.sigmoid(_lane_pick(ab, NH + h)) * ones
    i = lax.broadcasted_iota(jnp.int32, (CH, CH), 0)
    j = lax.broadcasted_iota(jnp.int32, (CH, CH), 1)
    incl, strict = j <= i, j < i
    g = _pdot(incl.astype(f32), la * ones, 2)
    diff = g - g.T
    gamma = jnp.where(incl, jnp.exp(jnp.where(incl, diff, 0.0)), 0.0)
    lower = jnp.where(strict, beta * _pdot_nt(k, k, 1) * gamma, 0.0)
    eye = (i == j).astype(f32)
    inv = eye - lower
    pw = _pdot(lower, lower, SOLVE_PASSES)
    for m in range(1, int(math.log2(CH))):
        inv = inv + _pdot(inv, pw, SOLVE_PASSES)
        if m < int(math.log2(CH)) - 1:
            pw = _pdot(pw, pw, SOLVE_PASSES)
    eg = jnp.exp(g)
    u = _pdot(inv, cv * beta, SOLVE_PASSES)
    w = _pdot(inv, k * (beta * eg), SOLVE_PASSES)
    qk = _pdot_nt(q, k, 1) * gamma
    g_last = g[CH - 1:CH, :]
    return u, w, q * eg, k * jnp.exp(g_last - g), qk, jnp.exp(g_last)


def gdn_a_fwd(c, proj, alog, dtb, name):
    s = c.shape[0]
    nc = s // CH

    def body(c_ref, ab_ref, al_ref, dt_ref, u_ref, w_ref, qd_ref, kd_ref, qk_ref, dec_ref):
        for h in range(NH):
            sl = slice(h * HD, (h + 1) * HD)
            cq, ck, cv = (c_ref[:, t * WH + h * HD:t * WH + (h + 1) * HD] for t in range(3))
            res = _gdn_chunk(cq, ck, cv, ab_ref[...], al_ref[...], dt_ref[...], h)
            for r, v in zip((u_ref, w_ref, qd_ref, kd_ref, qk_ref), res[:5]):
                r[:, sl] = v
            dec_ref[:, sl] = jnp.broadcast_to(res[5], (8, HD))

    row = pl.BlockSpec((CH, WH), lambda n: (n, 0))
    par = pl.BlockSpec((1, HD), lambda n: (0, 0))
    return pl.pallas_call(
        body, name=name, grid=(nc,),
        in_specs=[pl.BlockSpec((CH, 3 * WH), lambda n: (n, 0)), pl.BlockSpec((CH, HD), lambda n: (n, OFF_AB // HD)), par, par],
        out_specs=[row] * 5 + [pl.BlockSpec((None, 8, WH), lambda n: (n, 0, 0))],
        out_shape=[_sds((s, WH), f32)] * 5 + [_sds((nc, 8, WH), f32)],
        compiler_params=_cp(("parallel",)),
    )(c, proj, alog, dtb)


def gdn_a_bwd(c, proj, alog, dtb, cots, ddec, name):
    s = c.shape[0]
    nc = s // CH

    def body(c_ref, ab_ref, al_ref, dt_ref, du_ref, dw_ref, dqd_ref, dkd_ref, dqk_ref, ddec_ref,
             dc_ref, dab_ref, dal_ref, ddt_ref):
        n = pl.program_id(0)
        dab, dal, ddt = jnp.zeros((CH, HD), f32), jnp.zeros((1, HD), f32), jnp.zeros((1, HD), f32)
        for h in range(NH):
            sl = slice(h * HD, (h + 1) * HD)
            cq, ck, cv = (c_ref[:, t * WH + h * HD:t * WH + (h + 1) * HD] for t in range(3))
            _, vjp = jax.vjp(functools.partial(_gdn_chunk, h=h), cq, ck, cv, ab_ref[...], al_ref[...], dt_ref[...])
            lane = lax.broadcasted_iota(jnp.int32, (1, HD), 1)
            dd = jnp.where(lane == 0, ddec_ref[0:1, sl], 0.0)
            dcs = vjp((du_ref[:, sl], dw_ref[:, sl], dqd_ref[:, sl], dkd_ref[:, sl], dqk_ref[:, sl], dd))
            for t in range(3):
                dc_ref[:, t * WH + h * HD:t * WH + (h + 1) * HD] = dcs[t]
            dab, dal, ddt = dab + dcs[3], dal + dcs[4], ddt + dcs[5]
        dab_ref[...] = dab.astype(bf16)

        @pl.when(n == 0)
        def _():
            dal_ref[...] = dal
            ddt_ref[...] = ddt

        @pl.when(n > 0)
        def _():
            dal_ref[...] += dal
            ddt_ref[...] += ddt

    row = pl.BlockSpec((CH, WH), lambda n: (n, 0))
    wide = pl.BlockSpec((CH, 3 * WH), lambda n: (n, 0))
    par = pl.BlockSpec((1, HD), lambda n: (0, 0))
    return pl.pallas_call(
        body, name=name, grid=(nc,),
        in_specs=[wide, pl.BlockSpec((CH, HD), lambda n: (n, OFF_AB // HD)), par, par] + [row] * 5
        + [pl.BlockSpec((None, 8, WH), lambda n: (n, 0, 0))],
        out_specs=[wide, pl.BlockSpec((CH, HD), lambda n: (n, 0)), par, par],
        out_shape=[_sds((s, 3 * WH), f32), _sds((s, HD), bf16), _sds((1, HD), f32), _sds((1, HD), f32)],
        compiler_params=_cp(("arbitrary",)),
    )(c, proj, alog, dtb, *cots, ddec)


def gdn_b_fwd(u, w, qd, kd, qk, dec, name):
    s = u.shape[0]
    nc = s // CH

    def body(u_ref, w_ref, qd_ref, kd_ref, qk_ref, dec_ref, o_ref, st_ref, state):
        n = pl.program_id(0)

        @pl.when(n == 0)
        def _():
            state[...] = jnp.zeros_like(state)

        for h in range(NH):
            sl = slice(h * HD, (h + 1) * HD)
            st = state[sl, :]
            st_ref[sl, :] = st
            sb = st.astype(bf16)
            vn = u_ref[:, sl] - jnp.dot(w_ref[:, sl].astype(bf16), sb, preferred_element_type=f32)
            vb = vn.astype(bf16)
            o_ref[:, sl] = (jnp.dot(qd_ref[:, sl].astype(bf16), sb, preferred_element_type=f32)
                            + jnp.dot(qk_ref[:, sl].astype(bf16), vb, preferred_element_type=f32))
            state[sl, :] = st * dec_ref[0:1, sl] + lax.dot_general(
                kd_ref[:, sl].astype(bf16), vb, (((0,), (0,)), ((), ())), preferred_element_type=f32)

    row = pl.BlockSpec((CH, WH), lambda n: (n, 0))
    return pl.pallas_call(
        body, name=name, grid=(nc,),
        in_specs=[row] * 5 + [pl.BlockSpec((None, 8, WH), lambda n: (n, 0, 0))],
        out_specs=[row, pl.BlockSpec((None, WH, HD), lambda n: (n, 0, 0))],
        out_shape=[_sds((s, WH), f32), _sds((nc, WH, HD), f32)],
        scratch_shapes=[pltpu.VMEM((WH, HD), f32)],
        compiler_params=_cp(("arbitrary",)),
    )(u, w, qd, kd, qk, dec)


def gdn_b_bwd(u, w, qd, kd, qk, dec, states, do, name):
    s = u.shape[0]
    nc = s // CH

    def body(u_ref, w_ref, qd_ref, kd_ref, qk_ref, dec_ref, st_ref, do_ref,
             du_ref, dw_ref, dqd_ref, dkd_ref, dqk_ref, ddec_ref, dstate):
        n = pl.program_id(0)

        @pl.when(n == 0)
        def _():
            dstate[...] = jnp.zeros_like(dstate)

        for h in range(NH):
            sl = slice(h * HD, (h + 1) * HD)
            st, ds = st_ref[sl, :], dstate[sl, :]
            sb, dsb = st.astype(bf16), ds.astype(bf16)
            wb, qdb, kdb, qkb = (r[:, sl].astype(bf16) for r in (w_ref, qd_ref, kd_ref, qk_ref))
            dob = do_ref[:, sl].astype(bf16)
            vn = u_ref[:, sl] - jnp.dot(wb, sb, preferred_element_type=f32)
            vb = vn.astype(bf16)
            dvn = (lax.dot_general(qkb, dob, (((0,), (0,)), ((), ())), preferred_element_type=f32)
                   + jnp.dot(kdb, dsb, preferred_element_type=f32))
            dvb = dvn.astype(bf16)
            du_ref[:, sl] = dvn
            dw_ref[:, sl] = -_dot_nt(dvb, sb)
            dqd_ref[:, sl] = _dot_nt(dob, sb)
            dkd_ref[:, sl] = _dot_nt(vb, dsb)
            dqk_ref[:, sl] = _dot_nt(dob, vb)
            tot = jnp.sum(jnp.sum(ds * st, axis=1, keepdims=True), axis=0, keepdims=True)
            ddec_ref[:, sl] = jnp.broadcast_to(tot, (8, HD))
            dstate[sl, :] = (ds * dec_ref[0:1, sl]
                             + lax.dot_general(qdb, dob, (((0,), (0,)), ((), ())), preferred_element_type=f32)
                             - lax.dot_general(wb, dvb, (((0,), (0,)), ((), ())), preferred_element_type=f32))

    row = pl.BlockSpec((CH, WH), lambda n: (nc - 1 - n, 0))
    small = pl.BlockSpec((None, 8, WH), lambda n: (nc - 1 - n, 0, 0))
    return pl.pallas_call(
        body, name=name, grid=(nc,),
        in_specs=[row] * 5 + [small, pl.BlockSpec((None, WH, HD), lambda n: (nc - 1 - n, 0, 0)), row],
        out_specs=[row] * 5 + [small],
        out_shape=[_sds((s, WH), f32)] * 5 + [_sds((nc, 8, WH), f32)],
        scratch_shapes=[pltpu.VMEM((WH, HD), f32)],
        compiler_params=_cp(("arbitrary",)),
    )(u, w, qd, kd, qk, dec, states, do)


ANY = pl.BlockSpec(memory_space=pl.ANY)


def _dev_index(p):
    return 4 * p[0] + 2 * p[1] + p[2]


def _shard_of(ref, axis, size, idx):
    return ref.at[pl.ds(idx * size, size), :] if axis == 0 else ref.at[:, pl.ds(idx * size, size)]


def _launch(body, ins, out_sds, name, sequencer_id):
    n = len(ins)
    sems = [pltpu.SemaphoreType.DMA((n, 7)), pltpu.SemaphoreType.DMA((n, 7)), pltpu.SemaphoreType.DMA((n,))]
    if sequencer_id is None:
        return pl.pallas_call(
            lambda *refs: body(refs[:n], refs[n:2 * n], *refs[2 * n:]), name=name, in_specs=[ANY] * n, out_specs=[ANY] * n,
            out_shape=out_sds, scratch_shapes=sems, compiler_params=pltpu.CompilerParams(has_side_effects=True),
        )(*ins)
    in_refs = [jax.new_ref(a, memory_space=pltpu.MemorySpace.HBM) for a in ins]
    out_refs = [jax.empty_ref(sd, memory_space=pltpu.MemorySpace.HBM) for sd in out_sds]

    @pl.kernel(mesh=plsc.ScalarSubcoreMesh(axis_name="sequencer", num_cores=1), name=name, scratch_types=sems,
               compiler_params=pltpu.CompilerParams(collective_id=sequencer_id))
    def launch(send, recv, loc):
        x, y, c = lax.axis_index("x"), lax.axis_index("y"), lax.axis_index("c")
        barrier = pltpu.get_barrier_semaphore()
        for r in range(1, NDEV):
            peer = (1 - x if r & 4 else x, 1 - y if r & 2 else y, 1 - c if r & 1 else c)
            pl.semaphore_signal(barrier, inc=1, device_id=peer, device_id_type=MESH)
        pl.semaphore_wait(barrier, NDEV - 1)
        body(in_refs, out_refs, send, recv, loc)

    launch()
    return [r[...] for r in out_refs]


def all_gather(xs, axes, name, sequencer_id=None):
    n = len(xs)
    fulls = [tuple(d * (NDEV if a == ax else 1) for a, d in enumerate(x.shape)) for x, ax in zip(xs, axes)]

    def body(x_refs, o_refs, send, recv, loc):
        x, y, c = lax.axis_index("x"), lax.axis_index("y"), lax.axis_index("c")
        me, sib = (x, y, c), (x, y, 1 - c)
        chips = [(1 - x, y), (x, 1 - y), (1 - x, 1 - y)]

        def part(t, p):
            return _shard_of(o_refs[t], axes[t], xs[t].shape[axes[t]], _dev_index(p))

        def copy(t, k, block, to, src=None):
            return pltpu.make_async_remote_copy(
                src_ref=part(t, block) if src is None else src, dst_ref=part(t, block),
                send_sem=send.at[t, k], recv_sem=recv.at[t, k], device_id=to, device_id_type=MESH)

        mine = [pltpu.make_async_copy(x_refs[t], part(t, me), loc.at[t]) for t in range(n)]
        for cp in mine:
            cp.start()
        first = []
        for t in range(n):
            first.append(copy(t, 0, me, sib, src=x_refs[t]))
            first += [copy(t, 1 + j, me, (*chip, c), src=x_refs[t]) for j, chip in enumerate(chips)]
        for cp in first:
            cp.start()
        passed = []
        for t in range(n):
            for j, chip in enumerate(chips):
                copy(t, 1 + j, (*chip, c), me).wait_recv()
                passed.append(copy(t, 4 + j, (*chip, c), sib))
                passed[-1].start()
        for t in range(n):
            copy(t, 0, sib, me).wait_recv()
            for j, chip in enumerate(chips):
                copy(t, 4 + j, (*chip, 1 - c), me).wait_recv()
        for cp in first + passed:
            cp.wait_send()
        for cp in mine:
            cp.wait()

    return _launch(body, xs, [_sds(f, x.dtype) for f, x in zip(fulls, xs)], name, sequencer_id)


def exchange(gs, axes, name, sequencer_id=None):
    n = len(gs)
    shards = [tuple(d // (NDEV if a == ax else 1) for a, d in enumerate(g.shape)) for g, ax in zip(gs, axes)]

    def body(g_refs, r_refs, send, recv, loc):
        x, y, c = lax.axis_index("x"), lax.axis_index("y"), lax.axis_index("c")
        me = _dev_index((x, y, c))
        peers = []
        for r in range(1, NDEV):
            peers.append((1 - x if r & 4 else x, 1 - y if r & 2 else y, 1 - c if r & 1 else c))

        def src(t, idx):
            return _shard_of(g_refs[t], axes[t], shards[t][axes[t]], idx)

        mine = [pltpu.make_async_copy(src(t, me), r_refs[t].at[me], loc.at[t]) for t in range(n)]
        for cp in mine:
            cp.start()
        sends = []
        for t in range(n):
            for k, p in enumerate(peers):
                sends.append(pltpu.make_async_remote_copy(
                    src_ref=src(t, _dev_index(p)), dst_ref=r_refs[t].at[me], send_sem=send.at[t, k], recv_sem=recv.at[t, k],
                    device_id=p, device_id_type=MESH))
                sends[-1].start()
        for t in range(n):
            for k, p in enumerate(peers):
                pltpu.make_async_remote_copy(
                    src_ref=src(t, me), dst_ref=r_refs[t].at[_dev_index(p)], send_sem=send.at[t, k], recv_sem=recv.at[t, k],
                    device_id=p, device_id_type=MESH).wait_recv()
        for cp in sends:
            cp.wait_send()
        for cp in mine:
            cp.wait()

    return _launch(body, gs, [_sds((NDEV,) + sh, g.dtype) for sh, g in zip(shards, gs)], name, sequencer_id)


def _adamw(w, g, m, v):
    m = ADAM_B1 * m + (1.0 - ADAM_B1) * g
    v = ADAM_B2 * v + (1.0 - ADAM_B2) * jnp.square(g)
    m_hat = m / (1.0 - ADAM_B1 ** ADAM_STEP)
    v_hat = v / (1.0 - ADAM_B2 ** ADAM_STEP)
    return -ADAM_LR * (m_hat / (jnp.sqrt(v_hat) + ADAM_EPS) + ADAM_WD * w), m, v


def _sum8(r_ref):
    g = r_ref[0].astype(f32)
    for j in range(1, NDEV):
        g = g + r_ref[j].astype(f32)
    return g


def _row_tile(rows, cols):
    tr = min(rows, max(8, 1 << int(math.log2((1 << 18) / cols))))
    assert rows % tr == 0, (rows, cols)
    return tr


def sum_partials(r0, r1, name):
    _, rows, cols = r0.shape
    tr = _row_tile(rows, cols)
    nb = rows // tr

    def body(r0_ref, r1_ref, g_ref):
        layer = pl.program_id(0)

        @pl.when(layer == 0)
        def _():
            g_ref[...] = _sum8(r0_ref)

        @pl.when(layer == 1)
        def _():
            g_ref[...] = _sum8(r1_ref)

    return pl.pallas_call(
        body, name=name, grid=(NL, nb),
        in_specs=[pl.BlockSpec((NDEV, tr, cols), lambda l, i: (0, jnp.where(l == 0, i, nb - 1), 0)),
                  pl.BlockSpec((NDEV, tr, cols), lambda l, i: (0, jnp.where(l == 1, i, 0), 0))],
        out_specs=pl.BlockSpec((None, tr, cols), lambda l, i: (l, i, 0)), out_shape=_sds((NL, rows, cols), f32),
        compiler_params=_cp(("arbitrary", "arbitrary")),
    )(r0, r1)


def adamw(w, m, v, name, g=None, r0=None, r1=None):
    _, rows, cols = w.shape
    tr = _row_tile(rows, cols)
    nb = rows // tr
    blk = pl.BlockSpec((None, tr, cols), lambda l, i: (l, i, 0))
    summed = g is None

    def body(*refs):
        w_ref, m_ref, v_ref = refs[:3]
        outs = refs[-4:] if summed else refs[-3:]
        if summed:
            layer = pl.program_id(0)

            @pl.when(layer == 0)
            def _():
                outs[0][...] = _sum8(refs[3])

            @pl.when(layer == 1)
            def _():
                outs[0][...] = _sum8(refs[4])

            grad = outs[0][...]
        else:
            grad = refs[3][...]
        delta, nm, nv = _adamw(w_ref[...], grad, m_ref[...], v_ref[...])
        outs[-3][...], outs[-2][...], outs[-1][...] = delta, nm, nv

    if summed:
        extra = [r0, r1]
        extra_specs = [pl.BlockSpec((NDEV, tr, cols), lambda l, i: (0, jnp.where(l == 0, i, nb - 1), 0)),
                       pl.BlockSpec((NDEV, tr, cols), lambda l, i: (0, jnp.where(l == 1, i, 0), 0))]
    else:
        extra, extra_specs = [g], [blk]
    nout = 4 if summed else 3
    res = pl.pallas_call(
        body, name=name, grid=(NL, nb), in_specs=[blk] * 3 + extra_specs, out_specs=[blk] * nout,
        out_shape=[_sds(w.shape, f32)] * nout, compiler_params=_cp(("arbitrary", "arbitrary")),
    )(w, m, v, *extra)
    return tuple(res) if summed else (g,) + tuple(res)


def small_adamw(parts, w, m, v, name):
    def body(p_ref, w_ref, m_ref, v_ref, g_ref, d_ref, nm_ref, nv_ref):
        g = _sum8(p_ref)
        g_ref[...] = g
        d_ref[...], nm_ref[...], nv_ref[...] = _adamw(w_ref[...], g, m_ref[...], v_ref[...])

    return pl.pallas_call(body, name=name, out_shape=[_sds(w.shape, f32)] * 4, compiler_params=_cp())(parts, w, m, v)


def _pack_cols(w):
    pad = jnp.zeros((w.shape[0], NP - OFF_AB - 12), w.dtype)
    return jnp.concatenate([w[:, 5900:12044], w[:, 512:2816], w[:, 2816:5120], w[:, 5120:5888], w[:, 0:512],
                            w[:, 5888:5900], pad], axis=1)


def _unpack_cols(g):
    return jnp.concatenate([g[..., OFF_P:OFF_P + 512], g[..., OFF_SB:OFF_SB + 2304], g[..., OFF_GQ:OFF_GQ + 2304],
                            g[..., OFF_Z:OFF_Z + 768], g[..., OFF_AB:OFF_AB + 12], g[..., OFF_G:OFF_G + 6144]], axis=-1)


def _lanes(v):
    flat = v.reshape(-1)
    n = -(-flat.shape[0] // HD) * HD
    return jnp.pad(flat, (0, n - flat.shape[0])).reshape(n // HD, HD)


def _lanes8(v):
    rows = _lanes(v)
    return jnp.pad(rows, ((0, -rows.shape[0] % 8), (0, 0)))


def _layer_fwd(x, p, l):
    nm = lambda s: f"{s}_l{l}"
    u = rms_fwd(x, p["attn_norm"], nm("rms1"))
    proj = matmul(u, p["w_in"], name=nm("inproj"))
    y_pool = pool_fwd(proj, p["pool_w"], p["pool_scale"], nm("pool"))
    y_sb = sb_fwd(proj, nm("sb"))
    c = conv_fwd(proj, p["conv"], nm("conv"))
    ga = gdn_a_fwd(c, proj, p["alog"], p["dtb"], nm("gdna"))
    o, states = gdn_b_fwd(*ga, nm("gdnb"))
    y_gdn = gdn_out_fwd(o, proj, p["gdn_norm"], nm("gdno"))
    ups = [matmul(y, p[k], name=nm(k)) for y, k in ((y_pool, "w_pool_up"), (y_sb, "w_sb_up"), (y_gdn, "w_gdn_up"))]
    merged = merge_fwd(proj, ups, nm("merge"))
    x1 = matmul(merged, p["w_out"], name=nm("outproj"), epilogue=lambda acc, r: acc + r, extras=(x,))
    u2 = rms_fwd(x1, p["mlp_norm"], nm("rms2"))
    h2 = matmul(u2, p["w_ff1"], name=nm("ff1"), out_dtype=bf16, epilogue=lambda acc: jnp.square(jnp.maximum(acc, 0.0)))
    x2 = matmul(h2, p["w_ff2"], name=nm("ff2"), epilogue=lambda acc, r: acc + r, extras=(x1,))
    saved = dict(x=x, u=u, proj=proj, y_pool=y_pool, y_sb=y_sb, c=c, ga=ga, o=o, states=states, y_gdn=y_gdn, ups=ups,
                 merged=merged, x1=x1, u2=u2, h2=h2)
    return x2, saved


def _layer_bwd(dx2, dx2b, sv, p, l, send):
    nm = lambda s: f"{s}_l{l}"
    s = dx2.shape[0]
    dh = matmul(dx2b, p["w_ff2"], name=nm("d_ff2_x"), tb=True, out_dtype=bf16,
                epilogue=lambda acc, h2: acc * (2.0 * jnp.sqrt(h2.astype(f32))), extras=(sv["h2"],))
    g_ff2 = matmul(sv["h2"], dx2b, name=nm("d_ff2_w"), ta=True, out_dtype=bf16)
    du2 = matmul(dh, p["w_ff1"], name=nm("d_ff1_x"), tb=True)
    g_ff1 = matmul(sv["u2"], dh, name=nm("d_ff1_w"), ta=True, out_dtype=bf16)
    r_ff = send([g_ff1, g_ff2], (1, 0), "ff")
    dx1, dx1b, d_mlp_norm = rms_bwd(sv["x1"], du2, dx2, p["mlp_norm"], nm("d_rms2"))
    dmerged = matmul(dx1b, p["w_out"], name=nm("d_out_x"), tb=True, out_dtype=bf16)
    g_out = matmul(sv["merged"], dx1b, name=nm("d_out_w"), ta=True, out_dtype=bf16)
    dgates, *dups = merge_bwd(sv["proj"], sv["ups"], dmerged, nm("d_merge"))
    dys, g_ups = [], []
    for dup, y, k in zip(dups, (sv["y_pool"], sv["y_sb"], sv["y_gdn"]), ("w_pool_up", "w_sb_up", "w_gdn_up")):
        dys.append(matmul(dup, p[k], name=nm("d_" + k + "_x"), tb=True, out_dtype=bf16))
        g_ups.append(matmul(y, dup, name=nm("d_" + k + "_w"), ta=True, out_dtype=bf16))
    r_mid = send(g_ups + [g_out], (1, 1, 1, 0), "mid")
    do, dz, d_gdn_norm = gdn_out_bwd(sv["o"], sv["proj"], p["gdn_norm"], dys[2], nm("d_gdno"))
    cots = gdn_b_bwd(*sv["ga"], sv["states"], do, nm("d_gdnb"))
    dc, dab, d_alog, d_dtb = gdn_a_bwd(sv["c"], sv["proj"], p["alog"], p["dtb"], cots[:5], cots[5], nm("d_gdna"))
    dgq, d_conv = conv_bwd(sv["proj"], p["conv"], dc, nm("d_conv"))
    dq, dk, dv = sb_bwd(sv["proj"], dys[1], nm("d_sb"))
    dp, d_pool_w, d_pool_scale = pool_bwd(sv["proj"], p["pool_w"], p["pool_scale"], dys[0], nm("d_pool"))
    dproj = jnp.concatenate([dgates, dq, dk, dv, dgq, dz, dp, dab, jnp.zeros((s, NP - OFF_AB - HD), bf16)], axis=1)
    g_in = matmul(sv["u"], dproj, name=nm("d_in_w"), ta=True, out_dtype=bf16)
    r_in = send([g_in], (0,), "in")
    du = matmul(dproj, p["w_in"], name=nm("d_in_x"), tb=True)
    dx, dxb, d_attn_norm = rms_bwd(sv["x"], du, dx1, p["attn_norm"], nm("d_rms1"))
    big = [r_in[0], r_mid[0], r_mid[1], r_mid[2], r_mid[3], r_ff[0], r_ff[1]]
    small = [d_attn_norm, d_pool_w, d_pool_scale, d_conv, d_alog, d_dtb, d_gdn_norm, d_mlp_norm]
    return dx, dxb, big, small


BIG_AXES = (0, 1, 1, 1, 0, 1, 0)
GATHER_ID, EXCHANGE_ID = 1, 2


def kernel(x, attn_norm, w_in, pool_w, pool_scale, gdn_conv, gdn_a_log, gdn_dt_bias, gdn_norm, w_pool_up, w_sb_up, w_gdn_up, w_out, mlp_norm, w_ff1, w_ff2, final_norm, loss_target, m_attn_norm, m_w_in, m_pool_w, m_pool_scale, m_gdn_conv, m_gdn_a_log, m_gdn_dt_bias, m_gdn_norm, m_w_pool_up, m_w_sb_up, m_w_gdn_up, m_w_out, m_mlp_norm, m_w_ff1, m_w_ff2, m_final_norm, v_attn_norm, v_w_in, v_pool_w, v_pool_scale, v_gdn_conv, v_gdn_a_log, v_gdn_dt_bias, v_gdn_norm, v_w_pool_up, v_w_sb_up, v_w_gdn_up, v_w_out, v_mlp_norm, v_w_ff1, v_w_ff2, v_final_norm):
    s = x.shape[1]
    me = _dev_index((lax.axis_index("x"), lax.axis_index("y"), lax.axis_index("c")))
    ncv = gdn_conv.shape[2]

    full = []
    for l in range(NL):
        shards = [_pack_cols(w_in[l]).astype(bf16), w_pool_up[l].astype(bf16), w_sb_up[l].astype(bf16),
                  w_gdn_up[l].astype(bf16), w_out[l].astype(bf16), w_ff1[l].astype(bf16), w_ff2[l].astype(bf16)]
        if l == 0:
            first = all_gather([shards[0], gdn_conv.reshape(NL * 4, ncv)], (0, 0), "gather_first", sequencer_id=GATHER_ID)
            rest = all_gather(shards[1:], BIG_AXES[1:], "gather_rest_l0", sequencer_id=GATHER_ID)
            full.append([first[0]] + rest)
            conv_full = first[1].reshape(NDEV, NL, 4, ncv).transpose(1, 2, 0, 3).reshape(NL, 4, NDEV * ncv)
        else:
            full.append(all_gather(shards, BIG_AXES, f"gather_weights_l{l}", sequencer_id=GATHER_ID))
    params = []
    for l in range(NL):
        p = dict(zip(("w_in", "w_pool_up", "w_sb_up", "w_gdn_up", "w_out", "w_ff1", "w_ff2"), full[l][:7]))
        p.update(attn_norm=attn_norm[l][None], mlp_norm=mlp_norm[l][None], pool_w=pool_w[l], pool_scale=pool_scale[l][None],
                 conv=conv_full[l], alog=_lanes(gdn_a_log[l]), dtb=_lanes(gdn_dt_bias[l]), gdn_norm=gdn_norm[l][None])
        params.append(p)

    h = x[0]
    saved = []
    for l in range(NL):
        h, sv = _layer_fwd(h, params[l], l)
        saved.append(sv)
    dh, dhb, loss_row, d_final = loss_head(h, loss_target[0], final_norm[None], "loss_head")
    recv, smalls = [None] * NL, [None] * NL
    for l in reversed(range(NL)):
        send = functools.partial(lambda gs, axes, tag, l: exchange(gs, axes, f"exchange_{tag}_l{l}", sequencer_id=EXCHANGE_ID), l=l)
        dh, dhb, recv[l], smalls[l] = _layer_bwd(dh, dhb, saved[l], params[l], l, send)
    recv = recv[0] + recv[1]

    small_rows = [_lanes8(t) for l in range(NL) for t in smalls[l]] + [_lanes8(d_final), _lanes8(loss_row)]
    packed = jnp.concatenate(small_rows, axis=0)
    parts = all_gather([packed], (0,), "gather_small")[0].reshape(NDEV, packed.shape[0], HD)

    def pack_small(tree):
        rows = []
        for l in range(NL):
            rows += [_lanes8(tree["attn_norm"][l]), _lanes8(tree["pool_w"][l]), _lanes8(tree["pool_scale"][l]),
                     jnp.zeros((4 * NDEV * ncv // HD, HD), f32), _lanes8(tree["gdn_a_log"][l]), _lanes8(tree["gdn_dt_bias"][l]),
                     _lanes8(tree["gdn_norm"][l]), _lanes8(tree["mlp_norm"][l])]
        rows += [_lanes8(tree["final_norm"]), jnp.zeros((8, HD), f32)]
        return jnp.concatenate(rows, axis=0)

    names = ("attn_norm", "pool_w", "pool_scale", "gdn_a_log", "gdn_dt_bias", "gdn_norm", "mlp_norm", "final_norm")
    w_small = pack_small(dict(zip(names, (attn_norm, pool_w, pool_scale, gdn_a_log, gdn_dt_bias, gdn_norm, mlp_norm, final_norm))))
    m_small = pack_small(dict(zip(names, (m_attn_norm, m_pool_w, m_pool_scale, m_gdn_a_log, m_gdn_dt_bias, m_gdn_norm, m_mlp_norm, m_final_norm))))
    v_small = pack_small(dict(zip(names, (v_attn_norm, v_pool_w, v_pool_scale, v_gdn_a_log, v_gdn_dt_bias, v_gdn_norm, v_mlp_norm, v_final_norm))))
    small_out = small_adamw(parts, w_small, m_small, v_small, "adamw_small")

    def unpack_small(buf):
        out, conv_g, r = {}, [], 0
        layer_items = (("attn_norm", (D,)), ("pool_w", (4, HD, HD)), ("pool_scale", (W_POOL,)), ("conv", (4, NDEV * ncv)),
                       ("gdn_a_log", (NH,)), ("gdn_dt_bias", (NH,)), ("gdn_norm", (HD,)), ("mlp_norm", (D,)))
        per_layer = {k: [] for k, _ in layer_items}
        for l in range(NL):
            for k, shape in layer_items:
                size = math.prod(shape)
                nrow = -(-size // (8 * HD)) * 8
                per_layer[k].append(buf[r:r + nrow].reshape(-1)[:size].reshape(shape))
                r += nrow
        for k, _ in layer_items:
            out[k] = jnp.stack(per_layer[k])
        out["final_norm"] = buf[r:r + D // HD].reshape(D)
        out["loss"] = buf[r + D // HD, 0]
        return out

    sm = [unpack_small(b) for b in small_out]
    loss = sm[0]["loss"]
    g_conv = lax.dynamic_slice_in_dim(sm[0]["conv"], me * ncv, ncv, axis=2)
    conv_out = adamw(gdn_conv, m_gdn_conv, v_gdn_conv, "adamw_conv", g=g_conv)

    big_out = {}
    big_names = ("w_in", "w_pool_up", "w_sb_up", "w_gdn_up", "w_out", "w_ff1", "w_ff2")
    big_w = dict(zip(big_names, ((w_in, m_w_in, v_w_in), (w_pool_up, m_w_pool_up, v_w_pool_up), (w_sb_up, m_w_sb_up, v_w_sb_up),
                                 (w_gdn_up, m_w_gdn_up, v_w_gdn_up), (w_out, m_w_out, v_w_out), (w_ff1, m_w_ff1, v_w_ff1),
                                 (w_ff2, m_w_ff2, v_w_ff2))))
    for t, k in enumerate(big_names):
        w, m, v = big_w[k]
        r0, r1 = recv[t], recv[7 + t]
        if k == "w_in":
            g = _unpack_cols(sum_partials(r0, r1, "sum_w_in"))
            big_out[k] = adamw(w, m, v, "adamw_" + k, g=g)
        else:
            big_out[k] = adamw(w, m, v, "adamw_" + k, r0=r0, r1=r1)

    def leaf(i, k):
        if k in big_out:
            return big_out[k][i]
        if k == "gdn_conv":
            return conv_out[i]
        return sm[i][k]

    order = ("attn_norm", "w_in", "pool_w", "pool_scale", "gdn_conv", "gdn_a_log", "gdn_dt_bias", "gdn_norm", "w_pool_up",
             "w_sb_up", "w_gdn_up", "w_out", "mlp_norm", "w_ff1", "w_ff2", "final_norm")
    return (loss, dh[None]) + tuple(leaf(i, k) for i in range(4) for k in order)
```

```python
import functools
import math

import jax
import jax.numpy as jnp
from jax import lax
from jax.experimental import pallas as pl
from jax.experimental.pallas import tpu as pltpu
from jax.experimental.pallas import tpu_sc as plsc

f32, bf16 = jnp.float32, jnp.bfloat16

D = 2048
NDEV = 8
NL = 2
HD = 128
NH = 6
WH = NH * HD
W_POOL = 512
EPS = 1e-6
N_IN = 12044
NP = 12288
OFF_G, OFF_SB, OFF_GQ, OFF_Z, OFF_P, OFF_AB = 0, 6144, 8448, 10752, 11520, 12032
POOL_WINDOWS = (2, 4, 8, 16)
CH = 128
TQ = 256
VMEM_LIMIT = 56 * 1024 * 1024
ADAM_LR, ADAM_B1, ADAM_B2, ADAM_EPS, ADAM_WD, ADAM_STEP = 0.001, 0.9, 0.999, 1e-08, 0.01, 10
MESH = pl.DeviceIdType.MESH


def _cp(sem=None):
    return pltpu.CompilerParams(dimension_semantics=sem, vmem_limit_bytes=VMEM_LIMIT)


def _sds(shape, dtype):
    return jax.ShapeDtypeStruct(tuple(shape), dtype)


def matmul(a, b, *, name, ta=False, tb=False, out_dtype=f32, tm=1024, tn=1024, tk=2048, epilogue=None, extras=()):
    m, k = (a.shape[1], a.shape[0]) if ta else a.shape
    n = b.shape[0] if tb else b.shape[1]
    assert k == (b.shape[1] if tb else b.shape[0]) and a.dtype == bf16 and b.dtype == bf16
    tm, tn, tk = min(tm, m), min(tn, n), min(tk, k)
    assert m % tm == 0 and n % tn == 0 and k % tk == 0, (m, n, k, tm, tn, tk)
    nk = k // tk
    a_spec = pl.BlockSpec((tk, tm), lambda i, j, q: (q, i)) if ta else pl.BlockSpec((tm, tk), lambda i, j, q: (i, q))
    b_spec = pl.BlockSpec((tn, tk), lambda i, j, q: (j, q)) if tb else pl.BlockSpec((tk, tn), lambda i, j, q: (q, j))
    e_specs = [pl.BlockSpec((tm, tn), lambda i, j, q: (i, j)) for _ in extras]
    dn = (((0 if ta else 1,), (1 if tb else 0,)), ((), ()))
    ne = len(extras)

    def body(a_ref, b_ref, *rest):
        e_refs, o_ref = rest[:ne], rest[ne]
        part = lax.dot_general(a_ref[...], b_ref[...], dn, preferred_element_type=f32)

        def finish(acc):
            if epilogue is not None:
                acc = epilogue(acc, *[e[...] for e in e_refs])
            o_ref[...] = acc.astype(out_dtype)

        if nk == 1:
            finish(part)
        else:
            acc_ref = rest[ne + 1]
            q = pl.program_id(2)

            @pl.when(q == 0)
            def _():
                acc_ref[...] = part

            @pl.when(jnp.logical_and(q > 0, q < nk - 1))
            def _():
                acc_ref[...] += part

            @pl.when(q == nk - 1)
            def _():
                finish(acc_ref[...] + part)

    return pl.pallas_call(
        body, name=name, grid=(m // tm, n // tn, nk),
        in_specs=[a_spec, b_spec] + e_specs, out_specs=pl.BlockSpec((tm, tn), lambda i, j, q: (i, j)),
        out_shape=_sds((m, n), out_dtype),
        scratch_shapes=[pltpu.VMEM((tm, tn), f32)] if nk > 1 else [],
        compiler_params=_cp(("parallel", "parallel", "arbitrary")),
    )(a, b, *extras)


def rowwise(name, fn, rows, params, outs, sums=(), tr=256):
    s = rows[0][0].shape[0]
    tr = min(tr, s)
    nin, nout = len(rows) + len(params), len(outs)
    in_specs = [pl.BlockSpec((tr, w), functools.partial(lambda i, c: (i, c), c=c)) for (_, w, c) in rows]
    in_specs += [pl.BlockSpec(p.shape, lambda i: (0, 0)) for p in params]
    out_specs = [pl.BlockSpec((tr, w), lambda i: (i, 0)) for (w, _) in outs]
    out_specs += [pl.BlockSpec(sh, lambda i: (0, 0)) for sh in sums]
    out_shape = [_sds((s, w), dt) for (w, dt) in outs] + [_sds(sh, f32) for sh in sums]

    def body(*refs):
        res = fn(*[r[...] for r in refs[:nin]])
        for r, v in zip(refs[nin:nin + nout], res[:nout]):
            r[...] = v.astype(r.dtype)
        i = pl.program_id(0)
        for r, v in zip(refs[nin + nout:], res[nout:]):
            @pl.when(i == 0)
            def _(r=r, v=v):
                r[...] = v

            @pl.when(i > 0)
            def _(r=r, v=v):
                r[...] += v

    res = pl.pallas_call(
        body, name=name, grid=(s // tr,), in_specs=in_specs, out_specs=out_specs, out_shape=out_shape,
        compiler_params=_cp(("arbitrary",)),
    )(*[r[0] for r in rows], *params)
    return res


def _rms(x, g):
    return x * lax.rsqrt(jnp.mean(x * x, axis=-1, keepdims=True) + EPS) * g


def rms_fwd(x, g, name):
    return rowwise(name, lambda xb, gb: (_rms(xb, gb),), [(x, D, 0)], [g], [(D, bf16)])[0]


def rms_bwd(x, du, dres, g, name):
    def fn(xb, dub, drb, gb):
        _, vjp = jax.vjp(_rms, xb, gb)
        dx, dg = vjp(dub.astype(f32))
        return drb + dx, drb + dx, dg

    return rowwise(name, fn, [(x, D, 0), (du, D, 0), (dres, D, 0)], [g], [(D, f32), (D, bf16)], [(1, D)])


def _merge(gates, up_p, up_s, up_g):
    sg = jax.nn.sigmoid(gates)
    return sg[:, :D] * up_p + sg[:, D:2 * D] * up_s + sg[:, 2 * D:] * up_g


def merge_fwd(proj, ups, name):
    return rowwise(name, lambda g, a, b, c: (_merge(g, a, b, c),),
                   [(proj, 3 * D, 0)] + [(u, D, 0) for u in ups], [], [(D, bf16)], tr=128)[0]


def merge_bwd(proj, ups, dmerged, name):
    def fn(g, a, b, c, dm):
        _, vjp = jax.vjp(_merge, g, a, b, c)
        return vjp(dm.astype(f32))

    return rowwise(name, fn, [(proj, 3 * D, 0)] + [(u, D, 0) for u in ups] + [(dmerged, D, 0)], [],
                   [(3 * D, bf16), (D, bf16), (D, bf16), (D, bf16)], tr=128)


def _gdn_out(o, z, g):
    ys = []
    for h in range(NH):
        sl = slice(h * HD, (h + 1) * HD)
        ys.append(_rms(o[:, sl], g) * jax.nn.silu(z[:, sl]))
    return jnp.concatenate(ys, axis=1)


def gdn_out_fwd(o, proj, g, name):
    return rowwise(name, lambda ob, zb, gb: (_gdn_out(ob, zb, gb),), [(o, WH, 0), (proj, WH, OFF_Z // WH)], [g],
                   [(WH, bf16)])[0]


def gdn_out_bwd(o, proj, g, dy, name):
    def fn(ob, zb, dyb, gb):
        _, vjp = jax.vjp(_gdn_out, ob, zb, gb)
        return vjp(dyb.astype(f32))

    return rowwise(name, fn, [(o, WH, 0), (proj, WH, OFF_Z // WH), (dy, WH, 0)], [g], [(WH, f32), (WH, bf16)],
                   [(1, HD)])


def loss_head(x, target, g, name):
    def loss_fn(xb, gb, tb):
        err = _rms(xb, gb) - tb
        return (0.5 / D) * jnp.sum(jnp.sum(err * err, axis=1, keepdims=True), axis=0, keepdims=True)

    def fn(xb, tb, gb):
        val, vjp = jax.vjp(functools.partial(loss_fn, tb=tb), xb, gb)
        dx, dg = vjp(jnp.ones((1, 1), f32))
        return dx, dx, jnp.broadcast_to(val, (1, HD)), dg

    return rowwise(name, fn, [(x, D, 0), (target, D, 0)], [g], [(D, f32), (D, bf16)], [(1, HD), (1, D)])


PB = 256


def _split(v):
    hi = v.astype(bf16)
    return hi, (v - hi.astype(f32)).astype(bf16)


def _band_dot(make_band, v, s, forward):
    hi, lo = _split(v)
    nb = s // PB
    outs = []
    for r in range(nb):
        lo_r = max(r - 1, 0) if forward else r
        hi_r = r + 1 if forward else min(r + 2, nb)
        band = make_band(r * PB, lo_r * PB, (hi_r - lo_r) * PB)
        sl = slice(lo_r * PB, hi_r * PB)
        outs.append(jnp.dot(band, hi[sl], preferred_element_type=f32) + jnp.dot(band, lo[sl], preferred_element_type=f32))
    return jnp.concatenate(outs, axis=0)


def _pool_common(p, win, s):
    def band(row0, col0, ncol):
        t = row0 + lax.broadcasted_iota(jnp.int32, (PB, ncol), 0)
        u = col0 + lax.broadcasted_iota(jnp.int32, (PB, ncol), 1)
        return jnp.logical_and(u <= t, t < u + win).astype(bf16)

    def band_t(row0, col0, ncol):
        u = row0 + lax.broadcasted_iota(jnp.int32, (PB, ncol), 0)
        t = col0 + lax.broadcasted_iota(jnp.int32, (PB, ncol), 1)
        return jnp.logical_and(u <= t, t < u + win).astype(bf16)

    t = lax.broadcasted_iota(jnp.int32, (s, 1), 0)
    inv_n = 1.0 / jnp.minimum(t + 1, win).astype(f32)
    d = _band_dot(band, p, s, True) * inv_n - p
    return d, inv_n, band_t


def pool_fwd(proj, pool_w, pool_scale, name):
    s = proj.shape[0]

    def body(p_ref, w_ref, sc_ref, y_ref):
        win = jnp.left_shift(2, pl.program_id(0))
        d, _, _ = _pool_common(p_ref[...], win, s)
        y = jnp.dot(d.astype(bf16), w_ref[...].astype(bf16), preferred_element_type=f32) * sc_ref[...]
        y_ref[...] = y.astype(bf16)

    return pl.pallas_call(
        body, name=name, grid=(4,),
        in_specs=[pl.BlockSpec((s, HD), lambda g: (0, OFF_P // HD + g)), pl.BlockSpec((None, HD, HD), lambda g: (g, 0, 0)),
                  pl.BlockSpec((1, HD), lambda g: (0, g))],
        out_specs=pl.BlockSpec((s, HD), lambda g: (0, g)), out_shape=_sds((s, W_POOL), bf16),
        compiler_params=_cp(("arbitrary",)),
    )(proj, pool_w, pool_scale)


def pool_bwd(proj, pool_w, pool_scale, dy, name):
    s = proj.shape[0]

    def body(p_ref, w_ref, sc_ref, dy_ref, dp_ref, dw_ref, dsc_ref):
        win = jnp.left_shift(2, pl.program_id(0))
        d, inv_n, band_t = _pool_common(p_ref[...], win, s)
        w = w_ref[...].astype(bf16)
        dyf = dy_ref[...].astype(f32)
        dsc_ref[...] = jnp.sum(dyf * jnp.dot(d.astype(bf16), w, preferred_element_type=f32), axis=0, keepdims=True)
        dys = (dyf * sc_ref[...]).astype(bf16)
        dd = lax.dot_general(dys, w, (((1,), (1,)), ((), ())), preferred_element_type=f32)
        dw_ref[...] = lax.dot_general(d.astype(bf16), dys, (((0,), (0,)), ((), ())), preferred_element_type=f32)
        dp_ref[...] = (_band_dot(band_t, dd * inv_n, s, False) - dd).astype(bf16)

    return pl.pallas_call(
        body, name=name, grid=(4,),
        in_specs=[pl.BlockSpec((s, HD), lambda g: (0, OFF_P // HD + g)), pl.BlockSpec((None, HD, HD), lambda g: (g, 0, 0)),
                  pl.BlockSpec((1, HD), lambda g: (0, g)), pl.BlockSpec((s, HD), lambda g: (0, g))],
        out_specs=[pl.BlockSpec((s, HD), lambda g: (0, g)), pl.BlockSpec((None, HD, HD), lambda g: (g, 0, 0)),
                   pl.BlockSpec((1, HD), lambda g: (0, g))],
        out_shape=[_sds((s, W_POOL), bf16), _sds((4, HD, HD), f32), _sds((1, W_POOL), f32)],
        compiler_params=_cp(("arbitrary",)),
    )(proj, pool_w, pool_scale, dy)


def _sb_tile(q, k_ref, kb, qi, carry):
    k = k_ref[pl.ds(pl.multiple_of(kb * TQ, TQ), TQ), :].astype(bf16)
    z = lax.dot_general(q, k, (((1,), (1,)), ((), ())), preferred_element_type=f32)
    row = qi * TQ + lax.broadcasted_iota(jnp.int32, (TQ, TQ), 0)
    col = kb * TQ + lax.broadcasted_iota(jnp.int32, (TQ, TQ), 1)
    mask = col < row
    ls = jnp.where(mask, jax.nn.log_sigmoid(-z), 0.0)
    j = lax.broadcasted_iota(jnp.int32, (TQ, TQ), 0)
    u = lax.broadcasted_iota(jnp.int32, (TQ, TQ), 1)
    later = (j > u).astype(bf16)
    hi, lo = _split(ls)
    lw = jnp.dot(hi, later, preferred_element_type=f32) + jnp.dot(lo, later, preferred_element_type=f32)
    a = jnp.where(mask, jnp.exp(ls + z + lw + carry), 0.0)
    return z, mask, ls, a


def sb_fwd(proj, name):
    s = proj.shape[0]
    nq = s // TQ
    scale = HD ** -0.5

    def body(q_ref, k_ref, v_ref, y_ref):
        qi = pl.program_id(1)
        q = (q_ref[...] * scale).astype(bf16)

        def step(j, c):
            acc, carry = c
            kb = qi - j
            _, _, ls, a = _sb_tile(q, k_ref, kb, qi, carry)
            v = v_ref[pl.ds(pl.multiple_of(kb * TQ, TQ), TQ), :].astype(bf16)
            acc = acc + jnp.dot(a.astype(bf16), v, preferred_element_type=f32)
            return acc, carry + jnp.sum(ls, axis=1, keepdims=True)

        acc, _ = lax.fori_loop(0, qi + 1, step, (jnp.zeros((TQ, HD), f32), jnp.zeros((TQ, 1), f32)))
        y_ref[...] = acc.astype(bf16)

    c0 = OFF_SB // HD
    return pl.pallas_call(
        body, name=name, grid=(NH, nq),
        in_specs=[pl.BlockSpec((TQ, HD), lambda h, i: (i, c0 + h)), pl.BlockSpec((s, HD), lambda h, i: (0, c0 + NH + h)),
                  pl.BlockSpec((s, HD), lambda h, i: (0, c0 + 2 * NH + h))],
        out_specs=pl.BlockSpec((TQ, HD), lambda h, i: (i, h)), out_shape=_sds((s, WH), bf16),
        compiler_params=_cp(("arbitrary", "arbitrary")),
    )(proj, proj, proj)


def sb_bwd(proj, dy, name):
    s = proj.shape[0]
    nq = s // TQ
    scale = HD ** -0.5

    def body(q_ref, k_ref, v_ref, do_ref, dq_ref, dk_ref, dv_ref, e_scr, z_scr, dk_acc, dv_acc):
        qi = pl.program_id(1)
        q = (q_ref[...] * scale).astype(bf16)
        do = do_ref[...]

        @pl.when(qi == 0)
        def _():
            dk_acc[...] = jnp.zeros_like(dk_acc)
            dv_acc[...] = jnp.zeros_like(dv_acc)

        def sweep_back(j, carry):
            kb = qi - j
            rows = pl.ds(pl.multiple_of(kb * TQ, TQ), TQ)
            z, _, ls, a = _sb_tile(q, k_ref, kb, qi, carry)
            v = v_ref[rows, :].astype(bf16)
            da = lax.dot_general(do, v, (((1,), (1,)), ((), ())), preferred_element_type=f32)
            e_scr[kb] = da * a
            z_scr[kb] = z
            dv_acc[rows, :] += lax.dot_general(a.astype(bf16), do, (((0,), (0,)), ((), ())), preferred_element_type=f32)
            return carry + jnp.sum(ls, axis=1, keepdims=True)

        lax.fori_loop(0, qi + 1, sweep_back, jnp.zeros((TQ, 1), f32))

        def sweep_fwd(kb, c):
            dq, carry = c
            rows = pl.ds(pl.multiple_of(kb * TQ, TQ), TQ)
            e, z = e_scr[kb], z_scr[kb]
            row = qi * TQ + lax.broadcasted_iota(jnp.int32, (TQ, TQ), 0)
            col = kb * TQ + lax.broadcasted_iota(jnp.int32, (TQ, TQ), 1)
            j = lax.broadcasted_iota(jnp.int32, (TQ, TQ), 0)
            u = lax.broadcasted_iota(jnp.int32, (TQ, TQ), 1)
            earlier = (j < u).astype(bf16)
            hi, lo = _split(e)
            ew = jnp.dot(hi, earlier, preferred_element_type=f32) + jnp.dot(lo, earlier, preferred_element_type=f32)
            sig = jax.nn.sigmoid(z)
            dz = jnp.where(col < row, e * (1.0 - sig) - (ew + carry) * sig, 0.0).astype(bf16)
            k = k_ref[rows, :].astype(bf16)
            dq = dq + jnp.dot(dz, k, preferred_element_type=f32)
            dk_acc[rows, :] += lax.dot_general(dz, q, (((0,), (0,)), ((), ())), preferred_element_type=f32)
            return dq, carry + jnp.sum(e, axis=1, keepdims=True)

        dq, _ = lax.fori_loop(0, qi + 1, sweep_fwd, (jnp.zeros((TQ, HD), f32), jnp.zeros((TQ, 1), f32)))
        dq_ref[...] = (dq * scale).astype(bf16)

        @pl.when(qi == nq - 1)
        def _():
            dk_ref[...] = dk_acc[...].astype(bf16)
            dv_ref[...] = dv_acc[...].astype(bf16)

    c0 = OFF_SB // HD
    return pl.pallas_call(
        body, name=name, grid=(NH, nq),
        in_specs=[pl.BlockSpec((TQ, HD), lambda h, i: (i, c0 + h)), pl.BlockSpec((s, HD), lambda h, i: (0, c0 + NH + h)),
                  pl.BlockSpec((s, HD), lambda h, i: (0, c0 + 2 * NH + h)), pl.BlockSpec((TQ, HD), lambda h, i: (i, h))],
        out_specs=[pl.BlockSpec((TQ, HD), lambda h, i: (i, h)), pl.BlockSpec((s, HD), lambda h, i: (0, h)),
                   pl.BlockSpec((s, HD), lambda h, i: (0, h))],
        out_shape=[_sds((s, WH), bf16)] * 3,
        scratch_shapes=[pltpu.VMEM((nq, TQ, TQ), f32), pltpu.VMEM((nq, TQ, TQ), f32), pltpu.VMEM((s, HD), f32),
                        pltpu.VMEM((s, HD), f32)],
        compiler_params=_cp(("arbitrary", "arbitrary")),
    )(proj, proj, proj, dy)


CB = 256


def _shift_down(v, k, s):
    if k == 0:
        return v
    row = lax.broadcasted_iota(jnp.int32, v.shape, 0)
    return jnp.where(row < k, 0.0, pltpu.roll(v, k, axis=0))


def _shift_up(v, k, s):
    if k == 0:
        return v
    row = lax.broadcasted_iota(jnp.int32, v.shape, 0)
    return jnp.where(row >= s - k, 0.0, pltpu.roll(v, s - k, axis=0))


def conv_fwd(proj, w, name):
    s = proj.shape[0]

    def body(x_ref, w_ref, y_ref):
        x, wv = x_ref[...], w_ref[...]
        y = sum(wv[3 - k:4 - k, :] * _shift_down(x, k, s) for k in range(4))
        y_ref[...] = jax.nn.silu(y)

    return pl.pallas_call(
        body, name=name, grid=(3 * WH // CB,),
        in_specs=[pl.BlockSpec((s, CB), lambda j: (0, OFF_GQ // CB + j)), pl.BlockSpec((4, CB), lambda j: (0, j))],
        out_specs=pl.BlockSpec((s, CB), lambda j: (0, j)), out_shape=_sds((s, 3 * WH), f32),
        compiler_params=_cp(("parallel",)),
    )(proj, w)


def conv_bwd(proj, w, dc, name):
    s = proj.shape[0]

    def body(x_ref, w_ref, dc_ref, dx_ref, dw_ref):
        x, wv = x_ref[...], w_ref[...]
        xs = [_shift_down(x, k, s) for k in range(4)]
        y = sum(wv[3 - k:4 - k, :] * xs[k] for k in range(4))
        sig = jax.nn.sigmoid(y)
        dy = dc_ref[...] * (sig * (1.0 + y * (1.0 - sig)))
        dx_ref[...] = sum(wv[3 - k:4 - k, :] * _shift_up(dy, k, s) for k in range(4)).astype(bf16)
        dw_ref[...] = jnp.concatenate([jnp.sum(dy * xs[3 - i], axis=0, keepdims=True) for i in range(4)], axis=0)

    return pl.pallas_call(
        body, name=name, grid=(3 * WH // CB,),
        in_specs=[pl.BlockSpec((s, CB), lambda j: (0, OFF_GQ // CB + j)), pl.BlockSpec((4, CB), lambda j: (0, j)),
                  pl.BlockSpec((s, CB), lambda j: (0, j))],
        out_specs=[pl.BlockSpec((s, CB), lambda j: (0, j)), pl.BlockSpec((4, CB), lambda j: (0, j))],
        out_shape=[_sds((s, 3 * WH), bf16), _sds((4, 3 * WH), f32)],
        compiler_params=_cp(("parallel",)),
    )(proj, w, dc)


SOLVE_PASSES = 3
NN, NT, TN = (((1,), (0,)), ((), ())), (((1,), (1,)), ((), ())), (((0,), (0,)), ((), ()))


def _pdot_impl(a, b, dn, passes):
    ah, al = _split(a)
    bh, bl = _split(b)
    dot = lambda p, q: lax.dot_general(p, q, dn, preferred_element_type=f32)
    if passes == 1:
        return dot(ah, bh)
    if passes == 2:
        return dot(ah, bh) + dot(ah, bl)
    return dot(ah, bh) + (dot(ah, bl) + dot(al, bh))


@functools.partial(jax.custom_vjp, nondiff_argnums=(2,))
def _pdot(a, b, passes):
    return _pdot_impl(a, b, NN, passes)


def _pdot_fwd(a, b, passes):
    return _pdot_impl(a, b, NN, passes), (a, b)


def _pdot_bwd(passes, res, ct):
    a, b = res
    return _pdot_impl(ct, b, NT, passes), _pdot_impl(a, ct, TN, passes)


_pdot.defvjp(_pdot_fwd, _pdot_bwd)


@functools.partial(jax.custom_vjp, nondiff_argnums=(2,))
def _pdot_nt(a, b, passes):
    return _pdot_impl(a, b, NT, passes)


def _pdot_nt_fwd(a, b, passes):
    return _pdot_impl(a, b, NT, passes), (a, b)


def _pdot_nt_bwd(passes, res, ct):
    a, b = res
    return _pdot_impl(ct, b, NN, passes), _pdot_impl(ct, a, TN, passes)


_pdot_nt.defvjp(_pdot_nt_fwd, _pdot_nt_bwd)


def _lane_pick(v, h):
    lane = lax.broadcasted_iota(jnp.int32, v.shape, v.ndim - 1)
    return jnp.sum(jnp.where(lane == h, v, 0.0), axis=-1, keepdims=True)


def _l2n(v):
    return v * lax.rsqrt(jnp.sum(v * v, axis=-1, keepdims=True) + EPS)


def _dot_nt(a, b):
    return lax.dot_general(a, b, (((1,), (1,)), ((), ())), preferred_element_type=f32)


def _gdn_chunk(cq, ck, cv, ab, alog, dtb, h):
    ones = jnp.ones((CH, HD), f32)
    q = _l2n(cq) * (HD ** -0.5)
    k = _l2n(ck)
    la = -jnp.exp(_lane_pick(alog, h)) * jax.nn.softplus(_lane_pick(ab, h) + _lane_pick(dtb, h))
    beta = jax.nn.sigmoid(_lane_pick(ab, NH + h)) * ones
    i = lax.broadcasted_iota(jnp.int32, (CH, CH), 0)
    j = lax.broadcasted_iota(jnp.int32, (CH, CH), 1)
    incl, strict = j <= i, j < i
    g = _pdot(incl.astype(f32), la * ones, 2)
    diff = g - g.T
    gamma = jnp.where(incl, jnp.exp(jnp.where(incl, diff, 0.0)), 0.0)
    lower = jnp.where(strict, beta * _pdot_nt(k, k, 1) * gamma, 0.0)
    eye = (i == j).astype(f32)
    inv = eye - lower
    pw = _pdot(lower, lower, SOLVE_PASSES)
    for m in range(1, int(math.log2(CH))):
        inv = inv + _pdot(inv, pw, SOLVE_PASSES)
        if m < int(math.log2(CH)) - 1:
            pw = _pdot(pw, pw, SOLVE_PASSES)
    eg = jnp.exp(g)
    u = _pdot(inv, cv * beta, SOLVE_PASSES)
    w = _pdot(inv, k * (beta * eg), SOLVE_PASSES)
    qk = _pdot_nt(q, k, 1) * gamma
    g_last = g[CH - 1:CH, :]
    return u, w, q * eg, k * jnp.exp(g_last - g), qk, jnp.exp(g_last)


def gdn_a_fwd(c, proj, alog, dtb, name):
    s = c.shape[0]
    nc = s // CH

    def body(c_ref, ab_ref, al_ref, dt_ref, u_ref, w_ref, qd_ref, kd_ref, qk_ref, dec_ref):
        for h in range(NH):
            sl = slice(h * HD, (h + 1) * HD)
            cq, ck, cv = (c_ref[:, t * WH + h * HD:t * WH + (h + 1) * HD] for t in range(3))
            res = _gdn_chunk(cq, ck, cv, ab_ref[...], al_ref[...], dt_ref[...], h)
            for r, v in zip((u_ref, w_ref, qd_ref, kd_ref, qk_ref), res[:5]):
                r[:, sl] = v
            dec_ref[:, sl] = jnp.broadcast_to(res[5], (8, HD))

    row = pl.BlockSpec((CH, WH), lambda n: (n, 0))
    par = pl.BlockSpec((1, HD), lambda n: (0, 0))
    return pl.pallas_call(
        body, name=name, grid=(nc,),
        in_specs=[pl.BlockSpec((CH, 3 * WH), lambda n: (n, 0)), pl.BlockSpec((CH, HD), lambda n: (n, OFF_AB // HD)), par, par],
        out_specs=[row] * 5 + [pl.BlockSpec((None, 8, WH), lambda n: (n, 0, 0))],
        out_shape=[_sds((s, WH), f32)] * 5 + [_sds((nc, 8, WH), f32)],
        compiler_params=_cp(("parallel",)),
    )(c, proj, alog, dtb)


def gdn_a_bwd(c, proj, alog, dtb, cots, ddec, name):
    s = c.shape[0]
    nc = s // CH

    def body(c_ref, ab_ref, al_ref, dt_ref, du_ref, dw_ref, dqd_ref, dkd_ref, dqk_ref, ddec_ref,
             dc_ref, dab_ref, dal_ref, ddt_ref):
        n = pl.program_id(0)
        dab, dal, ddt = jnp.zeros((CH, HD), f32), jnp.zeros((1, HD), f32), jnp.zeros((1, HD), f32)
        for h in range(NH):
            sl = slice(h * HD, (h + 1) * HD)
            cq, ck, cv = (c_ref[:, t * WH + h * HD:t * WH + (h + 1) * HD] for t in range(3))
            _, vjp = jax.vjp(functools.partial(_gdn_chunk, h=h), cq, ck, cv, ab_ref[...], al_ref[...], dt_ref[...])
            lane = lax.broadcasted_iota(jnp.int32, (1, HD), 1)
            dd = jnp.where(lane == 0, ddec_ref[0:1, sl], 0.0)
            dcs = vjp((du_ref[:, sl], dw_ref[:, sl], dqd_ref[:, sl], dkd_ref[:, sl], dqk_ref[:, sl], dd))
            for t in range(3):
                dc_ref[:, t * WH + h * HD:t * WH + (h + 1) * HD] = dcs[t]
            dab, dal, ddt = dab + dcs[3], dal + dcs[4], ddt + dcs[5]
        dab_ref[...] = dab.astype(bf16)

        @pl.when(n == 0)
        def _():
            dal_ref[...] = dal
            ddt_ref[...] = ddt

        @pl.when(n > 0)
        def _():
            dal_ref[...] += dal
            ddt_ref[...] += ddt

    row = pl.BlockSpec((CH, WH), lambda n: (n, 0))
    wide = pl.BlockSpec((CH, 3 * WH), lambda n: (n, 0))
    par = pl.BlockSpec((1, HD), lambda n: (0, 0))
    return pl.pallas_call(
        body, name=name, grid=(nc,),
        in_specs=[wide, pl.BlockSpec((CH, HD), lambda n: (n, OFF_AB // HD)), par, par] + [row] * 5
        + [pl.BlockSpec((None, 8, WH), lambda n: (n, 0, 0))],
        out_specs=[wide, pl.BlockSpec((CH, HD), lambda n: (n, 0)), par, par],
        out_shape=[_sds((s, 3 * WH), f32), _sds((s, HD), bf16), _sds((1, HD), f32), _sds((1, HD), f32)],
        compiler_params=_cp(("arbitrary",)),
    )(c, proj, alog, dtb, *cots, ddec)


def gdn_b_fwd(u, w, qd, kd, qk, dec, name):
    s = u.shape[0]
    nc = s // CH

    def body(u_ref, w_ref, qd_ref, kd_ref, qk_ref, dec_ref, o_ref, st_ref, state):
        n = pl.program_id(0)

        @pl.when(n == 0)
        def _():
            state[...] = jnp.zeros_like(state)

        for h in range(NH):
            sl = slice(h * HD, (h + 1) * HD)
            st = state[sl, :]
            st_ref[sl, :] = st
            sb = st.astype(bf16)
            vn = u_ref[:, sl] - jnp.dot(w_ref[:, sl].astype(bf16), sb, preferred_element_type=f32)
            vb = vn.astype(bf16)
            o_ref[:, sl] = (jnp.dot(qd_ref[:, sl].astype(bf16), sb, preferred_element_type=f32)
                            + jnp.dot(qk_ref[:, sl].astype(bf16), vb, preferred_element_type=f32))
            state[sl, :] = st * dec_ref[0:1, sl] + lax.dot_general(
                kd_ref[:, sl].astype(bf16), vb, (((0,), (0,)), ((), ())), preferred_element_type=f32)

    row = pl.BlockSpec((CH, WH), lambda n: (n, 0))
    return pl.pallas_call(
        body, name=name, grid=(nc,),
        in_specs=[row] * 5 + [pl.BlockSpec((None, 8, WH), lambda n: (n, 0, 0))],
        out_specs=[row, pl.BlockSpec((None, WH, HD), lambda n: (n, 0, 0))],
        out_shape=[_sds((s, WH), f32), _sds((nc, WH, HD), f32)],
        scratch_shapes=[pltpu.VMEM((WH, HD), f32)],
        compiler_params=_cp(("arbitrary",)),
    )(u, w, qd, kd, qk, dec)


def gdn_b_bwd(u, w, qd, kd, qk, dec, states, do, name):
    s = u.shape[0]
    nc = s // CH

    def body(u_ref, w_ref, qd_ref, kd_ref, qk_ref, dec_ref, st_ref, do_ref,
             du_ref, dw_ref, dqd_ref, dkd_ref, dqk_ref, ddec_ref, dstate):
        n = pl.program_id(0)

        @pl.when(n == 0)
        def _():
            dstate[...] = jnp.zeros_like(dstate)

        for h in range(NH):
            sl = slice(h * HD, (h + 1) * HD)
            st, ds = st_ref[sl, :], dstate[sl, :]
            sb, dsb = st.astype(bf16), ds.astype(bf16)
            wb, qdb, kdb, qkb = (r[:, sl].astype(bf16) for r in (w_ref, qd_ref, kd_ref, qk_ref))
            dob = do_ref[:, sl].astype(bf16)
            vn = u_ref[:, sl] - jnp.dot(wb, sb, preferred_element_type=f32)
            vb = vn.astype(bf16)
            dvn = (lax.dot_general(qkb, dob, (((0,), (0,)), ((), ())), preferred_element_type=f32)
                   + jnp.dot(kdb, dsb, preferred_element_type=f32))
            dvb = dvn.astype(bf16)
            du_ref[:, sl] = dvn
            dw_ref[:, sl] = -_dot_nt(dvb, sb)
            dqd_ref[:, sl] = _dot_nt(dob, sb)
            dkd_ref[:, sl] = _dot_nt(vb, dsb)
            dqk_ref[:, sl] = _dot_nt(dob, vb)
            tot = jnp.sum(jnp.sum(ds * st, axis=1, keepdims=True), axis=0, keepdims=True)
            ddec_ref[:, sl] = jnp.broadcast_to(tot, (8, HD))
            dstate[sl, :] = (ds * dec_ref[0:1, sl]
                             + lax.dot_general(qdb, dob, (((0,), (0,)), ((), ())), preferred_element_type=f32)
                             - lax.dot_general(wb, dvb, (((0,), (0,)), ((), ())), preferred_element_type=f32))

    row = pl.BlockSpec((CH, WH), lambda n: (nc - 1 - n, 0))
    small = pl.BlockSpec((None, 8, WH), lambda n: (nc - 1 - n, 0, 0))
    return pl.pallas_call(
        body, name=name, grid=(nc,),
        in_specs=[row] * 5 + [small, pl.BlockSpec((None, WH, HD), lambda n: (nc - 1 - n, 0, 0)), row],
        out_specs=[row] * 5 + [small],
        out_shape=[_sds((s, WH), f32)] * 5 + [_sds((nc, 8, WH), f32)],
        scratch_shapes=[pltpu.VMEM((WH, HD), f32)],
        compiler_params=_cp(("arbitrary",)),
    )(u, w, qd, kd, qk, dec, states, do)


ANY = pl.BlockSpec(memory_space=pl.ANY)


def _dev_index(p):
    return 4 * p[0] + 2 * p[1] + p[2]


def _shard_of(ref, axis, size, idx):
    return ref.at[pl.ds(idx * size, size), :] if axis == 0 else ref.at[:, pl.ds(idx * size, size)]


def _launch(body, ins, out_sds, name, sequencer_id):
    n = len(ins)
    sems = [pltpu.SemaphoreType.DMA((n, 7)), pltpu.SemaphoreType.DMA((n, 7)), pltpu.SemaphoreType.DMA((n,))]
    if sequencer_id is None:
        return pl.pallas_call(
            lambda *refs: body(refs[:n], refs[n:2 * n], *refs[2 * n:]), name=name, in_specs=[ANY] * n, out_specs=[ANY] * n,
            out_shape=out_sds, scratch_shapes=sems, compiler_params=pltpu.CompilerParams(has_side_effects=True),
        )(*ins)
    in_refs = [jax.new_ref(a, memory_space=pltpu.MemorySpace.HBM) for a in ins]
    out_refs = [jax.empty_ref(sd, memory_space=pltpu.MemorySpace.HBM) for sd in out_sds]

    @pl.kernel(mesh=plsc.ScalarSubcoreMesh(axis_name="sequencer", num_cores=1), name=name, scratch_types=sems,
               compiler_params=pltpu.CompilerParams(collective_id=sequencer_id))
    def launch(send, recv, loc):
        x, y, c = lax.axis_index("x"), lax.axis_index("y"), lax.axis_index("c")
        barrier = pltpu.get_barrier_semaphore()
        for r in range(1, NDEV):
            peer = (1 - x if r & 4 else x, 1 - y if r & 2 else y, 1 - c if r & 1 else c)
            pl.semaphore_signal(barrier, inc=1, device_id=peer, device_id_type=MESH)
        pl.semaphore_wait(barrier, NDEV - 1)
        body(in_refs, out_refs, send, recv, loc)

    launch()
    return [r[...] for r in out_refs]


def all_gather(xs, axes, name, sequencer_id=None):
    n = len(xs)
    fulls = [tuple(d * (NDEV if a == ax else 1) for a, d in enumerate(x.shape)) for x, ax in zip(xs, axes)]

    def body(x_refs, o_refs, send, recv, loc):
        x, y, c = lax.axis_index("x"), lax.axis_index("y"), lax.axis_index("c")
        me, sib = (x, y, c), (x, y, 1 - c)
        chips = [(1 - x, y), (x, 1 - y), (1 - x, 1 - y)]

        def part(t, p):
            return _shard_of(o_refs[t], axes[t], xs[t].shape[axes[t]], _dev_index(p))

        def copy(t, k, block, to, src=None):
            return pltpu.make_async_remote_copy(
                src_ref=part(t, block) if src is None else src, dst_ref=part(t, block),
                send_sem=send.at[t, k], recv_sem=recv.at[t, k], device_id=to, device_id_type=MESH)

        mine = [pltpu.make_async_copy(x_refs[t], part(t, me), loc.at[t]) for t in range(n)]
        for cp in mine:
            cp.start()
        first = []
        for t in range(n):
            first.append(copy(t, 0, me, sib, src=x_refs[t]))
            first += [copy(t, 1 + j, me, (*chip, c), src=x_refs[t]) for j, chip in enumerate(chips)]
        for cp in first:
            cp.start()
        passed = []
        for t in range(n):
            for j, chip in enumerate(chips):
                copy(t, 1 + j, (*chip, c), me).wait_recv()
                passed.append(copy(t, 4 + j, (*chip, c), sib))
                passed[-1].start()
        for t in range(n):
            copy(t, 0, sib, me).wait_recv()
            for j, chip in enumerate(chips):
                copy(t, 4 + j, (*chip, 1 - c), me).wait_recv()
        for cp in first + passed:
            cp.wait_send()
        for cp in mine:
            cp.wait()

    return _launch(body, xs, [_sds(f, x.dtype) for f, x in zip(fulls, xs)], name, sequencer_id)


def exchange(gs, axes, name, sequencer_id=None):
    n = len(gs)
    shards = [tuple(d // (NDEV if a == ax else 1) for a, d in enumerate(g.shape)) for g, ax in zip(gs, axes)]

    def body(g_refs, r_refs, send, recv, loc):
        x, y, c = lax.axis_index("x"), lax.axis_index("y"), lax.axis_index("c")
        me = _dev_index((x, y, c))
        peers = []
        for r in range(1, NDEV):
            peers.append((1 - x if r & 4 else x, 1 - y if r & 2 else y, 1 - c if r & 1 else c))

        def src(t, idx):
            return _shard_of(g_refs[t], axes[t], shards[t][axes[t]], idx)

        mine = [pltpu.make_async_copy(src(t, me), r_refs[t].at[me], loc.at[t]) for t in range(n)]
        for cp in mine:
            cp.start()
        sends = []
        for t in range(n):
            for k, p in enumerate(peers):
                sends.append(pltpu.make_async_remote_copy(
                    src_ref=src(t, _dev_index(p)), dst_ref=r_refs[t].at[me], send_sem=send.at[t, k], recv_sem=recv.at[t, k],
                    device_id=p, device_id_type=MESH))
                sends[-1].start()
        for t in range(n):
            for k, p in enumerate(peers):
                pltpu.make_async_remote_copy(
                    src_ref=src(t, me), dst_ref=r_refs[t].at[_dev_index(p)], send_sem=send.at[t, k], recv_sem=recv.at[t, k],
                    device_id=p, device_id_type=MESH).wait_recv()
        for cp in sends:
            cp.wait_send()
        for cp in mine:
            cp.wait()

    return _launch(body, gs, [_sds((NDEV,) + sh, g.dtype) for sh, g in zip(shards, gs)], name, sequencer_id)


def _adamw(w, g, m, v):
    m = ADAM_B1 * m + (1.0 - ADAM_B1) * g
    v = ADAM_B2 * v + (1.0 - ADAM_B2) * jnp.square(g)
    m_hat = m / (1.0 - ADAM_B1 ** ADAM_STEP)
    v_hat = v / (1.0 - ADAM_B2 ** ADAM_STEP)
    return -ADAM_LR * (m_hat / (jnp.sqrt(v_hat) + ADAM_EPS) + ADAM_WD * w), m, v


def _sum8(r_ref):
    g = r_ref[0].astype(f32)
    for j in range(1, NDEV):
        g = g + r_ref[j].astype(f32)
    return g


def _row_tile(rows, cols):
    tr = min(rows, max(8, 1 << int(math.log2((1 << 18) / cols))))
    assert rows % tr == 0, (rows, cols)
    return tr


def sum_partials(r, name):
    _, rows, cols = r.shape
    tr = _row_tile(rows, cols)

    def body(r_ref, g_ref):
        g_ref[...] = _sum8(r_ref)

    return pl.pallas_call(
        body, name=name, grid=(rows // tr,), in_specs=[pl.BlockSpec((NDEV, tr, cols), lambda i: (0, i, 0))],
        out_specs=pl.BlockSpec((tr, cols), lambda i: (i, 0)), out_shape=_sds((rows, cols), f32),
        compiler_params=_cp(("parallel",)),
    )(r)


def adamw(w, m, v, layer, name, r=None, g=None, prev=None):
    _, rows, cols = w.shape
    tr = _row_tile(rows, cols)
    blk = pl.BlockSpec((None, tr, cols), lambda i: (layer, i, 0))
    nprev = 0 if prev is None else 4

    def body(w_ref, m_ref, v_ref, src_ref, *rest):
        g_ref, d_ref, nm_ref, nv_ref = rest[nprev:]
        grad = _sum8(src_ref) if g is None else src_ref[...]
        g_ref[...] = grad
        d_ref[...], nm_ref[...], nv_ref[...] = _adamw(w_ref[...], grad, m_ref[...], v_ref[...])

    src, src_spec = (r, pl.BlockSpec((NDEV, tr, cols), lambda i: (0, i, 0))) if g is None else (g, pl.BlockSpec((tr, cols), lambda i: (i, 0)))
    return tuple(pl.pallas_call(
        body, name=name, grid=(rows // tr,), in_specs=[blk] * 3 + [src_spec] + [ANY] * nprev, out_specs=[blk] * 4,
        out_shape=[_sds(w.shape, f32)] * 4, input_output_aliases={4 + k: k for k in range(nprev)},
        compiler_params=_cp(("parallel",)),
    )(w, m, v, src, *(prev or ())))


def small_adamw(parts, w, m, v, name):
    def body(p_ref, w_ref, m_ref, v_ref, g_ref, d_ref, nm_ref, nv_ref):
        g = _sum8(p_ref)
        g_ref[...] = g
        d_ref[...], nm_ref[...], nv_ref[...] = _adamw(w_ref[...], g, m_ref[...], v_ref[...])

    return pl.pallas_call(body, name=name, out_shape=[_sds(w.shape, f32)] * 4, compiler_params=_cp())(parts, w, m, v)


def _pack_cols(w):
    pad = jnp.zeros((w.shape[0], NP - OFF_AB - 12), w.dtype)
    return jnp.concatenate([w[:, 5900:12044], w[:, 512:2816], w[:, 2816:5120], w[:, 5120:5888], w[:, 0:512],
                            w[:, 5888:5900], pad], axis=1)


def _unpack_cols(g):
    return jnp.concatenate([g[..., OFF_P:OFF_P + 512], g[..., OFF_SB:OFF_SB + 2304], g[..., OFF_GQ:OFF_GQ + 2304],
                            g[..., OFF_Z:OFF_Z + 768], g[..., OFF_AB:OFF_AB + 12], g[..., OFF_G:OFF_G + 6144]], axis=-1)


def _lanes(v):
    flat = v.reshape(-1)
    n = -(-flat.shape[0] // HD) * HD
    return jnp.pad(flat, (0, n - flat.shape[0])).reshape(n // HD, HD)


def _lanes8(v):
    rows = _lanes(v)
    return jnp.pad(rows, ((0, -rows.shape[0] % 8), (0, 0)))


def _layer_fwd(x, p, l):
    nm = lambda s: f"{s}_l{l}"
    u = rms_fwd(x, p["attn_norm"], nm("rms1"))
    proj = matmul(u, p["w_in"], name=nm("inproj"))
    y_pool = pool_fwd(proj, p["pool_w"], p["pool_scale"], nm("pool"))
    y_sb = sb_fwd(proj, nm("sb"))
    c = conv_fwd(proj, p["conv"], nm("conv"))
    ga = gdn_a_fwd(c, proj, p["alog"], p["dtb"], nm("gdna"))
    o, states = gdn_b_fwd(*ga, nm("gdnb"))
    y_gdn = gdn_out_fwd(o, proj, p["gdn_norm"], nm("gdno"))
    ups = [matmul(y, p[k], name=nm(k)) for y, k in ((y_pool, "w_pool_up"), (y_sb, "w_sb_up"), (y_gdn, "w_gdn_up"))]
    merged = merge_fwd(proj, ups, nm("merge"))
    x1 = matmul(merged, p["w_out"], name=nm("outproj"), epilogue=lambda acc, r: acc + r, extras=(x,))
    u2 = rms_fwd(x1, p["mlp_norm"], nm("rms2"))
    h2 = matmul(u2, p["w_ff1"], name=nm("ff1"), out_dtype=bf16, epilogue=lambda acc: jnp.square(jnp.maximum(acc, 0.0)))
    x2 = matmul(h2, p["w_ff2"], name=nm("ff2"), epilogue=lambda acc, r: acc + r, extras=(x1,))
    saved = dict(x=x, u=u, proj=proj, y_pool=y_pool, y_sb=y_sb, c=c, ga=ga, o=o, states=states, y_gdn=y_gdn, ups=ups,
                 merged=merged, x1=x1, u2=u2, h2=h2)
    return x2, saved


def _layer_bwd(dx2, dx2b, sv, p, l, send):
    nm = lambda s: f"{s}_l{l}"
    s = dx2.shape[0]
    dh = matmul(dx2b, p["w_ff2"], name=nm("d_ff2_x"), tb=True, out_dtype=bf16,
                epilogue=lambda acc, h2: acc * (2.0 * jnp.sqrt(h2.astype(f32))), extras=(sv["h2"],))
    g_ff2 = matmul(sv["h2"], dx2b, name=nm("d_ff2_w"), ta=True, out_dtype=bf16)
    du2 = matmul(dh, p["w_ff1"], name=nm("d_ff1_x"), tb=True)
    g_ff1 = matmul(sv["u2"], dh, name=nm("d_ff1_w"), ta=True, out_dtype=bf16)
    dx1, dx1b, d_mlp_norm = rms_bwd(sv["x1"], du2, dx2, p["mlp_norm"], nm("d_rms2"))
    dmerged = matmul(dx1b, p["w_out"], name=nm("d_out_x"), tb=True, out_dtype=bf16)
    g_out = matmul(sv["merged"], dx1b, name=nm("d_out_w"), ta=True, out_dtype=bf16)
    dgates, *dups = merge_bwd(sv["proj"], sv["ups"], dmerged, nm("d_merge"))
    dys, g_ups = [], []
    for dup, y, k in zip(dups, (sv["y_pool"], sv["y_sb"], sv["y_gdn"]), ("w_pool_up", "w_sb_up", "w_gdn_up")):
        dys.append(matmul(dup, p[k], name=nm("d_" + k + "_x"), tb=True, out_dtype=bf16))
        g_ups.append(matmul(y, dup, name=nm("d_" + k + "_w"), ta=True, out_dtype=bf16))
    do, dz, d_gdn_norm = gdn_out_bwd(sv["o"], sv["proj"], p["gdn_norm"], dys[2], nm("d_gdno"))
    cots = gdn_b_bwd(*sv["ga"], sv["states"], do, nm("d_gdnb"))
    dc, dab, d_alog, d_dtb = gdn_a_bwd(sv["c"], sv["proj"], p["alog"], p["dtb"], cots[:5], cots[5], nm("d_gdna"))
    dgq, d_conv = conv_bwd(sv["proj"], p["conv"], dc, nm("d_conv"))
    dq, dk, dv = sb_bwd(sv["proj"], dys[1], nm("d_sb"))
    dp, d_pool_w, d_pool_scale = pool_bwd(sv["proj"], p["pool_w"], p["pool_scale"], dys[0], nm("d_pool"))
    dproj = jnp.concatenate([dgates, dq, dk, dv, dgq, dz, dp, dab, jnp.zeros((s, NP - OFF_AB - HD), bf16)], axis=1)
    g_in = matmul(sv["u"], dproj, name=nm("d_in_w"), ta=True, out_dtype=bf16)
    recv = send([g_in, g_ups[0], g_ups[1], g_ups[2], g_out, g_ff1, g_ff2])
    du = matmul(dproj, p["w_in"], name=nm("d_in_x"), tb=True)
    dx, dxb, d_attn_norm = rms_bwd(sv["x"], du, dx1, p["attn_norm"], nm("d_rms1"))
    small = [d_attn_norm, d_pool_w, d_pool_scale, d_conv, d_alog, d_dtb, d_gdn_norm, d_mlp_norm]
    return dx, dxb, recv, small


BIG_AXES = (0, 1, 1, 1, 0, 1, 0)
GATHER_ID, EXCHANGE_ID = 1, 2


def kernel(x, attn_norm, w_in, pool_w, pool_scale, gdn_conv, gdn_a_log, gdn_dt_bias, gdn_norm, w_pool_up, w_sb_up, w_gdn_up, w_out, mlp_norm, w_ff1, w_ff2, final_norm, loss_target, m_attn_norm, m_w_in, m_pool_w, m_pool_scale, m_gdn_conv, m_gdn_a_log, m_gdn_dt_bias, m_gdn_norm, m_w_pool_up, m_w_sb_up, m_w_gdn_up, m_w_out, m_mlp_norm, m_w_ff1, m_w_ff2, m_final_norm, v_attn_norm, v_w_in, v_pool_w, v_pool_scale, v_gdn_conv, v_gdn_a_log, v_gdn_dt_bias, v_gdn_norm, v_w_pool_up, v_w_sb_up, v_w_gdn_up, v_w_out, v_mlp_norm, v_w_ff1, v_w_ff2, v_final_norm):
    s = x.shape[1]
    me = _dev_index((lax.axis_index("x"), lax.axis_index("y"), lax.axis_index("c")))
    ncv = gdn_conv.shape[2]

    full = []
    for l in range(NL):
        shards = [_pack_cols(w_in[l]).astype(bf16), w_pool_up[l].astype(bf16), w_sb_up[l].astype(bf16),
                  w_gdn_up[l].astype(bf16), w_out[l].astype(bf16), w_ff1[l].astype(bf16), w_ff2[l].astype(bf16)]
        if l == 0:
            first = all_gather([shards[0], gdn_conv.reshape(NL * 4, ncv)], (0, 0), "gather_first", sequencer_id=GATHER_ID)
            rest = all_gather(shards[1:], BIG_AXES[1:], "gather_rest_l0", sequencer_id=GATHER_ID)
            full.append([first[0]] + rest)
            conv_full = first[1].reshape(NDEV, NL, 4, ncv).transpose(1, 2, 0, 3).reshape(NL, 4, NDEV * ncv)
        else:
            full.append(all_gather(shards, BIG_AXES, f"gather_weights_l{l}", sequencer_id=GATHER_ID))
    params = []
    for l in range(NL):
        p = dict(zip(("w_in", "w_pool_up", "w_sb_up", "w_gdn_up", "w_out", "w_ff1", "w_ff2"), full[l][:7]))
        p.update(attn_norm=attn_norm[l][None], mlp_norm=mlp_norm[l][None], pool_w=pool_w[l], pool_scale=pool_scale[l][None],
                 conv=conv_full[l], alog=_lanes(gdn_a_log[l]), dtb=_lanes(gdn_dt_bias[l]), gdn_norm=gdn_norm[l][None])
        params.append(p)

    h = x[0]
    saved = []
    for l in range(NL):
        h, sv = _layer_fwd(h, params[l], l)
        saved.append(sv)
    dh, dhb, loss_row, d_final = loss_head(h, loss_target[0], final_norm[None], "loss_head")
    recv, smalls = [None] * NL, [None] * NL
    for l in reversed(range(NL)):
        send = functools.partial(lambda gs, l: exchange(gs, BIG_AXES, f"exchange_grads_l{l}", sequencer_id=EXCHANGE_ID), l=l)
        dh, dhb, recv[l], smalls[l] = _layer_bwd(dh, dhb, saved[l], params[l], l, send)

    small_rows = [_lanes8(t) for l in range(NL) for t in smalls[l]] + [_lanes8(d_final), _lanes8(loss_row)]
    packed = jnp.concatenate(small_rows, axis=0)
    parts = all_gather([packed], (0,), "gather_small")[0].reshape(NDEV, packed.shape[0], HD)

    def pack_small(tree):
        rows = []
        for l in range(NL):
            rows += [_lanes8(tree["attn_norm"][l]), _lanes8(tree["pool_w"][l]), _lanes8(tree["pool_scale"][l]),
                     jnp.zeros((4 * NDEV * ncv // HD, HD), f32), _lanes8(tree["gdn_a_log"][l]), _lanes8(tree["gdn_dt_bias"][l]),
                     _lanes8(tree["gdn_norm"][l]), _lanes8(tree["mlp_norm"][l])]
        rows += [_lanes8(tree["final_norm"]), jnp.zeros((8, HD), f32)]
        return jnp.concatenate(rows, axis=0)

    names = ("attn_norm", "pool_w", "pool_scale", "gdn_a_log", "gdn_dt_bias", "gdn_norm", "mlp_norm", "final_norm")
    w_small = pack_small(dict(zip(names, (attn_norm, pool_w, pool_scale, gdn_a_log, gdn_dt_bias, gdn_norm, mlp_norm, final_norm))))
    m_small = pack_small(dict(zip(names, (m_attn_norm, m_pool_w, m_pool_scale, m_gdn_a_log, m_gdn_dt_bias, m_gdn_norm, m_mlp_norm, m_final_norm))))
    v_small = pack_small(dict(zip(names, (v_attn_norm, v_pool_w, v_pool_scale, v_gdn_a_log, v_gdn_dt_bias, v_gdn_norm, v_mlp_norm, v_final_norm))))
    small_out = small_adamw(parts, w_small, m_small, v_small, "adamw_small")

    def unpack_small(buf):
        out, conv_g, r = {}, [], 0
        layer_items = (("attn_norm", (D,)), ("pool_w", (4, HD, HD)), ("pool_scale", (W_POOL,)), ("conv", (4, NDEV * ncv)),
                       ("gdn_a_log", (NH,)), ("gdn_dt_bias", (NH,)), ("gdn_norm", (HD,)), ("mlp_norm", (D,)))
        per_layer = {k: [] for k, _ in layer_items}
        for l in range(NL):
            for k, shape in layer_items:
                size = math.prod(shape)
                nrow = -(-size // (8 * HD)) * 8
                per_layer[k].append(buf[r:r + nrow].reshape(-1)[:size].reshape(shape))
                r += nrow
        for k, _ in layer_items:
            out[k] = jnp.stack(per_layer[k])
        out["final_norm"] = buf[r:r + D // HD].reshape(D)
        out["loss"] = buf[r + D // HD, 0]
        return out

    sm = [unpack_small(b) for b in small_out]
    loss = sm[0]["loss"]
    g_conv = lax.dynamic_slice_in_dim(sm[0]["conv"], me * ncv, ncv, axis=2)
    conv_out = None
    for l in reversed(range(NL)):
        conv_out = adamw(gdn_conv, m_gdn_conv, v_gdn_conv, l, f"adamw_conv_l{l}", g=g_conv[l], prev=conv_out)

    big_out = {}
    big_names = ("w_in", "w_pool_up", "w_sb_up", "w_gdn_up", "w_out", "w_ff1", "w_ff2")
    big_w = dict(zip(big_names, ((w_in, m_w_in, v_w_in), (w_pool_up, m_w_pool_up, v_w_pool_up), (w_sb_up, m_w_sb_up, v_w_sb_up),
                                 (w_gdn_up, m_w_gdn_up, v_w_gdn_up), (w_out, m_w_out, v_w_out), (w_ff1, m_w_ff1, v_w_ff1),
                                 (w_ff2, m_w_ff2, v_w_ff2))))
    for l in reversed(range(NL)):
        for t, k in enumerate(big_names):
            w, m, v = big_w[k]
            if k == "w_in":
                g = _unpack_cols(sum_partials(recv[l][t], f"sum_w_in_l{l}"))
                big_out[k] = adamw(w, m, v, l, f"adamw_{k}_l{l}", g=g, prev=big_out.get(k))
            else:
                big_out[k] = adamw(w, m, v, l, f"adamw_{k}_l{l}", r=recv[l][t], prev=big_out.get(k))

    def leaf(i, k):
        if k in big_out:
            return big_out[k][i]
        if k == "gdn_conv":
            return conv_out[i]
        return sm[i][k]

    order = ("attn_norm", "w_in", "pool_w", "pool_scale", "gdn_conv", "gdn_a_log", "gdn_dt_bias", "gdn_norm", "w_pool_up",
             "w_sb_up", "w_gdn_up", "w_out", "mlp_norm", "w_ff1", "w_ff2", "final_norm")
    return (loss, dh[None]) + tuple(leaf(i, k) for i in range(4) for k in order)
```

```python
import functools
import math

import jax
import jax.numpy as jnp
from jax import lax
from jax.experimental import pallas as pl
from jax.experimental.pallas import tpu as pltpu
from jax.experimental.pallas import tpu_sc as plsc

f32, bf16 = jnp.float32, jnp.bfloat16

D = 2048
NDEV = 8
NL = 2
HD = 128
NH = 6
WH = NH * HD
W_POOL = 512
EPS = 1e-6
N_IN = 12044
NP = 12288
OFF_G, OFF_SB, OFF_GQ, OFF_Z, OFF_P, OFF_AB = 0, 6144, 8448, 10752, 11520, 12032
POOL_WINDOWS = (2, 4, 8, 16)
CH = 128
TQ = 256
VMEM_LIMIT = 56 * 1024 * 1024
ADAM_LR, ADAM_B1, ADAM_B2, ADAM_EPS, ADAM_WD, ADAM_STEP = 0.001, 0.9, 0.999, 1e-08, 0.01, 10
MESH = pl.DeviceIdType.MESH


def _cp(sem=None):
    return pltpu.CompilerParams(dimension_semantics=sem, vmem_limit_bytes=VMEM_LIMIT)


def _sds(shape, dtype):
    return jax.ShapeDtypeStruct(tuple(shape), dtype)


def matmul(a, b, *, name, ta=False, tb=False, out_dtype=f32, tm=1024, tn=1024, tk=2048, epilogue=None, extras=()):
    m, k = (a.shape[1], a.shape[0]) if ta else a.shape
    n = b.shape[0] if tb else b.shape[1]
    assert k == (b.shape[1] if tb else b.shape[0]) and a.dtype == bf16 and b.dtype == bf16
    tm, tn, tk = min(tm, m), min(tn, n), min(tk, k)
    assert m % tm == 0 and n % tn == 0 and k % tk == 0, (m, n, k, tm, tn, tk)
    nk = k // tk
    a_spec = pl.BlockSpec((tk, tm), lambda i, j, q: (q, i)) if ta else pl.BlockSpec((tm, tk), lambda i, j, q: (i, q))
    b_spec = pl.BlockSpec((tn, tk), lambda i, j, q: (j, q)) if tb else pl.BlockSpec((tk, tn), lambda i, j, q: (q, j))
    e_specs = [pl.BlockSpec((tm, tn), lambda i, j, q: (i, j)) for _ in extras]
    dn = (((0 if ta else 1,), (1 if tb else 0,)), ((), ()))
    ne = len(extras)

    def body(a_ref, b_ref, *rest):
        e_refs, o_ref = rest[:ne], rest[ne]
        part = lax.dot_general(a_ref[...], b_ref[...], dn, preferred_element_type=f32)

        def finish(acc):
            if epilogue is not None:
                acc = epilogue(acc, *[e[...] for e in e_refs])
            o_ref[...] = acc.astype(out_dtype)

        if nk == 1:
            finish(part)
        else:
            acc_ref = rest[ne + 1]
            q = pl.program_id(2)

            @pl.when(q == 0)
            def _():
                acc_ref[...] = part

            @pl.when(jnp.logical_and(q > 0, q < nk - 1))
            def _():
                acc_ref[...] += part

            @pl.when(q == nk - 1)
            def _():
                finish(acc_ref[...] + part)

    return pl.pallas_call(
        body, name=name, grid=(m // tm, n // tn, nk),
        in_specs=[a_spec, b_spec] + e_specs, out_specs=pl.BlockSpec((tm, tn), lambda i, j, q: (i, j)),
        out_shape=_sds((m, n), out_dtype),
        scratch_shapes=[pltpu.VMEM((tm, tn), f32)] if nk > 1 else [],
        compiler_params=_cp(("parallel", "parallel", "arbitrary")),
    )(a, b, *extras)


def rowwise(name, fn, rows, params, outs, sums=(), tr=256):
    s = rows[0][0].shape[0]
    tr = min(tr, s)
    nin, nout = len(rows) + len(params), len(outs)
    in_specs = [pl.BlockSpec((tr, w), functools.partial(lambda i, c: (i, c), c=c)) for (_, w, c) in rows]
    in_specs += [pl.BlockSpec(p.shape, lambda i: (0, 0)) for p in params]
    out_specs = [pl.BlockSpec((tr, w), lambda i: (i, 0)) for (w, _) in outs]
    out_specs += [pl.BlockSpec(sh, lambda i: (0, 0)) for sh in sums]
    out_shape = [_sds((s, w), dt) for (w, dt) in outs] + [_sds(sh, f32) for sh in sums]

    def body(*refs):
        res = fn(*[r[...] for r in refs[:nin]])
        for r, v in zip(refs[nin:nin + nout], res[:nout]):
            r[...] = v.astype(r.dtype)
        i = pl.program_id(0)
        for r, v in zip(refs[nin + nout:], res[nout:]):
            @pl.when(i == 0)
            def _(r=r, v=v):
                r[...] = v

            @pl.when(i > 0)
            def _(r=r, v=v):
                r[...] += v

    res = pl.pallas_call(
        body, name=name, grid=(s // tr,), in_specs=in_specs, out_specs=out_specs, out_shape=out_shape,
        compiler_params=_cp(("arbitrary",)),
    )(*[r[0] for r in rows], *params)
    return res


def _rms(x, g):
    return x * lax.rsqrt(jnp.mean(x * x, axis=-1, keepdims=True) + EPS) * g


def rms_fwd(x, g, name):
    return rowwise(name, lambda xb, gb: (_rms(xb, gb),), [(x, D, 0)], [g], [(D, bf16)])[0]


def rms_bwd(x, du, dres, g, name):
    def fn(xb, dub, drb, gb):
        _, vjp = jax.vjp(_rms, xb, gb)
        dx, dg = vjp(dub.astype(f32))
        return drb + dx, drb + dx, dg

    return rowwise(name, fn, [(x, D, 0), (du, D, 0), (dres, D, 0)], [g], [(D, f32), (D, bf16)], [(1, D)])


def _merge(gates, up_p, up_s, up_g):
    sg = jax.nn.sigmoid(gates)
    return sg[:, :D] * up_p + sg[:, D:2 * D] * up_s + sg[:, 2 * D:] * up_g


def merge_fwd(proj, ups, name):
    return rowwise(name, lambda g, a, b, c: (_merge(g, a, b, c),),
                   [(proj, 3 * D, 0)] + [(u, D, 0) for u in ups], [], [(D, bf16)], tr=128)[0]


def merge_bwd(proj, ups, dmerged, name):
    def fn(g, a, b, c, dm):
        _, vjp = jax.vjp(_merge, g, a, b, c)
        return vjp(dm.astype(f32))

    return rowwise(name, fn, [(proj, 3 * D, 0)] + [(u, D, 0) for u in ups] + [(dmerged, D, 0)], [],
                   [(3 * D, bf16), (D, bf16), (D, bf16), (D, bf16)], tr=128)


def _gdn_out(o, z, g):
    ys = []
    for h in range(NH):
        sl = slice(h * HD, (h + 1) * HD)
        ys.append(_rms(o[:, sl], g) * jax.nn.silu(z[:, sl]))
    return jnp.concatenate(ys, axis=1)


def gdn_out_fwd(o, proj, g, name):
    return rowwise(name, lambda ob, zb, gb: (_gdn_out(ob, zb, gb),), [(o, WH, 0), (proj, WH, OFF_Z // WH)], [g],
                   [(WH, bf16)])[0]


def gdn_out_bwd(o, proj, g, dy, name):
    def fn(ob, zb, dyb, gb):
        _, vjp = jax.vjp(_gdn_out, ob, zb, gb)
        return vjp(dyb.astype(f32))

    return rowwise(name, fn, [(o, WH, 0), (proj, WH, OFF_Z // WH), (dy, WH, 0)], [g], [(WH, f32), (WH, bf16)],
                   [(1, HD)])


def loss_head(x, target, g, name):
    def loss_fn(xb, gb, tb):
        err = _rms(xb, gb) - tb
        return (0.5 / D) * jnp.sum(jnp.sum(err * err, axis=1, keepdims=True), axis=0, keepdims=True)

    def fn(xb, tb, gb):
        val, vjp = jax.vjp(functools.partial(loss_fn, tb=tb), xb, gb)
        dx, dg = vjp(jnp.ones((1, 1), f32))
        return dx, dx, jnp.broadcast_to(val, (1, HD)), dg

    return rowwise(name, fn, [(x, D, 0), (target, D, 0)], [g], [(D, f32), (D, bf16)], [(1, HD), (1, D)])


PB = 256


def _split(v):
    hi = v.astype(bf16)
    return hi, (v - hi.astype(f32)).astype(bf16)


def _band_dot(make_band, v, s, forward):
    hi, lo = _split(v)
    nb = s // PB
    outs = []
    for r in range(nb):
        lo_r = max(r - 1, 0) if forward else r
        hi_r = r + 1 if forward else min(r + 2, nb)
        band = make_band(r * PB, lo_r * PB, (hi_r - lo_r) * PB)
        sl = slice(lo_r * PB, hi_r * PB)
        outs.append(jnp.dot(band, hi[sl], preferred_element_type=f32) + jnp.dot(band, lo[sl], preferred_element_type=f32))
    return jnp.concatenate(outs, axis=0)


def _pool_common(p, win, s):
    def band(row0, col0, ncol):
        t = row0 + lax.broadcasted_iota(jnp.int32, (PB, ncol), 0)
        u = col0 + lax.broadcasted_iota(jnp.int32, (PB, ncol), 1)
        return jnp.logical_and(u <= t, t < u + win).astype(bf16)

    def band_t(row0, col0, ncol):
        u = row0 + lax.broadcasted_iota(jnp.int32, (PB, ncol), 0)
        t = col0 + lax.broadcasted_iota(jnp.int32, (PB, ncol), 1)
        return jnp.logical_and(u <= t, t < u + win).astype(bf16)

    t = lax.broadcasted_iota(jnp.int32, (s, 1), 0)
    inv_n = 1.0 / jnp.minimum(t + 1, win).astype(f32)
    d = _band_dot(band, p, s, True) * inv_n - p
    return d, inv_n, band_t


def pool_fwd(proj, pool_w, pool_scale, name):
    s = proj.shape[0]

    def body(p_ref, w_ref, sc_ref, y_ref):
        win = jnp.left_shift(2, pl.program_id(0))
        d, _, _ = _pool_common(p_ref[...], win, s)
        y = jnp.dot(d.astype(bf16), w_ref[...].astype(bf16), preferred_element_type=f32) * sc_ref[...]
        y_ref[...] = y.astype(bf16)

    return pl.pallas_call(
        body, name=name, grid=(4,),
        in_specs=[pl.BlockSpec((s, HD), lambda g: (0, OFF_P // HD + g)), pl.BlockSpec((None, HD, HD), lambda g: (g, 0, 0)),
                  pl.BlockSpec((1, HD), lambda g: (0, g))],
        out_specs=pl.BlockSpec((s, HD), lambda g: (0, g)), out_shape=_sds((s, W_POOL), bf16),
        compiler_params=_cp(("arbitrary",)),
    )(proj, pool_w, pool_scale)


def pool_bwd(proj, pool_w, pool_scale, dy, name):
    s = proj.shape[0]

    def body(p_ref, w_ref, sc_ref, dy_ref, dp_ref, dw_ref, dsc_ref):
        win = jnp.left_shift(2, pl.program_id(0))
        d, inv_n, band_t = _pool_common(p_ref[...], win, s)
        w = w_ref[...].astype(bf16)
        dyf = dy_ref[...].astype(f32)
        dsc_ref[...] = jnp.sum(dyf * jnp.dot(d.astype(bf16), w, preferred_element_type=f32), axis=0, keepdims=True)
        dys = (dyf * sc_ref[...]).astype(bf16)
        dd = lax.dot_general(dys, w, (((1,), (1,)), ((), ())), preferred_element_type=f32)
        dw_ref[...] = lax.dot_general(d.astype(bf16), dys, (((0,), (0,)), ((), ())), preferred_element_type=f32)
        dp_ref[...] = (_band_dot(band_t, dd * inv_n, s, False) - dd).astype(bf16)

    return pl.pallas_call(
        body, name=name, grid=(4,),
        in_specs=[pl.BlockSpec((s, HD), lambda g: (0, OFF_P // HD + g)), pl.BlockSpec((None, HD, HD), lambda g: (g, 0, 0)),
                  pl.BlockSpec((1, HD), lambda g: (0, g)), pl.BlockSpec((s, HD), lambda g: (0, g))],
        out_specs=[pl.BlockSpec((s, HD), lambda g: (0, g)), pl.BlockSpec((None, HD, HD), lambda g: (g, 0, 0)),
                   pl.BlockSpec((1, HD), lambda g: (0, g))],
        out_shape=[_sds((s, W_POOL), bf16), _sds((4, HD, HD), f32), _sds((1, W_POOL), f32)],
        compiler_params=_cp(("arbitrary",)),
    )(proj, pool_w, pool_scale, dy)


def _sb_tile(q, k_ref, kb, qi, carry):
    k = k_ref[pl.ds(pl.multiple_of(kb * TQ, TQ), TQ), :].astype(bf16)
    z = lax.dot_general(q, k, (((1,), (1,)), ((), ())), preferred_element_type=f32)
    row = qi * TQ + lax.broadcasted_iota(jnp.int32, (TQ, TQ), 0)
    col = kb * TQ + lax.broadcasted_iota(jnp.int32, (TQ, TQ), 1)
    mask = col < row
    ls = jnp.where(mask, jax.nn.log_sigmoid(-z), 0.0)
    j = lax.broadcasted_iota(jnp.int32, (TQ, TQ), 0)
    u = lax.broadcasted_iota(jnp.int32, (TQ, TQ), 1)
    later = (j > u).astype(bf16)
    hi, lo = _split(ls)
    lw = jnp.dot(hi, later, preferred_element_type=f32) + jnp.dot(lo, later, preferred_element_type=f32)
    a = jnp.where(mask, jnp.exp(ls + z + lw + carry), 0.0)
    return z, mask, ls, a


def sb_fwd(proj, name):
    s = proj.shape[0]
    nq = s // TQ
    scale = HD ** -0.5

    def body(q_ref, k_ref, v_ref, y_ref):
        qi = pl.program_id(1)
        q = (q_ref[...] * scale).astype(bf16)

        def step(j, c):
            acc, carry = c
            kb = qi - j
            _, _, ls, a = _sb_tile(q, k_ref, kb, qi, carry)
            v = v_ref[pl.ds(pl.multiple_of(kb * TQ, TQ), TQ), :].astype(bf16)
            acc = acc + jnp.dot(a.astype(bf16), v, preferred_element_type=f32)
            return acc, carry + jnp.sum(ls, axis=1, keepdims=True)

        acc, _ = lax.fori_loop(0, qi + 1, step, (jnp.zeros((TQ, HD), f32), jnp.zeros((TQ, 1), f32)))
        y_ref[...] = acc.astype(bf16)

    c0 = OFF_SB // HD
    return pl.pallas_call(
        body, name=name, grid=(NH, nq),
        in_specs=[pl.BlockSpec((TQ, HD), lambda h, i: (i, c0 + h)), pl.BlockSpec((s, HD), lambda h, i: (0, c0 + NH + h)),
                  pl.BlockSpec((s, HD), lambda h, i: (0, c0 + 2 * NH + h))],
        out_specs=pl.BlockSpec((TQ, HD), lambda h, i: (i, h)), out_shape=_sds((s, WH), bf16),
        compiler_params=_cp(("arbitrary", "arbitrary")),
    )(proj, proj, proj)


def sb_bwd(proj, dy, name):
    s = proj.shape[0]
    nq = s // TQ
    scale = HD ** -0.5

    def body(q_ref, k_ref, v_ref, do_ref, dq_ref, dk_ref, dv_ref, e_scr, z_scr, dk_acc, dv_acc):
        qi = pl.program_id(1)
        q = (q_ref[...] * scale).astype(bf16)
        do = do_ref[...]

        @pl.when(qi == 0)
        def _():
            dk_acc[...] = jnp.zeros_like(dk_acc)
            dv_acc[...] = jnp.zeros_like(dv_acc)

        def sweep_back(j, carry):
            kb = qi - j
            rows = pl.ds(pl.multiple_of(kb * TQ, TQ), TQ)
            z, _, ls, a = _sb_tile(q, k_ref, kb, qi, carry)
            v = v_ref[rows, :].astype(bf16)
            da = lax.dot_general(do, v, (((1,), (1,)), ((), ())), preferred_element_type=f32)
            e_scr[kb] = da * a
            z_scr[kb] = z
            dv_acc[rows, :] += lax.dot_general(a.astype(bf16), do, (((0,), (0,)), ((), ())), preferred_element_type=f32)
            return carry + jnp.sum(ls, axis=1, keepdims=True)

        lax.fori_loop(0, qi + 1, sweep_back, jnp.zeros((TQ, 1), f32))

        def sweep_fwd(kb, c):
            dq, carry = c
            rows = pl.ds(pl.multiple_of(kb * TQ, TQ), TQ)
            e, z = e_scr[kb], z_scr[kb]
            row = qi * TQ + lax.broadcasted_iota(jnp.int32, (TQ, TQ), 0)
            col = kb * TQ + lax.broadcasted_iota(jnp.int32, (TQ, TQ), 1)
            j = lax.broadcasted_iota(jnp.int32, (TQ, TQ), 0)
            u = lax.broadcasted_iota(jnp.int32, (TQ, TQ), 1)
            earlier = (j < u).astype(bf16)
            hi, lo = _split(e)
            ew = jnp.dot(hi, earlier, preferred_element_type=f32) + jnp.dot(lo, earlier, preferred_element_type=f32)
            sig = jax.nn.sigmoid(z)
            dz = jnp.where(col < row, e * (1.0 - sig) - (ew + carry) * sig, 0.0).astype(bf16)
            k = k_ref[rows, :].astype(bf16)
            dq = dq + jnp.dot(dz, k, preferred_element_type=f32)
            dk_acc[rows, :] += lax.dot_general(dz, q, (((0,), (0,)), ((), ())), preferred_element_type=f32)
            return dq, carry + jnp.sum(e, axis=1, keepdims=True)

        dq, _ = lax.fori_loop(0, qi + 1, sweep_fwd, (jnp.zeros((TQ, HD), f32), jnp.zeros((TQ, 1), f32)))
        dq_ref[...] = (dq * scale).astype(bf16)

        @pl.when(qi == nq - 1)
        def _():
            dk_ref[...] = dk_acc[...].astype(bf16)
            dv_ref[...] = dv_acc[...].astype(bf16)

    c0 = OFF_SB // HD
    return pl.pallas_call(
        body, name=name, grid=(NH, nq),
        in_specs=[pl.BlockSpec((TQ, HD), lambda h, i: (i, c0 + h)), pl.BlockSpec((s, HD), lambda h, i: (0, c0 + NH + h)),
                  pl.BlockSpec((s, HD), lambda h, i: (0, c0 + 2 * NH + h)), pl.BlockSpec((TQ, HD), lambda h, i: (i, h))],
        out_specs=[pl.BlockSpec((TQ, HD), lambda h, i: (i, h)), pl.BlockSpec((s, HD), lambda h, i: (0, h)),
                   pl.BlockSpec((s, HD), lambda h, i: (0, h))],
        out_shape=[_sds((s, WH), bf16)] * 3,
        scratch_shapes=[pltpu.VMEM((nq, TQ, TQ), f32), pltpu.VMEM((nq, TQ, TQ), f32), pltpu.VMEM((s, HD), f32),
                        pltpu.VMEM((s, HD), f32)],
        compiler_params=_cp(("arbitrary", "arbitrary")),
    )(proj, proj, proj, dy)


CB = 256


def _shift_down(v, k, s):
    if k == 0:
        return v
    row = lax.broadcasted_iota(jnp.int32, v.shape, 0)
    return jnp.where(row < k, 0.0, pltpu.roll(v, k, axis=0))


def _shift_up(v, k, s):
    if k == 0:
        return v
    row = lax.broadcasted_iota(jnp.int32, v.shape, 0)
    return jnp.where(row >= s - k, 0.0, pltpu.roll(v, s - k, axis=0))


def conv_fwd(proj, w, name):
    s = proj.shape[0]

    def body(x_ref, w_ref, y_ref):
        x, wv = x_ref[...], w_ref[...]
        y = sum(wv[3 - k:4 - k, :] * _shift_down(x, k, s) for k in range(4))
        y_ref[...] = jax.nn.silu(y)

    return pl.pallas_call(
        body, name=name, grid=(3 * WH // CB,),
        in_specs=[pl.BlockSpec((s, CB), lambda j: (0, OFF_GQ // CB + j)), pl.BlockSpec((4, CB), lambda j: (0, j))],
        out_specs=pl.BlockSpec((s, CB), lambda j: (0, j)), out_shape=_sds((s, 3 * WH), f32),
        compiler_params=_cp(("parallel",)),
    )(proj, w)


def conv_bwd(proj, w, dc, name):
    s = proj.shape[0]

    def body(x_ref, w_ref, dc_ref, dx_ref, dw_ref):
        x, wv = x_ref[...], w_ref[...]
        xs = [_shift_down(x, k, s) for k in range(4)]
        y = sum(wv[3 - k:4 - k, :] * xs[k] for k in range(4))
        sig = jax.nn.sigmoid(y)
        dy = dc_ref[...] * (sig * (1.0 + y * (1.0 - sig)))
        dx_ref[...] = sum(wv[3 - k:4 - k, :] * _shift_up(dy, k, s) for k in range(4)).astype(bf16)
        dw_ref[...] = jnp.concatenate([jnp.sum(dy * xs[3 - i], axis=0, keepdims=True) for i in range(4)], axis=0)

    return pl.pallas_call(
        body, name=name, grid=(3 * WH // CB,),
        in_specs=[pl.BlockSpec((s, CB), lambda j: (0, OFF_GQ // CB + j)), pl.BlockSpec((4, CB), lambda j: (0, j)),
                  pl.BlockSpec((s, CB), lambda j: (0, j))],
        out_specs=[pl.BlockSpec((s, CB), lambda j: (0, j)), pl.BlockSpec((4, CB), lambda j: (0, j))],
        out_shape=[_sds((s, 3 * WH), bf16), _sds((4, 3 * WH), f32)],
        compiler_params=_cp(("parallel",)),
    )(proj, w, dc)


SOLVE_PASSES = 3
NN, NT, TN = (((1,), (0,)), ((), ())), (((1,), (1,)), ((), ())), (((0,), (0,)), ((), ()))


def _pdot_impl(a, b, dn, passes):
    ah, al = _split(a)
    bh, bl = _split(b)
    dot = lambda p, q: lax.dot_general(p, q, dn, preferred_element_type=f32)
    if passes == 1:
        return dot(ah, bh)
    if passes == 2:
        return dot(ah, bh) + dot(ah, bl)
    return dot(ah, bh) + (dot(ah, bl) + dot(al, bh))


@functools.partial(jax.custom_vjp, nondiff_argnums=(2,))
def _pdot(a, b, passes):
    return _pdot_impl(a, b, NN, passes)


def _pdot_fwd(a, b, passes):
    return _pdot_impl(a, b, NN, passes), (a, b)


def _pdot_bwd(passes, res, ct):
    a, b = res
    return _pdot_impl(ct, b, NT, passes), _pdot_impl(a, ct, TN, passes)


_pdot.defvjp(_pdot_fwd, _pdot_bwd)


@functools.partial(jax.custom_vjp, nondiff_argnums=(2,))
def _pdot_nt(a, b, passes):
    return _pdot_impl(a, b, NT, passes)


def _pdot_nt_fwd(a, b, passes):
    return _pdot_impl(a, b, NT, passes), (a, b)


def _pdot_nt_bwd(passes, res, ct):
    a, b = res
    return _pdot_impl(ct, b, NN, passes), _pdot_impl(ct, a, TN, passes)


_pdot_nt.defvjp(_pdot_nt_fwd, _pdot_nt_bwd)


def _lane_pick(v, h):
    lane = lax.broadcasted_iota(jnp.int32, v.shape, v.ndim - 1)
    return jnp.sum(jnp.where(lane == h, v, 0.0), axis=-1, keepdims=True)


def _l2n(v):
    return v * lax.rsqrt(jnp.sum(v * v, axis=-1, keepdims=True) + EPS)


def _dot_nt(a, b):
    return lax.dot_general(a, b, (((1,), (1,)), ((), ())), preferred_element_type=f32)


def _gdn_chunk(cq, ck, cv, ab, alog, dtb, h):
    ones = jnp.ones((CH, HD), f32)
    q = _l2n(cq) * (HD ** -0.5)
    k = _l2n(ck)
    la = -jnp.exp(_lane_pick(alog, h)) * jax.nn.softplus(_lane_pick(ab, h) + _lane_pick(dtb, h))
    beta = jax.nn.sigmoid(_lane_pick(ab, NH + h)) * ones
    i = lax.broadcasted_iota(jnp.int32, (CH, CH), 0)
    j = lax.broadcasted_iota(jnp.int32, (CH, CH), 1)
    incl, strict = j <= i, j < i
    g = _pdot(incl.astype(f32), la * ones, 2)
    diff = g - g.T
    gamma = jnp.where(incl, jnp.exp(jnp.where(incl, diff, 0.0)), 0.0)
    lower = jnp.where(strict, beta * _pdot_nt(k, k, 1) * gamma, 0.0)
    eye = (i == j).astype(f32)
    inv = eye - lower
    pw = _pdot(lower, lower, SOLVE_PASSES)
    for m in range(1, int(math.log2(CH))):
        inv = inv + _pdot(inv, pw, SOLVE_PASSES)
        if m < int(math.log2(CH)) - 1:
            pw = _pdot(pw, pw, SOLVE_PASSES)
    eg = jnp.exp(g)
    u = _pdot(inv, cv * beta, SOLVE_PASSES)
    w = _pdot(inv, k * (beta * eg), SOLVE_PASSES)
    qk = _pdot_nt(q, k, 1) * gamma
    g_last = g[CH - 1:CH, :]
    return u, w, q * eg, k * jnp.exp(g_last - g), qk, jnp.exp(g_last)


def gdn_a_fwd(c, proj, alog, dtb, name):
    s = c.shape[0]
    nc = s // CH

    def body(c_ref, ab_ref, al_ref, dt_ref, u_ref, w_ref, qd_ref, kd_ref, qk_ref, dec_ref):
        for h in range(NH):
            sl = slice(h * HD, (h + 1) * HD)
            cq, ck, cv = (c_ref[:, t * WH + h * HD:t * WH + (h + 1) * HD] for t in range(3))
            res = _gdn_chunk(cq, ck, cv, ab_ref[...], al_ref[...], dt_ref[...], h)
            for r, v in zip((u_ref, w_ref, qd_ref, kd_ref, qk_ref), res[:5]):
                r[:, sl] = v
            dec_ref[:, sl] = jnp.broadcast_to(res[5], (8, HD))

    row = pl.BlockSpec((CH, WH), lambda n: (n, 0))
    par = pl.BlockSpec((1, HD), lambda n: (0, 0))
    return pl.pallas_call(
        body, name=name, grid=(nc,),
        in_specs=[pl.BlockSpec((CH, 3 * WH), lambda n: (n, 0)), pl.BlockSpec((CH, HD), lambda n: (n, OFF_AB // HD)), par, par],
        out_specs=[row] * 5 + [pl.BlockSpec((None, 8, WH), lambda n: (n, 0, 0))],
        out_shape=[_sds((s, WH), f32)] * 5 + [_sds((nc, 8, WH), f32)],
        compiler_params=_cp(("parallel",)),
    )(c, proj, alog, dtb)


def gdn_a_bwd(c, proj, alog, dtb, cots, ddec, name):
    s = c.shape[0]
    nc = s // CH

    def body(c_ref, ab_ref, al_ref, dt_ref, du_ref, dw_ref, dqd_ref, dkd_ref, dqk_ref, ddec_ref,
             dc_ref, dab_ref, dal_ref, ddt_ref):
        n = pl.program_id(0)
        dab, dal, ddt = jnp.zeros((CH, HD), f32), jnp.zeros((1, HD), f32), jnp.zeros((1, HD), f32)
        for h in range(NH):
            sl = slice(h * HD, (h + 1) * HD)
            cq, ck, cv = (c_ref[:, t * WH + h * HD:t * WH + (h + 1) * HD] for t in range(3))
            _, vjp = jax.vjp(functools.partial(_gdn_chunk, h=h), cq, ck, cv, ab_ref[...], al_ref[...], dt_ref[...])
            lane = lax.broadcasted_iota(jnp.int32, (1, HD), 1)
            dd = jnp.where(lane == 0, ddec_ref[0:1, sl], 0.0)
            dcs = vjp((du_ref[:, sl], dw_ref[:, sl], dqd_ref[:, sl], dkd_ref[:, sl], dqk_ref[:, sl], dd))
            for t in range(3):
                dc_ref[:, t * WH + h * HD:t * WH + (h + 1) * HD] = dcs[t]
            dab, dal, ddt = dab + dcs[3], dal + dcs[4], ddt + dcs[5]
        dab_ref[...] = dab.astype(bf16)

        @pl.when(n == 0)
        def _():
            dal_ref[...] = dal
            ddt_ref[...] = ddt

        @pl.when(n > 0)
        def _():
            dal_ref[...] += dal
            ddt_ref[...] += ddt

    row = pl.BlockSpec((CH, WH), lambda n: (n, 0))
    wide = pl.BlockSpec((CH, 3 * WH), lambda n: (n, 0))
    par = pl.BlockSpec((1, HD), lambda n: (0, 0))
    return pl.pallas_call(
        body, name=name, grid=(nc,),
        in_specs=[wide, pl.BlockSpec((CH, HD), lambda n: (n, OFF_AB // HD)), par, par] + [row] * 5
        + [pl.BlockSpec((None, 8, WH), lambda n: (n, 0, 0))],
        out_specs=[wide, pl.BlockSpec((CH, HD), lambda n: (n, 0)), par, par],
        out_shape=[_sds((s, 3 * WH), f32), _sds((s, HD), bf16), _sds((1, HD), f32), _sds((1, HD), f32)],
        compiler_params=_cp(("arbitrary",)),
    )(c, proj, alog, dtb, *cots, ddec)


def gdn_b_fwd(u, w, qd, kd, qk, dec, name):
    s = u.shape[0]
    nc = s // CH

    def body(u_ref, w_ref, qd_ref, kd_ref, qk_ref, dec_ref, o_ref, st_ref, state):
        n = pl.program_id(0)

        @pl.when(n == 0)
        def _():
            state[...] = jnp.zeros_like(state)

        for h in range(NH):
            sl = slice(h * HD, (h + 1) * HD)
            st = state[sl, :]
            st_ref[sl, :] = st
            sb = st.astype(bf16)
            vn = u_ref[:, sl] - jnp.dot(w_ref[:, sl].astype(bf16), sb, preferred_element_type=f32)
            vb = vn.astype(bf16)
            o_ref[:, sl] = (jnp.dot(qd_ref[:, sl].astype(bf16), sb, preferred_element_type=f32)
                            + jnp.dot(qk_ref[:, sl].astype(bf16), vb, preferred_element_type=f32))
            state[sl, :] = st * dec_ref[0:1, sl] + lax.dot_general(
                kd_ref[:, sl].astype(bf16), vb, (((0,), (0,)), ((), ())), preferred_element_type=f32)

    row = pl.BlockSpec((CH, WH), lambda n: (n, 0))
    return pl.pallas_call(
        body, name=name, grid=(nc,),
        in_specs=[row] * 5 + [pl.BlockSpec((None, 8, WH), lambda n: (n, 0, 0))],
        out_specs=[row, pl.BlockSpec((None, WH, HD), lambda n: (n, 0, 0))],
        out_shape=[_sds((s, WH), f32), _sds((nc, WH, HD), f32)],
        scratch_shapes=[pltpu.VMEM((WH, HD), f32)],
        compiler_params=_cp(("arbitrary",)),
    )(u, w, qd, kd, qk, dec)


def gdn_b_bwd(u, w, qd, kd, qk, dec, states, do, name):
    s = u.shape[0]
    nc = s // CH

    def body(u_ref, w_ref, qd_ref, kd_ref, qk_ref, dec_ref, st_ref, do_ref,
             du_ref, dw_ref, dqd_ref, dkd_ref, dqk_ref, ddec_ref, dstate):
        n = pl.program_id(0)

        @pl.when(n == 0)
        def _():
            dstate[...] = jnp.zeros_like(dstate)

        for h in range(NH):
            sl = slice(h * HD, (h + 1) * HD)
            st, ds = st_ref[sl, :], dstate[sl, :]
            sb, dsb = st.astype(bf16), ds.astype(bf16)
            wb, qdb, kdb, qkb = (r[:, sl].astype(bf16) for r in (w_ref, qd_ref, kd_ref, qk_ref))
            dob = do_ref[:, sl].astype(bf16)
            vn = u_ref[:, sl] - jnp.dot(wb, sb, preferred_element_type=f32)
            vb = vn.astype(bf16)
            dvn = (lax.dot_general(qkb, dob, (((0,), (0,)), ((), ())), preferred_element_type=f32)
                   + jnp.dot(kdb, dsb, preferred_element_type=f32))
            dvb = dvn.astype(bf16)
            du_ref[:, sl] = dvn
            dw_ref[:, sl] = -_dot_nt(dvb, sb)
            dqd_ref[:, sl] = _dot_nt(dob, sb)
            dkd_ref[:, sl] = _dot_nt(vb, dsb)
            dqk_ref[:, sl] = _dot_nt(dob, vb)
            tot = jnp.sum(jnp.sum(ds * st, axis=1, keepdims=True), axis=0, keepdims=True)
            ddec_ref[:, sl] = jnp.broadcast_to(tot, (8, HD))
            dstate[sl, :] = (ds * dec_ref[0:1, sl]
                             + lax.dot_general(qdb, dob, (((0,), (0,)), ((), ())), preferred_element_type=f32)
                             - lax.dot_general(wb, dvb, (((0,), (0,)), ((), ())), preferred_element_type=f32))

    row = pl.BlockSpec((CH, WH), lambda n: (nc - 1 - n, 0))
    small = pl.BlockSpec((None, 8, WH), lambda n: (nc - 1 - n, 0, 0))
    return pl.pallas_call(
        body, name=name, grid=(nc,),
        in_specs=[row] * 5 + [small, pl.BlockSpec((None, WH, HD), lambda n: (nc - 1 - n, 0, 0)), row],
        out_specs=[row] * 5 + [small],
        out_shape=[_sds((s, WH), f32)] * 5 + [_sds((nc, 8, WH), f32)],
        scratch_shapes=[pltpu.VMEM((WH, HD), f32)],
        compiler_params=_cp(("arbitrary",)),
    )(u, w, qd, kd, qk, dec, states, do)


ANY = pl.BlockSpec(memory_space=pl.ANY)


def _dev_index(p):
    return 4 * p[0] + 2 * p[1] + p[2]


def _shard_of(ref, axis, size, idx):
    return ref.at[pl.ds(idx * size, size), :] if axis == 0 else ref.at[:, pl.ds(idx * size, size)]


def _peer(x, y, c, r):
    return (1 - x if r & 4 else x, 1 - y if r & 2 else y, 1 - c if r & 1 else c)


ALL_PEERS, SIBLING, OTHER_CHIPS = tuple(range(1, NDEV)), (1,), (4, 2, 6)


def _launch(body, ins, out_sds, name, sequencer_id, relations=ALL_PEERS):
    n = len(ins)
    sems = [pltpu.SemaphoreType.DMA((n, 7)), pltpu.SemaphoreType.DMA((n, 7)), pltpu.SemaphoreType.DMA((n,))]
    if sequencer_id is None:
        return pl.pallas_call(
            lambda *refs: body(refs[:n], refs[n:2 * n], *refs[2 * n:]), name=name, in_specs=[ANY] * n, out_specs=[ANY] * n,
            out_shape=out_sds, scratch_shapes=sems, compiler_params=pltpu.CompilerParams(has_side_effects=True),
        )(*ins)
    in_refs = [jax.new_ref(a, memory_space=pltpu.MemorySpace.HBM) for a in ins]
    out_refs = [jax.empty_ref(sd, memory_space=pltpu.MemorySpace.HBM) for sd in out_sds]

    @pl.kernel(mesh=plsc.ScalarSubcoreMesh(axis_name="sequencer", num_cores=1), name=name, scratch_types=sems,
               compiler_params=pltpu.CompilerParams(collective_id=sequencer_id))
    def launch(send, recv, loc):
        x, y, c = lax.axis_index("x"), lax.axis_index("y"), lax.axis_index("c")
        barrier = pltpu.get_barrier_semaphore()
        for r in relations:
            pl.semaphore_signal(barrier, inc=1, device_id=_peer(x, y, c, r), device_id_type=MESH)
        pl.semaphore_wait(barrier, len(relations))
        body(in_refs, out_refs, send, recv, loc)

    launch()
    return [r[...] for r in out_refs]


def all_gather(xs, axes, name, sequencer_id=None):
    n = len(xs)
    fulls = [tuple(d * (NDEV if a == ax else 1) for a, d in enumerate(x.shape)) for x, ax in zip(xs, axes)]

    def body(x_refs, o_refs, send, recv, loc):
        x, y, c = lax.axis_index("x"), lax.axis_index("y"), lax.axis_index("c")
        me, sib = (x, y, c), (x, y, 1 - c)
        chips = [(1 - x, y), (x, 1 - y), (1 - x, 1 - y)]

        def part(t, p):
            return _shard_of(o_refs[t], axes[t], xs[t].shape[axes[t]], _dev_index(p))

        def copy(t, k, block, to, src=None):
            return pltpu.make_async_remote_copy(
                src_ref=part(t, block) if src is None else src, dst_ref=part(t, block),
                send_sem=send.at[t, k], recv_sem=recv.at[t, k], device_id=to, device_id_type=MESH)

        mine = [pltpu.make_async_copy(x_refs[t], part(t, me), loc.at[t]) for t in range(n)]
        for cp in mine:
            cp.start()
        first = []
        for t in range(n):
            first.append(copy(t, 0, me, sib, src=x_refs[t]))
            first += [copy(t, 1 + j, me, (*chip, c), src=x_refs[t]) for j, chip in enumerate(chips)]
        for cp in first:
            cp.start()
        passed = []
        for t in range(n):
            for j, chip in enumerate(chips):
                copy(t, 1 + j, (*chip, c), me).wait_recv()
                passed.append(copy(t, 4 + j, (*chip, c), sib))
                passed[-1].start()
        for t in range(n):
            copy(t, 0, sib, me).wait_recv()
            for j, chip in enumerate(chips):
                copy(t, 4 + j, (*chip, 1 - c), me).wait_recv()
        for cp in first + passed:
            cp.wait_send()
        for cp in mine:
            cp.wait()

    return _launch(body, xs, [_sds(f, x.dtype) for f, x in zip(fulls, xs)], name, sequencer_id)


def pair_swap(gs, axes, name, sequencer_id):
    n = len(gs)
    shards = [tuple(d // (NDEV if a == ax else 1) for a, d in enumerate(g.shape)) for g, ax in zip(gs, axes)]

    def body(g_refs, p_refs, send, recv, loc):
        x, y, c = lax.axis_index("x"), lax.axis_index("y"), lax.axis_index("c")

        def copy(t, j):
            return pltpu.make_async_remote_copy(
                src_ref=_shard_of(g_refs[t], axes[t], shards[t][axes[t]], 2 * j + (1 - c)), dst_ref=p_refs[t].at[j],
                send_sem=send.at[t, j], recv_sem=recv.at[t, j], device_id=(x, y, 1 - c), device_id_type=MESH)

        copies = [copy(t, j) for t in range(n) for j in range(4)]
        for cp in copies:
            cp.start()
        for cp in copies:
            cp.wait_recv()
        for cp in copies:
            cp.wait_send()

    return _launch(body, gs, [_sds((4,) + sh, g.dtype) for sh, g in zip(shards, gs)], name, sequencer_id, SIBLING)


def pair_add(g, p, axis, name):
    _, rows, cols = p.shape
    tr = _row_tile(rows, cols)
    nb = rows // tr
    if axis == 0:
        g_spec = pl.BlockSpec((tr, cols), lambda j, i, c: ((2 * j + c[0]) * nb + i, 0))
    else:
        g_spec = pl.BlockSpec((tr, cols), lambda j, i, c: (i, 2 * j + c[0]))
    blk = pl.BlockSpec((None, tr, cols), lambda j, i, c: (j, i, 0))

    def body(c_ref, g_ref, p_ref, q_ref):
        q_ref[...] = (g_ref[...].astype(f32) + p_ref[...].astype(f32)).astype(bf16)

    return pl.pallas_call(
        body, name=name, out_shape=_sds(p.shape, bf16),
        grid_spec=pltpu.PrefetchScalarGridSpec(num_scalar_prefetch=1, grid=(4, nb), in_specs=[g_spec, blk], out_specs=blk),
        compiler_params=_cp(("parallel", "parallel")),
    )(lax.axis_index("c").astype(jnp.int32).reshape(1), g, p)


def chip_exchange(qs, name, sequencer_id):
    n = len(qs)

    def body(q_refs, r_refs, send, recv, loc):
        x, y, c = lax.axis_index("x"), lax.axis_index("y"), lax.axis_index("c")
        my_chip = 2 * x + y
        peers = [_peer(x, y, c, r) for r in OTHER_CHIPS]
        mine = [pltpu.make_async_copy(q_refs[t].at[my_chip], r_refs[t].at[my_chip], loc.at[t]) for t in range(n)]
        for cp in mine:
            cp.start()

        def copy(t, k, slot):
            p = peers[k]
            return pltpu.make_async_remote_copy(
                src_ref=q_refs[t].at[2 * p[0] + p[1]], dst_ref=r_refs[t].at[slot], send_sem=send.at[t, k], recv_sem=recv.at[t, k],
                device_id=p, device_id_type=MESH)

        sends = [copy(t, k, my_chip) for t in range(n) for k in range(3)]
        for cp in sends:
            cp.start()
        for t in range(n):
            for k in range(3):
                copy(t, k, 2 * peers[k][0] + peers[k][1]).wait_recv()
        for cp in sends:
            cp.wait_send()
        for cp in mine:
            cp.wait()

    return _launch(body, qs, [_sds(q.shape, q.dtype) for q in qs], name, sequencer_id, OTHER_CHIPS)


def _adamw(w, g, m, v):
    m = ADAM_B1 * m + (1.0 - ADAM_B1) * g
    v = ADAM_B2 * v + (1.0 - ADAM_B2) * jnp.square(g)
    m_hat = m / (1.0 - ADAM_B1 ** ADAM_STEP)
    v_hat = v / (1.0 - ADAM_B2 ** ADAM_STEP)
    return -ADAM_LR * (m_hat / (jnp.sqrt(v_hat) + ADAM_EPS) + ADAM_WD * w), m, v


def _sum8(r_ref):
    g = r_ref[0].astype(f32)
    for j in range(1, r_ref.shape[0]):
        g = g + r_ref[j].astype(f32)
    return g


def _row_tile(rows, cols):
    tr = min(rows, max(8, 1 << int(math.log2((1 << 18) / cols))))
    assert rows % tr == 0, (rows, cols)
    return tr


def sum_partials(r, name):
    _, rows, cols = r.shape
    tr = _row_tile(rows, cols)

    def body(r_ref, g_ref):
        g_ref[...] = _sum8(r_ref)

    return pl.pallas_call(
        body, name=name, grid=(rows // tr,), in_specs=[pl.BlockSpec((r.shape[0], tr, cols), lambda i: (0, i, 0))],
        out_specs=pl.BlockSpec((tr, cols), lambda i: (i, 0)), out_shape=_sds((rows, cols), f32),
        compiler_params=_cp(("parallel",)),
    )(r)


def adamw(w, m, v, layer, name, r=None, g=None, prev=None):
    _, rows, cols = w.shape
    tr = _row_tile(rows, cols)
    blk = pl.BlockSpec((None, tr, cols), lambda i: (layer, i, 0))
    nprev = 0 if prev is None else 4

    def body(w_ref, m_ref, v_ref, src_ref, *rest):
        g_ref, d_ref, nm_ref, nv_ref = rest[nprev:]
        grad = _sum8(src_ref) if g is None else src_ref[...]
        g_ref[...] = grad
        d_ref[...], nm_ref[...], nv_ref[...] = _adamw(w_ref[...], grad, m_ref[...], v_ref[...])

    src, src_spec = (r, pl.BlockSpec((r.shape[0], tr, cols), lambda i: (0, i, 0))) if g is None else (g, pl.BlockSpec((tr, cols), lambda i: (i, 0)))
    return tuple(pl.pallas_call(
        body, name=name, grid=(rows // tr,), in_specs=[blk] * 3 + [src_spec] + [ANY] * nprev, out_specs=[blk] * 4,
        out_shape=[_sds(w.shape, f32)] * 4, input_output_aliases={4 + k: k for k in range(nprev)},
        compiler_params=_cp(("parallel",)),
    )(w, m, v, src, *(prev or ())))


def small_adamw(parts, w, m, v, name):
    def body(p_ref, w_ref, m_ref, v_ref, g_ref, d_ref, nm_ref, nv_ref):
        g = _sum8(p_ref)
        g_ref[...] = g
        d_ref[...], nm_ref[...], nv_ref[...] = _adamw(w_ref[...], g, m_ref[...], v_ref[...])

    return pl.pallas_call(body, name=name, out_shape=[_sds(w.shape, f32)] * 4, compiler_params=_cp())(parts, w, m, v)


def _pack_cols(w):
    pad = jnp.zeros((w.shape[0], NP - OFF_AB - 12), w.dtype)
    return jnp.concatenate([w[:, 5900:12044], w[:, 512:2816], w[:, 2816:5120], w[:, 5120:5888], w[:, 0:512],
                            w[:, 5888:5900], pad], axis=1)


def _unpack_cols(g):
    return jnp.concatenate([g[..., OFF_P:OFF_P + 512], g[..., OFF_SB:OFF_SB + 2304], g[..., OFF_GQ:OFF_GQ + 2304],
                            g[..., OFF_Z:OFF_Z + 768], g[..., OFF_AB:OFF_AB + 12], g[..., OFF_G:OFF_G + 6144]], axis=-1)


def _lanes(v):
    flat = v.reshape(-1)
    n = -(-flat.shape[0] // HD) * HD
    return jnp.pad(flat, (0, n - flat.shape[0])).reshape(n // HD, HD)


def _lanes8(v):
    rows = _lanes(v)
    return jnp.pad(rows, ((0, -rows.shape[0] % 8), (0, 0)))


def _layer_fwd(x, p, l):
    nm = lambda s: f"{s}_l{l}"
    u = rms_fwd(x, p["attn_norm"], nm("rms1"))
    proj = matmul(u, p["w_in"], name=nm("inproj"))
    y_pool = pool_fwd(proj, p["pool_w"], p["pool_scale"], nm("pool"))
    y_sb = sb_fwd(proj, nm("sb"))
    c = conv_fwd(proj, p["conv"], nm("conv"))
    ga = gdn_a_fwd(c, proj, p["alog"], p["dtb"], nm("gdna"))
    o, states = gdn_b_fwd(*ga, nm("gdnb"))
    y_gdn = gdn_out_fwd(o, proj, p["gdn_norm"], nm("gdno"))
    ups = [matmul(y, p[k], name=nm(k)) for y, k in ((y_pool, "w_pool_up"), (y_sb, "w_sb_up"), (y_gdn, "w_gdn_up"))]
    merged = merge_fwd(proj, ups, nm("merge"))
    x1 = matmul(merged, p["w_out"], name=nm("outproj"), epilogue=lambda acc, r: acc + r, extras=(x,))
    u2 = rms_fwd(x1, p["mlp_norm"], nm("rms2"))
    h2 = matmul(u2, p["w_ff1"], name=nm("ff1"), out_dtype=bf16, epilogue=lambda acc: jnp.square(jnp.maximum(acc, 0.0)))
    x2 = matmul(h2, p["w_ff2"], name=nm("ff2"), epilogue=lambda acc, r: acc + r, extras=(x1,))
    saved = dict(x=x, u=u, proj=proj, y_pool=y_pool, y_sb=y_sb, c=c, ga=ga, o=o, states=states, y_gdn=y_gdn, ups=ups,
                 merged=merged, x1=x1, u2=u2, h2=h2)
    return x2, saved


def _layer_bwd(dx2, dx2b, sv, p, l, send):
    nm = lambda s: f"{s}_l{l}"
    s = dx2.shape[0]
    dh = matmul(dx2b, p["w_ff2"], name=nm("d_ff2_x"), tb=True, out_dtype=bf16,
                epilogue=lambda acc, h2: acc * (2.0 * jnp.sqrt(h2.astype(f32))), extras=(sv["h2"],))
    g_ff2 = matmul(sv["h2"], dx2b, name=nm("d_ff2_w"), ta=True, out_dtype=bf16)
    du2 = matmul(dh, p["w_ff1"], name=nm("d_ff1_x"), tb=True)
    g_ff1 = matmul(sv["u2"], dh, name=nm("d_ff1_w"), ta=True, out_dtype=bf16)
    dx1, dx1b, d_mlp_norm = rms_bwd(sv["x1"], du2, dx2, p["mlp_norm"], nm("d_rms2"))
    dmerged = matmul(dx1b, p["w_out"], name=nm("d_out_x"), tb=True, out_dtype=bf16)
    g_out = matmul(sv["merged"], dx1b, name=nm("d_out_w"), ta=True, out_dtype=bf16)
    dgates, *dups = merge_bwd(sv["proj"], sv["ups"], dmerged, nm("d_merge"))
    dys, g_ups = [], []
    for dup, y, k in zip(dups, (sv["y_pool"], sv["y_sb"], sv["y_gdn"]), ("w_pool_up", "w_sb_up", "w_gdn_up")):
        dys.append(matmul(dup, p[k], name=nm("d_" + k + "_x"), tb=True, out_dtype=bf16))
        g_ups.append(matmul(y, dup, name=nm("d_" + k + "_w"), ta=True, out_dtype=bf16))
    do, dz, d_gdn_norm = gdn_out_bwd(sv["o"], sv["proj"], p["gdn_norm"], dys[2], nm("d_gdno"))
    cots = gdn_b_bwd(*sv["ga"], sv["states"], do, nm("d_gdnb"))
    dc, dab, d_alog, d_dtb = gdn_a_bwd(sv["c"], sv["proj"], p["alog"], p["dtb"], cots[:5], cots[5], nm("d_gdna"))
    dgq, d_conv = conv_bwd(sv["proj"], p["conv"], dc, nm("d_conv"))
    dq, dk, dv = sb_bwd(sv["proj"], dys[1], nm("d_sb"))
    dp, d_pool_w, d_pool_scale = pool_bwd(sv["proj"], p["pool_w"], p["pool_scale"], dys[0], nm("d_pool"))
    dproj = jnp.concatenate([dgates, dq, dk, dv, dgq, dz, dp, dab, jnp.zeros((s, NP - OFF_AB - HD), bf16)], axis=1)
    g_in = matmul(sv["u"], dproj, name=nm("d_in_w"), ta=True, out_dtype=bf16)
    recv = send([g_in, g_ups[0], g_ups[1], g_ups[2], g_out, g_ff1, g_ff2])
    du = matmul(dproj, p["w_in"], name=nm("d_in_x"), tb=True)
    dx, dxb, d_attn_norm = rms_bwd(sv["x"], du, dx1, p["attn_norm"], nm("d_rms1"))
    small = [d_attn_norm, d_pool_w, d_pool_scale, d_conv, d_alog, d_dtb, d_gdn_norm, d_mlp_norm]
    return dx, dxb, recv, small


BIG_AXES = (0, 1, 1, 1, 0, 1, 0)
GATHER_ID, SWAP_ID, EXCHANGE_ID = 1, 2, 3
BIG_NAMES = ("w_in", "w_pool_up", "w_sb_up", "w_gdn_up", "w_out", "w_ff1", "w_ff2")


def kernel(x, attn_norm, w_in, pool_w, pool_scale, gdn_conv, gdn_a_log, gdn_dt_bias, gdn_norm, w_pool_up, w_sb_up, w_gdn_up, w_out, mlp_norm, w_ff1, w_ff2, final_norm, loss_target, m_attn_norm, m_w_in, m_pool_w, m_pool_scale, m_gdn_conv, m_gdn_a_log, m_gdn_dt_bias, m_gdn_norm, m_w_pool_up, m_w_sb_up, m_w_gdn_up, m_w_out, m_mlp_norm, m_w_ff1, m_w_ff2, m_final_norm, v_attn_norm, v_w_in, v_pool_w, v_pool_scale, v_gdn_conv, v_gdn_a_log, v_gdn_dt_bias, v_gdn_norm, v_w_pool_up, v_w_sb_up, v_w_gdn_up, v_w_out, v_mlp_norm, v_w_ff1, v_w_ff2, v_final_norm):
    s = x.shape[1]
    me = _dev_index((lax.axis_index("x"), lax.axis_index("y"), lax.axis_index("c")))
    ncv = gdn_conv.shape[2]

    full = []
    for l in range(NL):
        shards = [_pack_cols(w_in[l]).astype(bf16), w_pool_up[l].astype(bf16), w_sb_up[l].astype(bf16),
                  w_gdn_up[l].astype(bf16), w_out[l].astype(bf16), w_ff1[l].astype(bf16), w_ff2[l].astype(bf16)]
        if l == 0:
            first = all_gather([shards[0], gdn_conv.reshape(NL * 4, ncv)], (0, 0), "gather_first", sequencer_id=GATHER_ID)
            rest = all_gather(shards[1:], BIG_AXES[1:], "gather_rest_l0", sequencer_id=GATHER_ID)
            full.append([first[0]] + rest)
            conv_full = first[1].reshape(NDEV, NL, 4, ncv).transpose(1, 2, 0, 3).reshape(NL, 4, NDEV * ncv)
        else:
            full.append(all_gather(shards, BIG_AXES, f"gather_weights_l{l}", sequencer_id=GATHER_ID))
    params = []
    for l in range(NL):
        p = dict(zip(("w_in", "w_pool_up", "w_sb_up", "w_gdn_up", "w_out", "w_ff1", "w_ff2"), full[l][:7]))
        p.update(attn_norm=attn_norm[l][None], mlp_norm=mlp_norm[l][None], pool_w=pool_w[l], pool_scale=pool_scale[l][None],
                 conv=conv_full[l], alog=_lanes(gdn_a_log[l]), dtb=_lanes(gdn_dt_bias[l]), gdn_norm=gdn_norm[l][None])
        params.append(p)

    h = x[0]
    saved = []
    for l in range(NL):
        h, sv = _layer_fwd(h, params[l], l)
        saved.append(sv)
    dh, dhb, loss_row, d_final = loss_head(h, loss_target[0], final_norm[None], "loss_head")
    recv, smalls = [None] * NL, [None] * NL
    for l in reversed(range(NL)):
        def send(gs, l=l):
            ps = pair_swap(gs, BIG_AXES, f"swap_grads_l{l}", SWAP_ID)
            qs = [pair_add(g, p, ax, f"pair_add_{k}_l{l}") for g, p, ax, k in zip(gs, ps, BIG_AXES, BIG_NAMES)]
            return chip_exchange(qs, f"exchange_grads_l{l}", EXCHANGE_ID)

        dh, dhb, recv[l], smalls[l] = _layer_bwd(dh, dhb, saved[l], params[l], l, send)

    small_rows = [_lanes8(t) for l in range(NL) for t in smalls[l]] + [_lanes8(d_final), _lanes8(loss_row)]
    packed = jnp.concatenate(small_rows, axis=0)
    parts = all_gather([packed], (0,), "gather_small")[0].reshape(NDEV, packed.shape[0], HD)

    def pack_small(tree):
        rows = []
        for l in range(NL):
            rows += [_lanes8(tree["attn_norm"][l]), _lanes8(tree["pool_w"][l]), _lanes8(tree["pool_scale"][l]),
                     jnp.zeros((4 * NDEV * ncv // HD, HD), f32), _lanes8(tree["gdn_a_log"][l]), _lanes8(tree["gdn_dt_bias"][l]),
                     _lanes8(tree["gdn_norm"][l]), _lanes8(tree["mlp_norm"][l])]
        rows += [_lanes8(tree["final_norm"]), jnp.zeros((8, HD), f32)]
        return jnp.concatenate(rows, axis=0)

    names = ("attn_norm", "pool_w", "pool_scale", "gdn_a_log", "gdn_dt_bias", "gdn_norm", "mlp_norm", "final_norm")
    w_small = pack_small(dict(zip(names, (attn_norm, pool_w, pool_scale, gdn_a_log, gdn_dt_bias, gdn_norm, mlp_norm, final_norm))))
    m_small = pack_small(dict(zip(names, (m_attn_norm, m_pool_w, m_pool_scale, m_gdn_a_log, m_gdn_dt_bias, m_gdn_norm, m_mlp_norm, m_final_norm))))
    v_small = pack_small(dict(zip(names, (v_attn_norm, v_pool_w, v_pool_scale, v_gdn_a_log, v_gdn_dt_bias, v_gdn_norm, v_mlp_norm, v_final_norm))))
    small_out = small_adamw(parts, w_small, m_small, v_small, "adamw_small")

    def unpack_small(buf):
        out, conv_g, r = {}, [], 0
        layer_items = (("attn_norm", (D,)), ("pool_w", (4, HD, HD)), ("pool_scale", (W_POOL,)), ("conv", (4, NDEV * ncv)),
                       ("gdn_a_log", (NH,)), ("gdn_dt_bias", (NH,)), ("gdn_norm", (HD,)), ("mlp_norm", (D,)))
        per_layer = {k: [] for k, _ in layer_items}
        for l in range(NL):
            for k, shape in layer_items:
                size = math.prod(shape)
                nrow = -(-size // (8 * HD)) * 8
                per_layer[k].append(buf[r:r + nrow].reshape(-1)[:size].reshape(shape))
                r += nrow
        for k, _ in layer_items:
            out[k] = jnp.stack(per_layer[k])
        out["final_norm"] = buf[r:r + D // HD].reshape(D)
        out["loss"] = buf[r + D // HD, 0]
        return out

    sm = [unpack_small(b) for b in small_out]
    loss = sm[0]["loss"]
    g_conv = lax.dynamic_slice_in_dim(sm[0]["conv"], me * ncv, ncv, axis=2)
    conv_out = None
    for l in reversed(range(NL)):
        conv_out = adamw(gdn_conv, m_gdn_conv, v_gdn_conv, l, f"adamw_conv_l{l}", g=g_conv[l], prev=conv_out)

    big_out = {}
    big_names = BIG_NAMES
    big_w = dict(zip(big_names, ((w_in, m_w_in, v_w_in), (w_pool_up, m_w_pool_up, v_w_pool_up), (w_sb_up, m_w_sb_up, v_w_sb_up),
                                 (w_gdn_up, m_w_gdn_up, v_w_gdn_up), (w_out, m_w_out, v_w_out), (w_ff1, m_w_ff1, v_w_ff1),
                                 (w_ff2, m_w_ff2, v_w_ff2))))
    for l in reversed(range(NL)):
        for t, k in enumerate(big_names):
            w, m, v = big_w[k]
            if k == "w_in":
                g = _unpack_cols(sum_partials(recv[l][t], f"sum_w_in_l{l}"))
                big_out[k] = adamw(w, m, v, l, f"adamw_{k}_l{l}", g=g, prev=big_out.get(k))
            else:
                big_out[k] = adamw(w, m, v, l, f"adamw_{k}_l{l}", r=recv[l][t], prev=big_out.get(k))

    def leaf(i, k):
        if k in big_out:
            return big_out[k][i]
        if k == "gdn_conv":
            return conv_out[i]
        return sm[i][k]

    order = ("attn_norm", "w_in", "pool_w", "pool_scale", "gdn_conv", "gdn_a_log", "gdn_dt_bias", "gdn_norm", "w_pool_up",
             "w_sb_up", "w_gdn_up", "w_out", "mlp_norm", "w_ff1", "w_ff2", "final_norm")
    return (loss, dh[None]) + tuple(leaf(i, k) for i in range(4) for k in order)
```

```python
import functools
import math

import jax
import jax.numpy as jnp
from jax import lax
from jax.experimental import pallas as pl
from jax.experimental.pallas import tpu as pltpu
from jax.experimental.pallas import tpu_sc as plsc

f32, bf16 = jnp.float32, jnp.bfloat16

D = 2048
NDEV = 8
NL = 2
HD = 128
NH = 6
WH = NH * HD
W_POOL = 512
EPS = 1e-6
N_IN = 12044
NP = 12288
OFF_G, OFF_SB, OFF_GQ, OFF_Z, OFF_P, OFF_AB = 0, 6144, 8448, 10752, 11520, 12032
POOL_WINDOWS = (2, 4, 8, 16)
CH = 128
TQ = 256
VMEM_LIMIT = 56 * 1024 * 1024
ADAM_LR, ADAM_B1, ADAM_B2, ADAM_EPS, ADAM_WD, ADAM_STEP = 0.001, 0.9, 0.999, 1e-08, 0.01, 10
MESH = pl.DeviceIdType.MESH


def _cp(sem=None):
    return pltpu.CompilerParams(dimension_semantics=sem, vmem_limit_bytes=VMEM_LIMIT)


def _sds(shape, dtype):
    return jax.ShapeDtypeStruct(tuple(shape), dtype)


def matmul(a, b, *, name, ta=False, tb=False, out_dtype=f32, tm=1024, tn=1024, tk=2048, epilogue=None, extras=(), after=()):
    m, k = (a.shape[1], a.shape[0]) if ta else a.shape
    n = b.shape[0] if tb else b.shape[1]
    assert k == (b.shape[1] if tb else b.shape[0]) and a.dtype == bf16 and b.dtype == bf16
    tm, tn, tk = min(tm, m), min(tn, n), min(tk, k)
    assert m % tm == 0 and n % tn == 0 and k % tk == 0, (m, n, k, tm, tn, tk)
    nk = k // tk
    a_spec = pl.BlockSpec((tk, tm), lambda i, j, q: (q, i)) if ta else pl.BlockSpec((tm, tk), lambda i, j, q: (i, q))
    b_spec = pl.BlockSpec((tn, tk), lambda i, j, q: (j, q)) if tb else pl.BlockSpec((tk, tn), lambda i, j, q: (q, j))
    e_specs = [pl.BlockSpec((tm, tn), lambda i, j, q: (i, j)) for _ in extras]
    dn = (((0 if ta else 1,), (1 if tb else 0,)), ((), ()))
    ne = len(extras)

    def body(*refs):
        a_ref, b_ref, *rest = refs[len(after):]
        e_refs, o_ref = rest[:ne], rest[ne]
        part = lax.dot_general(a_ref[...], b_ref[...], dn, preferred_element_type=f32)

        def finish(acc):
            if epilogue is not None:
                acc = epilogue(acc, *[e[...] for e in e_refs])
            o_ref[...] = acc.astype(out_dtype)

        if nk == 1:
            finish(part)
        else:
            acc_ref = rest[ne + 1]
            q = pl.program_id(2)

            @pl.when(q == 0)
            def _():
                acc_ref[...] = part

            @pl.when(jnp.logical_and(q > 0, q < nk - 1))
            def _():
                acc_ref[...] += part

            @pl.when(q == nk - 1)
            def _():
                finish(acc_ref[...] + part)

    return pl.pallas_call(
        body, name=name, grid=(m // tm, n // tn, nk),
        in_specs=[pl.BlockSpec(memory_space=pl.ANY)] * len(after) + [a_spec, b_spec] + e_specs,
        out_specs=pl.BlockSpec((tm, tn), lambda i, j, q: (i, j)), out_shape=_sds((m, n), out_dtype),
        scratch_shapes=[pltpu.VMEM((tm, tn), f32)] if nk > 1 else [],
        compiler_params=_cp(("parallel", "parallel", "arbitrary")),
    )(*after, a, b, *extras)


def rowwise(name, fn, rows, params, outs, sums=(), tr=256, after=()):
    s = rows[0][0].shape[0]
    tr = min(tr, s)
    nin, nout = len(rows) + len(params), len(outs)
    in_specs = [pl.BlockSpec((tr, w), functools.partial(lambda i, c: (i, c), c=c)) for (_, w, c) in rows]
    in_specs += [pl.BlockSpec(p.shape, lambda i: (0, 0)) for p in params]
    out_specs = [pl.BlockSpec((tr, w), lambda i: (i, 0)) for (w, _) in outs]
    out_specs += [pl.BlockSpec(sh, lambda i: (0, 0)) for sh in sums]
    out_shape = [_sds((s, w), dt) for (w, dt) in outs] + [_sds(sh, f32) for sh in sums]

    def body(*refs):
        refs = refs[len(after):]
        res = fn(*[r[...] for r in refs[:nin]])
        for r, v in zip(refs[nin:nin + nout], res[:nout]):
            r[...] = v.astype(r.dtype)
        i = pl.program_id(0)
        for r, v in zip(refs[nin + nout:], res[nout:]):
            @pl.when(i == 0)
            def _(r=r, v=v):
                r[...] = v

            @pl.when(i > 0)
            def _(r=r, v=v):
                r[...] += v

    res = pl.pallas_call(
        body, name=name, grid=(s // tr,), in_specs=[pl.BlockSpec(memory_space=pl.ANY)] * len(after) + in_specs,
        out_specs=out_specs, out_shape=out_shape, compiler_params=_cp(("arbitrary",)),
    )(*after, *[r[0] for r in rows], *params)
    return res


def _rms(x, g):
    return x * lax.rsqrt(jnp.mean(x * x, axis=-1, keepdims=True) + EPS) * g


def rms_fwd(x, g, name):
    return rowwise(name, lambda xb, gb: (_rms(xb, gb),), [(x, D, 0)], [g], [(D, bf16)])[0]


def rms_bwd(x, du, dres, g, name, after=()):
    def fn(xb, dub, drb, gb):
        _, vjp = jax.vjp(_rms, xb, gb)
        dx, dg = vjp(dub.astype(f32))
        return drb + dx, drb + dx, dg

    return rowwise(name, fn, [(x, D, 0), (du, D, 0), (dres, D, 0)], [g], [(D, f32), (D, bf16)], [(1, D)], after=after)


def _merge(gates, up_p, up_s, up_g):
    sg = jax.nn.sigmoid(gates)
    return sg[:, :D] * up_p + sg[:, D:2 * D] * up_s + sg[:, 2 * D:] * up_g


def merge_fwd(proj, ups, name):
    return rowwise(name, lambda g, a, b, c: (_merge(g, a, b, c),),
                   [(proj, 3 * D, 0)] + [(u, D, 0) for u in ups], [], [(D, bf16)], tr=128)[0]


def merge_bwd(proj, ups, dmerged, name):
    def fn(g, a, b, c, dm):
        _, vjp = jax.vjp(_merge, g, a, b, c)
        return vjp(dm.astype(f32))

    return rowwise(name, fn, [(proj, 3 * D, 0)] + [(u, D, 0) for u in ups] + [(dmerged, D, 0)], [],
                   [(3 * D, bf16), (D, bf16), (D, bf16), (D, bf16)], tr=128)


def _gdn_out(o, z, g):
    ys = []
    for h in range(NH):
        sl = slice(h * HD, (h + 1) * HD)
        ys.append(_rms(o[:, sl], g) * jax.nn.silu(z[:, sl]))
    return jnp.concatenate(ys, axis=1)


def gdn_out_fwd(o, proj, g, name):
    return rowwise(name, lambda ob, zb, gb: (_gdn_out(ob, zb, gb),), [(o, WH, 0), (proj, WH, OFF_Z // WH)], [g],
                   [(WH, bf16)])[0]


def gdn_out_bwd(o, proj, g, dy, name):
    def fn(ob, zb, dyb, gb):
        _, vjp = jax.vjp(_gdn_out, ob, zb, gb)
        return vjp(dyb.astype(f32))

    return rowwise(name, fn, [(o, WH, 0), (proj, WH, OFF_Z // WH), (dy, WH, 0)], [g], [(WH, f32), (WH, bf16)],
                   [(1, HD)])


def loss_head(x, target, g, name):
    def loss_fn(xb, gb, tb):
        err = _rms(xb, gb) - tb
        return (0.5 / D) * jnp.sum(jnp.sum(err * err, axis=1, keepdims=True), axis=0, keepdims=True)

    def fn(xb, tb, gb):
        val, vjp = jax.vjp(functools.partial(loss_fn, tb=tb), xb, gb)
        dx, dg = vjp(jnp.ones((1, 1), f32))
        return dx, dx, jnp.broadcast_to(val, (1, HD)), dg

    return rowwise(name, fn, [(x, D, 0), (target, D, 0)], [g], [(D, f32), (D, bf16)], [(1, HD), (1, D)])


PB = 256


def _split(v):
    hi = v.astype(bf16)
    return hi, (v - hi.astype(f32)).astype(bf16)


def _band_dot(make_band, v, s, forward):
    hi, lo = _split(v)
    nb = s // PB
    outs = []
    for r in range(nb):
        lo_r = max(r - 1, 0) if forward else r
        hi_r = r + 1 if forward else min(r + 2, nb)
        band = make_band(r * PB, lo_r * PB, (hi_r - lo_r) * PB)
        sl = slice(lo_r * PB, hi_r * PB)
        outs.append(jnp.dot(band, hi[sl], preferred_element_type=f32) + jnp.dot(band, lo[sl], preferred_element_type=f32))
    return jnp.concatenate(outs, axis=0)


def _pool_common(p, win, s):
    def band(row0, col0, ncol):
        t = row0 + lax.broadcasted_iota(jnp.int32, (PB, ncol), 0)
        u = col0 + lax.broadcasted_iota(jnp.int32, (PB, ncol), 1)
        return jnp.logical_and(u <= t, t < u + win).astype(bf16)

    def band_t(row0, col0, ncol):
        u = row0 + lax.broadcasted_iota(jnp.int32, (PB, ncol), 0)
        t = col0 + lax.broadcasted_iota(jnp.int32, (PB, ncol), 1)
        return jnp.logical_and(u <= t, t < u + win).astype(bf16)

    t = lax.broadcasted_iota(jnp.int32, (s, 1), 0)
    inv_n = 1.0 / jnp.minimum(t + 1, win).astype(f32)
    d = _band_dot(band, p, s, True) * inv_n - p
    return d, inv_n, band_t


def pool_fwd(proj, pool_w, pool_scale, name):
    s = proj.shape[0]

    def body(p_ref, w_ref, sc_ref, y_ref):
        win = jnp.left_shift(2, pl.program_id(0))
        d, _, _ = _pool_common(p_ref[...], win, s)
        y = jnp.dot(d.astype(bf16), w_ref[...].astype(bf16), preferred_element_type=f32) * sc_ref[...]
        y_ref[...] = y.astype(bf16)

    return pl.pallas_call(
        body, name=name, grid=(4,),
        in_specs=[pl.BlockSpec((s, HD), lambda g: (0, OFF_P // HD + g)), pl.BlockSpec((None, HD, HD), lambda g: (g, 0, 0)),
                  pl.BlockSpec((1, HD), lambda g: (0, g))],
        out_specs=pl.BlockSpec((s, HD), lambda g: (0, g)), out_shape=_sds((s, W_POOL), bf16),
        compiler_params=_cp(("arbitrary",)),
    )(proj, pool_w, pool_scale)


def pool_bwd(proj, pool_w, pool_scale, dy, name):
    s = proj.shape[0]

    def body(p_ref, w_ref, sc_ref, dy_ref, dp_ref, dw_ref, dsc_ref):
        win = jnp.left_shift(2, pl.program_id(0))
        d, inv_n, band_t = _pool_common(p_ref[...], win, s)
        w = w_ref[...].astype(bf16)
        dyf = dy_ref[...].astype(f32)
        dsc_ref[...] = jnp.sum(dyf * jnp.dot(d.astype(bf16), w, preferred_element_type=f32), axis=0, keepdims=True)
        dys = (dyf * sc_ref[...]).astype(bf16)
        dd = lax.dot_general(dys, w, (((1,), (1,)), ((), ())), preferred_element_type=f32)
        dw_ref[...] = lax.dot_general(d.astype(bf16), dys, (((0,), (0,)), ((), ())), preferred_element_type=f32)
        dp_ref[...] = (_band_dot(band_t, dd * inv_n, s, False) - dd).astype(bf16)

    return pl.pallas_call(
        body, name=name, grid=(4,),
        in_specs=[pl.BlockSpec((s, HD), lambda g: (0, OFF_P // HD + g)), pl.BlockSpec((None, HD, HD), lambda g: (g, 0, 0)),
                  pl.BlockSpec((1, HD), lambda g: (0, g)), pl.BlockSpec((s, HD), lambda g: (0, g))],
        out_specs=[pl.BlockSpec((s, HD), lambda g: (0, g)), pl.BlockSpec((None, HD, HD), lambda g: (g, 0, 0)),
                   pl.BlockSpec((1, HD), lambda g: (0, g))],
        out_shape=[_sds((s, W_POOL), bf16), _sds((4, HD, HD), f32), _sds((1, W_POOL), f32)],
        compiler_params=_cp(("arbitrary",)),
    )(proj, pool_w, pool_scale, dy)


def _sb_tile(q, k_ref, kb, qi, carry):
    k = k_ref[pl.ds(pl.multiple_of(kb * TQ, TQ), TQ), :].astype(bf16)
    z = lax.dot_general(q, k, (((1,), (1,)), ((), ())), preferred_element_type=f32)
    row = qi * TQ + lax.broadcasted_iota(jnp.int32, (TQ, TQ), 0)
    col = kb * TQ + lax.broadcasted_iota(jnp.int32, (TQ, TQ), 1)
    mask = col < row
    ls = jnp.where(mask, jax.nn.log_sigmoid(-z), 0.0)
    j = lax.broadcasted_iota(jnp.int32, (TQ, TQ), 0)
    u = lax.broadcasted_iota(jnp.int32, (TQ, TQ), 1)
    later = (j > u).astype(bf16)
    hi, lo = _split(ls)
    lw = jnp.dot(hi, later, preferred_element_type=f32) + jnp.dot(lo, later, preferred_element_type=f32)
    a = jnp.where(mask, jnp.exp(ls + z + lw + carry), 0.0)
    return z, mask, ls, a


def sb_fwd(proj, name):
    s = proj.shape[0]
    nq = s // TQ
    scale = HD ** -0.5

    def body(q_ref, k_ref, v_ref, y_ref):
        qi = pl.program_id(1)
        q = (q_ref[...] * scale).astype(bf16)

        def step(j, c):
            acc, carry = c
            kb = qi - j
            _, _, ls, a = _sb_tile(q, k_ref, kb, qi, carry)
            v = v_ref[pl.ds(pl.multiple_of(kb * TQ, TQ), TQ), :].astype(bf16)
            acc = acc + jnp.dot(a.astype(bf16), v, preferred_element_type=f32)
            return acc, carry + jnp.sum(ls, axis=1, keepdims=True)

        acc, _ = lax.fori_loop(0, qi + 1, step, (jnp.zeros((TQ, HD), f32), jnp.zeros((TQ, 1), f32)))
        y_ref[...] = acc.astype(bf16)

    c0 = OFF_SB // HD
    return pl.pallas_call(
        body, name=name, grid=(NH, nq),
        in_specs=[pl.BlockSpec((TQ, HD), lambda h, i: (i, c0 + h)), pl.BlockSpec((s, HD), lambda h, i: (0, c0 + NH + h)),
                  pl.BlockSpec((s, HD), lambda h, i: (0, c0 + 2 * NH + h))],
        out_specs=pl.BlockSpec((TQ, HD), lambda h, i: (i, h)), out_shape=_sds((s, WH), bf16),
        compiler_params=_cp(("arbitrary", "arbitrary")),
    )(proj, proj, proj)


def sb_bwd(proj, dy, name):
    s = proj.shape[0]
    nq = s // TQ
    scale = HD ** -0.5

    def body(q_ref, k_ref, v_ref, do_ref, dq_ref, dk_ref, dv_ref, e_scr, z_scr, dk_acc, dv_acc):
        qi = pl.program_id(1)
        q = (q_ref[...] * scale).astype(bf16)
        do = do_ref[...]

        @pl.when(qi == 0)
        def _():
            dk_acc[...] = jnp.zeros_like(dk_acc)
            dv_acc[...] = jnp.zeros_like(dv_acc)

        def sweep_back(j, carry):
            kb = qi - j
            rows = pl.ds(pl.multiple_of(kb * TQ, TQ), TQ)
            z, _, ls, a = _sb_tile(q, k_ref, kb, qi, carry)
            v = v_ref[rows, :].astype(bf16)
            da = lax.dot_general(do, v, (((1,), (1,)), ((), ())), preferred_element_type=f32)
            e_scr[kb] = da * a
            z_scr[kb] = z
            dv_acc[rows, :] += lax.dot_general(a.astype(bf16), do, (((0,), (0,)), ((), ())), preferred_element_type=f32)
            return carry + jnp.sum(ls, axis=1, keepdims=True)

        lax.fori_loop(0, qi + 1, sweep_back, jnp.zeros((TQ, 1), f32))

        def sweep_fwd(kb, c):
            dq, carry = c
            rows = pl.ds(pl.multiple_of(kb * TQ, TQ), TQ)
            e, z = e_scr[kb], z_scr[kb]
            row = qi * TQ + lax.broadcasted_iota(jnp.int32, (TQ, TQ), 0)
            col = kb * TQ + lax.broadcasted_iota(jnp.int32, (TQ, TQ), 1)
            j = lax.broadcasted_iota(jnp.int32, (TQ, TQ), 0)
            u = lax.broadcasted_iota(jnp.int32, (TQ, TQ), 1)
            earlier = (j < u).astype(bf16)
            hi, lo = _split(e)
            ew = jnp.dot(hi, earlier, preferred_element_type=f32) + jnp.dot(lo, earlier, preferred_element_type=f32)
            sig = jax.nn.sigmoid(z)
            dz = jnp.where(col < row, e * (1.0 - sig) - (ew + carry) * sig, 0.0).astype(bf16)
            k = k_ref[rows, :].astype(bf16)
            dq = dq + jnp.dot(dz, k, preferred_element_type=f32)
            dk_acc[rows, :] += lax.dot_general(dz, q, (((0,), (0,)), ((), ())), preferred_element_type=f32)
            return dq, carry + jnp.sum(e, axis=1, keepdims=True)

        dq, _ = lax.fori_loop(0, qi + 1, sweep_fwd, (jnp.zeros((TQ, HD), f32), jnp.zeros((TQ, 1), f32)))
        dq_ref[...] = (dq * scale).astype(bf16)

        @pl.when(qi == nq - 1)
        def _():
            dk_ref[...] = dk_acc[...].astype(bf16)
            dv_ref[...] = dv_acc[...].astype(bf16)

    c0 = OFF_SB // HD
    return pl.pallas_call(
        body, name=name, grid=(NH, nq),
        in_specs=[pl.BlockSpec((TQ, HD), lambda h, i: (i, c0 + h)), pl.BlockSpec((s, HD), lambda h, i: (0, c0 + NH + h)),
                  pl.BlockSpec((s, HD), lambda h, i: (0, c0 + 2 * NH + h)), pl.BlockSpec((TQ, HD), lambda h, i: (i, h))],
        out_specs=[pl.BlockSpec((TQ, HD), lambda h, i: (i, h)), pl.BlockSpec((s, HD), lambda h, i: (0, h)),
                   pl.BlockSpec((s, HD), lambda h, i: (0, h))],
        out_shape=[_sds((s, WH), bf16)] * 3,
        scratch_shapes=[pltpu.VMEM((nq, TQ, TQ), f32), pltpu.VMEM((nq, TQ, TQ), f32), pltpu.VMEM((s, HD), f32),
                        pltpu.VMEM((s, HD), f32)],
        compiler_params=_cp(("arbitrary", "arbitrary")),
    )(proj, proj, proj, dy)


CB = 256


def _shift_down(v, k, s):
    if k == 0:
        return v
    row = lax.broadcasted_iota(jnp.int32, v.shape, 0)
    return jnp.where(row < k, 0.0, pltpu.roll(v, k, axis=0))


def _shift_up(v, k, s):
    if k == 0:
        return v
    row = lax.broadcasted_iota(jnp.int32, v.shape, 0)
    return jnp.where(row >= s - k, 0.0, pltpu.roll(v, s - k, axis=0))


def conv_fwd(proj, w, name):
    s = proj.shape[0]

    def body(x_ref, w_ref, y_ref):
        x, wv = x_ref[...], w_ref[...]
        y = sum(wv[3 - k:4 - k, :] * _shift_down(x, k, s) for k in range(4))
        y_ref[...] = jax.nn.silu(y)

    return pl.pallas_call(
        body, name=name, grid=(3 * WH // CB,),
        in_specs=[pl.BlockSpec((s, CB), lambda j: (0, OFF_GQ // CB + j)), pl.BlockSpec((4, CB), lambda j: (0, j))],
        out_specs=pl.BlockSpec((s, CB), lambda j: (0, j)), out_shape=_sds((s, 3 * WH), f32),
        compiler_params=_cp(("parallel",)),
    )(proj, w)


def conv_bwd(proj, w, dc, name, after=()):
    s = proj.shape[0]

    def body(*refs):
        x_ref, w_ref, dc_ref, dx_ref, dw_ref = refs[len(after):]
        x, wv = x_ref[...], w_ref[...]
        xs = [_shift_down(x, k, s) for k in range(4)]
        y = sum(wv[3 - k:4 - k, :] * xs[k] for k in range(4))
        sig = jax.nn.sigmoid(y)
        dy = dc_ref[...] * (sig * (1.0 + y * (1.0 - sig)))
        dx_ref[...] = sum(wv[3 - k:4 - k, :] * _shift_up(dy, k, s) for k in range(4)).astype(bf16)
        dw_ref[...] = jnp.concatenate([jnp.sum(dy * xs[3 - i], axis=0, keepdims=True) for i in range(4)], axis=0)

    return pl.pallas_call(
        body, name=name, grid=(3 * WH // CB,),
        in_specs=[pl.BlockSpec(memory_space=pl.ANY)] * len(after)
        + [pl.BlockSpec((s, CB), lambda j: (0, OFF_GQ // CB + j)), pl.BlockSpec((4, CB), lambda j: (0, j)),
           pl.BlockSpec((s, CB), lambda j: (0, j))],
        out_specs=[pl.BlockSpec((s, CB), lambda j: (0, j)), pl.BlockSpec((4, CB), lambda j: (0, j))],
        out_shape=[_sds((s, 3 * WH), bf16), _sds((4, 3 * WH), f32)],
        compiler_params=_cp(("parallel",)),
    )(*after, proj, w, dc)


SOLVE_PASSES = 3
NN, NT, TN = (((1,), (0,)), ((), ())), (((1,), (1,)), ((), ())), (((0,), (0,)), ((), ()))


def _pdot_impl(a, b, dn, passes):
    ah, al = _split(a)
    bh, bl = _split(b)
    dot = lambda p, q: lax.dot_general(p, q, dn, preferred_element_type=f32)
    if passes == 1:
        return dot(ah, bh)
    if passes == 2:
        return dot(ah, bh) + dot(ah, bl)
    return dot(ah, bh) + (dot(ah, bl) + dot(al, bh))


@functools.partial(jax.custom_vjp, nondiff_argnums=(2,))
def _pdot(a, b, passes):
    return _pdot_impl(a, b, NN, passes)


def _pdot_fwd(a, b, passes):
    return _pdot_impl(a, b, NN, passes), (a, b)


def _pdot_bwd(passes, res, ct):
    a, b = res
    return _pdot_impl(ct, b, NT, passes), _pdot_impl(a, ct, TN, passes)


_pdot.defvjp(_pdot_fwd, _pdot_bwd)


@functools.partial(jax.custom_vjp, nondiff_argnums=(2,))
def _pdot_nt(a, b, passes):
    return _pdot_impl(a, b, NT, passes)


def _pdot_nt_fwd(a, b, passes):
    return _pdot_impl(a, b, NT, passes), (a, b)


def _pdot_nt_bwd(passes, res, ct):
    a, b = res
    return _pdot_impl(ct, b, NN, passes), _pdot_impl(ct, a, TN, passes)


_pdot_nt.defvjp(_pdot_nt_fwd, _pdot_nt_bwd)


def _lane_pick(v, h):
    lane = lax.broadcasted_iota(jnp.int32, v.shape, v.ndim - 1)
    return jnp.sum(jnp.where(lane == h, v, 0.0), axis=-1, keepdims=True)


def _l2n(v):
    return v * lax.rsqrt(jnp.sum(v * v, axis=-1, keepdims=True) + EPS)


def _dot_nt(a, b):
    return lax.dot_general(a, b, (((1,), (1,)), ((), ())), preferred_element_type=f32)


def _gdn_chunk(cq, ck, cv, ab, alog, dtb, h):
    ones = jnp.ones((CH, HD), f32)
    q = _l2n(cq) * (HD ** -0.5)
    k = _l2n(ck)
    la = -jnp.exp(_lane_pick(alog, h)) * jax.nn.softplus(_lane_pick(ab, h) + _lane_pick(dtb, h))
    beta = jax.nn.sigmoid(_lane_pick(ab, NH + h)) * ones
    i = lax.broadcasted_iota(jnp.int32, (CH, CH), 0)
    j = lax.broadcasted_iota(jnp.int32, (CH, CH), 1)
    incl, strict = j <= i, j < i
    g = _pdot(incl.astype(f32), la * ones, 2)
    diff = g - g.T
    gamma = jnp.where(incl, jnp.exp(jnp.where(incl, diff, 0.0)), 0.0)
    lower = jnp.where(strict, beta * _pdot_nt(k, k, 1) * gamma, 0.0)
    eye = (i == j).astype(f32)
    inv = eye - lower
    pw = _pdot(lower, lower, SOLVE_PASSES)
    for m in range(1, int(math.log2(CH))):
        inv = inv + _pdot(inv, pw, SOLVE_PASSES)
        if m < int(math.log2(CH)) - 1:
            pw = _pdot(pw, pw, SOLVE_PASSES)
    eg = jnp.exp(g)
    u = _pdot(inv, cv * beta, SOLVE_PASSES)
    w = _pdot(inv, k * (beta * eg), SOLVE_PASSES)
    qk = _pdot_nt(q, k, 1) * gamma
    g_last = g[CH - 1:CH, :]
    return u, w, q * eg, k * jnp.exp(g_last - g), qk, jnp.exp(g_last)


def gdn_a_fwd(c, proj, alog, dtb, name):
    s = c.shape[0]
    nc = s // CH

    def body(c_ref, ab_ref, al_ref, dt_ref, u_ref, w_ref, qd_ref, kd_ref, qk_ref, dec_ref):
        for h in range(NH):
            sl = slice(h * HD, (h + 1) * HD)
            cq, ck, cv = (c_ref[:, t * WH + h * HD:t * WH + (h + 1) * HD] for t in range(3))
            res = _gdn_chunk(cq, ck, cv, ab_ref[...], al_ref[...], dt_ref[...], h)
            for r, v in zip((u_ref, w_ref, qd_ref, kd_ref, qk_ref), res[:5]):
                r[:, sl] = v
            dec_ref[:, sl] = jnp.broadcast_to(res[5], (8, HD))

    row = pl.BlockSpec((CH, WH), lambda n: (n, 0))
    par = pl.BlockSpec((1, HD), lambda n: (0, 0))
    return pl.pallas_call(
        body, name=name, grid=(nc,),
        in_specs=[pl.BlockSpec((CH, 3 * WH), lambda n: (n, 0)), pl.BlockSpec((CH, HD), lambda n: (n, OFF_AB // HD)), par, par],
        out_specs=[row] * 5 + [pl.BlockSpec((None, 8, WH), lambda n: (n, 0, 0))],
        out_shape=[_sds((s, WH), f32)] * 5 + [_sds((nc, 8, WH), f32)],
        compiler_params=_cp(("parallel",)),
    )(c, proj, alog, dtb)


def gdn_a_bwd(c, proj, alog, dtb, cots, ddec, name):
    s = c.shape[0]
    nc = s // CH

    def body(c_ref, ab_ref, al_ref, dt_ref, du_ref, dw_ref, dqd_ref, dkd_ref, dqk_ref, ddec_ref,
             dc_ref, dab_ref, dal_ref, ddt_ref):
        n = pl.program_id(0)
        dab, dal, ddt = jnp.zeros((CH, HD), f32), jnp.zeros((1, HD), f32), jnp.zeros((1, HD), f32)
        for h in range(NH):
            sl = slice(h * HD, (h + 1) * HD)
            cq, ck, cv = (c_ref[:, t * WH + h * HD:t * WH + (h + 1) * HD] for t in range(3))
            _, vjp = jax.vjp(functools.partial(_gdn_chunk, h=h), cq, ck, cv, ab_ref[...], al_ref[...], dt_ref[...])
            lane = lax.broadcasted_iota(jnp.int32, (1, HD), 1)
            dd = jnp.where(lane == 0, ddec_ref[0:1, sl], 0.0)
            dcs = vjp((du_ref[:, sl], dw_ref[:, sl], dqd_ref[:, sl], dkd_ref[:, sl], dqk_ref[:, sl], dd))
            for t in range(3):
                dc_ref[:, t * WH + h * HD:t * WH + (h + 1) * HD] = dcs[t]
            dab, dal, ddt = dab + dcs[3], dal + dcs[4], ddt + dcs[5]
        dab_ref[...] = dab.astype(bf16)

        @pl.when(n == 0)
        def _():
            dal_ref[...] = dal
            ddt_ref[...] = ddt

        @pl.when(n > 0)
        def _():
            dal_ref[...] += dal
            ddt_ref[...] += ddt

    row = pl.BlockSpec((CH, WH), lambda n: (n, 0))
    wide = pl.BlockSpec((CH, 3 * WH), lambda n: (n, 0))
    par = pl.BlockSpec((1, HD), lambda n: (0, 0))
    return pl.pallas_call(
        body, name=name, grid=(nc,),
        in_specs=[wide, pl.BlockSpec((CH, HD), lambda n: (n, OFF_AB // HD)), par, par] + [row] * 5
        + [pl.BlockSpec((None, 8, WH), lambda n: (n, 0, 0))],
        out_specs=[wide, pl.BlockSpec((CH, HD), lambda n: (n, 0)), par, par],
        out_shape=[_sds((s, 3 * WH), f32), _sds((s, HD), bf16), _sds((1, HD), f32), _sds((1, HD), f32)],
        compiler_params=_cp(("arbitrary",)),
    )(c, proj, alog, dtb, *cots, ddec)


def gdn_b_fwd(u, w, qd, kd, qk, dec, name):
    s = u.shape[0]
    nc = s // CH

    def body(u_ref, w_ref, qd_ref, kd_ref, qk_ref, dec_ref, o_ref, st_ref, state):
        n = pl.program_id(0)

        @pl.when(n == 0)
        def _():
            state[...] = jnp.zeros_like(state)

        for h in range(NH):
            sl = slice(h * HD, (h + 1) * HD)
            st = state[sl, :]
            st_ref[sl, :] = st
            sb = st.astype(bf16)
            vn = u_ref[:, sl] - jnp.dot(w_ref[:, sl].astype(bf16), sb, preferred_element_type=f32)
            vb = vn.astype(bf16)
            o_ref[:, sl] = (jnp.dot(qd_ref[:, sl].astype(bf16), sb, preferred_element_type=f32)
                            + jnp.dot(qk_ref[:, sl].astype(bf16), vb, preferred_element_type=f32))
            state[sl, :] = st * dec_ref[0:1, sl] + lax.dot_general(
                kd_ref[:, sl].astype(bf16), vb, (((0,), (0,)), ((), ())), preferred_element_type=f32)

    row = pl.BlockSpec((CH, WH), lambda n: (n, 0))
    return pl.pallas_call(
        body, name=name, grid=(nc,),
        in_specs=[row] * 5 + [pl.BlockSpec((None, 8, WH), lambda n: (n, 0, 0))],
        out_specs=[row, pl.BlockSpec((None, WH, HD), lambda n: (n, 0, 0))],
        out_shape=[_sds((s, WH), f32), _sds((nc, WH, HD), f32)],
        scratch_shapes=[pltpu.VMEM((WH, HD), f32)],
        compiler_params=_cp(("arbitrary",)),
    )(u, w, qd, kd, qk, dec)


def gdn_b_bwd(u, w, qd, kd, qk, dec, states, do, name):
    s = u.shape[0]
    nc = s // CH

    def body(u_ref, w_ref, qd_ref, kd_ref, qk_ref, dec_ref, st_ref, do_ref,
             du_ref, dw_ref, dqd_ref, dkd_ref, dqk_ref, ddec_ref, dstate):
        n = pl.program_id(0)

        @pl.when(n == 0)
        def _():
            dstate[...] = jnp.zeros_like(dstate)

        for h in range(NH):
            sl = slice(h * HD, (h + 1) * HD)
            st, ds = st_ref[sl, :], dstate[sl, :]
            sb, dsb = st.astype(bf16), ds.astype(bf16)
            wb, qdb, kdb, qkb = (r[:, sl].astype(bf16) for r in (w_ref, qd_ref, kd_ref, qk_ref))
            dob = do_ref[:, sl].astype(bf16)
            vn = u_ref[:, sl] - jnp.dot(wb, sb, preferred_element_type=f32)
            vb = vn.astype(bf16)
            dvn = (lax.dot_general(qkb, dob, (((0,), (0,)), ((), ())), preferred_element_type=f32)
                   + jnp.dot(kdb, dsb, preferred_element_type=f32))
            dvb = dvn.astype(bf16)
            du_ref[:, sl] = dvn
            dw_ref[:, sl] = -_dot_nt(dvb, sb)
            dqd_ref[:, sl] = _dot_nt(dob, sb)
            dkd_ref[:, sl] = _dot_nt(vb, dsb)
            dqk_ref[:, sl] = _dot_nt(dob, vb)
            tot = jnp.sum(jnp.sum(ds * st, axis=1, keepdims=True), axis=0, keepdims=True)
            ddec_ref[:, sl] = jnp.broadcast_to(tot, (8, HD))
            dstate[sl, :] = (ds * dec_ref[0:1, sl]
                             + lax.dot_general(qdb, dob, (((0,), (0,)), ((), ())), preferred_element_type=f32)
                             - lax.dot_general(wb, dvb, (((0,), (0,)), ((), ())), preferred_element_type=f32))

    row = pl.BlockSpec((CH, WH), lambda n: (nc - 1 - n, 0))
    small = pl.BlockSpec((None, 8, WH), lambda n: (nc - 1 - n, 0, 0))
    return pl.pallas_call(
        body, name=name, grid=(nc,),
        in_specs=[row] * 5 + [small, pl.BlockSpec((None, WH, HD), lambda n: (nc - 1 - n, 0, 0)), row],
        out_specs=[row] * 5 + [small],
        out_shape=[_sds((s, WH), f32)] * 5 + [_sds((nc, 8, WH), f32)],
        scratch_shapes=[pltpu.VMEM((WH, HD), f32)],
        compiler_params=_cp(("arbitrary",)),
    )(u, w, qd, kd, qk, dec, states, do)


ANY = pl.BlockSpec(memory_space=pl.ANY)


def _dev_index(p):
    return 4 * p[0] + 2 * p[1] + p[2]


def _shard_of(ref, axis, size, idx):
    return ref.at[pl.ds(idx * size, size), :] if axis == 0 else ref.at[:, pl.ds(idx * size, size)]


def _peer(x, y, c, r):
    return (1 - x if r & 4 else x, 1 - y if r & 2 else y, 1 - c if r & 1 else c)


ALL_PEERS, SIBLING, OTHER_CHIPS = tuple(range(1, NDEV)), (1,), (4, 2, 6)


def _launch(body, ins, out_sds, name, sequencer_id, relations=ALL_PEERS):
    n = len(ins)
    sems = [pltpu.SemaphoreType.DMA((n, 7)), pltpu.SemaphoreType.DMA((n, 7)), pltpu.SemaphoreType.DMA((n,))]
    if sequencer_id is None:
        return pl.pallas_call(
            lambda *refs: body(refs[:n], refs[n:2 * n], *refs[2 * n:]), name=name, in_specs=[ANY] * n, out_specs=[ANY] * n,
            out_shape=out_sds, scratch_shapes=sems, compiler_params=pltpu.CompilerParams(has_side_effects=True),
        )(*ins)
    in_refs = [jax.new_ref(a, memory_space=pltpu.MemorySpace.HBM) for a in ins]
    out_refs = [jax.empty_ref(sd, memory_space=pltpu.MemorySpace.HBM) for sd in out_sds]

    @pl.kernel(mesh=plsc.ScalarSubcoreMesh(axis_name="sequencer", num_cores=1), name=name, scratch_types=sems,
               compiler_params=pltpu.CompilerParams(collective_id=sequencer_id))
    def launch(send, recv, loc):
        x, y, c = lax.axis_index("x"), lax.axis_index("y"), lax.axis_index("c")
        barrier = pltpu.get_barrier_semaphore()
        for r in relations:
            pl.semaphore_signal(barrier, inc=1, device_id=_peer(x, y, c, r), device_id_type=MESH)
        pl.semaphore_wait(barrier, len(relations))
        body(in_refs, out_refs, send, recv, loc)

    launch()
    return [r[...] for r in out_refs]


def all_gather(xs, axes, name, sequencer_id=None):
    n = len(xs)
    fulls = [tuple(d * (NDEV if a == ax else 1) for a, d in enumerate(x.shape)) for x, ax in zip(xs, axes)]

    def body(x_refs, o_refs, send, recv, loc):
        x, y, c = lax.axis_index("x"), lax.axis_index("y"), lax.axis_index("c")
        me, sib = (x, y, c), (x, y, 1 - c)
        chips = [(1 - x, y), (x, 1 - y), (1 - x, 1 - y)]

        def part(t, p):
            return _shard_of(o_refs[t], axes[t], xs[t].shape[axes[t]], _dev_index(p))

        def copy(t, k, block, to, src=None):
            return pltpu.make_async_remote_copy(
                src_ref=part(t, block) if src is None else src, dst_ref=part(t, block),
                send_sem=send.at[t, k], recv_sem=recv.at[t, k], device_id=to, device_id_type=MESH)

        mine = [pltpu.make_async_copy(x_refs[t], part(t, me), loc.at[t]) for t in range(n)]
        for cp in mine:
            cp.start()
        first = []
        for t in range(n):
            first.append(copy(t, 0, me, sib, src=x_refs[t]))
            first += [copy(t, 1 + j, me, (*chip, c), src=x_refs[t]) for j, chip in enumerate(chips)]
        for cp in first:
            cp.start()
        passed = []
        for t in range(n):
            for j, chip in enumerate(chips):
                copy(t, 1 + j, (*chip, c), me).wait_recv()
                passed.append(copy(t, 4 + j, (*chip, c), sib))
                passed[-1].start()
        for t in range(n):
            copy(t, 0, sib, me).wait_recv()
            for j, chip in enumerate(chips):
                copy(t, 4 + j, (*chip, 1 - c), me).wait_recv()
        for cp in first + passed:
            cp.wait_send()
        for cp in mine:
            cp.wait()

    return _launch(body, xs, [_sds(f, x.dtype) for f, x in zip(fulls, xs)], name, sequencer_id)


def pair_swap(gs, axes, name, sequencer_id):
    n = len(gs)
    shards = [tuple(d // (NDEV if a == ax else 1) for a, d in enumerate(g.shape)) for g, ax in zip(gs, axes)]

    def body(g_refs, p_refs, send, recv, loc):
        x, y, c = lax.axis_index("x"), lax.axis_index("y"), lax.axis_index("c")

        def copy(t, j):
            return pltpu.make_async_remote_copy(
                src_ref=_shard_of(g_refs[t], axes[t], shards[t][axes[t]], 2 * j + (1 - c)), dst_ref=p_refs[t].at[j],
                send_sem=send.at[t, j], recv_sem=recv.at[t, j], device_id=(x, y, 1 - c), device_id_type=MESH)

        copies = [copy(t, j) for t in range(n) for j in range(4)]
        for cp in copies:
            cp.start()
        for cp in copies:
            cp.wait_recv()
        for cp in copies:
            cp.wait_send()

    return _launch(body, gs, [_sds((4,) + sh, g.dtype) for sh, g in zip(shards, gs)], name, sequencer_id, SIBLING)


def pair_add(g, p, axis, name, after=()):
    _, rows, cols = p.shape
    tr = _row_tile(rows, cols)
    nb = rows // tr
    if axis == 0:
        g_spec = pl.BlockSpec((tr, cols), lambda j, i, c: ((2 * j + c[0]) * nb + i, 0))
    else:
        g_spec = pl.BlockSpec((tr, cols), lambda j, i, c: (i, 2 * j + c[0]))
    blk = pl.BlockSpec((None, tr, cols), lambda j, i, c: (j, i, 0))

    na = len(after)

    def body(c_ref, *refs):
        g_ref, p_ref, q_ref = refs[na:]
        q_ref[...] = (g_ref[...].astype(f32) + p_ref[...].astype(f32)).astype(bf16)

    return pl.pallas_call(
        body, name=name, out_shape=_sds(p.shape, bf16),
        grid_spec=pltpu.PrefetchScalarGridSpec(num_scalar_prefetch=1, grid=(4, nb), in_specs=[ANY] * na + [g_spec, blk],
                                               out_specs=blk),
        compiler_params=_cp(("parallel", "parallel")),
    )(lax.axis_index("c").astype(jnp.int32).reshape(1), *after, g, p)


def chip_exchange(qs, name, sequencer_id):
    n = len(qs)

    def body(q_refs, r_refs, send, recv, loc):
        x, y, c = lax.axis_index("x"), lax.axis_index("y"), lax.axis_index("c")
        my_chip = 2 * x + y
        peers = [_peer(x, y, c, r) for r in OTHER_CHIPS]
        mine = [pltpu.make_async_copy(q_refs[t].at[my_chip], r_refs[t].at[my_chip], loc.at[t]) for t in range(n)]
        for cp in mine:
            cp.start()

        def copy(t, k, slot):
            p = peers[k]
            return pltpu.make_async_remote_copy(
                src_ref=q_refs[t].at[2 * p[0] + p[1]], dst_ref=r_refs[t].at[slot], send_sem=send.at[t, k], recv_sem=recv.at[t, k],
                device_id=p, device_id_type=MESH)

        sends = [copy(t, k, my_chip) for t in range(n) for k in range(3)]
        for cp in sends:
            cp.start()
        for t in range(n):
            for k in range(3):
                copy(t, k, 2 * peers[k][0] + peers[k][1]).wait_recv()
        for cp in sends:
            cp.wait_send()
        for cp in mine:
            cp.wait()

    return _launch(body, qs, [_sds(q.shape, q.dtype) for q in qs], name, sequencer_id, OTHER_CHIPS)


def _adamw(w, g, m, v):
    m = ADAM_B1 * m + (1.0 - ADAM_B1) * g
    v = ADAM_B2 * v + (1.0 - ADAM_B2) * jnp.square(g)
    m_hat = m / (1.0 - ADAM_B1 ** ADAM_STEP)
    v_hat = v / (1.0 - ADAM_B2 ** ADAM_STEP)
    return -ADAM_LR * (m_hat / (jnp.sqrt(v_hat) + ADAM_EPS) + ADAM_WD * w), m, v


def _sum8(r_ref):
    g = r_ref[0].astype(f32)
    for j in range(1, r_ref.shape[0]):
        g = g + r_ref[j].astype(f32)
    return g


def _row_tile(rows, cols):
    tr = min(rows, max(8, 1 << int(math.log2((1 << 18) / cols))))
    assert rows % tr == 0, (rows, cols)
    return tr


def sum_partials(r, name, after=()):
    _, rows, cols = r.shape
    tr = _row_tile(rows, cols)
    na = len(after)

    def body(*refs):
        refs[na + 1][...] = _sum8(refs[na])

    return pl.pallas_call(
        body, name=name, grid=(rows // tr,),
        in_specs=[ANY] * na + [pl.BlockSpec((r.shape[0], tr, cols), lambda i: (0, i, 0))],
        out_specs=pl.BlockSpec((tr, cols), lambda i: (i, 0)), out_shape=_sds((rows, cols), f32),
        compiler_params=_cp(("parallel",)),
    )(*after, r)


def adamw(w, m, v, layer, name, r=None, g=None, prev=None):
    _, rows, cols = w.shape
    tr = _row_tile(rows, cols)
    blk = pl.BlockSpec((None, tr, cols), lambda i: (layer, i, 0))
    nprev = 0 if prev is None else 4

    def body(w_ref, m_ref, v_ref, src_ref, *rest):
        g_ref, d_ref, nm_ref, nv_ref, token_ref = rest[nprev:]
        grad = _sum8(src_ref) if g is None else src_ref[...]
        g_ref[...] = grad
        d_ref[...], nm_ref[...], nv_ref[...] = _adamw(w_ref[...], grad, m_ref[...], v_ref[...])
        token_ref[...] = jnp.zeros_like(token_ref)

    src, src_spec = (r, pl.BlockSpec((r.shape[0], tr, cols), lambda i: (0, i, 0))) if g is None else (g, pl.BlockSpec((tr, cols), lambda i: (i, 0)))
    *outs, token = pl.pallas_call(
        body, name=name, grid=(rows // tr,), in_specs=[blk] * 3 + [src_spec] + [ANY] * nprev,
        out_specs=[blk] * 4 + [pl.BlockSpec((8, HD), lambda i: (0, 0))],
        out_shape=[_sds(w.shape, f32)] * 4 + [_sds((8, HD), f32)], input_output_aliases={4 + k: k for k in range(nprev)},
        compiler_params=_cp(("arbitrary",)),
    )(w, m, v, src, *(prev or ()))
    return tuple(outs), token


def small_adamw(parts, w, m, v, name):
    def body(p_ref, w_ref, m_ref, v_ref, g_ref, d_ref, nm_ref, nv_ref):
        g = _sum8(p_ref)
        g_ref[...] = g
        d_ref[...], nm_ref[...], nv_ref[...] = _adamw(w_ref[...], g, m_ref[...], v_ref[...])

    return pl.pallas_call(body, name=name, out_shape=[_sds(w.shape, f32)] * 4, compiler_params=_cp())(parts, w, m, v)


def _pack_cols(w):
    pad = jnp.zeros((w.shape[0], NP - OFF_AB - 12), w.dtype)
    return jnp.concatenate([w[:, 5900:12044], w[:, 512:2816], w[:, 2816:5120], w[:, 5120:5888], w[:, 0:512],
                            w[:, 5888:5900], pad], axis=1)


def _unpack_cols(g):
    return jnp.concatenate([g[..., OFF_P:OFF_P + 512], g[..., OFF_SB:OFF_SB + 2304], g[..., OFF_GQ:OFF_GQ + 2304],
                            g[..., OFF_Z:OFF_Z + 768], g[..., OFF_AB:OFF_AB + 12], g[..., OFF_G:OFF_G + 6144]], axis=-1)


def _lanes(v):
    flat = v.reshape(-1)
    n = -(-flat.shape[0] // HD) * HD
    return jnp.pad(flat, (0, n - flat.shape[0])).reshape(n // HD, HD)


def _lanes8(v):
    rows = _lanes(v)
    return jnp.pad(rows, ((0, -rows.shape[0] % 8), (0, 0)))


def _layer_fwd(x, p, l):
    nm = lambda s: f"{s}_l{l}"
    u = rms_fwd(x, p["attn_norm"], nm("rms1"))
    proj = matmul(u, p["w_in"], name=nm("inproj"))
    y_pool = pool_fwd(proj, p["pool_w"], p["pool_scale"], nm("pool"))
    y_sb = sb_fwd(proj, nm("sb"))
    c = conv_fwd(proj, p["conv"], nm("conv"))
    ga = gdn_a_fwd(c, proj, p["alog"], p["dtb"], nm("gdna"))
    o, states = gdn_b_fwd(*ga, nm("gdnb"))
    y_gdn = gdn_out_fwd(o, proj, p["gdn_norm"], nm("gdno"))
    ups = [matmul(y, p[k], name=nm(k)) for y, k in ((y_pool, "w_pool_up"), (y_sb, "w_sb_up"), (y_gdn, "w_gdn_up"))]
    merged = merge_fwd(proj, ups, nm("merge"))
    x1 = matmul(merged, p["w_out"], name=nm("outproj"), epilogue=lambda acc, r: acc + r, extras=(x,))
    u2 = rms_fwd(x1, p["mlp_norm"], nm("rms2"))
    h2 = matmul(u2, p["w_ff1"], name=nm("ff1"), out_dtype=bf16, epilogue=lambda acc: jnp.square(jnp.maximum(acc, 0.0)))
    x2 = matmul(h2, p["w_ff2"], name=nm("ff2"), epilogue=lambda acc, r: acc + r, extras=(x1,))
    saved = dict(x=x, u=u, proj=proj, y_pool=y_pool, y_sb=y_sb, c=c, ga=ga, o=o, states=states, y_gdn=y_gdn, ups=ups,
                 merged=merged, x1=x1, u2=u2, h2=h2)
    return x2, saved


def _layer_bwd(dx2, dx2b, sv, p, l, swap, finish):
    nm = lambda s: f"{s}_l{l}"
    s = dx2.shape[0]
    dh = matmul(dx2b, p["w_ff2"], name=nm("d_ff2_x"), tb=True, out_dtype=bf16,
                epilogue=lambda acc, h2: acc * (2.0 * jnp.sqrt(h2.astype(f32))), extras=(sv["h2"],))
    g_ff2 = matmul(sv["h2"], dx2b, name=nm("d_ff2_w"), ta=True, out_dtype=bf16)
    du2 = matmul(dh, p["w_ff1"], name=nm("d_ff1_x"), tb=True)
    g_ff1 = matmul(sv["u2"], dh, name=nm("d_ff1_w"), ta=True, out_dtype=bf16)
    dx1, dx1b, d_mlp_norm = rms_bwd(sv["x1"], du2, dx2, p["mlp_norm"], nm("d_rms2"))
    dmerged = matmul(dx1b, p["w_out"], name=nm("d_out_x"), tb=True, out_dtype=bf16)
    g_out = matmul(sv["merged"], dx1b, name=nm("d_out_w"), ta=True, out_dtype=bf16)
    dgates, *dups = merge_bwd(sv["proj"], sv["ups"], dmerged, nm("d_merge"))
    dys, g_ups = [], []
    for dup, y, k in zip(dups, (sv["y_pool"], sv["y_sb"], sv["y_gdn"]), ("w_pool_up", "w_sb_up", "w_gdn_up")):
        dys.append(matmul(dup, p[k], name=nm("d_" + k + "_x"), tb=True, out_dtype=bf16))
        g_ups.append(matmul(y, dup, name=nm("d_" + k + "_w"), ta=True, out_dtype=bf16))
    early = g_ups + [g_out, g_ff1, g_ff2]
    swapped = swap(early, BIG_AXES[1:], "a")
    do, dz, d_gdn_norm = gdn_out_bwd(sv["o"], sv["proj"], p["gdn_norm"], dys[2], nm("d_gdno"))
    cots = gdn_b_bwd(*sv["ga"], sv["states"], do, nm("d_gdnb"))
    dc, dab, d_alog, d_dtb = gdn_a_bwd(sv["c"], sv["proj"], p["alog"], p["dtb"], cots[:5], cots[5], nm("d_gdna"))
    r_early, sent_early = finish(early, swapped, BIG_AXES[1:], BIG_NAMES[1:], "a", (dab,))
    dgq, d_conv = conv_bwd(sv["proj"], p["conv"], dc, nm("d_conv"), sent_early)
    dq, dk, dv = sb_bwd(sv["proj"], dys[1], nm("d_sb"))
    dp, d_pool_w, d_pool_scale = pool_bwd(sv["proj"], p["pool_w"], p["pool_scale"], dys[0], nm("d_pool"))
    dproj = jnp.concatenate([dgates, dq, dk, dv, dgq, dz, dp, dab, jnp.zeros((s, NP - OFF_AB - HD), bf16)], axis=1)
    g_in = matmul(sv["u"], dproj, name=nm("d_in_w"), ta=True, out_dtype=bf16)
    r_in, sent_in = finish([g_in], swap([g_in], BIG_AXES[:1], "b"), BIG_AXES[:1], BIG_NAMES[:1], "b", ())
    du = matmul(dproj, p["w_in"], name=nm("d_in_x"), tb=True, after=sent_in)
    dx, dxb, d_attn_norm = rms_bwd(sv["x"], du, dx1, p["attn_norm"], nm("d_rms1"))
    recv = r_in + r_early
    small = [d_attn_norm, d_pool_w, d_pool_scale, d_conv, d_alog, d_dtb, d_gdn_norm, d_mlp_norm]
    return dx, dxb, recv, small


BIG_AXES = (0, 1, 1, 1, 0, 1, 0)
GATHER_ID, EXCHANGE_ID = 1, 2
BIG_NAMES = ("w_in", "w_pool_up", "w_sb_up", "w_gdn_up", "w_out", "w_ff1", "w_ff2")


def kernel(x, attn_norm, w_in, pool_w, pool_scale, gdn_conv, gdn_a_log, gdn_dt_bias, gdn_norm, w_pool_up, w_sb_up, w_gdn_up, w_out, mlp_norm, w_ff1, w_ff2, final_norm, loss_target, m_attn_norm, m_w_in, m_pool_w, m_pool_scale, m_gdn_conv, m_gdn_a_log, m_gdn_dt_bias, m_gdn_norm, m_w_pool_up, m_w_sb_up, m_w_gdn_up, m_w_out, m_mlp_norm, m_w_ff1, m_w_ff2, m_final_norm, v_attn_norm, v_w_in, v_pool_w, v_pool_scale, v_gdn_conv, v_gdn_a_log, v_gdn_dt_bias, v_gdn_norm, v_w_pool_up, v_w_sb_up, v_w_gdn_up, v_w_out, v_mlp_norm, v_w_ff1, v_w_ff2, v_final_norm):
    s = x.shape[1]
    me = _dev_index((lax.axis_index("x"), lax.axis_index("y"), lax.axis_index("c")))
    ncv = gdn_conv.shape[2]

    full = []
    for l in range(NL):
        shards = [_pack_cols(w_in[l]).astype(bf16), w_pool_up[l].astype(bf16), w_sb_up[l].astype(bf16),
                  w_gdn_up[l].astype(bf16), w_out[l].astype(bf16), w_ff1[l].astype(bf16), w_ff2[l].astype(bf16)]
        if l == 0:
            first = all_gather([shards[0], gdn_conv.reshape(NL * 4, ncv)], (0, 0), "gather_first", sequencer_id=GATHER_ID)
            rest = all_gather(shards[1:], BIG_AXES[1:], "gather_rest_l0", sequencer_id=GATHER_ID)
            full.append([first[0]] + rest)
            conv_full = first[1].reshape(NDEV, NL, 4, ncv).transpose(1, 2, 0, 3).reshape(NL, 4, NDEV * ncv)
        else:
            full.append(all_gather(shards, BIG_AXES, f"gather_weights_l{l}", sequencer_id=GATHER_ID))
    params = []
    for l in range(NL):
        p = dict(zip(("w_in", "w_pool_up", "w_sb_up", "w_gdn_up", "w_out", "w_ff1", "w_ff2"), full[l][:7]))
        p.update(attn_norm=attn_norm[l][None], mlp_norm=mlp_norm[l][None], pool_w=pool_w[l], pool_scale=pool_scale[l][None],
                 conv=conv_full[l], alog=_lanes(gdn_a_log[l]), dtb=_lanes(gdn_dt_bias[l]), gdn_norm=gdn_norm[l][None])
        params.append(p)

    h = x[0]
    saved = []
    for l in range(NL):
        h, sv = _layer_fwd(h, params[l], l)
        saved.append(sv)
    dh, dhb, loss_row, d_final = loss_head(h, loss_target[0], final_norm[None], "loss_head")
    recv, smalls = [None] * NL, [None] * NL
    for l in reversed(range(NL)):
        def swap(gs, axes, tag, l=l):
            return pair_swap(gs, axes, f"swap_{tag}_l{l}", None)

        def finish(gs, ps, axes, names, tag, after, l=l):
            qs = [pair_add(g, p, ax, f"pair_add_{k}_l{l}", after) for g, p, ax, k in zip(gs, ps, axes, names)]
            return chip_exchange(qs, f"exchange_{tag}_l{l}", EXCHANGE_ID), tuple(qs)

        dh, dhb, recv[l], smalls[l] = _layer_bwd(dh, dhb, saved[l], params[l], l, swap, finish)

    small_rows = [_lanes8(t) for l in range(NL) for t in smalls[l]] + [_lanes8(d_final), _lanes8(loss_row)]
    packed = jnp.concatenate(small_rows, axis=0)
    parts = all_gather([packed], (0,), "gather_small")[0].reshape(NDEV, packed.shape[0], HD)

    def pack_small(tree):
        rows = []
        for l in range(NL):
            rows += [_lanes8(tree["attn_norm"][l]), _lanes8(tree["pool_w"][l]), _lanes8(tree["pool_scale"][l]),
                     jnp.zeros((4 * NDEV * ncv // HD, HD), f32), _lanes8(tree["gdn_a_log"][l]), _lanes8(tree["gdn_dt_bias"][l]),
                     _lanes8(tree["gdn_norm"][l]), _lanes8(tree["mlp_norm"][l])]
        rows += [_lanes8(tree["final_norm"]), jnp.zeros((8, HD), f32)]
        return jnp.concatenate(rows, axis=0)

    names = ("attn_norm", "pool_w", "pool_scale", "gdn_a_log", "gdn_dt_bias", "gdn_norm", "mlp_norm", "final_norm")
    w_small = pack_small(dict(zip(names, (attn_norm, pool_w, pool_scale, gdn_a_log, gdn_dt_bias, gdn_norm, mlp_norm, final_norm))))
    m_small = pack_small(dict(zip(names, (m_attn_norm, m_pool_w, m_pool_scale, m_gdn_a_log, m_gdn_dt_bias, m_gdn_norm, m_mlp_norm, m_final_norm))))
    v_small = pack_small(dict(zip(names, (v_attn_norm, v_pool_w, v_pool_scale, v_gdn_a_log, v_gdn_dt_bias, v_gdn_norm, v_mlp_norm, v_final_norm))))
    small_out = small_adamw(parts, w_small, m_small, v_small, "adamw_small")

    def unpack_small(buf):
        out, conv_g, r = {}, [], 0
        layer_items = (("attn_norm", (D,)), ("pool_w", (4, HD, HD)), ("pool_scale", (W_POOL,)), ("conv", (4, NDEV * ncv)),
                       ("gdn_a_log", (NH,)), ("gdn_dt_bias", (NH,)), ("gdn_norm", (HD,)), ("mlp_norm", (D,)))
        per_layer = {k: [] for k, _ in layer_items}
        for l in range(NL):
            for k, shape in layer_items:
                size = math.prod(shape)
                nrow = -(-size // (8 * HD)) * 8
                per_layer[k].append(buf[r:r + nrow].reshape(-1)[:size].reshape(shape))
                r += nrow
        for k, _ in layer_items:
            out[k] = jnp.stack(per_layer[k])
        out["final_norm"] = buf[r:r + D // HD].reshape(D)
        out["loss"] = buf[r + D // HD, 0]
        return out

    sm = [unpack_small(b) for b in small_out]
    loss = sm[0]["loss"]
    g_conv = lax.dynamic_slice_in_dim(sm[0]["conv"], me * ncv, ncv, axis=2)
    conv_out = None
    for l in reversed(range(NL)):
        conv_out, _ = adamw(gdn_conv, m_gdn_conv, v_gdn_conv, l, f"adamw_conv_l{l}", g=g_conv[l], prev=conv_out)

    big_out = {}
    big_names = BIG_NAMES
    big_w = dict(zip(big_names, ((w_in, m_w_in, v_w_in), (w_pool_up, m_w_pool_up, v_w_pool_up), (w_sb_up, m_w_sb_up, v_w_sb_up),
                                 (w_gdn_up, m_w_gdn_up, v_w_gdn_up), (w_out, m_w_out, v_w_out), (w_ff1, m_w_ff1, v_w_ff1),
                                 (w_ff2, m_w_ff2, v_w_ff2))))
    tokens = []
    for l, k in [(l, k) for l in reversed(range(NL)) for k in big_names[1:]] + [(l, "w_in") for l in reversed(range(NL))]:
        w, m, v = big_w[k]
        r = recv[l][big_names.index(k)]
        if k == "w_in":
            g = _unpack_cols(sum_partials(r, f"sum_w_in_l{l}", tuple(tokens) if l == 0 else ()))
            big_out[k], token = adamw(w, m, v, l, f"adamw_{k}_l{l}", g=g, prev=big_out.get(k))
        else:
            big_out[k], token = adamw(w, m, v, l, f"adamw_{k}_l{l}", r=r, prev=big_out.get(k))
        tokens.append(token)

    def leaf(i, k):
        if k in big_out:
            return big_out[k][i]
        if k == "gdn_conv":
            return conv_out[i]
        return sm[i][k]

    order = ("attn_norm", "w_in", "pool_w", "pool_scale", "gdn_conv", "gdn_a_log", "gdn_dt_bias", "gdn_norm", "w_pool_up",
             "w_sb_up", "w_gdn_up", "w_out", "mlp_norm", "w_ff1", "w_ff2", "final_norm")
    return (loss, dh[None]) + tuple(leaf(i, k) for i in range(4) for k in order)
```

```python
import functools
import math

import jax
import jax.numpy as jnp
from jax import lax
from jax.experimental import pallas as pl
from jax.experimental.pallas import tpu as pltpu
from jax.experimental.pallas import tpu_sc as plsc

f32, bf16 = jnp.float32, jnp.bfloat16

D = 2048
NDEV = 8
NL = 2
HD = 128
NH = 6
WH = NH * HD
W_POOL = 512
EPS = 1e-6
N_IN = 12044
NP = 12288
OFF_SB, OFF_GQ, OFF_Z, OFF_P, OFF_AB, OFF_G = 0, 2304, 4608, 5376, 6016, 6144
AB_LANE = HD - 2 * NH
POOL_WINDOWS = (2, 4, 8, 16)
CH = 128
TQ = 256
VMEM_LIMIT = 56 * 1024 * 1024
ADAM_LR, ADAM_B1, ADAM_B2, ADAM_EPS, ADAM_WD, ADAM_STEP = 0.001, 0.9, 0.999, 1e-08, 0.01, 10
MESH = pl.DeviceIdType.MESH


def _cp(sem=None):
    return pltpu.CompilerParams(dimension_semantics=sem, vmem_limit_bytes=VMEM_LIMIT)


def _sds(shape, dtype):
    return jax.ShapeDtypeStruct(tuple(shape), dtype)


def matmul(a, b, *, name, ta=False, tb=False, out_dtype=f32, tm=1024, tn=1024, tk=2048, epilogue=None, extras=(), after=()):
    m, k = (a.shape[1], a.shape[0]) if ta else a.shape
    n = b.shape[0] if tb else b.shape[1]
    assert k == (b.shape[1] if tb else b.shape[0]) and a.dtype == bf16 and b.dtype == bf16
    tm, tn, tk = min(tm, m), min(tn, n), min(tk, k)
    assert m % tm == 0 and n % tn == 0 and k % tk == 0, (m, n, k, tm, tn, tk)
    nk = k // tk
    a_spec = pl.BlockSpec((tk, tm), lambda i, j, q: (q, i)) if ta else pl.BlockSpec((tm, tk), lambda i, j, q: (i, q))
    b_spec = pl.BlockSpec((tn, tk), lambda i, j, q: (j, q)) if tb else pl.BlockSpec((tk, tn), lambda i, j, q: (q, j))
    e_specs = [pl.BlockSpec((tm, tn), lambda i, j, q: (i, j)) for _ in extras]
    dn = (((0 if ta else 1,), (1 if tb else 0,)), ((), ()))
    ne = len(extras)

    def body(*refs):
        a_ref, b_ref, *rest = refs[len(after):]
        e_refs, o_ref = rest[:ne], rest[ne]
        part = lax.dot_general(a_ref[...], b_ref[...], dn, preferred_element_type=f32)

        def finish(acc):
            if epilogue is not None:
                acc = epilogue(acc, *[e[...] for e in e_refs])
            o_ref[...] = acc.astype(out_dtype)

        if nk == 1:
            finish(part)
        else:
            acc_ref = rest[ne + 1]
            q = pl.program_id(2)

            @pl.when(q == 0)
            def _():
                acc_ref[...] = part

            @pl.when(jnp.logical_and(q > 0, q < nk - 1))
            def _():
                acc_ref[...] += part

            @pl.when(q == nk - 1)
            def _():
                finish(acc_ref[...] + part)

    return pl.pallas_call(
        body, name=name, grid=(m // tm, n // tn, nk),
        in_specs=[pl.BlockSpec(memory_space=pl.ANY)] * len(after) + [a_spec, b_spec] + e_specs,
        out_specs=pl.BlockSpec((tm, tn), lambda i, j, q: (i, j)), out_shape=_sds((m, n), out_dtype),
        scratch_shapes=[pltpu.VMEM((tm, tn), f32)] if nk > 1 else [],
        compiler_params=_cp(("parallel", "parallel", "arbitrary")),
    )(*after, a, b, *extras)


def rowwise(name, fn, rows, params, outs, sums=(), tr=256, after=()):
    s = rows[0][0].shape[0]
    tr = min(tr, s)
    nin, nout = len(rows) + len(params), len(outs)
    in_specs = [pl.BlockSpec((tr, w), functools.partial(lambda i, c: (i, c), c=c)) for (_, w, c) in rows]
    in_specs += [pl.BlockSpec(p.shape, lambda i: (0, 0)) for p in params]
    out_specs = [pl.BlockSpec((tr, w), lambda i: (i, 0)) for (w, _) in outs]
    out_specs += [pl.BlockSpec(sh, lambda i: (0, 0)) for sh in sums]
    out_shape = [_sds((s, w), dt) for (w, dt) in outs] + [_sds(sh, f32) for sh in sums]

    def body(*refs):
        refs = refs[len(after):]
        res = fn(*[r[...] for r in refs[:nin]])
        for r, v in zip(refs[nin:nin + nout], res[:nout]):
            r[...] = v.astype(r.dtype)
        i = pl.program_id(0)
        for r, v in zip(refs[nin + nout:], res[nout:]):
            @pl.when(i == 0)
            def _(r=r, v=v):
                r[...] = v

            @pl.when(i > 0)
            def _(r=r, v=v):
                r[...] += v

    res = pl.pallas_call(
        body, name=name, grid=(s // tr,), in_specs=[pl.BlockSpec(memory_space=pl.ANY)] * len(after) + in_specs,
        out_specs=out_specs, out_shape=out_shape, compiler_params=_cp(("arbitrary",)),
    )(*after, *[r[0] for r in rows], *params)
    return res


def _rms(x, g):
    return x * lax.rsqrt(jnp.mean(x * x, axis=-1, keepdims=True) + EPS) * g


def rms_fwd(x, g, name):
    return rowwise(name, lambda xb, gb: (_rms(xb, gb),), [(x, D, 0)], [g], [(D, bf16)])[0]


def rms_bwd(x, du, dres, g, name, after=()):
    def fn(xb, dub, drb, gb):
        _, vjp = jax.vjp(_rms, xb, gb)
        dx, dg = vjp(dub.astype(f32))
        return drb + dx, drb + dx, dg

    return rowwise(name, fn, [(x, D, 0), (du, D, 0), (dres, D, 0)], [g], [(D, f32), (D, bf16)], [(1, D)], after=after)


def _merge(gates, up_p, up_s, up_g):
    sg = jax.nn.sigmoid(gates)
    return sg[:, :D] * up_p + sg[:, D:2 * D] * up_s + sg[:, 2 * D:] * up_g


def merge_fwd(proj, ups, name):
    return rowwise(name, lambda g, a, b, c: (_merge(g, a, b, c),),
                   [(proj, 3 * D, OFF_G // (3 * D))] + [(u, D, 0) for u in ups], [], [(D, bf16)], tr=128)[0]


def merge_bwd(proj, ups, dmerged, name):
    def fn(g, a, b, c, dm):
        _, vjp = jax.vjp(_merge, g, a, b, c)
        return vjp(dm.astype(f32))

    return rowwise(name, fn, [(proj, 3 * D, OFF_G // (3 * D))] + [(u, D, 0) for u in ups] + [(dmerged, D, 0)], [],
                   [(3 * D, bf16), (D, bf16), (D, bf16), (D, bf16)], tr=128)


def _gdn_out(o, z, g):
    ys = []
    for h in range(NH):
        sl = slice(h * HD, (h + 1) * HD)
        ys.append(_rms(o[:, sl], g) * jax.nn.silu(z[:, sl]))
    return jnp.concatenate(ys, axis=1)


def gdn_out_fwd(o, proj, g, name):
    return rowwise(name, lambda ob, zb, gb: (_gdn_out(ob, zb, gb),), [(o, WH, 0), (proj, WH, OFF_Z // WH)], [g],
                   [(WH, bf16)])[0]


def gdn_out_bwd(o, proj, g, dy, name):
    def fn(ob, zb, dyb, gb):
        _, vjp = jax.vjp(_gdn_out, ob, zb, gb)
        return vjp(dyb.astype(f32))

    return rowwise(name, fn, [(o, WH, 0), (proj, WH, OFF_Z // WH), (dy, WH, 0)], [g], [(WH, f32), (WH, bf16)],
                   [(1, HD)])


def loss_head(x, target, g, name):
    def loss_fn(xb, gb, tb):
        err = _rms(xb, gb) - tb
        return (0.5 / D) * jnp.sum(jnp.sum(err * err, axis=1, keepdims=True), axis=0, keepdims=True)

    def fn(xb, tb, gb):
        val, vjp = jax.vjp(functools.partial(loss_fn, tb=tb), xb, gb)
        dx, dg = vjp(jnp.ones((1, 1), f32))
        return dx, dx, jnp.broadcast_to(val, (1, HD)), dg

    return rowwise(name, fn, [(x, D, 0), (target, D, 0)], [g], [(D, f32), (D, bf16)], [(1, HD), (1, D)])


PB = 256


def _split(v):
    hi = v.astype(bf16)
    return hi, (v - hi.astype(f32)).astype(bf16)


def _band_dot(make_band, v, s, forward):
    hi, lo = _split(v)
    nb = s // PB
    outs = []
    for r in range(nb):
        lo_r = max(r - 1, 0) if forward else r
        hi_r = r + 1 if forward else min(r + 2, nb)
        band = make_band(r * PB, lo_r * PB, (hi_r - lo_r) * PB)
        sl = slice(lo_r * PB, hi_r * PB)
        outs.append(jnp.dot(band, hi[sl], preferred_element_type=f32) + jnp.dot(band, lo[sl], preferred_element_type=f32))
    return jnp.concatenate(outs, axis=0)


def _pool_common(p, win, s):
    def band(row0, col0, ncol):
        t = row0 + lax.broadcasted_iota(jnp.int32, (PB, ncol), 0)
        u = col0 + lax.broadcasted_iota(jnp.int32, (PB, ncol), 1)
        return jnp.logical_and(u <= t, t < u + win).astype(bf16)

    def band_t(row0, col0, ncol):
        u = row0 + lax.broadcasted_iota(jnp.int32, (PB, ncol), 0)
        t = col0 + lax.broadcasted_iota(jnp.int32, (PB, ncol), 1)
        return jnp.logical_and(u <= t, t < u + win).astype(bf16)

    t = lax.broadcasted_iota(jnp.int32, (s, 1), 0)
    inv_n = 1.0 / jnp.minimum(t + 1, win).astype(f32)
    d = _band_dot(band, p, s, True) * inv_n - p
    return d, inv_n, band_t


def pool_fwd(proj, pool_w, pool_scale, name):
    s = proj.shape[0]

    def body(p_ref, w_ref, sc_ref, y_ref):
        win = jnp.left_shift(2, pl.program_id(0))
        d, _, _ = _pool_common(p_ref[...], win, s)
        y = jnp.dot(d.astype(bf16), w_ref[...].astype(bf16), preferred_element_type=f32) * sc_ref[...]
        y_ref[...] = y.astype(bf16)

    return pl.pallas_call(
        body, name=name, grid=(4,),
        in_specs=[pl.BlockSpec((s, HD), lambda g: (0, OFF_P // HD + g)), pl.BlockSpec((None, HD, HD), lambda g: (g, 0, 0)),
                  pl.BlockSpec((1, HD), lambda g: (0, g))],
        out_specs=pl.BlockSpec((s, HD), lambda g: (0, g)), out_shape=_sds((s, W_POOL), bf16),
        compiler_params=_cp(("arbitrary",)),
    )(proj, pool_w, pool_scale)


def pool_bwd(proj, pool_w, pool_scale, dy, name):
    s = proj.shape[0]

    def body(p_ref, w_ref, sc_ref, dy_ref, dp_ref, dw_ref, dsc_ref):
        win = jnp.left_shift(2, pl.program_id(0))
        d, inv_n, band_t = _pool_common(p_ref[...], win, s)
        w = w_ref[...].astype(bf16)
        dyf = dy_ref[...].astype(f32)
        dsc_ref[...] = jnp.sum(dyf * jnp.dot(d.astype(bf16), w, preferred_element_type=f32), axis=0, keepdims=True)
        dys = (dyf * sc_ref[...]).astype(bf16)
        dd = lax.dot_general(dys, w, (((1,), (1,)), ((), ())), preferred_element_type=f32)
        dw_ref[...] = lax.dot_general(d.astype(bf16), dys, (((0,), (0,)), ((), ())), preferred_element_type=f32)
        dp_ref[...] = (_band_dot(band_t, dd * inv_n, s, False) - dd).astype(bf16)

    return pl.pallas_call(
        body, name=name, grid=(4,),
        in_specs=[pl.BlockSpec((s, HD), lambda g: (0, OFF_P // HD + g)), pl.BlockSpec((None, HD, HD), lambda g: (g, 0, 0)),
                  pl.BlockSpec((1, HD), lambda g: (0, g)), pl.BlockSpec((s, HD), lambda g: (0, g))],
        out_specs=[pl.BlockSpec((s, HD), lambda g: (0, g)), pl.BlockSpec((None, HD, HD), lambda g: (g, 0, 0)),
                   pl.BlockSpec((1, HD), lambda g: (0, g))],
        out_shape=[_sds((s, W_POOL), bf16), _sds((4, HD, HD), f32), _sds((1, W_POOL), f32)],
        compiler_params=_cp(("arbitrary",)),
    )(proj, pool_w, pool_scale, dy)


def _sb_tile(q, k_ref, kb, qi, carry):
    k = k_ref[pl.ds(pl.multiple_of(kb * TQ, TQ), TQ), :].astype(bf16)
    z = lax.dot_general(q, k, (((1,), (1,)), ((), ())), preferred_element_type=f32)
    row = qi * TQ + lax.broadcasted_iota(jnp.int32, (TQ, TQ), 0)
    col = kb * TQ + lax.broadcasted_iota(jnp.int32, (TQ, TQ), 1)
    mask = col < row
    ls = jnp.where(mask, jax.nn.log_sigmoid(-z), 0.0)
    j = lax.broadcasted_iota(jnp.int32, (TQ, TQ), 0)
    u = lax.broadcasted_iota(jnp.int32, (TQ, TQ), 1)
    later = (j > u).astype(bf16)
    hi, lo = _split(ls)
    lw = jnp.dot(hi, later, preferred_element_type=f32) + jnp.dot(lo, later, preferred_element_type=f32)
    a = jnp.where(mask, jnp.exp(ls + z + lw + carry), 0.0)
    return z, mask, ls, a


def sb_fwd(proj, name):
    s = proj.shape[0]
    nq = s // TQ
    scale = HD ** -0.5

    def body(q_ref, k_ref, v_ref, y_ref):
        qi = pl.program_id(1)
        q = (q_ref[...] * scale).astype(bf16)

        def step(j, c):
            acc, carry = c
            kb = qi - j
            _, _, ls, a = _sb_tile(q, k_ref, kb, qi, carry)
            v = v_ref[pl.ds(pl.multiple_of(kb * TQ, TQ), TQ), :].astype(bf16)
            acc = acc + jnp.dot(a.astype(bf16), v, preferred_element_type=f32)
            return acc, carry + jnp.sum(ls, axis=1, keepdims=True)

        acc, _ = lax.fori_loop(0, qi + 1, step, (jnp.zeros((TQ, HD), f32), jnp.zeros((TQ, 1), f32)))
        y_ref[...] = acc.astype(bf16)

    c0 = OFF_SB // HD
    return pl.pallas_call(
        body, name=name, grid=(NH, nq),
        in_specs=[pl.BlockSpec((TQ, HD), lambda h, i: (i, c0 + h)), pl.BlockSpec((s, HD), lambda h, i: (0, c0 + NH + h)),
                  pl.BlockSpec((s, HD), lambda h, i: (0, c0 + 2 * NH + h))],
        out_specs=pl.BlockSpec((TQ, HD), lambda h, i: (i, h)), out_shape=_sds((s, WH), bf16),
        compiler_params=_cp(("arbitrary", "arbitrary")),
    )(proj, proj, proj)


def sb_bwd(proj, dy, name):
    s = proj.shape[0]
    nq = s // TQ
    scale = HD ** -0.5

    def body(q_ref, k_ref, v_ref, do_ref, dq_ref, dk_ref, dv_ref, e_scr, z_scr, dk_acc, dv_acc):
        qi = pl.program_id(1)
        q = (q_ref[...] * scale).astype(bf16)
        do = do_ref[...]

        @pl.when(qi == 0)
        def _():
            dk_acc[...] = jnp.zeros_like(dk_acc)
            dv_acc[...] = jnp.zeros_like(dv_acc)

        def sweep_back(j, carry):
            kb = qi - j
            rows = pl.ds(pl.multiple_of(kb * TQ, TQ), TQ)
            z, _, ls, a = _sb_tile(q, k_ref, kb, qi, carry)
            v = v_ref[rows, :].astype(bf16)
            da = lax.dot_general(do, v, (((1,), (1,)), ((), ())), preferred_element_type=f32)
            e_scr[kb] = da * a
            z_scr[kb] = z
            dv_acc[rows, :] += lax.dot_general(a.astype(bf16), do, (((0,), (0,)), ((), ())), preferred_element_type=f32)
            return carry + jnp.sum(ls, axis=1, keepdims=True)

        lax.fori_loop(0, qi + 1, sweep_back, jnp.zeros((TQ, 1), f32))

        def sweep_fwd(kb, c):
            dq, carry = c
            rows = pl.ds(pl.multiple_of(kb * TQ, TQ), TQ)
            e, z = e_scr[kb], z_scr[kb]
            row = qi * TQ + lax.broadcasted_iota(jnp.int32, (TQ, TQ), 0)
            col = kb * TQ + lax.broadcasted_iota(jnp.int32, (TQ, TQ), 1)
            j = lax.broadcasted_iota(jnp.int32, (TQ, TQ), 0)
            u = lax.broadcasted_iota(jnp.int32, (TQ, TQ), 1)
            earlier = (j < u).astype(bf16)
            hi, lo = _split(e)
            ew = jnp.dot(hi, earlier, preferred_element_type=f32) + jnp.dot(lo, earlier, preferred_element_type=f32)
            sig = jax.nn.sigmoid(z)
            dz = jnp.where(col < row, e * (1.0 - sig) - (ew + carry) * sig, 0.0).astype(bf16)
            k = k_ref[rows, :].astype(bf16)
            dq = dq + jnp.dot(dz, k, preferred_element_type=f32)
            dk_acc[rows, :] += lax.dot_general(dz, q, (((0,), (0,)), ((), ())), preferred_element_type=f32)
            return dq, carry + jnp.sum(e, axis=1, keepdims=True)

        dq, _ = lax.fori_loop(0, qi + 1, sweep_fwd, (jnp.zeros((TQ, HD), f32), jnp.zeros((TQ, 1), f32)))
        dq_ref[...] = (dq * scale).astype(bf16)

        @pl.when(qi == nq - 1)
        def _():
            dk_ref[...] = dk_acc[...].astype(bf16)
            dv_ref[...] = dv_acc[...].astype(bf16)

    c0 = OFF_SB // HD
    return pl.pallas_call(
        body, name=name, grid=(NH, nq),
        in_specs=[pl.BlockSpec((TQ, HD), lambda h, i: (i, c0 + h)), pl.BlockSpec((s, HD), lambda h, i: (0, c0 + NH + h)),
                  pl.BlockSpec((s, HD), lambda h, i: (0, c0 + 2 * NH + h)), pl.BlockSpec((TQ, HD), lambda h, i: (i, h))],
        out_specs=[pl.BlockSpec((TQ, HD), lambda h, i: (i, h)), pl.BlockSpec((s, HD), lambda h, i: (0, h)),
                   pl.BlockSpec((s, HD), lambda h, i: (0, h))],
        out_shape=[_sds((s, WH), bf16)] * 3,
        scratch_shapes=[pltpu.VMEM((nq, TQ, TQ), f32), pltpu.VMEM((nq, TQ, TQ), f32), pltpu.VMEM((s, HD), f32),
                        pltpu.VMEM((s, HD), f32)],
        compiler_params=_cp(("arbitrary", "arbitrary")),
    )(proj, proj, proj, dy)


CB = 256


def _shift_down(v, k, s):
    if k == 0:
        return v
    row = lax.broadcasted_iota(jnp.int32, v.shape, 0)
    return jnp.where(row < k, 0.0, pltpu.roll(v, k, axis=0))


def _shift_up(v, k, s):
    if k == 0:
        return v
    row = lax.broadcasted_iota(jnp.int32, v.shape, 0)
    return jnp.where(row >= s - k, 0.0, pltpu.roll(v, s - k, axis=0))


def conv_fwd(proj, w, name):
    s = proj.shape[0]

    def body(x_ref, w_ref, y_ref):
        x, wv = x_ref[...], w_ref[...]
        y = sum(wv[3 - k:4 - k, :] * _shift_down(x, k, s) for k in range(4))
        y_ref[...] = jax.nn.silu(y)

    return pl.pallas_call(
        body, name=name, grid=(3 * WH // CB,),
        in_specs=[pl.BlockSpec((s, CB), lambda j: (0, OFF_GQ // CB + j)), pl.BlockSpec((4, CB), lambda j: (0, j))],
        out_specs=pl.BlockSpec((s, CB), lambda j: (0, j)), out_shape=_sds((s, 3 * WH), f32),
        compiler_params=_cp(("parallel",)),
    )(proj, w)


def conv_bwd(proj, w, dc, name, after=()):
    s = proj.shape[0]

    def body(*refs):
        x_ref, w_ref, dc_ref, dx_ref, dw_ref = refs[len(after):]
        x, wv = x_ref[...], w_ref[...]
        xs = [_shift_down(x, k, s) for k in range(4)]
        y = sum(wv[3 - k:4 - k, :] * xs[k] for k in range(4))
        sig = jax.nn.sigmoid(y)
        dy = dc_ref[...] * (sig * (1.0 + y * (1.0 - sig)))
        dx_ref[...] = sum(wv[3 - k:4 - k, :] * _shift_up(dy, k, s) for k in range(4)).astype(bf16)
        dw_ref[...] = jnp.concatenate([jnp.sum(dy * xs[3 - i], axis=0, keepdims=True) for i in range(4)], axis=0)

    return pl.pallas_call(
        body, name=name, grid=(3 * WH // CB,),
        in_specs=[pl.BlockSpec(memory_space=pl.ANY)] * len(after)
        + [pl.BlockSpec((s, CB), lambda j: (0, OFF_GQ // CB + j)), pl.BlockSpec((4, CB), lambda j: (0, j)),
           pl.BlockSpec((s, CB), lambda j: (0, j))],
        out_specs=[pl.BlockSpec((s, CB), lambda j: (0, j)), pl.BlockSpec((4, CB), lambda j: (0, j))],
        out_shape=[_sds((s, 3 * WH), bf16), _sds((4, 3 * WH), f32)],
        compiler_params=_cp(("parallel",)),
    )(*after, proj, w, dc)


SOLVE_PASSES = 3
NN, NT, TN = (((1,), (0,)), ((), ())), (((1,), (1,)), ((), ())), (((0,), (0,)), ((), ()))


def _pdot_impl(a, b, dn, passes):
    ah, al = _split(a)
    bh, bl = _split(b)
    dot = lambda p, q: lax.dot_general(p, q, dn, preferred_element_type=f32)
    if passes == 1:
        return dot(ah, bh)
    if passes == 2:
        return dot(ah, bh) + dot(ah, bl)
    return dot(ah, bh) + (dot(ah, bl) + dot(al, bh))


@functools.partial(jax.custom_vjp, nondiff_argnums=(2,))
def _pdot(a, b, passes):
    return _pdot_impl(a, b, NN, passes)


def _pdot_fwd(a, b, passes):
    return _pdot_impl(a, b, NN, passes), (a, b)


def _pdot_bwd(passes, res, ct):
    a, b = res
    return _pdot_impl(ct, b, NT, passes), _pdot_impl(a, ct, TN, passes)


_pdot.defvjp(_pdot_fwd, _pdot_bwd)


@functools.partial(jax.custom_vjp, nondiff_argnums=(2,))
def _pdot_nt(a, b, passes):
    return _pdot_impl(a, b, NT, passes)


def _pdot_nt_fwd(a, b, passes):
    return _pdot_impl(a, b, NT, passes), (a, b)


def _pdot_nt_bwd(passes, res, ct):
    a, b = res
    return _pdot_impl(ct, b, NN, passes), _pdot_impl(ct, a, TN, passes)


_pdot_nt.defvjp(_pdot_nt_fwd, _pdot_nt_bwd)


def _lane_pick(v, h):
    lane = lax.broadcasted_iota(jnp.int32, v.shape, v.ndim - 1)
    return jnp.sum(jnp.where(lane == h, v, 0.0), axis=-1, keepdims=True)


def _l2n(v):
    return v * lax.rsqrt(jnp.sum(v * v, axis=-1, keepdims=True) + EPS)


def _dot_nt(a, b):
    return lax.dot_general(a, b, (((1,), (1,)), ((), ())), preferred_element_type=f32)


def _gdn_chunk(cq, ck, cv, ab, alog, dtb, h):
    ones = jnp.ones((CH, HD), f32)
    q = _l2n(cq) * (HD ** -0.5)
    k = _l2n(ck)
    la = -jnp.exp(_lane_pick(alog, h)) * jax.nn.softplus(_lane_pick(ab, AB_LANE + h) + _lane_pick(dtb, h))
    beta = jax.nn.sigmoid(_lane_pick(ab, AB_LANE + NH + h)) * ones
    i = lax.broadcasted_iota(jnp.int32, (CH, CH), 0)
    j = lax.broadcasted_iota(jnp.int32, (CH, CH), 1)
    incl, strict = j <= i, j < i
    g = _pdot(incl.astype(f32), la * ones, 2)
    diff = g - g.T
    gamma = jnp.where(incl, jnp.exp(jnp.where(incl, diff, 0.0)), 0.0)
    lower = jnp.where(strict, beta * _pdot_nt(k, k, 1) * gamma, 0.0)
    eye = (i == j).astype(f32)
    inv = eye - lower
    pw = _pdot(lower, lower, SOLVE_PASSES)
    for m in range(1, int(math.log2(CH))):
        inv = inv + _pdot(inv, pw, SOLVE_PASSES)
        if m < int(math.log2(CH)) - 1:
            pw = _pdot(pw, pw, SOLVE_PASSES)
    eg = jnp.exp(g)
    u = _pdot(inv, cv * beta, SOLVE_PASSES)
    w = _pdot(inv, k * (beta * eg), SOLVE_PASSES)
    qk = _pdot_nt(q, k, 1) * gamma
    g_last = g[CH - 1:CH, :]
    return u, w, q * eg, k * jnp.exp(g_last - g), qk, jnp.exp(g_last)


def gdn_a_fwd(c, proj, alog, dtb, name):
    s = c.shape[0]
    nc = s // CH

    def body(c_ref, ab_ref, al_ref, dt_ref, u_ref, w_ref, qd_ref, kd_ref, qk_ref, dec_ref):
        for h in range(NH):
            sl = slice(h * HD, (h + 1) * HD)
            cq, ck, cv = (c_ref[:, t * WH + h * HD:t * WH + (h + 1) * HD] for t in range(3))
            res = _gdn_chunk(cq, ck, cv, ab_ref[...], al_ref[...], dt_ref[...], h)
            for r, v in zip((u_ref, w_ref, qd_ref, kd_ref, qk_ref), res[:5]):
                r[:, sl] = v
            dec_ref[:, sl] = jnp.broadcast_to(res[5], (8, HD))

    row = pl.BlockSpec((CH, WH), lambda n: (n, 0))
    par = pl.BlockSpec((1, HD), lambda n: (0, 0))
    return pl.pallas_call(
        body, name=name, grid=(nc,),
        in_specs=[pl.BlockSpec((CH, 3 * WH), lambda n: (n, 0)), pl.BlockSpec((CH, HD), lambda n: (n, OFF_AB // HD)), par, par],
        out_specs=[row] * 5 + [pl.BlockSpec((None, 8, WH), lambda n: (n, 0, 0))],
        out_shape=[_sds((s, WH), f32)] * 5 + [_sds((nc, 8, WH), f32)],
        compiler_params=_cp(("parallel",)),
    )(c, proj, alog, dtb)


def gdn_a_bwd(c, proj, alog, dtb, cots, ddec, name):
    s = c.shape[0]
    nc = s // CH

    def body(c_ref, ab_ref, al_ref, dt_ref, du_ref, dw_ref, dqd_ref, dkd_ref, dqk_ref, ddec_ref,
             dc_ref, dab_ref, dal_ref, ddt_ref):
        n = pl.program_id(0)
        dab, dal, ddt = jnp.zeros((CH, HD), f32), jnp.zeros((1, HD), f32), jnp.zeros((1, HD), f32)
        for h in range(NH):
            sl = slice(h * HD, (h + 1) * HD)
            cq, ck, cv = (c_ref[:, t * WH + h * HD:t * WH + (h + 1) * HD] for t in range(3))
            _, vjp = jax.vjp(functools.partial(_gdn_chunk, h=h), cq, ck, cv, ab_ref[...], al_ref[...], dt_ref[...])
            lane = lax.broadcasted_iota(jnp.int32, (1, HD), 1)
            dd = jnp.where(lane == 0, ddec_ref[0:1, sl], 0.0)
            dcs = vjp((du_ref[:, sl], dw_ref[:, sl], dqd_ref[:, sl], dkd_ref[:, sl], dqk_ref[:, sl], dd))
            for t in range(3):
                dc_ref[:, t * WH + h * HD:t * WH + (h + 1) * HD] = dcs[t]
            dab, dal, ddt = dab + dcs[3], dal + dcs[4], ddt + dcs[5]
        dab_ref[...] = dab.astype(bf16)

        @pl.when(n == 0)
        def _():
            dal_ref[...] = dal
            ddt_ref[...] = ddt

        @pl.when(n > 0)
        def _():
            dal_ref[...] += dal
            ddt_ref[...] += ddt

    row = pl.BlockSpec((CH, WH), lambda n: (n, 0))
    wide = pl.BlockSpec((CH, 3 * WH), lambda n: (n, 0))
    par = pl.BlockSpec((1, HD), lambda n: (0, 0))
    return pl.pallas_call(
        body, name=name, grid=(nc,),
        in_specs=[wide, pl.BlockSpec((CH, HD), lambda n: (n, OFF_AB // HD)), par, par] + [row] * 5
        + [pl.BlockSpec((None, 8, WH), lambda n: (n, 0, 0))],
        out_specs=[wide, pl.BlockSpec((CH, HD), lambda n: (n, 0)), par, par],
        out_shape=[_sds((s, 3 * WH), f32), _sds((s, HD), bf16), _sds((1, HD), f32), _sds((1, HD), f32)],
        compiler_params=_cp(("arbitrary",)),
    )(c, proj, alog, dtb, *cots, ddec)


def gdn_b_fwd(u, w, qd, kd, qk, dec, name):
    s = u.shape[0]
    nc = s // CH

    def body(u_ref, w_ref, qd_ref, kd_ref, qk_ref, dec_ref, o_ref, st_ref, state):
        n = pl.program_id(0)

        @pl.when(n == 0)
        def _():
            state[...] = jnp.zeros_like(state)

        for h in range(NH):
            sl = slice(h * HD, (h + 1) * HD)
            st = state[sl, :]
            st_ref[sl, :] = st
            sb = st.astype(bf16)
            vn = u_ref[:, sl] - jnp.dot(w_ref[:, sl].astype(bf16), sb, preferred_element_type=f32)
            vb = vn.astype(bf16)
            o_ref[:, sl] = (jnp.dot(qd_ref[:, sl].astype(bf16), sb, preferred_element_type=f32)
                            + jnp.dot(qk_ref[:, sl].astype(bf16), vb, preferred_element_type=f32))
            state[sl, :] = st * dec_ref[0:1, sl] + lax.dot_general(
                kd_ref[:, sl].astype(bf16), vb, (((0,), (0,)), ((), ())), preferred_element_type=f32)

    row = pl.BlockSpec((CH, WH), lambda n: (n, 0))
    return pl.pallas_call(
        body, name=name, grid=(nc,),
        in_specs=[row] * 5 + [pl.BlockSpec((None, 8, WH), lambda n: (n, 0, 0))],
        out_specs=[row, pl.BlockSpec((None, WH, HD), lambda n: (n, 0, 0))],
        out_shape=[_sds((s, WH), f32), _sds((nc, WH, HD), f32)],
        scratch_shapes=[pltpu.VMEM((WH, HD), f32)],
        compiler_params=_cp(("arbitrary",)),
    )(u, w, qd, kd, qk, dec)


def gdn_b_bwd(u, w, qd, kd, qk, dec, states, do, name):
    s = u.shape[0]
    nc = s // CH

    def body(u_ref, w_ref, qd_ref, kd_ref, qk_ref, dec_ref, st_ref, do_ref,
             du_ref, dw_ref, dqd_ref, dkd_ref, dqk_ref, ddec_ref, dstate):
        n = pl.program_id(0)

        @pl.when(n == 0)
        def _():
            dstate[...] = jnp.zeros_like(dstate)

        for h in range(NH):
            sl = slice(h * HD, (h + 1) * HD)
            st, ds = st_ref[sl, :], dstate[sl, :]
            sb, dsb = st.astype(bf16), ds.astype(bf16)
            wb, qdb, kdb, qkb = (r[:, sl].astype(bf16) for r in (w_ref, qd_ref, kd_ref, qk_ref))
            dob = do_ref[:, sl].astype(bf16)
            vn = u_ref[:, sl] - jnp.dot(wb, sb, preferred_element_type=f32)
            vb = vn.astype(bf16)
            dvn = (lax.dot_general(qkb, dob, (((0,), (0,)), ((), ())), preferred_element_type=f32)
                   + jnp.dot(kdb, dsb, preferred_element_type=f32))
            dvb = dvn.astype(bf16)
            du_ref[:, sl] = dvn
            dw_ref[:, sl] = -_dot_nt(dvb, sb)
            dqd_ref[:, sl] = _dot_nt(dob, sb)
            dkd_ref[:, sl] = _dot_nt(vb, dsb)
            dqk_ref[:, sl] = _dot_nt(dob, vb)
            tot = jnp.sum(jnp.sum(ds * st, axis=1, keepdims=True), axis=0, keepdims=True)
            ddec_ref[:, sl] = jnp.broadcast_to(tot, (8, HD))
            dstate[sl, :] = (ds * dec_ref[0:1, sl]
                             + lax.dot_general(qdb, dob, (((0,), (0,)), ((), ())), preferred_element_type=f32)
                             - lax.dot_general(wb, dvb, (((0,), (0,)), ((), ())), preferred_element_type=f32))

    row = pl.BlockSpec((CH, WH), lambda n: (nc - 1 - n, 0))
    small = pl.BlockSpec((None, 8, WH), lambda n: (nc - 1 - n, 0, 0))
    return pl.pallas_call(
        body, name=name, grid=(nc,),
        in_specs=[row] * 5 + [small, pl.BlockSpec((None, WH, HD), lambda n: (nc - 1 - n, 0, 0)), row],
        out_specs=[row] * 5 + [small],
        out_shape=[_sds((s, WH), f32)] * 5 + [_sds((nc, 8, WH), f32)],
        scratch_shapes=[pltpu.VMEM((WH, HD), f32)],
        compiler_params=_cp(("arbitrary",)),
    )(u, w, qd, kd, qk, dec, states, do)


ANY = pl.BlockSpec(memory_space=pl.ANY)


def _dev_index(p):
    return 4 * p[0] + 2 * p[1] + p[2]


def _shard_of(ref, axis, size, idx):
    return ref.at[pl.ds(idx * size, size), :] if axis == 0 else ref.at[:, pl.ds(idx * size, size)]


def _peer(x, y, c, r):
    return (1 - x if r & 4 else x, 1 - y if r & 2 else y, 1 - c if r & 1 else c)


ALL_PEERS, SIBLING, OTHER_CHIPS = tuple(range(1, NDEV)), (1,), (4, 2, 6)


def _launch(body, ins, out_sds, name, sequencer_id, relations=ALL_PEERS):
    n = len(ins)
    sems = [pltpu.SemaphoreType.DMA((n, 7)), pltpu.SemaphoreType.DMA((n, 7)), pltpu.SemaphoreType.DMA((n,))]
    if sequencer_id is None:
        return pl.pallas_call(
            lambda *refs: body(refs[:n], refs[n:2 * n], *refs[2 * n:]), name=name, in_specs=[ANY] * n, out_specs=[ANY] * n,
            out_shape=out_sds, scratch_shapes=sems, compiler_params=pltpu.CompilerParams(has_side_effects=True),
        )(*ins)
    in_refs = [jax.new_ref(a, memory_space=pltpu.MemorySpace.HBM) for a in ins]
    out_refs = [jax.empty_ref(sd, memory_space=pltpu.MemorySpace.HBM) for sd in out_sds]

    @pl.kernel(mesh=plsc.ScalarSubcoreMesh(axis_name="sequencer", num_cores=1), name=name, scratch_types=sems,
               compiler_params=pltpu.CompilerParams(collective_id=sequencer_id))
    def launch(send, recv, loc):
        x, y, c = lax.axis_index("x"), lax.axis_index("y"), lax.axis_index("c")
        barrier = pltpu.get_barrier_semaphore()
        for r in relations:
            pl.semaphore_signal(barrier, inc=1, device_id=_peer(x, y, c, r), device_id_type=MESH)
        pl.semaphore_wait(barrier, len(relations))
        body(in_refs, out_refs, send, recv, loc)

    launch()
    return [r[...] for r in out_refs]


def all_gather(xs, axes, name, sequencer_id=None):
    n = len(xs)
    fulls = [tuple(d * (NDEV if a == ax else 1) for a, d in enumerate(x.shape)) for x, ax in zip(xs, axes)]

    def body(x_refs, o_refs, send, recv, loc):
        x, y, c = lax.axis_index("x"), lax.axis_index("y"), lax.axis_index("c")
        me, sib = (x, y, c), (x, y, 1 - c)
        chips = [(1 - x, y), (x, 1 - y), (1 - x, 1 - y)]

        def part(t, p):
            return _shard_of(o_refs[t], axes[t], xs[t].shape[axes[t]], _dev_index(p))

        def copy(t, k, block, to, src=None):
            return pltpu.make_async_remote_copy(
                src_ref=part(t, block) if src is None else src, dst_ref=part(t, block),
                send_sem=send.at[t, k], recv_sem=recv.at[t, k], device_id=to, device_id_type=MESH)

        mine = [pltpu.make_async_copy(x_refs[t], part(t, me), loc.at[t]) for t in range(n)]
        for cp in mine:
            cp.start()
        first = []
        for t in range(n):
            first.append(copy(t, 0, me, sib, src=x_refs[t]))
            first += [copy(t, 1 + j, me, (*chip, c), src=x_refs[t]) for j, chip in enumerate(chips)]
        for cp in first:
            cp.start()
        passed = []
        for t in range(n):
            for j, chip in enumerate(chips):
                copy(t, 1 + j, (*chip, c), me).wait_recv()
                passed.append(copy(t, 4 + j, (*chip, c), sib))
                passed[-1].start()
        for t in range(n):
            copy(t, 0, sib, me).wait_recv()
            for j, chip in enumerate(chips):
                copy(t, 4 + j, (*chip, 1 - c), me).wait_recv()
        for cp in first + passed:
            cp.wait_send()
        for cp in mine:
            cp.wait()

    return _launch(body, xs, [_sds(f, x.dtype) for f, x in zip(fulls, xs)], name, sequencer_id)


def pair_swap(gs, axes, name, sequencer_id):
    n = len(gs)
    shards = [tuple(d // (NDEV if a == ax else 1) for a, d in enumerate(g.shape)) for g, ax in zip(gs, axes)]

    def body(g_refs, p_refs, send, recv, loc):
        x, y, c = lax.axis_index("x"), lax.axis_index("y"), lax.axis_index("c")

        def copy(t, j):
            return pltpu.make_async_remote_copy(
                src_ref=_shard_of(g_refs[t], axes[t], shards[t][axes[t]], 2 * j + (1 - c)), dst_ref=p_refs[t].at[j],
                send_sem=send.at[t, j], recv_sem=recv.at[t, j], device_id=(x, y, 1 - c), device_id_type=MESH)

        copies = [copy(t, j) for t in range(n) for j in range(4)]
        for cp in copies:
            cp.start()
        for cp in copies:
            cp.wait_recv()
        for cp in copies:
            cp.wait_send()

    return _launch(body, gs, [_sds((4,) + sh, g.dtype) for sh, g in zip(shards, gs)], name, sequencer_id, SIBLING)


def pair_add(g, p, axis, name, after=()):
    _, rows, cols = p.shape
    tr = _row_tile(rows, cols, 1 << 20)
    nb = rows // tr
    if axis == 0:
        g_spec = pl.BlockSpec((tr, cols), lambda j, i, c: ((2 * j + c[0]) * nb + i, 0))
    else:
        g_spec = pl.BlockSpec((tr, cols), lambda j, i, c: (i, 2 * j + c[0]))
    blk = pl.BlockSpec((None, tr, cols), lambda j, i, c: (j, i, 0))

    na = len(after)

    def body(c_ref, *refs):
        g_ref, p_ref, q_ref = refs[na:]
        q_ref[...] = (g_ref[...].astype(f32) + p_ref[...].astype(f32)).astype(bf16)

    return pl.pallas_call(
        body, name=name, out_shape=_sds(p.shape, bf16),
        grid_spec=pltpu.PrefetchScalarGridSpec(num_scalar_prefetch=1, grid=(4, nb), in_specs=[ANY] * na + [g_spec, blk],
                                               out_specs=blk),
        compiler_params=_cp(("parallel", "parallel")),
    )(lax.axis_index("c").astype(jnp.int32).reshape(1), *after, g, p)


def chip_exchange(qs, name, sequencer_id):
    n = len(qs)

    def body(q_refs, r_refs, send, recv, loc):
        x, y, c = lax.axis_index("x"), lax.axis_index("y"), lax.axis_index("c")
        my_chip = 2 * x + y
        peers = [_peer(x, y, c, r) for r in OTHER_CHIPS]
        mine = [pltpu.make_async_copy(q_refs[t].at[my_chip], r_refs[t].at[my_chip], loc.at[t]) for t in range(n)]
        for cp in mine:
            cp.start()

        def copy(t, k, slot):
            p = peers[k]
            return pltpu.make_async_remote_copy(
                src_ref=q_refs[t].at[2 * p[0] + p[1]], dst_ref=r_refs[t].at[slot], send_sem=send.at[t, k], recv_sem=recv.at[t, k],
                device_id=p, device_id_type=MESH)

        sends = [copy(t, k, my_chip) for t in range(n) for k in range(3)]
        for cp in sends:
            cp.start()
        for t in range(n):
            for k in range(3):
                copy(t, k, 2 * peers[k][0] + peers[k][1]).wait_recv()
        for cp in sends:
            cp.wait_send()
        for cp in mine:
            cp.wait()

    return _launch(body, qs, [_sds(q.shape, q.dtype) for q in qs], name, sequencer_id, OTHER_CHIPS)


def _adamw(w, g, m, v):
    m = ADAM_B1 * m + (1.0 - ADAM_B1) * g
    v = ADAM_B2 * v + (1.0 - ADAM_B2) * jnp.square(g)
    m_hat = m / (1.0 - ADAM_B1 ** ADAM_STEP)
    v_hat = v / (1.0 - ADAM_B2 ** ADAM_STEP)
    return -ADAM_LR * (m_hat / (jnp.sqrt(v_hat) + ADAM_EPS) + ADAM_WD * w), m, v


def _sum8(r_ref):
    g = r_ref[0].astype(f32)
    for j in range(1, r_ref.shape[0]):
        g = g + r_ref[j].astype(f32)
    return g


def _row_tile(rows, cols, elems=1 << 18):
    tr = min(rows, max(8, 1 << int(math.log2(elems / cols))))
    assert rows % tr == 0, (rows, cols)
    return tr


def sum_partials(r, name, after=()):
    _, rows, cols = r.shape
    tr = _row_tile(rows, cols)
    na = len(after)

    def body(*refs):
        refs[na + 1][...] = _sum8(refs[na])

    return pl.pallas_call(
        body, name=name, grid=(rows // tr,),
        in_specs=[ANY] * na + [pl.BlockSpec((r.shape[0], tr, cols), lambda i: (0, i, 0))],
        out_specs=pl.BlockSpec((tr, cols), lambda i: (i, 0)), out_shape=_sds((rows, cols), f32),
        compiler_params=_cp(("parallel",)),
    )(*after, r)


def adamw_t(w, m, v, grads, name):
    rows, nl, cols = w.shape
    tr = min(rows, (1 << 16) // cols)
    blk = pl.BlockSpec((tr, nl, cols), lambda i: (i, 0, 0))
    flat = pl.BlockSpec((tr, cols), lambda i: (i, 0))

    def body(w_ref, m_ref, v_ref, *rest):
        g_refs, (g_ref, d_ref, nm_ref, nv_ref) = rest[:nl], rest[nl:]
        for l in range(nl):
            grad = g_refs[l][...]
            g_ref[:, l, :] = grad
            d_ref[:, l, :], nm_ref[:, l, :], nv_ref[:, l, :] = _adamw(w_ref[:, l, :], grad, m_ref[:, l, :], v_ref[:, l, :])

    return tuple(pl.pallas_call(
        body, name=name, grid=(pl.cdiv(rows, tr),), in_specs=[blk] * 3 + [flat] * nl, out_specs=[blk] * 4,
        out_shape=[_sds(w.shape, f32)] * 4, compiler_params=_cp(("parallel",)),
    )(w, m, v, *grads))


def adamw(w, m, v, layer, name, r=None, g=None, prev=None):
    _, rows, cols = w.shape
    tr = _row_tile(rows, cols)
    blk = pl.BlockSpec((None, tr, cols), lambda i: (layer, i, 0))
    nprev = 0 if prev is None else 4

    def body(w_ref, m_ref, v_ref, src_ref, *rest):
        g_ref, d_ref, nm_ref, nv_ref, token_ref = rest[nprev:]
        grad = _sum8(src_ref) if g is None else src_ref[...]
        g_ref[...] = grad
        d_ref[...], nm_ref[...], nv_ref[...] = _adamw(w_ref[...], grad, m_ref[...], v_ref[...])
        token_ref[...] = jnp.zeros_like(token_ref)

    src, src_spec = (r, pl.BlockSpec((r.shape[0], tr, cols), lambda i: (0, i, 0))) if g is None else (g, pl.BlockSpec((tr, cols), lambda i: (i, 0)))
    *outs, token = pl.pallas_call(
        body, name=name, grid=(rows // tr,), in_specs=[blk] * 3 + [src_spec] + [ANY] * nprev,
        out_specs=[blk] * 4 + [pl.BlockSpec((8, HD), lambda i: (0, 0))],
        out_shape=[_sds(w.shape, f32)] * 4 + [_sds((8, HD), f32)], input_output_aliases={4 + k: k for k in range(nprev)},
        compiler_params=_cp(("arbitrary",)),
    )(w, m, v, src, *(prev or ()))
    return tuple(outs), token


def small_adamw(parts, w, m, v, name):
    def body(p_ref, w_ref, m_ref, v_ref, g_ref, d_ref, nm_ref, nv_ref):
        g = _sum8(p_ref)
        g_ref[...] = g
        d_ref[...], nm_ref[...], nv_ref[...] = _adamw(w_ref[...], g, m_ref[...], v_ref[...])

    return pl.pallas_call(body, name=name, out_shape=[_sds(w.shape, f32)] * 4, compiler_params=_cp())(parts, w, m, v)


def _pack_rows(wt):
    tail = jnp.pad(wt[5888:N_IN], ((OFF_G - 12 - (OFF_P + 512), 0), (0, 0)))
    return jnp.concatenate([wt[512:2816], wt[2816:5120], wt[5120:5888], wt[0:512], tail], axis=0)


def _unpack_rows(g):
    return jnp.concatenate([g[OFF_P:OFF_P + 512], g[OFF_SB:OFF_SB + 2304], g[OFF_GQ:OFF_GQ + 2304], g[OFF_Z:OFF_Z + 768],
                            g[OFF_G - 12:NP]], axis=0)


def _lanes(v):
    flat = v.reshape(-1)
    n = -(-flat.shape[0] // HD) * HD
    return jnp.pad(flat, (0, n - flat.shape[0])).reshape(n // HD, HD)


def _lanes8(v):
    rows = _lanes(v)
    return jnp.pad(rows, ((0, -rows.shape[0] % 8), (0, 0)))


def _layer_fwd(x, p, l):
    nm = lambda s: f"{s}_l{l}"
    u = rms_fwd(x, p["attn_norm"], nm("rms1"))
    proj = matmul(u, p["w_in"], name=nm("inproj"), tb=True)
    y_pool = pool_fwd(proj, p["pool_w"], p["pool_scale"], nm("pool"))
    y_sb = sb_fwd(proj, nm("sb"))
    c = conv_fwd(proj, p["conv"], nm("conv"))
    ga = gdn_a_fwd(c, proj, p["alog"], p["dtb"], nm("gdna"))
    o, states = gdn_b_fwd(*ga, nm("gdnb"))
    y_gdn = gdn_out_fwd(o, proj, p["gdn_norm"], nm("gdno"))
    ups = [matmul(y, p[k], name=nm(k)) for y, k in ((y_pool, "w_pool_up"), (y_sb, "w_sb_up"), (y_gdn, "w_gdn_up"))]
    merged = merge_fwd(proj, ups, nm("merge"))
    x1 = matmul(merged, p["w_out"], name=nm("outproj"), epilogue=lambda acc, r: acc + r, extras=(x,))
    u2 = rms_fwd(x1, p["mlp_norm"], nm("rms2"))
    h2 = matmul(u2, p["w_ff1"], name=nm("ff1"), out_dtype=bf16, epilogue=lambda acc: jnp.square(jnp.maximum(acc, 0.0)))
    x2 = matmul(h2, p["w_ff2"], name=nm("ff2"), epilogue=lambda acc, r: acc + r, extras=(x1,))
    saved = dict(x=x, u=u, proj=proj, y_pool=y_pool, y_sb=y_sb, c=c, ga=ga, o=o, states=states, y_gdn=y_gdn, ups=ups,
                 merged=merged, x1=x1, u2=u2, h2=h2)
    return x2, saved


def _layer_bwd(dx2, dx2b, sv, p, l, swap, finish):
    nm = lambda s: f"{s}_l{l}"
    s = dx2.shape[0]
    dh = matmul(dx2b, p["w_ff2"], name=nm("d_ff2_x"), tb=True, out_dtype=bf16,
                epilogue=lambda acc, h2: acc * (2.0 * jnp.sqrt(h2.astype(f32))), extras=(sv["h2"],))
    g_ff2 = matmul(sv["h2"], dx2b, name=nm("d_ff2_w"), ta=True, out_dtype=bf16)
    du2 = matmul(dh, p["w_ff1"], name=nm("d_ff1_x"), tb=True)
    g_ff1 = matmul(sv["u2"], dh, name=nm("d_ff1_w"), ta=True, out_dtype=bf16)
    dx1, dx1b, d_mlp_norm = rms_bwd(sv["x1"], du2, dx2, p["mlp_norm"], nm("d_rms2"))
    dmerged = matmul(dx1b, p["w_out"], name=nm("d_out_x"), tb=True, out_dtype=bf16)
    g_out = matmul(sv["merged"], dx1b, name=nm("d_out_w"), ta=True, out_dtype=bf16)
    dgates, *dups = merge_bwd(sv["proj"], sv["ups"], dmerged, nm("d_merge"))
    dys, g_ups = [], []
    for dup, y, k in zip(dups, (sv["y_pool"], sv["y_sb"], sv["y_gdn"]), ("w_pool_up", "w_sb_up", "w_gdn_up")):
        dys.append(matmul(dup, p[k], name=nm("d_" + k + "_x"), tb=True, out_dtype=bf16))
        g_ups.append(matmul(y, dup, name=nm("d_" + k + "_w"), ta=True, out_dtype=bf16))
    early = g_ups + [g_out, g_ff1, g_ff2]
    swapped = swap(early, BIG_AXES[1:], "a")
    do, dz, d_gdn_norm = gdn_out_bwd(sv["o"], sv["proj"], p["gdn_norm"], dys[2], nm("d_gdno"))
    cots = gdn_b_bwd(*sv["ga"], sv["states"], do, nm("d_gdnb"))
    dc, dab, d_alog, d_dtb = gdn_a_bwd(sv["c"], sv["proj"], p["alog"], p["dtb"], cots[:5], cots[5], nm("d_gdna"))
    r_early, sent_early = finish(early, swapped, BIG_AXES[1:], BIG_NAMES[1:], "a", (dab,))
    dgq, d_conv = conv_bwd(sv["proj"], p["conv"], dc, nm("d_conv"), sent_early)
    dq, dk, dv = sb_bwd(sv["proj"], dys[1], nm("d_sb"))
    dp, d_pool_w, d_pool_scale = pool_bwd(sv["proj"], p["pool_w"], p["pool_scale"], dys[0], nm("d_pool"))
    dproj = jnp.concatenate([dq, dk, dv, dgq, dz, dp, jnp.zeros((s, OFF_AB - OFF_P - W_POOL), bf16), dab, dgates], axis=1)
    g_in = matmul(dproj, sv["u"], name=nm("d_in_w"), ta=True, out_dtype=bf16)
    r_in, sent_in = finish([g_in], swap([g_in], BIG_AXES[:1], "b"), BIG_AXES[:1], BIG_NAMES[:1], "b", ())
    du = matmul(dproj, p["w_in"], name=nm("d_in_x"), after=sent_in)
    dx, dxb, d_attn_norm = rms_bwd(sv["x"], du, dx1, p["attn_norm"], nm("d_rms1"))
    recv = r_in + r_early
    small = [d_attn_norm, d_pool_w, d_pool_scale, d_conv, d_alog, d_dtb, d_gdn_norm, d_mlp_norm]
    return dx, dxb, recv, small


BIG_AXES = (1, 1, 1, 1, 0, 1, 0)
GATHER_ID, EXCHANGE_ID = 1, 2
BIG_NAMES = ("w_in", "w_pool_up", "w_sb_up", "w_gdn_up", "w_out", "w_ff1", "w_ff2")


def kernel(x, attn_norm, w_in, pool_w, pool_scale, gdn_conv, gdn_a_log, gdn_dt_bias, gdn_norm, w_pool_up, w_sb_up, w_gdn_up, w_out, mlp_norm, w_ff1, w_ff2, final_norm, loss_target, m_attn_norm, m_w_in, m_pool_w, m_pool_scale, m_gdn_conv, m_gdn_a_log, m_gdn_dt_bias, m_gdn_norm, m_w_pool_up, m_w_sb_up, m_w_gdn_up, m_w_out, m_mlp_norm, m_w_ff1, m_w_ff2, m_final_norm, v_attn_norm, v_w_in, v_pool_w, v_pool_scale, v_gdn_conv, v_gdn_a_log, v_gdn_dt_bias, v_gdn_norm, v_w_pool_up, v_w_sb_up, v_w_gdn_up, v_w_out, v_mlp_norm, v_w_ff1, v_w_ff2, v_final_norm):
    s = x.shape[1]
    me = _dev_index((lax.axis_index("x"), lax.axis_index("y"), lax.axis_index("c")))
    ncv = gdn_conv.shape[2]

    w_in_t, m_w_in_t, v_w_in_t = (jnp.transpose(a, (2, 0, 1)) for a in (w_in, m_w_in, v_w_in))
    full = []
    for l in range(NL):
        shards = [_pack_rows(w_in_t[:, l]).astype(bf16), w_pool_up[l].astype(bf16), w_sb_up[l].astype(bf16),
                  w_gdn_up[l].astype(bf16), w_out[l].astype(bf16), w_ff1[l].astype(bf16), w_ff2[l].astype(bf16)]
        if l == 0:
            first = all_gather([shards[0], gdn_conv.reshape(NL * 4, ncv)], (1, 0), "gather_first", sequencer_id=GATHER_ID)
            rest = all_gather(shards[1:], BIG_AXES[1:], "gather_rest_l0", sequencer_id=GATHER_ID)
            full.append([first[0]] + rest)
            conv_full = first[1].reshape(NDEV, NL, 4, ncv).transpose(1, 2, 0, 3).reshape(NL, 4, NDEV * ncv)
        else:
            full.append(all_gather(shards, BIG_AXES, f"gather_weights_l{l}", sequencer_id=GATHER_ID))
    params = []
    for l in range(NL):
        p = dict(zip(("w_in", "w_pool_up", "w_sb_up", "w_gdn_up", "w_out", "w_ff1", "w_ff2"), full[l][:7]))
        p.update(attn_norm=attn_norm[l][None], mlp_norm=mlp_norm[l][None], pool_w=pool_w[l], pool_scale=pool_scale[l][None],
                 conv=conv_full[l], alog=_lanes(gdn_a_log[l]), dtb=_lanes(gdn_dt_bias[l]), gdn_norm=gdn_norm[l][None])
        params.append(p)

    h = x[0]
    saved = []
    for l in range(NL):
        h, sv = _layer_fwd(h, params[l], l)
        saved.append(sv)
    dh, dhb, loss_row, d_final = loss_head(h, loss_target[0], final_norm[None], "loss_head")
    recv, smalls = [None] * NL, [None] * NL
    for l in reversed(range(NL)):
        def swap(gs, axes, tag, l=l):
            return pair_swap(gs, axes, f"swap_{tag}_l{l}", None)

        def finish(gs, ps, axes, names, tag, after, l=l):
            qs = [pair_add(g, p, ax, f"pair_add_{k}_l{l}", after) for g, p, ax, k in zip(gs, ps, axes, names)]
            return chip_exchange(qs, f"exchange_{tag}_l{l}", EXCHANGE_ID), tuple(qs)

        dh, dhb, recv[l], smalls[l] = _layer_bwd(dh, dhb, saved[l], params[l], l, swap, finish)

    small_rows = [_lanes8(t) for l in range(NL) for t in smalls[l]] + [_lanes8(d_final), _lanes8(loss_row)]
    packed = jnp.concatenate(small_rows, axis=0)
    parts = all_gather([packed], (0,), "gather_small")[0].reshape(NDEV, packed.shape[0], HD)

    def pack_small(tree):
        rows = []
        for l in range(NL):
            rows += [_lanes8(tree["attn_norm"][l]), _lanes8(tree["pool_w"][l]), _lanes8(tree["pool_scale"][l]),
                     jnp.zeros((4 * NDEV * ncv // HD, HD), f32), _lanes8(tree["gdn_a_log"][l]), _lanes8(tree["gdn_dt_bias"][l]),
                     _lanes8(tree["gdn_norm"][l]), _lanes8(tree["mlp_norm"][l])]
        rows += [_lanes8(tree["final_norm"]), jnp.zeros((8, HD), f32)]
        return jnp.concatenate(rows, axis=0)

    names = ("attn_norm", "pool_w", "pool_scale", "gdn_a_log", "gdn_dt_bias", "gdn_norm", "mlp_norm", "final_norm")
    w_small = pack_small(dict(zip(names, (attn_norm, pool_w, pool_scale, gdn_a_log, gdn_dt_bias, gdn_norm, mlp_norm, final_norm))))
    m_small = pack_small(dict(zip(names, (m_attn_norm, m_pool_w, m_pool_scale, m_gdn_a_log, m_gdn_dt_bias, m_gdn_norm, m_mlp_norm, m_final_norm))))
    v_small = pack_small(dict(zip(names, (v_attn_norm, v_pool_w, v_pool_scale, v_gdn_a_log, v_gdn_dt_bias, v_gdn_norm, v_mlp_norm, v_final_norm))))
    small_out = small_adamw(parts, w_small, m_small, v_small, "adamw_small")

    def unpack_small(buf):
        out, conv_g, r = {}, [], 0
        layer_items = (("attn_norm", (D,)), ("pool_w", (4, HD, HD)), ("pool_scale", (W_POOL,)), ("conv", (4, NDEV * ncv)),
                       ("gdn_a_log", (NH,)), ("gdn_dt_bias", (NH,)), ("gdn_norm", (HD,)), ("mlp_norm", (D,)))
        per_layer = {k: [] for k, _ in layer_items}
        for l in range(NL):
            for k, shape in layer_items:
                size = math.prod(shape)
                nrow = -(-size // (8 * HD)) * 8
                per_layer[k].append(buf[r:r + nrow].reshape(-1)[:size].reshape(shape))
                r += nrow
        for k, _ in layer_items:
            out[k] = jnp.stack(per_layer[k])
        out["final_norm"] = buf[r:r + D // HD].reshape(D)
        out["loss"] = buf[r + D // HD, 0]
        return out

    sm = [unpack_small(b) for b in small_out]
    loss = sm[0]["loss"]
    g_conv = lax.dynamic_slice_in_dim(sm[0]["conv"], me * ncv, ncv, axis=2)
    conv_out = None
    for l in reversed(range(NL)):
        conv_out, _ = adamw(gdn_conv, m_gdn_conv, v_gdn_conv, l, f"adamw_conv_l{l}", g=g_conv[l], prev=conv_out)

    big_out = {}
    big_names = BIG_NAMES
    big_w = dict(zip(big_names, ((w_in_t, m_w_in_t, v_w_in_t), (w_pool_up, m_w_pool_up, v_w_pool_up), (w_sb_up, m_w_sb_up, v_w_sb_up),
                                 (w_gdn_up, m_w_gdn_up, v_w_gdn_up), (w_out, m_w_out, v_w_out), (w_ff1, m_w_ff1, v_w_ff1),
                                 (w_ff2, m_w_ff2, v_w_ff2))))
    tokens = []
    for l, k in [(l, k) for l in reversed(range(NL)) for k in big_names[1:]]:
        w, m, v = big_w[k]
        big_out[k], token = adamw(w, m, v, l, f"adamw_{k}_l{l}", r=recv[l][big_names.index(k)], prev=big_out.get(k))
        tokens.append(token)
    g_in = [_unpack_rows(sum_partials(recv[l][0], f"sum_w_in_l{l}", tuple(tokens) if l == 0 else ())) for l in range(NL)]
    big_out["w_in"] = adamw_t(*big_w["w_in"], g_in, "adamw_w_in")

    def leaf(i, k):
        if k == "w_in":
            return jnp.transpose(big_out[k][i], (1, 2, 0))
        if k in big_out:
            return big_out[k][i]
        if k == "gdn_conv":
            return conv_out[i]
        return sm[i][k]

    order = ("attn_norm", "w_in", "pool_w", "pool_scale", "gdn_conv", "gdn_a_log", "gdn_dt_bias", "gdn_norm", "w_pool_up",
             "w_sb_up", "w_gdn_up", "w_out", "mlp_norm", "w_ff1", "w_ff2", "final_norm")
    return (loss, dh[None]) + tuple(leaf(i, k) for i in range(4) for k in order)
```

```python
import functools
import math

import jax
import jax.numpy as jnp
from jax import lax
from jax.experimental import pallas as pl
from jax.experimental.pallas import tpu as pltpu
from jax.experimental.pallas import tpu_sc as plsc

f32, bf16 = jnp.float32, jnp.bfloat16

D = 2048
NDEV = 8
NL = 2
HD = 128
NH = 6
WH = NH * HD
W_POOL = 512
EPS = 1e-6
N_IN = 12044
NP = 12288
OFF_SB, OFF_GQ, OFF_Z, OFF_P, OFF_AB, OFF_G = 0, 2304, 4608, 5376, 6016, 6144
AB_LANE = HD - 2 * NH
POOL_WINDOWS = (2, 4, 8, 16)
CH = 128
TQ = 256
VMEM_LIMIT = 56 * 1024 * 1024
ADAM_LR, ADAM_B1, ADAM_B2, ADAM_EPS, ADAM_WD, ADAM_STEP = 0.001, 0.9, 0.999, 1e-08, 0.01, 10
MESH = pl.DeviceIdType.MESH


def _cp(sem=None):
    return pltpu.CompilerParams(dimension_semantics=sem, vmem_limit_bytes=VMEM_LIMIT)


def _sds(shape, dtype):
    return jax.ShapeDtypeStruct(tuple(shape), dtype)


def matmul(a, b, *, name, ta=False, tb=False, out_dtype=f32, tm=1024, tn=1024, tk=2048, epilogue=None, extras=(), after=()):
    m, k = (a.shape[1], a.shape[0]) if ta else a.shape
    n = b.shape[0] if tb else b.shape[1]
    assert k == (b.shape[1] if tb else b.shape[0]) and a.dtype == bf16 and b.dtype == bf16
    tm, tn, tk = min(tm, m), min(tn, n), min(tk, k)
    assert m % tm == 0 and n % tn == 0 and k % tk == 0, (m, n, k, tm, tn, tk)
    nk = k // tk
    a_spec = pl.BlockSpec((tk, tm), lambda i, j, q: (q, i)) if ta else pl.BlockSpec((tm, tk), lambda i, j, q: (i, q))
    b_spec = pl.BlockSpec((tn, tk), lambda i, j, q: (j, q)) if tb else pl.BlockSpec((tk, tn), lambda i, j, q: (q, j))
    e_specs = [pl.BlockSpec((tm, tn), lambda i, j, q: (i, j)) for _ in extras]
    dn = (((0 if ta else 1,), (1 if tb else 0,)), ((), ()))
    ne = len(extras)

    def body(*refs):
        a_ref, b_ref, *rest = refs[len(after):]
        e_refs, o_ref = rest[:ne], rest[ne]
        part = lax.dot_general(a_ref[...], b_ref[...], dn, preferred_element_type=f32)

        def finish(acc):
            if epilogue is not None:
                acc = epilogue(acc, *[e[...] for e in e_refs])
            o_ref[...] = acc.astype(out_dtype)

        if nk == 1:
            finish(part)
        else:
            acc_ref = rest[ne + 1]
            q = pl.program_id(2)

            @pl.when(q == 0)
            def _():
                acc_ref[...] = part

            @pl.when(jnp.logical_and(q > 0, q < nk - 1))
            def _():
                acc_ref[...] += part

            @pl.when(q == nk - 1)
            def _():
                finish(acc_ref[...] + part)

    return pl.pallas_call(
        body, name=name, grid=(m // tm, n // tn, nk),
        in_specs=[pl.BlockSpec(memory_space=pl.ANY)] * len(after) + [a_spec, b_spec] + e_specs,
        out_specs=pl.BlockSpec((tm, tn), lambda i, j, q: (i, j)), out_shape=_sds((m, n), out_dtype),
        scratch_shapes=[pltpu.VMEM((tm, tn), f32)] if nk > 1 else [],
        compiler_params=_cp(("parallel", "parallel", "arbitrary")),
    )(*after, a, b, *extras)


def rowwise(name, fn, rows, params, outs, sums=(), tr=256, after=()):
    s = rows[0][0].shape[0]
    tr = min(tr, s)
    nin, nout = len(rows) + len(params), len(outs)
    in_specs = [pl.BlockSpec((tr, w), functools.partial(lambda i, c: (i, c), c=c)) for (_, w, c) in rows]
    in_specs += [pl.BlockSpec(p.shape, lambda i: (0, 0)) for p in params]
    out_specs = [pl.BlockSpec((tr, w), lambda i: (i, 0)) for (w, _) in outs]
    out_specs += [pl.BlockSpec(sh, lambda i: (0, 0)) for sh in sums]
    out_shape = [_sds((s, w), dt) for (w, dt) in outs] + [_sds(sh, f32) for sh in sums]

    def body(*refs):
        refs = refs[len(after):]
        res = fn(*[r[...] for r in refs[:nin]])
        for r, v in zip(refs[nin:nin + nout], res[:nout]):
            r[...] = v.astype(r.dtype)
        i = pl.program_id(0)
        for r, v in zip(refs[nin + nout:], res[nout:]):
            @pl.when(i == 0)
            def _(r=r, v=v):
                r[...] = v

            @pl.when(i > 0)
            def _(r=r, v=v):
                r[...] += v

    res = pl.pallas_call(
        body, name=name, grid=(s // tr,), in_specs=[pl.BlockSpec(memory_space=pl.ANY)] * len(after) + in_specs,
        out_specs=out_specs, out_shape=out_shape, compiler_params=_cp(("arbitrary",)),
    )(*after, *[r[0] for r in rows], *params)
    return res


def _rms(x, g):
    return x * lax.rsqrt(jnp.mean(x * x, axis=-1, keepdims=True) + EPS) * g


def rms_fwd(x, g, name):
    return rowwise(name, lambda xb, gb: (_rms(xb, gb),), [(x, D, 0)], [g], [(D, bf16)])[0]


def rms_bwd(x, du, dres, g, name, after=()):
    def fn(xb, dub, drb, gb):
        _, vjp = jax.vjp(_rms, xb, gb)
        dx, dg = vjp(dub.astype(f32))
        return drb + dx, drb + dx, dg

    return rowwise(name, fn, [(x, D, 0), (du, D, 0), (dres, D, 0)], [g], [(D, f32), (D, bf16)], [(1, D)], after=after)


def _merge(gates, up_p, up_s, up_g):
    sg = jax.nn.sigmoid(gates)
    return sg[:, :D] * up_p + sg[:, D:2 * D] * up_s + sg[:, 2 * D:] * up_g


def merge_fwd(proj, ups, name):
    return rowwise(name, lambda g, a, b, c: (_merge(g, a, b, c),),
                   [(proj, 3 * D, OFF_G // (3 * D))] + [(u, D, 0) for u in ups], [], [(D, bf16)], tr=128)[0]


def merge_bwd(proj, ups, dmerged, name):
    def fn(g, a, b, c, dm):
        _, vjp = jax.vjp(_merge, g, a, b, c)
        return vjp(dm.astype(f32))

    return rowwise(name, fn, [(proj, 3 * D, OFF_G // (3 * D))] + [(u, D, 0) for u in ups] + [(dmerged, D, 0)], [],
                   [(3 * D, bf16), (D, bf16), (D, bf16), (D, bf16)], tr=128)


def _gdn_out(o, z, g):
    ys = []
    for h in range(NH):
        sl = slice(h * HD, (h + 1) * HD)
        ys.append(_rms(o[:, sl], g) * jax.nn.silu(z[:, sl]))
    return jnp.concatenate(ys, axis=1)


def gdn_out_fwd(o, proj, g, name):
    return rowwise(name, lambda ob, zb, gb: (_gdn_out(ob, zb, gb),), [(o, WH, 0), (proj, WH, OFF_Z // WH)], [g],
                   [(WH, bf16)])[0]


def gdn_out_bwd(o, proj, g, dy, name):
    def fn(ob, zb, dyb, gb):
        _, vjp = jax.vjp(_gdn_out, ob, zb, gb)
        return vjp(dyb.astype(f32))

    return rowwise(name, fn, [(o, WH, 0), (proj, WH, OFF_Z // WH), (dy, WH, 0)], [g], [(WH, f32), (WH, bf16)],
                   [(1, HD)])


def loss_head(x, target, g, name):
    def loss_fn(xb, gb, tb):
        err = _rms(xb, gb) - tb
        return (0.5 / D) * jnp.sum(jnp.sum(err * err, axis=1, keepdims=True), axis=0, keepdims=True)

    def fn(xb, tb, gb):
        val, vjp = jax.vjp(functools.partial(loss_fn, tb=tb), xb, gb)
        dx, dg = vjp(jnp.ones((1, 1), f32))
        return dx, dx, jnp.broadcast_to(val, (1, HD)), dg

    return rowwise(name, fn, [(x, D, 0), (target, D, 0)], [g], [(D, f32), (D, bf16)], [(1, HD), (1, D)])


PB = 256


def _split(v):
    hi = v.astype(bf16)
    return hi, (v - hi.astype(f32)).astype(bf16)


def _band_dot(make_band, v, s, forward):
    hi, lo = _split(v)
    nb = s // PB
    outs = []
    for r in range(nb):
        lo_r = max(r - 1, 0) if forward else r
        hi_r = r + 1 if forward else min(r + 2, nb)
        band = make_band(r * PB, lo_r * PB, (hi_r - lo_r) * PB)
        sl = slice(lo_r * PB, hi_r * PB)
        outs.append(jnp.dot(band, hi[sl], preferred_element_type=f32) + jnp.dot(band, lo[sl], preferred_element_type=f32))
    return jnp.concatenate(outs, axis=0)


def _pool_common(p, win, s):
    def band(row0, col0, ncol):
        t = row0 + lax.broadcasted_iota(jnp.int32, (PB, ncol), 0)
        u = col0 + lax.broadcasted_iota(jnp.int32, (PB, ncol), 1)
        return jnp.logical_and(u <= t, t < u + win).astype(bf16)

    def band_t(row0, col0, ncol):
        u = row0 + lax.broadcasted_iota(jnp.int32, (PB, ncol), 0)
        t = col0 + lax.broadcasted_iota(jnp.int32, (PB, ncol), 1)
        return jnp.logical_and(u <= t, t < u + win).astype(bf16)

    t = lax.broadcasted_iota(jnp.int32, (s, 1), 0)
    inv_n = 1.0 / jnp.minimum(t + 1, win).astype(f32)
    d = _band_dot(band, p, s, True) * inv_n - p
    return d, inv_n, band_t


def pool_fwd(proj, pool_w, pool_scale, name):
    s = proj.shape[0]

    def body(p_ref, w_ref, sc_ref, y_ref):
        win = jnp.left_shift(2, pl.program_id(0))
        d, _, _ = _pool_common(p_ref[...], win, s)
        y = jnp.dot(d.astype(bf16), w_ref[...].astype(bf16), preferred_element_type=f32) * sc_ref[...]
        y_ref[...] = y.astype(bf16)

    return pl.pallas_call(
        body, name=name, grid=(4,),
        in_specs=[pl.BlockSpec((s, HD), lambda g: (0, OFF_P // HD + g)), pl.BlockSpec((None, HD, HD), lambda g: (g, 0, 0)),
                  pl.BlockSpec((1, HD), lambda g: (0, g))],
        out_specs=pl.BlockSpec((s, HD), lambda g: (0, g)), out_shape=_sds((s, W_POOL), bf16),
        compiler_params=_cp(("arbitrary",)),
    )(proj, pool_w, pool_scale)


def pool_bwd(proj, pool_w, pool_scale, dy, name):
    s = proj.shape[0]

    def body(p_ref, w_ref, sc_ref, dy_ref, dp_ref, dw_ref, dsc_ref):
        win = jnp.left_shift(2, pl.program_id(0))
        d, inv_n, band_t = _pool_common(p_ref[...], win, s)
        w = w_ref[...].astype(bf16)
        dyf = dy_ref[...].astype(f32)
        dsc_ref[...] = jnp.sum(dyf * jnp.dot(d.astype(bf16), w, preferred_element_type=f32), axis=0, keepdims=True)
        dys = (dyf * sc_ref[...]).astype(bf16)
        dd = lax.dot_general(dys, w, (((1,), (1,)), ((), ())), preferred_element_type=f32)
        dw_ref[...] = lax.dot_general(d.astype(bf16), dys, (((0,), (0,)), ((), ())), preferred_element_type=f32)
        dp_ref[...] = (_band_dot(band_t, dd * inv_n, s, False) - dd).astype(bf16)

    return pl.pallas_call(
        body, name=name, grid=(4,),
        in_specs=[pl.BlockSpec((s, HD), lambda g: (0, OFF_P // HD + g)), pl.BlockSpec((None, HD, HD), lambda g: (g, 0, 0)),
                  pl.BlockSpec((1, HD), lambda g: (0, g)), pl.BlockSpec((s, HD), lambda g: (0, g))],
        out_specs=[pl.BlockSpec((s, HD), lambda g: (0, g)), pl.BlockSpec((None, HD, HD), lambda g: (g, 0, 0)),
                   pl.BlockSpec((1, HD), lambda g: (0, g))],
        out_shape=[_sds((s, W_POOL), bf16), _sds((4, HD, HD), f32), _sds((1, W_POOL), f32)],
        compiler_params=_cp(("arbitrary",)),
    )(proj, pool_w, pool_scale, dy)


def _sb_tile(q, k_ref, kb, qi, carry):
    k = k_ref[pl.ds(pl.multiple_of(kb * TQ, TQ), TQ), :].astype(bf16)
    z = lax.dot_general(q, k, (((1,), (1,)), ((), ())), preferred_element_type=f32)
    row = qi * TQ + lax.broadcasted_iota(jnp.int32, (TQ, TQ), 0)
    col = kb * TQ + lax.broadcasted_iota(jnp.int32, (TQ, TQ), 1)
    mask = col < row
    ls = jnp.where(mask, jax.nn.log_sigmoid(-z), 0.0)
    j = lax.broadcasted_iota(jnp.int32, (TQ, TQ), 0)
    u = lax.broadcasted_iota(jnp.int32, (TQ, TQ), 1)
    later = (j > u).astype(bf16)
    hi, lo = _split(ls)
    lw = jnp.dot(hi, later, preferred_element_type=f32) + jnp.dot(lo, later, preferred_element_type=f32)
    a = jnp.where(mask, jnp.exp(ls + z + lw + carry), 0.0)
    return z, mask, ls, a


def sb_fwd(proj, name):
    s = proj.shape[0]
    nq = s // TQ
    scale = HD ** -0.5

    def body(q_ref, k_ref, v_ref, y_ref):
        qi = pl.program_id(1)
        q = (q_ref[...] * scale).astype(bf16)

        def step(j, c):
            acc, carry = c
            kb = qi - j
            _, _, ls, a = _sb_tile(q, k_ref, kb, qi, carry)
            v = v_ref[pl.ds(pl.multiple_of(kb * TQ, TQ), TQ), :].astype(bf16)
            acc = acc + jnp.dot(a.astype(bf16), v, preferred_element_type=f32)
            return acc, carry + jnp.sum(ls, axis=1, keepdims=True)

        acc, _ = lax.fori_loop(0, qi + 1, step, (jnp.zeros((TQ, HD), f32), jnp.zeros((TQ, 1), f32)))
        y_ref[...] = acc.astype(bf16)

    c0 = OFF_SB // HD
    return pl.pallas_call(
        body, name=name, grid=(NH, nq),
        in_specs=[pl.BlockSpec((TQ, HD), lambda h, i: (i, c0 + h)), pl.BlockSpec((s, HD), lambda h, i: (0, c0 + NH + h)),
                  pl.BlockSpec((s, HD), lambda h, i: (0, c0 + 2 * NH + h))],
        out_specs=pl.BlockSpec((TQ, HD), lambda h, i: (i, h)), out_shape=_sds((s, WH), bf16),
        compiler_params=_cp(("arbitrary", "arbitrary")),
    )(proj, proj, proj)


def sb_bwd(proj, dy, name):
    s = proj.shape[0]
    nq = s // TQ
    scale = HD ** -0.5

    def body(q_ref, k_ref, v_ref, do_ref, dq_ref, dk_ref, dv_ref, e_scr, z_scr, dk_acc, dv_acc):
        qi = pl.program_id(1)
        q = (q_ref[...] * scale).astype(bf16)
        do = do_ref[...]

        @pl.when(qi == 0)
        def _():
            dk_acc[...] = jnp.zeros_like(dk_acc)
            dv_acc[...] = jnp.zeros_like(dv_acc)

        def sweep_back(j, carry):
            kb = qi - j
            rows = pl.ds(pl.multiple_of(kb * TQ, TQ), TQ)
            z, _, ls, a = _sb_tile(q, k_ref, kb, qi, carry)
            v = v_ref[rows, :].astype(bf16)
            da = lax.dot_general(do, v, (((1,), (1,)), ((), ())), preferred_element_type=f32)
            e_scr[kb] = da * a
            z_scr[kb] = z
            dv_acc[rows, :] += lax.dot_general(a.astype(bf16), do, (((0,), (0,)), ((), ())), preferred_element_type=f32)
            return carry + jnp.sum(ls, axis=1, keepdims=True)

        lax.fori_loop(0, qi + 1, sweep_back, jnp.zeros((TQ, 1), f32))

        def sweep_fwd(kb, c):
            dq, carry = c
            rows = pl.ds(pl.multiple_of(kb * TQ, TQ), TQ)
            e, z = e_scr[kb], z_scr[kb]
            row = qi * TQ + lax.broadcasted_iota(jnp.int32, (TQ, TQ), 0)
            col = kb * TQ + lax.broadcasted_iota(jnp.int32, (TQ, TQ), 1)
            j = lax.broadcasted_iota(jnp.int32, (TQ, TQ), 0)
            u = lax.broadcasted_iota(jnp.int32, (TQ, TQ), 1)
            earlier = (j < u).astype(bf16)
            hi, lo = _split(e)
            ew = jnp.dot(hi, earlier, preferred_element_type=f32) + jnp.dot(lo, earlier, preferred_element_type=f32)
            sig = jax.nn.sigmoid(z)
            dz = jnp.where(col < row, e * (1.0 - sig) - (ew + carry) * sig, 0.0).astype(bf16)
            k = k_ref[rows, :].astype(bf16)
            dq = dq + jnp.dot(dz, k, preferred_element_type=f32)
            dk_acc[rows, :] += lax.dot_general(dz, q, (((0,), (0,)), ((), ())), preferred_element_type=f32)
            return dq, carry + jnp.sum(e, axis=1, keepdims=True)

        dq, _ = lax.fori_loop(0, qi + 1, sweep_fwd, (jnp.zeros((TQ, HD), f32), jnp.zeros((TQ, 1), f32)))
        dq_ref[...] = (dq * scale).astype(bf16)

        @pl.when(qi == nq - 1)
        def _():
            dk_ref[...] = dk_acc[...].astype(bf16)
            dv_ref[...] = dv_acc[...].astype(bf16)

    c0 = OFF_SB // HD
    return pl.pallas_call(
        body, name=name, grid=(NH, nq),
        in_specs=[pl.BlockSpec((TQ, HD), lambda h, i: (i, c0 + h)), pl.BlockSpec((s, HD), lambda h, i: (0, c0 + NH + h)),
                  pl.BlockSpec((s, HD), lambda h, i: (0, c0 + 2 * NH + h)), pl.BlockSpec((TQ, HD), lambda h, i: (i, h))],
        out_specs=[pl.BlockSpec((TQ, HD), lambda h, i: (i, h)), pl.BlockSpec((s, HD), lambda h, i: (0, h)),
                   pl.BlockSpec((s, HD), lambda h, i: (0, h))],
        out_shape=[_sds((s, WH), bf16)] * 3,
        scratch_shapes=[pltpu.VMEM((nq, TQ, TQ), f32), pltpu.VMEM((nq, TQ, TQ), f32), pltpu.VMEM((s, HD), f32),
                        pltpu.VMEM((s, HD), f32)],
        compiler_params=_cp(("arbitrary", "arbitrary")),
    )(proj, proj, proj, dy)


CB = 256


def _shift_down(v, k, s):
    if k == 0:
        return v
    row = lax.broadcasted_iota(jnp.int32, v.shape, 0)
    return jnp.where(row < k, 0.0, pltpu.roll(v, k, axis=0))


def _shift_up(v, k, s):
    if k == 0:
        return v
    row = lax.broadcasted_iota(jnp.int32, v.shape, 0)
    return jnp.where(row >= s - k, 0.0, pltpu.roll(v, s - k, axis=0))


def conv_fwd(proj, w, name):
    s = proj.shape[0]

    def body(x_ref, w_ref, y_ref):
        x, wv = x_ref[...], w_ref[...]
        y = sum(wv[3 - k:4 - k, :] * _shift_down(x, k, s) for k in range(4))
        y_ref[...] = jax.nn.silu(y)

    return pl.pallas_call(
        body, name=name, grid=(3 * WH // CB,),
        in_specs=[pl.BlockSpec((s, CB), lambda j: (0, OFF_GQ // CB + j)), pl.BlockSpec((4, CB), lambda j: (0, j))],
        out_specs=pl.BlockSpec((s, CB), lambda j: (0, j)), out_shape=_sds((s, 3 * WH), f32),
        compiler_params=_cp(("parallel",)),
    )(proj, w)


def conv_bwd(proj, w, dc, name, after=()):
    s = proj.shape[0]

    def body(*refs):
        x_ref, w_ref, dc_ref, dx_ref, dw_ref = refs[len(after):]
        x, wv = x_ref[...], w_ref[...]
        xs = [_shift_down(x, k, s) for k in range(4)]
        y = sum(wv[3 - k:4 - k, :] * xs[k] for k in range(4))
        sig = jax.nn.sigmoid(y)
        dy = dc_ref[...] * (sig * (1.0 + y * (1.0 - sig)))
        dx_ref[...] = sum(wv[3 - k:4 - k, :] * _shift_up(dy, k, s) for k in range(4)).astype(bf16)
        dw_ref[...] = jnp.concatenate([jnp.sum(dy * xs[3 - i], axis=0, keepdims=True) for i in range(4)], axis=0)

    return pl.pallas_call(
        body, name=name, grid=(3 * WH // CB,),
        in_specs=[pl.BlockSpec(memory_space=pl.ANY)] * len(after)
        + [pl.BlockSpec((s, CB), lambda j: (0, OFF_GQ // CB + j)), pl.BlockSpec((4, CB), lambda j: (0, j)),
           pl.BlockSpec((s, CB), lambda j: (0, j))],
        out_specs=[pl.BlockSpec((s, CB), lambda j: (0, j)), pl.BlockSpec((4, CB), lambda j: (0, j))],
        out_shape=[_sds((s, 3 * WH), bf16), _sds((4, 3 * WH), f32)],
        compiler_params=_cp(("parallel",)),
    )(*after, proj, w, dc)


SOLVE_PASSES = 3
NN, NT, TN = (((1,), (0,)), ((), ())), (((1,), (1,)), ((), ())), (((0,), (0,)), ((), ()))


def _pdot_impl(a, b, dn, passes):
    ah, al = _split(a)
    bh, bl = _split(b)
    dot = lambda p, q: lax.dot_general(p, q, dn, preferred_element_type=f32)
    if passes == 1:
        return dot(ah, bh)
    if passes == 2:
        return dot(ah, bh) + dot(ah, bl)
    return dot(ah, bh) + (dot(ah, bl) + dot(al, bh))


@functools.partial(jax.custom_vjp, nondiff_argnums=(2,))
def _pdot(a, b, passes):
    return _pdot_impl(a, b, NN, passes)


def _pdot_fwd(a, b, passes):
    return _pdot_impl(a, b, NN, passes), (a, b)


def _pdot_bwd(passes, res, ct):
    a, b = res
    return _pdot_impl(ct, b, NT, passes), _pdot_impl(a, ct, TN, passes)


_pdot.defvjp(_pdot_fwd, _pdot_bwd)


@functools.partial(jax.custom_vjp, nondiff_argnums=(2,))
def _pdot_nt(a, b, passes):
    return _pdot_impl(a, b, NT, passes)


def _pdot_nt_fwd(a, b, passes):
    return _pdot_impl(a, b, NT, passes), (a, b)


def _pdot_nt_bwd(passes, res, ct):
    a, b = res
    return _pdot_impl(ct, b, NN, passes), _pdot_impl(ct, a, TN, passes)


_pdot_nt.defvjp(_pdot_nt_fwd, _pdot_nt_bwd)


def _lane_pick(v, h):
    lane = lax.broadcasted_iota(jnp.int32, v.shape, v.ndim - 1)
    return jnp.sum(jnp.where(lane == h, v, 0.0), axis=-1, keepdims=True)


def _l2n(v):
    return v * lax.rsqrt(jnp.sum(v * v, axis=-1, keepdims=True) + EPS)


def _dot_nt(a, b):
    return lax.dot_general(a, b, (((1,), (1,)), ((), ())), preferred_element_type=f32)


def _gdn_chunk(cq, ck, cv, ab, alog, dtb, h):
    ones = jnp.ones((CH, HD), f32)
    q = _l2n(cq) * (HD ** -0.5)
    k = _l2n(ck)
    la = -jnp.exp(_lane_pick(alog, h)) * jax.nn.softplus(_lane_pick(ab, AB_LANE + h) + _lane_pick(dtb, h))
    beta = jax.nn.sigmoid(_lane_pick(ab, AB_LANE + NH + h)) * ones
    i = lax.broadcasted_iota(jnp.int32, (CH, CH), 0)
    j = lax.broadcasted_iota(jnp.int32, (CH, CH), 1)
    incl, strict = j <= i, j < i
    g = _pdot(incl.astype(f32), la * ones, 2)
    diff = g - g.T
    gamma = jnp.where(incl, jnp.exp(jnp.where(incl, diff, 0.0)), 0.0)
    lower = jnp.where(strict, beta * _pdot_nt(k, k, 1) * gamma, 0.0)
    eye = (i == j).astype(f32)
    inv = eye - lower
    pw = _pdot(lower, lower, SOLVE_PASSES)
    for m in range(1, int(math.log2(CH))):
        inv = inv + _pdot(inv, pw, SOLVE_PASSES)
        if m < int(math.log2(CH)) - 1:
            pw = _pdot(pw, pw, SOLVE_PASSES)
    eg = jnp.exp(g)
    u = _pdot(inv, cv * beta, SOLVE_PASSES)
    w = _pdot(inv, k * (beta * eg), SOLVE_PASSES)
    qk = _pdot_nt(q, k, 1) * gamma
    g_last = g[CH - 1:CH, :]
    return u, w, q * eg, k * jnp.exp(g_last - g), qk, jnp.exp(g_last)


def gdn_a_fwd(c, proj, alog, dtb, name):
    s = c.shape[0]
    nc = s // CH

    def body(c_ref, ab_ref, al_ref, dt_ref, u_ref, w_ref, qd_ref, kd_ref, qk_ref, dec_ref):
        for h in range(NH):
            sl = slice(h * HD, (h + 1) * HD)
            cq, ck, cv = (c_ref[:, t * WH + h * HD:t * WH + (h + 1) * HD] for t in range(3))
            res = _gdn_chunk(cq, ck, cv, ab_ref[...], al_ref[...], dt_ref[...], h)
            for r, v in zip((u_ref, w_ref, qd_ref, kd_ref, qk_ref), res[:5]):
                r[:, sl] = v
            dec_ref[:, sl] = jnp.broadcast_to(res[5], (8, HD))

    row = pl.BlockSpec((CH, WH), lambda n: (n, 0))
    par = pl.BlockSpec((1, HD), lambda n: (0, 0))
    return pl.pallas_call(
        body, name=name, grid=(nc,),
        in_specs=[pl.BlockSpec((CH, 3 * WH), lambda n: (n, 0)), pl.BlockSpec((CH, HD), lambda n: (n, OFF_AB // HD)), par, par],
        out_specs=[row] * 5 + [pl.BlockSpec((None, 8, WH), lambda n: (n, 0, 0))],
        out_shape=[_sds((s, WH), f32)] * 5 + [_sds((nc, 8, WH), f32)],
        compiler_params=_cp(("parallel",)),
    )(c, proj, alog, dtb)


def gdn_a_bwd(c, proj, alog, dtb, cots, ddec, name):
    s = c.shape[0]
    nc = s // CH

    def body(c_ref, ab_ref, al_ref, dt_ref, du_ref, dw_ref, dqd_ref, dkd_ref, dqk_ref, ddec_ref,
             dc_ref, dab_ref, dal_ref, ddt_ref):
        n = pl.program_id(0)
        dab, dal, ddt = jnp.zeros((CH, HD), f32), jnp.zeros((1, HD), f32), jnp.zeros((1, HD), f32)
        for h in range(NH):
            sl = slice(h * HD, (h + 1) * HD)
            cq, ck, cv = (c_ref[:, t * WH + h * HD:t * WH + (h + 1) * HD] for t in range(3))
            _, vjp = jax.vjp(functools.partial(_gdn_chunk, h=h), cq, ck, cv, ab_ref[...], al_ref[...], dt_ref[...])
            lane = lax.broadcasted_iota(jnp.int32, (1, HD), 1)
            dd = jnp.where(lane == 0, ddec_ref[0:1, sl], 0.0)
            dcs = vjp((du_ref[:, sl], dw_ref[:, sl], dqd_ref[:, sl], dkd_ref[:, sl], dqk_ref[:, sl], dd))
            for t in range(3):
                dc_ref[:, t * WH + h * HD:t * WH + (h + 1) * HD] = dcs[t]
            dab, dal, ddt = dab + dcs[3], dal + dcs[4], ddt + dcs[5]
        dab_ref[...] = dab.astype(bf16)

        @pl.when(n == 0)
        def _():
            dal_ref[...] = dal
            ddt_ref[...] = ddt

        @pl.when(n > 0)
        def _():
            dal_ref[...] += dal
            ddt_ref[...] += ddt

    row = pl.BlockSpec((CH, WH), lambda n: (n, 0))
    wide = pl.BlockSpec((CH, 3 * WH), lambda n: (n, 0))
    par = pl.BlockSpec((1, HD), lambda n: (0, 0))
    return pl.pallas_call(
        body, name=name, grid=(nc,),
        in_specs=[wide, pl.BlockSpec((CH, HD), lambda n: (n, OFF_AB // HD)), par, par] + [row] * 5
        + [pl.BlockSpec((None, 8, WH), lambda n: (n, 0, 0))],
        out_specs=[wide, pl.BlockSpec((CH, HD), lambda n: (n, 0)), par, par],
        out_shape=[_sds((s, 3 * WH), f32), _sds((s, HD), bf16), _sds((1, HD), f32), _sds((1, HD), f32)],
        compiler_params=_cp(("arbitrary",)),
    )(c, proj, alog, dtb, *cots, ddec)


def gdn_b_fwd(u, w, qd, kd, qk, dec, name):
    s = u.shape[0]
    nc = s // CH

    def body(u_ref, w_ref, qd_ref, kd_ref, qk_ref, dec_ref, o_ref, st_ref, state):
        n = pl.program_id(0)

        @pl.when(n == 0)
        def _():
            state[...] = jnp.zeros_like(state)

        for h in range(NH):
            sl = slice(h * HD, (h + 1) * HD)
            st = state[sl, :]
            st_ref[sl, :] = st
            sb = st.astype(bf16)
            vn = u_ref[:, sl] - jnp.dot(w_ref[:, sl].astype(bf16), sb, preferred_element_type=f32)
            vb = vn.astype(bf16)
            o_ref[:, sl] = (jnp.dot(qd_ref[:, sl].astype(bf16), sb, preferred_element_type=f32)
                            + jnp.dot(qk_ref[:, sl].astype(bf16), vb, preferred_element_type=f32))
            state[sl, :] = st * dec_ref[0:1, sl] + lax.dot_general(
                kd_ref[:, sl].astype(bf16), vb, (((0,), (0,)), ((), ())), preferred_element_type=f32)

    row = pl.BlockSpec((CH, WH), lambda n: (n, 0))
    return pl.pallas_call(
        body, name=name, grid=(nc,),
        in_specs=[row] * 5 + [pl.BlockSpec((None, 8, WH), lambda n: (n, 0, 0))],
        out_specs=[row, pl.BlockSpec((None, WH, HD), lambda n: (n, 0, 0))],
        out_shape=[_sds((s, WH), f32), _sds((nc, WH, HD), f32)],
        scratch_shapes=[pltpu.VMEM((WH, HD), f32)],
        compiler_params=_cp(("arbitrary",)),
    )(u, w, qd, kd, qk, dec)


def gdn_b_bwd(u, w, qd, kd, qk, dec, states, do, name):
    s = u.shape[0]
    nc = s // CH

    def body(u_ref, w_ref, qd_ref, kd_ref, qk_ref, dec_ref, st_ref, do_ref,
             du_ref, dw_ref, dqd_ref, dkd_ref, dqk_ref, ddec_ref, dstate):
        n = pl.program_id(0)

        @pl.when(n == 0)
        def _():
            dstate[...] = jnp.zeros_like(dstate)

        for h in range(NH):
            sl = slice(h * HD, (h + 1) * HD)
            st, ds = st_ref[sl, :], dstate[sl, :]
            sb, dsb = st.astype(bf16), ds.astype(bf16)
            wb, qdb, kdb, qkb = (r[:, sl].astype(bf16) for r in (w_ref, qd_ref, kd_ref, qk_ref))
            dob = do_ref[:, sl].astype(bf16)
            vn = u_ref[:, sl] - jnp.dot(wb, sb, preferred_element_type=f32)
            vb = vn.astype(bf16)
            dvn = (lax.dot_general(qkb, dob, (((0,), (0,)), ((), ())), preferred_element_type=f32)
                   + jnp.dot(kdb, dsb, preferred_element_type=f32))
            dvb = dvn.astype(bf16)
            du_ref[:, sl] = dvn
            dw_ref[:, sl] = -_dot_nt(dvb, sb)
            dqd_ref[:, sl] = _dot_nt(dob, sb)
            dkd_ref[:, sl] = _dot_nt(vb, dsb)
            dqk_ref[:, sl] = _dot_nt(dob, vb)
            tot = jnp.sum(jnp.sum(ds * st, axis=1, keepdims=True), axis=0, keepdims=True)
            ddec_ref[:, sl] = jnp.broadcast_to(tot, (8, HD))
            dstate[sl, :] = (ds * dec_ref[0:1, sl]
                             + lax.dot_general(qdb, dob, (((0,), (0,)), ((), ())), preferred_element_type=f32)
                             - lax.dot_general(wb, dvb, (((0,), (0,)), ((), ())), preferred_element_type=f32))

    row = pl.BlockSpec((CH, WH), lambda n: (nc - 1 - n, 0))
    small = pl.BlockSpec((None, 8, WH), lambda n: (nc - 1 - n, 0, 0))
    return pl.pallas_call(
        body, name=name, grid=(nc,),
        in_specs=[row] * 5 + [small, pl.BlockSpec((None, WH, HD), lambda n: (nc - 1 - n, 0, 0)), row],
        out_specs=[row] * 5 + [small],
        out_shape=[_sds((s, WH), f32)] * 5 + [_sds((nc, 8, WH), f32)],
        scratch_shapes=[pltpu.VMEM((WH, HD), f32)],
        compiler_params=_cp(("arbitrary",)),
    )(u, w, qd, kd, qk, dec, states, do)


ANY = pl.BlockSpec(memory_space=pl.ANY)


def _dev_index(p):
    return 4 * p[0] + 2 * p[1] + p[2]


def _shard_of(ref, axis, size, idx):
    return ref.at[pl.ds(idx * size, size), :] if axis == 0 else ref.at[:, pl.ds(idx * size, size)]


def _peer(x, y, c, r):
    return (1 - x if r & 4 else x, 1 - y if r & 2 else y, 1 - c if r & 1 else c)


ALL_PEERS, SIBLING, OTHER_CHIPS = tuple(range(1, NDEV)), (1,), (4, 2, 6)


def _launch(body, ins, out_sds, name, sequencer_id, relations=ALL_PEERS, kinds=7):
    n = len(ins)
    sems = [pltpu.SemaphoreType.DMA((n, kinds)), pltpu.SemaphoreType.DMA((n, kinds)), pltpu.SemaphoreType.DMA((n,))]
    if sequencer_id is None:
        return pl.pallas_call(
            lambda *refs: body(refs[:n], refs[n:2 * n], *refs[2 * n:]), name=name, in_specs=[ANY] * n, out_specs=[ANY] * n,
            out_shape=out_sds, scratch_shapes=sems, compiler_params=pltpu.CompilerParams(has_side_effects=True),
        )(*ins)
    in_refs = [jax.new_ref(a, memory_space=pltpu.MemorySpace.HBM) for a in ins]
    out_refs = [jax.empty_ref(sd, memory_space=pltpu.MemorySpace.HBM) for sd in out_sds]

    @pl.kernel(mesh=plsc.ScalarSubcoreMesh(axis_name="sequencer", num_cores=1), name=name, scratch_types=sems,
               compiler_params=pltpu.CompilerParams(collective_id=sequencer_id))
    def launch(send, recv, loc):
        x, y, c = lax.axis_index("x"), lax.axis_index("y"), lax.axis_index("c")
        barrier = pltpu.get_barrier_semaphore()
        for r in relations:
            pl.semaphore_signal(barrier, inc=1, device_id=_peer(x, y, c, r), device_id_type=MESH)
        pl.semaphore_wait(barrier, len(relations))
        body(in_refs, out_refs, send, recv, loc)

    launch()
    return [r[...] for r in out_refs]


def all_gather(xs, axes, name, sequencer_id=None):
    n = len(xs)
    fulls = [tuple(d * (NDEV if a == ax else 1) for a, d in enumerate(x.shape)) for x, ax in zip(xs, axes)]
    halved = [x.shape[0] % 32 == 0 for x in xs]

    def body(x_refs, o_refs, send, recv, loc):
        x, y, c = lax.axis_index("x"), lax.axis_index("y"), lax.axis_index("c")
        me, sib = (x, y, c), (x, y, 1 - c)
        xn, yn, dg = (1 - x, y), (x, 1 - y), (1 - x, 1 - y)

        def part(t, p, half=None):
            ref = _shard_of(o_refs[t], axes[t], xs[t].shape[axes[t]], _dev_index(p))
            rows = xs[t].shape[0] // 2
            return ref if half is None else ref.at[pl.ds(half * rows, rows), :]

        def copy(t, k, block, to, half=None, src=None):
            return pltpu.make_async_remote_copy(
                src_ref=part(t, block, half) if src is None else src, dst_ref=part(t, block, half),
                send_sem=send.at[t, k], recv_sem=recv.at[t, k], device_id=to, device_id_type=MESH)

        mine = [pltpu.make_async_copy(x_refs[t], part(t, me), loc.at[t]) for t in range(n)]
        for cp in mine:
            cp.start()
        sends = []
        for t in range(n):
            sends += [copy(t, 0, me, sib, src=x_refs[t]), copy(t, 1, me, (*xn, c), src=x_refs[t]),
                      copy(t, 2, me, (*yn, c), src=x_refs[t])]
            if not halved[t]:
                sends.append(copy(t, 3, me, (*dg, c), src=x_refs[t]))
        for cp in sends:
            cp.start()

        def pass_on(cp):
            cp.start()
            sends.append(cp)

        for t in range(n):
            h0, h1 = (0, 1) if halved[t] else (None, None)
            copy(t, 1, (*xn, c), me).wait_recv()
            if halved[t]:
                pass_on(copy(t, 3, (*xn, c), (*yn, c), 0))
            pass_on(copy(t, 5, (*xn, c), sib))
            copy(t, 2, (*yn, c), me).wait_recv()
            if halved[t]:
                pass_on(copy(t, 4, (*yn, c), (*xn, c), 1))
            pass_on(copy(t, 6, (*yn, c), sib))
            copy(t, 3, (*dg, c), me, h0).wait_recv()
            pass_on(copy(t, 7, (*dg, c), sib, h0))
            if halved[t]:
                copy(t, 4, (*dg, c), me, h1).wait_recv()
                pass_on(copy(t, 8, (*dg, c), sib, h1))
        for t in range(n):
            h0, h1 = (0, 1) if halved[t] else (None, None)
            copy(t, 0, sib, me).wait_recv()
            copy(t, 5, (*xn, 1 - c), me).wait_recv()
            copy(t, 6, (*yn, 1 - c), me).wait_recv()
            copy(t, 7, (*dg, 1 - c), me, h0).wait_recv()
            if halved[t]:
                copy(t, 8, (*dg, 1 - c), me, h1).wait_recv()
        for cp in sends:
            cp.wait_send()
        for cp in mine:
            cp.wait()

    return _launch(body, xs, [_sds(f, x.dtype) for f, x in zip(fulls, xs)], name, sequencer_id, kinds=9)


def pair_swap(gs, axes, name, sequencer_id):
    n = len(gs)
    shards = [tuple(d // (NDEV if a == ax else 1) for a, d in enumerate(g.shape)) for g, ax in zip(gs, axes)]

    def body(g_refs, p_refs, send, recv, loc):
        x, y, c = lax.axis_index("x"), lax.axis_index("y"), lax.axis_index("c")

        def copy(t, j):
            return pltpu.make_async_remote_copy(
                src_ref=_shard_of(g_refs[t], axes[t], shards[t][axes[t]], 2 * j + (1 - c)), dst_ref=p_refs[t].at[j],
                send_sem=send.at[t, j], recv_sem=recv.at[t, j], device_id=(x, y, 1 - c), device_id_type=MESH)

        copies = [copy(t, j) for t in range(n) for j in range(4)]
        for cp in copies:
            cp.start()
        for cp in copies:
            cp.wait_recv()
        for cp in copies:
            cp.wait_send()

    return _launch(body, gs, [_sds((4,) + sh, g.dtype) for sh, g in zip(shards, gs)], name, sequencer_id, SIBLING)


def pair_add(g, p, axis, name, after=()):
    _, rows, cols = p.shape
    tr = _row_tile(rows, cols, 1 << 20)
    nb = rows // tr
    if axis == 0:
        g_spec = pl.BlockSpec((tr, cols), lambda j, i, c: ((2 * j + c[0]) * nb + i, 0))
    else:
        g_spec = pl.BlockSpec((tr, cols), lambda j, i, c: (i, 2 * j + c[0]))
    blk = pl.BlockSpec((None, tr, cols), lambda j, i, c: (j, i, 0))

    na = len(after)

    def body(c_ref, *refs):
        g_ref, p_ref, q_ref = refs[na:]
        q_ref[...] = (g_ref[...].astype(f32) + p_ref[...].astype(f32)).astype(bf16)

    return pl.pallas_call(
        body, name=name, out_shape=_sds(p.shape, bf16),
        grid_spec=pltpu.PrefetchScalarGridSpec(num_scalar_prefetch=1, grid=(4, nb), in_specs=[ANY] * na + [g_spec, blk],
                                               out_specs=blk),
        compiler_params=_cp(("parallel", "parallel")),
    )(lax.axis_index("c").astype(jnp.int32).reshape(1), *after, g, p)


def chip_exchange(qs, name, sequencer_id):
    n = len(qs)

    def body(q_refs, r_refs, send, recv, loc):
        x, y, c = lax.axis_index("x"), lax.axis_index("y"), lax.axis_index("c")
        my_chip = 2 * x + y
        peers = [_peer(x, y, c, r) for r in OTHER_CHIPS]
        mine = [pltpu.make_async_copy(q_refs[t].at[my_chip], r_refs[t].at[my_chip], loc.at[t]) for t in range(n)]
        for cp in mine:
            cp.start()

        def copy(t, k, slot):
            p = peers[k]
            return pltpu.make_async_remote_copy(
                src_ref=q_refs[t].at[2 * p[0] + p[1]], dst_ref=r_refs[t].at[slot], send_sem=send.at[t, k], recv_sem=recv.at[t, k],
                device_id=p, device_id_type=MESH)

        sends = [copy(t, k, my_chip) for t in range(n) for k in range(3)]
        for cp in sends:
            cp.start()
        for t in range(n):
            for k in range(3):
                copy(t, k, 2 * peers[k][0] + peers[k][1]).wait_recv()
        for cp in sends:
            cp.wait_send()
        for cp in mine:
            cp.wait()

    return _launch(body, qs, [_sds(q.shape, q.dtype) for q in qs], name, sequencer_id, OTHER_CHIPS)


def _adamw(w, g, m, v):
    m = ADAM_B1 * m + (1.0 - ADAM_B1) * g
    v = ADAM_B2 * v + (1.0 - ADAM_B2) * jnp.square(g)
    m_hat = m / (1.0 - ADAM_B1 ** ADAM_STEP)
    v_hat = v / (1.0 - ADAM_B2 ** ADAM_STEP)
    return -ADAM_LR * (m_hat / (jnp.sqrt(v_hat) + ADAM_EPS) + ADAM_WD * w), m, v


def _sum8(r_ref):
    g = r_ref[0].astype(f32)
    for j in range(1, r_ref.shape[0]):
        g = g + r_ref[j].astype(f32)
    return g


def _row_tile(rows, cols, elems=1 << 18):
    tr = min(rows, max(8, 1 << int(math.log2(elems / cols))))
    assert rows % tr == 0, (rows, cols)
    return tr


def sum_partials(r, name, after=()):
    _, rows, cols = r.shape
    tr = _row_tile(rows, cols)
    na = len(after)

    def body(*refs):
        refs[na + 1][...] = _sum8(refs[na])

    return pl.pallas_call(
        body, name=name, grid=(rows // tr,),
        in_specs=[ANY] * na + [pl.BlockSpec((r.shape[0], tr, cols), lambda i: (0, i, 0))],
        out_specs=pl.BlockSpec((tr, cols), lambda i: (i, 0)), out_shape=_sds((rows, cols), f32),
        compiler_params=_cp(("parallel",)),
    )(*after, r)


def adamw_t(w, m, v, grads, name):
    rows, nl, cols = w.shape
    tr = min(rows, (1 << 16) // cols)
    blk = pl.BlockSpec((tr, nl, cols), lambda i: (i, 0, 0))
    flat = pl.BlockSpec((tr, cols), lambda i: (i, 0))

    def body(w_ref, m_ref, v_ref, *rest):
        g_refs, (g_ref, d_ref, nm_ref, nv_ref) = rest[:nl], rest[nl:]
        for l in range(nl):
            grad = g_refs[l][...]
            g_ref[:, l, :] = grad
            d_ref[:, l, :], nm_ref[:, l, :], nv_ref[:, l, :] = _adamw(w_ref[:, l, :], grad, m_ref[:, l, :], v_ref[:, l, :])

    return tuple(pl.pallas_call(
        body, name=name, grid=(pl.cdiv(rows, tr),), in_specs=[blk] * 3 + [flat] * nl, out_specs=[blk] * 4,
        out_shape=[_sds(w.shape, f32)] * 4, compiler_params=_cp(("parallel",)),
    )(w, m, v, *grads))


def adamw(w, m, v, layer, name, r=None, g=None, prev=None):
    _, rows, cols = w.shape
    tr = _row_tile(rows, cols)
    blk = pl.BlockSpec((None, tr, cols), lambda i: (layer, i, 0))
    nprev = 0 if prev is None else 4

    def body(w_ref, m_ref, v_ref, src_ref, *rest):
        g_ref, d_ref, nm_ref, nv_ref, token_ref = rest[nprev:]
        grad = _sum8(src_ref) if g is None else src_ref[...]
        g_ref[...] = grad
        d_ref[...], nm_ref[...], nv_ref[...] = _adamw(w_ref[...], grad, m_ref[...], v_ref[...])
        token_ref[...] = jnp.zeros_like(token_ref)

    src, src_spec = (r, pl.BlockSpec((r.shape[0], tr, cols), lambda i: (0, i, 0))) if g is None else (g, pl.BlockSpec((tr, cols), lambda i: (i, 0)))
    *outs, token = pl.pallas_call(
        body, name=name, grid=(rows // tr,), in_specs=[blk] * 3 + [src_spec] + [ANY] * nprev,
        out_specs=[blk] * 4 + [pl.BlockSpec((8, HD), lambda i: (0, 0))],
        out_shape=[_sds(w.shape, f32)] * 4 + [_sds((8, HD), f32)], input_output_aliases={4 + k: k for k in range(nprev)},
        compiler_params=_cp(("arbitrary",)),
    )(w, m, v, src, *(prev or ()))
    return tuple(outs), token


def small_adamw(parts, w, m, v, name):
    def body(p_ref, w_ref, m_ref, v_ref, g_ref, d_ref, nm_ref, nv_ref):
        g = _sum8(p_ref)
        g_ref[...] = g
        d_ref[...], nm_ref[...], nv_ref[...] = _adamw(w_ref[...], g, m_ref[...], v_ref[...])

    return pl.pallas_call(body, name=name, out_shape=[_sds(w.shape, f32)] * 4, compiler_params=_cp())(parts, w, m, v)


def _pack_rows(wt):
    tail = jnp.pad(wt[5888:N_IN], ((OFF_G - 12 - (OFF_P + 512), 0), (0, 0)))
    return jnp.concatenate([wt[512:2816], wt[2816:5120], wt[5120:5888], wt[0:512], tail], axis=0)


def _unpack_rows(g):
    return jnp.concatenate([g[OFF_P:OFF_P + 512], g[OFF_SB:OFF_SB + 2304], g[OFF_GQ:OFF_GQ + 2304], g[OFF_Z:OFF_Z + 768],
                            g[OFF_G - 12:NP]], axis=0)


def _lanes(v):
    flat = v.reshape(-1)
    n = -(-flat.shape[0] // HD) * HD
    return jnp.pad(flat, (0, n - flat.shape[0])).reshape(n // HD, HD)


def _lanes8(v):
    rows = _lanes(v)
    return jnp.pad(rows, ((0, -rows.shape[0] % 8), (0, 0)))


def _layer_fwd(x, p, l):
    nm = lambda s: f"{s}_l{l}"
    u = rms_fwd(x, p["attn_norm"], nm("rms1"))
    proj = matmul(u, p["w_in"], name=nm("inproj"), tb=True)
    y_pool = pool_fwd(proj, p["pool_w"], p["pool_scale"], nm("pool"))
    y_sb = sb_fwd(proj, nm("sb"))
    c = conv_fwd(proj, p["conv"], nm("conv"))
    ga = gdn_a_fwd(c, proj, p["alog"], p["dtb"], nm("gdna"))
    o, states = gdn_b_fwd(*ga, nm("gdnb"))
    y_gdn = gdn_out_fwd(o, proj, p["gdn_norm"], nm("gdno"))
    ups = [matmul(y, p[k], name=nm(k)) for y, k in ((y_pool, "w_pool_up"), (y_sb, "w_sb_up"), (y_gdn, "w_gdn_up"))]
    merged = merge_fwd(proj, ups, nm("merge"))
    x1 = matmul(merged, p["w_out"], name=nm("outproj"), epilogue=lambda acc, r: acc + r, extras=(x,))
    u2 = rms_fwd(x1, p["mlp_norm"], nm("rms2"))
    h2 = matmul(u2, p["w_ff1"], name=nm("ff1"), out_dtype=bf16, epilogue=lambda acc: jnp.square(jnp.maximum(acc, 0.0)))
    x2 = matmul(h2, p["w_ff2"], name=nm("ff2"), epilogue=lambda acc, r: acc + r, extras=(x1,))
    saved = dict(x=x, u=u, proj=proj, y_pool=y_pool, y_sb=y_sb, c=c, ga=ga, o=o, states=states, y_gdn=y_gdn, ups=ups,
                 merged=merged, x1=x1, u2=u2, h2=h2)
    return x2, saved


def _layer_bwd(dx2, dx2b, sv, p, l, swap, finish):
    nm = lambda s: f"{s}_l{l}"
    s = dx2.shape[0]
    dh = matmul(dx2b, p["w_ff2"], name=nm("d_ff2_x"), tb=True, out_dtype=bf16,
                epilogue=lambda acc, h2: acc * (2.0 * jnp.sqrt(h2.astype(f32))), extras=(sv["h2"],))
    g_ff2 = matmul(sv["h2"], dx2b, name=nm("d_ff2_w"), ta=True, out_dtype=bf16)
    du2 = matmul(dh, p["w_ff1"], name=nm("d_ff1_x"), tb=True)
    g_ff1 = matmul(sv["u2"], dh, name=nm("d_ff1_w"), ta=True, out_dtype=bf16)
    dx1, dx1b, d_mlp_norm = rms_bwd(sv["x1"], du2, dx2, p["mlp_norm"], nm("d_rms2"))
    dmerged = matmul(dx1b, p["w_out"], name=nm("d_out_x"), tb=True, out_dtype=bf16)
    g_out = matmul(sv["merged"], dx1b, name=nm("d_out_w"), ta=True, out_dtype=bf16)
    dgates, *dups = merge_bwd(sv["proj"], sv["ups"], dmerged, nm("d_merge"))
    dys, g_ups = [], []
    for dup, y, k in zip(dups, (sv["y_pool"], sv["y_sb"], sv["y_gdn"]), ("w_pool_up", "w_sb_up", "w_gdn_up")):
        dys.append(matmul(dup, p[k], name=nm("d_" + k + "_x"), tb=True, out_dtype=bf16))
        g_ups.append(matmul(y, dup, name=nm("d_" + k + "_w"), ta=True, out_dtype=bf16))
    early = g_ups + [g_out, g_ff1, g_ff2]
    swapped = swap(early, BIG_AXES[1:], "a")
    do, dz, d_gdn_norm = gdn_out_bwd(sv["o"], sv["proj"], p["gdn_norm"], dys[2], nm("d_gdno"))
    cots = gdn_b_bwd(*sv["ga"], sv["states"], do, nm("d_gdnb"))
    dc, dab, d_alog, d_dtb = gdn_a_bwd(sv["c"], sv["proj"], p["alog"], p["dtb"], cots[:5], cots[5], nm("d_gdna"))
    r_early, sent_early = finish(early, swapped, BIG_AXES[1:], BIG_NAMES[1:], "a", (dab,))
    dgq, d_conv = conv_bwd(sv["proj"], p["conv"], dc, nm("d_conv"), sent_early)
    dq, dk, dv = sb_bwd(sv["proj"], dys[1], nm("d_sb"))
    dp, d_pool_w, d_pool_scale = pool_bwd(sv["proj"], p["pool_w"], p["pool_scale"], dys[0], nm("d_pool"))
    dproj = jnp.concatenate([dq, dk, dv, dgq, dz, dp, jnp.zeros((s, OFF_AB - OFF_P - W_POOL), bf16), dab, dgates], axis=1)
    g_in = matmul(dproj, sv["u"], name=nm("d_in_w"), ta=True, out_dtype=bf16)
    r_in, sent_in = finish([g_in], swap([g_in], BIG_AXES[:1], "b"), BIG_AXES[:1], BIG_NAMES[:1], "b", ())
    du = matmul(dproj, p["w_in"], name=nm("d_in_x"), after=sent_in)
    dx, dxb, d_attn_norm = rms_bwd(sv["x"], du, dx1, p["attn_norm"], nm("d_rms1"))
    recv = r_in + r_early
    small = [d_attn_norm, d_pool_w, d_pool_scale, d_conv, d_alog, d_dtb, d_gdn_norm, d_mlp_norm]
    return dx, dxb, recv, small


BIG_AXES = (1, 1, 1, 1, 0, 1, 0)
GATHER_ID, EXCHANGE_ID = 1, 2
BIG_NAMES = ("w_in", "w_pool_up", "w_sb_up", "w_gdn_up", "w_out", "w_ff1", "w_ff2")


def kernel(x, attn_norm, w_in, pool_w, pool_scale, gdn_conv, gdn_a_log, gdn_dt_bias, gdn_norm, w_pool_up, w_sb_up, w_gdn_up, w_out, mlp_norm, w_ff1, w_ff2, final_norm, loss_target, m_attn_norm, m_w_in, m_pool_w, m_pool_scale, m_gdn_conv, m_gdn_a_log, m_gdn_dt_bias, m_gdn_norm, m_w_pool_up, m_w_sb_up, m_w_gdn_up, m_w_out, m_mlp_norm, m_w_ff1, m_w_ff2, m_final_norm, v_attn_norm, v_w_in, v_pool_w, v_pool_scale, v_gdn_conv, v_gdn_a_log, v_gdn_dt_bias, v_gdn_norm, v_w_pool_up, v_w_sb_up, v_w_gdn_up, v_w_out, v_mlp_norm, v_w_ff1, v_w_ff2, v_final_norm):
    s = x.shape[1]
    me = _dev_index((lax.axis_index("x"), lax.axis_index("y"), lax.axis_index("c")))
    ncv = gdn_conv.shape[2]

    w_in_t, m_w_in_t, v_w_in_t = (jnp.transpose(a, (2, 0, 1)) for a in (w_in, m_w_in, v_w_in))
    full = []
    for l in range(NL):
        shards = [_pack_rows(w_in_t[:, l]).astype(bf16), w_pool_up[l].astype(bf16), w_sb_up[l].astype(bf16),
                  w_gdn_up[l].astype(bf16), w_out[l].astype(bf16), w_ff1[l].astype(bf16), w_ff2[l].astype(bf16)]
        if l == 0:
            first = all_gather([shards[0], gdn_conv.reshape(NL * 4, ncv)], (1, 0), "gather_first", sequencer_id=GATHER_ID)
            rest = all_gather(shards[1:], BIG_AXES[1:], "gather_rest_l0", sequencer_id=GATHER_ID)
            full.append([first[0]] + rest)
            conv_full = first[1].reshape(NDEV, NL, 4, ncv).transpose(1, 2, 0, 3).reshape(NL, 4, NDEV * ncv)
        else:
            full.append(all_gather(shards, BIG_AXES, f"gather_weights_l{l}", sequencer_id=GATHER_ID))
    params = []
    for l in range(NL):
        p = dict(zip(("w_in", "w_pool_up", "w_sb_up", "w_gdn_up", "w_out", "w_ff1", "w_ff2"), full[l][:7]))
        p.update(attn_norm=attn_norm[l][None], mlp_norm=mlp_norm[l][None], pool_w=pool_w[l], pool_scale=pool_scale[l][None],
                 conv=conv_full[l], alog=_lanes(gdn_a_log[l]), dtb=_lanes(gdn_dt_bias[l]), gdn_norm=gdn_norm[l][None])
        params.append(p)

    h = x[0]
    saved = []
    for l in range(NL):
        h, sv = _layer_fwd(h, params[l], l)
        saved.append(sv)
    dh, dhb, loss_row, d_final = loss_head(h, loss_target[0], final_norm[None], "loss_head")
    recv, smalls = [None] * NL, [None] * NL
    for l in reversed(range(NL)):
        def swap(gs, axes, tag, l=l):
            return pair_swap(gs, axes, f"swap_{tag}_l{l}", None)

        def finish(gs, ps, axes, names, tag, after, l=l):
            qs = [pair_add(g, p, ax, f"pair_add_{k}_l{l}", after) for g, p, ax, k in zip(gs, ps, axes, names)]
            return chip_exchange(qs, f"exchange_{tag}_l{l}", EXCHANGE_ID), tuple(qs)

        dh, dhb, recv[l], smalls[l] = _layer_bwd(dh, dhb, saved[l], params[l], l, swap, finish)

    small_rows = [_lanes8(t) for l in range(NL) for t in smalls[l]] + [_lanes8(d_final), _lanes8(loss_row)]
    packed = jnp.concatenate(small_rows, axis=0)
    parts = all_gather([packed], (0,), "gather_small")[0].reshape(NDEV, packed.shape[0], HD)

    def pack_small(tree):
        rows = []
        for l in range(NL):
            rows += [_lanes8(tree["attn_norm"][l]), _lanes8(tree["pool_w"][l]), _lanes8(tree["pool_scale"][l]),
                     jnp.zeros((4 * NDEV * ncv // HD, HD), f32), _lanes8(tree["gdn_a_log"][l]), _lanes8(tree["gdn_dt_bias"][l]),
                     _lanes8(tree["gdn_norm"][l]), _lanes8(tree["mlp_norm"][l])]
        rows += [_lanes8(tree["final_norm"]), jnp.zeros((8, HD), f32)]
        return jnp.concatenate(rows, axis=0)

    names = ("attn_norm", "pool_w", "pool_scale", "gdn_a_log", "gdn_dt_bias", "gdn_norm", "mlp_norm", "final_norm")
    w_small = pack_small(dict(zip(names, (attn_norm, pool_w, pool_scale, gdn_a_log, gdn_dt_bias, gdn_norm, mlp_norm, final_norm))))
    m_small = pack_small(dict(zip(names, (m_attn_norm, m_pool_w, m_pool_scale, m_gdn_a_log, m_gdn_dt_bias, m_gdn_norm, m_mlp_norm, m_final_norm))))
    v_small = pack_small(dict(zip(names, (v_attn_norm, v_pool_w, v_pool_scale, v_gdn_a_log, v_gdn_dt_bias, v_gdn_norm, v_mlp_norm, v_final_norm))))
    small_out = small_adamw(parts, w_small, m_small, v_small, "adamw_small")

    def unpack_small(buf):
        out, conv_g, r = {}, [], 0
        layer_items = (("attn_norm", (D,)), ("pool_w", (4, HD, HD)), ("pool_scale", (W_POOL,)), ("conv", (4, NDEV * ncv)),
                       ("gdn_a_log", (NH,)), ("gdn_dt_bias", (NH,)), ("gdn_norm", (HD,)), ("mlp_norm", (D,)))
        per_layer = {k: [] for k, _ in layer_items}
        for l in range(NL):
            for k, shape in layer_items:
                size = math.prod(shape)
                nrow = -(-size // (8 * HD)) * 8
                per_layer[k].append(buf[r:r + nrow].reshape(-1)[:size].reshape(shape))
                r += nrow
        for k, _ in layer_items:
            out[k] = jnp.stack(per_layer[k])
        out["final_norm"] = buf[r:r + D // HD].reshape(D)
        out["loss"] = buf[r + D // HD, 0]
        return out

    sm = [unpack_small(b) for b in small_out]
    loss = sm[0]["loss"]
    g_conv = lax.dynamic_slice_in_dim(sm[0]["conv"], me * ncv, ncv, axis=2)
    conv_out = None
    for l in reversed(range(NL)):
        conv_out, _ = adamw(gdn_conv, m_gdn_conv, v_gdn_conv, l, f"adamw_conv_l{l}", g=g_conv[l], prev=conv_out)

    big_out = {}
    big_names = BIG_NAMES
    big_w = dict(zip(big_names, ((w_in_t, m_w_in_t, v_w_in_t), (w_pool_up, m_w_pool_up, v_w_pool_up), (w_sb_up, m_w_sb_up, v_w_sb_up),
                                 (w_gdn_up, m_w_gdn_up, v_w_gdn_up), (w_out, m_w_out, v_w_out), (w_ff1, m_w_ff1, v_w_ff1),
                                 (w_ff2, m_w_ff2, v_w_ff2))))
    tokens = []
    for l, k in [(l, k) for l in reversed(range(NL)) for k in big_names[1:]]:
        w, m, v = big_w[k]
        big_out[k], token = adamw(w, m, v, l, f"adamw_{k}_l{l}", r=recv[l][big_names.index(k)], prev=big_out.get(k))
        tokens.append(token)
    g_in = [_unpack_rows(sum_partials(recv[l][0], f"sum_w_in_l{l}", tuple(tokens) if l == 0 else ())) for l in range(NL)]
    big_out["w_in"] = adamw_t(*big_w["w_in"], g_in, "adamw_w_in")

    def leaf(i, k):
        if k == "w_in":
            return jnp.transpose(big_out[k][i], (1, 2, 0))
        if k in big_out:
            return big_out[k][i]
        if k == "gdn_conv":
            return conv_out[i]
        return sm[i][k]

    order = ("attn_norm", "w_in", "pool_w", "pool_scale", "gdn_conv", "gdn_a_log", "gdn_dt_bias", "gdn_norm", "w_pool_up",
             "w_sb_up", "w_gdn_up", "w_out", "mlp_norm", "w_ff1", "w_ff2", "final_norm")
    return (loss, dh[None]) + tuple(leaf(i, k) for i in range(4) for k in order)
```

```python
import functools
import math

import jax
import jax.numpy as jnp
from jax import lax
from jax.experimental import pallas as pl
from jax.experimental.pallas import tpu as pltpu
from jax.experimental.pallas import tpu_sc as plsc

f32, bf16 = jnp.float32, jnp.bfloat16

D = 2048
NDEV = 8
NL = 2
HD = 128
NH = 6
WH = NH * HD
W_POOL = 512
EPS = 1e-6
N_IN = 12044
NP = 12288
OFF_SB, OFF_GQ, OFF_Z, OFF_P, OFF_AB, OFF_G = 0, 2304, 4608, 5376, 6016, 6144
AB_LANE = HD - 2 * NH
POOL_WINDOWS = (2, 4, 8, 16)
CH = 128
TQ = 256
VMEM_LIMIT = 56 * 1024 * 1024
ADAM_LR, ADAM_B1, ADAM_B2, ADAM_EPS, ADAM_WD, ADAM_STEP = 0.001, 0.9, 0.999, 1e-08, 0.01, 10
MESH = pl.DeviceIdType.MESH


def _cp(sem=None):
    return pltpu.CompilerParams(dimension_semantics=sem, vmem_limit_bytes=VMEM_LIMIT)


def _sds(shape, dtype):
    return jax.ShapeDtypeStruct(tuple(shape), dtype)


def matmul(a, b, *, name, ta=False, tb=False, out_dtype=f32, tm=1024, tn=1024, tk=2048, epilogue=None, extras=(), after=()):
    m, k = (a.shape[1], a.shape[0]) if ta else a.shape
    n = b.shape[0] if tb else b.shape[1]
    assert k == (b.shape[1] if tb else b.shape[0]) and a.dtype == bf16 and b.dtype == bf16
    tm, tn, tk = min(tm, m), min(tn, n), min(tk, k)
    assert m % tm == 0 and n % tn == 0 and k % tk == 0, (m, n, k, tm, tn, tk)
    nk = k // tk
    a_spec = pl.BlockSpec((tk, tm), lambda i, j, q: (q, i)) if ta else pl.BlockSpec((tm, tk), lambda i, j, q: (i, q))
    b_spec = pl.BlockSpec((tn, tk), lambda i, j, q: (j, q)) if tb else pl.BlockSpec((tk, tn), lambda i, j, q: (q, j))
    e_specs = [pl.BlockSpec((tm, tn), lambda i, j, q: (i, j)) for _ in extras]
    dn = (((0 if ta else 1,), (1 if tb else 0,)), ((), ()))
    ne = len(extras)

    def body(*refs):
        a_ref, b_ref, *rest = refs[len(after):]
        e_refs, o_ref = rest[:ne], rest[ne]
        part = lax.dot_general(a_ref[...], b_ref[...], dn, preferred_element_type=f32)

        def finish(acc):
            if epilogue is not None:
                acc = epilogue(acc, *[e[...] for e in e_refs])
            o_ref[...] = acc.astype(out_dtype)

        if nk == 1:
            finish(part)
        else:
            acc_ref = rest[ne + 1]
            q = pl.program_id(2)

            @pl.when(q == 0)
            def _():
                acc_ref[...] = part

            @pl.when(jnp.logical_and(q > 0, q < nk - 1))
            def _():
                acc_ref[...] += part

            @pl.when(q == nk - 1)
            def _():
                finish(acc_ref[...] + part)

    return pl.pallas_call(
        body, name=name, grid=(m // tm, n // tn, nk),
        in_specs=[pl.BlockSpec(memory_space=pl.ANY)] * len(after) + [a_spec, b_spec] + e_specs,
        out_specs=pl.BlockSpec((tm, tn), lambda i, j, q: (i, j)), out_shape=_sds((m, n), out_dtype),
        scratch_shapes=[pltpu.VMEM((tm, tn), f32)] if nk > 1 else [],
        compiler_params=_cp(("parallel", "parallel", "arbitrary")),
    )(*after, a, b, *extras)


def rowwise(name, fn, rows, params, outs, sums=(), tr=256, after=()):
    s = rows[0][0].shape[0]
    tr = min(tr, s)
    nin, nout = len(rows) + len(params), len(outs)
    in_specs = [pl.BlockSpec((tr, w), functools.partial(lambda i, c: (i, c), c=c)) for (_, w, c) in rows]
    in_specs += [pl.BlockSpec(p.shape, lambda i: (0, 0)) for p in params]
    out_specs = [pl.BlockSpec((tr, w), lambda i: (i, 0)) for (w, _) in outs]
    out_specs += [pl.BlockSpec(sh, lambda i: (0, 0)) for sh in sums]
    out_shape = [_sds((s, w), dt) for (w, dt) in outs] + [_sds(sh, f32) for sh in sums]

    def body(*refs):
        refs = refs[len(after):]
        res = fn(*[r[...] for r in refs[:nin]])
        for r, v in zip(refs[nin:nin + nout], res[:nout]):
            r[...] = v.astype(r.dtype)
        i = pl.program_id(0)
        for r, v in zip(refs[nin + nout:], res[nout:]):
            @pl.when(i == 0)
            def _(r=r, v=v):
                r[...] = v

            @pl.when(i > 0)
            def _(r=r, v=v):
                r[...] += v

    res = pl.pallas_call(
        body, name=name, grid=(s // tr,), in_specs=[pl.BlockSpec(memory_space=pl.ANY)] * len(after) + in_specs,
        out_specs=out_specs, out_shape=out_shape, compiler_params=_cp(("arbitrary",)),
    )(*after, *[r[0] for r in rows], *params)
    return res


def _rms(x, g):
    return x * lax.rsqrt(jnp.mean(x * x, axis=-1, keepdims=True) + EPS) * g


def rms_fwd(x, g, name):
    return rowwise(name, lambda xb, gb: (_rms(xb, gb),), [(x, D, 0)], [g], [(D, bf16)])[0]


def rms_bwd(x, du, dres, g, name, after=()):
    def fn(xb, dub, drb, gb):
        _, vjp = jax.vjp(_rms, xb, gb)
        dx, dg = vjp(dub.astype(f32))
        return drb + dx, drb + dx, dg

    return rowwise(name, fn, [(x, D, 0), (du, D, 0), (dres, D, 0)], [g], [(D, f32), (D, bf16)], [(1, D)], after=after)


def _merge(gates, up_p, up_s, up_g):
    sg = jax.nn.sigmoid(gates)
    return sg[:, :D] * up_p + sg[:, D:2 * D] * up_s + sg[:, 2 * D:] * up_g


def merge_fwd(proj, ups, name):
    return rowwise(name, lambda g, a, b, c: (_merge(g, a, b, c),),
                   [(proj, 3 * D, OFF_G // (3 * D))] + [(u, D, 0) for u in ups], [], [(D, bf16)], tr=128)[0]


def merge_bwd(proj, ups, dmerged, name):
    def fn(g, a, b, c, dm):
        _, vjp = jax.vjp(_merge, g, a, b, c)
        return vjp(dm.astype(f32))

    return rowwise(name, fn, [(proj, 3 * D, OFF_G // (3 * D))] + [(u, D, 0) for u in ups] + [(dmerged, D, 0)], [],
                   [(3 * D, bf16), (D, bf16), (D, bf16), (D, bf16)], tr=128)


def _gdn_out(o, z, g):
    ys = []
    for h in range(NH):
        sl = slice(h * HD, (h + 1) * HD)
        ys.append(_rms(o[:, sl], g) * jax.nn.silu(z[:, sl]))
    return jnp.concatenate(ys, axis=1)


def gdn_out_fwd(o, proj, g, name):
    return rowwise(name, lambda ob, zb, gb: (_gdn_out(ob, zb, gb),), [(o, WH, 0), (proj, WH, OFF_Z // WH)], [g],
                   [(WH, bf16)])[0]


def gdn_out_bwd(o, proj, g, dy, name):
    def fn(ob, zb, dyb, gb):
        _, vjp = jax.vjp(_gdn_out, ob, zb, gb)
        return vjp(dyb.astype(f32))

    return rowwise(name, fn, [(o, WH, 0), (proj, WH, OFF_Z // WH), (dy, WH, 0)], [g], [(WH, f32), (WH, bf16)],
                   [(1, HD)])


def loss_head(x, target, g, name):
    def loss_fn(xb, gb, tb):
        err = _rms(xb, gb) - tb
        return (0.5 / D) * jnp.sum(jnp.sum(err * err, axis=1, keepdims=True), axis=0, keepdims=True)

    def fn(xb, tb, gb):
        val, vjp = jax.vjp(functools.partial(loss_fn, tb=tb), xb, gb)
        dx, dg = vjp(jnp.ones((1, 1), f32))
        return dx, dx, jnp.broadcast_to(val, (1, HD)), dg

    return rowwise(name, fn, [(x, D, 0), (target, D, 0)], [g], [(D, f32), (D, bf16)], [(1, HD), (1, D)])


PB = 256


def _split(v):
    hi = v.astype(bf16)
    return hi, (v - hi.astype(f32)).astype(bf16)


def _band_dot(make_band, v, s, forward):
    hi, lo = _split(v)
    nb = s // PB
    outs = []
    for r in range(nb):
        lo_r = max(r - 1, 0) if forward else r
        hi_r = r + 1 if forward else min(r + 2, nb)
        band = make_band(r * PB, lo_r * PB, (hi_r - lo_r) * PB)
        sl = slice(lo_r * PB, hi_r * PB)
        outs.append(jnp.dot(band, hi[sl], preferred_element_type=f32) + jnp.dot(band, lo[sl], preferred_element_type=f32))
    return jnp.concatenate(outs, axis=0)


def _pool_common(p, win, s):
    def band(row0, col0, ncol):
        t = row0 + lax.broadcasted_iota(jnp.int32, (PB, ncol), 0)
        u = col0 + lax.broadcasted_iota(jnp.int32, (PB, ncol), 1)
        return jnp.logical_and(u <= t, t < u + win).astype(bf16)

    def band_t(row0, col0, ncol):
        u = row0 + lax.broadcasted_iota(jnp.int32, (PB, ncol), 0)
        t = col0 + lax.broadcasted_iota(jnp.int32, (PB, ncol), 1)
        return jnp.logical_and(u <= t, t < u + win).astype(bf16)

    t = lax.broadcasted_iota(jnp.int32, (s, 1), 0)
    inv_n = 1.0 / jnp.minimum(t + 1, win).astype(f32)
    d = _band_dot(band, p, s, True) * inv_n - p
    return d, inv_n, band_t


def pool_fwd(proj, pool_w, pool_scale, name):
    s = proj.shape[0]

    def body(p_ref, w_ref, sc_ref, y_ref):
        win = jnp.left_shift(2, pl.program_id(0))
        d, _, _ = _pool_common(p_ref[...], win, s)
        y = jnp.dot(d.astype(bf16), w_ref[...].astype(bf16), preferred_element_type=f32) * sc_ref[...]
        y_ref[...] = y.astype(bf16)

    return pl.pallas_call(
        body, name=name, grid=(4,),
        in_specs=[pl.BlockSpec((s, HD), lambda g: (0, OFF_P // HD + g)), pl.BlockSpec((None, HD, HD), lambda g: (g, 0, 0)),
                  pl.BlockSpec((1, HD), lambda g: (0, g))],
        out_specs=pl.BlockSpec((s, HD), lambda g: (0, g)), out_shape=_sds((s, W_POOL), bf16),
        compiler_params=_cp(("arbitrary",)),
    )(proj, pool_w, pool_scale)


def pool_bwd(proj, pool_w, pool_scale, dy, name):
    s = proj.shape[0]

    def body(p_ref, w_ref, sc_ref, dy_ref, dp_ref, dw_ref, dsc_ref):
        win = jnp.left_shift(2, pl.program_id(0))
        d, inv_n, band_t = _pool_common(p_ref[...], win, s)
        w = w_ref[...].astype(bf16)
        dyf = dy_ref[...].astype(f32)
        dsc_ref[...] = jnp.sum(dyf * jnp.dot(d.astype(bf16), w, preferred_element_type=f32), axis=0, keepdims=True)
        dys = (dyf * sc_ref[...]).astype(bf16)
        dd = lax.dot_general(dys, w, (((1,), (1,)), ((), ())), preferred_element_type=f32)
        dw_ref[...] = lax.dot_general(d.astype(bf16), dys, (((0,), (0,)), ((), ())), preferred_element_type=f32)
        dp_ref[...] = (_band_dot(band_t, dd * inv_n, s, False) - dd).astype(bf16)

    return pl.pallas_call(
        body, name=name, grid=(4,),
        in_specs=[pl.BlockSpec((s, HD), lambda g: (0, OFF_P // HD + g)), pl.BlockSpec((None, HD, HD), lambda g: (g, 0, 0)),
                  pl.BlockSpec((1, HD), lambda g: (0, g)), pl.BlockSpec((s, HD), lambda g: (0, g))],
        out_specs=[pl.BlockSpec((s, HD), lambda g: (0, g)), pl.BlockSpec((None, HD, HD), lambda g: (g, 0, 0)),
                   pl.BlockSpec((1, HD), lambda g: (0, g))],
        out_shape=[_sds((s, W_POOL), bf16), _sds((4, HD, HD), f32), _sds((1, W_POOL), f32)],
        compiler_params=_cp(("arbitrary",)),
    )(proj, pool_w, pool_scale, dy)


def _sb_tile(q, k_ref, kb, qi, carry):
    k = k_ref[pl.ds(pl.multiple_of(kb * TQ, TQ), TQ), :].astype(bf16)
    z = lax.dot_general(q, k, (((1,), (1,)), ((), ())), preferred_element_type=f32)
    row = qi * TQ + lax.broadcasted_iota(jnp.int32, (TQ, TQ), 0)
    col = kb * TQ + lax.broadcasted_iota(jnp.int32, (TQ, TQ), 1)
    mask = col < row
    ls = jnp.where(mask, jax.nn.log_sigmoid(-z), 0.0)
    j = lax.broadcasted_iota(jnp.int32, (TQ, TQ), 0)
    u = lax.broadcasted_iota(jnp.int32, (TQ, TQ), 1)
    later = (j > u).astype(bf16)
    hi, lo = _split(ls)
    lw = jnp.dot(hi, later, preferred_element_type=f32) + jnp.dot(lo, later, preferred_element_type=f32)
    a = jnp.where(mask, jnp.exp(ls + z + lw + carry), 0.0)
    return z, mask, ls, a


def sb_fwd(proj, name):
    s = proj.shape[0]
    nq = s // TQ
    scale = HD ** -0.5

    def body(q_ref, k_ref, v_ref, y_ref):
        qi = pl.program_id(1)
        q = (q_ref[...] * scale).astype(bf16)

        def step(j, c):
            acc, carry = c
            kb = qi - j
            _, _, ls, a = _sb_tile(q, k_ref, kb, qi, carry)
            v = v_ref[pl.ds(pl.multiple_of(kb * TQ, TQ), TQ), :].astype(bf16)
            acc = acc + jnp.dot(a.astype(bf16), v, preferred_element_type=f32)
            return acc, carry + jnp.sum(ls, axis=1, keepdims=True)

        acc, _ = lax.fori_loop(0, qi + 1, step, (jnp.zeros((TQ, HD), f32), jnp.zeros((TQ, 1), f32)))
        y_ref[...] = acc.astype(bf16)

    c0 = OFF_SB // HD
    return pl.pallas_call(
        body, name=name, grid=(NH, nq),
        in_specs=[pl.BlockSpec((TQ, HD), lambda h, i: (i, c0 + h)), pl.BlockSpec((s, HD), lambda h, i: (0, c0 + NH + h)),
                  pl.BlockSpec((s, HD), lambda h, i: (0, c0 + 2 * NH + h))],
        out_specs=pl.BlockSpec((TQ, HD), lambda h, i: (i, h)), out_shape=_sds((s, WH), bf16),
        compiler_params=_cp(("arbitrary", "arbitrary")),
    )(proj, proj, proj)


def sb_bwd(proj, dy, name):
    s = proj.shape[0]
    nq = s // TQ
    scale = HD ** -0.5

    def body(q_ref, k_ref, v_ref, do_ref, dq_ref, dk_ref, dv_ref, e_scr, z_scr, dk_acc, dv_acc):
        qi = pl.program_id(1)
        q = (q_ref[...] * scale).astype(bf16)
        do = do_ref[...]

        @pl.when(qi == 0)
        def _():
            dk_acc[...] = jnp.zeros_like(dk_acc)
            dv_acc[...] = jnp.zeros_like(dv_acc)

        def sweep_back(j, carry):
            kb = qi - j
            rows = pl.ds(pl.multiple_of(kb * TQ, TQ), TQ)
            z, _, ls, a = _sb_tile(q, k_ref, kb, qi, carry)
            v = v_ref[rows, :].astype(bf16)
            da = lax.dot_general(do, v, (((1,), (1,)), ((), ())), preferred_element_type=f32)
            e_scr[kb] = da * a
            z_scr[kb] = z
            dv_acc[rows, :] += lax.dot_general(a.astype(bf16), do, (((0,), (0,)), ((), ())), preferred_element_type=f32)
            return carry + jnp.sum(ls, axis=1, keepdims=True)

        lax.fori_loop(0, qi + 1, sweep_back, jnp.zeros((TQ, 1), f32))

        def sweep_fwd(kb, c):
            dq, carry = c
            rows = pl.ds(pl.multiple_of(kb * TQ, TQ), TQ)
            e, z = e_scr[kb], z_scr[kb]
            row = qi * TQ + lax.broadcasted_iota(jnp.int32, (TQ, TQ), 0)
            col = kb * TQ + lax.broadcasted_iota(jnp.int32, (TQ, TQ), 1)
            j = lax.broadcasted_iota(jnp.int32, (TQ, TQ), 0)
            u = lax.broadcasted_iota(jnp.int32, (TQ, TQ), 1)
            earlier = (j < u).astype(bf16)
            hi, lo = _split(e)
            ew = jnp.dot(hi, earlier, preferred_element_type=f32) + jnp.dot(lo, earlier, preferred_element_type=f32)
            sig = jax.nn.sigmoid(z)
            dz = jnp.where(col < row, e * (1.0 - sig) - (ew + carry) * sig, 0.0).astype(bf16)
            k = k_ref[rows, :].astype(bf16)
            dq = dq + jnp.dot(dz, k, preferred_element_type=f32)
            dk_acc[rows, :] += lax.dot_general(dz, q, (((0,), (0,)), ((), ())), preferred_element_type=f32)
            return dq, carry + jnp.sum(e, axis=1, keepdims=True)

        dq, _ = lax.fori_loop(0, qi + 1, sweep_fwd, (jnp.zeros((TQ, HD), f32), jnp.zeros((TQ, 1), f32)))
        dq_ref[...] = (dq * scale).astype(bf16)

        @pl.when(qi == nq - 1)
        def _():
            dk_ref[...] = dk_acc[...].astype(bf16)
            dv_ref[...] = dv_acc[...].astype(bf16)

    c0 = OFF_SB // HD
    return pl.pallas_call(
        body, name=name, grid=(NH, nq),
        in_specs=[pl.BlockSpec((TQ, HD), lambda h, i: (i, c0 + h)), pl.BlockSpec((s, HD), lambda h, i: (0, c0 + NH + h)),
                  pl.BlockSpec((s, HD), lambda h, i: (0, c0 + 2 * NH + h)), pl.BlockSpec((TQ, HD), lambda h, i: (i, h))],
        out_specs=[pl.BlockSpec((TQ, HD), lambda h, i: (i, h)), pl.BlockSpec((s, HD), lambda h, i: (0, h)),
                   pl.BlockSpec((s, HD), lambda h, i: (0, h))],
        out_shape=[_sds((s, WH), bf16)] * 3,
        scratch_shapes=[pltpu.VMEM((nq, TQ, TQ), f32), pltpu.VMEM((nq, TQ, TQ), f32), pltpu.VMEM((s, HD), f32),
                        pltpu.VMEM((s, HD), f32)],
        compiler_params=_cp(("arbitrary", "arbitrary")),
    )(proj, proj, proj, dy)


CB = 256


def _shift_down(v, k, s):
    if k == 0:
        return v
    row = lax.broadcasted_iota(jnp.int32, v.shape, 0)
    return jnp.where(row < k, 0.0, pltpu.roll(v, k, axis=0))


def _shift_up(v, k, s):
    if k == 0:
        return v
    row = lax.broadcasted_iota(jnp.int32, v.shape, 0)
    return jnp.where(row >= s - k, 0.0, pltpu.roll(v, s - k, axis=0))


def conv_fwd(proj, w, name):
    s = proj.shape[0]

    def body(x_ref, w_ref, y_ref):
        x, wv = x_ref[...], w_ref[...]
        y = sum(wv[3 - k:4 - k, :] * _shift_down(x, k, s) for k in range(4))
        y_ref[...] = jax.nn.silu(y)

    return pl.pallas_call(
        body, name=name, grid=(3 * WH // CB,),
        in_specs=[pl.BlockSpec((s, CB), lambda j: (0, OFF_GQ // CB + j)), pl.BlockSpec((4, CB), lambda j: (0, j))],
        out_specs=pl.BlockSpec((s, CB), lambda j: (0, j)), out_shape=_sds((s, 3 * WH), f32),
        compiler_params=_cp(("parallel",)),
    )(proj, w)


def conv_bwd(proj, w, dc, name, after=()):
    s = proj.shape[0]

    def body(*refs):
        x_ref, w_ref, dc_ref, dx_ref, dw_ref = refs[len(after):]
        x, wv = x_ref[...], w_ref[...]
        xs = [_shift_down(x, k, s) for k in range(4)]
        y = sum(wv[3 - k:4 - k, :] * xs[k] for k in range(4))
        sig = jax.nn.sigmoid(y)
        dy = dc_ref[...] * (sig * (1.0 + y * (1.0 - sig)))
        dx_ref[...] = sum(wv[3 - k:4 - k, :] * _shift_up(dy, k, s) for k in range(4)).astype(bf16)
        dw_ref[...] = jnp.concatenate([jnp.sum(dy * xs[3 - i], axis=0, keepdims=True) for i in range(4)], axis=0)

    return pl.pallas_call(
        body, name=name, grid=(3 * WH // CB,),
        in_specs=[pl.BlockSpec(memory_space=pl.ANY)] * len(after)
        + [pl.BlockSpec((s, CB), lambda j: (0, OFF_GQ // CB + j)), pl.BlockSpec((4, CB), lambda j: (0, j)),
           pl.BlockSpec((s, CB), lambda j: (0, j))],
        out_specs=[pl.BlockSpec((s, CB), lambda j: (0, j)), pl.BlockSpec((4, CB), lambda j: (0, j))],
        out_shape=[_sds((s, 3 * WH), bf16), _sds((4, 3 * WH), f32)],
        compiler_params=_cp(("parallel",)),
    )(*after, proj, w, dc)


SOLVE_PASSES = 3


def _pdot_impl(a, b, dn, passes):
    ah, al = _split(a)
    bh, bl = _split(b)
    dot = lambda p, q: lax.dot_general(p, q, dn, preferred_element_type=f32)
    if passes == 1:
        return dot(ah, bh)
    if passes == 2:
        return dot(ah, bh) + dot(ah, bl)
    return dot(ah, bh) + (dot(ah, bl) + dot(al, bh))


BNN, BNT, BTN = (((2,), (1,)), ((0,), (0,))), (((2,), (2,)), ((0,), (0,))), (((1,), (1,)), ((0,), (0,)))


@functools.partial(jax.custom_vjp, nondiff_argnums=(2,))
def _bdot(a, b, passes):
    return _pdot_impl(a, b, BNN, passes)


def _bdot_fwd(a, b, passes):
    return _pdot_impl(a, b, BNN, passes), (a, b)


def _bdot_bwd(passes, res, ct):
    a, b = res
    return _pdot_impl(ct, b, BNT, passes), _pdot_impl(a, ct, BTN, passes)


_bdot.defvjp(_bdot_fwd, _bdot_bwd)


@functools.partial(jax.custom_vjp, nondiff_argnums=(2,))
def _bdot_nt(a, b, passes):
    return _pdot_impl(a, b, BNT, passes)


def _bdot_nt_fwd(a, b, passes):
    return _pdot_impl(a, b, BNT, passes), (a, b)


def _bdot_nt_bwd(passes, res, ct):
    a, b = res
    return _pdot_impl(ct, b, BNN, passes), _pdot_impl(ct, a, BTN, passes)


_bdot_nt.defvjp(_bdot_nt_fwd, _bdot_nt_bwd)


def _lane_pick(v, h):
    lane = lax.broadcasted_iota(jnp.int32, v.shape, v.ndim - 1)
    return jnp.sum(jnp.where(lane == h, v, 0.0), axis=-1, keepdims=True)


def _stack(parts):
    return jnp.concatenate([p[None] for p in parts], axis=0)


def _heads(v, first):
    return _stack([_lane_pick(v, first + h) for h in range(NH)])


def _l2n(v):
    return v * lax.rsqrt(jnp.sum(v * v, axis=-1, keepdims=True) + EPS)


def _dot_nt(a, b):
    return lax.dot_general(a, b, (((1,), (1,)), ((), ())), preferred_element_type=f32)


def _gdn_chunk(cq, ck, cv, ab, alog, dtb):
    ones = jnp.ones((NH, CH, HD), f32)
    q = _l2n(cq) * (HD ** -0.5)
    k = _l2n(ck)
    la = -jnp.exp(_heads(alog, 0)) * jax.nn.softplus(_heads(ab, AB_LANE) + _heads(dtb, 0))
    beta = jax.nn.sigmoid(_heads(ab, AB_LANE + NH)) * ones
    i = lax.broadcasted_iota(jnp.int32, (CH, CH), 0)
    j = lax.broadcasted_iota(jnp.int32, (CH, CH), 1)
    incl, strict = j <= i, j < i
    eye = (i == j).astype(f32)
    g = _bdot(jnp.broadcast_to(incl.astype(f32), (NH, CH, CH)), la * ones, 2)
    g_row = _stack([g[h].T for h in range(NH)])
    gamma = jnp.where(incl, jnp.exp(jnp.where(incl, g - g_row, 0.0)), 0.0)
    lower = jnp.where(strict, beta * _bdot_nt(k, k, 1) * gamma, 0.0)
    inv = eye - lower
    pw = _bdot(lower, lower, SOLVE_PASSES)
    for m in range(1, int(math.log2(CH))):
        inv = inv + _bdot(inv, pw, SOLVE_PASSES)
        if m < int(math.log2(CH)) - 1:
            pw = _bdot(pw, pw, SOLVE_PASSES)
    eg = jnp.exp(g)
    u = _bdot(inv, cv * beta, SOLVE_PASSES)
    w = _bdot(inv, k * (beta * eg), SOLVE_PASSES)
    qk = _bdot_nt(q, k, 1) * gamma
    g_last = g[:, CH - 1:CH, :]
    return u, w, q * eg, k * jnp.exp(g_last - g), qk, jnp.exp(g_last)


def _by_head(ref, t=0):
    return _stack([ref[:, t * WH + h * HD:t * WH + (h + 1) * HD] for h in range(NH)])


def gdn_a_fwd(c, proj, alog, dtb, name):
    s = c.shape[0]
    nc = s // CH

    def body(c_ref, ab_ref, al_ref, dt_ref, u_ref, w_ref, qd_ref, kd_ref, qk_ref, dec_ref):
        res = _gdn_chunk(_by_head(c_ref, 0), _by_head(c_ref, 1), _by_head(c_ref, 2), ab_ref[...], al_ref[...], dt_ref[...])
        for h in range(NH):
            sl = slice(h * HD, (h + 1) * HD)
            for r, v in zip((u_ref, w_ref, qd_ref, kd_ref, qk_ref), res[:5]):
                r[:, sl] = v[h]
            dec_ref[:, sl] = jnp.broadcast_to(res[5][h], (8, HD))

    row = pl.BlockSpec((CH, WH), lambda n: (n, 0))
    par = pl.BlockSpec((1, HD), lambda n: (0, 0))
    return pl.pallas_call(
        body, name=name, grid=(nc,),
        in_specs=[pl.BlockSpec((CH, 3 * WH), lambda n: (n, 0)), pl.BlockSpec((CH, HD), lambda n: (n, OFF_AB // HD)), par, par],
        out_specs=[row] * 5 + [pl.BlockSpec((None, 8, WH), lambda n: (n, 0, 0))],
        out_shape=[_sds((s, WH), f32)] * 5 + [_sds((nc, 8, WH), f32)],
        compiler_params=_cp(("parallel",)),
    )(c, proj, alog, dtb)


def gdn_a_bwd(c, proj, alog, dtb, cots, ddec, name):
    s = c.shape[0]
    nc = s // CH

    def body(c_ref, ab_ref, al_ref, dt_ref, du_ref, dw_ref, dqd_ref, dkd_ref, dqk_ref, ddec_ref,
             dc_ref, dab_ref, dal_ref, ddt_ref):
        n = pl.program_id(0)
        _, vjp = jax.vjp(_gdn_chunk, _by_head(c_ref, 0), _by_head(c_ref, 1), _by_head(c_ref, 2), ab_ref[...], al_ref[...],
                         dt_ref[...])
        lane = lax.broadcasted_iota(jnp.int32, (1, HD), 1)
        dd = _stack([jnp.where(lane == 0, ddec_ref[0:1, h * HD:(h + 1) * HD], 0.0) for h in range(NH)])
        dcq, dck, dcv, dab, dal, ddt = vjp(tuple(_by_head(r) for r in (du_ref, dw_ref, dqd_ref, dkd_ref, dqk_ref)) + (dd,))
        for h in range(NH):
            for t, v in enumerate((dcq, dck, dcv)):
                dc_ref[:, t * WH + h * HD:t * WH + (h + 1) * HD] = v[h]
        dab_ref[...] = dab.astype(bf16)

        @pl.when(n == 0)
        def _():
            dal_ref[...] = dal
            ddt_ref[...] = ddt

        @pl.when(n > 0)
        def _():
            dal_ref[...] += dal
            ddt_ref[...] += ddt

    row = pl.BlockSpec((CH, WH), lambda n: (n, 0))
    wide = pl.BlockSpec((CH, 3 * WH), lambda n: (n, 0))
    par = pl.BlockSpec((1, HD), lambda n: (0, 0))
    return pl.pallas_call(
        body, name=name, grid=(nc,),
        in_specs=[wide, pl.BlockSpec((CH, HD), lambda n: (n, OFF_AB // HD)), par, par] + [row] * 5
        + [pl.BlockSpec((None, 8, WH), lambda n: (n, 0, 0))],
        out_specs=[wide, pl.BlockSpec((CH, HD), lambda n: (n, 0)), par, par],
        out_shape=[_sds((s, 3 * WH), f32), _sds((s, HD), bf16), _sds((1, HD), f32), _sds((1, HD), f32)],
        compiler_params=_cp(("arbitrary",)),
    )(c, proj, alog, dtb, *cots, ddec)


def gdn_b_fwd(u, w, qd, kd, qk, dec, name):
    s = u.shape[0]
    nc = s // CH

    def body(u_ref, w_ref, qd_ref, kd_ref, qk_ref, dec_ref, o_ref, st_ref, state):
        n = pl.program_id(0)

        @pl.when(n == 0)
        def _():
            state[...] = jnp.zeros_like(state)

        for h in range(NH):
            sl = slice(h * HD, (h + 1) * HD)
            st = state[sl, :]
            st_ref[sl, :] = st
            sb = st.astype(bf16)
            vn = u_ref[:, sl] - jnp.dot(w_ref[:, sl].astype(bf16), sb, preferred_element_type=f32)
            vb = vn.astype(bf16)
            o_ref[:, sl] = (jnp.dot(qd_ref[:, sl].astype(bf16), sb, preferred_element_type=f32)
                            + jnp.dot(qk_ref[:, sl].astype(bf16), vb, preferred_element_type=f32))
            state[sl, :] = st * dec_ref[0:1, sl] + lax.dot_general(
                kd_ref[:, sl].astype(bf16), vb, (((0,), (0,)), ((), ())), preferred_element_type=f32)

    row = pl.BlockSpec((CH, WH), lambda n: (n, 0))
    return pl.pallas_call(
        body, name=name, grid=(nc,),
        in_specs=[row] * 5 + [pl.BlockSpec((None, 8, WH), lambda n: (n, 0, 0))],
        out_specs=[row, pl.BlockSpec((None, WH, HD), lambda n: (n, 0, 0))],
        out_shape=[_sds((s, WH), f32), _sds((nc, WH, HD), f32)],
        scratch_shapes=[pltpu.VMEM((WH, HD), f32)],
        compiler_params=_cp(("arbitrary",)),
    )(u, w, qd, kd, qk, dec)


def gdn_b_bwd(u, w, qd, kd, qk, dec, states, do, name):
    s = u.shape[0]
    nc = s // CH

    def body(u_ref, w_ref, qd_ref, kd_ref, qk_ref, dec_ref, st_ref, do_ref,
             du_ref, dw_ref, dqd_ref, dkd_ref, dqk_ref, ddec_ref, dstate):
        n = pl.program_id(0)

        @pl.when(n == 0)
        def _():
            dstate[...] = jnp.zeros_like(dstate)

        for h in range(NH):
            sl = slice(h * HD, (h + 1) * HD)
            st, ds = st_ref[sl, :], dstate[sl, :]
            sb, dsb = st.astype(bf16), ds.astype(bf16)
            wb, qdb, kdb, qkb = (r[:, sl].astype(bf16) for r in (w_ref, qd_ref, kd_ref, qk_ref))
            dob = do_ref[:, sl].astype(bf16)
            vn = u_ref[:, sl] - jnp.dot(wb, sb, preferred_element_type=f32)
            vb = vn.astype(bf16)
            dvn = (lax.dot_general(qkb, dob, (((0,), (0,)), ((), ())), preferred_element_type=f32)
                   + jnp.dot(kdb, dsb, preferred_element_type=f32))
            dvb = dvn.astype(bf16)
            du_ref[:, sl] = dvn
            dw_ref[:, sl] = -_dot_nt(dvb, sb)
            dqd_ref[:, sl] = _dot_nt(dob, sb)
            dkd_ref[:, sl] = _dot_nt(vb, dsb)
            dqk_ref[:, sl] = _dot_nt(dob, vb)
            tot = jnp.sum(jnp.sum(ds * st, axis=1, keepdims=True), axis=0, keepdims=True)
            ddec_ref[:, sl] = jnp.broadcast_to(tot, (8, HD))
            dstate[sl, :] = (ds * dec_ref[0:1, sl]
                             + lax.dot_general(qdb, dob, (((0,), (0,)), ((), ())), preferred_element_type=f32)
                             - lax.dot_general(wb, dvb, (((0,), (0,)), ((), ())), preferred_element_type=f32))

    row = pl.BlockSpec((CH, WH), lambda n: (nc - 1 - n, 0))
    small = pl.BlockSpec((None, 8, WH), lambda n: (nc - 1 - n, 0, 0))
    return pl.pallas_call(
        body, name=name, grid=(nc,),
        in_specs=[row] * 5 + [small, pl.BlockSpec((None, WH, HD), lambda n: (nc - 1 - n, 0, 0)), row],
        out_specs=[row] * 5 + [small],
        out_shape=[_sds((s, WH), f32)] * 5 + [_sds((nc, 8, WH), f32)],
        scratch_shapes=[pltpu.VMEM((WH, HD), f32)],
        compiler_params=_cp(("arbitrary",)),
    )(u, w, qd, kd, qk, dec, states, do)


ANY = pl.BlockSpec(memory_space=pl.ANY)


def _dev_index(p):
    return 4 * p[0] + 2 * p[1] + p[2]


def _shard_of(ref, axis, size, idx):
    return ref.at[pl.ds(idx * size, size), :] if axis == 0 else ref.at[:, pl.ds(idx * size, size)]


def _peer(x, y, c, r):
    return (1 - x if r & 4 else x, 1 - y if r & 2 else y, 1 - c if r & 1 else c)


ALL_PEERS, SIBLING, OTHER_CHIPS = tuple(range(1, NDEV)), (1,), (4, 2, 6)


def _launch(body, ins, out_sds, name, sequencer_id, relations=ALL_PEERS, kinds=7):
    n = len(ins)
    sems = [pltpu.SemaphoreType.DMA((n, kinds)), pltpu.SemaphoreType.DMA((n, kinds)), pltpu.SemaphoreType.DMA((n,))]
    if sequencer_id is None:
        return pl.pallas_call(
            lambda *refs: body(refs[:n], refs[n:2 * n], *refs[2 * n:]), name=name, in_specs=[ANY] * n, out_specs=[ANY] * n,
            out_shape=out_sds, scratch_shapes=sems, compiler_params=pltpu.CompilerParams(has_side_effects=True),
        )(*ins)
    in_refs = [jax.new_ref(a, memory_space=pltpu.MemorySpace.HBM) for a in ins]
    out_refs = [jax.empty_ref(sd, memory_space=pltpu.MemorySpace.HBM) for sd in out_sds]

    @pl.kernel(mesh=plsc.ScalarSubcoreMesh(axis_name="sequencer", num_cores=1), name=name, scratch_types=sems,
               compiler_params=pltpu.CompilerParams(collective_id=sequencer_id))
    def launch(send, recv, loc):
        x, y, c = lax.axis_index("x"), lax.axis_index("y"), lax.axis_index("c")
        barrier = pltpu.get_barrier_semaphore()
        for r in relations:
            pl.semaphore_signal(barrier, inc=1, device_id=_peer(x, y, c, r), device_id_type=MESH)
        pl.semaphore_wait(barrier, len(relations))
        body(in_refs, out_refs, send, recv, loc)

    launch()
    return [r[...] for r in out_refs]


def all_gather(xs, axes, name, sequencer_id=None):
    n = len(xs)
    fulls = [tuple(d * (NDEV if a == ax else 1) for a, d in enumerate(x.shape)) for x, ax in zip(xs, axes)]
    halved = [x.shape[0] % 32 == 0 for x in xs]

    def body(x_refs, o_refs, send, recv, loc):
        x, y, c = lax.axis_index("x"), lax.axis_index("y"), lax.axis_index("c")
        me, sib = (x, y, c), (x, y, 1 - c)
        xn, yn, dg = (1 - x, y), (x, 1 - y), (1 - x, 1 - y)

        def part(t, p, half=None):
            ref = _shard_of(o_refs[t], axes[t], xs[t].shape[axes[t]], _dev_index(p))
            rows = xs[t].shape[0] // 2
            return ref if half is None else ref.at[pl.ds(half * rows, rows), :]

        def copy(t, k, block, to, half=None, src=None):
            return pltpu.make_async_remote_copy(
                src_ref=part(t, block, half) if src is None else src, dst_ref=part(t, block, half),
                send_sem=send.at[t, k], recv_sem=recv.at[t, k], device_id=to, device_id_type=MESH)

        mine = [pltpu.make_async_copy(x_refs[t], part(t, me), loc.at[t]) for t in range(n)]
        for cp in mine:
            cp.start()
        sends = []
        for t in range(n):
            sends += [copy(t, 0, me, sib, src=x_refs[t]), copy(t, 1, me, (*xn, c), src=x_refs[t]),
                      copy(t, 2, me, (*yn, c), src=x_refs[t])]
            if not halved[t]:
                sends.append(copy(t, 3, me, (*dg, c), src=x_refs[t]))
        for cp in sends:
            cp.start()

        def pass_on(cp):
            cp.start()
            sends.append(cp)

        for t in range(n):
            h0, h1 = (0, 1) if halved[t] else (None, None)
            copy(t, 1, (*xn, c), me).wait_recv()
            if halved[t]:
                pass_on(copy(t, 3, (*xn, c), (*yn, c), 0))
            pass_on(copy(t, 5, (*xn, c), sib))
            copy(t, 2, (*yn, c), me).wait_recv()
            if halved[t]:
                pass_on(copy(t, 4, (*yn, c), (*xn, c), 1))
            pass_on(copy(t, 6, (*yn, c), sib))
            copy(t, 3, (*dg, c), me, h0).wait_recv()
            pass_on(copy(t, 7, (*dg, c), sib, h0))
            if halved[t]:
                copy(t, 4, (*dg, c), me, h1).wait_recv()
                pass_on(copy(t, 8, (*dg, c), sib, h1))
        for t in range(n):
            h0, h1 = (0, 1) if halved[t] else (None, None)
            copy(t, 0, sib, me).wait_recv()
            copy(t, 5, (*xn, 1 - c), me).wait_recv()
            copy(t, 6, (*yn, 1 - c), me).wait_recv()
            copy(t, 7, (*dg, 1 - c), me, h0).wait_recv()
            if halved[t]:
                copy(t, 8, (*dg, 1 - c), me, h1).wait_recv()
        for cp in sends:
            cp.wait_send()
        for cp in mine:
            cp.wait()

    return _launch(body, xs, [_sds(f, x.dtype) for f, x in zip(fulls, xs)], name, sequencer_id, kinds=9)


def pair_swap(gs, axes, name, sequencer_id):
    n = len(gs)
    shards = [tuple(d // (NDEV if a == ax else 1) for a, d in enumerate(g.shape)) for g, ax in zip(gs, axes)]

    def body(g_refs, p_refs, send, recv, loc):
        x, y, c = lax.axis_index("x"), lax.axis_index("y"), lax.axis_index("c")

        def copy(t, j):
            return pltpu.make_async_remote_copy(
                src_ref=_shard_of(g_refs[t], axes[t], shards[t][axes[t]], 2 * j + (1 - c)), dst_ref=p_refs[t].at[j],
                send_sem=send.at[t, j], recv_sem=recv.at[t, j], device_id=(x, y, 1 - c), device_id_type=MESH)

        copies = [copy(t, j) for t in range(n) for j in range(4)]
        for cp in copies:
            cp.start()
        for cp in copies:
            cp.wait_recv()
        for cp in copies:
            cp.wait_send()

    return _launch(body, gs, [_sds((4,) + sh, g.dtype) for sh, g in zip(shards, gs)], name, sequencer_id, SIBLING)


def pair_add(g, p, axis, name, after=()):
    _, rows, cols = p.shape
    tr = _row_tile(rows, cols, 1 << 20)
    nb = rows // tr
    if axis == 0:
        g_spec = pl.BlockSpec((tr, cols), lambda j, i, c: ((2 * j + c[0]) * nb + i, 0))
    else:
        g_spec = pl.BlockSpec((tr, cols), lambda j, i, c: (i, 2 * j + c[0]))
    blk = pl.BlockSpec((None, tr, cols), lambda j, i, c: (j, i, 0))

    na = len(after)

    def body(c_ref, *refs):
        g_ref, p_ref, q_ref = refs[na:]
        q_ref[...] = (g_ref[...].astype(f32) + p_ref[...].astype(f32)).astype(bf16)

    return pl.pallas_call(
        body, name=name, out_shape=_sds(p.shape, bf16),
        grid_spec=pltpu.PrefetchScalarGridSpec(num_scalar_prefetch=1, grid=(4, nb), in_specs=[ANY] * na + [g_spec, blk],
                                               out_specs=blk),
        compiler_params=_cp(("parallel", "parallel")),
    )(lax.axis_index("c").astype(jnp.int32).reshape(1), *after, g, p)


def chip_exchange(qs, name, sequencer_id):
    n = len(qs)

    def body(q_refs, r_refs, send, recv, loc):
        x, y, c = lax.axis_index("x"), lax.axis_index("y"), lax.axis_index("c")
        my_chip = 2 * x + y
        peers = [_peer(x, y, c, r) for r in OTHER_CHIPS]
        mine = [pltpu.make_async_copy(q_refs[t].at[my_chip], r_refs[t].at[my_chip], loc.at[t]) for t in range(n)]
        for cp in mine:
            cp.start()

        def copy(t, k, slot):
            p = peers[k]
            return pltpu.make_async_remote_copy(
                src_ref=q_refs[t].at[2 * p[0] + p[1]], dst_ref=r_refs[t].at[slot], send_sem=send.at[t, k], recv_sem=recv.at[t, k],
                device_id=p, device_id_type=MESH)

        sends = [copy(t, k, my_chip) for t in range(n) for k in range(3)]
        for cp in sends:
            cp.start()
        for t in range(n):
            for k in range(3):
                copy(t, k, 2 * peers[k][0] + peers[k][1]).wait_recv()
        for cp in sends:
            cp.wait_send()
        for cp in mine:
            cp.wait()

    return _launch(body, qs, [_sds(q.shape, q.dtype) for q in qs], name, sequencer_id, OTHER_CHIPS)


def _adamw(w, g, m, v):
    m = ADAM_B1 * m + (1.0 - ADAM_B1) * g
    v = ADAM_B2 * v + (1.0 - ADAM_B2) * jnp.square(g)
    m_hat = m / (1.0 - ADAM_B1 ** ADAM_STEP)
    v_hat = v / (1.0 - ADAM_B2 ** ADAM_STEP)
    return -ADAM_LR * (m_hat / (jnp.sqrt(v_hat) + ADAM_EPS) + ADAM_WD * w), m, v


def _sum8(r_ref):
    g = r_ref[0].astype(f32)
    for j in range(1, r_ref.shape[0]):
        g = g + r_ref[j].astype(f32)
    return g


def _row_tile(rows, cols, elems=1 << 18):
    tr = min(rows, max(8, 1 << int(math.log2(elems / cols))))
    assert rows % tr == 0, (rows, cols)
    return tr


def sum_partials(r, name, after=()):
    _, rows, cols = r.shape
    tr = _row_tile(rows, cols)
    na = len(after)

    def body(*refs):
        refs[na + 1][...] = _sum8(refs[na])

    return pl.pallas_call(
        body, name=name, grid=(rows // tr,),
        in_specs=[ANY] * na + [pl.BlockSpec((r.shape[0], tr, cols), lambda i: (0, i, 0))],
        out_specs=pl.BlockSpec((tr, cols), lambda i: (i, 0)), out_shape=_sds((rows, cols), f32),
        compiler_params=_cp(("parallel",)),
    )(*after, r)


def adamw_t(w, m, v, grads, name):
    rows, nl, cols = w.shape
    tr = min(rows, (1 << 16) // cols)
    blk = pl.BlockSpec((tr, nl, cols), lambda i: (i, 0, 0))
    flat = pl.BlockSpec((tr, cols), lambda i: (i, 0))

    def body(w_ref, m_ref, v_ref, *rest):
        g_refs, (g_ref, d_ref, nm_ref, nv_ref) = rest[:nl], rest[nl:]
        for l in range(nl):
            grad = g_refs[l][...]
            g_ref[:, l, :] = grad
            d_ref[:, l, :], nm_ref[:, l, :], nv_ref[:, l, :] = _adamw(w_ref[:, l, :], grad, m_ref[:, l, :], v_ref[:, l, :])

    return tuple(pl.pallas_call(
        body, name=name, grid=(pl.cdiv(rows, tr),), in_specs=[blk] * 3 + [flat] * nl, out_specs=[blk] * 4,
        out_shape=[_sds(w.shape, f32)] * 4, compiler_params=_cp(("parallel",)),
    )(w, m, v, *grads))


def adamw(w, m, v, layer, name, r=None, g=None, prev=None):
    _, rows, cols = w.shape
    tr = _row_tile(rows, cols)
    blk = pl.BlockSpec((None, tr, cols), lambda i: (layer, i, 0))
    nprev = 0 if prev is None else 4

    def body(w_ref, m_ref, v_ref, src_ref, *rest):
        g_ref, d_ref, nm_ref, nv_ref, token_ref = rest[nprev:]
        grad = _sum8(src_ref) if g is None else src_ref[...]
        g_ref[...] = grad
        d_ref[...], nm_ref[...], nv_ref[...] = _adamw(w_ref[...], grad, m_ref[...], v_ref[...])
        token_ref[...] = jnp.zeros_like(token_ref)

    src, src_spec = (r, pl.BlockSpec((r.shape[0], tr, cols), lambda i: (0, i, 0))) if g is None else (g, pl.BlockSpec((tr, cols), lambda i: (i, 0)))
    *outs, token = pl.pallas_call(
        body, name=name, grid=(rows // tr,), in_specs=[blk] * 3 + [src_spec] + [ANY] * nprev,
        out_specs=[blk] * 4 + [pl.BlockSpec((8, HD), lambda i: (0, 0))],
        out_shape=[_sds(w.shape, f32)] * 4 + [_sds((8, HD), f32)], input_output_aliases={4 + k: k for k in range(nprev)},
        compiler_params=_cp(("arbitrary",)),
    )(w, m, v, src, *(prev or ()))
    return tuple(outs), token


def small_adamw(parts, w, m, v, name):
    def body(p_ref, w_ref, m_ref, v_ref, g_ref, d_ref, nm_ref, nv_ref):
        g = _sum8(p_ref)
        g_ref[...] = g
        d_ref[...], nm_ref[...], nv_ref[...] = _adamw(w_ref[...], g, m_ref[...], v_ref[...])

    return pl.pallas_call(body, name=name, out_shape=[_sds(w.shape, f32)] * 4, compiler_params=_cp())(parts, w, m, v)


def _pack_rows(wt):
    tail = jnp.pad(wt[5888:N_IN], ((OFF_G - 12 - (OFF_P + 512), 0), (0, 0)))
    return jnp.concatenate([wt[512:2816], wt[2816:5120], wt[5120:5888], wt[0:512], tail], axis=0)


def _unpack_rows(g):
    return jnp.concatenate([g[OFF_P:OFF_P + 512], g[OFF_SB:OFF_SB + 2304], g[OFF_GQ:OFF_GQ + 2304], g[OFF_Z:OFF_Z + 768],
                            g[OFF_G - 12:NP]], axis=0)


def _lanes(v):
    flat = v.reshape(-1)
    n = -(-flat.shape[0] // HD) * HD
    return jnp.pad(flat, (0, n - flat.shape[0])).reshape(n // HD, HD)


def _lanes8(v):
    rows = _lanes(v)
    return jnp.pad(rows, ((0, -rows.shape[0] % 8), (0, 0)))


def _layer_fwd(x, p, l):
    nm = lambda s: f"{s}_l{l}"
    u = rms_fwd(x, p["attn_norm"], nm("rms1"))
    proj = matmul(u, p["w_in"], name=nm("inproj"), tb=True)
    y_pool = pool_fwd(proj, p["pool_w"], p["pool_scale"], nm("pool"))
    y_sb = sb_fwd(proj, nm("sb"))
    c = conv_fwd(proj, p["conv"], nm("conv"))
    ga = gdn_a_fwd(c, proj, p["alog"], p["dtb"], nm("gdna"))
    o, states = gdn_b_fwd(*ga, nm("gdnb"))
    y_gdn = gdn_out_fwd(o, proj, p["gdn_norm"], nm("gdno"))
    ups = [matmul(y, p[k], name=nm(k)) for y, k in ((y_pool, "w_pool_up"), (y_sb, "w_sb_up"), (y_gdn, "w_gdn_up"))]
    merged = merge_fwd(proj, ups, nm("merge"))
    x1 = matmul(merged, p["w_out"], name=nm("outproj"), epilogue=lambda acc, r: acc + r, extras=(x,))
    u2 = rms_fwd(x1, p["mlp_norm"], nm("rms2"))
    h2 = matmul(u2, p["w_ff1"], name=nm("ff1"), out_dtype=bf16, epilogue=lambda acc: jnp.square(jnp.maximum(acc, 0.0)))
    x2 = matmul(h2, p["w_ff2"], name=nm("ff2"), epilogue=lambda acc, r: acc + r, extras=(x1,))
    saved = dict(x=x, u=u, proj=proj, y_pool=y_pool, y_sb=y_sb, c=c, ga=ga, o=o, states=states, y_gdn=y_gdn, ups=ups,
                 merged=merged, x1=x1, u2=u2, h2=h2)
    return x2, saved


def _layer_bwd(dx2, dx2b, sv, p, l, swap, finish):
    nm = lambda s: f"{s}_l{l}"
    s = dx2.shape[0]
    dh = matmul(dx2b, p["w_ff2"], name=nm("d_ff2_x"), tb=True, out_dtype=bf16,
                epilogue=lambda acc, h2: acc * (2.0 * jnp.sqrt(h2.astype(f32))), extras=(sv["h2"],))
    g_ff2 = matmul(sv["h2"], dx2b, name=nm("d_ff2_w"), ta=True, out_dtype=bf16)
    du2 = matmul(dh, p["w_ff1"], name=nm("d_ff1_x"), tb=True)
    g_ff1 = matmul(sv["u2"], dh, name=nm("d_ff1_w"), ta=True, out_dtype=bf16)
    dx1, dx1b, d_mlp_norm = rms_bwd(sv["x1"], du2, dx2, p["mlp_norm"], nm("d_rms2"))
    dmerged = matmul(dx1b, p["w_out"], name=nm("d_out_x"), tb=True, out_dtype=bf16)
    g_out = matmul(sv["merged"], dx1b, name=nm("d_out_w"), ta=True, out_dtype=bf16)
    dgates, *dups = merge_bwd(sv["proj"], sv["ups"], dmerged, nm("d_merge"))
    dys, g_ups = [], []
    for dup, y, k in zip(dups, (sv["y_pool"], sv["y_sb"], sv["y_gdn"]), ("w_pool_up", "w_sb_up", "w_gdn_up")):
        dys.append(matmul(dup, p[k], name=nm("d_" + k + "_x"), tb=True, out_dtype=bf16))
        g_ups.append(matmul(y, dup, name=nm("d_" + k + "_w"), ta=True, out_dtype=bf16))
    early = g_ups + [g_out, g_ff1, g_ff2]
    swapped = swap(early, BIG_AXES[1:], "a")
    do, dz, d_gdn_norm = gdn_out_bwd(sv["o"], sv["proj"], p["gdn_norm"], dys[2], nm("d_gdno"))
    cots = gdn_b_bwd(*sv["ga"], sv["states"], do, nm("d_gdnb"))
    dc, dab, d_alog, d_dtb = gdn_a_bwd(sv["c"], sv["proj"], p["alog"], p["dtb"], cots[:5], cots[5], nm("d_gdna"))
    r_early, sent_early = finish(early, swapped, BIG_AXES[1:], BIG_NAMES[1:], "a", (dab,))
    dgq, d_conv = conv_bwd(sv["proj"], p["conv"], dc, nm("d_conv"), sent_early)
    dq, dk, dv = sb_bwd(sv["proj"], dys[1], nm("d_sb"))
    dp, d_pool_w, d_pool_scale = pool_bwd(sv["proj"], p["pool_w"], p["pool_scale"], dys[0], nm("d_pool"))
    dproj = jnp.concatenate([dq, dk, dv, dgq, dz, dp, jnp.zeros((s, OFF_AB - OFF_P - W_POOL), bf16), dab, dgates], axis=1)
    g_in = matmul(dproj, sv["u"], name=nm("d_in_w"), ta=True, out_dtype=bf16)
    r_in, sent_in = finish([g_in], swap([g_in], BIG_AXES[:1], "b"), BIG_AXES[:1], BIG_NAMES[:1], "b", ())
    du = matmul(dproj, p["w_in"], name=nm("d_in_x"), after=sent_in)
    dx, dxb, d_attn_norm = rms_bwd(sv["x"], du, dx1, p["attn_norm"], nm("d_rms1"))
    recv = r_in + r_early
    small = [d_attn_norm, d_pool_w, d_pool_scale, d_conv, d_alog, d_dtb, d_gdn_norm, d_mlp_norm]
    return dx, dxb, recv, small


BIG_AXES = (1, 1, 1, 1, 0, 1, 0)
GATHER_ID, EXCHANGE_ID = 1, 2
BIG_NAMES = ("w_in", "w_pool_up", "w_sb_up", "w_gdn_up", "w_out", "w_ff1", "w_ff2")


def kernel(x, attn_norm, w_in, pool_w, pool_scale, gdn_conv, gdn_a_log, gdn_dt_bias, gdn_norm, w_pool_up, w_sb_up, w_gdn_up, w_out, mlp_norm, w_ff1, w_ff2, final_norm, loss_target, m_attn_norm, m_w_in, m_pool_w, m_pool_scale, m_gdn_conv, m_gdn_a_log, m_gdn_dt_bias, m_gdn_norm, m_w_pool_up, m_w_sb_up, m_w_gdn_up, m_w_out, m_mlp_norm, m_w_ff1, m_w_ff2, m_final_norm, v_attn_norm, v_w_in, v_pool_w, v_pool_scale, v_gdn_conv, v_gdn_a_log, v_gdn_dt_bias, v_gdn_norm, v_w_pool_up, v_w_sb_up, v_w_gdn_up, v_w_out, v_mlp_norm, v_w_ff1, v_w_ff2, v_final_norm):
    s = x.shape[1]
    me = _dev_index((lax.axis_index("x"), lax.axis_index("y"), lax.axis_index("c")))
    ncv = gdn_conv.shape[2]

    w_in_t, m_w_in_t, v_w_in_t = (jnp.transpose(a, (2, 0, 1)) for a in (w_in, m_w_in, v_w_in))
    full = []
    for l in range(NL):
        shards = [_pack_rows(w_in_t[:, l]).astype(bf16), w_pool_up[l].astype(bf16), w_sb_up[l].astype(bf16),
                  w_gdn_up[l].astype(bf16), w_out[l].astype(bf16), w_ff1[l].astype(bf16), w_ff2[l].astype(bf16)]
        if l == 0:
            first = all_gather([shards[0], gdn_conv.reshape(NL * 4, ncv)], (1, 0), "gather_first", sequencer_id=GATHER_ID)
            rest = all_gather(shards[1:], BIG_AXES[1:], "gather_rest_l0", sequencer_id=GATHER_ID)
            full.append([first[0]] + rest)
            conv_full = first[1].reshape(NDEV, NL, 4, ncv).transpose(1, 2, 0, 3).reshape(NL, 4, NDEV * ncv)
        else:
            full.append(all_gather(shards, BIG_AXES, f"gather_weights_l{l}", sequencer_id=GATHER_ID))
    params = []
    for l in range(NL):
        p = dict(zip(("w_in", "w_pool_up", "w_sb_up", "w_gdn_up", "w_out", "w_ff1", "w_ff2"), full[l][:7]))
        p.update(attn_norm=attn_norm[l][None], mlp_norm=mlp_norm[l][None], pool_w=pool_w[l], pool_scale=pool_scale[l][None],
                 conv=conv_full[l], alog=_lanes(gdn_a_log[l]), dtb=_lanes(gdn_dt_bias[l]), gdn_norm=gdn_norm[l][None])
        params.append(p)

    h = x[0]
    saved = []
    for l in range(NL):
        h, sv = _layer_fwd(h, params[l], l)
        saved.append(sv)
    dh, dhb, loss_row, d_final = loss_head(h, loss_target[0], final_norm[None], "loss_head")
    recv, smalls = [None] * NL, [None] * NL
    for l in reversed(range(NL)):
        def swap(gs, axes, tag, l=l):
            return pair_swap(gs, axes, f"swap_{tag}_l{l}", None)

        def finish(gs, ps, axes, names, tag, after, l=l):
            qs = [pair_add(g, p, ax, f"pair_add_{k}_l{l}", after) for g, p, ax, k in zip(gs, ps, axes, names)]
            return chip_exchange(qs, f"exchange_{tag}_l{l}", EXCHANGE_ID), tuple(qs)

        dh, dhb, recv[l], smalls[l] = _layer_bwd(dh, dhb, saved[l], params[l], l, swap, finish)

    small_rows = [_lanes8(t) for l in range(NL) for t in smalls[l]] + [_lanes8(d_final), _lanes8(loss_row)]
    packed = jnp.concatenate(small_rows, axis=0)
    parts = all_gather([packed], (0,), "gather_small")[0].reshape(NDEV, packed.shape[0], HD)

    def pack_small(tree):
        rows = []
        for l in range(NL):
            rows += [_lanes8(tree["attn_norm"][l]), _lanes8(tree["pool_w"][l]), _lanes8(tree["pool_scale"][l]),
                     jnp.zeros((4 * NDEV * ncv // HD, HD), f32), _lanes8(tree["gdn_a_log"][l]), _lanes8(tree["gdn_dt_bias"][l]),
                     _lanes8(tree["gdn_norm"][l]), _lanes8(tree["mlp_norm"][l])]
        rows += [_lanes8(tree["final_norm"]), jnp.zeros((8, HD), f32)]
        return jnp.concatenate(rows, axis=0)

    names = ("attn_norm", "pool_w", "pool_scale", "gdn_a_log", "gdn_dt_bias", "gdn_norm", "mlp_norm", "final_norm")
    w_small = pack_small(dict(zip(names, (attn_norm, pool_w, pool_scale, gdn_a_log, gdn_dt_bias, gdn_norm, mlp_norm, final_norm))))
    m_small = pack_small(dict(zip(names, (m_attn_norm, m_pool_w, m_pool_scale, m_gdn_a_log, m_gdn_dt_bias, m_gdn_norm, m_mlp_norm, m_final_norm))))
    v_small = pack_small(dict(zip(names, (v_attn_norm, v_pool_w, v_pool_scale, v_gdn_a_log, v_gdn_dt_bias, v_gdn_norm, v_mlp_norm, v_final_norm))))
    small_out = small_adamw(parts, w_small, m_small, v_small, "adamw_small")

    def unpack_small(buf):
        out, conv_g, r = {}, [], 0
        layer_items = (("attn_norm", (D,)), ("pool_w", (4, HD, HD)), ("pool_scale", (W_POOL,)), ("conv", (4, NDEV * ncv)),
                       ("gdn_a_log", (NH,)), ("gdn_dt_bias", (NH,)), ("gdn_norm", (HD,)), ("mlp_norm", (D,)))
        per_layer = {k: [] for k, _ in layer_items}
        for l in range(NL):
            for k, shape in layer_items:
                size = math.prod(shape)
                nrow = -(-size // (8 * HD)) * 8
                per_layer[k].append(buf[r:r + nrow].reshape(-1)[:size].reshape(shape))
                r += nrow
        for k, _ in layer_items:
            out[k] = jnp.stack(per_layer[k])
        out["final_norm"] = buf[r:r + D // HD].reshape(D)
        out["loss"] = buf[r + D // HD, 0]
        return out

    sm = [unpack_small(b) for b in small_out]
    loss = sm[0]["loss"]
    g_conv = lax.dynamic_slice_in_dim(sm[0]["conv"], me * ncv, ncv, axis=2)
    conv_out = None
    for l in reversed(range(NL)):
        conv_out, _ = adamw(gdn_conv, m_gdn_conv, v_gdn_conv, l, f"adamw_conv_l{l}", g=g_conv[l], prev=conv_out)

    big_out = {}
    big_names = BIG_NAMES
    big_w = dict(zip(big_names, ((w_in_t, m_w_in_t, v_w_in_t), (w_pool_up, m_w_pool_up, v_w_pool_up), (w_sb_up, m_w_sb_up, v_w_sb_up),
                                 (w_gdn_up, m_w_gdn_up, v_w_gdn_up), (w_out, m_w_out, v_w_out), (w_ff1, m_w_ff1, v_w_ff1),
                                 (w_ff2, m_w_ff2, v_w_ff2))))
    tokens = []
    for l, k in [(l, k) for l in reversed(range(NL)) for k in big_names[1:]]:
        w, m, v = big_w[k]
        big_out[k], token = adamw(w, m, v, l, f"adamw_{k}_l{l}", r=recv[l][big_names.index(k)], prev=big_out.get(k))
        tokens.append(token)
    g_in = [_unpack_rows(sum_partials(recv[l][0], f"sum_w_in_l{l}", tuple(tokens) if l == 0 else ())) for l in range(NL)]
    big_out["w_in"] = adamw_t(*big_w["w_in"], g_in, "adamw_w_in")

    def leaf(i, k):
        if k == "w_in":
            return jnp.transpose(big_out[k][i], (1, 2, 0))
        if k in big_out:
            return big_out[k][i]
        if k == "gdn_conv":
            return conv_out[i]
        return sm[i][k]

    order = ("attn_norm", "w_in", "pool_w", "pool_scale", "gdn_conv", "gdn_a_log", "gdn_dt_bias", "gdn_norm", "w_pool_up",
             "w_sb_up", "w_gdn_up", "w_out", "mlp_norm", "w_ff1", "w_ff2", "final_norm")
    return (loss, dh[None]) + tuple(leaf(i, k) for i in range(4) for k in order)
```

```python
import functools
import math

import jax
import jax.numpy as jnp
from jax import lax
from jax.experimental import pallas as pl
from jax.experimental.pallas import tpu as pltpu
from jax.experimental.pallas import tpu_sc as plsc

f32, bf16 = jnp.float32, jnp.bfloat16

D = 2048
NDEV = 8
NL = 2
HD = 128
NH = 6
WH = NH * HD
W_POOL = 512
EPS = 1e-6
N_IN = 12044
NP = 12288
OFF_SB, OFF_GQ, OFF_Z, OFF_P, OFF_AB, OFF_G = 0, 2304, 4608, 5376, 6016, 6144
AB_LANE = HD - 2 * NH
POOL_WINDOWS = (2, 4, 8, 16)
CH = 128
TQ = 256
VMEM_LIMIT = 56 * 1024 * 1024
ADAM_LR, ADAM_B1, ADAM_B2, ADAM_EPS, ADAM_WD, ADAM_STEP = 0.001, 0.9, 0.999, 1e-08, 0.01, 10
MESH = pl.DeviceIdType.MESH


def _cp(sem=None):
    return pltpu.CompilerParams(dimension_semantics=sem, vmem_limit_bytes=VMEM_LIMIT)


def _sds(shape, dtype):
    return jax.ShapeDtypeStruct(tuple(shape), dtype)


def matmul(a, b, *, name, ta=False, tb=False, out_dtype=f32, tm=1024, tn=1024, tk=2048, epilogue=None, extras=(), after=()):
    m, k = (a.shape[1], a.shape[0]) if ta else a.shape
    n = b.shape[0] if tb else b.shape[1]
    assert k == (b.shape[1] if tb else b.shape[0]) and a.dtype == bf16 and b.dtype == bf16
    tm, tn, tk = min(tm, m), min(tn, n), min(tk, k)
    assert m % tm == 0 and n % tn == 0 and k % tk == 0, (m, n, k, tm, tn, tk)
    nk = k // tk
    a_spec = pl.BlockSpec((tk, tm), lambda i, j, q: (q, i)) if ta else pl.BlockSpec((tm, tk), lambda i, j, q: (i, q))
    b_spec = pl.BlockSpec((tn, tk), lambda i, j, q: (j, q)) if tb else pl.BlockSpec((tk, tn), lambda i, j, q: (q, j))
    e_specs = [pl.BlockSpec((tm, tn), lambda i, j, q: (i, j)) for _ in extras]
    dn = (((0 if ta else 1,), (1 if tb else 0,)), ((), ()))
    ne = len(extras)

    def body(*refs):
        a_ref, b_ref, *rest = refs[len(after):]
        e_refs, o_ref = rest[:ne], rest[ne]
        part = lax.dot_general(a_ref[...], b_ref[...], dn, preferred_element_type=f32)

        def finish(acc):
            if epilogue is not None:
                acc = epilogue(acc, *[e[...] for e in e_refs])
            o_ref[...] = acc.astype(out_dtype)

        if nk == 1:
            finish(part)
        else:
            acc_ref = rest[ne + 1]
            q = pl.program_id(2)

            @pl.when(q == 0)
            def _():
                acc_ref[...] = part

            @pl.when(jnp.logical_and(q > 0, q < nk - 1))
            def _():
                acc_ref[...] += part

            @pl.when(q == nk - 1)
            def _():
                finish(acc_ref[...] + part)

    return pl.pallas_call(
        body, name=name, grid=(m // tm, n // tn, nk),
        in_specs=[pl.BlockSpec(memory_space=pl.ANY)] * len(after) + [a_spec, b_spec] + e_specs,
        out_specs=pl.BlockSpec((tm, tn), lambda i, j, q: (i, j)), out_shape=_sds((m, n), out_dtype),
        scratch_shapes=[pltpu.VMEM((tm, tn), f32)] if nk > 1 else [],
        compiler_params=_cp(("parallel", "parallel", "arbitrary")),
    )(*after, a, b, *extras)


def rowwise(name, fn, rows, params, outs, sums=(), tr=256, after=()):
    s = rows[0][0].shape[0]
    tr = min(tr, s)
    nin, nout = len(rows) + len(params), len(outs)
    in_specs = [pl.BlockSpec((tr, w), functools.partial(lambda i, c: (i, c), c=c)) for (_, w, c) in rows]
    in_specs += [pl.BlockSpec(p.shape, lambda i: (0, 0)) for p in params]
    out_specs = [pl.BlockSpec((tr, w), lambda i: (i, 0)) for (w, _) in outs]
    out_specs += [pl.BlockSpec(sh, lambda i: (0, 0)) for sh in sums]
    out_shape = [_sds((s, w), dt) for (w, dt) in outs] + [_sds(sh, f32) for sh in sums]

    def body(*refs):
        refs = refs[len(after):]
        res = fn(*[r[...] for r in refs[:nin]])
        for r, v in zip(refs[nin:nin + nout], res[:nout]):
            r[...] = v.astype(r.dtype)
        i = pl.program_id(0)
        for r, v in zip(refs[nin + nout:], res[nout:]):
            @pl.when(i == 0)
            def _(r=r, v=v):
                r[...] = v

            @pl.when(i > 0)
            def _(r=r, v=v):
                r[...] += v

    res = pl.pallas_call(
        body, name=name, grid=(s // tr,), in_specs=[pl.BlockSpec(memory_space=pl.ANY)] * len(after) + in_specs,
        out_specs=out_specs, out_shape=out_shape, compiler_params=_cp(("arbitrary",)),
    )(*after, *[r[0] for r in rows], *params)
    return res


def _rms(x, g):
    return x * lax.rsqrt(jnp.mean(x * x, axis=-1, keepdims=True) + EPS) * g


def rms_fwd(x, g, name):
    return rowwise(name, lambda xb, gb: (_rms(xb, gb),), [(x, D, 0)], [g], [(D, bf16)])[0]


def rms_bwd(x, du, dres, g, name, after=()):
    def fn(xb, dub, drb, gb):
        _, vjp = jax.vjp(_rms, xb, gb)
        dx, dg = vjp(dub.astype(f32))
        return drb + dx, drb + dx, dg

    return rowwise(name, fn, [(x, D, 0), (du, D, 0), (dres, D, 0)], [g], [(D, f32), (D, bf16)], [(1, D)], after=after)


def _merge(gates, up_p, up_s, up_g):
    sg = jax.nn.sigmoid(gates)
    return sg[:, :D] * up_p + sg[:, D:2 * D] * up_s + sg[:, 2 * D:] * up_g


def merge_fwd(proj, ups, name):
    return rowwise(name, lambda g, a, b, c: (_merge(g, a, b, c),),
                   [(proj, 3 * D, OFF_G // (3 * D))] + [(u, D, 0) for u in ups], [], [(D, bf16)], tr=128)[0]


def merge_bwd(proj, ups, dmerged, name):
    def fn(g, a, b, c, dm):
        _, vjp = jax.vjp(_merge, g, a, b, c)
        return vjp(dm.astype(f32))

    return rowwise(name, fn, [(proj, 3 * D, OFF_G // (3 * D))] + [(u, D, 0) for u in ups] + [(dmerged, D, 0)], [],
                   [(3 * D, bf16), (D, bf16), (D, bf16), (D, bf16)], tr=128)


def _gdn_out(o, z, g):
    ys = []
    for h in range(NH):
        sl = slice(h * HD, (h + 1) * HD)
        ys.append(_rms(o[:, sl], g) * jax.nn.silu(z[:, sl]))
    return jnp.concatenate(ys, axis=1)


def gdn_out_fwd(o, proj, g, name):
    return rowwise(name, lambda ob, zb, gb: (_gdn_out(ob, zb, gb),), [(o, WH, 0), (proj, WH, OFF_Z // WH)], [g],
                   [(WH, bf16)])[0]


def gdn_out_bwd(o, proj, g, dy, name):
    def fn(ob, zb, dyb, gb):
        _, vjp = jax.vjp(_gdn_out, ob, zb, gb)
        return vjp(dyb.astype(f32))

    return rowwise(name, fn, [(o, WH, 0), (proj, WH, OFF_Z // WH), (dy, WH, 0)], [g], [(WH, f32), (WH, bf16)],
                   [(1, HD)])


def loss_head(x, target, g, name):
    def loss_fn(xb, gb, tb):
        err = _rms(xb, gb) - tb
        return (0.5 / D) * jnp.sum(jnp.sum(err * err, axis=1, keepdims=True), axis=0, keepdims=True)

    def fn(xb, tb, gb):
        val, vjp = jax.vjp(functools.partial(loss_fn, tb=tb), xb, gb)
        dx, dg = vjp(jnp.ones((1, 1), f32))
        return dx, dx, jnp.broadcast_to(val, (1, HD)), dg

    return rowwise(name, fn, [(x, D, 0), (target, D, 0)], [g], [(D, f32), (D, bf16)], [(1, HD), (1, D)])


PB = 256


def _split(v):
    hi = v.astype(bf16)
    return hi, (v - hi.astype(f32)).astype(bf16)


def _band_dot(make_band, v, s, forward):
    hi, lo = _split(v)
    nb = s // PB
    outs = []
    for r in range(nb):
        lo_r = max(r - 1, 0) if forward else r
        hi_r = r + 1 if forward else min(r + 2, nb)
        band = make_band(r * PB, lo_r * PB, (hi_r - lo_r) * PB)
        sl = slice(lo_r * PB, hi_r * PB)
        outs.append(jnp.dot(band, hi[sl], preferred_element_type=f32) + jnp.dot(band, lo[sl], preferred_element_type=f32))
    return jnp.concatenate(outs, axis=0)


def _pool_common(p, win, s):
    def band(row0, col0, ncol):
        t = row0 + lax.broadcasted_iota(jnp.int32, (PB, ncol), 0)
        u = col0 + lax.broadcasted_iota(jnp.int32, (PB, ncol), 1)
        return jnp.logical_and(u <= t, t < u + win).astype(bf16)

    def band_t(row0, col0, ncol):
        u = row0 + lax.broadcasted_iota(jnp.int32, (PB, ncol), 0)
        t = col0 + lax.broadcasted_iota(jnp.int32, (PB, ncol), 1)
        return jnp.logical_and(u <= t, t < u + win).astype(bf16)

    t = lax.broadcasted_iota(jnp.int32, (s, 1), 0)
    inv_n = 1.0 / jnp.minimum(t + 1, win).astype(f32)
    d = _band_dot(band, p, s, True) * inv_n - p
    return d, inv_n, band_t


def pool_fwd(proj, pool_w, pool_scale, name):
    s = proj.shape[0]

    def body(p_ref, w_ref, sc_ref, y_ref):
        win = jnp.left_shift(2, pl.program_id(0))
        d, _, _ = _pool_common(p_ref[...], win, s)
        y = jnp.dot(d.astype(bf16), w_ref[...].astype(bf16), preferred_element_type=f32) * sc_ref[...]
        y_ref[...] = y.astype(bf16)

    return pl.pallas_call(
        body, name=name, grid=(4,),
        in_specs=[pl.BlockSpec((s, HD), lambda g: (0, OFF_P // HD + g)), pl.BlockSpec((None, HD, HD), lambda g: (g, 0, 0)),
                  pl.BlockSpec((1, HD), lambda g: (0, g))],
        out_specs=pl.BlockSpec((s, HD), lambda g: (0, g)), out_shape=_sds((s, W_POOL), bf16),
        compiler_params=_cp(("arbitrary",)),
    )(proj, pool_w, pool_scale)


def pool_bwd(proj, pool_w, pool_scale, dy, name):
    s = proj.shape[0]

    def body(p_ref, w_ref, sc_ref, dy_ref, dp_ref, dw_ref, dsc_ref):
        win = jnp.left_shift(2, pl.program_id(0))
        d, inv_n, band_t = _pool_common(p_ref[...], win, s)
        w = w_ref[...].astype(bf16)
        dyf = dy_ref[...].astype(f32)
        dsc_ref[...] = jnp.sum(dyf * jnp.dot(d.astype(bf16), w, preferred_element_type=f32), axis=0, keepdims=True)
        dys = (dyf * sc_ref[...]).astype(bf16)
        dd = lax.dot_general(dys, w, (((1,), (1,)), ((), ())), preferred_element_type=f32)
        dw_ref[...] = lax.dot_general(d.astype(bf16), dys, (((0,), (0,)), ((), ())), preferred_element_type=f32)
        dp_ref[...] = (_band_dot(band_t, dd * inv_n, s, False) - dd).astype(bf16)

    return pl.pallas_call(
        body, name=name, grid=(4,),
        in_specs=[pl.BlockSpec((s, HD), lambda g: (0, OFF_P // HD + g)), pl.BlockSpec((None, HD, HD), lambda g: (g, 0, 0)),
                  pl.BlockSpec((1, HD), lambda g: (0, g)), pl.BlockSpec((s, HD), lambda g: (0, g))],
        out_specs=[pl.BlockSpec((s, HD), lambda g: (0, g)), pl.BlockSpec((None, HD, HD), lambda g: (g, 0, 0)),
                   pl.BlockSpec((1, HD), lambda g: (0, g))],
        out_shape=[_sds((s, W_POOL), bf16), _sds((4, HD, HD), f32), _sds((1, W_POOL), f32)],
        compiler_params=_cp(("arbitrary",)),
    )(proj, pool_w, pool_scale, dy)


HB = 3
WB = HB * HD


def _heads_of(v):
    return _stack([v[:, h * HD:(h + 1) * HD] for h in range(HB)])


def _bd(a, b, dn):
    return lax.dot_general(a, b, dn, preferred_element_type=f32)


def _run_sum(v, tri):
    hi, lo = _split(v.reshape(HB * TQ, TQ))
    return (jnp.dot(hi, tri, preferred_element_type=f32) + jnp.dot(lo, tri, preferred_element_type=f32)).reshape(HB, TQ, TQ)


def _sb_tile(q, k_ref, kb, qi, carry):
    k = _heads_of(k_ref[pl.ds(pl.multiple_of(kb * TQ, TQ), TQ), :].astype(bf16))
    z = _bd(q, k, BNT)
    row = qi * TQ + lax.broadcasted_iota(jnp.int32, (TQ, TQ), 0)
    col = kb * TQ + lax.broadcasted_iota(jnp.int32, (TQ, TQ), 1)
    mask = col < row
    ls = jnp.where(mask, jax.nn.log_sigmoid(-z), 0.0)
    j = lax.broadcasted_iota(jnp.int32, (TQ, TQ), 0)
    u = lax.broadcasted_iota(jnp.int32, (TQ, TQ), 1)
    lw = _run_sum(ls, (j > u).astype(bf16))
    a = jnp.where(mask, jnp.exp(ls + z + lw + carry), 0.0)
    return z, mask, ls, a


def sb_fwd(proj, name):
    s = proj.shape[0]
    nq = s // TQ
    scale = HD ** -0.5

    def body(q_ref, k_ref, v_ref, y_ref):
        qi = pl.program_id(1)
        q = _heads_of((q_ref[...] * scale).astype(bf16))

        def step(j, c):
            acc, carry = c
            kb = qi - j
            _, _, ls, a = _sb_tile(q, k_ref, kb, qi, carry)
            v = _heads_of(v_ref[pl.ds(pl.multiple_of(kb * TQ, TQ), TQ), :].astype(bf16))
            return acc + _bd(a.astype(bf16), v, BNN), carry + jnp.sum(ls, axis=2, keepdims=True)

        acc, _ = lax.fori_loop(0, qi + 1, step, (jnp.zeros((HB, TQ, HD), f32), jnp.zeros((HB, TQ, 1), f32)))
        for h in range(HB):
            y_ref[:, h * HD:(h + 1) * HD] = acc[h].astype(bf16)

    c0, ng = OFF_SB // WB, NH // HB
    return pl.pallas_call(
        body, name=name, grid=(ng, nq),
        in_specs=[pl.BlockSpec((TQ, WB), lambda g, i: (i, c0 + g)), pl.BlockSpec((s, WB), lambda g, i: (0, c0 + ng + g)),
                  pl.BlockSpec((s, WB), lambda g, i: (0, c0 + 2 * ng + g))],
        out_specs=pl.BlockSpec((TQ, WB), lambda g, i: (i, g)), out_shape=_sds((s, WH), bf16),
        compiler_params=_cp(("arbitrary", "arbitrary")),
    )(proj, proj, proj)


def sb_bwd(proj, dy, name):
    s = proj.shape[0]
    nq = s // TQ
    scale = HD ** -0.5

    def body(q_ref, k_ref, v_ref, do_ref, dq_ref, dk_ref, dv_ref, e_scr, z_scr, dk_acc, dv_acc):
        qi = pl.program_id(1)
        q = _heads_of((q_ref[...] * scale).astype(bf16))
        do = _heads_of(do_ref[...])

        @pl.when(qi == 0)
        def _():
            dk_acc[...] = jnp.zeros_like(dk_acc)
            dv_acc[...] = jnp.zeros_like(dv_acc)

        def add_heads(acc_ref, rows, upd):
            for h in range(HB):
                acc_ref[rows, h * HD:(h + 1) * HD] += upd[h]

        def sweep_back(j, carry):
            kb = qi - j
            rows = pl.ds(pl.multiple_of(kb * TQ, TQ), TQ)
            z, _, ls, a = _sb_tile(q, k_ref, kb, qi, carry)
            v = _heads_of(v_ref[rows, :].astype(bf16))
            e_scr[kb] = _bd(do, v, BNT) * a
            z_scr[kb] = z
            add_heads(dv_acc, rows, _bd(a.astype(bf16), do, BTN))
            return carry + jnp.sum(ls, axis=2, keepdims=True)

        lax.fori_loop(0, qi + 1, sweep_back, jnp.zeros((HB, TQ, 1), f32))

        def sweep_fwd(kb, c):
            dq, carry = c
            rows = pl.ds(pl.multiple_of(kb * TQ, TQ), TQ)
            e, z = e_scr[kb], z_scr[kb]
            row = qi * TQ + lax.broadcasted_iota(jnp.int32, (TQ, TQ), 0)
            col = kb * TQ + lax.broadcasted_iota(jnp.int32, (TQ, TQ), 1)
            j = lax.broadcasted_iota(jnp.int32, (TQ, TQ), 0)
            u = lax.broadcasted_iota(jnp.int32, (TQ, TQ), 1)
            ew = _run_sum(e, (j < u).astype(bf16))
            sig = jax.nn.sigmoid(z)
            dz = jnp.where(col < row, e * (1.0 - sig) - (ew + carry) * sig, 0.0).astype(bf16)
            k = _heads_of(k_ref[rows, :].astype(bf16))
            add_heads(dk_acc, rows, _bd(dz, q, BTN))
            return dq + _bd(dz, k, BNN), carry + jnp.sum(e, axis=2, keepdims=True)

        dq, _ = lax.fori_loop(0, qi + 1, sweep_fwd, (jnp.zeros((HB, TQ, HD), f32), jnp.zeros((HB, TQ, 1), f32)))
        for h in range(HB):
            dq_ref[:, h * HD:(h + 1) * HD] = (dq[h] * scale).astype(bf16)

        @pl.when(qi == nq - 1)
        def _():
            dk_ref[...] = dk_acc[...].astype(bf16)
            dv_ref[...] = dv_acc[...].astype(bf16)

    c0, ng = OFF_SB // WB, NH // HB
    return pl.pallas_call(
        body, name=name, grid=(ng, nq),
        in_specs=[pl.BlockSpec((TQ, WB), lambda g, i: (i, c0 + g)), pl.BlockSpec((s, WB), lambda g, i: (0, c0 + ng + g)),
                  pl.BlockSpec((s, WB), lambda g, i: (0, c0 + 2 * ng + g)), pl.BlockSpec((TQ, WB), lambda g, i: (i, g))],
        out_specs=[pl.BlockSpec((TQ, WB), lambda g, i: (i, g)), pl.BlockSpec((s, WB), lambda g, i: (0, g)),
                   pl.BlockSpec((s, WB), lambda g, i: (0, g))],
        out_shape=[_sds((s, WH), bf16)] * 3,
        scratch_shapes=[pltpu.VMEM((nq, HB, TQ, TQ), f32), pltpu.VMEM((nq, HB, TQ, TQ), f32), pltpu.VMEM((s, WB), f32),
                        pltpu.VMEM((s, WB), f32)],
        compiler_params=_cp(("arbitrary", "arbitrary")),
    )(proj, proj, proj, dy)


CB = 256


def _shift_down(v, k, s):
    if k == 0:
        return v
    row = lax.broadcasted_iota(jnp.int32, v.shape, 0)
    return jnp.where(row < k, 0.0, pltpu.roll(v, k, axis=0))


def _shift_up(v, k, s):
    if k == 0:
        return v
    row = lax.broadcasted_iota(jnp.int32, v.shape, 0)
    return jnp.where(row >= s - k, 0.0, pltpu.roll(v, s - k, axis=0))


def conv_fwd(proj, w, name):
    s = proj.shape[0]

    def body(x_ref, w_ref, y_ref):
        x, wv = x_ref[...], w_ref[...]
        y = sum(wv[3 - k:4 - k, :] * _shift_down(x, k, s) for k in range(4))
        y_ref[...] = jax.nn.silu(y)

    return pl.pallas_call(
        body, name=name, grid=(3 * WH // CB,),
        in_specs=[pl.BlockSpec((s, CB), lambda j: (0, OFF_GQ // CB + j)), pl.BlockSpec((4, CB), lambda j: (0, j))],
        out_specs=pl.BlockSpec((s, CB), lambda j: (0, j)), out_shape=_sds((s, 3 * WH), f32),
        compiler_params=_cp(("parallel",)),
    )(proj, w)


def conv_bwd(proj, w, dc, name, after=()):
    s = proj.shape[0]

    def body(*refs):
        x_ref, w_ref, dc_ref, dx_ref, dw_ref = refs[len(after):]
        x, wv = x_ref[...], w_ref[...]
        xs = [_shift_down(x, k, s) for k in range(4)]
        y = sum(wv[3 - k:4 - k, :] * xs[k] for k in range(4))
        sig = jax.nn.sigmoid(y)
        dy = dc_ref[...] * (sig * (1.0 + y * (1.0 - sig)))
        dx_ref[...] = sum(wv[3 - k:4 - k, :] * _shift_up(dy, k, s) for k in range(4)).astype(bf16)
        dw_ref[...] = jnp.concatenate([jnp.sum(dy * xs[3 - i], axis=0, keepdims=True) for i in range(4)], axis=0)

    return pl.pallas_call(
        body, name=name, grid=(3 * WH // CB,),
        in_specs=[pl.BlockSpec(memory_space=pl.ANY)] * len(after)
        + [pl.BlockSpec((s, CB), lambda j: (0, OFF_GQ // CB + j)), pl.BlockSpec((4, CB), lambda j: (0, j)),
           pl.BlockSpec((s, CB), lambda j: (0, j))],
        out_specs=[pl.BlockSpec((s, CB), lambda j: (0, j)), pl.BlockSpec((4, CB), lambda j: (0, j))],
        out_shape=[_sds((s, 3 * WH), bf16), _sds((4, 3 * WH), f32)],
        compiler_params=_cp(("parallel",)),
    )(*after, proj, w, dc)


SOLVE_PASSES = 3


def _pdot_impl(a, b, dn, passes):
    ah, al = _split(a)
    bh, bl = _split(b)
    dot = lambda p, q: lax.dot_general(p, q, dn, preferred_element_type=f32)
    if passes == 1:
        return dot(ah, bh)
    if passes == 2:
        return dot(ah, bh) + dot(ah, bl)
    return dot(ah, bh) + (dot(ah, bl) + dot(al, bh))


BNN, BNT, BTN = (((2,), (1,)), ((0,), (0,))), (((2,), (2,)), ((0,), (0,))), (((1,), (1,)), ((0,), (0,)))


@functools.partial(jax.custom_vjp, nondiff_argnums=(2,))
def _bdot(a, b, passes):
    return _pdot_impl(a, b, BNN, passes)


def _bdot_fwd(a, b, passes):
    return _pdot_impl(a, b, BNN, passes), (a, b)


def _bdot_bwd(passes, res, ct):
    a, b = res
    return _pdot_impl(ct, b, BNT, passes), _pdot_impl(a, ct, BTN, passes)


_bdot.defvjp(_bdot_fwd, _bdot_bwd)


@functools.partial(jax.custom_vjp, nondiff_argnums=(2,))
def _bdot_nt(a, b, passes):
    return _pdot_impl(a, b, BNT, passes)


def _bdot_nt_fwd(a, b, passes):
    return _pdot_impl(a, b, BNT, passes), (a, b)


def _bdot_nt_bwd(passes, res, ct):
    a, b = res
    return _pdot_impl(ct, b, BNN, passes), _pdot_impl(ct, a, BTN, passes)


_bdot_nt.defvjp(_bdot_nt_fwd, _bdot_nt_bwd)


def _lane_pick(v, h):
    lane = lax.broadcasted_iota(jnp.int32, v.shape, v.ndim - 1)
    return jnp.sum(jnp.where(lane == h, v, 0.0), axis=-1, keepdims=True)


def _stack(parts):
    return jnp.concatenate([p[None] for p in parts], axis=0)


def _heads(v, first):
    return _stack([_lane_pick(v, first + h) for h in range(NH)])


def _l2n(v):
    return v * lax.rsqrt(jnp.sum(v * v, axis=-1, keepdims=True) + EPS)


def _dot_nt(a, b):
    return lax.dot_general(a, b, (((1,), (1,)), ((), ())), preferred_element_type=f32)


def _gdn_chunk(cq, ck, cv, ab, alog, dtb):
    ones = jnp.ones((NH, CH, HD), f32)
    q = _l2n(cq) * (HD ** -0.5)
    k = _l2n(ck)
    la = -jnp.exp(_heads(alog, 0)) * jax.nn.softplus(_heads(ab, AB_LANE) + _heads(dtb, 0))
    beta = jax.nn.sigmoid(_heads(ab, AB_LANE + NH)) * ones
    i = lax.broadcasted_iota(jnp.int32, (CH, CH), 0)
    j = lax.broadcasted_iota(jnp.int32, (CH, CH), 1)
    incl, strict = j <= i, j < i
    eye = (i == j).astype(f32)
    g = _bdot(jnp.broadcast_to(incl.astype(f32), (NH, CH, CH)), la * ones, 2)
    g_row = _stack([g[h].T for h in range(NH)])
    gamma = jnp.where(incl, jnp.exp(jnp.where(incl, g - g_row, 0.0)), 0.0)
    lower = jnp.where(strict, beta * _bdot_nt(k, k, 1) * gamma, 0.0)
    inv = eye - lower
    pw = _bdot(lower, lower, SOLVE_PASSES)
    for m in range(1, int(math.log2(CH))):
        inv = inv + _bdot(inv, pw, SOLVE_PASSES)
        if m < int(math.log2(CH)) - 1:
            pw = _bdot(pw, pw, SOLVE_PASSES)
    eg = jnp.exp(g)
    u = _bdot(inv, cv * beta, SOLVE_PASSES)
    w = _bdot(inv, k * (beta * eg), SOLVE_PASSES)
    qk = _bdot_nt(q, k, 1) * gamma
    g_last = g[:, CH - 1:CH, :]
    return u, w, q * eg, k * jnp.exp(g_last - g), qk, jnp.exp(g_last)


def _by_head(ref, t=0):
    return _stack([ref[:, t * WH + h * HD:t * WH + (h + 1) * HD] for h in range(NH)])


def gdn_a_fwd(c, proj, alog, dtb, name):
    s = c.shape[0]
    nc = s // CH

    def body(c_ref, ab_ref, al_ref, dt_ref, u_ref, w_ref, qd_ref, kd_ref, qk_ref, dec_ref):
        res = _gdn_chunk(_by_head(c_ref, 0), _by_head(c_ref, 1), _by_head(c_ref, 2), ab_ref[...], al_ref[...], dt_ref[...])
        for h in range(NH):
            sl = slice(h * HD, (h + 1) * HD)
            for r, v in zip((u_ref, w_ref, qd_ref, kd_ref, qk_ref), res[:5]):
                r[:, sl] = v[h]
            dec_ref[:, sl] = jnp.broadcast_to(res[5][h], (8, HD))

    row = pl.BlockSpec((CH, WH), lambda n: (n, 0))
    par = pl.BlockSpec((1, HD), lambda n: (0, 0))
    return pl.pallas_call(
        body, name=name, grid=(nc,),
        in_specs=[pl.BlockSpec((CH, 3 * WH), lambda n: (n, 0)), pl.BlockSpec((CH, HD), lambda n: (n, OFF_AB // HD)), par, par],
        out_specs=[row] * 5 + [pl.BlockSpec((None, 8, WH), lambda n: (n, 0, 0))],
        out_shape=[_sds((s, WH), f32)] * 5 + [_sds((nc, 8, WH), f32)],
        compiler_params=_cp(("parallel",)),
    )(c, proj, alog, dtb)


def gdn_a_bwd(c, proj, alog, dtb, cots, ddec, name):
    s = c.shape[0]
    nc = s // CH

    def body(c_ref, ab_ref, al_ref, dt_ref, du_ref, dw_ref, dqd_ref, dkd_ref, dqk_ref, ddec_ref,
             dc_ref, dab_ref, dal_ref, ddt_ref):
        n = pl.program_id(0)
        _, vjp = jax.vjp(_gdn_chunk, _by_head(c_ref, 0), _by_head(c_ref, 1), _by_head(c_ref, 2), ab_ref[...], al_ref[...],
                         dt_ref[...])
        lane = lax.broadcasted_iota(jnp.int32, (1, HD), 1)
        dd = _stack([jnp.where(lane == 0, ddec_ref[0:1, h * HD:(h + 1) * HD], 0.0) for h in range(NH)])
        dcq, dck, dcv, dab, dal, ddt = vjp(tuple(_by_head(r) for r in (du_ref, dw_ref, dqd_ref, dkd_ref, dqk_ref)) + (dd,))
        for h in range(NH):
            for t, v in enumerate((dcq, dck, dcv)):
                dc_ref[:, t * WH + h * HD:t * WH + (h + 1) * HD] = v[h]
        dab_ref[...] = dab.astype(bf16)

        @pl.when(n == 0)
        def _():
            dal_ref[...] = dal
            ddt_ref[...] = ddt

        @pl.when(n > 0)
        def _():
            dal_ref[...] += dal
            ddt_ref[...] += ddt

    row = pl.BlockSpec((CH, WH), lambda n: (n, 0))
    wide = pl.BlockSpec((CH, 3 * WH), lambda n: (n, 0))
    par = pl.BlockSpec((1, HD), lambda n: (0, 0))
    return pl.pallas_call(
        body, name=name, grid=(nc,),
        in_specs=[wide, pl.BlockSpec((CH, HD), lambda n: (n, OFF_AB // HD)), par, par] + [row] * 5
        + [pl.BlockSpec((None, 8, WH), lambda n: (n, 0, 0))],
        out_specs=[wide, pl.BlockSpec((CH, HD), lambda n: (n, 0)), par, par],
        out_shape=[_sds((s, 3 * WH), f32), _sds((s, HD), bf16), _sds((1, HD), f32), _sds((1, HD), f32)],
        compiler_params=_cp(("arbitrary",)),
    )(c, proj, alog, dtb, *cots, ddec)


def gdn_b_fwd(u, w, qd, kd, qk, dec, name):
    s = u.shape[0]
    nc = s // CH

    def body(u_ref, w_ref, qd_ref, kd_ref, qk_ref, dec_ref, o_ref, st_ref, state):
        n = pl.program_id(0)

        @pl.when(n == 0)
        def _():
            state[...] = jnp.zeros_like(state)

        for h in range(NH):
            sl = slice(h * HD, (h + 1) * HD)
            st = state[sl, :]
            st_ref[sl, :] = st
            sb = st.astype(bf16)
            vn = u_ref[:, sl] - jnp.dot(w_ref[:, sl].astype(bf16), sb, preferred_element_type=f32)
            vb = vn.astype(bf16)
            o_ref[:, sl] = (jnp.dot(qd_ref[:, sl].astype(bf16), sb, preferred_element_type=f32)
                            + jnp.dot(qk_ref[:, sl].astype(bf16), vb, preferred_element_type=f32))
            state[sl, :] = st * dec_ref[0:1, sl] + lax.dot_general(
                kd_ref[:, sl].astype(bf16), vb, (((0,), (0,)), ((), ())), preferred_element_type=f32)

    row = pl.BlockSpec((CH, WH), lambda n: (n, 0))
    return pl.pallas_call(
        body, name=name, grid=(nc,),
        in_specs=[row] * 5 + [pl.BlockSpec((None, 8, WH), lambda n: (n, 0, 0))],
        out_specs=[row, pl.BlockSpec((None, WH, HD), lambda n: (n, 0, 0))],
        out_shape=[_sds((s, WH), f32), _sds((nc, WH, HD), f32)],
        scratch_shapes=[pltpu.VMEM((WH, HD), f32)],
        compiler_params=_cp(("arbitrary",)),
    )(u, w, qd, kd, qk, dec)


def gdn_b_bwd(u, w, qd, kd, qk, dec, states, do, name):
    s = u.shape[0]
    nc = s // CH

    def body(u_ref, w_ref, qd_ref, kd_ref, qk_ref, dec_ref, st_ref, do_ref,
             du_ref, dw_ref, dqd_ref, dkd_ref, dqk_ref, ddec_ref, dstate):
        n = pl.program_id(0)

        @pl.when(n == 0)
        def _():
            dstate[...] = jnp.zeros_like(dstate)

        for h in range(NH):
            sl = slice(h * HD, (h + 1) * HD)
            st, ds = st_ref[sl, :], dstate[sl, :]
            sb, dsb = st.astype(bf16), ds.astype(bf16)
            wb, qdb, kdb, qkb = (r[:, sl].astype(bf16) for r in (w_ref, qd_ref, kd_ref, qk_ref))
            dob = do_ref[:, sl].astype(bf16)
            vn = u_ref[:, sl] - jnp.dot(wb, sb, preferred_element_type=f32)
            vb = vn.astype(bf16)
            dvn = (lax.dot_general(qkb, dob, (((0,), (0,)), ((), ())), preferred_element_type=f32)
                   + jnp.dot(kdb, dsb, preferred_element_type=f32))
            dvb = dvn.astype(bf16)
            du_ref[:, sl] = dvn
            dw_ref[:, sl] = -_dot_nt(dvb, sb)
            dqd_ref[:, sl] = _dot_nt(dob, sb)
            dkd_ref[:, sl] = _dot_nt(vb, dsb)
            dqk_ref[:, sl] = _dot_nt(dob, vb)
            tot = jnp.sum(jnp.sum(ds * st, axis=1, keepdims=True), axis=0, keepdims=True)
            ddec_ref[:, sl] = jnp.broadcast_to(tot, (8, HD))
            dstate[sl, :] = (ds * dec_ref[0:1, sl]
                             + lax.dot_general(qdb, dob, (((0,), (0,)), ((), ())), preferred_element_type=f32)
                             - lax.dot_general(wb, dvb, (((0,), (0,)), ((), ())), preferred_element_type=f32))

    row = pl.BlockSpec((CH, WH), lambda n: (nc - 1 - n, 0))
    small = pl.BlockSpec((None, 8, WH), lambda n: (nc - 1 - n, 0, 0))
    return pl.pallas_call(
        body, name=name, grid=(nc,),
        in_specs=[row] * 5 + [small, pl.BlockSpec((None, WH, HD), lambda n: (nc - 1 - n, 0, 0)), row],
        out_specs=[row] * 5 + [small],
        out_shape=[_sds((s, WH), f32)] * 5 + [_sds((nc, 8, WH), f32)],
        scratch_shapes=[pltpu.VMEM((WH, HD), f32)],
        compiler_params=_cp(("arbitrary",)),
    )(u, w, qd, kd, qk, dec, states, do)


ANY = pl.BlockSpec(memory_space=pl.ANY)


def _dev_index(p):
    return 4 * p[0] + 2 * p[1] + p[2]


def _shard_of(ref, axis, size, idx):
    return ref.at[pl.ds(idx * size, size), :] if axis == 0 else ref.at[:, pl.ds(idx * size, size)]


def _peer(x, y, c, r):
    return (1 - x if r & 4 else x, 1 - y if r & 2 else y, 1 - c if r & 1 else c)


ALL_PEERS, SIBLING, OTHER_CHIPS = tuple(range(1, NDEV)), (1,), (4, 2, 6)


def _launch(body, ins, out_sds, name, sequencer_id, relations=ALL_PEERS, kinds=7):
    n = len(ins)
    sems = [pltpu.SemaphoreType.DMA((n, kinds)), pltpu.SemaphoreType.DMA((n, kinds)), pltpu.SemaphoreType.DMA((n,))]
    if sequencer_id is None:
        return pl.pallas_call(
            lambda *refs: body(refs[:n], refs[n:2 * n], *refs[2 * n:]), name=name, in_specs=[ANY] * n, out_specs=[ANY] * n,
            out_shape=out_sds, scratch_shapes=sems, compiler_params=pltpu.CompilerParams(has_side_effects=True),
        )(*ins)
    in_refs = [jax.new_ref(a, memory_space=pltpu.MemorySpace.HBM) for a in ins]
    out_refs = [jax.empty_ref(sd, memory_space=pltpu.MemorySpace.HBM) for sd in out_sds]

    @pl.kernel(mesh=plsc.ScalarSubcoreMesh(axis_name="sequencer", num_cores=1), name=name, scratch_types=sems,
               compiler_params=pltpu.CompilerParams(collective_id=sequencer_id))
    def launch(send, recv, loc):
        x, y, c = lax.axis_index("x"), lax.axis_index("y"), lax.axis_index("c")
        barrier = pltpu.get_barrier_semaphore()
        for r in relations:
            pl.semaphore_signal(barrier, inc=1, device_id=_peer(x, y, c, r), device_id_type=MESH)
        pl.semaphore_wait(barrier, len(relations))
        body(in_refs, out_refs, send, recv, loc)

    launch()
    return [r[...] for r in out_refs]


def all_gather(xs, axes, name, sequencer_id=None):
    n = len(xs)
    fulls = [tuple(d * (NDEV if a == ax else 1) for a, d in enumerate(x.shape)) for x, ax in zip(xs, axes)]
    halved = [x.shape[0] % 32 == 0 for x in xs]

    def body(x_refs, o_refs, send, recv, loc):
        x, y, c = lax.axis_index("x"), lax.axis_index("y"), lax.axis_index("c")
        me, sib = (x, y, c), (x, y, 1 - c)
        xn, yn, dg = (1 - x, y), (x, 1 - y), (1 - x, 1 - y)

        def part(t, p, half=None):
            ref = _shard_of(o_refs[t], axes[t], xs[t].shape[axes[t]], _dev_index(p))
            rows = xs[t].shape[0] // 2
            return ref if half is None else ref.at[pl.ds(half * rows, rows), :]

        def copy(t, k, block, to, half=None, src=None):
            return pltpu.make_async_remote_copy(
                src_ref=part(t, block, half) if src is None else src, dst_ref=part(t, block, half),
                send_sem=send.at[t, k], recv_sem=recv.at[t, k], device_id=to, device_id_type=MESH)

        mine = [pltpu.make_async_copy(x_refs[t], part(t, me), loc.at[t]) for t in range(n)]
        for cp in mine:
            cp.start()
        sends = []
        for t in range(n):
            sends += [copy(t, 0, me, sib, src=x_refs[t]), copy(t, 1, me, (*xn, c), src=x_refs[t]),
                      copy(t, 2, me, (*yn, c), src=x_refs[t])]
            if not halved[t]:
                sends.append(copy(t, 3, me, (*dg, c), src=x_refs[t]))
        for cp in sends:
            cp.start()

        def pass_on(cp):
            cp.start()
            sends.append(cp)

        for t in range(n):
            h0, h1 = (0, 1) if halved[t] else (None, None)
            copy(t, 1, (*xn, c), me).wait_recv()
            if halved[t]:
                pass_on(copy(t, 3, (*xn, c), (*yn, c), 0))
            pass_on(copy(t, 5, (*xn, c), sib))
            copy(t, 2, (*yn, c), me).wait_recv()
            if halved[t]:
                pass_on(copy(t, 4, (*yn, c), (*xn, c), 1))
            pass_on(copy(t, 6, (*yn, c), sib))
            copy(t, 3, (*dg, c), me, h0).wait_recv()
            pass_on(copy(t, 7, (*dg, c), sib, h0))
            if halved[t]:
                copy(t, 4, (*dg, c), me, h1).wait_recv()
                pass_on(copy(t, 8, (*dg, c), sib, h1))
        for t in range(n):
            h0, h1 = (0, 1) if halved[t] else (None, None)
            copy(t, 0, sib, me).wait_recv()
            copy(t, 5, (*xn, 1 - c), me).wait_recv()
            copy(t, 6, (*yn, 1 - c), me).wait_recv()
            copy(t, 7, (*dg, 1 - c), me, h0).wait_recv()
            if halved[t]:
                copy(t, 8, (*dg, 1 - c), me, h1).wait_recv()
        for cp in sends:
            cp.wait_send()
        for cp in mine:
            cp.wait()

    return _launch(body, xs, [_sds(f, x.dtype) for f, x in zip(fulls, xs)], name, sequencer_id, kinds=9)


def pair_swap(gs, axes, name, sequencer_id):
    n = len(gs)
    shards = [tuple(d // (NDEV if a == ax else 1) for a, d in enumerate(g.shape)) for g, ax in zip(gs, axes)]

    def body(g_refs, p_refs, send, recv, loc):
        x, y, c = lax.axis_index("x"), lax.axis_index("y"), lax.axis_index("c")

        def copy(t, j):
            return pltpu.make_async_remote_copy(
                src_ref=_shard_of(g_refs[t], axes[t], shards[t][axes[t]], 2 * j + (1 - c)), dst_ref=p_refs[t].at[j],
                send_sem=send.at[t, j], recv_sem=recv.at[t, j], device_id=(x, y, 1 - c), device_id_type=MESH)

        copies = [copy(t, j) for t in range(n) for j in range(4)]
        for cp in copies:
            cp.start()
        for cp in copies:
            cp.wait_recv()
        for cp in copies:
            cp.wait_send()

    return _launch(body, gs, [_sds((4,) + sh, g.dtype) for sh, g in zip(shards, gs)], name, sequencer_id, SIBLING)


def pair_add(g, p, axis, name, after=()):
    _, rows, cols = p.shape
    tr = _row_tile(rows, cols, 1 << 20)
    nb = rows // tr
    if axis == 0:
        g_spec = pl.BlockSpec((tr, cols), lambda j, i, c: ((2 * j + c[0]) * nb + i, 0))
    else:
        g_spec = pl.BlockSpec((tr, cols), lambda j, i, c: (i, 2 * j + c[0]))
    blk = pl.BlockSpec((None, tr, cols), lambda j, i, c: (j, i, 0))

    na = len(after)

    def body(c_ref, *refs):
        g_ref, p_ref, q_ref = refs[na:]
        q_ref[...] = (g_ref[...].astype(f32) + p_ref[...].astype(f32)).astype(bf16)

    return pl.pallas_call(
        body, name=name, out_shape=_sds(p.shape, bf16),
        grid_spec=pltpu.PrefetchScalarGridSpec(num_scalar_prefetch=1, grid=(4, nb), in_specs=[ANY] * na + [g_spec, blk],
                                               out_specs=blk),
        compiler_params=_cp(("parallel", "parallel")),
    )(lax.axis_index("c").astype(jnp.int32).reshape(1), *after, g, p)


def chip_exchange(qs, name, sequencer_id):
    n = len(qs)

    def body(q_refs, r_refs, send, recv, loc):
        x, y, c = lax.axis_index("x"), lax.axis_index("y"), lax.axis_index("c")
        my_chip = 2 * x + y
        peers = [_peer(x, y, c, r) for r in OTHER_CHIPS]
        mine = [pltpu.make_async_copy(q_refs[t].at[my_chip], r_refs[t].at[my_chip], loc.at[t]) for t in range(n)]
        for cp in mine:
            cp.start()

        def copy(t, k, slot):
            p = peers[k]
            return pltpu.make_async_remote_copy(
                src_ref=q_refs[t].at[2 * p[0] + p[1]], dst_ref=r_refs[t].at[slot], send_sem=send.at[t, k], recv_sem=recv.at[t, k],
                device_id=p, device_id_type=MESH)

        sends = [copy(t, k, my_chip) for t in range(n) for k in range(3)]
        for cp in sends:
            cp.start()
        for t in range(n):
            for k in range(3):
                copy(t, k, 2 * peers[k][0] + peers[k][1]).wait_recv()
        for cp in sends:
            cp.wait_send()
        for cp in mine:
            cp.wait()

    return _launch(body, qs, [_sds(q.shape, q.dtype) for q in qs], name, sequencer_id, OTHER_CHIPS)


def _adamw(w, g, m, v):
    m = ADAM_B1 * m + (1.0 - ADAM_B1) * g
    v = ADAM_B2 * v + (1.0 - ADAM_B2) * jnp.square(g)
    m_hat = m / (1.0 - ADAM_B1 ** ADAM_STEP)
    v_hat = v / (1.0 - ADAM_B2 ** ADAM_STEP)
    return -ADAM_LR * (m_hat / (jnp.sqrt(v_hat) + ADAM_EPS) + ADAM_WD * w), m, v


def _sum8(r_ref):
    g = r_ref[0].astype(f32)
    for j in range(1, r_ref.shape[0]):
        g = g + r_ref[j].astype(f32)
    return g


def _row_tile(rows, cols, elems=1 << 18):
    tr = min(rows, max(8, 1 << int(math.log2(elems / cols))))
    assert rows % tr == 0, (rows, cols)
    return tr


def sum_partials(r, name, after=()):
    _, rows, cols = r.shape
    tr = _row_tile(rows, cols)
    na = len(after)

    def body(*refs):
        refs[na + 1][...] = _sum8(refs[na])

    return pl.pallas_call(
        body, name=name, grid=(rows // tr,),
        in_specs=[ANY] * na + [pl.BlockSpec((r.shape[0], tr, cols), lambda i: (0, i, 0))],
        out_specs=pl.BlockSpec((tr, cols), lambda i: (i, 0)), out_shape=_sds((rows, cols), f32),
        compiler_params=_cp(("parallel",)),
    )(*after, r)


def adamw_t(w, m, v, grads, name):
    rows, nl, cols = w.shape
    tr = min(rows, (1 << 16) // cols)
    blk = pl.BlockSpec((tr, nl, cols), lambda i: (i, 0, 0))
    flat = pl.BlockSpec((tr, cols), lambda i: (i, 0))

    def body(w_ref, m_ref, v_ref, *rest):
        g_refs, (g_ref, d_ref, nm_ref, nv_ref) = rest[:nl], rest[nl:]
        for l in range(nl):
            grad = g_refs[l][...]
            g_ref[:, l, :] = grad
            d_ref[:, l, :], nm_ref[:, l, :], nv_ref[:, l, :] = _adamw(w_ref[:, l, :], grad, m_ref[:, l, :], v_ref[:, l, :])

    return tuple(pl.pallas_call(
        body, name=name, grid=(pl.cdiv(rows, tr),), in_specs=[blk] * 3 + [flat] * nl, out_specs=[blk] * 4,
        out_shape=[_sds(w.shape, f32)] * 4, compiler_params=_cp(("parallel",)),
    )(w, m, v, *grads))


def adamw(w, m, v, layer, name, r=None, g=None, prev=None):
    _, rows, cols = w.shape
    tr = _row_tile(rows, cols)
    blk = pl.BlockSpec((None, tr, cols), lambda i: (layer, i, 0))
    nprev = 0 if prev is None else 4

    def body(w_ref, m_ref, v_ref, src_ref, *rest):
        g_ref, d_ref, nm_ref, nv_ref, token_ref = rest[nprev:]
        grad = _sum8(src_ref) if g is None else src_ref[...]
        g_ref[...] = grad
        d_ref[...], nm_ref[...], nv_ref[...] = _adamw(w_ref[...], grad, m_ref[...], v_ref[...])
        token_ref[...] = jnp.zeros_like(token_ref)

    src, src_spec = (r, pl.BlockSpec((r.shape[0], tr, cols), lambda i: (0, i, 0))) if g is None else (g, pl.BlockSpec((tr, cols), lambda i: (i, 0)))
    *outs, token = pl.pallas_call(
        body, name=name, grid=(rows // tr,), in_specs=[blk] * 3 + [src_spec] + [ANY] * nprev,
        out_specs=[blk] * 4 + [pl.BlockSpec((8, HD), lambda i: (0, 0))],
        out_shape=[_sds(w.shape, f32)] * 4 + [_sds((8, HD), f32)], input_output_aliases={4 + k: k for k in range(nprev)},
        compiler_params=_cp(("arbitrary",)),
    )(w, m, v, src, *(prev or ()))
    return tuple(outs), token


def small_adamw(parts, w, m, v, name):
    def body(p_ref, w_ref, m_ref, v_ref, g_ref, d_ref, nm_ref, nv_ref):
        g = _sum8(p_ref)
        g_ref[...] = g
        d_ref[...], nm_ref[...], nv_ref[...] = _adamw(w_ref[...], g, m_ref[...], v_ref[...])

    return pl.pallas_call(body, name=name, out_shape=[_sds(w.shape, f32)] * 4, compiler_params=_cp())(parts, w, m, v)


def _pack_rows(wt):
    tail = jnp.pad(wt[5888:N_IN], ((OFF_G - 12 - (OFF_P + 512), 0), (0, 0)))
    return jnp.concatenate([wt[512:2816], wt[2816:5120], wt[5120:5888], wt[0:512], tail], axis=0)


def _unpack_rows(g):
    return jnp.concatenate([g[OFF_P:OFF_P + 512], g[OFF_SB:OFF_SB + 2304], g[OFF_GQ:OFF_GQ + 2304], g[OFF_Z:OFF_Z + 768],
                            g[OFF_G - 12:NP]], axis=0)


def _lanes(v):
    flat = v.reshape(-1)
    n = -(-flat.shape[0] // HD) * HD
    return jnp.pad(flat, (0, n - flat.shape[0])).reshape(n // HD, HD)


def _lanes8(v):
    rows = _lanes(v)
    return jnp.pad(rows, ((0, -rows.shape[0] % 8), (0, 0)))


def _layer_fwd(x, p, l):
    nm = lambda s: f"{s}_l{l}"
    u = rms_fwd(x, p["attn_norm"], nm("rms1"))
    proj = matmul(u, p["w_in"], name=nm("inproj"), tb=True)
    y_pool = pool_fwd(proj, p["pool_w"], p["pool_scale"], nm("pool"))
    y_sb = sb_fwd(proj, nm("sb"))
    c = conv_fwd(proj, p["conv"], nm("conv"))
    ga = gdn_a_fwd(c, proj, p["alog"], p["dtb"], nm("gdna"))
    o, states = gdn_b_fwd(*ga, nm("gdnb"))
    y_gdn = gdn_out_fwd(o, proj, p["gdn_norm"], nm("gdno"))
    ups = [matmul(y, p[k], name=nm(k)) for y, k in ((y_pool, "w_pool_up"), (y_sb, "w_sb_up"), (y_gdn, "w_gdn_up"))]
    merged = merge_fwd(proj, ups, nm("merge"))
    x1 = matmul(merged, p["w_out"], name=nm("outproj"), epilogue=lambda acc, r: acc + r, extras=(x,))
    u2 = rms_fwd(x1, p["mlp_norm"], nm("rms2"))
    h2 = matmul(u2, p["w_ff1"], name=nm("ff1"), out_dtype=bf16, epilogue=lambda acc: jnp.square(jnp.maximum(acc, 0.0)))
    x2 = matmul(h2, p["w_ff2"], name=nm("ff2"), epilogue=lambda acc, r: acc + r, extras=(x1,))
    saved = dict(x=x, u=u, proj=proj, y_pool=y_pool, y_sb=y_sb, c=c, ga=ga, o=o, states=states, y_gdn=y_gdn, ups=ups,
                 merged=merged, x1=x1, u2=u2, h2=h2)
    return x2, saved


def _layer_bwd(dx2, dx2b, sv, p, l, swap, finish):
    nm = lambda s: f"{s}_l{l}"
    s = dx2.shape[0]
    dh = matmul(dx2b, p["w_ff2"], name=nm("d_ff2_x"), tb=True, out_dtype=bf16,
                epilogue=lambda acc, h2: acc * (2.0 * jnp.sqrt(h2.astype(f32))), extras=(sv["h2"],))
    g_ff2 = matmul(sv["h2"], dx2b, name=nm("d_ff2_w"), ta=True, out_dtype=bf16)
    du2 = matmul(dh, p["w_ff1"], name=nm("d_ff1_x"), tb=True)
    g_ff1 = matmul(sv["u2"], dh, name=nm("d_ff1_w"), ta=True, out_dtype=bf16)
    dx1, dx1b, d_mlp_norm = rms_bwd(sv["x1"], du2, dx2, p["mlp_norm"], nm("d_rms2"))
    dmerged = matmul(dx1b, p["w_out"], name=nm("d_out_x"), tb=True, out_dtype=bf16)
    g_out = matmul(sv["merged"], dx1b, name=nm("d_out_w"), ta=True, out_dtype=bf16)
    dgates, *dups = merge_bwd(sv["proj"], sv["ups"], dmerged, nm("d_merge"))
    dys, g_ups = [], []
    for dup, y, k in zip(dups, (sv["y_pool"], sv["y_sb"], sv["y_gdn"]), ("w_pool_up", "w_sb_up", "w_gdn_up")):
        dys.append(matmul(dup, p[k], name=nm("d_" + k + "_x"), tb=True, out_dtype=bf16))
        g_ups.append(matmul(y, dup, name=nm("d_" + k + "_w"), ta=True, out_dtype=bf16))
    early = g_ups + [g_out, g_ff1, g_ff2]
    swapped = swap(early, BIG_AXES[1:], "a")
    do, dz, d_gdn_norm = gdn_out_bwd(sv["o"], sv["proj"], p["gdn_norm"], dys[2], nm("d_gdno"))
    cots = gdn_b_bwd(*sv["ga"], sv["states"], do, nm("d_gdnb"))
    dc, dab, d_alog, d_dtb = gdn_a_bwd(sv["c"], sv["proj"], p["alog"], p["dtb"], cots[:5], cots[5], nm("d_gdna"))
    r_early, sent_early = finish(early, swapped, BIG_AXES[1:], BIG_NAMES[1:], "a", (dab,))
    dgq, d_conv = conv_bwd(sv["proj"], p["conv"], dc, nm("d_conv"), sent_early)
    dq, dk, dv = sb_bwd(sv["proj"], dys[1], nm("d_sb"))
    dp, d_pool_w, d_pool_scale = pool_bwd(sv["proj"], p["pool_w"], p["pool_scale"], dys[0], nm("d_pool"))
    dproj = jnp.concatenate([dq, dk, dv, dgq, dz, dp, jnp.zeros((s, OFF_AB - OFF_P - W_POOL), bf16), dab, dgates], axis=1)
    g_in = matmul(dproj, sv["u"], name=nm("d_in_w"), ta=True, out_dtype=bf16)
    r_in, sent_in = finish([g_in], swap([g_in], BIG_AXES[:1], "b"), BIG_AXES[:1], BIG_NAMES[:1], "b", ())
    du = matmul(dproj, p["w_in"], name=nm("d_in_x"), after=sent_in)
    dx, dxb, d_attn_norm = rms_bwd(sv["x"], du, dx1, p["attn_norm"], nm("d_rms1"))
    recv = r_in + r_early
    small = [d_attn_norm, d_pool_w, d_pool_scale, d_conv, d_alog, d_dtb, d_gdn_norm, d_mlp_norm]
    return dx, dxb, recv, small


BIG_AXES = (1, 1, 1, 1, 0, 1, 0)
GATHER_ID, EXCHANGE_ID = 1, 2
BIG_NAMES = ("w_in", "w_pool_up", "w_sb_up", "w_gdn_up", "w_out", "w_ff1", "w_ff2")


def kernel(x, attn_norm, w_in, pool_w, pool_scale, gdn_conv, gdn_a_log, gdn_dt_bias, gdn_norm, w_pool_up, w_sb_up, w_gdn_up, w_out, mlp_norm, w_ff1, w_ff2, final_norm, loss_target, m_attn_norm, m_w_in, m_pool_w, m_pool_scale, m_gdn_conv, m_gdn_a_log, m_gdn_dt_bias, m_gdn_norm, m_w_pool_up, m_w_sb_up, m_w_gdn_up, m_w_out, m_mlp_norm, m_w_ff1, m_w_ff2, m_final_norm, v_attn_norm, v_w_in, v_pool_w, v_pool_scale, v_gdn_conv, v_gdn_a_log, v_gdn_dt_bias, v_gdn_norm, v_w_pool_up, v_w_sb_up, v_w_gdn_up, v_w_out, v_mlp_norm, v_w_ff1, v_w_ff2, v_final_norm):
    s = x.shape[1]
    me = _dev_index((lax.axis_index("x"), lax.axis_index("y"), lax.axis_index("c")))
    ncv = gdn_conv.shape[2]

    w_in_t, m_w_in_t, v_w_in_t = (jnp.transpose(a, (2, 0, 1)) for a in (w_in, m_w_in, v_w_in))
    full = []
    for l in range(NL):
        shards = [_pack_rows(w_in_t[:, l]).astype(bf16), w_pool_up[l].astype(bf16), w_sb_up[l].astype(bf16),
                  w_gdn_up[l].astype(bf16), w_out[l].astype(bf16), w_ff1[l].astype(bf16), w_ff2[l].astype(bf16)]
        if l == 0:
            first = all_gather([shards[0], gdn_conv.reshape(NL * 4, ncv)], (1, 0), "gather_first", sequencer_id=GATHER_ID)
            rest = all_gather(shards[1:], BIG_AXES[1:], "gather_rest_l0", sequencer_id=GATHER_ID)
            full.append([first[0]] + rest)
            conv_full = first[1].reshape(NDEV, NL, 4, ncv).transpose(1, 2, 0, 3).reshape(NL, 4, NDEV * ncv)
        else:
            full.append(all_gather(shards, BIG_AXES, f"gather_weights_l{l}", sequencer_id=GATHER_ID))
    params = []
    for l in range(NL):
        p = dict(zip(("w_in", "w_pool_up", "w_sb_up", "w_gdn_up", "w_out", "w_ff1", "w_ff2"), full[l][:7]))
        p.update(attn_norm=attn_norm[l][None], mlp_norm=mlp_norm[l][None], pool_w=pool_w[l], pool_scale=pool_scale[l][None],
                 conv=conv_full[l], alog=_lanes(gdn_a_log[l]), dtb=_lanes(gdn_dt_bias[l]), gdn_norm=gdn_norm[l][None])
        params.append(p)

    h = x[0]
    saved = []
    for l in range(NL):
        h, sv = _layer_fwd(h, params[l], l)
        saved.append(sv)
    dh, dhb, loss_row, d_final = loss_head(h, loss_target[0], final_norm[None], "loss_head")
    recv, smalls = [None] * NL, [None] * NL
    for l in reversed(range(NL)):
        def swap(gs, axes, tag, l=l):
            return pair_swap(gs, axes, f"swap_{tag}_l{l}", None)

        def finish(gs, ps, axes, names, tag, after, l=l):
            qs = [pair_add(g, p, ax, f"pair_add_{k}_l{l}", after) for g, p, ax, k in zip(gs, ps, axes, names)]
            return chip_exchange(qs, f"exchange_{tag}_l{l}", EXCHANGE_ID), tuple(qs)

        dh, dhb, recv[l], smalls[l] = _layer_bwd(dh, dhb, saved[l], params[l], l, swap, finish)

    small_rows = [_lanes8(t) for l in range(NL) for t in smalls[l]] + [_lanes8(d_final), _lanes8(loss_row)]
    packed = jnp.concatenate(small_rows, axis=0)
    parts = all_gather([packed], (0,), "gather_small")[0].reshape(NDEV, packed.shape[0], HD)

    def pack_small(tree):
        rows = []
        for l in range(NL):
            rows += [_lanes8(tree["attn_norm"][l]), _lanes8(tree["pool_w"][l]), _lanes8(tree["pool_scale"][l]),
                     jnp.zeros((4 * NDEV * ncv // HD, HD), f32), _lanes8(tree["gdn_a_log"][l]), _lanes8(tree["gdn_dt_bias"][l]),
                     _lanes8(tree["gdn_norm"][l]), _lanes8(tree["mlp_norm"][l])]
        rows += [_lanes8(tree["final_norm"]), jnp.zeros((8, HD), f32)]
        return jnp.concatenate(rows, axis=0)

    names = ("attn_norm", "pool_w", "pool_scale", "gdn_a_log", "gdn_dt_bias", "gdn_norm", "mlp_norm", "final_norm")
    w_small = pack_small(dict(zip(names, (attn_norm, pool_w, pool_scale, gdn_a_log, gdn_dt_bias, gdn_norm, mlp_norm, final_norm))))
    m_small = pack_small(dict(zip(names, (m_attn_norm, m_pool_w, m_pool_scale, m_gdn_a_log, m_gdn_dt_bias, m_gdn_norm, m_mlp_norm, m_final_norm))))
    v_small = pack_small(dict(zip(names, (v_attn_norm, v_pool_w, v_pool_scale, v_gdn_a_log, v_gdn_dt_bias, v_gdn_norm, v_mlp_norm, v_final_norm))))
    small_out = small_adamw(parts, w_small, m_small, v_small, "adamw_small")

    def unpack_small(buf):
        out, conv_g, r = {}, [], 0
        layer_items = (("attn_norm", (D,)), ("pool_w", (4, HD, HD)), ("pool_scale", (W_POOL,)), ("conv", (4, NDEV * ncv)),
                       ("gdn_a_log", (NH,)), ("gdn_dt_bias", (NH,)), ("gdn_norm", (HD,)), ("mlp_norm", (D,)))
        per_layer = {k: [] for k, _ in layer_items}
        for l in range(NL):
            for k, shape in layer_items:
                size = math.prod(shape)
                nrow = -(-size // (8 * HD)) * 8
                per_layer[k].append(buf[r:r + nrow].reshape(-1)[:size].reshape(shape))
                r += nrow
        for k, _ in layer_items:
            out[k] = jnp.stack(per_layer[k])
        out["final_norm"] = buf[r:r + D // HD].reshape(D)
        out["loss"] = buf[r + D // HD, 0]
        return out

    sm = [unpack_small(b) for b in small_out]
    loss = sm[0]["loss"]
    g_conv = lax.dynamic_slice_in_dim(sm[0]["conv"], me * ncv, ncv, axis=2)
    conv_out = None
    for l in reversed(range(NL)):
        conv_out, _ = adamw(gdn_conv, m_gdn_conv, v_gdn_conv, l, f"adamw_conv_l{l}", g=g_conv[l], prev=conv_out)

    big_out = {}
    big_names = BIG_NAMES
    big_w = dict(zip(big_names, ((w_in_t, m_w_in_t, v_w_in_t), (w_pool_up, m_w_pool_up, v_w_pool_up), (w_sb_up, m_w_sb_up, v_w_sb_up),
                                 (w_gdn_up, m_w_gdn_up, v_w_gdn_up), (w_out, m_w_out, v_w_out), (w_ff1, m_w_ff1, v_w_ff1),
                                 (w_ff2, m_w_ff2, v_w_ff2))))
    tokens = []
    for l, k in [(l, k) for l in reversed(range(NL)) for k in big_names[1:]]:
        w, m, v = big_w[k]
        big_out[k], token = adamw(w, m, v, l, f"adamw_{k}_l{l}", r=recv[l][big_names.index(k)], prev=big_out.get(k))
        tokens.append(token)
    g_in = [_unpack_rows(sum_partials(recv[l][0], f"sum_w_in_l{l}", tuple(tokens) if l == 0 else ())) for l in range(NL)]
    big_out["w_in"] = adamw_t(*big_w["w_in"], g_in, "adamw_w_in")

    def leaf(i, k):
        if k == "w_in":
            return jnp.transpose(big_out[k][i], (1, 2, 0))
        if k in big_out:
            return big_out[k][i]
        if k == "gdn_conv":
            return conv_out[i]
        return sm[i][k]

    order = ("attn_norm", "w_in", "pool_w", "pool_scale", "gdn_conv", "gdn_a_log", "gdn_dt_bias", "gdn_norm", "w_pool_up",
             "w_sb_up", "w_gdn_up", "w_out", "mlp_norm", "w_ff1", "w_ff2", "final_norm")
    return (loss, dh[None]) + tuple(leaf(i, k) for i in range(4) for k in order)
```

```python
import functools
import math

import jax
import jax.numpy as jnp
from jax import lax
from jax.experimental import pallas as pl
from jax.experimental.pallas import tpu as pltpu
from jax.experimental.pallas import tpu_sc as plsc

f32, bf16 = jnp.float32, jnp.bfloat16

D = 2048
NDEV = 8
NL = 2
HD = 128
NH = 6
WH = NH * HD
W_POOL = 512
EPS = 1e-6
N_IN = 12044
NP = 12288
OFF_SB, OFF_GQ, OFF_Z, OFF_P, OFF_AB, OFF_G = 0, 2304, 4608, 5376, 6016, 6144
AB_LANE = HD - 2 * NH
POOL_WINDOWS = (2, 4, 8, 16)
CH = 128
TQ = 256
VMEM_LIMIT = 56 * 1024 * 1024
ADAM_LR, ADAM_B1, ADAM_B2, ADAM_EPS, ADAM_WD, ADAM_STEP = 0.001, 0.9, 0.999, 1e-08, 0.01, 10
MESH = pl.DeviceIdType.MESH


def _cp(sem=None):
    return pltpu.CompilerParams(dimension_semantics=sem, vmem_limit_bytes=VMEM_LIMIT)


def _sds(shape, dtype):
    return jax.ShapeDtypeStruct(tuple(shape), dtype)


def matmul(a, b, *, name, ta=False, tb=False, out_dtype=f32, tm=1024, tn=1024, tk=2048, epilogue=None, extras=(), after=()):
    m, k = (a.shape[1], a.shape[0]) if ta else a.shape
    n = b.shape[0] if tb else b.shape[1]
    assert k == (b.shape[1] if tb else b.shape[0]) and a.dtype == bf16 and b.dtype == bf16
    tm, tn, tk = min(tm, m), min(tn, n), min(tk, k)
    assert m % tm == 0 and n % tn == 0 and k % tk == 0, (m, n, k, tm, tn, tk)
    nk = k // tk
    a_spec = pl.BlockSpec((tk, tm), lambda i, j, q: (q, i)) if ta else pl.BlockSpec((tm, tk), lambda i, j, q: (i, q))
    b_spec = pl.BlockSpec((tn, tk), lambda i, j, q: (j, q)) if tb else pl.BlockSpec((tk, tn), lambda i, j, q: (q, j))
    e_specs = [pl.BlockSpec((tm, tn), lambda i, j, q: (i, j)) for _ in extras]
    dn = (((0 if ta else 1,), (1 if tb else 0,)), ((), ()))
    ne = len(extras)

    def body(*refs):
        a_ref, b_ref, *rest = refs[len(after):]
        e_refs, o_ref = rest[:ne], rest[ne]
        part = lax.dot_general(a_ref[...], b_ref[...], dn, preferred_element_type=f32)

        def finish(acc):
            if epilogue is not None:
                acc = epilogue(acc, *[e[...] for e in e_refs])
            o_ref[...] = acc.astype(out_dtype)

        if nk == 1:
            finish(part)
        else:
            acc_ref = rest[ne + 1]
            q = pl.program_id(2)

            @pl.when(q == 0)
            def _():
                acc_ref[...] = part

            @pl.when(jnp.logical_and(q > 0, q < nk - 1))
            def _():
                acc_ref[...] += part

            @pl.when(q == nk - 1)
            def _():
                finish(acc_ref[...] + part)

    return pl.pallas_call(
        body, name=name, grid=(m // tm, n // tn, nk),
        in_specs=[pl.BlockSpec(memory_space=pl.ANY)] * len(after) + [a_spec, b_spec] + e_specs,
        out_specs=pl.BlockSpec((tm, tn), lambda i, j, q: (i, j)), out_shape=_sds((m, n), out_dtype),
        scratch_shapes=[pltpu.VMEM((tm, tn), f32)] if nk > 1 else [],
        compiler_params=_cp(("parallel", "parallel", "arbitrary")),
    )(*after, a, b, *extras)


def rowwise(name, fn, rows, params, outs, sums=(), tr=256, after=()):
    s = rows[0][0].shape[0]
    tr = min(tr, s)
    nin, nout = len(rows) + len(params), len(outs)
    in_specs = [pl.BlockSpec((tr, w), functools.partial(lambda i, c: (i, c), c=c)) for (_, w, c) in rows]
    in_specs += [pl.BlockSpec(p.shape, lambda i: (0, 0)) for p in params]
    out_specs = [pl.BlockSpec((tr, w), lambda i: (i, 0)) for (w, _) in outs]
    out_specs += [pl.BlockSpec(sh, lambda i: (0, 0)) for sh in sums]
    out_shape = [_sds((s, w), dt) for (w, dt) in outs] + [_sds(sh, f32) for sh in sums]

    def body(*refs):
        refs = refs[len(after):]
        res = fn(*[r[...] for r in refs[:nin]])
        for r, v in zip(refs[nin:nin + nout], res[:nout]):
            r[...] = v.astype(r.dtype)
        i = pl.program_id(0)
        for r, v in zip(refs[nin + nout:], res[nout:]):
            @pl.when(i == 0)
            def _(r=r, v=v):
                r[...] = v

            @pl.when(i > 0)
            def _(r=r, v=v):
                r[...] += v

    res = pl.pallas_call(
        body, name=name, grid=(s // tr,), in_specs=[pl.BlockSpec(memory_space=pl.ANY)] * len(after) + in_specs,
        out_specs=out_specs, out_shape=out_shape, compiler_params=_cp(("arbitrary",)),
    )(*after, *[r[0] for r in rows], *params)
    return res


def _rms(x, g):
    return x * lax.rsqrt(jnp.mean(x * x, axis=-1, keepdims=True) + EPS) * g


def rms_fwd(x, g, name):
    return rowwise(name, lambda xb, gb: (_rms(xb, gb),), [(x, D, 0)], [g], [(D, bf16)])[0]


def rms_bwd(x, du, dres, g, name, after=()):
    def fn(xb, dub, drb, gb):
        _, vjp = jax.vjp(_rms, xb, gb)
        dx, dg = vjp(dub.astype(f32))
        return drb + dx, drb + dx, dg

    return rowwise(name, fn, [(x, D, 0), (du, D, 0), (dres, D, 0)], [g], [(D, f32), (D, bf16)], [(1, D)], after=after)


def _merge(gates, up_p, up_s, up_g):
    sg = jax.nn.sigmoid(gates)
    return sg[:, :D] * up_p + sg[:, D:2 * D] * up_s + sg[:, 2 * D:] * up_g


def merge_fwd(proj, ups, name):
    return rowwise(name, lambda g, a, b, c: (_merge(g, a, b, c),),
                   [(proj, 3 * D, OFF_G // (3 * D))] + [(u, D, 0) for u in ups], [], [(D, bf16)], tr=128)[0]


def merge_bwd(proj, ups, dmerged, name):
    def fn(g, a, b, c, dm):
        _, vjp = jax.vjp(_merge, g, a, b, c)
        return vjp(dm.astype(f32))

    return rowwise(name, fn, [(proj, 3 * D, OFF_G // (3 * D))] + [(u, D, 0) for u in ups] + [(dmerged, D, 0)], [],
                   [(3 * D, bf16), (D, bf16), (D, bf16), (D, bf16)], tr=128)


def _gdn_out(o, z, g):
    ys = []
    for h in range(NH):
        sl = slice(h * HD, (h + 1) * HD)
        ys.append(_rms(o[:, sl], g) * jax.nn.silu(z[:, sl]))
    return jnp.concatenate(ys, axis=1)


def gdn_out_fwd(o, proj, g, name):
    return rowwise(name, lambda ob, zb, gb: (_gdn_out(ob, zb, gb),), [(o, WH, 0), (proj, WH, OFF_Z // WH)], [g],
                   [(WH, bf16)])[0]


def gdn_out_bwd(o, proj, g, dy, name):
    def fn(ob, zb, dyb, gb):
        _, vjp = jax.vjp(_gdn_out, ob, zb, gb)
        return vjp(dyb.astype(f32))

    return rowwise(name, fn, [(o, WH, 0), (proj, WH, OFF_Z // WH), (dy, WH, 0)], [g], [(WH, f32), (WH, bf16)],
                   [(1, HD)])


def loss_head(x, target, g, name):
    def loss_fn(xb, gb, tb):
        err = _rms(xb, gb) - tb
        return (0.5 / D) * jnp.sum(jnp.sum(err * err, axis=1, keepdims=True), axis=0, keepdims=True)

    def fn(xb, tb, gb):
        val, vjp = jax.vjp(functools.partial(loss_fn, tb=tb), xb, gb)
        dx, dg = vjp(jnp.ones((1, 1), f32))
        return dx, dx, jnp.broadcast_to(val, (1, HD)), dg

    return rowwise(name, fn, [(x, D, 0), (target, D, 0)], [g], [(D, f32), (D, bf16)], [(1, HD), (1, D)])


PB = 256


def _split(v):
    hi = v.astype(bf16)
    return hi, (v - hi.astype(f32)).astype(bf16)


def _band_dot(make_band, v, s, forward):
    hi, lo = _split(v)
    nb = s // PB
    outs = []
    for r in range(nb):
        lo_r = max(r - 1, 0) if forward else r
        hi_r = r + 1 if forward else min(r + 2, nb)
        band = make_band(r * PB, lo_r * PB, (hi_r - lo_r) * PB)
        sl = slice(lo_r * PB, hi_r * PB)
        outs.append(jnp.dot(band, hi[sl], preferred_element_type=f32) + jnp.dot(band, lo[sl], preferred_element_type=f32))
    return jnp.concatenate(outs, axis=0)


def _pool_common(p, win, s):
    def band(row0, col0, ncol):
        t = row0 + lax.broadcasted_iota(jnp.int32, (PB, ncol), 0)
        u = col0 + lax.broadcasted_iota(jnp.int32, (PB, ncol), 1)
        return jnp.logical_and(u <= t, t < u + win).astype(bf16)

    def band_t(row0, col0, ncol):
        u = row0 + lax.broadcasted_iota(jnp.int32, (PB, ncol), 0)
        t = col0 + lax.broadcasted_iota(jnp.int32, (PB, ncol), 1)
        return jnp.logical_and(u <= t, t < u + win).astype(bf16)

    t = lax.broadcasted_iota(jnp.int32, (s, 1), 0)
    inv_n = 1.0 / jnp.minimum(t + 1, win).astype(f32)
    d = _band_dot(band, p, s, True) * inv_n - p
    return d, inv_n, band_t


def pool_fwd(proj, pool_w, pool_scale, name):
    s = proj.shape[0]

    def body(p_ref, w_ref, sc_ref, y_ref):
        win = jnp.left_shift(2, pl.program_id(0))
        d, _, _ = _pool_common(p_ref[...], win, s)
        y = jnp.dot(d.astype(bf16), w_ref[...].astype(bf16), preferred_element_type=f32) * sc_ref[...]
        y_ref[...] = y.astype(bf16)

    return pl.pallas_call(
        body, name=name, grid=(4,),
        in_specs=[pl.BlockSpec((s, HD), lambda g: (0, OFF_P // HD + g)), pl.BlockSpec((None, HD, HD), lambda g: (g, 0, 0)),
                  pl.BlockSpec((1, HD), lambda g: (0, g))],
        out_specs=pl.BlockSpec((s, HD), lambda g: (0, g)), out_shape=_sds((s, W_POOL), bf16),
        compiler_params=_cp(("arbitrary",)),
    )(proj, pool_w, pool_scale)


def pool_bwd(proj, pool_w, pool_scale, dy, name):
    s = proj.shape[0]

    def body(p_ref, w_ref, sc_ref, dy_ref, dp_ref, dw_ref, dsc_ref):
        win = jnp.left_shift(2, pl.program_id(0))
        d, inv_n, band_t = _pool_common(p_ref[...], win, s)
        w = w_ref[...].astype(bf16)
        dyf = dy_ref[...].astype(f32)
        dsc_ref[...] = jnp.sum(dyf * jnp.dot(d.astype(bf16), w, preferred_element_type=f32), axis=0, keepdims=True)
        dys = (dyf * sc_ref[...]).astype(bf16)
        dd = lax.dot_general(dys, w, (((1,), (1,)), ((), ())), preferred_element_type=f32)
        dw_ref[...] = lax.dot_general(d.astype(bf16), dys, (((0,), (0,)), ((), ())), preferred_element_type=f32)
        dp_ref[...] = (_band_dot(band_t, dd * inv_n, s, False) - dd).astype(bf16)

    return pl.pallas_call(
        body, name=name, grid=(4,),
        in_specs=[pl.BlockSpec((s, HD), lambda g: (0, OFF_P // HD + g)), pl.BlockSpec((None, HD, HD), lambda g: (g, 0, 0)),
                  pl.BlockSpec((1, HD), lambda g: (0, g)), pl.BlockSpec((s, HD), lambda g: (0, g))],
        out_specs=[pl.BlockSpec((s, HD), lambda g: (0, g)), pl.BlockSpec((None, HD, HD), lambda g: (g, 0, 0)),
                   pl.BlockSpec((1, HD), lambda g: (0, g))],
        out_shape=[_sds((s, W_POOL), bf16), _sds((4, HD, HD), f32), _sds((1, W_POOL), f32)],
        compiler_params=_cp(("arbitrary",)),
    )(proj, pool_w, pool_scale, dy)


HB = 3
WB = HB * HD


def _heads_of(v):
    return _stack([v[:, h * HD:(h + 1) * HD] for h in range(HB)])


def _bd(a, b, dn):
    return lax.dot_general(a, b, dn, preferred_element_type=f32)


def _run_sum(v, tri):
    hi, lo = _split(v.reshape(HB * TQ, TQ))
    return (jnp.dot(hi, tri, preferred_element_type=f32) + jnp.dot(lo, tri, preferred_element_type=f32)).reshape(HB, TQ, TQ)


def _sb_tile(q, k_ref, kb, qi, carry):
    k = _heads_of(k_ref[pl.ds(pl.multiple_of(kb * TQ, TQ), TQ), :].astype(bf16))
    z = _bd(q, k, BNT)
    row = qi * TQ + lax.broadcasted_iota(jnp.int32, (TQ, TQ), 0)
    col = kb * TQ + lax.broadcasted_iota(jnp.int32, (TQ, TQ), 1)
    mask = col < row
    ls = jnp.where(mask, jax.nn.log_sigmoid(-z), 0.0)
    j = lax.broadcasted_iota(jnp.int32, (TQ, TQ), 0)
    u = lax.broadcasted_iota(jnp.int32, (TQ, TQ), 1)
    lw = _run_sum(ls, (j > u).astype(bf16))
    a = jnp.where(mask, jnp.exp(ls + z + lw + carry), 0.0)
    return z, mask, ls, a


def sb_fwd(proj, name):
    s = proj.shape[0]
    nq = s // TQ
    scale = HD ** -0.5

    def body(q_ref, k_ref, v_ref, y_ref):
        qi = pl.program_id(1)
        q = _heads_of((q_ref[...] * scale).astype(bf16))

        def step(j, c):
            acc, carry = c
            kb = qi - j
            _, _, ls, a = _sb_tile(q, k_ref, kb, qi, carry)
            v = _heads_of(v_ref[pl.ds(pl.multiple_of(kb * TQ, TQ), TQ), :].astype(bf16))
            return acc + _bd(a.astype(bf16), v, BNN), carry + jnp.sum(ls, axis=2, keepdims=True)

        acc, _ = lax.fori_loop(0, qi + 1, step, (jnp.zeros((HB, TQ, HD), f32), jnp.zeros((HB, TQ, 1), f32)))
        for h in range(HB):
            y_ref[:, h * HD:(h + 1) * HD] = acc[h].astype(bf16)

    c0, ng = OFF_SB // WB, NH // HB
    return pl.pallas_call(
        body, name=name, grid=(ng, nq),
        in_specs=[pl.BlockSpec((TQ, WB), lambda g, i: (i, c0 + g)), pl.BlockSpec((s, WB), lambda g, i: (0, c0 + ng + g)),
                  pl.BlockSpec((s, WB), lambda g, i: (0, c0 + 2 * ng + g))],
        out_specs=pl.BlockSpec((TQ, WB), lambda g, i: (i, g)), out_shape=_sds((s, WH), bf16),
        compiler_params=_cp(("arbitrary", "arbitrary")),
    )(proj, proj, proj)


def sb_bwd(proj, dy, name):
    s = proj.shape[0]
    nq = s // TQ
    scale = HD ** -0.5

    def body(q_ref, k_ref, v_ref, do_ref, dq_ref, dk_ref, dv_ref, e_scr, z_scr, dk_acc, dv_acc):
        qi = pl.program_id(1)
        q = _heads_of((q_ref[...] * scale).astype(bf16))
        do = _heads_of(do_ref[...])

        @pl.when(qi == 0)
        def _():
            dk_acc[...] = jnp.zeros_like(dk_acc)
            dv_acc[...] = jnp.zeros_like(dv_acc)

        def add_heads(acc_ref, rows, upd):
            for h in range(HB):
                acc_ref[rows, h * HD:(h + 1) * HD] += upd[h]

        def sweep_back(j, carry):
            kb = qi - j
            rows = pl.ds(pl.multiple_of(kb * TQ, TQ), TQ)
            z, _, ls, a = _sb_tile(q, k_ref, kb, qi, carry)
            v = _heads_of(v_ref[rows, :].astype(bf16))
            e_scr[kb] = _bd(do, v, BNT) * a
            z_scr[kb] = z
            add_heads(dv_acc, rows, _bd(a.astype(bf16), do, BTN))
            return carry + jnp.sum(ls, axis=2, keepdims=True)

        lax.fori_loop(0, qi + 1, sweep_back, jnp.zeros((HB, TQ, 1), f32))

        def sweep_fwd(kb, c):
            dq, carry = c
            rows = pl.ds(pl.multiple_of(kb * TQ, TQ), TQ)
            e, z = e_scr[kb], z_scr[kb]
            row = qi * TQ + lax.broadcasted_iota(jnp.int32, (TQ, TQ), 0)
            col = kb * TQ + lax.broadcasted_iota(jnp.int32, (TQ, TQ), 1)
            j = lax.broadcasted_iota(jnp.int32, (TQ, TQ), 0)
            u = lax.broadcasted_iota(jnp.int32, (TQ, TQ), 1)
            ew = _run_sum(e, (j < u).astype(bf16))
            sig = jax.nn.sigmoid(z)
            dz = jnp.where(col < row, e * (1.0 - sig) - (ew + carry) * sig, 0.0).astype(bf16)
            k = _heads_of(k_ref[rows, :].astype(bf16))
            add_heads(dk_acc, rows, _bd(dz, q, BTN))
            return dq + _bd(dz, k, BNN), carry + jnp.sum(e, axis=2, keepdims=True)

        dq, _ = lax.fori_loop(0, qi + 1, sweep_fwd, (jnp.zeros((HB, TQ, HD), f32), jnp.zeros((HB, TQ, 1), f32)))
        for h in range(HB):
            dq_ref[:, h * HD:(h + 1) * HD] = (dq[h] * scale).astype(bf16)

        @pl.when(qi == nq - 1)
        def _():
            dk_ref[...] = dk_acc[...].astype(bf16)
            dv_ref[...] = dv_acc[...].astype(bf16)

    c0, ng = OFF_SB // WB, NH // HB
    return pl.pallas_call(
        body, name=name, grid=(ng, nq),
        in_specs=[pl.BlockSpec((TQ, WB), lambda g, i: (i, c0 + g)), pl.BlockSpec((s, WB), lambda g, i: (0, c0 + ng + g)),
                  pl.BlockSpec((s, WB), lambda g, i: (0, c0 + 2 * ng + g)), pl.BlockSpec((TQ, WB), lambda g, i: (i, g))],
        out_specs=[pl.BlockSpec((TQ, WB), lambda g, i: (i, g)), pl.BlockSpec((s, WB), lambda g, i: (0, g)),
                   pl.BlockSpec((s, WB), lambda g, i: (0, g))],
        out_shape=[_sds((s, WH), bf16)] * 3,
        scratch_shapes=[pltpu.VMEM((nq, HB, TQ, TQ), f32), pltpu.VMEM((nq, HB, TQ, TQ), f32), pltpu.VMEM((s, WB), f32),
                        pltpu.VMEM((s, WB), f32)],
        compiler_params=_cp(("arbitrary", "arbitrary")),
    )(proj, proj, proj, dy)


CB = 256


def _shift_down(v, k, s):
    if k == 0:
        return v
    row = lax.broadcasted_iota(jnp.int32, v.shape, 0)
    return jnp.where(row < k, 0.0, pltpu.roll(v, k, axis=0))


def _shift_up(v, k, s):
    if k == 0:
        return v
    row = lax.broadcasted_iota(jnp.int32, v.shape, 0)
    return jnp.where(row >= s - k, 0.0, pltpu.roll(v, s - k, axis=0))


def conv_fwd(proj, w, name):
    s = proj.shape[0]

    def body(x_ref, w_ref, y_ref):
        x, wv = x_ref[...], w_ref[...]
        y = sum(wv[3 - k:4 - k, :] * _shift_down(x, k, s) for k in range(4))
        y_ref[...] = jax.nn.silu(y)

    return pl.pallas_call(
        body, name=name, grid=(3 * WH // CB,),
        in_specs=[pl.BlockSpec((s, CB), lambda j: (0, OFF_GQ // CB + j)), pl.BlockSpec((4, CB), lambda j: (0, j))],
        out_specs=pl.BlockSpec((s, CB), lambda j: (0, j)), out_shape=_sds((s, 3 * WH), f32),
        compiler_params=_cp(("parallel",)),
    )(proj, w)


def conv_bwd(proj, w, dc, name, after=()):
    s = proj.shape[0]

    def body(*refs):
        x_ref, w_ref, dc_ref, dx_ref, dw_ref = refs[len(after):]
        x, wv = x_ref[...], w_ref[...]
        xs = [_shift_down(x, k, s) for k in range(4)]
        y = sum(wv[3 - k:4 - k, :] * xs[k] for k in range(4))
        sig = jax.nn.sigmoid(y)
        dy = dc_ref[...] * (sig * (1.0 + y * (1.0 - sig)))
        dx_ref[...] = sum(wv[3 - k:4 - k, :] * _shift_up(dy, k, s) for k in range(4)).astype(bf16)
        dw_ref[...] = jnp.concatenate([jnp.sum(dy * xs[3 - i], axis=0, keepdims=True) for i in range(4)], axis=0)

    return pl.pallas_call(
        body, name=name, grid=(3 * WH // CB,),
        in_specs=[pl.BlockSpec(memory_space=pl.ANY)] * len(after)
        + [pl.BlockSpec((s, CB), lambda j: (0, OFF_GQ // CB + j)), pl.BlockSpec((4, CB), lambda j: (0, j)),
           pl.BlockSpec((s, CB), lambda j: (0, j))],
        out_specs=[pl.BlockSpec((s, CB), lambda j: (0, j)), pl.BlockSpec((4, CB), lambda j: (0, j))],
        out_shape=[_sds((s, 3 * WH), bf16), _sds((4, 3 * WH), f32)],
        compiler_params=_cp(("parallel",)),
    )(*after, proj, w, dc)


SOLVE_PASSES = 3


def _pdot_impl(a, b, dn, passes):
    ah, al = _split(a)
    bh, bl = _split(b)
    dot = lambda p, q: lax.dot_general(p, q, dn, preferred_element_type=f32)
    if passes == 1:
        return dot(ah, bh)
    if passes == 2:
        return dot(ah, bh) + dot(ah, bl)
    return dot(ah, bh) + (dot(ah, bl) + dot(al, bh))


BNN, BNT, BTN = (((2,), (1,)), ((0,), (0,))), (((2,), (2,)), ((0,), (0,))), (((1,), (1,)), ((0,), (0,)))


@functools.partial(jax.custom_vjp, nondiff_argnums=(2,))
def _bdot(a, b, passes):
    return _pdot_impl(a, b, BNN, passes)


def _bdot_fwd(a, b, passes):
    return _pdot_impl(a, b, BNN, passes), (a, b)


def _bdot_bwd(passes, res, ct):
    a, b = res
    return _pdot_impl(ct, b, BNT, passes), _pdot_impl(a, ct, BTN, passes)


_bdot.defvjp(_bdot_fwd, _bdot_bwd)


@functools.partial(jax.custom_vjp, nondiff_argnums=(2,))
def _bdot_nt(a, b, passes):
    return _pdot_impl(a, b, BNT, passes)


def _bdot_nt_fwd(a, b, passes):
    return _pdot_impl(a, b, BNT, passes), (a, b)


def _bdot_nt_bwd(passes, res, ct):
    a, b = res
    return _pdot_impl(ct, b, BNN, passes), _pdot_impl(ct, a, BTN, passes)


_bdot_nt.defvjp(_bdot_nt_fwd, _bdot_nt_bwd)


def _lane_pick(v, h):
    lane = lax.broadcasted_iota(jnp.int32, v.shape, v.ndim - 1)
    return jnp.sum(jnp.where(lane == h, v, 0.0), axis=-1, keepdims=True)


def _stack(parts):
    return jnp.concatenate([p[None] for p in parts], axis=0)


def _heads(v, first):
    return _stack([_lane_pick(v, first + h) for h in range(NH)])


def _l2n(v):
    return v * lax.rsqrt(jnp.sum(v * v, axis=-1, keepdims=True) + EPS)


def _dot_nt(a, b):
    return lax.dot_general(a, b, (((1,), (1,)), ((), ())), preferred_element_type=f32)


def _gdn_chunk(cq, ck, cv, ab, alog, dtb):
    ones = jnp.ones((NH, CH, HD), f32)
    q = _l2n(cq) * (HD ** -0.5)
    k = _l2n(ck)
    la = -jnp.exp(_heads(alog, 0)) * jax.nn.softplus(_heads(ab, AB_LANE) + _heads(dtb, 0))
    beta = jax.nn.sigmoid(_heads(ab, AB_LANE + NH)) * ones
    i = lax.broadcasted_iota(jnp.int32, (CH, CH), 0)
    j = lax.broadcasted_iota(jnp.int32, (CH, CH), 1)
    incl, strict = j <= i, j < i
    eye = (i == j).astype(f32)
    g = _bdot(jnp.broadcast_to(incl.astype(f32), (NH, CH, CH)), la * ones, 2)
    g_row = _stack([g[h].T for h in range(NH)])
    gamma = jnp.where(incl, jnp.exp(jnp.where(incl, g - g_row, 0.0)), 0.0)
    lower = jnp.where(strict, beta * _bdot_nt(k, k, 1) * gamma, 0.0)
    inv = eye - lower
    pw = _bdot(lower, lower, SOLVE_PASSES)
    for m in range(1, int(math.log2(CH))):
        inv = inv + _bdot(inv, pw, SOLVE_PASSES)
        if m < int(math.log2(CH)) - 1:
            pw = _bdot(pw, pw, SOLVE_PASSES)
    eg = jnp.exp(g)
    u = _bdot(inv, cv * beta, SOLVE_PASSES)
    w = _bdot(inv, k * (beta * eg), SOLVE_PASSES)
    qk = _bdot_nt(q, k, 1) * gamma
    g_last = g[:, CH - 1:CH, :]
    return u, w, q * eg, k * jnp.exp(g_last - g), qk, jnp.exp(g_last)


def _by_head(ref, t=0):
    return _stack([ref[:, t * WH + h * HD:t * WH + (h + 1) * HD] for h in range(NH)])


def gdn_a_fwd(c, proj, alog, dtb, name):
    s = c.shape[0]
    nc = s // CH

    def body(c_ref, ab_ref, al_ref, dt_ref, u_ref, w_ref, qd_ref, kd_ref, qk_ref, dec_ref):
        res = _gdn_chunk(_by_head(c_ref, 0), _by_head(c_ref, 1), _by_head(c_ref, 2), ab_ref[...], al_ref[...], dt_ref[...])
        for h in range(NH):
            sl = slice(h * HD, (h + 1) * HD)
            for r, v in zip((u_ref, w_ref, qd_ref, kd_ref, qk_ref), res[:5]):
                r[:, sl] = v[h]
            dec_ref[:, sl] = jnp.broadcast_to(res[5][h], (8, HD))

    row = pl.BlockSpec((CH, WH), lambda n: (n, 0))
    par = pl.BlockSpec((1, HD), lambda n: (0, 0))
    return pl.pallas_call(
        body, name=name, grid=(nc,),
        in_specs=[pl.BlockSpec((CH, 3 * WH), lambda n: (n, 0)), pl.BlockSpec((CH, HD), lambda n: (n, OFF_AB // HD)), par, par],
        out_specs=[row] * 5 + [pl.BlockSpec((None, 8, WH), lambda n: (n, 0, 0))],
        out_shape=[_sds((s, WH), f32)] * 5 + [_sds((nc, 8, WH), f32)],
        compiler_params=_cp(("parallel",)),
    )(c, proj, alog, dtb)


def gdn_a_bwd(c, proj, alog, dtb, cots, ddec, name):
    s = c.shape[0]
    nc = s // CH

    def body(c_ref, ab_ref, al_ref, dt_ref, du_ref, dw_ref, dqd_ref, dkd_ref, dqk_ref, ddec_ref,
             dc_ref, dab_ref, dal_ref, ddt_ref):
        n = pl.program_id(0)
        _, vjp = jax.vjp(_gdn_chunk, _by_head(c_ref, 0), _by_head(c_ref, 1), _by_head(c_ref, 2), ab_ref[...], al_ref[...],
                         dt_ref[...])
        lane = lax.broadcasted_iota(jnp.int32, (1, HD), 1)
        dd = _stack([jnp.where(lane == 0, ddec_ref[0:1, h * HD:(h + 1) * HD], 0.0) for h in range(NH)])
        dcq, dck, dcv, dab, dal, ddt = vjp(tuple(_by_head(r) for r in (du_ref, dw_ref, dqd_ref, dkd_ref, dqk_ref)) + (dd,))
        for h in range(NH):
            for t, v in enumerate((dcq, dck, dcv)):
                dc_ref[:, t * WH + h * HD:t * WH + (h + 1) * HD] = v[h]
        dab_ref[...] = dab.astype(bf16)

        @pl.when(n == 0)
        def _():
            dal_ref[...] = dal
            ddt_ref[...] = ddt

        @pl.when(n > 0)
        def _():
            dal_ref[...] += dal
            ddt_ref[...] += ddt

    row = pl.BlockSpec((CH, WH), lambda n: (n, 0))
    wide = pl.BlockSpec((CH, 3 * WH), lambda n: (n, 0))
    par = pl.BlockSpec((1, HD), lambda n: (0, 0))
    return pl.pallas_call(
        body, name=name, grid=(nc,),
        in_specs=[wide, pl.BlockSpec((CH, HD), lambda n: (n, OFF_AB // HD)), par, par] + [row] * 5
        + [pl.BlockSpec((None, 8, WH), lambda n: (n, 0, 0))],
        out_specs=[wide, pl.BlockSpec((CH, HD), lambda n: (n, 0)), par, par],
        out_shape=[_sds((s, 3 * WH), f32), _sds((s, HD), bf16), _sds((1, HD), f32), _sds((1, HD), f32)],
        compiler_params=_cp(("arbitrary",)),
    )(c, proj, alog, dtb, *cots, ddec)


def gdn_b_fwd(u, w, qd, kd, qk, dec, name):
    s = u.shape[0]
    nc = s // CH

    def body(u_ref, w_ref, qd_ref, kd_ref, qk_ref, dec_ref, o_ref, st_ref, state):
        n = pl.program_id(0)

        @pl.when(n == 0)
        def _():
            state[...] = jnp.zeros_like(state)

        for h in range(NH):
            sl = slice(h * HD, (h + 1) * HD)
            st = state[sl, :]
            st_ref[sl, :] = st
            sb = st.astype(bf16)
            vn = u_ref[:, sl] - jnp.dot(w_ref[:, sl].astype(bf16), sb, preferred_element_type=f32)
            vb = vn.astype(bf16)
            o_ref[:, sl] = (jnp.dot(qd_ref[:, sl].astype(bf16), sb, preferred_element_type=f32)
                            + jnp.dot(qk_ref[:, sl].astype(bf16), vb, preferred_element_type=f32))
            state[sl, :] = st * dec_ref[0:1, sl] + lax.dot_general(
                kd_ref[:, sl].astype(bf16), vb, (((0,), (0,)), ((), ())), preferred_element_type=f32)

    row = pl.BlockSpec((CH, WH), lambda n: (n, 0))
    return pl.pallas_call(
        body, name=name, grid=(nc,),
        in_specs=[row] * 5 + [pl.BlockSpec((None, 8, WH), lambda n: (n, 0, 0))],
        out_specs=[row, pl.BlockSpec((None, WH, HD), lambda n: (n, 0, 0))],
        out_shape=[_sds((s, WH), f32), _sds((nc, WH, HD), f32)],
        scratch_shapes=[pltpu.VMEM((WH, HD), f32)],
        compiler_params=_cp(("arbitrary",)),
    )(u, w, qd, kd, qk, dec)


def gdn_b_bwd(u, w, qd, kd, qk, dec, states, do, name):
    s = u.shape[0]
    nc = s // CH

    def body(u_ref, w_ref, qd_ref, kd_ref, qk_ref, dec_ref, st_ref, do_ref,
             du_ref, dw_ref, dqd_ref, dkd_ref, dqk_ref, ddec_ref, dstate):
        n = pl.program_id(0)

        @pl.when(n == 0)
        def _():
            dstate[...] = jnp.zeros_like(dstate)

        for h in range(NH):
            sl = slice(h * HD, (h + 1) * HD)
            st, ds = st_ref[sl, :], dstate[sl, :]
            sb, dsb = st.astype(bf16), ds.astype(bf16)
            wb, qdb, kdb, qkb = (r[:, sl].astype(bf16) for r in (w_ref, qd_ref, kd_ref, qk_ref))
            dob = do_ref[:, sl].astype(bf16)
            vn = u_ref[:, sl] - jnp.dot(wb, sb, preferred_element_type=f32)
            vb = vn.astype(bf16)
            dvn = (lax.dot_general(qkb, dob, (((0,), (0,)), ((), ())), preferred_element_type=f32)
                   + jnp.dot(kdb, dsb, preferred_element_type=f32))
            dvb = dvn.astype(bf16)
            du_ref[:, sl] = dvn
            dw_ref[:, sl] = -_dot_nt(dvb, sb)
            dqd_ref[:, sl] = _dot_nt(dob, sb)
            dkd_ref[:, sl] = _dot_nt(vb, dsb)
            dqk_ref[:, sl] = _dot_nt(dob, vb)
            tot = jnp.sum(jnp.sum(ds * st, axis=1, keepdims=True), axis=0, keepdims=True)
            ddec_ref[:, sl] = jnp.broadcast_to(tot, (8, HD))
            dstate[sl, :] = (ds * dec_ref[0:1, sl]
                             + lax.dot_general(qdb, dob, (((0,), (0,)), ((), ())), preferred_element_type=f32)
                             - lax.dot_general(wb, dvb, (((0,), (0,)), ((), ())), preferred_element_type=f32))

    row = pl.BlockSpec((CH, WH), lambda n: (nc - 1 - n, 0))
    small = pl.BlockSpec((None, 8, WH), lambda n: (nc - 1 - n, 0, 0))
    return pl.pallas_call(
        body, name=name, grid=(nc,),
        in_specs=[row] * 5 + [small, pl.BlockSpec((None, WH, HD), lambda n: (nc - 1 - n, 0, 0)), row],
        out_specs=[row] * 5 + [small],
        out_shape=[_sds((s, WH), f32)] * 5 + [_sds((nc, 8, WH), f32)],
        scratch_shapes=[pltpu.VMEM((WH, HD), f32)],
        compiler_params=_cp(("arbitrary",)),
    )(u, w, qd, kd, qk, dec, states, do)


ANY = pl.BlockSpec(memory_space=pl.ANY)


def _dev_index(p):
    return 4 * p[0] + 2 * p[1] + p[2]


def _shard_of(ref, axis, size, idx):
    return ref.at[pl.ds(idx * size, size), :] if axis == 0 else ref.at[:, pl.ds(idx * size, size)]


def _peer(x, y, c, r):
    return (1 - x if r & 4 else x, 1 - y if r & 2 else y, 1 - c if r & 1 else c)


ALL_PEERS, SIBLING, OTHER_CHIPS = tuple(range(1, NDEV)), (1,), (4, 2, 6)


def _launch(body, ins, out_sds, name, sequencer_id, relations=ALL_PEERS, kinds=7):
    n = len(ins)
    sems = [pltpu.SemaphoreType.DMA((n, kinds)), pltpu.SemaphoreType.DMA((n, kinds)), pltpu.SemaphoreType.DMA((n,))]
    if sequencer_id is None:
        return pl.pallas_call(
            lambda *refs: body(refs[:n], refs[n:2 * n], *refs[2 * n:]), name=name, in_specs=[ANY] * n, out_specs=[ANY] * n,
            out_shape=out_sds, scratch_shapes=sems, compiler_params=pltpu.CompilerParams(has_side_effects=True),
        )(*ins)
    in_refs = [jax.new_ref(a, memory_space=pltpu.MemorySpace.HBM) for a in ins]
    out_refs = [jax.empty_ref(sd, memory_space=pltpu.MemorySpace.HBM) for sd in out_sds]

    @pl.kernel(mesh=plsc.ScalarSubcoreMesh(axis_name="sequencer", num_cores=1), name=name, scratch_types=sems,
               compiler_params=pltpu.CompilerParams(collective_id=sequencer_id))
    def launch(send, recv, loc):
        x, y, c = lax.axis_index("x"), lax.axis_index("y"), lax.axis_index("c")
        barrier = pltpu.get_barrier_semaphore()
        for r in relations:
            pl.semaphore_signal(barrier, inc=1, device_id=_peer(x, y, c, r), device_id_type=MESH)
        pl.semaphore_wait(barrier, len(relations))
        body(in_refs, out_refs, send, recv, loc)

    launch()
    return [r[...] for r in out_refs]


def all_gather(xs, axes, name, sequencer_id=None):
    n = len(xs)
    fulls = [tuple(d * (NDEV if a == ax else 1) for a, d in enumerate(x.shape)) for x, ax in zip(xs, axes)]
    halved = [x.shape[0] % 32 == 0 for x in xs]

    def body(x_refs, o_refs, send, recv, loc):
        x, y, c = lax.axis_index("x"), lax.axis_index("y"), lax.axis_index("c")
        me, sib = (x, y, c), (x, y, 1 - c)
        xn, yn, dg = (1 - x, y), (x, 1 - y), (1 - x, 1 - y)

        def part(t, p, half=None):
            ref = _shard_of(o_refs[t], axes[t], xs[t].shape[axes[t]], _dev_index(p))
            rows = xs[t].shape[0] // 2
            return ref if half is None else ref.at[pl.ds(half * rows, rows), :]

        def copy(t, k, block, to, half=None, src=None):
            return pltpu.make_async_remote_copy(
                src_ref=part(t, block, half) if src is None else src, dst_ref=part(t, block, half),
                send_sem=send.at[t, k], recv_sem=recv.at[t, k], device_id=to, device_id_type=MESH)

        mine = [pltpu.make_async_copy(x_refs[t], part(t, me), loc.at[t]) for t in range(n)]
        for cp in mine:
            cp.start()
        sends = []
        for t in range(n):
            sends += [copy(t, 0, me, sib, src=x_refs[t]), copy(t, 1, me, (*xn, c), src=x_refs[t]),
                      copy(t, 2, me, (*yn, c), src=x_refs[t])]
            if not halved[t]:
                sends.append(copy(t, 3, me, (*dg, c), src=x_refs[t]))
        for cp in sends:
            cp.start()

        def pass_on(cp):
            cp.start()
            sends.append(cp)

        for t in range(n):
            h0, h1 = (0, 1) if halved[t] else (None, None)
            copy(t, 1, (*xn, c), me).wait_recv()
            if halved[t]:
                pass_on(copy(t, 3, (*xn, c), (*yn, c), 0))
            pass_on(copy(t, 5, (*xn, c), sib))
            copy(t, 2, (*yn, c), me).wait_recv()
            if halved[t]:
                pass_on(copy(t, 4, (*yn, c), (*xn, c), 1))
            pass_on(copy(t, 6, (*yn, c), sib))
            copy(t, 3, (*dg, c), me, h0).wait_recv()
            pass_on(copy(t, 7, (*dg, c), sib, h0))
            if halved[t]:
                copy(t, 4, (*dg, c), me, h1).wait_recv()
                pass_on(copy(t, 8, (*dg, c), sib, h1))
        for t in range(n):
            h0, h1 = (0, 1) if halved[t] else (None, None)
            copy(t, 0, sib, me).wait_recv()
            copy(t, 5, (*xn, 1 - c), me).wait_recv()
            copy(t, 6, (*yn, 1 - c), me).wait_recv()
            copy(t, 7, (*dg, 1 - c), me, h0).wait_recv()
            if halved[t]:
                copy(t, 8, (*dg, 1 - c), me, h1).wait_recv()
        for cp in sends:
            cp.wait_send()
        for cp in mine:
            cp.wait()

    return _launch(body, xs, [_sds(f, x.dtype) for f, x in zip(fulls, xs)], name, sequencer_id, kinds=9)


def pair_swap(gs, axes, name, sequencer_id):
    n = len(gs)
    shards = [tuple(d // (NDEV if a == ax else 1) for a, d in enumerate(g.shape)) for g, ax in zip(gs, axes)]

    def body(g_refs, p_refs, send, recv, loc):
        x, y, c = lax.axis_index("x"), lax.axis_index("y"), lax.axis_index("c")

        def copy(t, j):
            return pltpu.make_async_remote_copy(
                src_ref=_shard_of(g_refs[t], axes[t], shards[t][axes[t]], 2 * j + (1 - c)), dst_ref=p_refs[t].at[j],
                send_sem=send.at[t, j], recv_sem=recv.at[t, j], device_id=(x, y, 1 - c), device_id_type=MESH)

        copies = [copy(t, j) for t in range(n) for j in range(4)]
        for cp in copies:
            cp.start()
        for cp in copies:
            cp.wait_recv()
        for cp in copies:
            cp.wait_send()

    return _launch(body, gs, [_sds((4,) + sh, g.dtype) for sh, g in zip(shards, gs)], name, sequencer_id, SIBLING)


def pair_add(g, p, axis, name, after=()):
    _, rows, cols = p.shape
    tr = _row_tile(rows, cols, 1 << 20)
    nb = rows // tr
    if axis == 0:
        g_spec = pl.BlockSpec((tr, cols), lambda j, i, c: ((2 * j + c[0]) * nb + i, 0))
    else:
        g_spec = pl.BlockSpec((tr, cols), lambda j, i, c: (i, 2 * j + c[0]))
    blk = pl.BlockSpec((None, tr, cols), lambda j, i, c: (j, i, 0))

    na = len(after)

    def body(c_ref, *refs):
        g_ref, p_ref, q_ref = refs[na:]
        q_ref[...] = (g_ref[...].astype(f32) + p_ref[...].astype(f32)).astype(bf16)

    return pl.pallas_call(
        body, name=name, out_shape=_sds(p.shape, bf16),
        grid_spec=pltpu.PrefetchScalarGridSpec(num_scalar_prefetch=1, grid=(4, nb), in_specs=[ANY] * na + [g_spec, blk],
                                               out_specs=blk),
        compiler_params=_cp(("parallel", "parallel")),
    )(lax.axis_index("c").astype(jnp.int32).reshape(1), *after, g, p)


def chip_exchange(qs, name, sequencer_id):
    n = len(qs)

    def body(q_refs, r_refs, send, recv, loc):
        x, y, c = lax.axis_index("x"), lax.axis_index("y"), lax.axis_index("c")
        my_chip = 2 * x + y
        peers = [_peer(x, y, c, r) for r in OTHER_CHIPS]
        mine = [pltpu.make_async_copy(q_refs[t].at[my_chip], r_refs[t].at[my_chip], loc.at[t]) for t in range(n)]
        for cp in mine:
            cp.start()

        def copy(t, k, slot):
            p = peers[k]
            return pltpu.make_async_remote_copy(
                src_ref=q_refs[t].at[2 * p[0] + p[1]], dst_ref=r_refs[t].at[slot], send_sem=send.at[t, k], recv_sem=recv.at[t, k],
                device_id=p, device_id_type=MESH)

        sends = [copy(t, k, my_chip) for t in range(n) for k in range(3)]
        for cp in sends:
            cp.start()
        for t in range(n):
            for k in range(3):
                copy(t, k, 2 * peers[k][0] + peers[k][1]).wait_recv()
        for cp in sends:
            cp.wait_send()
        for cp in mine:
            cp.wait()

    return _launch(body, qs, [_sds(q.shape, q.dtype) for q in qs], name, sequencer_id, OTHER_CHIPS)


def _adamw(w, g, m, v):
    m = ADAM_B1 * m + (1.0 - ADAM_B1) * g
    v = ADAM_B2 * v + (1.0 - ADAM_B2) * jnp.square(g)
    m_hat = m / (1.0 - ADAM_B1 ** ADAM_STEP)
    v_hat = v / (1.0 - ADAM_B2 ** ADAM_STEP)
    return -ADAM_LR * (m_hat / (jnp.sqrt(v_hat) + ADAM_EPS) + ADAM_WD * w), m, v


def _sum8(r_ref):
    g = r_ref[0].astype(f32)
    for j in range(1, r_ref.shape[0]):
        g = g + r_ref[j].astype(f32)
    return g


def _row_tile(rows, cols, elems=1 << 18):
    tr = min(rows, max(8, 1 << int(math.log2(elems / cols))))
    assert rows % tr == 0, (rows, cols)
    return tr


def sum_partials(r, name, after=()):
    _, rows, cols = r.shape
    tr = _row_tile(rows, cols)
    na = len(after)

    def body(*refs):
        refs[na + 1][...] = _sum8(refs[na])

    return pl.pallas_call(
        body, name=name, grid=(rows // tr,),
        in_specs=[ANY] * na + [pl.BlockSpec((r.shape[0], tr, cols), lambda i: (0, i, 0))],
        out_specs=pl.BlockSpec((tr, cols), lambda i: (i, 0)), out_shape=_sds((rows, cols), f32),
        compiler_params=_cp(("parallel",)),
    )(*after, r)


def adamw_t(w, m, v, grads, name):
    rows, nl, cols = w.shape
    tr = min(rows, (1 << 16) // cols)
    blk = pl.BlockSpec((tr, nl, cols), lambda i: (i, 0, 0))
    flat = pl.BlockSpec((tr, cols), lambda i: (i, 0))

    def body(w_ref, m_ref, v_ref, *rest):
        g_refs, (g_ref, d_ref, nm_ref, nv_ref) = rest[:nl], rest[nl:]
        for l in range(nl):
            grad = g_refs[l][...]
            g_ref[:, l, :] = grad
            d_ref[:, l, :], nm_ref[:, l, :], nv_ref[:, l, :] = _adamw(w_ref[:, l, :], grad, m_ref[:, l, :], v_ref[:, l, :])

    return tuple(pl.pallas_call(
        body, name=name, grid=(pl.cdiv(rows, tr),), in_specs=[blk] * 3 + [flat] * nl, out_specs=[blk] * 4,
        out_shape=[_sds(w.shape, f32)] * 4, compiler_params=_cp(("parallel",)),
    )(w, m, v, *grads))


def adamw(w, m, v, layer, name, r=None, g=None, prev=None):
    _, rows, cols = w.shape
    tr = _row_tile(rows, cols)
    blk = pl.BlockSpec((None, tr, cols), lambda i: (layer, i, 0))
    nprev = 0 if prev is None else 4

    def body(w_ref, m_ref, v_ref, src_ref, *rest):
        g_ref, d_ref, nm_ref, nv_ref, token_ref = rest[nprev:]
        grad = _sum8(src_ref) if g is None else src_ref[...]
        g_ref[...] = grad
        d_ref[...], nm_ref[...], nv_ref[...] = _adamw(w_ref[...], grad, m_ref[...], v_ref[...])
        token_ref[...] = jnp.zeros_like(token_ref)

    src, src_spec = (r, pl.BlockSpec((r.shape[0], tr, cols), lambda i: (0, i, 0))) if g is None else (g, pl.BlockSpec((tr, cols), lambda i: (i, 0)))
    *outs, token = pl.pallas_call(
        body, name=name, grid=(rows // tr,), in_specs=[blk] * 3 + [src_spec] + [ANY] * nprev,
        out_specs=[blk] * 4 + [pl.BlockSpec((8, HD), lambda i: (0, 0))],
        out_shape=[_sds(w.shape, f32)] * 4 + [_sds((8, HD), f32)], input_output_aliases={4 + k: k for k in range(nprev)},
        compiler_params=_cp(("arbitrary",)),
    )(w, m, v, src, *(prev or ()))
    return tuple(outs), token


def small_adamw(parts, w, m, v, name, after=()):
    def body(*refs):
        p_ref, w_ref, m_ref, v_ref, g_ref, d_ref, nm_ref, nv_ref = refs[len(after):]
        g = _sum8(p_ref)
        g_ref[...] = g
        d_ref[...], nm_ref[...], nv_ref[...] = _adamw(w_ref[...], g, m_ref[...], v_ref[...])

    vmem = pl.BlockSpec(memory_space=pltpu.VMEM)
    return pl.pallas_call(body, name=name, out_shape=[_sds(w.shape, f32)] * 4, in_specs=[ANY] * len(after) + [vmem] * 4,
                          out_specs=[vmem] * 4, compiler_params=_cp())(*after, parts, w, m, v)


def _pack_rows(wt):
    tail = jnp.pad(wt[5888:N_IN], ((OFF_G - 12 - (OFF_P + 512), 0), (0, 0)))
    return jnp.concatenate([wt[512:2816], wt[2816:5120], wt[5120:5888], wt[0:512], tail], axis=0)


def _unpack_rows(g):
    return jnp.concatenate([g[OFF_P:OFF_P + 512], g[OFF_SB:OFF_SB + 2304], g[OFF_GQ:OFF_GQ + 2304], g[OFF_Z:OFF_Z + 768],
                            g[OFF_G - 12:NP]], axis=0)


def _lanes(v):
    flat = v.reshape(-1)
    n = -(-flat.shape[0] // HD) * HD
    return jnp.pad(flat, (0, n - flat.shape[0])).reshape(n // HD, HD)


def _lanes8(v):
    rows = _lanes(v)
    return jnp.pad(rows, ((0, -rows.shape[0] % 8), (0, 0)))


def _layer_fwd(x, p, l):
    nm = lambda s: f"{s}_l{l}"
    u = rms_fwd(x, p["attn_norm"], nm("rms1"))
    proj = matmul(u, p["w_in"], name=nm("inproj"), tb=True)
    y_pool = pool_fwd(proj, p["pool_w"], p["pool_scale"], nm("pool"))
    y_sb = sb_fwd(proj, nm("sb"))
    c = conv_fwd(proj, p["conv"], nm("conv"))
    ga = gdn_a_fwd(c, proj, p["alog"], p["dtb"], nm("gdna"))
    o, states = gdn_b_fwd(*ga, nm("gdnb"))
    y_gdn = gdn_out_fwd(o, proj, p["gdn_norm"], nm("gdno"))
    ups = [matmul(y, p[k], name=nm(k)) for y, k in ((y_pool, "w_pool_up"), (y_sb, "w_sb_up"), (y_gdn, "w_gdn_up"))]
    merged = merge_fwd(proj, ups, nm("merge"))
    x1 = matmul(merged, p["w_out"], name=nm("outproj"), epilogue=lambda acc, r: acc + r, extras=(x,))
    u2 = rms_fwd(x1, p["mlp_norm"], nm("rms2"))
    h2 = matmul(u2, p["w_ff1"], name=nm("ff1"), out_dtype=bf16, epilogue=lambda acc: jnp.square(jnp.maximum(acc, 0.0)))
    x2 = matmul(h2, p["w_ff2"], name=nm("ff2"), epilogue=lambda acc, r: acc + r, extras=(x1,))
    saved = dict(x=x, u=u, proj=proj, y_pool=y_pool, y_sb=y_sb, c=c, ga=ga, o=o, states=states, y_gdn=y_gdn, ups=ups,
                 merged=merged, x1=x1, u2=u2, h2=h2)
    return x2, saved


def _layer_bwd(dx2, dx2b, sv, p, l, swap, finish):
    nm = lambda s: f"{s}_l{l}"
    s = dx2.shape[0]
    dh = matmul(dx2b, p["w_ff2"], name=nm("d_ff2_x"), tb=True, out_dtype=bf16,
                epilogue=lambda acc, h2: acc * (2.0 * jnp.sqrt(h2.astype(f32))), extras=(sv["h2"],))
    g_ff2 = matmul(sv["h2"], dx2b, name=nm("d_ff2_w"), ta=True, out_dtype=bf16)
    du2 = matmul(dh, p["w_ff1"], name=nm("d_ff1_x"), tb=True)
    g_ff1 = matmul(sv["u2"], dh, name=nm("d_ff1_w"), ta=True, out_dtype=bf16)
    dx1, dx1b, d_mlp_norm = rms_bwd(sv["x1"], du2, dx2, p["mlp_norm"], nm("d_rms2"))
    dmerged = matmul(dx1b, p["w_out"], name=nm("d_out_x"), tb=True, out_dtype=bf16)
    g_out = matmul(sv["merged"], dx1b, name=nm("d_out_w"), ta=True, out_dtype=bf16)
    dgates, *dups = merge_bwd(sv["proj"], sv["ups"], dmerged, nm("d_merge"))
    dys, g_ups = [], []
    for dup, y, k in zip(dups, (sv["y_pool"], sv["y_sb"], sv["y_gdn"]), ("w_pool_up", "w_sb_up", "w_gdn_up")):
        dys.append(matmul(dup, p[k], name=nm("d_" + k + "_x"), tb=True, out_dtype=bf16))
        g_ups.append(matmul(y, dup, name=nm("d_" + k + "_w"), ta=True, out_dtype=bf16))
    early = g_ups + [g_out, g_ff1, g_ff2]
    swapped = swap(early, BIG_AXES[1:], "a")
    do, dz, d_gdn_norm = gdn_out_bwd(sv["o"], sv["proj"], p["gdn_norm"], dys[2], nm("d_gdno"))
    cots = gdn_b_bwd(*sv["ga"], sv["states"], do, nm("d_gdnb"))
    dc, dab, d_alog, d_dtb = gdn_a_bwd(sv["c"], sv["proj"], p["alog"], p["dtb"], cots[:5], cots[5], nm("d_gdna"))
    r_early, sent_early = finish(early, swapped, BIG_AXES[1:], BIG_NAMES[1:], "a", (dab,))
    dgq, d_conv = conv_bwd(sv["proj"], p["conv"], dc, nm("d_conv"), sent_early)
    dq, dk, dv = sb_bwd(sv["proj"], dys[1], nm("d_sb"))
    dp, d_pool_w, d_pool_scale = pool_bwd(sv["proj"], p["pool_w"], p["pool_scale"], dys[0], nm("d_pool"))
    dproj = jnp.concatenate([dq, dk, dv, dgq, dz, dp, jnp.zeros((s, OFF_AB - OFF_P - W_POOL), bf16), dab, dgates], axis=1)
    g_in = matmul(dproj, sv["u"], name=nm("d_in_w"), ta=True, out_dtype=bf16)
    r_in, sent_in = finish([g_in], swap([g_in], BIG_AXES[:1], "b"), BIG_AXES[:1], BIG_NAMES[:1], "b", ())
    du = matmul(dproj, p["w_in"], name=nm("d_in_x"), after=sent_in)
    dx, dxb, d_attn_norm = rms_bwd(sv["x"], du, dx1, p["attn_norm"], nm("d_rms1"))
    recv = r_in + r_early
    small = [d_attn_norm, d_pool_w, d_pool_scale, d_conv, d_alog, d_dtb, d_gdn_norm, d_mlp_norm]
    return dx, dxb, recv, small


BIG_AXES = (1, 1, 1, 1, 0, 1, 0)
GATHER_ID, EXCHANGE_ID = 1, 2
BIG_NAMES = ("w_in", "w_pool_up", "w_sb_up", "w_gdn_up", "w_out", "w_ff1", "w_ff2")


def kernel(x, attn_norm, w_in, pool_w, pool_scale, gdn_conv, gdn_a_log, gdn_dt_bias, gdn_norm, w_pool_up, w_sb_up, w_gdn_up, w_out, mlp_norm, w_ff1, w_ff2, final_norm, loss_target, m_attn_norm, m_w_in, m_pool_w, m_pool_scale, m_gdn_conv, m_gdn_a_log, m_gdn_dt_bias, m_gdn_norm, m_w_pool_up, m_w_sb_up, m_w_gdn_up, m_w_out, m_mlp_norm, m_w_ff1, m_w_ff2, m_final_norm, v_attn_norm, v_w_in, v_pool_w, v_pool_scale, v_gdn_conv, v_gdn_a_log, v_gdn_dt_bias, v_gdn_norm, v_w_pool_up, v_w_sb_up, v_w_gdn_up, v_w_out, v_mlp_norm, v_w_ff1, v_w_ff2, v_final_norm):
    s = x.shape[1]
    me = _dev_index((lax.axis_index("x"), lax.axis_index("y"), lax.axis_index("c")))
    ncv = gdn_conv.shape[2]

    w_in_t, m_w_in_t, v_w_in_t = (jnp.transpose(a, (2, 0, 1)) for a in (w_in, m_w_in, v_w_in))
    full = []
    for l in range(NL):
        shards = [_pack_rows(w_in_t[:, l]).astype(bf16), w_pool_up[l].astype(bf16), w_sb_up[l].astype(bf16),
                  w_gdn_up[l].astype(bf16), w_out[l].astype(bf16), w_ff1[l].astype(bf16), w_ff2[l].astype(bf16)]
        if l == 0:
            first = all_gather([shards[0], gdn_conv.reshape(NL * 4, ncv)], (1, 0), "gather_first", sequencer_id=GATHER_ID)
            rest = all_gather(shards[1:], BIG_AXES[1:], "gather_rest_l0", sequencer_id=GATHER_ID)
            full.append([first[0]] + rest)
            conv_full = first[1].reshape(NDEV, NL, 4, ncv).transpose(1, 2, 0, 3).reshape(NL, 4, NDEV * ncv)
        else:
            first = all_gather(shards[:1], BIG_AXES[:1], f"gather_in_l{l}", sequencer_id=GATHER_ID)
            full.append(first + all_gather(shards[1:], BIG_AXES[1:], f"gather_rest_l{l}", sequencer_id=GATHER_ID))
    params = []
    for l in range(NL):
        p = dict(zip(("w_in", "w_pool_up", "w_sb_up", "w_gdn_up", "w_out", "w_ff1", "w_ff2"), full[l][:7]))
        p.update(attn_norm=attn_norm[l][None], mlp_norm=mlp_norm[l][None], pool_w=pool_w[l], pool_scale=pool_scale[l][None],
                 conv=conv_full[l], alog=_lanes(gdn_a_log[l]), dtb=_lanes(gdn_dt_bias[l]), gdn_norm=gdn_norm[l][None])
        params.append(p)

    h = x[0]
    saved = []
    for l in range(NL):
        h, sv = _layer_fwd(h, params[l], l)
        saved.append(sv)
    dh, dhb, loss_row, d_final = loss_head(h, loss_target[0], final_norm[None], "loss_head")
    recv, smalls = [None] * NL, [None] * NL
    for l in reversed(range(NL)):
        def swap(gs, axes, tag, l=l):
            return pair_swap(gs, axes, f"swap_{tag}_l{l}", None)

        def finish(gs, ps, axes, names, tag, after, l=l):
            qs = [pair_add(g, p, ax, f"pair_add_{k}_l{l}", after) for g, p, ax, k in zip(gs, ps, axes, names)]
            return chip_exchange(qs, f"exchange_{tag}_l{l}", EXCHANGE_ID), tuple(qs)

        dh, dhb, recv[l], smalls[l] = _layer_bwd(dh, dhb, saved[l], params[l], l, swap, finish)

    small_rows = [_lanes8(t) for l in range(NL) for t in smalls[l]] + [_lanes8(d_final), _lanes8(loss_row)]
    packed = jnp.concatenate(small_rows, axis=0)
    parts = all_gather([packed], (0,), "gather_small", sequencer_id=GATHER_ID)[0].reshape(NDEV, packed.shape[0], HD)

    def pack_small(tree):
        rows = []
        for l in range(NL):
            rows += [_lanes8(tree["attn_norm"][l]), _lanes8(tree["pool_w"][l]), _lanes8(tree["pool_scale"][l]),
                     jnp.zeros((4 * NDEV * ncv // HD, HD), f32), _lanes8(tree["gdn_a_log"][l]), _lanes8(tree["gdn_dt_bias"][l]),
                     _lanes8(tree["gdn_norm"][l]), _lanes8(tree["mlp_norm"][l])]
        rows += [_lanes8(tree["final_norm"]), jnp.zeros((8, HD), f32)]
        return jnp.concatenate(rows, axis=0)

    names = ("attn_norm", "pool_w", "pool_scale", "gdn_a_log", "gdn_dt_bias", "gdn_norm", "mlp_norm", "final_norm")
    w_small = pack_small(dict(zip(names, (attn_norm, pool_w, pool_scale, gdn_a_log, gdn_dt_bias, gdn_norm, mlp_norm, final_norm))))
    m_small = pack_small(dict(zip(names, (m_attn_norm, m_pool_w, m_pool_scale, m_gdn_a_log, m_gdn_dt_bias, m_gdn_norm, m_mlp_norm, m_final_norm))))
    v_small = pack_small(dict(zip(names, (v_attn_norm, v_pool_w, v_pool_scale, v_gdn_a_log, v_gdn_dt_bias, v_gdn_norm, v_mlp_norm, v_final_norm))))
    def unpack_small(buf):
        out, conv_g, r = {}, [], 0
        layer_items = (("attn_norm", (D,)), ("pool_w", (4, HD, HD)), ("pool_scale", (W_POOL,)), ("conv", (4, NDEV * ncv)),
                       ("gdn_a_log", (NH,)), ("gdn_dt_bias", (NH,)), ("gdn_norm", (HD,)), ("mlp_norm", (D,)))
        per_layer = {k: [] for k, _ in layer_items}
        for l in range(NL):
            for k, shape in layer_items:
                size = math.prod(shape)
                nrow = -(-size // (8 * HD)) * 8
                per_layer[k].append(buf[r:r + nrow].reshape(-1)[:size].reshape(shape))
                r += nrow
        for k, _ in layer_items:
            out[k] = jnp.stack(per_layer[k])
        out["final_norm"] = buf[r:r + D // HD].reshape(D)
        out["loss"] = buf[r + D // HD, 0]
        return out

    big_out = {}
    big_names = BIG_NAMES
    big_w = dict(zip(big_names, ((w_in_t, m_w_in_t, v_w_in_t), (w_pool_up, m_w_pool_up, v_w_pool_up), (w_sb_up, m_w_sb_up, v_w_sb_up),
                                 (w_gdn_up, m_w_gdn_up, v_w_gdn_up), (w_out, m_w_out, v_w_out), (w_ff1, m_w_ff1, v_w_ff1),
                                 (w_ff2, m_w_ff2, v_w_ff2))))
    tokens = []
    for l, k in [(l, k) for l in reversed(range(NL)) for k in big_names[1:]]:
        w, m, v = big_w[k]
        big_out[k], token = adamw(w, m, v, l, f"adamw_{k}_l{l}", r=recv[l][big_names.index(k)], prev=big_out.get(k))
        tokens.append(token)
    g_in = [_unpack_rows(sum_partials(recv[l][0], f"sum_w_in_l{l}", tuple(tokens) if l == 0 else ())) for l in range(NL)]
    big_out["w_in"] = adamw_t(*big_w["w_in"], g_in, "adamw_w_in")

    small_out = small_adamw(parts, w_small, m_small, v_small, "adamw_small", after=(big_out["w_in"][1],))
    sm = [unpack_small(b) for b in small_out]
    loss = sm[0]["loss"]
    g_conv = lax.dynamic_slice_in_dim(sm[0]["conv"], me * ncv, ncv, axis=2)
    conv_out = None
    for l in reversed(range(NL)):
        conv_out, _ = adamw(gdn_conv, m_gdn_conv, v_gdn_conv, l, f"adamw_conv_l{l}", g=g_conv[l], prev=conv_out)

    def leaf(i, k):
        if k == "w_in":
            return jnp.transpose(big_out[k][i], (1, 2, 0))
        if k in big_out:
            return big_out[k][i]
        if k == "gdn_conv":
            return conv_out[i]
        return sm[i][k]

    order = ("attn_norm", "w_in", "pool_w", "pool_scale", "gdn_conv", "gdn_a_log", "gdn_dt_bias", "gdn_norm", "w_pool_up",
             "w_sb_up", "w_gdn_up", "w_out", "mlp_norm", "w_ff1", "w_ff2", "final_norm")
    return (loss, dh[None]) + tuple(leaf(i, k) for i in range(4) for k in order)
```

```python
import functools
import math

import jax
import jax.numpy as jnp
from jax import lax
from jax.experimental import pallas as pl
from jax.experimental.pallas import tpu as pltpu
from jax.experimental.pallas import tpu_sc as plsc

f32, bf16 = jnp.float32, jnp.bfloat16

D = 2048
NDEV = 8
NL = 2
HD = 128
NH = 6
WH = NH * HD
W_POOL = 512
EPS = 1e-6
N_IN = 12044
NP = 12288
OFF_SB, OFF_GQ, OFF_Z, OFF_P, OFF_AB, OFF_G = 0, 2304, 4608, 5376, 6016, 6144
AB_LANE = HD - 2 * NH
POOL_WINDOWS = (2, 4, 8, 16)
CH = 128
TQ = 256
VMEM_LIMIT = 56 * 1024 * 1024
ADAM_LR, ADAM_B1, ADAM_B2, ADAM_EPS, ADAM_WD, ADAM_STEP = 0.001, 0.9, 0.999, 1e-08, 0.01, 10
MESH = pl.DeviceIdType.MESH


def _cp(sem=None):
    return pltpu.CompilerParams(dimension_semantics=sem, vmem_limit_bytes=VMEM_LIMIT)


def _sds(shape, dtype):
    return jax.ShapeDtypeStruct(tuple(shape), dtype)


def matmul(a, b, *, name, ta=False, tb=False, out_dtype=f32, tm=1024, tn=1024, tk=2048, epilogue=None, extras=(), after=()):
    m, k = (a.shape[1], a.shape[0]) if ta else a.shape
    n = b.shape[0] if tb else b.shape[1]
    assert k == (b.shape[1] if tb else b.shape[0]) and a.dtype == bf16 and b.dtype == bf16
    tm, tn, tk = min(tm, m), min(tn, n), min(tk, k)
    assert m % tm == 0 and n % tn == 0 and k % tk == 0, (m, n, k, tm, tn, tk)
    nk = k // tk
    a_spec = pl.BlockSpec((tk, tm), lambda i, j, q: (q, i)) if ta else pl.BlockSpec((tm, tk), lambda i, j, q: (i, q))
    b_spec = pl.BlockSpec((tn, tk), lambda i, j, q: (j, q)) if tb else pl.BlockSpec((tk, tn), lambda i, j, q: (q, j))
    e_specs = [pl.BlockSpec((tm, tn), lambda i, j, q: (i, j)) for _ in extras]
    dn = (((0 if ta else 1,), (1 if tb else 0,)), ((), ()))
    ne = len(extras)

    def body(*refs):
        a_ref, b_ref, *rest = refs[len(after):]
        e_refs, o_ref = rest[:ne], rest[ne]
        part = lax.dot_general(a_ref[...], b_ref[...], dn, preferred_element_type=f32)

        def finish(acc):
            if epilogue is not None:
                acc = epilogue(acc, *[e[...] for e in e_refs])
            o_ref[...] = acc.astype(out_dtype)

        if nk == 1:
            finish(part)
        else:
            acc_ref = rest[ne + 1]
            q = pl.program_id(2)

            @pl.when(q == 0)
            def _():
                acc_ref[...] = part

            @pl.when(jnp.logical_and(q > 0, q < nk - 1))
            def _():
                acc_ref[...] += part

            @pl.when(q == nk - 1)
            def _():
                finish(acc_ref[...] + part)

    return pl.pallas_call(
        body, name=name, grid=(m // tm, n // tn, nk),
        in_specs=[pl.BlockSpec(memory_space=pl.ANY)] * len(after) + [a_spec, b_spec] + e_specs,
        out_specs=pl.BlockSpec((tm, tn), lambda i, j, q: (i, j)), out_shape=_sds((m, n), out_dtype),
        scratch_shapes=[pltpu.VMEM((tm, tn), f32)] if nk > 1 else [],
        compiler_params=_cp(("parallel", "parallel", "arbitrary")),
    )(*after, a, b, *extras)


def rowwise(name, fn, rows, params, outs, sums=(), tr=256, after=()):
    s = rows[0][0].shape[0]
    tr = min(tr, s)
    nin, nout = len(rows) + len(params), len(outs)
    in_specs = [pl.BlockSpec((tr, w), functools.partial(lambda i, c: (i, c), c=c)) for (_, w, c) in rows]
    in_specs += [pl.BlockSpec(p.shape, lambda i: (0, 0)) for p in params]
    out_specs = [pl.BlockSpec((tr, w), lambda i: (i, 0)) for (w, _) in outs]
    out_specs += [pl.BlockSpec(sh, lambda i: (0, 0)) for sh in sums]
    out_shape = [_sds((s, w), dt) for (w, dt) in outs] + [_sds(sh, f32) for sh in sums]

    def body(*refs):
        refs = refs[len(after):]
        res = fn(*[r[...] for r in refs[:nin]])
        for r, v in zip(refs[nin:nin + nout], res[:nout]):
            r[...] = v.astype(r.dtype)
        i = pl.program_id(0)
        for r, v in zip(refs[nin + nout:], res[nout:]):
            @pl.when(i == 0)
            def _(r=r, v=v):
                r[...] = v

            @pl.when(i > 0)
            def _(r=r, v=v):
                r[...] += v

    res = pl.pallas_call(
        body, name=name, grid=(s // tr,), in_specs=[pl.BlockSpec(memory_space=pl.ANY)] * len(after) + in_specs,
        out_specs=out_specs, out_shape=out_shape, compiler_params=_cp(("arbitrary",)),
    )(*after, *[r[0] for r in rows], *params)
    return res


def _rms(x, g):
    return x * lax.rsqrt(jnp.mean(x * x, axis=-1, keepdims=True) + EPS) * g


def rms_fwd(x, g, name):
    return rowwise(name, lambda xb, gb: (_rms(xb, gb),), [(x, D, 0)], [g], [(D, bf16)])[0]


def rms_bwd(x, du, dres, g, name, after=()):
    def fn(xb, dub, drb, gb):
        _, vjp = jax.vjp(_rms, xb, gb)
        dx, dg = vjp(dub.astype(f32))
        return drb + dx, drb + dx, dg

    return rowwise(name, fn, [(x, D, 0), (du, D, 0), (dres, D, 0)], [g], [(D, f32), (D, bf16)], [(1, D)], after=after)


def _merge(gates, up_p, up_s, up_g):
    sg = jax.nn.sigmoid(gates)
    return sg[:, :D] * up_p + sg[:, D:2 * D] * up_s + sg[:, 2 * D:] * up_g


def merge_fwd(proj, ups, name):
    return rowwise(name, lambda g, a, b, c: (_merge(g, a, b, c),),
                   [(proj, 3 * D, OFF_G // (3 * D))] + [(u, D, 0) for u in ups], [], [(D, bf16)], tr=128)[0]


def merge_bwd(proj, ups, dmerged, name):
    def fn(g, a, b, c, dm):
        _, vjp = jax.vjp(_merge, g, a, b, c)
        return vjp(dm.astype(f32))

    return rowwise(name, fn, [(proj, 3 * D, OFF_G // (3 * D))] + [(u, D, 0) for u in ups] + [(dmerged, D, 0)], [],
                   [(3 * D, bf16), (D, bf16), (D, bf16), (D, bf16)], tr=128)


def _gdn_out(o, z, g):
    ys = []
    for h in range(NH):
        sl = slice(h * HD, (h + 1) * HD)
        ys.append(_rms(o[:, sl], g) * jax.nn.silu(z[:, sl]))
    return jnp.concatenate(ys, axis=1)


def gdn_out_fwd(o, proj, g, name):
    return rowwise(name, lambda ob, zb, gb: (_gdn_out(ob, zb, gb),), [(o, WH, 0), (proj, WH, OFF_Z // WH)], [g],
                   [(WH, bf16)])[0]


def gdn_out_bwd(o, proj, g, dy, name):
    def fn(ob, zb, dyb, gb):
        _, vjp = jax.vjp(_gdn_out, ob, zb, gb)
        return vjp(dyb.astype(f32))

    return rowwise(name, fn, [(o, WH, 0), (proj, WH, OFF_Z // WH), (dy, WH, 0)], [g], [(WH, f32), (WH, bf16)],
                   [(1, HD)])


def loss_head(x, target, g, name):
    def loss_fn(xb, gb, tb):
        err = _rms(xb, gb) - tb
        return (0.5 / D) * jnp.sum(jnp.sum(err * err, axis=1, keepdims=True), axis=0, keepdims=True)

    def fn(xb, tb, gb):
        val, vjp = jax.vjp(functools.partial(loss_fn, tb=tb), xb, gb)
        dx, dg = vjp(jnp.ones((1, 1), f32))
        return dx, dx, jnp.broadcast_to(val, (1, HD)), dg

    return rowwise(name, fn, [(x, D, 0), (target, D, 0)], [g], [(D, f32), (D, bf16)], [(1, HD), (1, D)])


PB = 256


def _split(v):
    hi = v.astype(bf16)
    return hi, (v - hi.astype(f32)).astype(bf16)


def _band_dot(make_band, v, s, forward):
    hi, lo = _split(v)
    nb = s // PB
    outs = []
    for r in range(nb):
        lo_r = max(r - 1, 0) if forward else r
        hi_r = r + 1 if forward else min(r + 2, nb)
        band = make_band(r * PB, lo_r * PB, (hi_r - lo_r) * PB)
        sl = slice(lo_r * PB, hi_r * PB)
        outs.append(jnp.dot(band, hi[sl], preferred_element_type=f32) + jnp.dot(band, lo[sl], preferred_element_type=f32))
    return jnp.concatenate(outs, axis=0)


def _pool_common(p, win, s):
    def band(row0, col0, ncol):
        t = row0 + lax.broadcasted_iota(jnp.int32, (PB, ncol), 0)
        u = col0 + lax.broadcasted_iota(jnp.int32, (PB, ncol), 1)
        return jnp.logical_and(u <= t, t < u + win).astype(bf16)

    def band_t(row0, col0, ncol):
        u = row0 + lax.broadcasted_iota(jnp.int32, (PB, ncol), 0)
        t = col0 + lax.broadcasted_iota(jnp.int32, (PB, ncol), 1)
        return jnp.logical_and(u <= t, t < u + win).astype(bf16)

    t = lax.broadcasted_iota(jnp.int32, (s, 1), 0)
    inv_n = 1.0 / jnp.minimum(t + 1, win).astype(f32)
    d = _band_dot(band, p, s, True) * inv_n - p
    return d, inv_n, band_t


def pool_fwd(proj, pool_w, pool_scale, name):
    s = proj.shape[0]

    def body(p_ref, w_ref, sc_ref, y_ref):
        win = jnp.left_shift(2, pl.program_id(0))
        d, _, _ = _pool_common(p_ref[...], win, s)
        y = jnp.dot(d.astype(bf16), w_ref[...].astype(bf16), preferred_element_type=f32) * sc_ref[...]
        y_ref[...] = y.astype(bf16)

    return pl.pallas_call(
        body, name=name, grid=(4,),
        in_specs=[pl.BlockSpec((s, HD), lambda g: (0, OFF_P // HD + g)), pl.BlockSpec((None, HD, HD), lambda g: (g, 0, 0)),
                  pl.BlockSpec((1, HD), lambda g: (0, g))],
        out_specs=pl.BlockSpec((s, HD), lambda g: (0, g)), out_shape=_sds((s, W_POOL), bf16),
        compiler_params=_cp(("arbitrary",)),
    )(proj, pool_w, pool_scale)


def pool_bwd(proj, pool_w, pool_scale, dy, name):
    s = proj.shape[0]

    def body(p_ref, w_ref, sc_ref, dy_ref, dp_ref, dw_ref, dsc_ref):
        win = jnp.left_shift(2, pl.program_id(0))
        d, inv_n, band_t = _pool_common(p_ref[...], win, s)
        w = w_ref[...].astype(bf16)
        dyf = dy_ref[...].astype(f32)
        dsc_ref[...] = jnp.sum(dyf * jnp.dot(d.astype(bf16), w, preferred_element_type=f32), axis=0, keepdims=True)
        dys = (dyf * sc_ref[...]).astype(bf16)
        dd = lax.dot_general(dys, w, (((1,), (1,)), ((), ())), preferred_element_type=f32)
        dw_ref[...] = lax.dot_general(d.astype(bf16), dys, (((0,), (0,)), ((), ())), preferred_element_type=f32)
        dp_ref[...] = (_band_dot(band_t, dd * inv_n, s, False) - dd).astype(bf16)

    return pl.pallas_call(
        body, name=name, grid=(4,),
        in_specs=[pl.BlockSpec((s, HD), lambda g: (0, OFF_P // HD + g)), pl.BlockSpec((None, HD, HD), lambda g: (g, 0, 0)),
                  pl.BlockSpec((1, HD), lambda g: (0, g)), pl.BlockSpec((s, HD), lambda g: (0, g))],
        out_specs=[pl.BlockSpec((s, HD), lambda g: (0, g)), pl.BlockSpec((None, HD, HD), lambda g: (g, 0, 0)),
                   pl.BlockSpec((1, HD), lambda g: (0, g))],
        out_shape=[_sds((s, W_POOL), bf16), _sds((4, HD, HD), f32), _sds((1, W_POOL), f32)],
        compiler_params=_cp(("arbitrary",)),
    )(proj, pool_w, pool_scale, dy)


HB = 3
WB = HB * HD


def _heads_of(v):
    return _stack([v[:, h * HD:(h + 1) * HD] for h in range(HB)])


def _bd(a, b, dn):
    return lax.dot_general(a, b, dn, preferred_element_type=f32)


def _run_sum(v, tri):
    hi, lo = _split(v.reshape(HB * TQ, TQ))
    return (jnp.dot(hi, tri, preferred_element_type=f32) + jnp.dot(lo, tri, preferred_element_type=f32)).reshape(HB, TQ, TQ)


def _sb_tile(q, k_ref, kb, qi, carry):
    k = _heads_of(k_ref[pl.ds(pl.multiple_of(kb * TQ, TQ), TQ), :].astype(bf16))
    z = _bd(q, k, BNT)
    row = qi * TQ + lax.broadcasted_iota(jnp.int32, (TQ, TQ), 0)
    col = kb * TQ + lax.broadcasted_iota(jnp.int32, (TQ, TQ), 1)
    mask = col < row
    ls = jnp.where(mask, jax.nn.log_sigmoid(-z), 0.0)
    j = lax.broadcasted_iota(jnp.int32, (TQ, TQ), 0)
    u = lax.broadcasted_iota(jnp.int32, (TQ, TQ), 1)
    lw = _run_sum(ls, (j > u).astype(bf16))
    a = jnp.where(mask, jnp.exp(ls + z + lw + carry), 0.0)
    return z, mask, ls, a


def sb_fwd(proj, name):
    s = proj.shape[0]
    nq = s // TQ
    scale = HD ** -0.5

    def body(q_ref, k_ref, v_ref, y_ref):
        qi = pl.program_id(1)
        q = _heads_of((q_ref[...] * scale).astype(bf16))

        def step(j, c):
            acc, carry = c
            kb = qi - j
            _, _, ls, a = _sb_tile(q, k_ref, kb, qi, carry)
            v = _heads_of(v_ref[pl.ds(pl.multiple_of(kb * TQ, TQ), TQ), :].astype(bf16))
            return acc + _bd(a.astype(bf16), v, BNN), carry + jnp.sum(ls, axis=2, keepdims=True)

        acc, _ = lax.fori_loop(0, qi + 1, step, (jnp.zeros((HB, TQ, HD), f32), jnp.zeros((HB, TQ, 1), f32)))
        for h in range(HB):
            y_ref[:, h * HD:(h + 1) * HD] = acc[h].astype(bf16)

    c0, ng = OFF_SB // WB, NH // HB
    return pl.pallas_call(
        body, name=name, grid=(ng, nq),
        in_specs=[pl.BlockSpec((TQ, WB), lambda g, i: (i, c0 + g)), pl.BlockSpec((s, WB), lambda g, i: (0, c0 + ng + g)),
                  pl.BlockSpec((s, WB), lambda g, i: (0, c0 + 2 * ng + g))],
        out_specs=pl.BlockSpec((TQ, WB), lambda g, i: (i, g)), out_shape=_sds((s, WH), bf16),
        compiler_params=_cp(("arbitrary", "arbitrary")),
    )(proj, proj, proj)


def sb_bwd(proj, dy, name):
    s = proj.shape[0]
    nq = s // TQ
    scale = HD ** -0.5

    def body(q_ref, k_ref, v_ref, do_ref, dq_ref, dk_ref, dv_ref, e_scr, z_scr, dk_acc, dv_acc):
        qi = pl.program_id(1)
        q = _heads_of((q_ref[...] * scale).astype(bf16))
        do = _heads_of(do_ref[...])

        @pl.when(qi == 0)
        def _():
            dk_acc[...] = jnp.zeros_like(dk_acc)
            dv_acc[...] = jnp.zeros_like(dv_acc)

        def add_heads(acc_ref, rows, upd):
            for h in range(HB):
                acc_ref[rows, h * HD:(h + 1) * HD] += upd[h]

        def sweep_back(j, carry):
            kb = qi - j
            rows = pl.ds(pl.multiple_of(kb * TQ, TQ), TQ)
            z, _, ls, a = _sb_tile(q, k_ref, kb, qi, carry)
            v = _heads_of(v_ref[rows, :].astype(bf16))
            e_scr[kb] = _bd(do, v, BNT) * a
            z_scr[kb] = z
            add_heads(dv_acc, rows, _bd(a.astype(bf16), do, BTN))
            return carry + jnp.sum(ls, axis=2, keepdims=True)

        lax.fori_loop(0, qi + 1, sweep_back, jnp.zeros((HB, TQ, 1), f32))

        def sweep_fwd(kb, c):
            dq, carry = c
            rows = pl.ds(pl.multiple_of(kb * TQ, TQ), TQ)
            e, z = e_scr[kb], z_scr[kb]
            row = qi * TQ + lax.broadcasted_iota(jnp.int32, (TQ, TQ), 0)
            col = kb * TQ + lax.broadcasted_iota(jnp.int32, (TQ, TQ), 1)
            j = lax.broadcasted_iota(jnp.int32, (TQ, TQ), 0)
            u = lax.broadcasted_iota(jnp.int32, (TQ, TQ), 1)
            ew = _run_sum(e, (j < u).astype(bf16))
            sig = jax.nn.sigmoid(z)
            dz = jnp.where(col < row, e * (1.0 - sig) - (ew + carry) * sig, 0.0).astype(bf16)
            k = _heads_of(k_ref[rows, :].astype(bf16))
            add_heads(dk_acc, rows, _bd(dz, q, BTN))
            return dq + _bd(dz, k, BNN), carry + jnp.sum(e, axis=2, keepdims=True)

        dq, _ = lax.fori_loop(0, qi + 1, sweep_fwd, (jnp.zeros((HB, TQ, HD), f32), jnp.zeros((HB, TQ, 1), f32)))
        for h in range(HB):
            dq_ref[:, h * HD:(h + 1) * HD] = (dq[h] * scale).astype(bf16)

        @pl.when(qi == nq - 1)
        def _():
            dk_ref[...] = dk_acc[...].astype(bf16)
            dv_ref[...] = dv_acc[...].astype(bf16)

    c0, ng = OFF_SB // WB, NH // HB
    return pl.pallas_call(
        body, name=name, grid=(ng, nq),
        in_specs=[pl.BlockSpec((TQ, WB), lambda g, i: (i, c0 + g)), pl.BlockSpec((s, WB), lambda g, i: (0, c0 + ng + g)),
                  pl.BlockSpec((s, WB), lambda g, i: (0, c0 + 2 * ng + g)), pl.BlockSpec((TQ, WB), lambda g, i: (i, g))],
        out_specs=[pl.BlockSpec((TQ, WB), lambda g, i: (i, g)), pl.BlockSpec((s, WB), lambda g, i: (0, g)),
                   pl.BlockSpec((s, WB), lambda g, i: (0, g))],
        out_shape=[_sds((s, WH), bf16)] * 3,
        scratch_shapes=[pltpu.VMEM((nq, HB, TQ, TQ), f32), pltpu.VMEM((nq, HB, TQ, TQ), f32), pltpu.VMEM((s, WB), f32),
                        pltpu.VMEM((s, WB), f32)],
        compiler_params=_cp(("arbitrary", "arbitrary")),
    )(proj, proj, proj, dy)


CB = 256


def _shift_down(v, k, s):
    if k == 0:
        return v
    row = lax.broadcasted_iota(jnp.int32, v.shape, 0)
    return jnp.where(row < k, 0.0, pltpu.roll(v, k, axis=0))


def _shift_up(v, k, s):
    if k == 0:
        return v
    row = lax.broadcasted_iota(jnp.int32, v.shape, 0)
    return jnp.where(row >= s - k, 0.0, pltpu.roll(v, s - k, axis=0))


def conv_fwd(proj, w, name):
    s = proj.shape[0]

    def body(x_ref, w_ref, y_ref):
        x, wv = x_ref[...], w_ref[...]
        y = sum(wv[3 - k:4 - k, :] * _shift_down(x, k, s) for k in range(4))
        y_ref[...] = jax.nn.silu(y)

    return pl.pallas_call(
        body, name=name, grid=(3 * WH // CB,),
        in_specs=[pl.BlockSpec((s, CB), lambda j: (0, OFF_GQ // CB + j)), pl.BlockSpec((4, CB), lambda j: (0, j))],
        out_specs=pl.BlockSpec((s, CB), lambda j: (0, j)), out_shape=_sds((s, 3 * WH), f32),
        compiler_params=_cp(("parallel",)),
    )(proj, w)


def conv_bwd(proj, w, dc, name, after=()):
    s = proj.shape[0]

    def body(*refs):
        x_ref, w_ref, dc_ref, dx_ref, dw_ref = refs[len(after):]
        x, wv = x_ref[...], w_ref[...]
        xs = [_shift_down(x, k, s) for k in range(4)]
        y = sum(wv[3 - k:4 - k, :] * xs[k] for k in range(4))
        sig = jax.nn.sigmoid(y)
        dy = dc_ref[...] * (sig * (1.0 + y * (1.0 - sig)))
        dx_ref[...] = sum(wv[3 - k:4 - k, :] * _shift_up(dy, k, s) for k in range(4)).astype(bf16)
        dw_ref[...] = jnp.concatenate([jnp.sum(dy * xs[3 - i], axis=0, keepdims=True) for i in range(4)], axis=0)

    return pl.pallas_call(
        body, name=name, grid=(3 * WH // CB,),
        in_specs=[pl.BlockSpec(memory_space=pl.ANY)] * len(after)
        + [pl.BlockSpec((s, CB), lambda j: (0, OFF_GQ // CB + j)), pl.BlockSpec((4, CB), lambda j: (0, j)),
           pl.BlockSpec((s, CB), lambda j: (0, j))],
        out_specs=[pl.BlockSpec((s, CB), lambda j: (0, j)), pl.BlockSpec((4, CB), lambda j: (0, j))],
        out_shape=[_sds((s, 3 * WH), bf16), _sds((4, 3 * WH), f32)],
        compiler_params=_cp(("parallel",)),
    )(*after, proj, w, dc)


SOLVE_PASSES = 3


def _pdot_impl(a, b, dn, passes):
    ah, al = _split(a)
    bh, bl = _split(b)
    dot = lambda p, q: lax.dot_general(p, q, dn, preferred_element_type=f32)
    if passes == 1:
        return dot(ah, bh)
    if passes == 2:
        return dot(ah, bh) + dot(ah, bl)
    return dot(ah, bh) + (dot(ah, bl) + dot(al, bh))


BNN, BNT, BTN = (((2,), (1,)), ((0,), (0,))), (((2,), (2,)), ((0,), (0,))), (((1,), (1,)), ((0,), (0,)))


@functools.partial(jax.custom_vjp, nondiff_argnums=(2,))
def _bdot(a, b, passes):
    return _pdot_impl(a, b, BNN, passes)


def _bdot_fwd(a, b, passes):
    return _pdot_impl(a, b, BNN, passes), (a, b)


def _bdot_bwd(passes, res, ct):
    a, b = res
    return _pdot_impl(ct, b, BNT, passes), _pdot_impl(a, ct, BTN, passes)


_bdot.defvjp(_bdot_fwd, _bdot_bwd)


@functools.partial(jax.custom_vjp, nondiff_argnums=(2,))
def _bdot_nt(a, b, passes):
    return _pdot_impl(a, b, BNT, passes)


def _bdot_nt_fwd(a, b, passes):
    return _pdot_impl(a, b, BNT, passes), (a, b)


def _bdot_nt_bwd(passes, res, ct):
    a, b = res
    return _pdot_impl(ct, b, BNN, passes), _pdot_impl(ct, a, BTN, passes)


_bdot_nt.defvjp(_bdot_nt_fwd, _bdot_nt_bwd)


def _lane_pick(v, h):
    lane = lax.broadcasted_iota(jnp.int32, v.shape, v.ndim - 1)
    return jnp.sum(jnp.where(lane == h, v, 0.0), axis=-1, keepdims=True)


def _stack(parts):
    return jnp.concatenate([p[None] for p in parts], axis=0)


def _heads(v, first):
    return _stack([_lane_pick(v, first + h) for h in range(NH)])


def _l2n(v):
    return v * lax.rsqrt(jnp.sum(v * v, axis=-1, keepdims=True) + EPS)


def _dot_nt(a, b):
    return lax.dot_general(a, b, (((1,), (1,)), ((), ())), preferred_element_type=f32)


def _inverse_impl(lower):
    i = lax.broadcasted_iota(jnp.int32, (CH, CH), 0)
    j = lax.broadcasted_iota(jnp.int32, (CH, CH), 1)
    inv = (i == j).astype(f32) - lower
    pw = _pdot_impl(lower, lower, BNN, SOLVE_PASSES)
    for m in range(1, int(math.log2(CH))):
        inv = inv + _pdot_impl(inv, pw, BNN, SOLVE_PASSES)
        if m < int(math.log2(CH)) - 1:
            pw = _pdot_impl(pw, pw, BNN, SOLVE_PASSES)
    return inv


@jax.custom_vjp
def _unit_lower_inverse(lower):
    return _inverse_impl(lower)


def _unit_lower_inverse_fwd(lower):
    inv = _inverse_impl(lower)
    return inv, inv


def _unit_lower_inverse_bwd(inv, ct):
    return (-_pdot_impl(_pdot_impl(inv, ct, BTN, SOLVE_PASSES), inv, BNT, SOLVE_PASSES),)


_unit_lower_inverse.defvjp(_unit_lower_inverse_fwd, _unit_lower_inverse_bwd)


def _gdn_chunk(cq, ck, cv, ab, alog, dtb):
    ones = jnp.ones((NH, CH, HD), f32)
    q = _l2n(cq) * (HD ** -0.5)
    k = _l2n(ck)
    la = -jnp.exp(_heads(alog, 0)) * jax.nn.softplus(_heads(ab, AB_LANE) + _heads(dtb, 0))
    beta = jax.nn.sigmoid(_heads(ab, AB_LANE + NH)) * ones
    i = lax.broadcasted_iota(jnp.int32, (CH, CH), 0)
    j = lax.broadcasted_iota(jnp.int32, (CH, CH), 1)
    incl, strict = j <= i, j < i
    g = _bdot(jnp.broadcast_to(incl.astype(f32), (NH, CH, CH)), la * ones, 2)
    g_row = _stack([g[h].T for h in range(NH)])
    gamma = jnp.where(incl, jnp.exp(jnp.where(incl, g - g_row, 0.0)), 0.0)
    lower = jnp.where(strict, beta * _bdot_nt(k, k, 1) * gamma, 0.0)
    inv = _unit_lower_inverse(lower)
    eg = jnp.exp(g)
    u = _bdot(inv, cv * beta, SOLVE_PASSES)
    w = _bdot(inv, k * (beta * eg), SOLVE_PASSES)
    qk = _bdot_nt(q, k, 1) * gamma
    g_last = g[:, CH - 1:CH, :]
    return u, w, q * eg, k * jnp.exp(g_last - g), qk, jnp.exp(g_last)


def _by_head(ref, t=0):
    return _stack([ref[:, t * WH + h * HD:t * WH + (h + 1) * HD] for h in range(NH)])


def gdn_a_fwd(c, proj, alog, dtb, name):
    s = c.shape[0]
    nc = s // CH

    def body(c_ref, ab_ref, al_ref, dt_ref, u_ref, w_ref, qd_ref, kd_ref, qk_ref, dec_ref):
        res = _gdn_chunk(_by_head(c_ref, 0), _by_head(c_ref, 1), _by_head(c_ref, 2), ab_ref[...], al_ref[...], dt_ref[...])
        for h in range(NH):
            sl = slice(h * HD, (h + 1) * HD)
            for r, v in zip((u_ref, w_ref, qd_ref, kd_ref, qk_ref), res[:5]):
                r[:, sl] = v[h]
            dec_ref[:, sl] = jnp.broadcast_to(res[5][h], (8, HD))

    row = pl.BlockSpec((CH, WH), lambda n: (n, 0))
    par = pl.BlockSpec((1, HD), lambda n: (0, 0))
    return pl.pallas_call(
        body, name=name, grid=(nc,),
        in_specs=[pl.BlockSpec((CH, 3 * WH), lambda n: (n, 0)), pl.BlockSpec((CH, HD), lambda n: (n, OFF_AB // HD)), par, par],
        out_specs=[row] * 5 + [pl.BlockSpec((None, 8, WH), lambda n: (n, 0, 0))],
        out_shape=[_sds((s, WH), f32)] * 5 + [_sds((nc, 8, WH), f32)],
        compiler_params=_cp(("parallel",)),
    )(c, proj, alog, dtb)


def gdn_a_bwd(c, proj, alog, dtb, cots, ddec, name):
    s = c.shape[0]
    nc = s // CH

    def body(c_ref, ab_ref, al_ref, dt_ref, du_ref, dw_ref, dqd_ref, dkd_ref, dqk_ref, ddec_ref,
             dc_ref, dab_ref, dal_ref, ddt_ref):
        n = pl.program_id(0)
        _, vjp = jax.vjp(_gdn_chunk, _by_head(c_ref, 0), _by_head(c_ref, 1), _by_head(c_ref, 2), ab_ref[...], al_ref[...],
                         dt_ref[...])
        lane = lax.broadcasted_iota(jnp.int32, (1, HD), 1)
        dd = _stack([jnp.where(lane == 0, ddec_ref[0:1, h * HD:(h + 1) * HD], 0.0) for h in range(NH)])
        dcq, dck, dcv, dab, dal, ddt = vjp(tuple(_by_head(r) for r in (du_ref, dw_ref, dqd_ref, dkd_ref, dqk_ref)) + (dd,))
        for h in range(NH):
            for t, v in enumerate((dcq, dck, dcv)):
                dc_ref[:, t * WH + h * HD:t * WH + (h + 1) * HD] = v[h]
        dab_ref[...] = dab.astype(bf16)

        @pl.when(n == 0)
        def _():
            dal_ref[...] = dal
            ddt_ref[...] = ddt

        @pl.when(n > 0)
        def _():
            dal_ref[...] += dal
            ddt_ref[...] += ddt

    row = pl.BlockSpec((CH, WH), lambda n: (n, 0))
    wide = pl.BlockSpec((CH, 3 * WH), lambda n: (n, 0))
    par = pl.BlockSpec((1, HD), lambda n: (0, 0))
    return pl.pallas_call(
        body, name=name, grid=(nc,),
        in_specs=[wide, pl.BlockSpec((CH, HD), lambda n: (n, OFF_AB // HD)), par, par] + [row] * 5
        + [pl.BlockSpec((None, 8, WH), lambda n: (n, 0, 0))],
        out_specs=[wide, pl.BlockSpec((CH, HD), lambda n: (n, 0)), par, par],
        out_shape=[_sds((s, 3 * WH), f32), _sds((s, HD), bf16), _sds((1, HD), f32), _sds((1, HD), f32)],
        compiler_params=_cp(("arbitrary",)),
    )(c, proj, alog, dtb, *cots, ddec)


def gdn_b_fwd(u, w, qd, kd, qk, dec, name):
    s = u.shape[0]
    nc = s // CH

    def body(u_ref, w_ref, qd_ref, kd_ref, qk_ref, dec_ref, o_ref, st_ref, state):
        n = pl.program_id(0)

        @pl.when(n == 0)
        def _():
            state[...] = jnp.zeros_like(state)

        for h in range(NH):
            sl = slice(h * HD, (h + 1) * HD)
            st = state[sl, :]
            st_ref[sl, :] = st
            sb = st.astype(bf16)
            vn = u_ref[:, sl] - jnp.dot(w_ref[:, sl].astype(bf16), sb, preferred_element_type=f32)
            vb = vn.astype(bf16)
            o_ref[:, sl] = (jnp.dot(qd_ref[:, sl].astype(bf16), sb, preferred_element_type=f32)
                            + jnp.dot(qk_ref[:, sl].astype(bf16), vb, preferred_element_type=f32))
            state[sl, :] = st * dec_ref[0:1, sl] + lax.dot_general(
                kd_ref[:, sl].astype(bf16), vb, (((0,), (0,)), ((), ())), preferred_element_type=f32)

    row = pl.BlockSpec((CH, WH), lambda n: (n, 0))
    return pl.pallas_call(
        body, name=name, grid=(nc,),
        in_specs=[row] * 5 + [pl.BlockSpec((None, 8, WH), lambda n: (n, 0, 0))],
        out_specs=[row, pl.BlockSpec((None, WH, HD), lambda n: (n, 0, 0))],
        out_shape=[_sds((s, WH), f32), _sds((nc, WH, HD), f32)],
        scratch_shapes=[pltpu.VMEM((WH, HD), f32)],
        compiler_params=_cp(("arbitrary",)),
    )(u, w, qd, kd, qk, dec)


def gdn_b_bwd(u, w, qd, kd, qk, dec, states, do, name):
    s = u.shape[0]
    nc = s // CH

    def body(u_ref, w_ref, qd_ref, kd_ref, qk_ref, dec_ref, st_ref, do_ref,
             du_ref, dw_ref, dqd_ref, dkd_ref, dqk_ref, ddec_ref, dstate):
        n = pl.program_id(0)

        @pl.when(n == 0)
        def _():
            dstate[...] = jnp.zeros_like(dstate)

        for h in range(NH):
            sl = slice(h * HD, (h + 1) * HD)
            st, ds = st_ref[sl, :], dstate[sl, :]
            sb, dsb = st.astype(bf16), ds.astype(bf16)
            wb, qdb, kdb, qkb = (r[:, sl].astype(bf16) for r in (w_ref, qd_ref, kd_ref, qk_ref))
            dob = do_ref[:, sl].astype(bf16)
            vn = u_ref[:, sl] - jnp.dot(wb, sb, preferred_element_type=f32)
            vb = vn.astype(bf16)
            dvn = (lax.dot_general(qkb, dob, (((0,), (0,)), ((), ())), preferred_element_type=f32)
                   + jnp.dot(kdb, dsb, preferred_element_type=f32))
            dvb = dvn.astype(bf16)
            du_ref[:, sl] = dvn
            dw_ref[:, sl] = -_dot_nt(dvb, sb)
            dqd_ref[:, sl] = _dot_nt(dob, sb)
            dkd_ref[:, sl] = _dot_nt(vb, dsb)
            dqk_ref[:, sl] = _dot_nt(dob, vb)
            tot = jnp.sum(jnp.sum(ds * st, axis=1, keepdims=True), axis=0, keepdims=True)
            ddec_ref[:, sl] = jnp.broadcast_to(tot, (8, HD))
            dstate[sl, :] = (ds * dec_ref[0:1, sl]
                             + lax.dot_general(qdb, dob, (((0,), (0,)), ((), ())), preferred_element_type=f32)
                             - lax.dot_general(wb, dvb, (((0,), (0,)), ((), ())), preferred_element_type=f32))

    row = pl.BlockSpec((CH, WH), lambda n: (nc - 1 - n, 0))
    small = pl.BlockSpec((None, 8, WH), lambda n: (nc - 1 - n, 0, 0))
    return pl.pallas_call(
        body, name=name, grid=(nc,),
        in_specs=[row] * 5 + [small, pl.BlockSpec((None, WH, HD), lambda n: (nc - 1 - n, 0, 0)), row],
        out_specs=[row] * 5 + [small],
        out_shape=[_sds((s, WH), f32)] * 5 + [_sds((nc, 8, WH), f32)],
        scratch_shapes=[pltpu.VMEM((WH, HD), f32)],
        compiler_params=_cp(("arbitrary",)),
    )(u, w, qd, kd, qk, dec, states, do)


ANY = pl.BlockSpec(memory_space=pl.ANY)


def _dev_index(p):
    return 4 * p[0] + 2 * p[1] + p[2]


def _shard_of(ref, axis, size, idx):
    return ref.at[pl.ds(idx * size, size), :] if axis == 0 else ref.at[:, pl.ds(idx * size, size)]


def _peer(x, y, c, r):
    return (1 - x if r & 4 else x, 1 - y if r & 2 else y, 1 - c if r & 1 else c)


ALL_PEERS, SIBLING, OTHER_CHIPS = tuple(range(1, NDEV)), (1,), (4, 2, 6)


def _launch(body, ins, out_sds, name, sequencer_id, relations=ALL_PEERS, kinds=7):
    n = len(ins)
    sems = [pltpu.SemaphoreType.DMA((n, kinds)), pltpu.SemaphoreType.DMA((n, kinds)), pltpu.SemaphoreType.DMA((n,))]
    if sequencer_id is None:
        return pl.pallas_call(
            lambda *refs: body(refs[:n], refs[n:2 * n], *refs[2 * n:]), name=name, in_specs=[ANY] * n, out_specs=[ANY] * n,
            out_shape=out_sds, scratch_shapes=sems, compiler_params=pltpu.CompilerParams(has_side_effects=True),
        )(*ins)
    in_refs = [jax.new_ref(a, memory_space=pltpu.MemorySpace.HBM) for a in ins]
    out_refs = [jax.empty_ref(sd, memory_space=pltpu.MemorySpace.HBM) for sd in out_sds]

    @pl.kernel(mesh=plsc.ScalarSubcoreMesh(axis_name="sequencer", num_cores=1), name=name, scratch_types=sems,
               compiler_params=pltpu.CompilerParams(collective_id=sequencer_id))
    def launch(send, recv, loc):
        x, y, c = lax.axis_index("x"), lax.axis_index("y"), lax.axis_index("c")
        barrier = pltpu.get_barrier_semaphore()
        for r in relations:
            pl.semaphore_signal(barrier, inc=1, device_id=_peer(x, y, c, r), device_id_type=MESH)
        pl.semaphore_wait(barrier, len(relations))
        body(in_refs, out_refs, send, recv, loc)

    launch()
    return [r[...] for r in out_refs]


def all_gather(xs, axes, name, sequencer_id=None):
    n = len(xs)
    fulls = [tuple(d * (NDEV if a == ax else 1) for a, d in enumerate(x.shape)) for x, ax in zip(xs, axes)]
    halved = [x.shape[0] % 32 == 0 for x in xs]

    def body(x_refs, o_refs, send, recv, loc):
        x, y, c = lax.axis_index("x"), lax.axis_index("y"), lax.axis_index("c")
        me, sib = (x, y, c), (x, y, 1 - c)
        xn, yn, dg = (1 - x, y), (x, 1 - y), (1 - x, 1 - y)

        def part(t, p, half=None):
            ref = _shard_of(o_refs[t], axes[t], xs[t].shape[axes[t]], _dev_index(p))
            rows = xs[t].shape[0] // 2
            return ref if half is None else ref.at[pl.ds(half * rows, rows), :]

        def copy(t, k, block, to, half=None, src=None):
            return pltpu.make_async_remote_copy(
                src_ref=part(t, block, half) if src is None else src, dst_ref=part(t, block, half),
                send_sem=send.at[t, k], recv_sem=recv.at[t, k], device_id=to, device_id_type=MESH)

        mine = [pltpu.make_async_copy(x_refs[t], part(t, me), loc.at[t]) for t in range(n)]
        for cp in mine:
            cp.start()
        sends = []
        for t in range(n):
            sends += [copy(t, 0, me, sib, src=x_refs[t]), copy(t, 1, me, (*xn, c), src=x_refs[t]),
                      copy(t, 2, me, (*yn, c), src=x_refs[t])]
            if not halved[t]:
                sends.append(copy(t, 3, me, (*dg, c), src=x_refs[t]))
        for cp in sends:
            cp.start()

        def pass_on(cp):
            cp.start()
            sends.append(cp)

        for t in range(n):
            h0, h1 = (0, 1) if halved[t] else (None, None)
            copy(t, 1, (*xn, c), me).wait_recv()
            if halved[t]:
                pass_on(copy(t, 3, (*xn, c), (*yn, c), 0))
            pass_on(copy(t, 5, (*xn, c), sib))
            copy(t, 2, (*yn, c), me).wait_recv()
            if halved[t]:
                pass_on(copy(t, 4, (*yn, c), (*xn, c), 1))
            pass_on(copy(t, 6, (*yn, c), sib))
            copy(t, 3, (*dg, c), me, h0).wait_recv()
            pass_on(copy(t, 7, (*dg, c), sib, h0))
            if halved[t]:
                copy(t, 4, (*dg, c), me, h1).wait_recv()
                pass_on(copy(t, 8, (*dg, c), sib, h1))
        for t in range(n):
            h0, h1 = (0, 1) if halved[t] else (None, None)
            copy(t, 0, sib, me).wait_recv()
            copy(t, 5, (*xn, 1 - c), me).wait_recv()
            copy(t, 6, (*yn, 1 - c), me).wait_recv()
            copy(t, 7, (*dg, 1 - c), me, h0).wait_recv()
            if halved[t]:
                copy(t, 8, (*dg, 1 - c), me, h1).wait_recv()
        for cp in sends:
            cp.wait_send()
        for cp in mine:
            cp.wait()

    return _launch(body, xs, [_sds(f, x.dtype) for f, x in zip(fulls, xs)], name, sequencer_id, kinds=9)


def pair_swap(gs, axes, name, sequencer_id):
    n = len(gs)
    shards = [tuple(d // (NDEV if a == ax else 1) for a, d in enumerate(g.shape)) for g, ax in zip(gs, axes)]

    def body(g_refs, p_refs, send, recv, loc):
        x, y, c = lax.axis_index("x"), lax.axis_index("y"), lax.axis_index("c")

        def copy(t, j):
            return pltpu.make_async_remote_copy(
                src_ref=_shard_of(g_refs[t], axes[t], shards[t][axes[t]], 2 * j + (1 - c)), dst_ref=p_refs[t].at[j],
                send_sem=send.at[t, j], recv_sem=recv.at[t, j], device_id=(x, y, 1 - c), device_id_type=MESH)

        copies = [copy(t, j) for t in range(n) for j in range(4)]
        for cp in copies:
            cp.start()
        for cp in copies:
            cp.wait_recv()
        for cp in copies:
            cp.wait_send()

    return _launch(body, gs, [_sds((4,) + sh, g.dtype) for sh, g in zip(shards, gs)], name, sequencer_id, SIBLING)


def pair_add(g, p, axis, name, after=()):
    _, rows, cols = p.shape
    tr = _row_tile(rows, cols, 1 << 20)
    nb = rows // tr
    if axis == 0:
        g_spec = pl.BlockSpec((tr, cols), lambda j, i, c: ((2 * j + c[0]) * nb + i, 0))
    else:
        g_spec = pl.BlockSpec((tr, cols), lambda j, i, c: (i, 2 * j + c[0]))
    blk = pl.BlockSpec((None, tr, cols), lambda j, i, c: (j, i, 0))

    na = len(after)

    def body(c_ref, *refs):
        g_ref, p_ref, q_ref = refs[na:]
        q_ref[...] = (g_ref[...].astype(f32) + p_ref[...].astype(f32)).astype(bf16)

    return pl.pallas_call(
        body, name=name, out_shape=_sds(p.shape, bf16),
        grid_spec=pltpu.PrefetchScalarGridSpec(num_scalar_prefetch=1, grid=(4, nb), in_specs=[ANY] * na + [g_spec, blk],
                                               out_specs=blk),
        compiler_params=_cp(("parallel", "parallel")),
    )(lax.axis_index("c").astype(jnp.int32).reshape(1), *after, g, p)


def chip_exchange(qs, name, sequencer_id):
    n = len(qs)

    def body(q_refs, r_refs, send, recv, loc):
        x, y, c = lax.axis_index("x"), lax.axis_index("y"), lax.axis_index("c")
        my_chip = 2 * x + y
        peers = [_peer(x, y, c, r) for r in OTHER_CHIPS]
        mine = [pltpu.make_async_copy(q_refs[t].at[my_chip], r_refs[t].at[my_chip], loc.at[t]) for t in range(n)]
        for cp in mine:
            cp.start()

        def copy(t, k, slot):
            p = peers[k]
            return pltpu.make_async_remote_copy(
                src_ref=q_refs[t].at[2 * p[0] + p[1]], dst_ref=r_refs[t].at[slot], send_sem=send.at[t, k], recv_sem=recv.at[t, k],
                device_id=p, device_id_type=MESH)

        sends = [copy(t, k, my_chip) for t in range(n) for k in range(3)]
        for cp in sends:
            cp.start()
        for t in range(n):
            for k in range(3):
                copy(t, k, 2 * peers[k][0] + peers[k][1]).wait_recv()
        for cp in sends:
            cp.wait_send()
        for cp in mine:
            cp.wait()

    return _launch(body, qs, [_sds(q.shape, q.dtype) for q in qs], name, sequencer_id, OTHER_CHIPS)


def _adamw(w, g, m, v):
    m = ADAM_B1 * m + (1.0 - ADAM_B1) * g
    v = ADAM_B2 * v + (1.0 - ADAM_B2) * jnp.square(g)
    m_hat = m / (1.0 - ADAM_B1 ** ADAM_STEP)
    v_hat = v / (1.0 - ADAM_B2 ** ADAM_STEP)
    return -ADAM_LR * (m_hat / (jnp.sqrt(v_hat) + ADAM_EPS) + ADAM_WD * w), m, v


def _sum8(r_ref):
    g = r_ref[0].astype(f32)
    for j in range(1, r_ref.shape[0]):
        g = g + r_ref[j].astype(f32)
    return g


def _row_tile(rows, cols, elems=1 << 18):
    tr = min(rows, max(8, 1 << int(math.log2(elems / cols))))
    assert rows % tr == 0, (rows, cols)
    return tr


def sum_partials(r, name, after=()):
    _, rows, cols = r.shape
    tr = _row_tile(rows, cols)
    na = len(after)

    def body(*refs):
        refs[na + 1][...] = _sum8(refs[na])

    return pl.pallas_call(
        body, name=name, grid=(rows // tr,),
        in_specs=[ANY] * na + [pl.BlockSpec((r.shape[0], tr, cols), lambda i: (0, i, 0))],
        out_specs=pl.BlockSpec((tr, cols), lambda i: (i, 0)), out_shape=_sds((rows, cols), f32),
        compiler_params=_cp(("parallel",)),
    )(*after, r)


def adamw_t(w, m, v, grads, name):
    rows, nl, cols = w.shape
    tr = min(rows, (1 << 16) // cols)
    blk = pl.BlockSpec((tr, nl, cols), lambda i: (i, 0, 0))
    flat = pl.BlockSpec((tr, cols), lambda i: (i, 0))

    def body(w_ref, m_ref, v_ref, *rest):
        g_refs, (g_ref, d_ref, nm_ref, nv_ref) = rest[:nl], rest[nl:]
        for l in range(nl):
            grad = g_refs[l][...]
            g_ref[:, l, :] = grad
            d_ref[:, l, :], nm_ref[:, l, :], nv_ref[:, l, :] = _adamw(w_ref[:, l, :], grad, m_ref[:, l, :], v_ref[:, l, :])

    return tuple(pl.pallas_call(
        body, name=name, grid=(pl.cdiv(rows, tr),), in_specs=[blk] * 3 + [flat] * nl, out_specs=[blk] * 4,
        out_shape=[_sds(w.shape, f32)] * 4, compiler_params=_cp(("parallel",)),
    )(w, m, v, *grads))


def adamw(w, m, v, layer, name, r=None, g=None, prev=None):
    _, rows, cols = w.shape
    tr = _row_tile(rows, cols)
    blk = pl.BlockSpec((None, tr, cols), lambda i: (layer, i, 0))
    nprev = 0 if prev is None else 4

    def body(w_ref, m_ref, v_ref, src_ref, *rest):
        g_ref, d_ref, nm_ref, nv_ref, token_ref = rest[nprev:]
        grad = _sum8(src_ref) if g is None else src_ref[...]
        g_ref[...] = grad
        d_ref[...], nm_ref[...], nv_ref[...] = _adamw(w_ref[...], grad, m_ref[...], v_ref[...])
        token_ref[...] = jnp.zeros_like(token_ref)

    src, src_spec = (r, pl.BlockSpec((r.shape[0], tr, cols), lambda i: (0, i, 0))) if g is None else (g, pl.BlockSpec((tr, cols), lambda i: (i, 0)))
    *outs, token = pl.pallas_call(
        body, name=name, grid=(rows // tr,), in_specs=[blk] * 3 + [src_spec] + [ANY] * nprev,
        out_specs=[blk] * 4 + [pl.BlockSpec((8, HD), lambda i: (0, 0))],
        out_shape=[_sds(w.shape, f32)] * 4 + [_sds((8, HD), f32)], input_output_aliases={4 + k: k for k in range(nprev)},
        compiler_params=_cp(("arbitrary",)),
    )(w, m, v, src, *(prev or ()))
    return tuple(outs), token


def small_adamw(parts, w, m, v, name, after=()):
    def body(*refs):
        p_ref, w_ref, m_ref, v_ref, g_ref, d_ref, nm_ref, nv_ref = refs[len(after):]
        g = _sum8(p_ref)
        g_ref[...] = g
        d_ref[...], nm_ref[...], nv_ref[...] = _adamw(w_ref[...], g, m_ref[...], v_ref[...])

    vmem = pl.BlockSpec(memory_space=pltpu.VMEM)
    return pl.pallas_call(body, name=name, out_shape=[_sds(w.shape, f32)] * 4, in_specs=[ANY] * len(after) + [vmem] * 4,
                          out_specs=[vmem] * 4, compiler_params=_cp())(*after, parts, w, m, v)


def _pack_rows(wt):
    tail = jnp.pad(wt[5888:N_IN], ((OFF_G - 12 - (OFF_P + 512), 0), (0, 0)))
    return jnp.concatenate([wt[512:2816], wt[2816:5120], wt[5120:5888], wt[0:512], tail], axis=0)


def _unpack_rows(g):
    return jnp.concatenate([g[OFF_P:OFF_P + 512], g[OFF_SB:OFF_SB + 2304], g[OFF_GQ:OFF_GQ + 2304], g[OFF_Z:OFF_Z + 768],
                            g[OFF_G - 12:NP]], axis=0)


def _lanes(v):
    flat = v.reshape(-1)
    n = -(-flat.shape[0] // HD) * HD
    return jnp.pad(flat, (0, n - flat.shape[0])).reshape(n // HD, HD)


def _lanes8(v):
    rows = _lanes(v)
    return jnp.pad(rows, ((0, -rows.shape[0] % 8), (0, 0)))


def _layer_fwd(x, p, l):
    nm = lambda s: f"{s}_l{l}"
    u = rms_fwd(x, p["attn_norm"], nm("rms1"))
    proj = matmul(u, p["w_in"], name=nm("inproj"), tb=True)
    y_pool = pool_fwd(proj, p["pool_w"], p["pool_scale"], nm("pool"))
    y_sb = sb_fwd(proj, nm("sb"))
    c = conv_fwd(proj, p["conv"], nm("conv"))
    ga = gdn_a_fwd(c, proj, p["alog"], p["dtb"], nm("gdna"))
    o, states = gdn_b_fwd(*ga, nm("gdnb"))
    y_gdn = gdn_out_fwd(o, proj, p["gdn_norm"], nm("gdno"))
    ups = [matmul(y, p[k], name=nm(k)) for y, k in ((y_pool, "w_pool_up"), (y_sb, "w_sb_up"), (y_gdn, "w_gdn_up"))]
    merged = merge_fwd(proj, ups, nm("merge"))
    x1 = matmul(merged, p["w_out"], name=nm("outproj"), epilogue=lambda acc, r: acc + r, extras=(x,))
    u2 = rms_fwd(x1, p["mlp_norm"], nm("rms2"))
    h2 = matmul(u2, p["w_ff1"], name=nm("ff1"), out_dtype=bf16, epilogue=lambda acc: jnp.square(jnp.maximum(acc, 0.0)))
    x2 = matmul(h2, p["w_ff2"], name=nm("ff2"), epilogue=lambda acc, r: acc + r, extras=(x1,))
    saved = dict(x=x, u=u, proj=proj, y_pool=y_pool, y_sb=y_sb, c=c, ga=ga, o=o, states=states, y_gdn=y_gdn, ups=ups,
                 merged=merged, x1=x1, u2=u2, h2=h2)
    return x2, saved


def _layer_bwd(dx2, dx2b, sv, p, l, swap, finish):
    nm = lambda s: f"{s}_l{l}"
    s = dx2.shape[0]
    dh = matmul(dx2b, p["w_ff2"], name=nm("d_ff2_x"), tb=True, out_dtype=bf16,
                epilogue=lambda acc, h2: acc * (2.0 * jnp.sqrt(h2.astype(f32))), extras=(sv["h2"],))
    g_ff2 = matmul(sv["h2"], dx2b, name=nm("d_ff2_w"), ta=True, out_dtype=bf16)
    du2 = matmul(dh, p["w_ff1"], name=nm("d_ff1_x"), tb=True)
    g_ff1 = matmul(sv["u2"], dh, name=nm("d_ff1_w"), ta=True, out_dtype=bf16)
    dx1, dx1b, d_mlp_norm = rms_bwd(sv["x1"], du2, dx2, p["mlp_norm"], nm("d_rms2"))
    dmerged = matmul(dx1b, p["w_out"], name=nm("d_out_x"), tb=True, out_dtype=bf16)
    g_out = matmul(sv["merged"], dx1b, name=nm("d_out_w"), ta=True, out_dtype=bf16)
    dgates, *dups = merge_bwd(sv["proj"], sv["ups"], dmerged, nm("d_merge"))
    dys, g_ups = [], []
    for dup, y, k in zip(dups, (sv["y_pool"], sv["y_sb"], sv["y_gdn"]), ("w_pool_up", "w_sb_up", "w_gdn_up")):
        dys.append(matmul(dup, p[k], name=nm("d_" + k + "_x"), tb=True, out_dtype=bf16))
        g_ups.append(matmul(y, dup, name=nm("d_" + k + "_w"), ta=True, out_dtype=bf16))
    early = g_ups + [g_out, g_ff1, g_ff2]
    swapped = swap(early, BIG_AXES[1:], "a")
    do, dz, d_gdn_norm = gdn_out_bwd(sv["o"], sv["proj"], p["gdn_norm"], dys[2], nm("d_gdno"))
    cots = gdn_b_bwd(*sv["ga"], sv["states"], do, nm("d_gdnb"))
    dc, dab, d_alog, d_dtb = gdn_a_bwd(sv["c"], sv["proj"], p["alog"], p["dtb"], cots[:5], cots[5], nm("d_gdna"))
    r_early, sent_early = finish(early, swapped, BIG_AXES[1:], BIG_NAMES[1:], "a", (dab,))
    dgq, d_conv = conv_bwd(sv["proj"], p["conv"], dc, nm("d_conv"), sent_early)
    dq, dk, dv = sb_bwd(sv["proj"], dys[1], nm("d_sb"))
    dp, d_pool_w, d_pool_scale = pool_bwd(sv["proj"], p["pool_w"], p["pool_scale"], dys[0], nm("d_pool"))
    dproj = jnp.concatenate([dq, dk, dv, dgq, dz, dp, jnp.zeros((s, OFF_AB - OFF_P - W_POOL), bf16), dab, dgates], axis=1)
    g_in = matmul(dproj, sv["u"], name=nm("d_in_w"), ta=True, out_dtype=bf16)
    r_in, sent_in = finish([g_in], swap([g_in], BIG_AXES[:1], "b"), BIG_AXES[:1], BIG_NAMES[:1], "b", ())
    du = matmul(dproj, p["w_in"], name=nm("d_in_x"), after=sent_in)
    dx, dxb, d_attn_norm = rms_bwd(sv["x"], du, dx1, p["attn_norm"], nm("d_rms1"))
    recv = r_in + r_early
    small = [d_attn_norm, d_pool_w, d_pool_scale, d_conv, d_alog, d_dtb, d_gdn_norm, d_mlp_norm]
    return dx, dxb, recv, small


BIG_AXES = (1, 1, 1, 1, 0, 1, 0)
GATHER_ID, EXCHANGE_ID = 1, 2
BIG_NAMES = ("w_in", "w_pool_up", "w_sb_up", "w_gdn_up", "w_out", "w_ff1", "w_ff2")


def kernel(x, attn_norm, w_in, pool_w, pool_scale, gdn_conv, gdn_a_log, gdn_dt_bias, gdn_norm, w_pool_up, w_sb_up, w_gdn_up, w_out, mlp_norm, w_ff1, w_ff2, final_norm, loss_target, m_attn_norm, m_w_in, m_pool_w, m_pool_scale, m_gdn_conv, m_gdn_a_log, m_gdn_dt_bias, m_gdn_norm, m_w_pool_up, m_w_sb_up, m_w_gdn_up, m_w_out, m_mlp_norm, m_w_ff1, m_w_ff2, m_final_norm, v_attn_norm, v_w_in, v_pool_w, v_pool_scale, v_gdn_conv, v_gdn_a_log, v_gdn_dt_bias, v_gdn_norm, v_w_pool_up, v_w_sb_up, v_w_gdn_up, v_w_out, v_mlp_norm, v_w_ff1, v_w_ff2, v_final_norm):
    s = x.shape[1]
    me = _dev_index((lax.axis_index("x"), lax.axis_index("y"), lax.axis_index("c")))
    ncv = gdn_conv.shape[2]

    w_in_t, m_w_in_t, v_w_in_t = (jnp.transpose(a, (2, 0, 1)) for a in (w_in, m_w_in, v_w_in))
    full = []
    for l in range(NL):
        shards = [_pack_rows(w_in_t[:, l]).astype(bf16), w_pool_up[l].astype(bf16), w_sb_up[l].astype(bf16),
                  w_gdn_up[l].astype(bf16), w_out[l].astype(bf16), w_ff1[l].astype(bf16), w_ff2[l].astype(bf16)]
        if l == 0:
            first = all_gather([shards[0], gdn_conv.reshape(NL * 4, ncv)], (1, 0), "gather_first", sequencer_id=GATHER_ID)
            rest = all_gather(shards[1:], BIG_AXES[1:], "gather_rest_l0", sequencer_id=GATHER_ID)
            full.append([first[0]] + rest)
            conv_full = first[1].reshape(NDEV, NL, 4, ncv).transpose(1, 2, 0, 3).reshape(NL, 4, NDEV * ncv)
        else:
            first = all_gather(shards[:1], BIG_AXES[:1], f"gather_in_l{l}", sequencer_id=GATHER_ID)
            full.append(first + all_gather(shards[1:], BIG_AXES[1:], f"gather_rest_l{l}", sequencer_id=GATHER_ID))
    params = []
    for l in range(NL):
        p = dict(zip(("w_in", "w_pool_up", "w_sb_up", "w_gdn_up", "w_out", "w_ff1", "w_ff2"), full[l][:7]))
        p.update(attn_norm=attn_norm[l][None], mlp_norm=mlp_norm[l][None], pool_w=pool_w[l], pool_scale=pool_scale[l][None],
                 conv=conv_full[l], alog=_lanes(gdn_a_log[l]), dtb=_lanes(gdn_dt_bias[l]), gdn_norm=gdn_norm[l][None])
        params.append(p)

    h = x[0]
    saved = []
    for l in range(NL):
        h, sv = _layer_fwd(h, params[l], l)
        saved.append(sv)
    dh, dhb, loss_row, d_final = loss_head(h, loss_target[0], final_norm[None], "loss_head")
    recv, smalls = [None] * NL, [None] * NL
    for l in reversed(range(NL)):
        def swap(gs, axes, tag, l=l):
            return pair_swap(gs, axes, f"swap_{tag}_l{l}", None)

        def finish(gs, ps, axes, names, tag, after, l=l):
            qs = [pair_add(g, p, ax, f"pair_add_{k}_l{l}", after) for g, p, ax, k in zip(gs, ps, axes, names)]
            return chip_exchange(qs, f"exchange_{tag}_l{l}", EXCHANGE_ID), tuple(qs)

        dh, dhb, recv[l], smalls[l] = _layer_bwd(dh, dhb, saved[l], params[l], l, swap, finish)

    small_rows = [_lanes8(t) for l in range(NL) for t in smalls[l]] + [_lanes8(d_final), _lanes8(loss_row)]
    packed = jnp.concatenate(small_rows, axis=0)
    parts = all_gather([packed], (0,), "gather_small", sequencer_id=GATHER_ID)[0].reshape(NDEV, packed.shape[0], HD)

    def pack_small(tree):
        rows = []
        for l in range(NL):
            rows += [_lanes8(tree["attn_norm"][l]), _lanes8(tree["pool_w"][l]), _lanes8(tree["pool_scale"][l]),
                     jnp.zeros((4 * NDEV * ncv // HD, HD), f32), _lanes8(tree["gdn_a_log"][l]), _lanes8(tree["gdn_dt_bias"][l]),
                     _lanes8(tree["gdn_norm"][l]), _lanes8(tree["mlp_norm"][l])]
        rows += [_lanes8(tree["final_norm"]), jnp.zeros((8, HD), f32)]
        return jnp.concatenate(rows, axis=0)

    names = ("attn_norm", "pool_w", "pool_scale", "gdn_a_log", "gdn_dt_bias", "gdn_norm", "mlp_norm", "final_norm")
    w_small = pack_small(dict(zip(names, (attn_norm, pool_w, pool_scale, gdn_a_log, gdn_dt_bias, gdn_norm, mlp_norm, final_norm))))
    m_small = pack_small(dict(zip(names, (m_attn_norm, m_pool_w, m_pool_scale, m_gdn_a_log, m_gdn_dt_bias, m_gdn_norm, m_mlp_norm, m_final_norm))))
    v_small = pack_small(dict(zip(names, (v_attn_norm, v_pool_w, v_pool_scale, v_gdn_a_log, v_gdn_dt_bias, v_gdn_norm, v_mlp_norm, v_final_norm))))
    def unpack_small(buf):
        out, conv_g, r = {}, [], 0
        layer_items = (("attn_norm", (D,)), ("pool_w", (4, HD, HD)), ("pool_scale", (W_POOL,)), ("conv", (4, NDEV * ncv)),
                       ("gdn_a_log", (NH,)), ("gdn_dt_bias", (NH,)), ("gdn_norm", (HD,)), ("mlp_norm", (D,)))
        per_layer = {k: [] for k, _ in layer_items}
        for l in range(NL):
            for k, shape in layer_items:
                size = math.prod(shape)
                nrow = -(-size // (8 * HD)) * 8
                per_layer[k].append(buf[r:r + nrow].reshape(-1)[:size].reshape(shape))
                r += nrow
        for k, _ in layer_items:
            out[k] = jnp.stack(per_layer[k])
        out["final_norm"] = buf[r:r + D // HD].reshape(D)
        out["loss"] = buf[r + D // HD, 0]
        return out

    big_out = {}
    big_names = BIG_NAMES
    big_w = dict(zip(big_names, ((w_in_t, m_w_in_t, v_w_in_t), (w_pool_up, m_w_pool_up, v_w_pool_up), (w_sb_up, m_w_sb_up, v_w_sb_up),
                                 (w_gdn_up, m_w_gdn_up, v_w_gdn_up), (w_out, m_w_out, v_w_out), (w_ff1, m_w_ff1, v_w_ff1),
                                 (w_ff2, m_w_ff2, v_w_ff2))))
    tokens = []
    for l, k in [(l, k) for l in reversed(range(NL)) for k in big_names[1:]]:
        w, m, v = big_w[k]
        big_out[k], token = adamw(w, m, v, l, f"adamw_{k}_l{l}", r=recv[l][big_names.index(k)], prev=big_out.get(k))
        tokens.append(token)
    g_in = [_unpack_rows(sum_partials(recv[l][0], f"sum_w_in_l{l}", tuple(tokens) if l == 0 else ())) for l in range(NL)]
    big_out["w_in"] = adamw_t(*big_w["w_in"], g_in, "adamw_w_in")

    small_out = small_adamw(parts, w_small, m_small, v_small, "adamw_small", after=(big_out["w_in"][1],))
    sm = [unpack_small(b) for b in small_out]
    loss = sm[0]["loss"]
    g_conv = lax.dynamic_slice_in_dim(sm[0]["conv"], me * ncv, ncv, axis=2)
    conv_out = None
    for l in reversed(range(NL)):
        conv_out, _ = adamw(gdn_conv, m_gdn_conv, v_gdn_conv, l, f"adamw_conv_l{l}", g=g_conv[l], prev=conv_out)

    def leaf(i, k):
        if k == "w_in":
            return jnp.transpose(big_out[k][i], (1, 2, 0))
        if k in big_out:
            return big_out[k][i]
        if k == "gdn_conv":
            return conv_out[i]
        return sm[i][k]

    order = ("attn_norm", "w_in", "pool_w", "pool_scale", "gdn_conv", "gdn_a_log", "gdn_dt_bias", "gdn_norm", "w_pool_up",
             "w_sb_up", "w_gdn_up", "w_out", "mlp_norm", "w_ff1", "w_ff2", "final_norm")
    return (loss, dh[None]) + tuple(leaf(i, k) for i in range(4) for k in order)
```

```python
import functools
import math

import jax
import jax.numpy as jnp
from jax import lax
from jax.experimental import pallas as pl
from jax.experimental.pallas import tpu as pltpu
from jax.experimental.pallas import tpu_sc as plsc

f32, bf16 = jnp.float32, jnp.bfloat16

D = 2048
NDEV = 8
NL = 2
HD = 128
NH = 6
WH = NH * HD
W_POOL = 512
EPS = 1e-6
N_IN = 12044
NP = 12288
OFF_SB, OFF_GQ, OFF_Z, OFF_P, OFF_AB, OFF_G = 0, 2304, 4608, 5376, 6016, 6144
AB_LANE = HD - 2 * NH
POOL_WINDOWS = (2, 4, 8, 16)
CH = 128
TQ = 256
VMEM_LIMIT = 56 * 1024 * 1024
ADAM_LR, ADAM_B1, ADAM_B2, ADAM_EPS, ADAM_WD, ADAM_STEP = 0.001, 0.9, 0.999, 1e-08, 0.01, 10
MESH = pl.DeviceIdType.MESH


def _cp(sem=None):
    return pltpu.CompilerParams(dimension_semantics=sem, vmem_limit_bytes=VMEM_LIMIT)


def _sds(shape, dtype):
    return jax.ShapeDtypeStruct(tuple(shape), dtype)


def matmul(a, b, *, name, ta=False, tb=False, out_dtype=f32, tm=1024, tn=1024, tk=2048, epilogue=None, extras=(), after=()):
    m, k = (a.shape[1], a.shape[0]) if ta else a.shape
    n = b.shape[0] if tb else b.shape[1]
    assert k == (b.shape[1] if tb else b.shape[0]) and a.dtype == bf16 and b.dtype == bf16
    tm, tn, tk = min(tm, m), min(tn, n), min(tk, k)
    assert m % tm == 0 and n % tn == 0 and k % tk == 0, (m, n, k, tm, tn, tk)
    nk = k // tk
    a_spec = pl.BlockSpec((tk, tm), lambda i, j, q: (q, i)) if ta else pl.BlockSpec((tm, tk), lambda i, j, q: (i, q))
    b_spec = pl.BlockSpec((tn, tk), lambda i, j, q: (j, q)) if tb else pl.BlockSpec((tk, tn), lambda i, j, q: (q, j))
    e_specs = [pl.BlockSpec((tm, tn), lambda i, j, q: (i, j)) for _ in extras]
    dn = (((0 if ta else 1,), (1 if tb else 0,)), ((), ()))
    ne = len(extras)

    def body(*refs):
        a_ref, b_ref, *rest = refs[len(after):]
        e_refs, o_ref = rest[:ne], rest[ne]
        part = lax.dot_general(a_ref[...], b_ref[...], dn, preferred_element_type=f32)

        def finish(acc):
            if epilogue is not None:
                acc = epilogue(acc, *[e[...] for e in e_refs])
            o_ref[...] = acc.astype(out_dtype)

        if nk == 1:
            finish(part)
        else:
            acc_ref = rest[ne + 1]
            q = pl.program_id(2)

            @pl.when(q == 0)
            def _():
                acc_ref[...] = part

            @pl.when(jnp.logical_and(q > 0, q < nk - 1))
            def _():
                acc_ref[...] += part

            @pl.when(q == nk - 1)
            def _():
                finish(acc_ref[...] + part)

    return pl.pallas_call(
        body, name=name, grid=(m // tm, n // tn, nk),
        in_specs=[pl.BlockSpec(memory_space=pl.ANY)] * len(after) + [a_spec, b_spec] + e_specs,
        out_specs=pl.BlockSpec((tm, tn), lambda i, j, q: (i, j)), out_shape=_sds((m, n), out_dtype),
        scratch_shapes=[pltpu.VMEM((tm, tn), f32)] if nk > 1 else [],
        compiler_params=_cp(("parallel", "parallel", "arbitrary")),
    )(*after, a, b, *extras)


def rowwise(name, fn, rows, params, outs, sums=(), tr=256, after=()):
    s = rows[0][0].shape[0]
    tr = min(tr, s)
    nin, nout = len(rows) + len(params), len(outs)
    in_specs = [pl.BlockSpec((tr, w), functools.partial(lambda i, c: (i, c), c=c)) for (_, w, c) in rows]
    in_specs += [pl.BlockSpec(p.shape, lambda i: (0, 0)) for p in params]
    out_specs = [pl.BlockSpec((tr, w), lambda i: (i, 0)) for (w, _) in outs]
    out_specs += [pl.BlockSpec(sh, lambda i: (0, 0)) for sh in sums]
    out_shape = [_sds((s, w), dt) for (w, dt) in outs] + [_sds(sh, f32) for sh in sums]

    def body(*refs):
        refs = refs[len(after):]
        res = fn(*[r[...] for r in refs[:nin]])
        for r, v in zip(refs[nin:nin + nout], res[:nout]):
            r[...] = v.astype(r.dtype)
        i = pl.program_id(0)
        for r, v in zip(refs[nin + nout:], res[nout:]):
            @pl.when(i == 0)
            def _(r=r, v=v):
                r[...] = v

            @pl.when(i > 0)
            def _(r=r, v=v):
                r[...] += v

    res = pl.pallas_call(
        body, name=name, grid=(s // tr,), in_specs=[pl.BlockSpec(memory_space=pl.ANY)] * len(after) + in_specs,
        out_specs=out_specs, out_shape=out_shape, compiler_params=_cp(("arbitrary",)),
    )(*after, *[r[0] for r in rows], *params)
    return res


def _rms(x, g):
    return x * lax.rsqrt(jnp.mean(x * x, axis=-1, keepdims=True) + EPS) * g


def rms_fwd(x, g, name):
    return rowwise(name, lambda xb, gb: (_rms(xb, gb),), [(x, D, 0)], [g], [(D, bf16)])[0]


def rms_bwd(x, du, dres, g, name, after=()):
    def fn(xb, dub, drb, gb):
        _, vjp = jax.vjp(_rms, xb, gb)
        dx, dg = vjp(dub.astype(f32))
        return drb + dx, drb + dx, dg

    return rowwise(name, fn, [(x, D, 0), (du, D, 0), (dres, D, 0)], [g], [(D, f32), (D, bf16)], [(1, D)], after=after)


def _merge(gates, up_p, up_s, up_g):
    sg = jax.nn.sigmoid(gates)
    return sg[:, :D] * up_p + sg[:, D:2 * D] * up_s + sg[:, 2 * D:] * up_g


def merge_fwd(proj, ups, name):
    return rowwise(name, lambda g, a, b, c: (_merge(g, a, b, c),),
                   [(proj, 3 * D, OFF_G // (3 * D))] + [(u, D, 0) for u in ups], [], [(D, bf16)], tr=128)[0]


def merge_bwd(proj, ups, dmerged, name):
    def fn(g, a, b, c, dm):
        _, vjp = jax.vjp(_merge, g, a, b, c)
        return vjp(dm.astype(f32))

    return rowwise(name, fn, [(proj, 3 * D, OFF_G // (3 * D))] + [(u, D, 0) for u in ups] + [(dmerged, D, 0)], [],
                   [(3 * D, bf16), (D, bf16), (D, bf16), (D, bf16)], tr=128)


def _gdn_out(o, z, g):
    ys = []
    for h in range(NH):
        sl = slice(h * HD, (h + 1) * HD)
        ys.append(_rms(o[:, sl], g) * jax.nn.silu(z[:, sl]))
    return jnp.concatenate(ys, axis=1)


def gdn_out_fwd(o, proj, g, name):
    return rowwise(name, lambda ob, zb, gb: (_gdn_out(ob, zb, gb),), [(o, WH, 0), (proj, WH, OFF_Z // WH)], [g],
                   [(WH, bf16)])[0]


def gdn_out_bwd(o, proj, g, dy, name):
    def fn(ob, zb, dyb, gb):
        _, vjp = jax.vjp(_gdn_out, ob, zb, gb)
        return vjp(dyb.astype(f32))

    return rowwise(name, fn, [(o, WH, 0), (proj, WH, OFF_Z // WH), (dy, WH, 0)], [g], [(WH, f32), (WH, bf16)],
                   [(1, HD)])


def loss_head(x, target, g, name):
    def loss_fn(xb, gb, tb):
        err = _rms(xb, gb) - tb
        return (0.5 / D) * jnp.sum(jnp.sum(err * err, axis=1, keepdims=True), axis=0, keepdims=True)

    def fn(xb, tb, gb):
        val, vjp = jax.vjp(functools.partial(loss_fn, tb=tb), xb, gb)
        dx, dg = vjp(jnp.ones((1, 1), f32))
        return dx, dx, jnp.broadcast_to(val, (1, HD)), dg

    return rowwise(name, fn, [(x, D, 0), (target, D, 0)], [g], [(D, f32), (D, bf16)], [(1, HD), (1, D)])


PB = 256


def _split(v):
    hi = v.astype(bf16)
    return hi, (v - hi.astype(f32)).astype(bf16)


def _band_dot(make_band, v, s, forward):
    hi, lo = _split(v)
    nb = s // PB
    outs = []
    for r in range(nb):
        lo_r = max(r - 1, 0) if forward else r
        hi_r = r + 1 if forward else min(r + 2, nb)
        band = make_band(r * PB, lo_r * PB, (hi_r - lo_r) * PB)
        sl = slice(lo_r * PB, hi_r * PB)
        outs.append(jnp.dot(band, hi[sl], preferred_element_type=f32) + jnp.dot(band, lo[sl], preferred_element_type=f32))
    return jnp.concatenate(outs, axis=0)


def _pool_common(p, win, s):
    def band(row0, col0, ncol):
        t = row0 + lax.broadcasted_iota(jnp.int32, (PB, ncol), 0)
        u = col0 + lax.broadcasted_iota(jnp.int32, (PB, ncol), 1)
        return jnp.logical_and(u <= t, t < u + win).astype(bf16)

    def band_t(row0, col0, ncol):
        u = row0 + lax.broadcasted_iota(jnp.int32, (PB, ncol), 0)
        t = col0 + lax.broadcasted_iota(jnp.int32, (PB, ncol), 1)
        return jnp.logical_and(u <= t, t < u + win).astype(bf16)

    t = lax.broadcasted_iota(jnp.int32, (s, 1), 0)
    inv_n = 1.0 / jnp.minimum(t + 1, win).astype(f32)
    d = _band_dot(band, p, s, True) * inv_n - p
    return d, inv_n, band_t


def pool_fwd(proj, pool_w, pool_scale, name):
    s = proj.shape[0]

    def body(p_ref, w_ref, sc_ref, y_ref):
        win = jnp.left_shift(2, pl.program_id(0))
        d, _, _ = _pool_common(p_ref[...], win, s)
        y = jnp.dot(d.astype(bf16), w_ref[...].astype(bf16), preferred_element_type=f32) * sc_ref[...]
        y_ref[...] = y.astype(bf16)

    return pl.pallas_call(
        body, name=name, grid=(4,),
        in_specs=[pl.BlockSpec((s, HD), lambda g: (0, OFF_P // HD + g)), pl.BlockSpec((None, HD, HD), lambda g: (g, 0, 0)),
                  pl.BlockSpec((1, HD), lambda g: (0, g))],
        out_specs=pl.BlockSpec((s, HD), lambda g: (0, g)), out_shape=_sds((s, W_POOL), bf16),
        compiler_params=_cp(("arbitrary",)),
    )(proj, pool_w, pool_scale)


def pool_bwd(proj, pool_w, pool_scale, dy, name):
    s = proj.shape[0]

    def body(p_ref, w_ref, sc_ref, dy_ref, dp_ref, dw_ref, dsc_ref):
        win = jnp.left_shift(2, pl.program_id(0))
        d, inv_n, band_t = _pool_common(p_ref[...], win, s)
        w = w_ref[...].astype(bf16)
        dyf = dy_ref[...].astype(f32)
        dsc_ref[...] = jnp.sum(dyf * jnp.dot(d.astype(bf16), w, preferred_element_type=f32), axis=0, keepdims=True)
        dys = (dyf * sc_ref[...]).astype(bf16)
        dd = lax.dot_general(dys, w, (((1,), (1,)), ((), ())), preferred_element_type=f32)
        dw_ref[...] = lax.dot_general(d.astype(bf16), dys, (((0,), (0,)), ((), ())), preferred_element_type=f32)
        dp_ref[...] = (_band_dot(band_t, dd * inv_n, s, False) - dd).astype(bf16)

    return pl.pallas_call(
        body, name=name, grid=(4,),
        in_specs=[pl.BlockSpec((s, HD), lambda g: (0, OFF_P // HD + g)), pl.BlockSpec((None, HD, HD), lambda g: (g, 0, 0)),
                  pl.BlockSpec((1, HD), lambda g: (0, g)), pl.BlockSpec((s, HD), lambda g: (0, g))],
        out_specs=[pl.BlockSpec((s, HD), lambda g: (0, g)), pl.BlockSpec((None, HD, HD), lambda g: (g, 0, 0)),
                   pl.BlockSpec((1, HD), lambda g: (0, g))],
        out_shape=[_sds((s, W_POOL), bf16), _sds((4, HD, HD), f32), _sds((1, W_POOL), f32)],
        compiler_params=_cp(("arbitrary",)),
    )(proj, pool_w, pool_scale, dy)


HB = 3
WB = HB * HD


def _heads_of(v):
    return _stack([v[:, h * HD:(h + 1) * HD] for h in range(HB)])


def _bd(a, b, dn):
    return lax.dot_general(a, b, dn, preferred_element_type=f32)


def _run_sum(v, tri):
    hi, lo = _split(v.reshape(HB * TQ, TQ))
    return (jnp.dot(hi, tri, preferred_element_type=f32) + jnp.dot(lo, tri, preferred_element_type=f32)).reshape(HB, TQ, TQ)


def _tri(later):
    j = lax.broadcasted_iota(jnp.int32, (TQ, TQ), 0)
    u = lax.broadcasted_iota(jnp.int32, (TQ, TQ), 1)
    return (j > u if later else j < u).astype(bf16)


def _sb_tile(q, k_ref, kb, carry, diagonal):
    k = _heads_of(k_ref[pl.ds(pl.multiple_of(kb * TQ, TQ), TQ), :].astype(bf16))
    z = _bd(q, k, BNT)
    ls = jax.nn.log_sigmoid(-z)
    if diagonal:
        mask = lax.broadcasted_iota(jnp.int32, (TQ, TQ), 1) < lax.broadcasted_iota(jnp.int32, (TQ, TQ), 0)
        ls = jnp.where(mask, ls, 0.0)
    a = jnp.exp(ls + z + _run_sum(ls, _tri(True)) + carry)
    return z, ls, jnp.where(mask, a, 0.0) if diagonal else a


def sb_fwd(proj, name):
    s = proj.shape[0]
    nq = s // TQ
    scale = HD ** -0.5

    def body(q_ref, k_ref, v_ref, y_ref):
        qi = pl.program_id(1)
        q = _heads_of((q_ref[...] * scale).astype(bf16))

        def step(j, c, diagonal=False):
            acc, carry = c
            kb = qi - j
            _, ls, a = _sb_tile(q, k_ref, kb, carry, diagonal)
            v = _heads_of(v_ref[pl.ds(pl.multiple_of(kb * TQ, TQ), TQ), :].astype(bf16))
            return acc + _bd(a.astype(bf16), v, BNN), carry + jnp.sum(ls, axis=2, keepdims=True)

        first = step(0, (jnp.zeros((HB, TQ, HD), f32), jnp.zeros((HB, TQ, 1), f32)), True)
        acc, _ = lax.fori_loop(1, qi + 1, step, first)
        for h in range(HB):
            y_ref[:, h * HD:(h + 1) * HD] = acc[h].astype(bf16)

    c0, ng = OFF_SB // WB, NH // HB
    return pl.pallas_call(
        body, name=name, grid=(ng, nq),
        in_specs=[pl.BlockSpec((TQ, WB), lambda g, i: (i, c0 + g)), pl.BlockSpec((s, WB), lambda g, i: (0, c0 + ng + g)),
                  pl.BlockSpec((s, WB), lambda g, i: (0, c0 + 2 * ng + g))],
        out_specs=pl.BlockSpec((TQ, WB), lambda g, i: (i, g)), out_shape=_sds((s, WH), bf16),
        compiler_params=_cp(("arbitrary", "arbitrary")),
    )(proj, proj, proj)


def sb_bwd(proj, dy, name):
    s = proj.shape[0]
    nq = s // TQ
    scale = HD ** -0.5

    def body(q_ref, k_ref, v_ref, do_ref, dq_ref, dk_ref, dv_ref, e_scr, z_scr, dk_acc, dv_acc):
        qi = pl.program_id(1)
        q = _heads_of((q_ref[...] * scale).astype(bf16))
        do = _heads_of(do_ref[...])

        @pl.when(qi == 0)
        def _():
            dk_acc[...] = jnp.zeros_like(dk_acc)
            dv_acc[...] = jnp.zeros_like(dv_acc)

        def add_heads(acc_ref, rows, upd):
            for h in range(HB):
                acc_ref[rows, h * HD:(h + 1) * HD] += upd[h]

        def sweep_back(j, carry, diagonal=False):
            kb = qi - j
            rows = pl.ds(pl.multiple_of(kb * TQ, TQ), TQ)
            z, ls, a = _sb_tile(q, k_ref, kb, carry, diagonal)
            v = _heads_of(v_ref[rows, :].astype(bf16))
            e_scr[kb] = _bd(do, v, BNT) * a
            z_scr[kb] = z
            add_heads(dv_acc, rows, _bd(a.astype(bf16), do, BTN))
            return carry + jnp.sum(ls, axis=2, keepdims=True)

        lax.fori_loop(1, qi + 1, sweep_back, sweep_back(0, jnp.zeros((HB, TQ, 1), f32), True))

        def sweep_fwd(kb, c, diagonal=False):
            dq, carry = c
            rows = pl.ds(pl.multiple_of(kb * TQ, TQ), TQ)
            e, z = e_scr[kb], z_scr[kb]
            sig = jax.nn.sigmoid(z)
            dz = e * (1.0 - sig) - (_run_sum(e, _tri(False)) + carry) * sig
            if diagonal:
                dz = jnp.where(lax.broadcasted_iota(jnp.int32, (TQ, TQ), 1) < lax.broadcasted_iota(jnp.int32, (TQ, TQ), 0), dz, 0.0)
            dz = dz.astype(bf16)
            k = _heads_of(k_ref[rows, :].astype(bf16))
            add_heads(dk_acc, rows, _bd(dz, q, BTN))
            return dq + _bd(dz, k, BNN), carry + jnp.sum(e, axis=2, keepdims=True)

        dq, _ = sweep_fwd(qi, lax.fori_loop(0, qi, sweep_fwd, (jnp.zeros((HB, TQ, HD), f32), jnp.zeros((HB, TQ, 1), f32))), True)
        for h in range(HB):
            dq_ref[:, h * HD:(h + 1) * HD] = (dq[h] * scale).astype(bf16)

        @pl.when(qi == nq - 1)
        def _():
            dk_ref[...] = dk_acc[...].astype(bf16)
            dv_ref[...] = dv_acc[...].astype(bf16)

    c0, ng = OFF_SB // WB, NH // HB
    return pl.pallas_call(
        body, name=name, grid=(ng, nq),
        in_specs=[pl.BlockSpec((TQ, WB), lambda g, i: (i, c0 + g)), pl.BlockSpec((s, WB), lambda g, i: (0, c0 + ng + g)),
                  pl.BlockSpec((s, WB), lambda g, i: (0, c0 + 2 * ng + g)), pl.BlockSpec((TQ, WB), lambda g, i: (i, g))],
        out_specs=[pl.BlockSpec((TQ, WB), lambda g, i: (i, g)), pl.BlockSpec((s, WB), lambda g, i: (0, g)),
                   pl.BlockSpec((s, WB), lambda g, i: (0, g))],
        out_shape=[_sds((s, WH), bf16)] * 3,
        scratch_shapes=[pltpu.VMEM((nq, HB, TQ, TQ), f32), pltpu.VMEM((nq, HB, TQ, TQ), f32), pltpu.VMEM((s, WB), f32),
                        pltpu.VMEM((s, WB), f32)],
        compiler_params=_cp(("arbitrary", "arbitrary")),
    )(proj, proj, proj, dy)


CB = 256


def _shift_down(v, k, s):
    if k == 0:
        return v
    row = lax.broadcasted_iota(jnp.int32, v.shape, 0)
    return jnp.where(row < k, 0.0, pltpu.roll(v, k, axis=0))


def _shift_up(v, k, s):
    if k == 0:
        return v
    row = lax.broadcasted_iota(jnp.int32, v.shape, 0)
    return jnp.where(row >= s - k, 0.0, pltpu.roll(v, s - k, axis=0))


def conv_fwd(proj, w, name):
    s = proj.shape[0]

    def body(x_ref, w_ref, y_ref):
        x, wv = x_ref[...], w_ref[...]
        y = sum(wv[3 - k:4 - k, :] * _shift_down(x, k, s) for k in range(4))
        y_ref[...] = jax.nn.silu(y)

    return pl.pallas_call(
        body, name=name, grid=(3 * WH // CB,),
        in_specs=[pl.BlockSpec((s, CB), lambda j: (0, OFF_GQ // CB + j)), pl.BlockSpec((4, CB), lambda j: (0, j))],
        out_specs=pl.BlockSpec((s, CB), lambda j: (0, j)), out_shape=_sds((s, 3 * WH), f32),
        compiler_params=_cp(("parallel",)),
    )(proj, w)


def conv_bwd(proj, w, dc, name, after=()):
    s = proj.shape[0]

    def body(*refs):
        x_ref, w_ref, dc_ref, dx_ref, dw_ref = refs[len(after):]
        x, wv = x_ref[...], w_ref[...]
        xs = [_shift_down(x, k, s) for k in range(4)]
        y = sum(wv[3 - k:4 - k, :] * xs[k] for k in range(4))
        sig = jax.nn.sigmoid(y)
        dy = dc_ref[...] * (sig * (1.0 + y * (1.0 - sig)))
        dx_ref[...] = sum(wv[3 - k:4 - k, :] * _shift_up(dy, k, s) for k in range(4)).astype(bf16)
        dw_ref[...] = jnp.concatenate([jnp.sum(dy * xs[3 - i], axis=0, keepdims=True) for i in range(4)], axis=0)

    return pl.pallas_call(
        body, name=name, grid=(3 * WH // CB,),
        in_specs=[pl.BlockSpec(memory_space=pl.ANY)] * len(after)
        + [pl.BlockSpec((s, CB), lambda j: (0, OFF_GQ // CB + j)), pl.BlockSpec((4, CB), lambda j: (0, j)),
           pl.BlockSpec((s, CB), lambda j: (0, j))],
        out_specs=[pl.BlockSpec((s, CB), lambda j: (0, j)), pl.BlockSpec((4, CB), lambda j: (0, j))],
        out_shape=[_sds((s, 3 * WH), bf16), _sds((4, 3 * WH), f32)],
        compiler_params=_cp(("parallel",)),
    )(*after, proj, w, dc)


SOLVE_PASSES = 3


def _pdot_impl(a, b, dn, passes):
    ah, al = _split(a)
    bh, bl = _split(b)
    dot = lambda p, q: lax.dot_general(p, q, dn, preferred_element_type=f32)
    if passes == 1:
        return dot(ah, bh)
    if passes == 2:
        return dot(ah, bh) + dot(ah, bl)
    return dot(ah, bh) + (dot(ah, bl) + dot(al, bh))


BNN, BNT, BTN = (((2,), (1,)), ((0,), (0,))), (((2,), (2,)), ((0,), (0,))), (((1,), (1,)), ((0,), (0,)))


@functools.partial(jax.custom_vjp, nondiff_argnums=(2,))
def _bdot(a, b, passes):
    return _pdot_impl(a, b, BNN, passes)


def _bdot_fwd(a, b, passes):
    return _pdot_impl(a, b, BNN, passes), (a, b)


def _bdot_bwd(passes, res, ct):
    a, b = res
    return _pdot_impl(ct, b, BNT, passes), _pdot_impl(a, ct, BTN, passes)


_bdot.defvjp(_bdot_fwd, _bdot_bwd)


@functools.partial(jax.custom_vjp, nondiff_argnums=(2,))
def _bdot_nt(a, b, passes):
    return _pdot_impl(a, b, BNT, passes)


def _bdot_nt_fwd(a, b, passes):
    return _pdot_impl(a, b, BNT, passes), (a, b)


def _bdot_nt_bwd(passes, res, ct):
    a, b = res
    return _pdot_impl(ct, b, BNN, passes), _pdot_impl(ct, a, BTN, passes)


_bdot_nt.defvjp(_bdot_nt_fwd, _bdot_nt_bwd)


def _lane_pick(v, h):
    lane = lax.broadcasted_iota(jnp.int32, v.shape, v.ndim - 1)
    return jnp.sum(jnp.where(lane == h, v, 0.0), axis=-1, keepdims=True)


def _stack(parts):
    return jnp.concatenate([p[None] for p in parts], axis=0)


def _heads(v, first):
    return _stack([_lane_pick(v, first + h) for h in range(NH)])


def _l2n(v):
    return v * lax.rsqrt(jnp.sum(v * v, axis=-1, keepdims=True) + EPS)


def _dot_nt(a, b):
    return lax.dot_general(a, b, (((1,), (1,)), ((), ())), preferred_element_type=f32)


def _inverse_impl(lower):
    i = lax.broadcasted_iota(jnp.int32, (CH, CH), 0)
    j = lax.broadcasted_iota(jnp.int32, (CH, CH), 1)
    inv = (i == j).astype(f32) - lower
    pw = _pdot_impl(lower, lower, BNN, SOLVE_PASSES)
    for m in range(1, int(math.log2(CH))):
        inv = inv + _pdot_impl(inv, pw, BNN, SOLVE_PASSES)
        if m < int(math.log2(CH)) - 1:
            pw = _pdot_impl(pw, pw, BNN, SOLVE_PASSES)
    return inv


@jax.custom_vjp
def _unit_lower_inverse(lower):
    return _inverse_impl(lower)


def _unit_lower_inverse_fwd(lower):
    inv = _inverse_impl(lower)
    return inv, inv


def _unit_lower_inverse_bwd(inv, ct):
    return (-_pdot_impl(_pdot_impl(inv, ct, BTN, SOLVE_PASSES), inv, BNT, SOLVE_PASSES),)


_unit_lower_inverse.defvjp(_unit_lower_inverse_fwd, _unit_lower_inverse_bwd)


def _gdn_chunk(cq, ck, cv, ab, alog, dtb):
    ones = jnp.ones((NH, CH, HD), f32)
    q = _l2n(cq) * (HD ** -0.5)
    k = _l2n(ck)
    la = -jnp.exp(_heads(alog, 0)) * jax.nn.softplus(_heads(ab, AB_LANE) + _heads(dtb, 0))
    beta = jax.nn.sigmoid(_heads(ab, AB_LANE + NH)) * ones
    i = lax.broadcasted_iota(jnp.int32, (CH, CH), 0)
    j = lax.broadcasted_iota(jnp.int32, (CH, CH), 1)
    incl, strict = j <= i, j < i
    g = _bdot(jnp.broadcast_to(incl.astype(f32), (NH, CH, CH)), la * ones, 2)
    g_row = _stack([g[h].T for h in range(NH)])
    gamma = jnp.where(incl, jnp.exp(jnp.where(incl, g - g_row, 0.0)), 0.0)
    lower = jnp.where(strict, beta * _bdot_nt(k, k, 1) * gamma, 0.0)
    inv = _unit_lower_inverse(lower)
    eg = jnp.exp(g)
    u = _bdot(inv, cv * beta, SOLVE_PASSES)
    w = _bdot(inv, k * (beta * eg), SOLVE_PASSES)
    qk = _bdot_nt(q, k, 1) * gamma
    g_last = g[:, CH - 1:CH, :]
    return u, w, q * eg, k * jnp.exp(g_last - g), qk, jnp.exp(g_last)


def _by_head(ref, t=0):
    return _stack([ref[:, t * WH + h * HD:t * WH + (h + 1) * HD] for h in range(NH)])


def gdn_a_fwd(c, proj, alog, dtb, name):
    s = c.shape[0]
    nc = s // CH

    def body(c_ref, ab_ref, al_ref, dt_ref, u_ref, w_ref, qd_ref, kd_ref, qk_ref, dec_ref):
        res = _gdn_chunk(_by_head(c_ref, 0), _by_head(c_ref, 1), _by_head(c_ref, 2), ab_ref[...], al_ref[...], dt_ref[...])
        for h in range(NH):
            sl = slice(h * HD, (h + 1) * HD)
            for r, v in zip((u_ref, w_ref, qd_ref, kd_ref, qk_ref), res[:5]):
                r[:, sl] = v[h]
            dec_ref[:, sl] = jnp.broadcast_to(res[5][h], (8, HD))

    row = pl.BlockSpec((CH, WH), lambda n: (n, 0))
    par = pl.BlockSpec((1, HD), lambda n: (0, 0))
    return pl.pallas_call(
        body, name=name, grid=(nc,),
        in_specs=[pl.BlockSpec((CH, 3 * WH), lambda n: (n, 0)), pl.BlockSpec((CH, HD), lambda n: (n, OFF_AB // HD)), par, par],
        out_specs=[row] * 5 + [pl.BlockSpec((None, 8, WH), lambda n: (n, 0, 0))],
        out_shape=[_sds((s, WH), f32)] * 5 + [_sds((nc, 8, WH), f32)],
        compiler_params=_cp(("parallel",)),
    )(c, proj, alog, dtb)


def gdn_a_bwd(c, proj, alog, dtb, cots, ddec, name):
    s = c.shape[0]
    nc = s // CH

    def body(c_ref, ab_ref, al_ref, dt_ref, du_ref, dw_ref, dqd_ref, dkd_ref, dqk_ref, ddec_ref,
             dc_ref, dab_ref, dal_ref, ddt_ref):
        n = pl.program_id(0)
        _, vjp = jax.vjp(_gdn_chunk, _by_head(c_ref, 0), _by_head(c_ref, 1), _by_head(c_ref, 2), ab_ref[...], al_ref[...],
                         dt_ref[...])
        lane = lax.broadcasted_iota(jnp.int32, (1, HD), 1)
        dd = _stack([jnp.where(lane == 0, ddec_ref[0:1, h * HD:(h + 1) * HD], 0.0) for h in range(NH)])
        dcq, dck, dcv, dab, dal, ddt = vjp(tuple(_by_head(r) for r in (du_ref, dw_ref, dqd_ref, dkd_ref, dqk_ref)) + (dd,))
        for h in range(NH):
            for t, v in enumerate((dcq, dck, dcv)):
                dc_ref[:, t * WH + h * HD:t * WH + (h + 1) * HD] = v[h]
        dab_ref[...] = dab.astype(bf16)

        @pl.when(n == 0)
        def _():
            dal_ref[...] = dal
            ddt_ref[...] = ddt

        @pl.when(n > 0)
        def _():
            dal_ref[...] += dal
            ddt_ref[...] += ddt

    row = pl.BlockSpec((CH, WH), lambda n: (n, 0))
    wide = pl.BlockSpec((CH, 3 * WH), lambda n: (n, 0))
    par = pl.BlockSpec((1, HD), lambda n: (0, 0))
    return pl.pallas_call(
        body, name=name, grid=(nc,),
        in_specs=[wide, pl.BlockSpec((CH, HD), lambda n: (n, OFF_AB // HD)), par, par] + [row] * 5
        + [pl.BlockSpec((None, 8, WH), lambda n: (n, 0, 0))],
        out_specs=[wide, pl.BlockSpec((CH, HD), lambda n: (n, 0)), par, par],
        out_shape=[_sds((s, 3 * WH), f32), _sds((s, HD), bf16), _sds((1, HD), f32), _sds((1, HD), f32)],
        compiler_params=_cp(("arbitrary",)),
    )(c, proj, alog, dtb, *cots, ddec)


def gdn_b_fwd(u, w, qd, kd, qk, dec, name):
    s = u.shape[0]
    nc = s // CH

    def body(u_ref, w_ref, qd_ref, kd_ref, qk_ref, dec_ref, o_ref, st_ref, state):
        n = pl.program_id(0)

        @pl.when(n == 0)
        def _():
            state[...] = jnp.zeros_like(state)

        for h in range(NH):
            sl = slice(h * HD, (h + 1) * HD)
            st = state[sl, :]
            st_ref[sl, :] = st
            sb = st.astype(bf16)
            vn = u_ref[:, sl] - jnp.dot(w_ref[:, sl].astype(bf16), sb, preferred_element_type=f32)
            vb = vn.astype(bf16)
            o_ref[:, sl] = (jnp.dot(qd_ref[:, sl].astype(bf16), sb, preferred_element_type=f32)
                            + jnp.dot(qk_ref[:, sl].astype(bf16), vb, preferred_element_type=f32))
            state[sl, :] = st * dec_ref[0:1, sl] + lax.dot_general(
                kd_ref[:, sl].astype(bf16), vb, (((0,), (0,)), ((), ())), preferred_element_type=f32)

    row = pl.BlockSpec((CH, WH), lambda n: (n, 0))
    return pl.pallas_call(
        body, name=name, grid=(nc,),
        in_specs=[row] * 5 + [pl.BlockSpec((None, 8, WH), lambda n: (n, 0, 0))],
        out_specs=[row, pl.BlockSpec((None, WH, HD), lambda n: (n, 0, 0))],
        out_shape=[_sds((s, WH), f32), _sds((nc, WH, HD), f32)],
        scratch_shapes=[pltpu.VMEM((WH, HD), f32)],
        compiler_params=_cp(("arbitrary",)),
    )(u, w, qd, kd, qk, dec)


def gdn_b_bwd(u, w, qd, kd, qk, dec, states, do, name):
    s = u.shape[0]
    nc = s // CH

    def body(u_ref, w_ref, qd_ref, kd_ref, qk_ref, dec_ref, st_ref, do_ref,
             du_ref, dw_ref, dqd_ref, dkd_ref, dqk_ref, ddec_ref, dstate):
        n = pl.program_id(0)

        @pl.when(n == 0)
        def _():
            dstate[...] = jnp.zeros_like(dstate)

        for h in range(NH):
            sl = slice(h * HD, (h + 1) * HD)
            st, ds = st_ref[sl, :], dstate[sl, :]
            sb, dsb = st.astype(bf16), ds.astype(bf16)
            wb, qdb, kdb, qkb = (r[:, sl].astype(bf16) for r in (w_ref, qd_ref, kd_ref, qk_ref))
            dob = do_ref[:, sl].astype(bf16)
            vn = u_ref[:, sl] - jnp.dot(wb, sb, preferred_element_type=f32)
            vb = vn.astype(bf16)
            dvn = (lax.dot_general(qkb, dob, (((0,), (0,)), ((), ())), preferred_element_type=f32)
                   + jnp.dot(kdb, dsb, preferred_element_type=f32))
            dvb = dvn.astype(bf16)
            du_ref[:, sl] = dvn
            dw_ref[:, sl] = -_dot_nt(dvb, sb)
            dqd_ref[:, sl] = _dot_nt(dob, sb)
            dkd_ref[:, sl] = _dot_nt(vb, dsb)
            dqk_ref[:, sl] = _dot_nt(dob, vb)
            tot = jnp.sum(jnp.sum(ds * st, axis=1, keepdims=True), axis=0, keepdims=True)
            ddec_ref[:, sl] = jnp.broadcast_to(tot, (8, HD))
            dstate[sl, :] = (ds * dec_ref[0:1, sl]
                             + lax.dot_general(qdb, dob, (((0,), (0,)), ((), ())), preferred_element_type=f32)
                             - lax.dot_general(wb, dvb, (((0,), (0,)), ((), ())), preferred_element_type=f32))

    row = pl.BlockSpec((CH, WH), lambda n: (nc - 1 - n, 0))
    small = pl.BlockSpec((None, 8, WH), lambda n: (nc - 1 - n, 0, 0))
    return pl.pallas_call(
        body, name=name, grid=(nc,),
        in_specs=[row] * 5 + [small, pl.BlockSpec((None, WH, HD), lambda n: (nc - 1 - n, 0, 0)), row],
        out_specs=[row] * 5 + [small],
        out_shape=[_sds((s, WH), f32)] * 5 + [_sds((nc, 8, WH), f32)],
        scratch_shapes=[pltpu.VMEM((WH, HD), f32)],
        compiler_params=_cp(("arbitrary",)),
    )(u, w, qd, kd, qk, dec, states, do)


ANY = pl.BlockSpec(memory_space=pl.ANY)


def _dev_index(p):
    return 4 * p[0] + 2 * p[1] + p[2]


def _shard_of(ref, axis, size, idx):
    return ref.at[pl.ds(idx * size, size), :] if axis == 0 else ref.at[:, pl.ds(idx * size, size)]


def _peer(x, y, c, r):
    return (1 - x if r & 4 else x, 1 - y if r & 2 else y, 1 - c if r & 1 else c)


ALL_PEERS, SIBLING, OTHER_CHIPS = tuple(range(1, NDEV)), (1,), (4, 2, 6)


def _launch(body, ins, out_sds, name, sequencer_id, relations=ALL_PEERS, kinds=7):
    n = len(ins)
    sems = [pltpu.SemaphoreType.DMA((n, kinds)), pltpu.SemaphoreType.DMA((n, kinds)), pltpu.SemaphoreType.DMA((n,))]
    if sequencer_id is None:
        return pl.pallas_call(
            lambda *refs: body(refs[:n], refs[n:2 * n], *refs[2 * n:]), name=name, in_specs=[ANY] * n, out_specs=[ANY] * n,
            out_shape=out_sds, scratch_shapes=sems, compiler_params=pltpu.CompilerParams(has_side_effects=True),
        )(*ins)
    in_refs = [jax.new_ref(a, memory_space=pltpu.MemorySpace.HBM) for a in ins]
    out_refs = [jax.empty_ref(sd, memory_space=pltpu.MemorySpace.HBM) for sd in out_sds]

    @pl.kernel(mesh=plsc.ScalarSubcoreMesh(axis_name="sequencer", num_cores=1), name=name, scratch_types=sems,
               compiler_params=pltpu.CompilerParams(collective_id=sequencer_id))
    def launch(send, recv, loc):
        x, y, c = lax.axis_index("x"), lax.axis_index("y"), lax.axis_index("c")
        barrier = pltpu.get_barrier_semaphore()
        for r in relations:
            pl.semaphore_signal(barrier, inc=1, device_id=_peer(x, y, c, r), device_id_type=MESH)
        pl.semaphore_wait(barrier, len(relations))
        body(in_refs, out_refs, send, recv, loc)

    launch()
    return [r[...] for r in out_refs]


def all_gather(xs, axes, name, sequencer_id=None):
    n = len(xs)
    fulls = [tuple(d * (NDEV if a == ax else 1) for a, d in enumerate(x.shape)) for x, ax in zip(xs, axes)]
    halved = [x.shape[0] % 32 == 0 for x in xs]

    def body(x_refs, o_refs, send, recv, loc):
        x, y, c = lax.axis_index("x"), lax.axis_index("y"), lax.axis_index("c")
        me, sib = (x, y, c), (x, y, 1 - c)
        xn, yn, dg = (1 - x, y), (x, 1 - y), (1 - x, 1 - y)

        def part(t, p, half=None):
            ref = _shard_of(o_refs[t], axes[t], xs[t].shape[axes[t]], _dev_index(p))
            rows = xs[t].shape[0] // 2
            return ref if half is None else ref.at[pl.ds(half * rows, rows), :]

        def copy(t, k, block, to, half=None, src=None):
            return pltpu.make_async_remote_copy(
                src_ref=part(t, block, half) if src is None else src, dst_ref=part(t, block, half),
                send_sem=send.at[t, k], recv_sem=recv.at[t, k], device_id=to, device_id_type=MESH)

        mine = [pltpu.make_async_copy(x_refs[t], part(t, me), loc.at[t]) for t in range(n)]
        for cp in mine:
            cp.start()
        sends = []
        for t in range(n):
            sends += [copy(t, 0, me, sib, src=x_refs[t]), copy(t, 1, me, (*xn, c), src=x_refs[t]),
                      copy(t, 2, me, (*yn, c), src=x_refs[t])]
            if not halved[t]:
                sends.append(copy(t, 3, me, (*dg, c), src=x_refs[t]))
        for cp in sends:
            cp.start()

        def pass_on(cp):
            cp.start()
            sends.append(cp)

        for t in range(n):
            h0, h1 = (0, 1) if halved[t] else (None, None)
            copy(t, 1, (*xn, c), me).wait_recv()
            if halved[t]:
                pass_on(copy(t, 3, (*xn, c), (*yn, c), 0))
            pass_on(copy(t, 5, (*xn, c), sib))
            copy(t, 2, (*yn, c), me).wait_recv()
            if halved[t]:
                pass_on(copy(t, 4, (*yn, c), (*xn, c), 1))
            pass_on(copy(t, 6, (*yn, c), sib))
            copy(t, 3, (*dg, c), me, h0).wait_recv()
            pass_on(copy(t, 7, (*dg, c), sib, h0))
            if halved[t]:
                copy(t, 4, (*dg, c), me, h1).wait_recv()
                pass_on(copy(t, 8, (*dg, c), sib, h1))
        for t in range(n):
            h0, h1 = (0, 1) if halved[t] else (None, None)
            copy(t, 0, sib, me).wait_recv()
            copy(t, 5, (*xn, 1 - c), me).wait_recv()
            copy(t, 6, (*yn, 1 - c), me).wait_recv()
            copy(t, 7, (*dg, 1 - c), me, h0).wait_recv()
            if halved[t]:
                copy(t, 8, (*dg, 1 - c), me, h1).wait_recv()
        for cp in sends:
            cp.wait_send()
        for cp in mine:
            cp.wait()

    return _launch(body, xs, [_sds(f, x.dtype) for f, x in zip(fulls, xs)], name, sequencer_id, kinds=9)


def pair_swap(gs, axes, name, sequencer_id):
    n = len(gs)
    shards = [tuple(d // (NDEV if a == ax else 1) for a, d in enumerate(g.shape)) for g, ax in zip(gs, axes)]

    def body(g_refs, p_refs, send, recv, loc):
        x, y, c = lax.axis_index("x"), lax.axis_index("y"), lax.axis_index("c")

        def copy(t, j):
            return pltpu.make_async_remote_copy(
                src_ref=_shard_of(g_refs[t], axes[t], shards[t][axes[t]], 2 * j + (1 - c)), dst_ref=p_refs[t].at[j],
                send_sem=send.at[t, j], recv_sem=recv.at[t, j], device_id=(x, y, 1 - c), device_id_type=MESH)

        copies = [copy(t, j) for t in range(n) for j in range(4)]
        for cp in copies:
            cp.start()
        for cp in copies:
            cp.wait_recv()
        for cp in copies:
            cp.wait_send()

    return _launch(body, gs, [_sds((4,) + sh, g.dtype) for sh, g in zip(shards, gs)], name, sequencer_id, SIBLING)


def pair_add(g, p, axis, name, after=()):
    _, rows, cols = p.shape
    tr = _row_tile(rows, cols, 1 << 20)
    nb = rows // tr
    if axis == 0:
        g_spec = pl.BlockSpec((tr, cols), lambda j, i, c: ((2 * j + c[0]) * nb + i, 0))
    else:
        g_spec = pl.BlockSpec((tr, cols), lambda j, i, c: (i, 2 * j + c[0]))
    blk = pl.BlockSpec((None, tr, cols), lambda j, i, c: (j, i, 0))

    na = len(after)

    def body(c_ref, *refs):
        g_ref, p_ref, q_ref = refs[na:]
        q_ref[...] = (g_ref[...].astype(f32) + p_ref[...].astype(f32)).astype(bf16)

    return pl.pallas_call(
        body, name=name, out_shape=_sds(p.shape, bf16),
        grid_spec=pltpu.PrefetchScalarGridSpec(num_scalar_prefetch=1, grid=(4, nb), in_specs=[ANY] * na + [g_spec, blk],
                                               out_specs=blk),
        compiler_params=_cp(("parallel", "parallel")),
    )(lax.axis_index("c").astype(jnp.int32).reshape(1), *after, g, p)


def chip_exchange(qs, name, sequencer_id):
    n = len(qs)

    def body(q_refs, r_refs, send, recv, loc):
        x, y, c = lax.axis_index("x"), lax.axis_index("y"), lax.axis_index("c")
        my_chip = 2 * x + y
        peers = [_peer(x, y, c, r) for r in OTHER_CHIPS]
        mine = [pltpu.make_async_copy(q_refs[t].at[my_chip], r_refs[t].at[my_chip], loc.at[t]) for t in range(n)]
        for cp in mine:
            cp.start()

        def copy(t, k, slot):
            p = peers[k]
            return pltpu.make_async_remote_copy(
                src_ref=q_refs[t].at[2 * p[0] + p[1]], dst_ref=r_refs[t].at[slot], send_sem=send.at[t, k], recv_sem=recv.at[t, k],
                device_id=p, device_id_type=MESH)

        sends = [copy(t, k, my_chip) for t in range(n) for k in range(3)]
        for cp in sends:
            cp.start()
        for t in range(n):
            for k in range(3):
                copy(t, k, 2 * peers[k][0] + peers[k][1]).wait_recv()
        for cp in sends:
            cp.wait_send()
        for cp in mine:
            cp.wait()

    return _launch(body, qs, [_sds(q.shape, q.dtype) for q in qs], name, sequencer_id, OTHER_CHIPS)


def _adamw(w, g, m, v):
    m = ADAM_B1 * m + (1.0 - ADAM_B1) * g
    v = ADAM_B2 * v + (1.0 - ADAM_B2) * jnp.square(g)
    m_hat = m / (1.0 - ADAM_B1 ** ADAM_STEP)
    v_hat = v / (1.0 - ADAM_B2 ** ADAM_STEP)
    return -ADAM_LR * (m_hat / (jnp.sqrt(v_hat) + ADAM_EPS) + ADAM_WD * w), m, v


def _sum8(r_ref):
    g = r_ref[0].astype(f32)
    for j in range(1, r_ref.shape[0]):
        g = g + r_ref[j].astype(f32)
    return g


def _row_tile(rows, cols, elems=1 << 18):
    tr = min(rows, max(8, 1 << int(math.log2(elems / cols))))
    assert rows % tr == 0, (rows, cols)
    return tr


def sum_partials(r, name, after=()):
    _, rows, cols = r.shape
    tr = _row_tile(rows, cols)
    na = len(after)

    def body(*refs):
        refs[na + 1][...] = _sum8(refs[na])

    return pl.pallas_call(
        body, name=name, grid=(rows // tr,),
        in_specs=[ANY] * na + [pl.BlockSpec((r.shape[0], tr, cols), lambda i: (0, i, 0))],
        out_specs=pl.BlockSpec((tr, cols), lambda i: (i, 0)), out_shape=_sds((rows, cols), f32),
        compiler_params=_cp(("parallel",)),
    )(*after, r)


def adamw_t(w, m, v, grads, name):
    rows, nl, cols = w.shape
    tr = min(rows, (1 << 16) // cols)
    blk = pl.BlockSpec((tr, nl, cols), lambda i: (i, 0, 0))
    flat = pl.BlockSpec((tr, cols), lambda i: (i, 0))

    def body(w_ref, m_ref, v_ref, *rest):
        g_refs, (g_ref, d_ref, nm_ref, nv_ref) = rest[:nl], rest[nl:]
        for l in range(nl):
            grad = g_refs[l][...]
            g_ref[:, l, :] = grad
            d_ref[:, l, :], nm_ref[:, l, :], nv_ref[:, l, :] = _adamw(w_ref[:, l, :], grad, m_ref[:, l, :], v_ref[:, l, :])

    return tuple(pl.pallas_call(
        body, name=name, grid=(pl.cdiv(rows, tr),), in_specs=[blk] * 3 + [flat] * nl, out_specs=[blk] * 4,
        out_shape=[_sds(w.shape, f32)] * 4, compiler_params=_cp(("parallel",)),
    )(w, m, v, *grads))


def adamw(w, m, v, layer, name, r=None, g=None, prev=None):
    _, rows, cols = w.shape
    tr = _row_tile(rows, cols)
    blk = pl.BlockSpec((None, tr, cols), lambda i: (layer, i, 0))
    nprev = 0 if prev is None else 4

    def body(w_ref, m_ref, v_ref, src_ref, *rest):
        g_ref, d_ref, nm_ref, nv_ref, token_ref = rest[nprev:]
        grad = _sum8(src_ref) if g is None else src_ref[...]
        g_ref[...] = grad
        d_ref[...], nm_ref[...], nv_ref[...] = _adamw(w_ref[...], grad, m_ref[...], v_ref[...])
        token_ref[...] = jnp.zeros_like(token_ref)

    src, src_spec = (r, pl.BlockSpec((r.shape[0], tr, cols), lambda i: (0, i, 0))) if g is None else (g, pl.BlockSpec((tr, cols), lambda i: (i, 0)))
    *outs, token = pl.pallas_call(
        body, name=name, grid=(rows // tr,), in_specs=[blk] * 3 + [src_spec] + [ANY] * nprev,
        out_specs=[blk] * 4 + [pl.BlockSpec((8, HD), lambda i: (0, 0))],
        out_shape=[_sds(w.shape, f32)] * 4 + [_sds((8, HD), f32)], input_output_aliases={4 + k: k for k in range(nprev)},
        compiler_params=_cp(("arbitrary",)),
    )(w, m, v, src, *(prev or ()))
    return tuple(outs), token


def small_adamw(parts, w, m, v, name, after=()):
    def body(*refs):
        p_ref, w_ref, m_ref, v_ref, g_ref, d_ref, nm_ref, nv_ref = refs[len(after):]
        g = _sum8(p_ref)
        g_ref[...] = g
        d_ref[...], nm_ref[...], nv_ref[...] = _adamw(w_ref[...], g, m_ref[...], v_ref[...])

    vmem = pl.BlockSpec(memory_space=pltpu.VMEM)
    return pl.pallas_call(body, name=name, out_shape=[_sds(w.shape, f32)] * 4, in_specs=[ANY] * len(after) + [vmem] * 4,
                          out_specs=[vmem] * 4, compiler_params=_cp())(*after, parts, w, m, v)


def _pack_rows(wt):
    tail = jnp.pad(wt[5888:N_IN], ((OFF_G - 12 - (OFF_P + 512), 0), (0, 0)))
    return jnp.concatenate([wt[512:2816], wt[2816:5120], wt[5120:5888], wt[0:512], tail], axis=0)


def _unpack_rows(g):
    return jnp.concatenate([g[OFF_P:OFF_P + 512], g[OFF_SB:OFF_SB + 2304], g[OFF_GQ:OFF_GQ + 2304], g[OFF_Z:OFF_Z + 768],
                            g[OFF_G - 12:NP]], axis=0)


def _lanes(v):
    flat = v.reshape(-1)
    n = -(-flat.shape[0] // HD) * HD
    return jnp.pad(flat, (0, n - flat.shape[0])).reshape(n // HD, HD)


def _lanes8(v):
    rows = _lanes(v)
    return jnp.pad(rows, ((0, -rows.shape[0] % 8), (0, 0)))


def _layer_fwd(x, p, l):
    nm = lambda s: f"{s}_l{l}"
    u = rms_fwd(x, p["attn_norm"], nm("rms1"))
    proj = matmul(u, p["w_in"], name=nm("inproj"), tb=True)
    y_pool = pool_fwd(proj, p["pool_w"], p["pool_scale"], nm("pool"))
    y_sb = sb_fwd(proj, nm("sb"))
    c = conv_fwd(proj, p["conv"], nm("conv"))
    ga = gdn_a_fwd(c, proj, p["alog"], p["dtb"], nm("gdna"))
    o, states = gdn_b_fwd(*ga, nm("gdnb"))
    y_gdn = gdn_out_fwd(o, proj, p["gdn_norm"], nm("gdno"))
    ups = [matmul(y, p[k], name=nm(k)) for y, k in ((y_pool, "w_pool_up"), (y_sb, "w_sb_up"), (y_gdn, "w_gdn_up"))]
    merged = merge_fwd(proj, ups, nm("merge"))
    x1 = matmul(merged, p["w_out"], name=nm("outproj"), epilogue=lambda acc, r: acc + r, extras=(x,))
    u2 = rms_fwd(x1, p["mlp_norm"], nm("rms2"))
    h2 = matmul(u2, p["w_ff1"], name=nm("ff1"), out_dtype=bf16, epilogue=lambda acc: jnp.square(jnp.maximum(acc, 0.0)))
    x2 = matmul(h2, p["w_ff2"], name=nm("ff2"), epilogue=lambda acc, r: acc + r, extras=(x1,))
    saved = dict(x=x, u=u, proj=proj, y_pool=y_pool, y_sb=y_sb, c=c, ga=ga, o=o, states=states, y_gdn=y_gdn, ups=ups,
                 merged=merged, x1=x1, u2=u2, h2=h2)
    return x2, saved


def _layer_bwd(dx2, dx2b, sv, p, l, swap, finish):
    nm = lambda s: f"{s}_l{l}"
    s = dx2.shape[0]
    dh = matmul(dx2b, p["w_ff2"], name=nm("d_ff2_x"), tb=True, out_dtype=bf16,
                epilogue=lambda acc, h2: acc * (2.0 * jnp.sqrt(h2.astype(f32))), extras=(sv["h2"],))
    g_ff2 = matmul(sv["h2"], dx2b, name=nm("d_ff2_w"), ta=True, out_dtype=bf16)
    du2 = matmul(dh, p["w_ff1"], name=nm("d_ff1_x"), tb=True)
    g_ff1 = matmul(sv["u2"], dh, name=nm("d_ff1_w"), ta=True, out_dtype=bf16)
    dx1, dx1b, d_mlp_norm = rms_bwd(sv["x1"], du2, dx2, p["mlp_norm"], nm("d_rms2"))
    dmerged = matmul(dx1b, p["w_out"], name=nm("d_out_x"), tb=True, out_dtype=bf16)
    g_out = matmul(sv["merged"], dx1b, name=nm("d_out_w"), ta=True, out_dtype=bf16)
    dgates, *dups = merge_bwd(sv["proj"], sv["ups"], dmerged, nm("d_merge"))
    dys, g_ups = [], []
    for dup, y, k in zip(dups, (sv["y_pool"], sv["y_sb"], sv["y_gdn"]), ("w_pool_up", "w_sb_up", "w_gdn_up")):
        dys.append(matmul(dup, p[k], name=nm("d_" + k + "_x"), tb=True, out_dtype=bf16))
        g_ups.append(matmul(y, dup, name=nm("d_" + k + "_w"), ta=True, out_dtype=bf16))
    early = g_ups + [g_out, g_ff1, g_ff2]
    swapped = swap(early, BIG_AXES[1:], "a")
    do, dz, d_gdn_norm = gdn_out_bwd(sv["o"], sv["proj"], p["gdn_norm"], dys[2], nm("d_gdno"))
    cots = gdn_b_bwd(*sv["ga"], sv["states"], do, nm("d_gdnb"))
    dc, dab, d_alog, d_dtb = gdn_a_bwd(sv["c"], sv["proj"], p["alog"], p["dtb"], cots[:5], cots[5], nm("d_gdna"))
    r_early, sent_early = finish(early, swapped, BIG_AXES[1:], BIG_NAMES[1:], "a", (dab,))
    dgq, d_conv = conv_bwd(sv["proj"], p["conv"], dc, nm("d_conv"), sent_early)
    dq, dk, dv = sb_bwd(sv["proj"], dys[1], nm("d_sb"))
    dp, d_pool_w, d_pool_scale = pool_bwd(sv["proj"], p["pool_w"], p["pool_scale"], dys[0], nm("d_pool"))
    dproj = jnp.concatenate([dq, dk, dv, dgq, dz, dp, jnp.zeros((s, OFF_AB - OFF_P - W_POOL), bf16), dab, dgates], axis=1)
    g_in = matmul(dproj, sv["u"], name=nm("d_in_w"), ta=True, out_dtype=bf16)
    r_in, sent_in = finish([g_in], swap([g_in], BIG_AXES[:1], "b"), BIG_AXES[:1], BIG_NAMES[:1], "b", ())
    du = matmul(dproj, p["w_in"], name=nm("d_in_x"), after=sent_in)
    dx, dxb, d_attn_norm = rms_bwd(sv["x"], du, dx1, p["attn_norm"], nm("d_rms1"))
    recv = r_in + r_early
    small = [d_attn_norm, d_pool_w, d_pool_scale, d_conv, d_alog, d_dtb, d_gdn_norm, d_mlp_norm]
    return dx, dxb, recv, small


BIG_AXES = (1, 1, 1, 1, 0, 1, 0)
GATHER_ID, EXCHANGE_ID = 1, 2
BIG_NAMES = ("w_in", "w_pool_up", "w_sb_up", "w_gdn_up", "w_out", "w_ff1", "w_ff2")


def kernel(x, attn_norm, w_in, pool_w, pool_scale, gdn_conv, gdn_a_log, gdn_dt_bias, gdn_norm, w_pool_up, w_sb_up, w_gdn_up, w_out, mlp_norm, w_ff1, w_ff2, final_norm, loss_target, m_attn_norm, m_w_in, m_pool_w, m_pool_scale, m_gdn_conv, m_gdn_a_log, m_gdn_dt_bias, m_gdn_norm, m_w_pool_up, m_w_sb_up, m_w_gdn_up, m_w_out, m_mlp_norm, m_w_ff1, m_w_ff2, m_final_norm, v_attn_norm, v_w_in, v_pool_w, v_pool_scale, v_gdn_conv, v_gdn_a_log, v_gdn_dt_bias, v_gdn_norm, v_w_pool_up, v_w_sb_up, v_w_gdn_up, v_w_out, v_mlp_norm, v_w_ff1, v_w_ff2, v_final_norm):
    s = x.shape[1]
    me = _dev_index((lax.axis_index("x"), lax.axis_index("y"), lax.axis_index("c")))
    ncv = gdn_conv.shape[2]

    w_in_t, m_w_in_t, v_w_in_t = (jnp.transpose(a, (2, 0, 1)) for a in (w_in, m_w_in, v_w_in))
    full = []
    for l in range(NL):
        shards = [_pack_rows(w_in_t[:, l]).astype(bf16), w_pool_up[l].astype(bf16), w_sb_up[l].astype(bf16),
                  w_gdn_up[l].astype(bf16), w_out[l].astype(bf16), w_ff1[l].astype(bf16), w_ff2[l].astype(bf16)]
        if l == 0:
            first = all_gather([shards[0], gdn_conv.reshape(NL * 4, ncv)], (1, 0), "gather_first", sequencer_id=GATHER_ID)
            rest = all_gather(shards[1:], BIG_AXES[1:], "gather_rest_l0", sequencer_id=GATHER_ID)
            full.append([first[0]] + rest)
            conv_full = first[1].reshape(NDEV, NL, 4, ncv).transpose(1, 2, 0, 3).reshape(NL, 4, NDEV * ncv)
        else:
            first = all_gather(shards[:1], BIG_AXES[:1], f"gather_in_l{l}", sequencer_id=GATHER_ID)
            full.append(first + all_gather(shards[1:], BIG_AXES[1:], f"gather_rest_l{l}", sequencer_id=GATHER_ID))
    params = []
    for l in range(NL):
        p = dict(zip(("w_in", "w_pool_up", "w_sb_up", "w_gdn_up", "w_out", "w_ff1", "w_ff2"), full[l][:7]))
        p.update(attn_norm=attn_norm[l][None], mlp_norm=mlp_norm[l][None], pool_w=pool_w[l], pool_scale=pool_scale[l][None],
                 conv=conv_full[l], alog=_lanes(gdn_a_log[l]), dtb=_lanes(gdn_dt_bias[l]), gdn_norm=gdn_norm[l][None])
        params.append(p)

    h = x[0]
    saved = []
    for l in range(NL):
        h, sv = _layer_fwd(h, params[l], l)
        saved.append(sv)
    dh, dhb, loss_row, d_final = loss_head(h, loss_target[0], final_norm[None], "loss_head")
    recv, smalls = [None] * NL, [None] * NL
    for l in reversed(range(NL)):
        def swap(gs, axes, tag, l=l):
            return pair_swap(gs, axes, f"swap_{tag}_l{l}", None)

        def finish(gs, ps, axes, names, tag, after, l=l):
            qs = [pair_add(g, p, ax, f"pair_add_{k}_l{l}", after) for g, p, ax, k in zip(gs, ps, axes, names)]
            return chip_exchange(qs, f"exchange_{tag}_l{l}", EXCHANGE_ID), tuple(qs)

        dh, dhb, recv[l], smalls[l] = _layer_bwd(dh, dhb, saved[l], params[l], l, swap, finish)

    small_rows = [_lanes8(t) for l in range(NL) for t in smalls[l]] + [_lanes8(d_final), _lanes8(loss_row)]
    packed = jnp.concatenate(small_rows, axis=0)
    parts = all_gather([packed], (0,), "gather_small", sequencer_id=GATHER_ID)[0].reshape(NDEV, packed.shape[0], HD)

    def pack_small(tree):
        rows = []
        for l in range(NL):
            rows += [_lanes8(tree["attn_norm"][l]), _lanes8(tree["pool_w"][l]), _lanes8(tree["pool_scale"][l]),
                     jnp.zeros((4 * NDEV * ncv // HD, HD), f32), _lanes8(tree["gdn_a_log"][l]), _lanes8(tree["gdn_dt_bias"][l]),
                     _lanes8(tree["gdn_norm"][l]), _lanes8(tree["mlp_norm"][l])]
        rows += [_lanes8(tree["final_norm"]), jnp.zeros((8, HD), f32)]
        return jnp.concatenate(rows, axis=0)

    names = ("attn_norm", "pool_w", "pool_scale", "gdn_a_log", "gdn_dt_bias", "gdn_norm", "mlp_norm", "final_norm")
    w_small = pack_small(dict(zip(names, (attn_norm, pool_w, pool_scale, gdn_a_log, gdn_dt_bias, gdn_norm, mlp_norm, final_norm))))
    m_small = pack_small(dict(zip(names, (m_attn_norm, m_pool_w, m_pool_scale, m_gdn_a_log, m_gdn_dt_bias, m_gdn_norm, m_mlp_norm, m_final_norm))))
    v_small = pack_small(dict(zip(names, (v_attn_norm, v_pool_w, v_pool_scale, v_gdn_a_log, v_gdn_dt_bias, v_gdn_norm, v_mlp_norm, v_final_norm))))
    def unpack_small(buf):
        out, conv_g, r = {}, [], 0
        layer_items = (("attn_norm", (D,)), ("pool_w", (4, HD, HD)), ("pool_scale", (W_POOL,)), ("conv", (4, NDEV * ncv)),
                       ("gdn_a_log", (NH,)), ("gdn_dt_bias", (NH,)), ("gdn_norm", (HD,)), ("mlp_norm", (D,)))
        per_layer = {k: [] for k, _ in layer_items}
        for l in range(NL):
            for k, shape in layer_items:
                size = math.prod(shape)
                nrow = -(-size // (8 * HD)) * 8
                per_layer[k].append(buf[r:r + nrow].reshape(-1)[:size].reshape(shape))
                r += nrow
        for k, _ in layer_items:
            out[k] = jnp.stack(per_layer[k])
        out["final_norm"] = buf[r:r + D // HD].reshape(D)
        out["loss"] = buf[r + D // HD, 0]
        return out

    big_out = {}
    big_names = BIG_NAMES
    big_w = dict(zip(big_names, ((w_in_t, m_w_in_t, v_w_in_t), (w_pool_up, m_w_pool_up, v_w_pool_up), (w_sb_up, m_w_sb_up, v_w_sb_up),
                                 (w_gdn_up, m_w_gdn_up, v_w_gdn_up), (w_out, m_w_out, v_w_out), (w_ff1, m_w_ff1, v_w_ff1),
                                 (w_ff2, m_w_ff2, v_w_ff2))))
    tokens = []
    for l, k in [(l, k) for l in reversed(range(NL)) for k in big_names[1:]]:
        w, m, v = big_w[k]
        big_out[k], token = adamw(w, m, v, l, f"adamw_{k}_l{l}", r=recv[l][big_names.index(k)], prev=big_out.get(k))
        tokens.append(token)
    g_in = [_unpack_rows(sum_partials(recv[l][0], f"sum_w_in_l{l}", tuple(tokens) if l == 0 else ())) for l in range(NL)]
    big_out["w_in"] = adamw_t(*big_w["w_in"], g_in, "adamw_w_in")

    small_out = small_adamw(parts, w_small, m_small, v_small, "adamw_small", after=(big_out["w_in"][1],))
    sm = [unpack_small(b) for b in small_out]
    loss = sm[0]["loss"]
    g_conv = lax.dynamic_slice_in_dim(sm[0]["conv"], me * ncv, ncv, axis=2)
    conv_out = None
    for l in reversed(range(NL)):
        conv_out, _ = adamw(gdn_conv, m_gdn_conv, v_gdn_conv, l, f"adamw_conv_l{l}", g=g_conv[l], prev=conv_out)

    def leaf(i, k):
        if k == "w_in":
            return jnp.transpose(big_out[k][i], (1, 2, 0))
        if k in big_out:
            return big_out[k][i]
        if k == "gdn_conv":
            return conv_out[i]
        return sm[i][k]

    order = ("attn_norm", "w_in", "pool_w", "pool_scale", "gdn_conv", "gdn_a_log", "gdn_dt_bias", "gdn_norm", "w_pool_up",
             "w_sb_up", "w_gdn_up", "w_out", "mlp_norm", "w_ff1", "w_ff2", "final_norm")
    return (loss, dh[None]) + tuple(leaf(i, k) for i in range(4) for k in order)
```

```python
import functools
import math

import jax
import jax.numpy as jnp
from jax import lax
from jax.experimental import pallas as pl
from jax.experimental.pallas import tpu as pltpu
from jax.experimental.pallas import tpu_sc as plsc

f32, bf16 = jnp.float32, jnp.bfloat16

D = 2048
NDEV = 8
NL = 2
HD = 128
NH = 6
WH = NH * HD
W_POOL = 512
EPS = 1e-6
N_IN = 12044
NP = 12288
OFF_SB, OFF_GQ, OFF_Z, OFF_P, OFF_AB, OFF_G = 0, 2304, 4608, 5376, 6016, 6144
AB_LANE = HD - 2 * NH
POOL_WINDOWS = (2, 4, 8, 16)
CH = 128
TQ = 256
VMEM_LIMIT = 56 * 1024 * 1024
ADAM_LR, ADAM_B1, ADAM_B2, ADAM_EPS, ADAM_WD, ADAM_STEP = 0.001, 0.9, 0.999, 1e-08, 0.01, 10
MESH = pl.DeviceIdType.MESH


def _cp(sem=None):
    return pltpu.CompilerParams(dimension_semantics=sem, vmem_limit_bytes=VMEM_LIMIT)


def _sds(shape, dtype):
    return jax.ShapeDtypeStruct(tuple(shape), dtype)


def matmul(a, b, *, name, ta=False, tb=False, out_dtype=f32, tm=1024, tn=1024, tk=2048, epilogue=None, extras=(), after=()):
    m, k = (a.shape[1], a.shape[0]) if ta else a.shape
    n = b.shape[0] if tb else b.shape[1]
    assert k == (b.shape[1] if tb else b.shape[0]) and a.dtype == bf16 and b.dtype == bf16
    tm, tn, tk = min(tm, m), min(tn, n), min(tk, k)
    assert m % tm == 0 and n % tn == 0 and k % tk == 0, (m, n, k, tm, tn, tk)
    nk = k // tk
    a_spec = pl.BlockSpec((tk, tm), lambda i, j, q: (q, i)) if ta else pl.BlockSpec((tm, tk), lambda i, j, q: (i, q))
    b_spec = pl.BlockSpec((tn, tk), lambda i, j, q: (j, q)) if tb else pl.BlockSpec((tk, tn), lambda i, j, q: (q, j))
    e_specs = [pl.BlockSpec((tm, tn), lambda i, j, q: (i, j)) for _ in extras]
    dn = (((0 if ta else 1,), (1 if tb else 0,)), ((), ()))
    ne = len(extras)

    def body(*refs):
        a_ref, b_ref, *rest = refs[len(after):]
        e_refs, o_ref = rest[:ne], rest[ne]
        part = lax.dot_general(a_ref[...], b_ref[...], dn, preferred_element_type=f32)

        def finish(acc):
            if epilogue is not None:
                acc = epilogue(acc, *[e[...] for e in e_refs])
            o_ref[...] = acc.astype(out_dtype)

        if nk == 1:
            finish(part)
        else:
            acc_ref = rest[ne + 1]
            q = pl.program_id(2)

            @pl.when(q == 0)
            def _():
                acc_ref[...] = part

            @pl.when(jnp.logical_and(q > 0, q < nk - 1))
            def _():
                acc_ref[...] += part

            @pl.when(q == nk - 1)
            def _():
                finish(acc_ref[...] + part)

    return pl.pallas_call(
        body, name=name, grid=(m // tm, n // tn, nk),
        in_specs=[pl.BlockSpec(memory_space=pl.ANY)] * len(after) + [a_spec, b_spec] + e_specs,
        out_specs=pl.BlockSpec((tm, tn), lambda i, j, q: (i, j)), out_shape=_sds((m, n), out_dtype),
        scratch_shapes=[pltpu.VMEM((tm, tn), f32)] if nk > 1 else [],
        compiler_params=_cp(("parallel", "parallel", "arbitrary")),
    )(*after, a, b, *extras)


def rowwise(name, fn, rows, params, outs, sums=(), tr=256, after=()):
    s = rows[0][0].shape[0]
    tr = min(tr, s)
    nin, nout = len(rows) + len(params), len(outs)
    in_specs = [pl.BlockSpec((tr, w), functools.partial(lambda i, c: (i, c), c=c)) for (_, w, c) in rows]
    in_specs += [pl.BlockSpec(p.shape, lambda i: (0, 0)) for p in params]
    out_specs = [pl.BlockSpec((tr, w), lambda i: (i, 0)) for (w, _) in outs]
    out_specs += [pl.BlockSpec(sh, lambda i: (0, 0)) for sh in sums]
    out_shape = [_sds((s, w), dt) for (w, dt) in outs] + [_sds(sh, f32) for sh in sums]

    def body(*refs):
        refs = refs[len(after):]
        res = fn(*[r[...] for r in refs[:nin]])
        for r, v in zip(refs[nin:nin + nout], res[:nout]):
            r[...] = v.astype(r.dtype)
        i = pl.program_id(0)
        for r, v in zip(refs[nin + nout:], res[nout:]):
            @pl.when(i == 0)
            def _(r=r, v=v):
                r[...] = v

            @pl.when(i > 0)
            def _(r=r, v=v):
                r[...] += v

    res = pl.pallas_call(
        body, name=name, grid=(s // tr,), in_specs=[pl.BlockSpec(memory_space=pl.ANY)] * len(after) + in_specs,
        out_specs=out_specs, out_shape=out_shape, compiler_params=_cp(("arbitrary",)),
    )(*after, *[r[0] for r in rows], *params)
    return res


def _rms(x, g):
    return x * lax.rsqrt(jnp.mean(x * x, axis=-1, keepdims=True) + EPS) * g


def rms_fwd(x, g, name):
    return rowwise(name, lambda xb, gb: (_rms(xb, gb),), [(x, D, 0)], [g], [(D, bf16)])[0]


def rms_bwd(x, du, dres, g, name, after=()):
    def fn(xb, dub, drb, gb):
        _, vjp = jax.vjp(_rms, xb, gb)
        dx, dg = vjp(dub.astype(f32))
        return drb + dx, drb + dx, dg

    return rowwise(name, fn, [(x, D, 0), (du, D, 0), (dres, D, 0)], [g], [(D, f32), (D, bf16)], [(1, D)], after=after)


def _merge(gates, up_p, up_s, up_g):
    sg = jax.nn.sigmoid(gates)
    return sg[:, :D] * up_p + sg[:, D:2 * D] * up_s + sg[:, 2 * D:] * up_g


def merge_fwd(gates, ups, name):
    return rowwise(name, lambda g, a, b, c: (_merge(g, a, b, c),),
                   [(gates, 3 * D, 0)] + [(u, D, 0) for u in ups], [], [(D, bf16)], tr=128)[0]


def merge_bwd(gates, ups, dmerged, name):
    def fn(g, a, b, c, dm):
        _, vjp = jax.vjp(_merge, g, a, b, c)
        return vjp(dm.astype(f32))

    return rowwise(name, fn, [(gates, 3 * D, 0)] + [(u, D, 0) for u in ups] + [(dmerged, D, 0)], [],
                   [(3 * D, bf16), (D, bf16), (D, bf16), (D, bf16)], tr=128)


def _gdn_out(o, z, g):
    ys = []
    for h in range(NH):
        sl = slice(h * HD, (h + 1) * HD)
        ys.append(_rms(o[:, sl], g) * jax.nn.silu(z[:, sl]))
    return jnp.concatenate(ys, axis=1)


def gdn_out_fwd(o, proj, g, name):
    return rowwise(name, lambda ob, zb, gb: (_gdn_out(ob, zb, gb),), [(o, WH, 0), (proj, WH, OFF_Z // WH)], [g],
                   [(WH, bf16)])[0]


def gdn_out_bwd(o, proj, g, dy, name):
    def fn(ob, zb, dyb, gb):
        _, vjp = jax.vjp(_gdn_out, ob, zb, gb)
        return vjp(dyb.astype(f32))

    return rowwise(name, fn, [(o, WH, 0), (proj, WH, OFF_Z // WH), (dy, WH, 0)], [g], [(WH, f32), (WH, bf16)],
                   [(1, HD)])


def loss_head(x, target, g, name):
    def loss_fn(xb, gb, tb):
        err = _rms(xb, gb) - tb
        return (0.5 / D) * jnp.sum(jnp.sum(err * err, axis=1, keepdims=True), axis=0, keepdims=True)

    def fn(xb, tb, gb):
        val, vjp = jax.vjp(functools.partial(loss_fn, tb=tb), xb, gb)
        dx, dg = vjp(jnp.ones((1, 1), f32))
        return dx, dx, jnp.broadcast_to(val, (1, HD)), dg

    return rowwise(name, fn, [(x, D, 0), (target, D, 0)], [g], [(D, f32), (D, bf16)], [(1, HD), (1, D)])


PB = 256


def _split(v):
    hi = v.astype(bf16)
    return hi, (v - hi.astype(f32)).astype(bf16)


def _band_dot(make_band, v, s, forward):
    hi, lo = _split(v)
    nb = s // PB
    outs = []
    for r in range(nb):
        lo_r = max(r - 1, 0) if forward else r
        hi_r = r + 1 if forward else min(r + 2, nb)
        band = make_band(r * PB, lo_r * PB, (hi_r - lo_r) * PB)
        sl = slice(lo_r * PB, hi_r * PB)
        outs.append(jnp.dot(band, hi[sl], preferred_element_type=f32) + jnp.dot(band, lo[sl], preferred_element_type=f32))
    return jnp.concatenate(outs, axis=0)


def _pool_common(p, win, s):
    def band(row0, col0, ncol):
        t = row0 + lax.broadcasted_iota(jnp.int32, (PB, ncol), 0)
        u = col0 + lax.broadcasted_iota(jnp.int32, (PB, ncol), 1)
        return jnp.logical_and(u <= t, t < u + win).astype(bf16)

    def band_t(row0, col0, ncol):
        u = row0 + lax.broadcasted_iota(jnp.int32, (PB, ncol), 0)
        t = col0 + lax.broadcasted_iota(jnp.int32, (PB, ncol), 1)
        return jnp.logical_and(u <= t, t < u + win).astype(bf16)

    t = lax.broadcasted_iota(jnp.int32, (s, 1), 0)
    inv_n = 1.0 / jnp.minimum(t + 1, win).astype(f32)
    d = _band_dot(band, p, s, True) * inv_n - p
    return d, inv_n, band_t


def pool_fwd(proj, pool_w, pool_scale, name):
    s = proj.shape[0]

    def body(p_ref, w_ref, sc_ref, y_ref):
        win = jnp.left_shift(2, pl.program_id(0))
        d, _, _ = _pool_common(p_ref[...], win, s)
        y = jnp.dot(d.astype(bf16), w_ref[...].astype(bf16), preferred_element_type=f32) * sc_ref[...]
        y_ref[...] = y.astype(bf16)

    return pl.pallas_call(
        body, name=name, grid=(4,),
        in_specs=[pl.BlockSpec((s, HD), lambda g: (0, OFF_P // HD + g)), pl.BlockSpec((None, HD, HD), lambda g: (g, 0, 0)),
                  pl.BlockSpec((1, HD), lambda g: (0, g))],
        out_specs=pl.BlockSpec((s, HD), lambda g: (0, g)), out_shape=_sds((s, W_POOL), bf16),
        compiler_params=_cp(("arbitrary",)),
    )(proj, pool_w, pool_scale)


def pool_bwd(proj, pool_w, pool_scale, dy, name):
    s = proj.shape[0]

    def body(p_ref, w_ref, sc_ref, dy_ref, dp_ref, dw_ref, dsc_ref):
        win = jnp.left_shift(2, pl.program_id(0))
        d, inv_n, band_t = _pool_common(p_ref[...], win, s)
        w = w_ref[...].astype(bf16)
        dyf = dy_ref[...].astype(f32)
        dsc_ref[...] = jnp.sum(dyf * jnp.dot(d.astype(bf16), w, preferred_element_type=f32), axis=0, keepdims=True)
        dys = (dyf * sc_ref[...]).astype(bf16)
        dd = lax.dot_general(dys, w, (((1,), (1,)), ((), ())), preferred_element_type=f32)
        dw_ref[...] = lax.dot_general(d.astype(bf16), dys, (((0,), (0,)), ((), ())), preferred_element_type=f32)
        dp_ref[...] = (_band_dot(band_t, dd * inv_n, s, False) - dd).astype(bf16)

    return pl.pallas_call(
        body, name=name, grid=(4,),
        in_specs=[pl.BlockSpec((s, HD), lambda g: (0, OFF_P // HD + g)), pl.BlockSpec((None, HD, HD), lambda g: (g, 0, 0)),
                  pl.BlockSpec((1, HD), lambda g: (0, g)), pl.BlockSpec((s, HD), lambda g: (0, g))],
        out_specs=[pl.BlockSpec((s, HD), lambda g: (0, g)), pl.BlockSpec((None, HD, HD), lambda g: (g, 0, 0)),
                   pl.BlockSpec((1, HD), lambda g: (0, g))],
        out_shape=[_sds((s, W_POOL), bf16), _sds((4, HD, HD), f32), _sds((1, W_POOL), f32)],
        compiler_params=_cp(("arbitrary",)),
    )(proj, pool_w, pool_scale, dy)


HB = 3
WB = HB * HD


def _heads_of(v):
    return _stack([v[:, h * HD:(h + 1) * HD] for h in range(HB)])


def _bd(a, b, dn):
    return lax.dot_general(a, b, dn, preferred_element_type=f32)


def _run_sum(v, tri):
    hi, lo = _split(v.reshape(HB * TQ, TQ))
    return (jnp.dot(hi, tri, preferred_element_type=f32) + jnp.dot(lo, tri, preferred_element_type=f32)).reshape(HB, TQ, TQ)


def _tri(later):
    j = lax.broadcasted_iota(jnp.int32, (TQ, TQ), 0)
    u = lax.broadcasted_iota(jnp.int32, (TQ, TQ), 1)
    return (j > u if later else j < u).astype(bf16)


def _sb_tile(q, k_ref, kb, carry, diagonal):
    k = _heads_of(k_ref[pl.ds(pl.multiple_of(kb * TQ, TQ), TQ), :].astype(bf16))
    z = _bd(q, k, BNT)
    ls = jax.nn.log_sigmoid(-z)
    if diagonal:
        mask = lax.broadcasted_iota(jnp.int32, (TQ, TQ), 1) < lax.broadcasted_iota(jnp.int32, (TQ, TQ), 0)
        ls = jnp.where(mask, ls, 0.0)
    a = jnp.exp(ls + z + _run_sum(ls, _tri(True)) + carry)
    return z, ls, jnp.where(mask, a, 0.0) if diagonal else a


def sb_fwd(proj, name):
    s = proj.shape[0]
    nq = s // TQ
    scale = HD ** -0.5

    def body(q_ref, k_ref, v_ref, y_ref):
        qi = pl.program_id(1)
        q = _heads_of((q_ref[...] * scale).astype(bf16))

        def step(j, c, diagonal=False):
            acc, carry = c
            kb = qi - j
            _, ls, a = _sb_tile(q, k_ref, kb, carry, diagonal)
            v = _heads_of(v_ref[pl.ds(pl.multiple_of(kb * TQ, TQ), TQ), :].astype(bf16))
            return acc + _bd(a.astype(bf16), v, BNN), carry + jnp.sum(ls, axis=2, keepdims=True)

        first = step(0, (jnp.zeros((HB, TQ, HD), f32), jnp.zeros((HB, TQ, 1), f32)), True)
        acc, _ = lax.fori_loop(1, qi + 1, step, first)
        for h in range(HB):
            y_ref[:, h * HD:(h + 1) * HD] = acc[h].astype(bf16)

    c0, ng = OFF_SB // WB, NH // HB
    return pl.pallas_call(
        body, name=name, grid=(ng, nq),
        in_specs=[pl.BlockSpec((TQ, WB), lambda g, i: (i, c0 + g)), pl.BlockSpec((s, WB), lambda g, i: (0, c0 + ng + g)),
                  pl.BlockSpec((s, WB), lambda g, i: (0, c0 + 2 * ng + g))],
        out_specs=pl.BlockSpec((TQ, WB), lambda g, i: (i, g)), out_shape=_sds((s, WH), bf16),
        compiler_params=_cp(("arbitrary", "arbitrary")),
    )(proj, proj, proj)


def sb_bwd(proj, dy, name):
    s = proj.shape[0]
    nq = s // TQ
    scale = HD ** -0.5

    def body(q_ref, k_ref, v_ref, do_ref, dq_ref, dk_ref, dv_ref, e_scr, z_scr, dk_acc, dv_acc):
        qi = pl.program_id(1)
        q = _heads_of((q_ref[...] * scale).astype(bf16))
        do = _heads_of(do_ref[...])

        @pl.when(qi == 0)
        def _():
            dk_acc[...] = jnp.zeros_like(dk_acc)
            dv_acc[...] = jnp.zeros_like(dv_acc)

        def add_heads(acc_ref, rows, upd):
            for h in range(HB):
                acc_ref[rows, h * HD:(h + 1) * HD] += upd[h]

        def sweep_back(j, carry, diagonal=False):
            kb = qi - j
            rows = pl.ds(pl.multiple_of(kb * TQ, TQ), TQ)
            z, ls, a = _sb_tile(q, k_ref, kb, carry, diagonal)
            v = _heads_of(v_ref[rows, :].astype(bf16))
            e_scr[kb] = _bd(do, v, BNT) * a
            z_scr[kb] = z
            add_heads(dv_acc, rows, _bd(a.astype(bf16), do, BTN))
            return carry + jnp.sum(ls, axis=2, keepdims=True)

        lax.fori_loop(1, qi + 1, sweep_back, sweep_back(0, jnp.zeros((HB, TQ, 1), f32), True))

        def sweep_fwd(kb, c, diagonal=False):
            dq, carry = c
            rows = pl.ds(pl.multiple_of(kb * TQ, TQ), TQ)
            e, z = e_scr[kb], z_scr[kb]
            sig = jax.nn.sigmoid(z)
            dz = e * (1.0 - sig) - (_run_sum(e, _tri(False)) + carry) * sig
            if diagonal:
                dz = jnp.where(lax.broadcasted_iota(jnp.int32, (TQ, TQ), 1) < lax.broadcasted_iota(jnp.int32, (TQ, TQ), 0), dz, 0.0)
            dz = dz.astype(bf16)
            k = _heads_of(k_ref[rows, :].astype(bf16))
            add_heads(dk_acc, rows, _bd(dz, q, BTN))
            return dq + _bd(dz, k, BNN), carry + jnp.sum(e, axis=2, keepdims=True)

        dq, _ = sweep_fwd(qi, lax.fori_loop(0, qi, sweep_fwd, (jnp.zeros((HB, TQ, HD), f32), jnp.zeros((HB, TQ, 1), f32))), True)
        for h in range(HB):
            dq_ref[:, h * HD:(h + 1) * HD] = (dq[h] * scale).astype(bf16)

        @pl.when(qi == nq - 1)
        def _():
            dk_ref[...] = dk_acc[...].astype(bf16)
            dv_ref[...] = dv_acc[...].astype(bf16)

    c0, ng = OFF_SB // WB, NH // HB
    return pl.pallas_call(
        body, name=name, grid=(ng, nq),
        in_specs=[pl.BlockSpec((TQ, WB), lambda g, i: (i, c0 + g)), pl.BlockSpec((s, WB), lambda g, i: (0, c0 + ng + g)),
                  pl.BlockSpec((s, WB), lambda g, i: (0, c0 + 2 * ng + g)), pl.BlockSpec((TQ, WB), lambda g, i: (i, g))],
        out_specs=[pl.BlockSpec((TQ, WB), lambda g, i: (i, g)), pl.BlockSpec((s, WB), lambda g, i: (0, g)),
                   pl.BlockSpec((s, WB), lambda g, i: (0, g))],
        out_shape=[_sds((s, WH), bf16)] * 3,
        scratch_shapes=[pltpu.VMEM((nq, HB, TQ, TQ), f32), pltpu.VMEM((nq, HB, TQ, TQ), f32), pltpu.VMEM((s, WB), f32),
                        pltpu.VMEM((s, WB), f32)],
        compiler_params=_cp(("arbitrary", "arbitrary")),
    )(proj, proj, proj, dy)


CB = 256


def _shift_down(v, k, s):
    if k == 0:
        return v
    row = lax.broadcasted_iota(jnp.int32, v.shape, 0)
    return jnp.where(row < k, 0.0, pltpu.roll(v, k, axis=0))


def _shift_up(v, k, s):
    if k == 0:
        return v
    row = lax.broadcasted_iota(jnp.int32, v.shape, 0)
    return jnp.where(row >= s - k, 0.0, pltpu.roll(v, s - k, axis=0))


def conv_fwd(proj, w, name):
    s = proj.shape[0]

    def body(x_ref, w_ref, y_ref):
        x, wv = x_ref[...], w_ref[...]
        y = sum(wv[3 - k:4 - k, :] * _shift_down(x, k, s) for k in range(4))
        y_ref[...] = jax.nn.silu(y)

    return pl.pallas_call(
        body, name=name, grid=(3 * WH // CB,),
        in_specs=[pl.BlockSpec((s, CB), lambda j: (0, OFF_GQ // CB + j)), pl.BlockSpec((4, CB), lambda j: (0, j))],
        out_specs=pl.BlockSpec((s, CB), lambda j: (0, j)), out_shape=_sds((s, 3 * WH), f32),
        compiler_params=_cp(("parallel",)),
    )(proj, w)


def conv_bwd(proj, w, dc, name, after=()):
    s = proj.shape[0]

    def body(*refs):
        x_ref, w_ref, dc_ref, dx_ref, dw_ref = refs[len(after):]
        x, wv = x_ref[...], w_ref[...]
        xs = [_shift_down(x, k, s) for k in range(4)]
        y = sum(wv[3 - k:4 - k, :] * xs[k] for k in range(4))
        sig = jax.nn.sigmoid(y)
        dy = dc_ref[...] * (sig * (1.0 + y * (1.0 - sig)))
        dx_ref[...] = sum(wv[3 - k:4 - k, :] * _shift_up(dy, k, s) for k in range(4)).astype(bf16)
        dw_ref[...] = jnp.concatenate([jnp.sum(dy * xs[3 - i], axis=0, keepdims=True) for i in range(4)], axis=0)

    return pl.pallas_call(
        body, name=name, grid=(3 * WH // CB,),
        in_specs=[pl.BlockSpec(memory_space=pl.ANY)] * len(after)
        + [pl.BlockSpec((s, CB), lambda j: (0, OFF_GQ // CB + j)), pl.BlockSpec((4, CB), lambda j: (0, j)),
           pl.BlockSpec((s, CB), lambda j: (0, j))],
        out_specs=[pl.BlockSpec((s, CB), lambda j: (0, j)), pl.BlockSpec((4, CB), lambda j: (0, j))],
        out_shape=[_sds((s, 3 * WH), bf16), _sds((4, 3 * WH), f32)],
        compiler_params=_cp(("parallel",)),
    )(*after, proj, w, dc)


SOLVE_PASSES = 3


def _pdot_impl(a, b, dn, passes):
    ah, al = _split(a)
    bh, bl = _split(b)
    dot = lambda p, q: lax.dot_general(p, q, dn, preferred_element_type=f32)
    if passes == 1:
        return dot(ah, bh)
    if passes == 2:
        return dot(ah, bh) + dot(ah, bl)
    return dot(ah, bh) + (dot(ah, bl) + dot(al, bh))


BNN, BNT, BTN = (((2,), (1,)), ((0,), (0,))), (((2,), (2,)), ((0,), (0,))), (((1,), (1,)), ((0,), (0,)))


@functools.partial(jax.custom_vjp, nondiff_argnums=(2,))
def _bdot(a, b, passes):
    return _pdot_impl(a, b, BNN, passes)


def _bdot_fwd(a, b, passes):
    return _pdot_impl(a, b, BNN, passes), (a, b)


def _bdot_bwd(passes, res, ct):
    a, b = res
    return _pdot_impl(ct, b, BNT, passes), _pdot_impl(a, ct, BTN, passes)


_bdot.defvjp(_bdot_fwd, _bdot_bwd)


@functools.partial(jax.custom_vjp, nondiff_argnums=(2,))
def _bdot_nt(a, b, passes):
    return _pdot_impl(a, b, BNT, passes)


def _bdot_nt_fwd(a, b, passes):
    return _pdot_impl(a, b, BNT, passes), (a, b)


def _bdot_nt_bwd(passes, res, ct):
    a, b = res
    return _pdot_impl(ct, b, BNN, passes), _pdot_impl(ct, a, BTN, passes)


_bdot_nt.defvjp(_bdot_nt_fwd, _bdot_nt_bwd)


def _lane_pick(v, h):
    lane = lax.broadcasted_iota(jnp.int32, v.shape, v.ndim - 1)
    return jnp.sum(jnp.where(lane == h, v, 0.0), axis=-1, keepdims=True)


def _stack(parts):
    return jnp.concatenate([p[None] for p in parts], axis=0)


def _heads(v, first):
    return _stack([_lane_pick(v, first + h) for h in range(NH)])


def _l2n(v):
    return v * lax.rsqrt(jnp.sum(v * v, axis=-1, keepdims=True) + EPS)


def _dot_nt(a, b):
    return lax.dot_general(a, b, (((1,), (1,)), ((), ())), preferred_element_type=f32)


def _inverse_impl(lower):
    i = lax.broadcasted_iota(jnp.int32, (CH, CH), 0)
    j = lax.broadcasted_iota(jnp.int32, (CH, CH), 1)
    inv = (i == j).astype(f32) - lower
    pw = _pdot_impl(lower, lower, BNN, SOLVE_PASSES)
    for m in range(1, int(math.log2(CH))):
        inv = inv + _pdot_impl(inv, pw, BNN, SOLVE_PASSES)
        if m < int(math.log2(CH)) - 1:
            pw = _pdot_impl(pw, pw, BNN, SOLVE_PASSES)
    return inv


@jax.custom_vjp
def _unit_lower_inverse(lower):
    return _inverse_impl(lower)


def _unit_lower_inverse_fwd(lower):
    inv = _inverse_impl(lower)
    return inv, inv


def _unit_lower_inverse_bwd(inv, ct):
    return (-_pdot_impl(_pdot_impl(inv, ct, BTN, SOLVE_PASSES), inv, BNT, SOLVE_PASSES),)


_unit_lower_inverse.defvjp(_unit_lower_inverse_fwd, _unit_lower_inverse_bwd)


def _gdn_chunk(cq, ck, cv, ab, alog, dtb):
    ones = jnp.ones((NH, CH, HD), f32)
    q = _l2n(cq) * (HD ** -0.5)
    k = _l2n(ck)
    la = -jnp.exp(_heads(alog, 0)) * jax.nn.softplus(_heads(ab, AB_LANE) + _heads(dtb, 0))
    beta = jax.nn.sigmoid(_heads(ab, AB_LANE + NH)) * ones
    i = lax.broadcasted_iota(jnp.int32, (CH, CH), 0)
    j = lax.broadcasted_iota(jnp.int32, (CH, CH), 1)
    incl, strict = j <= i, j < i
    g = _bdot(jnp.broadcast_to(incl.astype(f32), (NH, CH, CH)), la * ones, 2)
    g_row = _stack([g[h].T for h in range(NH)])
    gamma = jnp.where(incl, jnp.exp(jnp.where(incl, g - g_row, 0.0)), 0.0)
    lower = jnp.where(strict, beta * _bdot_nt(k, k, 1) * gamma, 0.0)
    inv = _unit_lower_inverse(lower)
    eg = jnp.exp(g)
    u = _bdot(inv, cv * beta, SOLVE_PASSES)
    w = _bdot(inv, k * (beta * eg), SOLVE_PASSES)
    qk = _bdot_nt(q, k, 1) * gamma
    g_last = g[:, CH - 1:CH, :]
    return u, w, q * eg, k * jnp.exp(g_last - g), qk, jnp.exp(g_last)


def _by_head(ref, t=0):
    return _stack([ref[:, t * WH + h * HD:t * WH + (h + 1) * HD] for h in range(NH)])


def gdn_a_fwd(c, proj, alog, dtb, name):
    s = c.shape[0]
    nc = s // CH

    def body(c_ref, ab_ref, al_ref, dt_ref, u_ref, w_ref, qd_ref, kd_ref, qk_ref, dec_ref):
        res = _gdn_chunk(_by_head(c_ref, 0), _by_head(c_ref, 1), _by_head(c_ref, 2), ab_ref[...], al_ref[...], dt_ref[...])
        for h in range(NH):
            sl = slice(h * HD, (h + 1) * HD)
            for r, v in zip((u_ref, w_ref, qd_ref, kd_ref, qk_ref), res[:5]):
                r[:, sl] = v[h]
            dec_ref[:, sl] = jnp.broadcast_to(res[5][h], (8, HD))

    row = pl.BlockSpec((CH, WH), lambda n: (n, 0))
    par = pl.BlockSpec((1, HD), lambda n: (0, 0))
    return pl.pallas_call(
        body, name=name, grid=(nc,),
        in_specs=[pl.BlockSpec((CH, 3 * WH), lambda n: (n, 0)), pl.BlockSpec((CH, HD), lambda n: (n, OFF_AB // HD)), par, par],
        out_specs=[row] * 5 + [pl.BlockSpec((None, 8, WH), lambda n: (n, 0, 0))],
        out_shape=[_sds((s, WH), f32)] * 5 + [_sds((nc, 8, WH), f32)],
        compiler_params=_cp(("parallel",)),
    )(c, proj, alog, dtb)


def gdn_a_bwd(c, proj, alog, dtb, cots, ddec, name):
    s = c.shape[0]
    nc = s // CH

    def body(c_ref, ab_ref, al_ref, dt_ref, du_ref, dw_ref, dqd_ref, dkd_ref, dqk_ref, ddec_ref,
             dc_ref, dab_ref, dal_ref, ddt_ref):
        n = pl.program_id(0)
        _, vjp = jax.vjp(_gdn_chunk, _by_head(c_ref, 0), _by_head(c_ref, 1), _by_head(c_ref, 2), ab_ref[...], al_ref[...],
                         dt_ref[...])
        lane = lax.broadcasted_iota(jnp.int32, (1, HD), 1)
        dd = _stack([jnp.where(lane == 0, ddec_ref[0:1, h * HD:(h + 1) * HD], 0.0) for h in range(NH)])
        dcq, dck, dcv, dab, dal, ddt = vjp(tuple(_by_head(r) for r in (du_ref, dw_ref, dqd_ref, dkd_ref, dqk_ref)) + (dd,))
        for h in range(NH):
            for t, v in enumerate((dcq, dck, dcv)):
                dc_ref[:, t * WH + h * HD:t * WH + (h + 1) * HD] = v[h]
        dab_ref[...] = dab.astype(bf16)

        @pl.when(n == 0)
        def _():
            dal_ref[...] = dal
            ddt_ref[...] = ddt

        @pl.when(n > 0)
        def _():
            dal_ref[...] += dal
            ddt_ref[...] += ddt

    row = pl.BlockSpec((CH, WH), lambda n: (n, 0))
    wide = pl.BlockSpec((CH, 3 * WH), lambda n: (n, 0))
    par = pl.BlockSpec((1, HD), lambda n: (0, 0))
    return pl.pallas_call(
        body, name=name, grid=(nc,),
        in_specs=[wide, pl.BlockSpec((CH, HD), lambda n: (n, OFF_AB // HD)), par, par] + [row] * 5
        + [pl.BlockSpec((None, 8, WH), lambda n: (n, 0, 0))],
        out_specs=[wide, pl.BlockSpec((CH, HD), lambda n: (n, 0)), par, par],
        out_shape=[_sds((s, 3 * WH), f32), _sds((s, HD), bf16), _sds((1, HD), f32), _sds((1, HD), f32)],
        compiler_params=_cp(("arbitrary",)),
    )(c, proj, alog, dtb, *cots, ddec)


def gdn_b_fwd(u, w, qd, kd, qk, dec, name):
    s = u.shape[0]
    nc = s // CH

    def body(u_ref, w_ref, qd_ref, kd_ref, qk_ref, dec_ref, o_ref, st_ref, state):
        n = pl.program_id(0)

        @pl.when(n == 0)
        def _():
            state[...] = jnp.zeros_like(state)

        for h in range(NH):
            sl = slice(h * HD, (h + 1) * HD)
            st = state[sl, :]
            st_ref[sl, :] = st
            sb = st.astype(bf16)
            vn = u_ref[:, sl] - jnp.dot(w_ref[:, sl].astype(bf16), sb, preferred_element_type=f32)
            vb = vn.astype(bf16)
            o_ref[:, sl] = (jnp.dot(qd_ref[:, sl].astype(bf16), sb, preferred_element_type=f32)
                            + jnp.dot(qk_ref[:, sl].astype(bf16), vb, preferred_element_type=f32))
            state[sl, :] = st * dec_ref[0:1, sl] + lax.dot_general(
                kd_ref[:, sl].astype(bf16), vb, (((0,), (0,)), ((), ())), preferred_element_type=f32)

    row = pl.BlockSpec((CH, WH), lambda n: (n, 0))
    return pl.pallas_call(
        body, name=name, grid=(nc,),
        in_specs=[row] * 5 + [pl.BlockSpec((None, 8, WH), lambda n: (n, 0, 0))],
        out_specs=[row, pl.BlockSpec((None, WH, HD), lambda n: (n, 0, 0))],
        out_shape=[_sds((s, WH), f32), _sds((nc, WH, HD), f32)],
        scratch_shapes=[pltpu.VMEM((WH, HD), f32)],
        compiler_params=_cp(("arbitrary",)),
    )(u, w, qd, kd, qk, dec)


def gdn_b_bwd(u, w, qd, kd, qk, dec, states, do, name):
    s = u.shape[0]
    nc = s // CH

    def body(u_ref, w_ref, qd_ref, kd_ref, qk_ref, dec_ref, st_ref, do_ref,
             du_ref, dw_ref, dqd_ref, dkd_ref, dqk_ref, ddec_ref, dstate):
        n = pl.program_id(0)

        @pl.when(n == 0)
        def _():
            dstate[...] = jnp.zeros_like(dstate)

        for h in range(NH):
            sl = slice(h * HD, (h + 1) * HD)
            st, ds = st_ref[sl, :], dstate[sl, :]
            sb, dsb = st.astype(bf16), ds.astype(bf16)
            wb, qdb, kdb, qkb = (r[:, sl].astype(bf16) for r in (w_ref, qd_ref, kd_ref, qk_ref))
            dob = do_ref[:, sl].astype(bf16)
            vn = u_ref[:, sl] - jnp.dot(wb, sb, preferred_element_type=f32)
            vb = vn.astype(bf16)
            dvn = (lax.dot_general(qkb, dob, (((0,), (0,)), ((), ())), preferred_element_type=f32)
                   + jnp.dot(kdb, dsb, preferred_element_type=f32))
            dvb = dvn.astype(bf16)
            du_ref[:, sl] = dvn
            dw_ref[:, sl] = -_dot_nt(dvb, sb)
            dqd_ref[:, sl] = _dot_nt(dob, sb)
            dkd_ref[:, sl] = _dot_nt(vb, dsb)
            dqk_ref[:, sl] = _dot_nt(dob, vb)
            tot = jnp.sum(jnp.sum(ds * st, axis=1, keepdims=True), axis=0, keepdims=True)
            ddec_ref[:, sl] = jnp.broadcast_to(tot, (8, HD))
            dstate[sl, :] = (ds * dec_ref[0:1, sl]
                             + lax.dot_general(qdb, dob, (((0,), (0,)), ((), ())), preferred_element_type=f32)
                             - lax.dot_general(wb, dvb, (((0,), (0,)), ((), ())), preferred_element_type=f32))

    row = pl.BlockSpec((CH, WH), lambda n: (nc - 1 - n, 0))
    small = pl.BlockSpec((None, 8, WH), lambda n: (nc - 1 - n, 0, 0))
    return pl.pallas_call(
        body, name=name, grid=(nc,),
        in_specs=[row] * 5 + [small, pl.BlockSpec((None, WH, HD), lambda n: (nc - 1 - n, 0, 0)), row],
        out_specs=[row] * 5 + [small],
        out_shape=[_sds((s, WH), f32)] * 5 + [_sds((nc, 8, WH), f32)],
        scratch_shapes=[pltpu.VMEM((WH, HD), f32)],
        compiler_params=_cp(("arbitrary",)),
    )(u, w, qd, kd, qk, dec, states, do)


ANY = pl.BlockSpec(memory_space=pl.ANY)


def _dev_index(p):
    return 4 * p[0] + 2 * p[1] + p[2]


def _shard_of(ref, axis, size, idx):
    return ref.at[pl.ds(idx * size, size), :] if axis == 0 else ref.at[:, pl.ds(idx * size, size)]


def _peer(x, y, c, r):
    return (1 - x if r & 4 else x, 1 - y if r & 2 else y, 1 - c if r & 1 else c)


ALL_PEERS, SIBLING, OTHER_CHIPS = tuple(range(1, NDEV)), (1,), (4, 2, 6)


def _launch(body, ins, out_sds, name, sequencer_id, relations=ALL_PEERS, kinds=7):
    n = len(ins)
    sems = [pltpu.SemaphoreType.DMA((n, kinds)), pltpu.SemaphoreType.DMA((n, kinds)), pltpu.SemaphoreType.DMA((n,))]
    if sequencer_id is None:
        return pl.pallas_call(
            lambda *refs: body(refs[:n], refs[n:2 * n], *refs[2 * n:]), name=name, in_specs=[ANY] * n, out_specs=[ANY] * n,
            out_shape=out_sds, scratch_shapes=sems, compiler_params=pltpu.CompilerParams(has_side_effects=True),
        )(*ins)
    in_refs = [jax.new_ref(a, memory_space=pltpu.MemorySpace.HBM) for a in ins]
    out_refs = [jax.empty_ref(sd, memory_space=pltpu.MemorySpace.HBM) for sd in out_sds]

    @pl.kernel(mesh=plsc.ScalarSubcoreMesh(axis_name="sequencer", num_cores=1), name=name, scratch_types=sems,
               compiler_params=pltpu.CompilerParams(collective_id=sequencer_id))
    def launch(send, recv, loc):
        x, y, c = lax.axis_index("x"), lax.axis_index("y"), lax.axis_index("c")
        barrier = pltpu.get_barrier_semaphore()
        for r in relations:
            pl.semaphore_signal(barrier, inc=1, device_id=_peer(x, y, c, r), device_id_type=MESH)
        pl.semaphore_wait(barrier, len(relations))
        body(in_refs, out_refs, send, recv, loc)

    launch()
    return [r[...] for r in out_refs]


def all_gather(xs, axes, name, sequencer_id=None):
    n = len(xs)
    fulls = [tuple(d * (NDEV if a == ax else 1) for a, d in enumerate(x.shape)) for x, ax in zip(xs, axes)]
    halved = [x.shape[0] % 32 == 0 for x in xs]

    def body(x_refs, o_refs, send, recv, loc):
        x, y, c = lax.axis_index("x"), lax.axis_index("y"), lax.axis_index("c")
        me, sib = (x, y, c), (x, y, 1 - c)
        xn, yn, dg = (1 - x, y), (x, 1 - y), (1 - x, 1 - y)

        def part(t, p, half=None):
            ref = _shard_of(o_refs[t], axes[t], xs[t].shape[axes[t]], _dev_index(p))
            rows = xs[t].shape[0] // 2
            return ref if half is None else ref.at[pl.ds(half * rows, rows), :]

        def copy(t, k, block, to, half=None, src=None):
            return pltpu.make_async_remote_copy(
                src_ref=part(t, block, half) if src is None else src, dst_ref=part(t, block, half),
                send_sem=send.at[t, k], recv_sem=recv.at[t, k], device_id=to, device_id_type=MESH)

        mine = [pltpu.make_async_copy(x_refs[t], part(t, me), loc.at[t]) for t in range(n)]
        for cp in mine:
            cp.start()
        sends = []
        for t in range(n):
            sends += [copy(t, 0, me, sib, src=x_refs[t]), copy(t, 1, me, (*xn, c), src=x_refs[t]),
                      copy(t, 2, me, (*yn, c), src=x_refs[t])]
            if not halved[t]:
                sends.append(copy(t, 3, me, (*dg, c), src=x_refs[t]))
        for cp in sends:
            cp.start()

        def pass_on(cp):
            cp.start()
            sends.append(cp)

        for t in range(n):
            h0, h1 = (0, 1) if halved[t] else (None, None)
            copy(t, 1, (*xn, c), me).wait_recv()
            if halved[t]:
                pass_on(copy(t, 3, (*xn, c), (*yn, c), 0))
            pass_on(copy(t, 5, (*xn, c), sib))
            copy(t, 2, (*yn, c), me).wait_recv()
            if halved[t]:
                pass_on(copy(t, 4, (*yn, c), (*xn, c), 1))
            pass_on(copy(t, 6, (*yn, c), sib))
            copy(t, 3, (*dg, c), me, h0).wait_recv()
            pass_on(copy(t, 7, (*dg, c), sib, h0))
            if halved[t]:
                copy(t, 4, (*dg, c), me, h1).wait_recv()
                pass_on(copy(t, 8, (*dg, c), sib, h1))
        for t in range(n):
            h0, h1 = (0, 1) if halved[t] else (None, None)
            copy(t, 0, sib, me).wait_recv()
            copy(t, 5, (*xn, 1 - c), me).wait_recv()
            copy(t, 6, (*yn, 1 - c), me).wait_recv()
            copy(t, 7, (*dg, 1 - c), me, h0).wait_recv()
            if halved[t]:
                copy(t, 8, (*dg, 1 - c), me, h1).wait_recv()
        for cp in sends:
            cp.wait_send()
        for cp in mine:
            cp.wait()

    return _launch(body, xs, [_sds(f, x.dtype) for f, x in zip(fulls, xs)], name, sequencer_id, kinds=9)


def pair_swap(gs, axes, name, sequencer_id):
    n = len(gs)
    shards = [tuple(d // (NDEV if a == ax else 1) for a, d in enumerate(g.shape)) for g, ax in zip(gs, axes)]

    def body(g_refs, p_refs, send, recv, loc):
        x, y, c = lax.axis_index("x"), lax.axis_index("y"), lax.axis_index("c")

        def copy(t, j):
            return pltpu.make_async_remote_copy(
                src_ref=_shard_of(g_refs[t], axes[t], shards[t][axes[t]], 2 * j + (1 - c)), dst_ref=p_refs[t].at[j],
                send_sem=send.at[t, j], recv_sem=recv.at[t, j], device_id=(x, y, 1 - c), device_id_type=MESH)

        copies = [copy(t, j) for t in range(n) for j in range(4)]
        for cp in copies:
            cp.start()
        for cp in copies:
            cp.wait_recv()
        for cp in copies:
            cp.wait_send()

    return _launch(body, gs, [_sds((4,) + sh, g.dtype) for sh, g in zip(shards, gs)], name, sequencer_id, SIBLING)


def pair_add(g, p, axis, name, after=()):
    _, rows, cols = p.shape
    tr = _row_tile(rows, cols, 1 << 20)
    nb = rows // tr
    if axis == 0:
        g_spec = pl.BlockSpec((tr, cols), lambda j, i, c: ((2 * j + c[0]) * nb + i, 0))
    else:
        g_spec = pl.BlockSpec((tr, cols), lambda j, i, c: (i, 2 * j + c[0]))
    blk = pl.BlockSpec((None, tr, cols), lambda j, i, c: (j, i, 0))

    na = len(after)

    def body(c_ref, *refs):
        g_ref, p_ref, q_ref = refs[na:]
        q_ref[...] = (g_ref[...].astype(f32) + p_ref[...].astype(f32)).astype(bf16)

    return pl.pallas_call(
        body, name=name, out_shape=_sds(p.shape, bf16),
        grid_spec=pltpu.PrefetchScalarGridSpec(num_scalar_prefetch=1, grid=(4, nb), in_specs=[ANY] * na + [g_spec, blk],
                                               out_specs=blk),
        compiler_params=_cp(("parallel", "parallel")),
    )(lax.axis_index("c").astype(jnp.int32).reshape(1), *after, g, p)


def chip_exchange(qs, name, sequencer_id):
    n = len(qs)

    def body(q_refs, r_refs, send, recv, loc):
        x, y, c = lax.axis_index("x"), lax.axis_index("y"), lax.axis_index("c")
        my_chip = 2 * x + y
        peers = [_peer(x, y, c, r) for r in OTHER_CHIPS]
        mine = [pltpu.make_async_copy(q_refs[t].at[my_chip], r_refs[t].at[my_chip], loc.at[t]) for t in range(n)]
        for cp in mine:
            cp.start()

        def copy(t, k, slot):
            p = peers[k]
            return pltpu.make_async_remote_copy(
                src_ref=q_refs[t].at[2 * p[0] + p[1]], dst_ref=r_refs[t].at[slot], send_sem=send.at[t, k], recv_sem=recv.at[t, k],
                device_id=p, device_id_type=MESH)

        sends = [copy(t, k, my_chip) for t in range(n) for k in range(3)]
        for cp in sends:
            cp.start()
        for t in range(n):
            for k in range(3):
                copy(t, k, 2 * peers[k][0] + peers[k][1]).wait_recv()
        for cp in sends:
            cp.wait_send()
        for cp in mine:
            cp.wait()

    return _launch(body, qs, [_sds(q.shape, q.dtype) for q in qs], name, sequencer_id, OTHER_CHIPS)


def _adamw(w, g, m, v):
    m = ADAM_B1 * m + (1.0 - ADAM_B1) * g
    v = ADAM_B2 * v + (1.0 - ADAM_B2) * jnp.square(g)
    m_hat = m / (1.0 - ADAM_B1 ** ADAM_STEP)
    v_hat = v / (1.0 - ADAM_B2 ** ADAM_STEP)
    return -ADAM_LR * (m_hat / (jnp.sqrt(v_hat) + ADAM_EPS) + ADAM_WD * w), m, v


def _sum8(r_ref):
    g = r_ref[0].astype(f32)
    for j in range(1, r_ref.shape[0]):
        g = g + r_ref[j].astype(f32)
    return g


def _row_tile(rows, cols, elems=1 << 18):
    tr = min(rows, max(8, 1 << int(math.log2(elems / cols))))
    while rows % tr:
        tr //= 2
    assert tr % 8 == 0 or tr == rows, (rows, cols)
    return tr


def sum_partials(r, name, after=()):
    _, rows, cols = r.shape
    tr = _row_tile(rows, cols)
    na = len(after)

    def body(*refs):
        refs[na + 1][...] = _sum8(refs[na])

    return pl.pallas_call(
        body, name=name, grid=(rows // tr,),
        in_specs=[ANY] * na + [pl.BlockSpec((r.shape[0], tr, cols), lambda i: (0, i, 0))],
        out_specs=pl.BlockSpec((tr, cols), lambda i: (i, 0)), out_shape=_sds((rows, cols), f32),
        compiler_params=_cp(("parallel",)),
    )(*after, r)


def adamw_t(w, m, v, grads, name):
    rows, nl, cols = w.shape
    tr = min(rows, (1 << 16) // cols)
    blk = pl.BlockSpec((tr, nl, cols), lambda i: (i, 0, 0))
    flat = pl.BlockSpec((tr, cols), lambda i: (i, 0))

    def body(w_ref, m_ref, v_ref, *rest):
        g_refs, (g_ref, d_ref, nm_ref, nv_ref) = rest[:nl], rest[nl:]
        for l in range(nl):
            grad = g_refs[l][...]
            g_ref[:, l, :] = grad
            d_ref[:, l, :], nm_ref[:, l, :], nv_ref[:, l, :] = _adamw(w_ref[:, l, :], grad, m_ref[:, l, :], v_ref[:, l, :])

    return tuple(pl.pallas_call(
        body, name=name, grid=(pl.cdiv(rows, tr),), in_specs=[blk] * 3 + [flat] * nl, out_specs=[blk] * 4,
        out_shape=[_sds(w.shape, f32)] * 4, compiler_params=_cp(("parallel",)),
    )(w, m, v, *grads))


def adamw(w, m, v, layer, name, r=None, g=None, prev=None):
    _, rows, cols = w.shape
    tr = _row_tile(rows, cols)
    blk = pl.BlockSpec((None, tr, cols), lambda i: (layer, i, 0))
    nprev = 0 if prev is None else 4

    def body(w_ref, m_ref, v_ref, src_ref, *rest):
        g_ref, d_ref, nm_ref, nv_ref, token_ref = rest[nprev:]
        grad = _sum8(src_ref) if g is None else src_ref[...]
        g_ref[...] = grad
        d_ref[...], nm_ref[...], nv_ref[...] = _adamw(w_ref[...], grad, m_ref[...], v_ref[...])
        token_ref[...] = jnp.zeros_like(token_ref)

    src, src_spec = (r, pl.BlockSpec((r.shape[0], tr, cols), lambda i: (0, i, 0))) if g is None else (g, pl.BlockSpec((tr, cols), lambda i: (i, 0)))
    *outs, token = pl.pallas_call(
        body, name=name, grid=(rows // tr,), in_specs=[blk] * 3 + [src_spec] + [ANY] * nprev,
        out_specs=[blk] * 4 + [pl.BlockSpec((8, HD), lambda i: (0, 0))],
        out_shape=[_sds(w.shape, f32)] * 4 + [_sds((8, HD), f32)], input_output_aliases={4 + k: k for k in range(nprev)},
        compiler_params=_cp(("arbitrary",)),
    )(w, m, v, src, *(prev or ()))
    return tuple(outs), token


def small_adamw(parts, w, m, v, name, after=()):
    def body(*refs):
        p_ref, w_ref, m_ref, v_ref, g_ref, d_ref, nm_ref, nv_ref = refs[len(after):]
        g = _sum8(p_ref)
        g_ref[...] = g
        d_ref[...], nm_ref[...], nv_ref[...] = _adamw(w_ref[...], g, m_ref[...], v_ref[...])

    vmem = pl.BlockSpec(memory_space=pltpu.VMEM)
    return pl.pallas_call(body, name=name, out_shape=[_sds(w.shape, f32)] * 4, in_specs=[ANY] * len(after) + [vmem] * 4,
                          out_specs=[vmem] * 4, compiler_params=_cp())(*after, parts, w, m, v)


def _pack_rows(wt):
    tail = jnp.pad(wt[5888:N_IN], ((OFF_G - 12 - (OFF_P + 512), 0), (0, 0)))
    return jnp.concatenate([wt[512:2816], wt[2816:5120], wt[5120:5888], wt[0:512], tail], axis=0)


def _unpack_rows(g):
    return jnp.concatenate([g[OFF_P:OFF_P + 512], g[OFF_SB:OFF_SB + 2304], g[OFF_GQ:OFF_GQ + 2304], g[OFF_Z:OFF_Z + 768],
                            g[OFF_G - 12:NP]], axis=0)


def _lanes(v):
    flat = v.reshape(-1)
    n = -(-flat.shape[0] // HD) * HD
    return jnp.pad(flat, (0, n - flat.shape[0])).reshape(n // HD, HD)


def _lanes8(v):
    rows = _lanes(v)
    return jnp.pad(rows, ((0, -rows.shape[0] % 8), (0, 0)))


def _layer_fwd(x, p, l):
    nm = lambda s: f"{s}_l{l}"
    u = rms_fwd(x, p["attn_norm"], nm("rms1"))
    proj = matmul(u, p["w_in"][0], name=nm("inproj"), tb=True)
    gates = matmul(u, p["w_in"][1], name=nm("inproj_gates"), tb=True)
    y_pool = pool_fwd(proj, p["pool_w"], p["pool_scale"], nm("pool"))
    y_sb = sb_fwd(proj, nm("sb"))
    c = conv_fwd(proj, p["conv"], nm("conv"))
    ga = gdn_a_fwd(c, proj, p["alog"], p["dtb"], nm("gdna"))
    o, states = gdn_b_fwd(*ga, nm("gdnb"))
    y_gdn = gdn_out_fwd(o, proj, p["gdn_norm"], nm("gdno"))
    ups = [matmul(y, p[k], name=nm(k)) for y, k in ((y_pool, "w_pool_up"), (y_sb, "w_sb_up"), (y_gdn, "w_gdn_up"))]
    merged = merge_fwd(gates, ups, nm("merge"))
    x1 = matmul(merged, p["w_out"], name=nm("outproj"), epilogue=lambda acc, r: acc + r, extras=(x,))
    u2 = rms_fwd(x1, p["mlp_norm"], nm("rms2"))
    h2 = matmul(u2, p["w_ff1"], name=nm("ff1"), out_dtype=bf16, epilogue=lambda acc: jnp.square(jnp.maximum(acc, 0.0)))
    x2 = matmul(h2, p["w_ff2"], name=nm("ff2"), epilogue=lambda acc, r: acc + r, extras=(x1,))
    saved = dict(x=x, u=u, proj=proj, gates=gates, y_pool=y_pool, y_sb=y_sb, c=c, ga=ga, o=o, states=states, y_gdn=y_gdn, ups=ups,
                 merged=merged, x1=x1, u2=u2, h2=h2)
    return x2, saved


def _layer_bwd(dx2, dx2b, sv, p, l, swap, finish):
    nm = lambda s: f"{s}_l{l}"
    s = dx2.shape[0]
    dh = matmul(dx2b, p["w_ff2"], name=nm("d_ff2_x"), tb=True, out_dtype=bf16,
                epilogue=lambda acc, h2: acc * (2.0 * jnp.sqrt(h2.astype(f32))), extras=(sv["h2"],))
    g_ff2 = matmul(sv["h2"], dx2b, name=nm("d_ff2_w"), ta=True, out_dtype=bf16)
    du2 = matmul(dh, p["w_ff1"], name=nm("d_ff1_x"), tb=True)
    g_ff1 = matmul(sv["u2"], dh, name=nm("d_ff1_w"), ta=True, out_dtype=bf16)
    dx1, dx1b, d_mlp_norm = rms_bwd(sv["x1"], du2, dx2, p["mlp_norm"], nm("d_rms2"))
    dmerged = matmul(dx1b, p["w_out"], name=nm("d_out_x"), tb=True, out_dtype=bf16)
    g_out = matmul(sv["merged"], dx1b, name=nm("d_out_w"), ta=True, out_dtype=bf16)
    dgates, *dups = merge_bwd(sv["gates"], sv["ups"], dmerged, nm("d_merge"))
    dys, g_ups = [], []
    for dup, y, k in zip(dups, (sv["y_pool"], sv["y_sb"], sv["y_gdn"]), ("w_pool_up", "w_sb_up", "w_gdn_up")):
        dys.append(matmul(dup, p[k], name=nm("d_" + k + "_x"), tb=True, out_dtype=bf16))
        g_ups.append(matmul(y, dup, name=nm("d_" + k + "_w"), ta=True, out_dtype=bf16))
    early = g_ups + [g_out, g_ff1, g_ff2]
    swapped = swap(early, BIG_AXES[1:], "a")
    do, dz, d_gdn_norm = gdn_out_bwd(sv["o"], sv["proj"], p["gdn_norm"], dys[2], nm("d_gdno"))
    cots = gdn_b_bwd(*sv["ga"], sv["states"], do, nm("d_gdnb"))
    dc, dab, d_alog, d_dtb = gdn_a_bwd(sv["c"], sv["proj"], p["alog"], p["dtb"], cots[:5], cots[5], nm("d_gdna"))
    r_early, sent_early = finish(early, swapped, BIG_AXES[1:], BIG_NAMES[1:], "a", (dab,))
    dgq, d_conv = conv_bwd(sv["proj"], p["conv"], dc, nm("d_conv"), sent_early)
    dq, dk, dv = sb_bwd(sv["proj"], dys[1], nm("d_sb"))
    dp, d_pool_w, d_pool_scale = pool_bwd(sv["proj"], p["pool_w"], p["pool_scale"], dys[0], nm("d_pool"))
    dproj = jnp.concatenate([dq, dk, dv, dgq, dz, dp, jnp.zeros((s, OFF_AB - OFF_P - W_POOL), bf16), dab], axis=1)
    g_in = [matmul(dproj, sv["u"], name=nm("d_in_w"), ta=True, out_dtype=bf16),
            matmul(dgates, sv["u"], name=nm("d_in_w_gates"), ta=True, out_dtype=bf16)]
    r_in, sent_in = finish(g_in, swap(g_in, (1, 1), "b"), (1, 1), ("w_in", "w_in_gates"), "b", ())
    du = matmul(dproj, p["w_in"][0], name=nm("d_in_x"), after=sent_in)
    du = matmul(dgates, p["w_in"][1], name=nm("d_in_x_gates"), epilogue=lambda acc, r: acc + r, extras=(du,))
    dx, dxb, d_attn_norm = rms_bwd(sv["x"], du, dx1, p["attn_norm"], nm("d_rms1"))
    recv = r_in + r_early
    small = [d_attn_norm, d_pool_w, d_pool_scale, d_conv, d_alog, d_dtb, d_gdn_norm, d_mlp_norm]
    return dx, dxb, recv, small


BIG_AXES = (1, 1, 1, 1, 0, 1, 0)
GATHER_ID, EXCHANGE_ID = 1, 2
BIG_NAMES = ("w_in", "w_pool_up", "w_sb_up", "w_gdn_up", "w_out", "w_ff1", "w_ff2")


def kernel(x, attn_norm, w_in, pool_w, pool_scale, gdn_conv, gdn_a_log, gdn_dt_bias, gdn_norm, w_pool_up, w_sb_up, w_gdn_up, w_out, mlp_norm, w_ff1, w_ff2, final_norm, loss_target, m_attn_norm, m_w_in, m_pool_w, m_pool_scale, m_gdn_conv, m_gdn_a_log, m_gdn_dt_bias, m_gdn_norm, m_w_pool_up, m_w_sb_up, m_w_gdn_up, m_w_out, m_mlp_norm, m_w_ff1, m_w_ff2, m_final_norm, v_attn_norm, v_w_in, v_pool_w, v_pool_scale, v_gdn_conv, v_gdn_a_log, v_gdn_dt_bias, v_gdn_norm, v_w_pool_up, v_w_sb_up, v_w_gdn_up, v_w_out, v_mlp_norm, v_w_ff1, v_w_ff2, v_final_norm):
    s = x.shape[1]
    me = _dev_index((lax.axis_index("x"), lax.axis_index("y"), lax.axis_index("c")))
    ncv = gdn_conv.shape[2]

    w_in_t, m_w_in_t, v_w_in_t = (jnp.transpose(a, (2, 0, 1)) for a in (w_in, m_w_in, v_w_in))
    full = []
    for l in range(NL):
        packed_in = _pack_rows(w_in_t[:, l]).astype(bf16)
        in_mix, in_gates = packed_in[:OFF_G], packed_in[OFF_G:]
        shards = [w_pool_up[l].astype(bf16), w_sb_up[l].astype(bf16), w_gdn_up[l].astype(bf16), w_out[l].astype(bf16),
                  w_ff1[l].astype(bf16), w_ff2[l].astype(bf16)]
        if l == 0:
            first = all_gather([in_mix, gdn_conv.reshape(NL * 4, ncv)], (1, 0), "gather_first", sequencer_id=GATHER_ID)
            rest = all_gather([in_gates] + shards, (1,) + BIG_AXES[1:], "gather_rest_l0", sequencer_id=GATHER_ID)
            full.append([(first[0], rest[0])] + rest[1:])
            conv_full = first[1].reshape(NDEV, NL, 4, ncv).transpose(1, 2, 0, 3).reshape(NL, 4, NDEV * ncv)
        else:
            first = all_gather([in_mix, in_gates], (1, 1), f"gather_in_l{l}", sequencer_id=GATHER_ID)
            full.append([tuple(first)] + all_gather(shards, BIG_AXES[1:], f"gather_rest_l{l}", sequencer_id=GATHER_ID))
    params = []
    for l in range(NL):
        p = dict(zip(("w_in", "w_pool_up", "w_sb_up", "w_gdn_up", "w_out", "w_ff1", "w_ff2"), full[l][:7]))
        p.update(attn_norm=attn_norm[l][None], mlp_norm=mlp_norm[l][None], pool_w=pool_w[l], pool_scale=pool_scale[l][None],
                 conv=conv_full[l], alog=_lanes(gdn_a_log[l]), dtb=_lanes(gdn_dt_bias[l]), gdn_norm=gdn_norm[l][None])
        params.append(p)

    h = x[0]
    saved = []
    for l in range(NL):
        h, sv = _layer_fwd(h, params[l], l)
        saved.append(sv)
    dh, dhb, loss_row, d_final = loss_head(h, loss_target[0], final_norm[None], "loss_head")
    recv, smalls = [None] * NL, [None] * NL
    for l in reversed(range(NL)):
        def swap(gs, axes, tag, l=l):
            return pair_swap(gs, axes, f"swap_{tag}_l{l}", None)

        def finish(gs, ps, axes, names, tag, after, l=l):
            qs = [pair_add(g, p, ax, f"pair_add_{k}_l{l}", after) for g, p, ax, k in zip(gs, ps, axes, names)]
            return chip_exchange(qs, f"exchange_{tag}_l{l}", EXCHANGE_ID), tuple(qs)

        dh, dhb, recv[l], smalls[l] = _layer_bwd(dh, dhb, saved[l], params[l], l, swap, finish)

    small_rows = [_lanes8(t) for l in range(NL) for t in smalls[l]] + [_lanes8(d_final), _lanes8(loss_row)]
    packed = jnp.concatenate(small_rows, axis=0)
    parts = all_gather([packed], (0,), "gather_small", sequencer_id=GATHER_ID)[0].reshape(NDEV, packed.shape[0], HD)

    def pack_small(tree):
        rows = []
        for l in range(NL):
            rows += [_lanes8(tree["attn_norm"][l]), _lanes8(tree["pool_w"][l]), _lanes8(tree["pool_scale"][l]),
                     jnp.zeros((4 * NDEV * ncv // HD, HD), f32), _lanes8(tree["gdn_a_log"][l]), _lanes8(tree["gdn_dt_bias"][l]),
                     _lanes8(tree["gdn_norm"][l]), _lanes8(tree["mlp_norm"][l])]
        rows += [_lanes8(tree["final_norm"]), jnp.zeros((8, HD), f32)]
        return jnp.concatenate(rows, axis=0)

    names = ("attn_norm", "pool_w", "pool_scale", "gdn_a_log", "gdn_dt_bias", "gdn_norm", "mlp_norm", "final_norm")
    w_small = pack_small(dict(zip(names, (attn_norm, pool_w, pool_scale, gdn_a_log, gdn_dt_bias, gdn_norm, mlp_norm, final_norm))))
    m_small = pack_small(dict(zip(names, (m_attn_norm, m_pool_w, m_pool_scale, m_gdn_a_log, m_gdn_dt_bias, m_gdn_norm, m_mlp_norm, m_final_norm))))
    v_small = pack_small(dict(zip(names, (v_attn_norm, v_pool_w, v_pool_scale, v_gdn_a_log, v_gdn_dt_bias, v_gdn_norm, v_mlp_norm, v_final_norm))))
    def unpack_small(buf):
        out, conv_g, r = {}, [], 0
        layer_items = (("attn_norm", (D,)), ("pool_w", (4, HD, HD)), ("pool_scale", (W_POOL,)), ("conv", (4, NDEV * ncv)),
                       ("gdn_a_log", (NH,)), ("gdn_dt_bias", (NH,)), ("gdn_norm", (HD,)), ("mlp_norm", (D,)))
        per_layer = {k: [] for k, _ in layer_items}
        for l in range(NL):
            for k, shape in layer_items:
                size = math.prod(shape)
                nrow = -(-size // (8 * HD)) * 8
                per_layer[k].append(buf[r:r + nrow].reshape(-1)[:size].reshape(shape))
                r += nrow
        for k, _ in layer_items:
            out[k] = jnp.stack(per_layer[k])
        out["final_norm"] = buf[r:r + D // HD].reshape(D)
        out["loss"] = buf[r + D // HD, 0]
        return out

    big_out = {}
    big_names = BIG_NAMES
    big_w = dict(zip(big_names, ((w_in_t, m_w_in_t, v_w_in_t), (w_pool_up, m_w_pool_up, v_w_pool_up), (w_sb_up, m_w_sb_up, v_w_sb_up),
                                 (w_gdn_up, m_w_gdn_up, v_w_gdn_up), (w_out, m_w_out, v_w_out), (w_ff1, m_w_ff1, v_w_ff1),
                                 (w_ff2, m_w_ff2, v_w_ff2))))
    tokens = []
    for l, k in [(l, k) for l in reversed(range(NL)) for k in big_names[1:]]:
        w, m, v = big_w[k]
        big_out[k], token = adamw(w, m, v, l, f"adamw_{k}_l{l}", r=recv[l][big_names.index(k) + 1], prev=big_out.get(k))
        tokens.append(token)
    g_in = []
    for l in range(NL):
        halves = [sum_partials(recv[l][h], f"sum_w_in_{h}_l{l}", tuple(tokens) if l == 0 else ()) for h in range(2)]
        g_in.append(_unpack_rows(jnp.concatenate(halves, axis=0)))
    big_out["w_in"] = adamw_t(*big_w["w_in"], g_in, "adamw_w_in")

    small_out = small_adamw(parts, w_small, m_small, v_small, "adamw_small", after=(big_out["w_in"][1],))
    sm = [unpack_small(b) for b in small_out]
    loss = sm[0]["loss"]
    g_conv = lax.dynamic_slice_in_dim(sm[0]["conv"], me * ncv, ncv, axis=2)
    conv_out = None
    for l in reversed(range(NL)):
        conv_out, _ = adamw(gdn_conv, m_gdn_conv, v_gdn_conv, l, f"adamw_conv_l{l}", g=g_conv[l], prev=conv_out)

    def leaf(i, k):
        if k == "w_in":
            return jnp.transpose(big_out[k][i], (1, 2, 0))
        if k in big_out:
            return big_out[k][i]
        if k == "gdn_conv":
            return conv_out[i]
        return sm[i][k]

    order = ("attn_norm", "w_in", "pool_w", "pool_scale", "gdn_conv", "gdn_a_log", "gdn_dt_bias", "gdn_norm", "w_pool_up",
             "w_sb_up", "w_gdn_up", "w_out", "mlp_norm", "w_ff1", "w_ff2", "final_norm")
    return (loss, dh[None]) + tuple(leaf(i, k) for i in range(4) for k in order)
```

```python
import functools
import math

import jax
import jax.numpy as jnp
from jax import lax
from jax.experimental import pallas as pl
from jax.experimental.pallas import tpu as pltpu
from jax.experimental.pallas import tpu_sc as plsc

f32, bf16 = jnp.float32, jnp.bfloat16

D = 2048
NDEV = 8
NL = 2
HD = 128
NH = 6
WH = NH * HD
W_POOL = 512
EPS = 1e-6
N_IN = 12044
NP = 12288
OFF_SB, OFF_GQ, OFF_Z, OFF_P, OFF_AB, OFF_G = 0, 2304, 4608, 5376, 6016, 6144
AB_LANE = HD - 2 * NH
POOL_WINDOWS = (2, 4, 8, 16)
CH = 128
TQ = 256
VMEM_LIMIT = 56 * 1024 * 1024
ADAM_LR, ADAM_B1, ADAM_B2, ADAM_EPS, ADAM_WD, ADAM_STEP = 0.001, 0.9, 0.999, 1e-08, 0.01, 10
MESH = pl.DeviceIdType.MESH


def _cp(sem=None):
    return pltpu.CompilerParams(dimension_semantics=sem, vmem_limit_bytes=VMEM_LIMIT)


def _sds(shape, dtype):
    return jax.ShapeDtypeStruct(tuple(shape), dtype)


def matmul(a, b, *, name, ta=False, tb=False, out_dtype=f32, tm=1024, tn=1024, tk=2048, epilogue=None, extras=(), after=()):
    m, k = (a.shape[1], a.shape[0]) if ta else a.shape
    n = b.shape[0] if tb else b.shape[1]
    assert k == (b.shape[1] if tb else b.shape[0]) and a.dtype == bf16 and b.dtype == bf16
    tm, tn, tk = min(tm, m), min(tn, n), min(tk, k)
    assert m % tm == 0 and n % tn == 0 and k % tk == 0, (m, n, k, tm, tn, tk)
    nk = k // tk
    a_spec = pl.BlockSpec((tk, tm), lambda i, j, q: (q, i)) if ta else pl.BlockSpec((tm, tk), lambda i, j, q: (i, q))
    b_spec = pl.BlockSpec((tn, tk), lambda i, j, q: (j, q)) if tb else pl.BlockSpec((tk, tn), lambda i, j, q: (q, j))
    e_specs = [pl.BlockSpec((tm, tn), lambda i, j, q: (i, j)) for _ in extras]
    dn = (((0 if ta else 1,), (1 if tb else 0,)), ((), ()))
    ne = len(extras)

    def body(*refs):
        a_ref, b_ref, *rest = refs[len(after):]
        e_refs, o_ref = rest[:ne], rest[ne]
        part = lax.dot_general(a_ref[...], b_ref[...], dn, preferred_element_type=f32)

        def finish(acc):
            if epilogue is not None:
                acc = epilogue(acc, *[e[...] for e in e_refs])
            o_ref[...] = acc.astype(out_dtype)

        if nk == 1:
            finish(part)
        else:
            acc_ref = rest[ne + 1]
            q = pl.program_id(2)

            @pl.when(q == 0)
            def _():
                acc_ref[...] = part

            @pl.when(jnp.logical_and(q > 0, q < nk - 1))
            def _():
                acc_ref[...] += part

            @pl.when(q == nk - 1)
            def _():
                finish(acc_ref[...] + part)

    return pl.pallas_call(
        body, name=name, grid=(m // tm, n // tn, nk),
        in_specs=[pl.BlockSpec(memory_space=pl.ANY)] * len(after) + [a_spec, b_spec] + e_specs,
        out_specs=pl.BlockSpec((tm, tn), lambda i, j, q: (i, j)), out_shape=_sds((m, n), out_dtype),
        scratch_shapes=[pltpu.VMEM((tm, tn), f32)] if nk > 1 else [],
        compiler_params=_cp(("parallel", "parallel", "arbitrary")),
    )(*after, a, b, *extras)


def rowwise(name, fn, rows, params, outs, sums=(), tr=256, after=()):
    s = rows[0][0].shape[0]
    tr = min(tr, s)
    nin, nout = len(rows) + len(params), len(outs)
    in_specs = [pl.BlockSpec((tr, w), functools.partial(lambda i, c: (i, c), c=c)) for (_, w, c) in rows]
    in_specs += [pl.BlockSpec(p.shape, lambda i: (0, 0)) for p in params]
    out_specs = [pl.BlockSpec((tr, w), lambda i: (i, 0)) for (w, _) in outs]
    out_specs += [pl.BlockSpec(sh, lambda i: (0, 0)) for sh in sums]
    out_shape = [_sds((s, w), dt) for (w, dt) in outs] + [_sds(sh, f32) for sh in sums]

    def body(*refs):
        refs = refs[len(after):]
        res = fn(*[r[...] for r in refs[:nin]])
        for r, v in zip(refs[nin:nin + nout], res[:nout]):
            r[...] = v.astype(r.dtype)
        i = pl.program_id(0)
        for r, v in zip(refs[nin + nout:], res[nout:]):
            @pl.when(i == 0)
            def _(r=r, v=v):
                r[...] = v

            @pl.when(i > 0)
            def _(r=r, v=v):
                r[...] += v

    res = pl.pallas_call(
        body, name=name, grid=(s // tr,), in_specs=[pl.BlockSpec(memory_space=pl.ANY)] * len(after) + in_specs,
        out_specs=out_specs, out_shape=out_shape, compiler_params=_cp(("arbitrary",)),
    )(*after, *[r[0] for r in rows], *params)
    return res


def _rms(x, g):
    return x * lax.rsqrt(jnp.mean(x * x, axis=-1, keepdims=True) + EPS) * g


def rms_fwd(x, g, name):
    return rowwise(name, lambda xb, gb: (_rms(xb, gb),), [(x, D, 0)], [g], [(D, bf16)])[0]


def rms_bwd(x, du, dres, g, name, after=()):
    def fn(xb, dub, drb, gb):
        _, vjp = jax.vjp(_rms, xb, gb)
        dx, dg = vjp(dub.astype(f32))
        return drb + dx, drb + dx, dg

    return rowwise(name, fn, [(x, D, 0), (du, D, 0), (dres, D, 0)], [g], [(D, f32), (D, bf16)], [(1, D)], after=after)


def _merge(gates, up_p, up_s, up_g):
    sg = jax.nn.sigmoid(gates)
    return sg[:, :D] * up_p + sg[:, D:2 * D] * up_s + sg[:, 2 * D:] * up_g


def merge_fwd(gates, ups, name):
    return rowwise(name, lambda g, a, b, c: (_merge(g, a, b, c),),
                   [(gates, 3 * D, 0)] + [(u, D, 0) for u in ups], [], [(D, bf16)], tr=128)[0]


def merge_bwd(gates, ups, dmerged, name):
    def fn(g, a, b, c, dm):
        _, vjp = jax.vjp(_merge, g, a, b, c)
        return vjp(dm.astype(f32))

    return rowwise(name, fn, [(gates, 3 * D, 0)] + [(u, D, 0) for u in ups] + [(dmerged, D, 0)], [],
                   [(3 * D, bf16), (D, bf16), (D, bf16), (D, bf16)], tr=128)


def _gdn_out(o, z, g):
    ys = []
    for h in range(NH):
        sl = slice(h * HD, (h + 1) * HD)
        ys.append(_rms(o[:, sl], g) * jax.nn.silu(z[:, sl]))
    return jnp.concatenate(ys, axis=1)


def gdn_out_fwd(o, proj, g, name):
    return rowwise(name, lambda ob, zb, gb: (_gdn_out(ob, zb, gb),), [(o, WH, 0), (proj, WH, OFF_Z // WH)], [g],
                   [(WH, bf16)])[0]


def gdn_out_bwd(o, proj, g, dy, name):
    def fn(ob, zb, dyb, gb):
        _, vjp = jax.vjp(_gdn_out, ob, zb, gb)
        return vjp(dyb.astype(f32))

    return rowwise(name, fn, [(o, WH, 0), (proj, WH, OFF_Z // WH), (dy, WH, 0)], [g], [(WH, f32), (WH, bf16)],
                   [(1, HD)])


def loss_head(x, target, g, name):
    def loss_fn(xb, gb, tb):
        err = _rms(xb, gb) - tb
        return (0.5 / D) * jnp.sum(jnp.sum(err * err, axis=1, keepdims=True), axis=0, keepdims=True)

    def fn(xb, tb, gb):
        val, vjp = jax.vjp(functools.partial(loss_fn, tb=tb), xb, gb)
        dx, dg = vjp(jnp.ones((1, 1), f32))
        return dx, dx, jnp.broadcast_to(val, (1, HD)), dg

    return rowwise(name, fn, [(x, D, 0), (target, D, 0)], [g], [(D, f32), (D, bf16)], [(1, HD), (1, D)])


PB = 256


def _split(v):
    hi = v.astype(bf16)
    return hi, (v - hi.astype(f32)).astype(bf16)


def _band_dot(make_band, v, s, forward):
    hi, lo = _split(v)
    nb = s // PB
    outs = []
    for r in range(nb):
        lo_r = max(r - 1, 0) if forward else r
        hi_r = r + 1 if forward else min(r + 2, nb)
        band = make_band(r * PB, lo_r * PB, (hi_r - lo_r) * PB)
        sl = slice(lo_r * PB, hi_r * PB)
        outs.append(jnp.dot(band, hi[sl], preferred_element_type=f32) + jnp.dot(band, lo[sl], preferred_element_type=f32))
    return jnp.concatenate(outs, axis=0)


def _pool_common(p, win, s):
    def band(row0, col0, ncol):
        t = row0 + lax.broadcasted_iota(jnp.int32, (PB, ncol), 0)
        u = col0 + lax.broadcasted_iota(jnp.int32, (PB, ncol), 1)
        return jnp.logical_and(u <= t, t < u + win).astype(bf16)

    def band_t(row0, col0, ncol):
        u = row0 + lax.broadcasted_iota(jnp.int32, (PB, ncol), 0)
        t = col0 + lax.broadcasted_iota(jnp.int32, (PB, ncol), 1)
        return jnp.logical_and(u <= t, t < u + win).astype(bf16)

    t = lax.broadcasted_iota(jnp.int32, (s, 1), 0)
    inv_n = 1.0 / jnp.minimum(t + 1, win).astype(f32)
    d = _band_dot(band, p, s, True) * inv_n - p
    return d, inv_n, band_t


def pool_fwd(proj, pool_w, pool_scale, name):
    s = proj.shape[0]

    def body(p_ref, w_ref, sc_ref, y_ref):
        win = jnp.left_shift(2, pl.program_id(0))
        d, _, _ = _pool_common(p_ref[...], win, s)
        y = jnp.dot(d.astype(bf16), w_ref[...].astype(bf16), preferred_element_type=f32) * sc_ref[...]
        y_ref[...] = y.astype(bf16)

    return pl.pallas_call(
        body, name=name, grid=(4,),
        in_specs=[pl.BlockSpec((s, HD), lambda g: (0, OFF_P // HD + g)), pl.BlockSpec((None, HD, HD), lambda g: (g, 0, 0)),
                  pl.BlockSpec((1, HD), lambda g: (0, g))],
        out_specs=pl.BlockSpec((s, HD), lambda g: (0, g)), out_shape=_sds((s, W_POOL), bf16),
        compiler_params=_cp(("arbitrary",)),
    )(proj, pool_w, pool_scale)


def pool_bwd(proj, pool_w, pool_scale, dy, name):
    s = proj.shape[0]

    def body(p_ref, w_ref, sc_ref, dy_ref, dp_ref, dw_ref, dsc_ref):
        win = jnp.left_shift(2, pl.program_id(0))
        d, inv_n, band_t = _pool_common(p_ref[...], win, s)
        w = w_ref[...].astype(bf16)
        dyf = dy_ref[...].astype(f32)
        dsc_ref[...] = jnp.sum(dyf * jnp.dot(d.astype(bf16), w, preferred_element_type=f32), axis=0, keepdims=True)
        dys = (dyf * sc_ref[...]).astype(bf16)
        dd = lax.dot_general(dys, w, (((1,), (1,)), ((), ())), preferred_element_type=f32)
        dw_ref[...] = lax.dot_general(d.astype(bf16), dys, (((0,), (0,)), ((), ())), preferred_element_type=f32)
        dp_ref[...] = (_band_dot(band_t, dd * inv_n, s, False) - dd).astype(bf16)

    return pl.pallas_call(
        body, name=name, grid=(4,),
        in_specs=[pl.BlockSpec((s, HD), lambda g: (0, OFF_P // HD + g)), pl.BlockSpec((None, HD, HD), lambda g: (g, 0, 0)),
                  pl.BlockSpec((1, HD), lambda g: (0, g)), pl.BlockSpec((s, HD), lambda g: (0, g))],
        out_specs=[pl.BlockSpec((s, HD), lambda g: (0, g)), pl.BlockSpec((None, HD, HD), lambda g: (g, 0, 0)),
                   pl.BlockSpec((1, HD), lambda g: (0, g))],
        out_shape=[_sds((s, W_POOL), bf16), _sds((4, HD, HD), f32), _sds((1, W_POOL), f32)],
        compiler_params=_cp(("arbitrary",)),
    )(proj, pool_w, pool_scale, dy)


HB = 3
WB = HB * HD


def _heads_of(v):
    return _stack([v[:, h * HD:(h + 1) * HD] for h in range(HB)])


def _bd(a, b, dn):
    return lax.dot_general(a, b, dn, preferred_element_type=f32)


def _run_sum(v, tri):
    hi, lo = _split(v.reshape(HB * TQ, TQ))
    return (jnp.dot(hi, tri, preferred_element_type=f32) + jnp.dot(lo, tri, preferred_element_type=f32)).reshape(HB, TQ, TQ)


def _tri(later):
    j = lax.broadcasted_iota(jnp.int32, (TQ, TQ), 0)
    u = lax.broadcasted_iota(jnp.int32, (TQ, TQ), 1)
    return (j > u if later else j < u).astype(bf16)


def _sb_tile(q, k_ref, kb, carry, diagonal):
    k = _heads_of(k_ref[pl.ds(pl.multiple_of(kb * TQ, TQ), TQ), :].astype(bf16))
    z = _bd(q, k, BNT)
    ls = jax.nn.log_sigmoid(-z)
    if diagonal:
        mask = lax.broadcasted_iota(jnp.int32, (TQ, TQ), 1) < lax.broadcasted_iota(jnp.int32, (TQ, TQ), 0)
        ls = jnp.where(mask, ls, 0.0)
    a = jnp.exp(ls + z + _run_sum(ls, _tri(True)) + carry)
    return z, ls, jnp.where(mask, a, 0.0) if diagonal else a


def sb_fwd(proj, name):
    s = proj.shape[0]
    nq = s // TQ
    scale = HD ** -0.5

    def body(q_ref, k_ref, v_ref, y_ref):
        qi = pl.program_id(1)
        q = _heads_of((q_ref[...] * scale).astype(bf16))

        def step(j, c, diagonal=False):
            acc, carry = c
            kb = qi - j
            _, ls, a = _sb_tile(q, k_ref, kb, carry, diagonal)
            v = _heads_of(v_ref[pl.ds(pl.multiple_of(kb * TQ, TQ), TQ), :].astype(bf16))
            return acc + _bd(a.astype(bf16), v, BNN), carry + jnp.sum(ls, axis=2, keepdims=True)

        first = step(0, (jnp.zeros((HB, TQ, HD), f32), jnp.zeros((HB, TQ, 1), f32)), True)
        acc, _ = lax.fori_loop(1, qi + 1, step, first)
        for h in range(HB):
            y_ref[:, h * HD:(h + 1) * HD] = acc[h].astype(bf16)

    c0, ng = OFF_SB // WB, NH // HB
    return pl.pallas_call(
        body, name=name, grid=(ng, nq),
        in_specs=[pl.BlockSpec((TQ, WB), lambda g, i: (i, c0 + g)), pl.BlockSpec((s, WB), lambda g, i: (0, c0 + ng + g)),
                  pl.BlockSpec((s, WB), lambda g, i: (0, c0 + 2 * ng + g))],
        out_specs=pl.BlockSpec((TQ, WB), lambda g, i: (i, g)), out_shape=_sds((s, WH), bf16),
        compiler_params=_cp(("arbitrary", "arbitrary")),
    )(proj, proj, proj)


def sb_bwd(proj, dy, name):
    s = proj.shape[0]
    nq = s // TQ
    scale = HD ** -0.5

    def body(q_ref, k_ref, v_ref, do_ref, dq_ref, dk_ref, dv_ref, e_scr, z_scr, dk_acc, dv_acc):
        qi = pl.program_id(1)
        q = _heads_of((q_ref[...] * scale).astype(bf16))
        do = _heads_of(do_ref[...])

        @pl.when(qi == 0)
        def _():
            dk_acc[...] = jnp.zeros_like(dk_acc)
            dv_acc[...] = jnp.zeros_like(dv_acc)

        def add_heads(acc_ref, rows, upd):
            for h in range(HB):
                acc_ref[rows, h * HD:(h + 1) * HD] += upd[h]

        def sweep_back(j, carry, diagonal=False):
            kb = qi - j
            rows = pl.ds(pl.multiple_of(kb * TQ, TQ), TQ)
            z, ls, a = _sb_tile(q, k_ref, kb, carry, diagonal)
            v = _heads_of(v_ref[rows, :].astype(bf16))
            e_scr[kb] = _bd(do, v, BNT) * a
            z_scr[kb] = z
            add_heads(dv_acc, rows, _bd(a.astype(bf16), do, BTN))
            return carry + jnp.sum(ls, axis=2, keepdims=True)

        lax.fori_loop(1, qi + 1, sweep_back, sweep_back(0, jnp.zeros((HB, TQ, 1), f32), True))

        def sweep_fwd(kb, c, diagonal=False):
            dq, carry = c
            rows = pl.ds(pl.multiple_of(kb * TQ, TQ), TQ)
            e, z = e_scr[kb], z_scr[kb]
            sig = jax.nn.sigmoid(z)
            dz = e * (1.0 - sig) - (_run_sum(e, _tri(False)) + carry) * sig
            if diagonal:
                dz = jnp.where(lax.broadcasted_iota(jnp.int32, (TQ, TQ), 1) < lax.broadcasted_iota(jnp.int32, (TQ, TQ), 0), dz, 0.0)
            dz = dz.astype(bf16)
            k = _heads_of(k_ref[rows, :].astype(bf16))
            add_heads(dk_acc, rows, _bd(dz, q, BTN))
            return dq + _bd(dz, k, BNN), carry + jnp.sum(e, axis=2, keepdims=True)

        dq, _ = sweep_fwd(qi, lax.fori_loop(0, qi, sweep_fwd, (jnp.zeros((HB, TQ, HD), f32), jnp.zeros((HB, TQ, 1), f32))), True)
        for h in range(HB):
            dq_ref[:, h * HD:(h + 1) * HD] = (dq[h] * scale).astype(bf16)

        @pl.when(qi == nq - 1)
        def _():
            dk_ref[...] = dk_acc[...].astype(bf16)
            dv_ref[...] = dv_acc[...].astype(bf16)

    c0, ng = OFF_SB // WB, NH // HB
    return pl.pallas_call(
        body, name=name, grid=(ng, nq),
        in_specs=[pl.BlockSpec((TQ, WB), lambda g, i: (i, c0 + g)), pl.BlockSpec((s, WB), lambda g, i: (0, c0 + ng + g)),
                  pl.BlockSpec((s, WB), lambda g, i: (0, c0 + 2 * ng + g)), pl.BlockSpec((TQ, WB), lambda g, i: (i, g))],
        out_specs=[pl.BlockSpec((TQ, WB), lambda g, i: (i, g)), pl.BlockSpec((s, WB), lambda g, i: (0, g)),
                   pl.BlockSpec((s, WB), lambda g, i: (0, g))],
        out_shape=[_sds((s, WH), bf16)] * 3,
        scratch_shapes=[pltpu.VMEM((nq, HB, TQ, TQ), f32), pltpu.VMEM((nq, HB, TQ, TQ), f32), pltpu.VMEM((s, WB), f32),
                        pltpu.VMEM((s, WB), f32)],
        compiler_params=_cp(("arbitrary", "arbitrary")),
    )(proj, proj, proj, dy)


CB = 256


def _shift_down(v, k, s):
    if k == 0:
        return v
    row = lax.broadcasted_iota(jnp.int32, v.shape, 0)
    return jnp.where(row < k, 0.0, pltpu.roll(v, k, axis=0))


def _shift_up(v, k, s):
    if k == 0:
        return v
    row = lax.broadcasted_iota(jnp.int32, v.shape, 0)
    return jnp.where(row >= s - k, 0.0, pltpu.roll(v, s - k, axis=0))


def conv_fwd(proj, w, name):
    s = proj.shape[0]

    def body(x_ref, w_ref, y_ref):
        x, wv = x_ref[...], w_ref[...]
        y = sum(wv[3 - k:4 - k, :] * _shift_down(x, k, s) for k in range(4))
        y_ref[...] = jax.nn.silu(y)

    return pl.pallas_call(
        body, name=name, grid=(3 * WH // CB,),
        in_specs=[pl.BlockSpec((s, CB), lambda j: (0, OFF_GQ // CB + j)), pl.BlockSpec((4, CB), lambda j: (0, j))],
        out_specs=pl.BlockSpec((s, CB), lambda j: (0, j)), out_shape=_sds((s, 3 * WH), f32),
        compiler_params=_cp(("parallel",)),
    )(proj, w)


def conv_bwd(proj, w, dc, name, after=()):
    s = proj.shape[0]

    def body(*refs):
        x_ref, w_ref, dc_ref, dx_ref, dw_ref = refs[len(after):]
        x, wv = x_ref[...], w_ref[...]
        xs = [_shift_down(x, k, s) for k in range(4)]
        y = sum(wv[3 - k:4 - k, :] * xs[k] for k in range(4))
        sig = jax.nn.sigmoid(y)
        dy = dc_ref[...] * (sig * (1.0 + y * (1.0 - sig)))
        dx_ref[...] = sum(wv[3 - k:4 - k, :] * _shift_up(dy, k, s) for k in range(4)).astype(bf16)
        dw_ref[...] = jnp.concatenate([jnp.sum(dy * xs[3 - i], axis=0, keepdims=True) for i in range(4)], axis=0)

    return pl.pallas_call(
        body, name=name, grid=(3 * WH // CB,),
        in_specs=[pl.BlockSpec(memory_space=pl.ANY)] * len(after)
        + [pl.BlockSpec((s, CB), lambda j: (0, OFF_GQ // CB + j)), pl.BlockSpec((4, CB), lambda j: (0, j)),
           pl.BlockSpec((s, CB), lambda j: (0, j))],
        out_specs=[pl.BlockSpec((s, CB), lambda j: (0, j)), pl.BlockSpec((4, CB), lambda j: (0, j))],
        out_shape=[_sds((s, 3 * WH), bf16), _sds((4, 3 * WH), f32)],
        compiler_params=_cp(("parallel",)),
    )(*after, proj, w, dc)


SOLVE_PASSES = 3


def _pdot_impl(a, b, dn, passes):
    ah, al = _split(a)
    bh, bl = _split(b)
    dot = lambda p, q: lax.dot_general(p, q, dn, preferred_element_type=f32)
    if passes == 1:
        return dot(ah, bh)
    if passes == 2:
        return dot(ah, bh) + dot(ah, bl)
    return dot(ah, bh) + (dot(ah, bl) + dot(al, bh))


BNN, BNT, BTN = (((2,), (1,)), ((0,), (0,))), (((2,), (2,)), ((0,), (0,))), (((1,), (1,)), ((0,), (0,)))


@functools.partial(jax.custom_vjp, nondiff_argnums=(2,))
def _bdot(a, b, passes):
    return _pdot_impl(a, b, BNN, passes)


def _bdot_fwd(a, b, passes):
    return _pdot_impl(a, b, BNN, passes), (a, b)


def _bdot_bwd(passes, res, ct):
    a, b = res
    return _pdot_impl(ct, b, BNT, passes), _pdot_impl(a, ct, BTN, passes)


_bdot.defvjp(_bdot_fwd, _bdot_bwd)


@functools.partial(jax.custom_vjp, nondiff_argnums=(2,))
def _bdot_nt(a, b, passes):
    return _pdot_impl(a, b, BNT, passes)


def _bdot_nt_fwd(a, b, passes):
    return _pdot_impl(a, b, BNT, passes), (a, b)


def _bdot_nt_bwd(passes, res, ct):
    a, b = res
    return _pdot_impl(ct, b, BNN, passes), _pdot_impl(ct, a, BTN, passes)


_bdot_nt.defvjp(_bdot_nt_fwd, _bdot_nt_bwd)


def _lane_pick(v, h):
    lane = lax.broadcasted_iota(jnp.int32, v.shape, v.ndim - 1)
    return jnp.sum(jnp.where(lane == h, v, 0.0), axis=-1, keepdims=True)


def _stack(parts):
    return jnp.concatenate([p[None] for p in parts], axis=0)


def _heads(v, first):
    return _stack([_lane_pick(v, first + h) for h in range(NH)])


def _l2n(v):
    return v * lax.rsqrt(jnp.sum(v * v, axis=-1, keepdims=True) + EPS)


def _dot_nt(a, b):
    return lax.dot_general(a, b, (((1,), (1,)), ((), ())), preferred_element_type=f32)


def _inverse_impl(lower):
    i = lax.broadcasted_iota(jnp.int32, (CH, CH), 0)
    j = lax.broadcasted_iota(jnp.int32, (CH, CH), 1)
    inv = (i == j).astype(f32) - lower
    pw = _pdot_impl(lower, lower, BNN, SOLVE_PASSES)
    for m in range(1, int(math.log2(CH))):
        inv = inv + _pdot_impl(inv, pw, BNN, SOLVE_PASSES)
        if m < int(math.log2(CH)) - 1:
            pw = _pdot_impl(pw, pw, BNN, SOLVE_PASSES)
    return inv


@jax.custom_vjp
def _unit_lower_inverse(lower):
    return _inverse_impl(lower)


def _unit_lower_inverse_fwd(lower):
    inv = _inverse_impl(lower)
    return inv, inv


def _unit_lower_inverse_bwd(inv, ct):
    return (-_pdot_impl(_pdot_impl(inv, ct, BTN, SOLVE_PASSES), inv, BNT, SOLVE_PASSES),)


_unit_lower_inverse.defvjp(_unit_lower_inverse_fwd, _unit_lower_inverse_bwd)


def _gdn_chunk(cq, ck, cv, ab, alog, dtb):
    ones = jnp.ones((NH, CH, HD), f32)
    q = _l2n(cq) * (HD ** -0.5)
    k = _l2n(ck)
    la = -jnp.exp(_heads(alog, 0)) * jax.nn.softplus(_heads(ab, AB_LANE) + _heads(dtb, 0))
    beta = jax.nn.sigmoid(_heads(ab, AB_LANE + NH)) * ones
    i = lax.broadcasted_iota(jnp.int32, (CH, CH), 0)
    j = lax.broadcasted_iota(jnp.int32, (CH, CH), 1)
    incl, strict = j <= i, j < i
    g = _bdot(jnp.broadcast_to(incl.astype(f32), (NH, CH, CH)), la * ones, 2)
    g_row = _stack([g[h].T for h in range(NH)])
    gamma = jnp.where(incl, jnp.exp(jnp.where(incl, g - g_row, 0.0)), 0.0)
    lower = jnp.where(strict, beta * _bdot_nt(k, k, 1) * gamma, 0.0)
    inv = _unit_lower_inverse(lower)
    eg = jnp.exp(g)
    u = _bdot(inv, cv * beta, SOLVE_PASSES)
    w = _bdot(inv, k * (beta * eg), SOLVE_PASSES)
    qk = _bdot_nt(q, k, 1) * gamma
    g_last = g[:, CH - 1:CH, :]
    return u, w, q * eg, k * jnp.exp(g_last - g), qk, jnp.exp(g_last)


def _by_head(ref, t=0):
    return _stack([ref[:, t * WH + h * HD:t * WH + (h + 1) * HD] for h in range(NH)])


def gdn_a_fwd(c, proj, alog, dtb, name):
    s = c.shape[0]
    nc = s // CH

    def body(c_ref, ab_ref, al_ref, dt_ref, u_ref, w_ref, qd_ref, kd_ref, qk_ref, dec_ref):
        res = _gdn_chunk(_by_head(c_ref, 0), _by_head(c_ref, 1), _by_head(c_ref, 2), ab_ref[...], al_ref[...], dt_ref[...])
        for h in range(NH):
            sl = slice(h * HD, (h + 1) * HD)
            for r, v in zip((u_ref, w_ref, qd_ref, kd_ref, qk_ref), res[:5]):
                r[:, sl] = v[h]
            dec_ref[:, sl] = jnp.broadcast_to(res[5][h], (8, HD))

    row = pl.BlockSpec((CH, WH), lambda n: (n, 0))
    par = pl.BlockSpec((1, HD), lambda n: (0, 0))
    return pl.pallas_call(
        body, name=name, grid=(nc,),
        in_specs=[pl.BlockSpec((CH, 3 * WH), lambda n: (n, 0)), pl.BlockSpec((CH, HD), lambda n: (n, OFF_AB // HD)), par, par],
        out_specs=[row] * 5 + [pl.BlockSpec((None, 8, WH), lambda n: (n, 0, 0))],
        out_shape=[_sds((s, WH), f32)] * 5 + [_sds((nc, 8, WH), f32)],
        compiler_params=_cp(("parallel",)),
    )(c, proj, alog, dtb)


def gdn_a_bwd(c, proj, alog, dtb, cots, ddec, name):
    s = c.shape[0]
    nc = s // CH

    def body(c_ref, ab_ref, al_ref, dt_ref, du_ref, dw_ref, dqd_ref, dkd_ref, dqk_ref, ddec_ref,
             dc_ref, dab_ref, dal_ref, ddt_ref):
        n = pl.program_id(0)
        _, vjp = jax.vjp(_gdn_chunk, _by_head(c_ref, 0), _by_head(c_ref, 1), _by_head(c_ref, 2), ab_ref[...], al_ref[...],
                         dt_ref[...])
        lane = lax.broadcasted_iota(jnp.int32, (1, HD), 1)
        dd = _stack([jnp.where(lane == 0, ddec_ref[0:1, h * HD:(h + 1) * HD], 0.0) for h in range(NH)])
        dcq, dck, dcv, dab, dal, ddt = vjp(tuple(_by_head(r) for r in (du_ref, dw_ref, dqd_ref, dkd_ref, dqk_ref)) + (dd,))
        for h in range(NH):
            for t, v in enumerate((dcq, dck, dcv)):
                dc_ref[:, t * WH + h * HD:t * WH + (h + 1) * HD] = v[h]
        dab_ref[...] = dab.astype(bf16)

        @pl.when(n == 0)
        def _():
            dal_ref[...] = dal
            ddt_ref[...] = ddt

        @pl.when(n > 0)
        def _():
            dal_ref[...] += dal
            ddt_ref[...] += ddt

    row = pl.BlockSpec((CH, WH), lambda n: (n, 0))
    wide = pl.BlockSpec((CH, 3 * WH), lambda n: (n, 0))
    par = pl.BlockSpec((1, HD), lambda n: (0, 0))
    return pl.pallas_call(
        body, name=name, grid=(nc,),
        in_specs=[wide, pl.BlockSpec((CH, HD), lambda n: (n, OFF_AB // HD)), par, par] + [row] * 5
        + [pl.BlockSpec((None, 8, WH), lambda n: (n, 0, 0))],
        out_specs=[wide, pl.BlockSpec((CH, HD), lambda n: (n, 0)), par, par],
        out_shape=[_sds((s, 3 * WH), f32), _sds((s, HD), bf16), _sds((1, HD), f32), _sds((1, HD), f32)],
        compiler_params=_cp(("arbitrary",)),
    )(c, proj, alog, dtb, *cots, ddec)


def gdn_b_fwd(u, w, qd, kd, qk, dec, name):
    s = u.shape[0]
    nc = s // CH

    def body(u_ref, w_ref, qd_ref, kd_ref, qk_ref, dec_ref, o_ref, st_ref, state):
        n = pl.program_id(0)

        @pl.when(n == 0)
        def _():
            state[...] = jnp.zeros_like(state)

        for h in range(NH):
            sl = slice(h * HD, (h + 1) * HD)
            st = state[sl, :]
            st_ref[sl, :] = st
            sb = st.astype(bf16)
            vn = u_ref[:, sl] - jnp.dot(w_ref[:, sl].astype(bf16), sb, preferred_element_type=f32)
            vb = vn.astype(bf16)
            o_ref[:, sl] = (jnp.dot(qd_ref[:, sl].astype(bf16), sb, preferred_element_type=f32)
                            + jnp.dot(qk_ref[:, sl].astype(bf16), vb, preferred_element_type=f32))
            state[sl, :] = st * dec_ref[0:1, sl] + lax.dot_general(
                kd_ref[:, sl].astype(bf16), vb, (((0,), (0,)), ((), ())), preferred_element_type=f32)

    row = pl.BlockSpec((CH, WH), lambda n: (n, 0))
    return pl.pallas_call(
        body, name=name, grid=(nc,),
        in_specs=[row] * 5 + [pl.BlockSpec((None, 8, WH), lambda n: (n, 0, 0))],
        out_specs=[row, pl.BlockSpec((None, WH, HD), lambda n: (n, 0, 0))],
        out_shape=[_sds((s, WH), f32), _sds((nc, WH, HD), f32)],
        scratch_shapes=[pltpu.VMEM((WH, HD), f32)],
        compiler_params=_cp(("arbitrary",)),
    )(u, w, qd, kd, qk, dec)


def gdn_b_bwd(u, w, qd, kd, qk, dec, states, do, name):
    s = u.shape[0]
    nc = s // CH

    def body(u_ref, w_ref, qd_ref, kd_ref, qk_ref, dec_ref, st_ref, do_ref,
             du_ref, dw_ref, dqd_ref, dkd_ref, dqk_ref, ddec_ref, dstate):
        n = pl.program_id(0)

        @pl.when(n == 0)
        def _():
            dstate[...] = jnp.zeros_like(dstate)

        for h in range(NH):
            sl = slice(h * HD, (h + 1) * HD)
            st, ds = st_ref[sl, :], dstate[sl, :]
            sb, dsb = st.astype(bf16), ds.astype(bf16)
            wb, qdb, kdb, qkb = (r[:, sl].astype(bf16) for r in (w_ref, qd_ref, kd_ref, qk_ref))
            dob = do_ref[:, sl].astype(bf16)
            vn = u_ref[:, sl] - jnp.dot(wb, sb, preferred_element_type=f32)
            vb = vn.astype(bf16)
            dvn = (lax.dot_general(qkb, dob, (((0,), (0,)), ((), ())), preferred_element_type=f32)
                   + jnp.dot(kdb, dsb, preferred_element_type=f32))
            dvb = dvn.astype(bf16)
            du_ref[:, sl] = dvn
            dw_ref[:, sl] = -_dot_nt(dvb, sb)
            dqd_ref[:, sl] = _dot_nt(dob, sb)
            dkd_ref[:, sl] = _dot_nt(vb, dsb)
            dqk_ref[:, sl] = _dot_nt(dob, vb)
            tot = jnp.sum(jnp.sum(ds * st, axis=1, keepdims=True), axis=0, keepdims=True)
            ddec_ref[:, sl] = jnp.broadcast_to(tot, (8, HD))
            dstate[sl, :] = (ds * dec_ref[0:1, sl]
                             + lax.dot_general(qdb, dob, (((0,), (0,)), ((), ())), preferred_element_type=f32)
                             - lax.dot_general(wb, dvb, (((0,), (0,)), ((), ())), preferred_element_type=f32))

    row = pl.BlockSpec((CH, WH), lambda n: (nc - 1 - n, 0))
    small = pl.BlockSpec((None, 8, WH), lambda n: (nc - 1 - n, 0, 0))
    return pl.pallas_call(
        body, name=name, grid=(nc,),
        in_specs=[row] * 5 + [small, pl.BlockSpec((None, WH, HD), lambda n: (nc - 1 - n, 0, 0)), row],
        out_specs=[row] * 5 + [small],
        out_shape=[_sds((s, WH), f32)] * 5 + [_sds((nc, 8, WH), f32)],
        scratch_shapes=[pltpu.VMEM((WH, HD), f32)],
        compiler_params=_cp(("arbitrary",)),
    )(u, w, qd, kd, qk, dec, states, do)


ANY = pl.BlockSpec(memory_space=pl.ANY)


def _dev_index(p):
    return 4 * p[0] + 2 * p[1] + p[2]


def _shard_of(ref, axis, size, idx):
    return ref.at[pl.ds(idx * size, size), :] if axis == 0 else ref.at[:, pl.ds(idx * size, size)]


def _peer(x, y, c, r):
    return (1 - x if r & 4 else x, 1 - y if r & 2 else y, 1 - c if r & 1 else c)


ALL_PEERS, SIBLING, OTHER_CHIPS = tuple(range(1, NDEV)), (1,), (4, 2, 6)


def _launch(body, ins, out_sds, name, sequencer_id, relations=ALL_PEERS, kinds=7):
    n = len(ins)
    sems = [pltpu.SemaphoreType.DMA((n, kinds)), pltpu.SemaphoreType.DMA((n, kinds)), pltpu.SemaphoreType.DMA((n,))]
    if sequencer_id is None:
        return pl.pallas_call(
            lambda *refs: body(refs[:n], refs[n:2 * n], *refs[2 * n:]), name=name, in_specs=[ANY] * n, out_specs=[ANY] * n,
            out_shape=out_sds, scratch_shapes=sems, compiler_params=pltpu.CompilerParams(has_side_effects=True),
        )(*ins)
    in_refs = [jax.new_ref(a, memory_space=pltpu.MemorySpace.HBM) for a in ins]
    out_refs = [jax.empty_ref(sd, memory_space=pltpu.MemorySpace.HBM) for sd in out_sds]

    @pl.kernel(mesh=plsc.ScalarSubcoreMesh(axis_name="sequencer", num_cores=1), name=name, scratch_types=sems,
               compiler_params=pltpu.CompilerParams(collective_id=sequencer_id))
    def launch(send, recv, loc):
        x, y, c = lax.axis_index("x"), lax.axis_index("y"), lax.axis_index("c")
        barrier = pltpu.get_barrier_semaphore()
        for r in relations:
            pl.semaphore_signal(barrier, inc=1, device_id=_peer(x, y, c, r), device_id_type=MESH)
        pl.semaphore_wait(barrier, len(relations))
        body(in_refs, out_refs, send, recv, loc)

    launch()
    return [r[...] for r in out_refs]


def all_gather(xs, axes, name, sequencer_id=None):
    n = len(xs)
    fulls = [tuple(d * (NDEV if a == ax else 1) for a, d in enumerate(x.shape)) for x, ax in zip(xs, axes)]
    halved = [x.shape[0] % 32 == 0 for x in xs]

    def body(x_refs, o_refs, send, recv, loc):
        x, y, c = lax.axis_index("x"), lax.axis_index("y"), lax.axis_index("c")
        me, sib = (x, y, c), (x, y, 1 - c)
        xn, yn, dg = (1 - x, y), (x, 1 - y), (1 - x, 1 - y)

        def part(t, p, half=None):
            ref = _shard_of(o_refs[t], axes[t], xs[t].shape[axes[t]], _dev_index(p))
            rows = xs[t].shape[0] // 2
            return ref if half is None else ref.at[pl.ds(half * rows, rows), :]

        def copy(t, k, block, to, half=None, src=None):
            return pltpu.make_async_remote_copy(
                src_ref=part(t, block, half) if src is None else src, dst_ref=part(t, block, half),
                send_sem=send.at[t, k], recv_sem=recv.at[t, k], device_id=to, device_id_type=MESH)

        mine = [pltpu.make_async_copy(x_refs[t], part(t, me), loc.at[t]) for t in range(n)]
        for cp in mine:
            cp.start()
        sends = []
        for t in range(n):
            sends += [copy(t, 0, me, sib, src=x_refs[t]), copy(t, 1, me, (*xn, c), src=x_refs[t]),
                      copy(t, 2, me, (*yn, c), src=x_refs[t])]
            if not halved[t]:
                sends.append(copy(t, 3, me, (*dg, c), src=x_refs[t]))
        for cp in sends:
            cp.start()

        def pass_on(cp):
            cp.start()
            sends.append(cp)

        for t in range(n):
            h0, h1 = (0, 1) if halved[t] else (None, None)
            copy(t, 1, (*xn, c), me).wait_recv()
            if halved[t]:
                pass_on(copy(t, 3, (*xn, c), (*yn, c), 0))
            pass_on(copy(t, 5, (*xn, c), sib))
            copy(t, 2, (*yn, c), me).wait_recv()
            if halved[t]:
                pass_on(copy(t, 4, (*yn, c), (*xn, c), 1))
            pass_on(copy(t, 6, (*yn, c), sib))
            copy(t, 3, (*dg, c), me, h0).wait_recv()
            pass_on(copy(t, 7, (*dg, c), sib, h0))
            if halved[t]:
                copy(t, 4, (*dg, c), me, h1).wait_recv()
                pass_on(copy(t, 8, (*dg, c), sib, h1))
        for t in range(n):
            h0, h1 = (0, 1) if halved[t] else (None, None)
            copy(t, 0, sib, me).wait_recv()
            copy(t, 5, (*xn, 1 - c), me).wait_recv()
            copy(t, 6, (*yn, 1 - c), me).wait_recv()
            copy(t, 7, (*dg, 1 - c), me, h0).wait_recv()
            if halved[t]:
                copy(t, 8, (*dg, 1 - c), me, h1).wait_recv()
        for cp in sends:
            cp.wait_send()
        for cp in mine:
            cp.wait()

    return _launch(body, xs, [_sds(f, x.dtype) for f, x in zip(fulls, xs)], name, sequencer_id, kinds=9)


def pair_swap(gs, axes, name, sequencer_id):
    n = len(gs)
    shards = [tuple(d // (NDEV if a == ax else 1) for a, d in enumerate(g.shape)) for g, ax in zip(gs, axes)]

    def body(g_refs, p_refs, send, recv, loc):
        x, y, c = lax.axis_index("x"), lax.axis_index("y"), lax.axis_index("c")

        def copy(t, j):
            return pltpu.make_async_remote_copy(
                src_ref=_shard_of(g_refs[t], axes[t], shards[t][axes[t]], 2 * j + (1 - c)), dst_ref=p_refs[t].at[j],
                send_sem=send.at[t, j], recv_sem=recv.at[t, j], device_id=(x, y, 1 - c), device_id_type=MESH)

        copies = [copy(t, j) for t in range(n) for j in range(4)]
        for cp in copies:
            cp.start()
        for cp in copies:
            cp.wait_recv()
        for cp in copies:
            cp.wait_send()

    return _launch(body, gs, [_sds((4,) + sh, g.dtype) for sh, g in zip(shards, gs)], name, sequencer_id, SIBLING)


def pair_add(g, p, axis, name, after=()):
    _, rows, cols = p.shape
    tr = _row_tile(rows, cols, 1 << 20)
    nb = rows // tr
    if axis == 0:
        g_spec = pl.BlockSpec((tr, cols), lambda j, i, c: ((2 * j + c[0]) * nb + i, 0))
    else:
        g_spec = pl.BlockSpec((tr, cols), lambda j, i, c: (i, 2 * j + c[0]))
    blk = pl.BlockSpec((None, tr, cols), lambda j, i, c: (j, i, 0))

    na = len(after)

    def body(c_ref, *refs):
        g_ref, p_ref, q_ref = refs[na:]
        q_ref[...] = (g_ref[...].astype(f32) + p_ref[...].astype(f32)).astype(bf16)

    return pl.pallas_call(
        body, name=name, out_shape=_sds(p.shape, bf16),
        grid_spec=pltpu.PrefetchScalarGridSpec(num_scalar_prefetch=1, grid=(4, nb), in_specs=[ANY] * na + [g_spec, blk],
                                               out_specs=blk),
        compiler_params=_cp(("parallel", "parallel")),
    )(lax.axis_index("c").astype(jnp.int32).reshape(1), *after, g, p)


def chip_exchange(qs, name, sequencer_id):
    n = len(qs)

    def body(q_refs, r_refs, send, recv, loc):
        x, y, c = lax.axis_index("x"), lax.axis_index("y"), lax.axis_index("c")
        my_chip = 2 * x + y
        peers = [_peer(x, y, c, r) for r in OTHER_CHIPS]
        mine = [pltpu.make_async_copy(q_refs[t].at[my_chip], r_refs[t].at[my_chip], loc.at[t]) for t in range(n)]
        for cp in mine:
            cp.start()

        def copy(t, k, slot):
            p = peers[k]
            return pltpu.make_async_remote_copy(
                src_ref=q_refs[t].at[2 * p[0] + p[1]], dst_ref=r_refs[t].at[slot], send_sem=send.at[t, k], recv_sem=recv.at[t, k],
                device_id=p, device_id_type=MESH)

        sends = [copy(t, k, my_chip) for t in range(n) for k in range(3)]
        for cp in sends:
            cp.start()
        for t in range(n):
            for k in range(3):
                copy(t, k, 2 * peers[k][0] + peers[k][1]).wait_recv()
        for cp in sends:
            cp.wait_send()
        for cp in mine:
            cp.wait()

    return _launch(body, qs, [_sds(q.shape, q.dtype) for q in qs], name, sequencer_id, OTHER_CHIPS)


def _adamw(w, g, m, v):
    m = ADAM_B1 * m + (1.0 - ADAM_B1) * g
    v = ADAM_B2 * v + (1.0 - ADAM_B2) * jnp.square(g)
    m_hat = m / (1.0 - ADAM_B1 ** ADAM_STEP)
    v_hat = v / (1.0 - ADAM_B2 ** ADAM_STEP)
    return -ADAM_LR * (m_hat / (jnp.sqrt(v_hat) + ADAM_EPS) + ADAM_WD * w), m, v


def _sum8(r_ref):
    g = r_ref[0].astype(f32)
    for j in range(1, r_ref.shape[0]):
        g = g + r_ref[j].astype(f32)
    return g


def _row_tile(rows, cols, elems=1 << 18):
    tr = min(rows, max(8, 1 << int(math.log2(elems / cols))))
    while rows % tr:
        tr //= 2
    assert tr % 8 == 0 or tr == rows, (rows, cols)
    return tr


def sum_partials(r, name, after=()):
    _, rows, cols = r.shape
    tr = _row_tile(rows, cols)
    na = len(after)

    def body(*refs):
        refs[na + 1][...] = _sum8(refs[na])

    return pl.pallas_call(
        body, name=name, grid=(rows // tr,),
        in_specs=[ANY] * na + [pl.BlockSpec((r.shape[0], tr, cols), lambda i: (0, i, 0))],
        out_specs=pl.BlockSpec((tr, cols), lambda i: (i, 0)), out_shape=_sds((rows, cols), f32),
        compiler_params=_cp(("parallel",)),
    )(*after, r)


def adamw_t(w, m, v, grads, name):
    rows, nl, cols = w.shape
    tr = min(rows, (1 << 16) // cols)
    blk = pl.BlockSpec((tr, nl, cols), lambda i: (i, 0, 0))
    flat = pl.BlockSpec((tr, cols), lambda i: (i, 0))

    def body(w_ref, m_ref, v_ref, *rest):
        g_refs, (g_ref, d_ref, nm_ref, nv_ref) = rest[:nl], rest[nl:]
        for l in range(nl):
            grad = g_refs[l][...]
            g_ref[:, l, :] = grad
            d_ref[:, l, :], nm_ref[:, l, :], nv_ref[:, l, :] = _adamw(w_ref[:, l, :], grad, m_ref[:, l, :], v_ref[:, l, :])

    return tuple(pl.pallas_call(
        body, name=name, grid=(pl.cdiv(rows, tr),), in_specs=[blk] * 3 + [flat] * nl, out_specs=[blk] * 4,
        out_shape=[_sds(w.shape, f32)] * 4, compiler_params=_cp(("parallel",)),
    )(w, m, v, *grads))


def adamw(w, m, v, layer, name, r=None, g=None, prev=None):
    _, rows, cols = w.shape
    tr = _row_tile(rows, cols)
    blk = pl.BlockSpec((None, tr, cols), lambda i: (layer, i, 0))
    nprev = 0 if prev is None else 4

    def body(w_ref, m_ref, v_ref, src_ref, *rest):
        g_ref, d_ref, nm_ref, nv_ref, token_ref = rest[nprev:]
        grad = _sum8(src_ref) if g is None else src_ref[...]
        g_ref[...] = grad
        d_ref[...], nm_ref[...], nv_ref[...] = _adamw(w_ref[...], grad, m_ref[...], v_ref[...])
        token_ref[...] = jnp.zeros_like(token_ref)

    src, src_spec = (r, pl.BlockSpec((r.shape[0], tr, cols), lambda i: (0, i, 0))) if g is None else (g, pl.BlockSpec((tr, cols), lambda i: (i, 0)))
    *outs, token = pl.pallas_call(
        body, name=name, grid=(rows // tr,), in_specs=[blk] * 3 + [src_spec] + [ANY] * nprev,
        out_specs=[blk] * 4 + [pl.BlockSpec((8, HD), lambda i: (0, 0))],
        out_shape=[_sds(w.shape, f32)] * 4 + [_sds((8, HD), f32)], input_output_aliases={4 + k: k for k in range(nprev)},
        compiler_params=_cp(("arbitrary",)),
    )(w, m, v, src, *(prev or ()))
    return tuple(outs), token


def small_adamw(parts, w, m, v, name, after=()):
    def body(*refs):
        p_ref, w_ref, m_ref, v_ref, g_ref, d_ref, nm_ref, nv_ref = refs[len(after):]
        g = _sum8(p_ref)
        g_ref[...] = g
        d_ref[...], nm_ref[...], nv_ref[...] = _adamw(w_ref[...], g, m_ref[...], v_ref[...])

    vmem = pl.BlockSpec(memory_space=pltpu.VMEM)
    return pl.pallas_call(body, name=name, out_shape=[_sds(w.shape, f32)] * 4, in_specs=[ANY] * len(after) + [vmem] * 4,
                          out_specs=[vmem] * 4, compiler_params=_cp())(*after, parts, w, m, v)


def _pack_rows(wt):
    tail = jnp.pad(wt[5888:N_IN], ((OFF_G - 12 - (OFF_P + 512), 0), (0, 0)))
    return jnp.concatenate([wt[512:2816], wt[2816:5120], wt[5120:5888], wt[0:512], tail], axis=0)


def _unpack_rows(g):
    return jnp.concatenate([g[OFF_P:OFF_P + 512], g[OFF_SB:OFF_SB + 2304], g[OFF_GQ:OFF_GQ + 2304], g[OFF_Z:OFF_Z + 768],
                            g[OFF_G - 12:NP]], axis=0)


def _lanes(v):
    flat = v.reshape(-1)
    n = -(-flat.shape[0] // HD) * HD
    return jnp.pad(flat, (0, n - flat.shape[0])).reshape(n // HD, HD)


def _lanes8(v):
    rows = _lanes(v)
    return jnp.pad(rows, ((0, -rows.shape[0] % 8), (0, 0)))


def _layer_fwd(x, p, l):
    nm = lambda s: f"{s}_l{l}"
    u = rms_fwd(x, p["attn_norm"], nm("rms1"))
    proj = matmul(u, p["w_in"][0], name=nm("inproj"), tb=True)
    y_pool = pool_fwd(proj, p["pool_w"], p["pool_scale"], nm("pool"))
    y_sb = sb_fwd(proj, nm("sb"))
    c = conv_fwd(proj, p["conv"], nm("conv"))
    ga = gdn_a_fwd(c, proj, p["alog"], p["dtb"], nm("gdna"))
    o, states = gdn_b_fwd(*ga, nm("gdnb"))
    y_gdn = gdn_out_fwd(o, proj, p["gdn_norm"], nm("gdno"))
    gates = matmul(u, p["w_in"][1], name=nm("inproj_gates"), tb=True, after=(y_gdn,))
    ups = [matmul(y, p[k], name=nm(k)) for y, k in ((y_pool, "w_pool_up"), (y_sb, "w_sb_up"), (y_gdn, "w_gdn_up"))]
    merged = merge_fwd(gates, ups, nm("merge"))
    x1 = matmul(merged, p["w_out"], name=nm("outproj"), epilogue=lambda acc, r: acc + r, extras=(x,))
    u2 = rms_fwd(x1, p["mlp_norm"], nm("rms2"))
    h2 = matmul(u2, p["w_ff1"], name=nm("ff1"), out_dtype=bf16, epilogue=lambda acc: jnp.square(jnp.maximum(acc, 0.0)))
    x2 = matmul(h2, p["w_ff2"], name=nm("ff2"), epilogue=lambda acc, r: acc + r, extras=(x1,))
    saved = dict(x=x, u=u, proj=proj, gates=gates, y_pool=y_pool, y_sb=y_sb, c=c, ga=ga, o=o, states=states, y_gdn=y_gdn, ups=ups,
                 merged=merged, x1=x1, u2=u2, h2=h2)
    return x2, saved


def _layer_bwd(dx2, dx2b, sv, p, l, swap, finish):
    nm = lambda s: f"{s}_l{l}"
    s = dx2.shape[0]
    dh = matmul(dx2b, p["w_ff2"], name=nm("d_ff2_x"), tb=True, out_dtype=bf16,
                epilogue=lambda acc, h2: acc * (2.0 * jnp.sqrt(h2.astype(f32))), extras=(sv["h2"],))
    g_ff2 = matmul(sv["h2"], dx2b, name=nm("d_ff2_w"), ta=True, out_dtype=bf16)
    du2 = matmul(dh, p["w_ff1"], name=nm("d_ff1_x"), tb=True)
    g_ff1 = matmul(sv["u2"], dh, name=nm("d_ff1_w"), ta=True, out_dtype=bf16)
    dx1, dx1b, d_mlp_norm = rms_bwd(sv["x1"], du2, dx2, p["mlp_norm"], nm("d_rms2"))
    dmerged = matmul(dx1b, p["w_out"], name=nm("d_out_x"), tb=True, out_dtype=bf16)
    g_out = matmul(sv["merged"], dx1b, name=nm("d_out_w"), ta=True, out_dtype=bf16)
    dgates, *dups = merge_bwd(sv["gates"], sv["ups"], dmerged, nm("d_merge"))
    dys, g_ups = [], []
    for dup, y, k in zip(dups, (sv["y_pool"], sv["y_sb"], sv["y_gdn"]), ("w_pool_up", "w_sb_up", "w_gdn_up")):
        dys.append(matmul(dup, p[k], name=nm("d_" + k + "_x"), tb=True, out_dtype=bf16))
        g_ups.append(matmul(y, dup, name=nm("d_" + k + "_w"), ta=True, out_dtype=bf16))
    early = g_ups + [g_out, g_ff1, g_ff2]
    swapped = swap(early, BIG_AXES[1:], "a")
    do, dz, d_gdn_norm = gdn_out_bwd(sv["o"], sv["proj"], p["gdn_norm"], dys[2], nm("d_gdno"))
    cots = gdn_b_bwd(*sv["ga"], sv["states"], do, nm("d_gdnb"))
    dc, dab, d_alog, d_dtb = gdn_a_bwd(sv["c"], sv["proj"], p["alog"], p["dtb"], cots[:5], cots[5], nm("d_gdna"))
    r_early, sent_early = finish(early, swapped, BIG_AXES[1:], BIG_NAMES[1:], "a", (dab,))
    dgq, d_conv = conv_bwd(sv["proj"], p["conv"], dc, nm("d_conv"), sent_early)
    dq, dk, dv = sb_bwd(sv["proj"], dys[1], nm("d_sb"))
    dp, d_pool_w, d_pool_scale = pool_bwd(sv["proj"], p["pool_w"], p["pool_scale"], dys[0], nm("d_pool"))
    dproj = jnp.concatenate([dq, dk, dv, dgq, dz, dp, jnp.zeros((s, OFF_AB - OFF_P - W_POOL), bf16), dab], axis=1)
    g_in = [matmul(dproj, sv["u"], name=nm("d_in_w"), ta=True, out_dtype=bf16),
            matmul(dgates, sv["u"], name=nm("d_in_w_gates"), ta=True, out_dtype=bf16)]
    r_in, sent_in = finish(g_in, swap(g_in, (1, 1), "b"), (1, 1), ("w_in", "w_in_gates"), "b", ())
    du = matmul(dproj, p["w_in"][0], name=nm("d_in_x"), after=sent_in)
    du = matmul(dgates, p["w_in"][1], name=nm("d_in_x_gates"), epilogue=lambda acc, r: acc + r, extras=(du,))
    dx, dxb, d_attn_norm = rms_bwd(sv["x"], du, dx1, p["attn_norm"], nm("d_rms1"))
    recv = r_in + r_early
    small = [d_attn_norm, d_pool_w, d_pool_scale, d_conv, d_alog, d_dtb, d_gdn_norm, d_mlp_norm]
    return dx, dxb, recv, small


BIG_AXES = (1, 1, 1, 1, 0, 1, 0)
GATHER_ID, EXCHANGE_ID = 1, 2
BIG_NAMES = ("w_in", "w_pool_up", "w_sb_up", "w_gdn_up", "w_out", "w_ff1", "w_ff2")


def kernel(x, attn_norm, w_in, pool_w, pool_scale, gdn_conv, gdn_a_log, gdn_dt_bias, gdn_norm, w_pool_up, w_sb_up, w_gdn_up, w_out, mlp_norm, w_ff1, w_ff2, final_norm, loss_target, m_attn_norm, m_w_in, m_pool_w, m_pool_scale, m_gdn_conv, m_gdn_a_log, m_gdn_dt_bias, m_gdn_norm, m_w_pool_up, m_w_sb_up, m_w_gdn_up, m_w_out, m_mlp_norm, m_w_ff1, m_w_ff2, m_final_norm, v_attn_norm, v_w_in, v_pool_w, v_pool_scale, v_gdn_conv, v_gdn_a_log, v_gdn_dt_bias, v_gdn_norm, v_w_pool_up, v_w_sb_up, v_w_gdn_up, v_w_out, v_mlp_norm, v_w_ff1, v_w_ff2, v_final_norm):
    s = x.shape[1]
    me = _dev_index((lax.axis_index("x"), lax.axis_index("y"), lax.axis_index("c")))
    ncv = gdn_conv.shape[2]

    w_in_t, m_w_in_t, v_w_in_t = (jnp.transpose(a, (2, 0, 1)) for a in (w_in, m_w_in, v_w_in))
    full = []
    for l in range(NL):
        packed_in = _pack_rows(w_in_t[:, l]).astype(bf16)
        in_mix, in_gates = packed_in[:OFF_G], packed_in[OFF_G:]
        shards = [w_pool_up[l].astype(bf16), w_sb_up[l].astype(bf16), w_gdn_up[l].astype(bf16), w_out[l].astype(bf16),
                  w_ff1[l].astype(bf16), w_ff2[l].astype(bf16)]
        if l == 0:
            first = all_gather([in_mix, gdn_conv.reshape(NL * 4, ncv)], (1, 0), "gather_first", sequencer_id=GATHER_ID)
            gates = all_gather([in_gates], (1,), "gather_gates_l0", sequencer_id=GATHER_ID)
            full.append([(first[0], gates[0])] + all_gather(shards, BIG_AXES[1:], "gather_rest_l0", sequencer_id=GATHER_ID))
            conv_full = first[1].reshape(NDEV, NL, 4, ncv).transpose(1, 2, 0, 3).reshape(NL, 4, NDEV * ncv)
        else:
            first = all_gather([in_mix, in_gates], (1, 1), f"gather_in_l{l}", sequencer_id=GATHER_ID)
            full.append([tuple(first)] + all_gather(shards, BIG_AXES[1:], f"gather_rest_l{l}", sequencer_id=GATHER_ID))
    params = []
    for l in range(NL):
        p = dict(zip(("w_in", "w_pool_up", "w_sb_up", "w_gdn_up", "w_out", "w_ff1", "w_ff2"), full[l][:7]))
        p.update(attn_norm=attn_norm[l][None], mlp_norm=mlp_norm[l][None], pool_w=pool_w[l], pool_scale=pool_scale[l][None],
                 conv=conv_full[l], alog=_lanes(gdn_a_log[l]), dtb=_lanes(gdn_dt_bias[l]), gdn_norm=gdn_norm[l][None])
        params.append(p)

    h = x[0]
    saved = []
    for l in range(NL):
        h, sv = _layer_fwd(h, params[l], l)
        saved.append(sv)
    dh, dhb, loss_row, d_final = loss_head(h, loss_target[0], final_norm[None], "loss_head")
    recv, smalls = [None] * NL, [None] * NL
    for l in reversed(range(NL)):
        def swap(gs, axes, tag, l=l):
            return pair_swap(gs, axes, f"swap_{tag}_l{l}", None)

        def finish(gs, ps, axes, names, tag, after, l=l):
            qs = [pair_add(g, p, ax, f"pair_add_{k}_l{l}", after) for g, p, ax, k in zip(gs, ps, axes, names)]
            return chip_exchange(qs, f"exchange_{tag}_l{l}", EXCHANGE_ID), tuple(qs)

        dh, dhb, recv[l], smalls[l] = _layer_bwd(dh, dhb, saved[l], params[l], l, swap, finish)

    small_rows = [_lanes8(t) for l in range(NL) for t in smalls[l]] + [_lanes8(d_final), _lanes8(loss_row)]
    packed = jnp.concatenate(small_rows, axis=0)
    parts = all_gather([packed], (0,), "gather_small", sequencer_id=GATHER_ID)[0].reshape(NDEV, packed.shape[0], HD)

    def pack_small(tree):
        rows = []
        for l in range(NL):
            rows += [_lanes8(tree["attn_norm"][l]), _lanes8(tree["pool_w"][l]), _lanes8(tree["pool_scale"][l]),
                     jnp.zeros((4 * NDEV * ncv // HD, HD), f32), _lanes8(tree["gdn_a_log"][l]), _lanes8(tree["gdn_dt_bias"][l]),
                     _lanes8(tree["gdn_norm"][l]), _lanes8(tree["mlp_norm"][l])]
        rows += [_lanes8(tree["final_norm"]), jnp.zeros((8, HD), f32)]
        return jnp.concatenate(rows, axis=0)

    names = ("attn_norm", "pool_w", "pool_scale", "gdn_a_log", "gdn_dt_bias", "gdn_norm", "mlp_norm", "final_norm")
    w_small = pack_small(dict(zip(names, (attn_norm, pool_w, pool_scale, gdn_a_log, gdn_dt_bias, gdn_norm, mlp_norm, final_norm))))
    m_small = pack_small(dict(zip(names, (m_attn_norm, m_pool_w, m_pool_scale, m_gdn_a_log, m_gdn_dt_bias, m_gdn_norm, m_mlp_norm, m_final_norm))))
    v_small = pack_small(dict(zip(names, (v_attn_norm, v_pool_w, v_pool_scale, v_gdn_a_log, v_gdn_dt_bias, v_gdn_norm, v_mlp_norm, v_final_norm))))
    def unpack_small(buf):
        out, conv_g, r = {}, [], 0
        layer_items = (("attn_norm", (D,)), ("pool_w", (4, HD, HD)), ("pool_scale", (W_POOL,)), ("conv", (4, NDEV * ncv)),
                       ("gdn_a_log", (NH,)), ("gdn_dt_bias", (NH,)), ("gdn_norm", (HD,)), ("mlp_norm", (D,)))
        per_layer = {k: [] for k, _ in layer_items}
        for l in range(NL):
            for k, shape in layer_items:
                size = math.prod(shape)
                nrow = -(-size // (8 * HD)) * 8
                per_layer[k].append(buf[r:r + nrow].reshape(-1)[:size].reshape(shape))
                r += nrow
        for k, _ in layer_items:
            out[k] = jnp.stack(per_layer[k])
        out["final_norm"] = buf[r:r + D // HD].reshape(D)
        out["loss"] = buf[r + D // HD, 0]
        return out

    big_out = {}
    big_names = BIG_NAMES
    big_w = dict(zip(big_names, ((w_in_t, m_w_in_t, v_w_in_t), (w_pool_up, m_w_pool_up, v_w_pool_up), (w_sb_up, m_w_sb_up, v_w_sb_up),
                                 (w_gdn_up, m_w_gdn_up, v_w_gdn_up), (w_out, m_w_out, v_w_out), (w_ff1, m_w_ff1, v_w_ff1),
                                 (w_ff2, m_w_ff2, v_w_ff2))))
    tokens = []
    for l, k in [(l, k) for l in reversed(range(NL)) for k in big_names[1:]]:
        w, m, v = big_w[k]
        big_out[k], token = adamw(w, m, v, l, f"adamw_{k}_l{l}", r=recv[l][big_names.index(k) + 1], prev=big_out.get(k))
        tokens.append(token)
    g_in = []
    for l in range(NL):
        halves = [sum_partials(recv[l][h], f"sum_w_in_{h}_l{l}", tuple(tokens) if l == 0 else ()) for h in range(2)]
        g_in.append(_unpack_rows(jnp.concatenate(halves, axis=0)))
    big_out["w_in"] = adamw_t(*big_w["w_in"], g_in, "adamw_w_in")

    small_out = small_adamw(parts, w_small, m_small, v_small, "adamw_small", after=(big_out["w_in"][1],))
    sm = [unpack_small(b) for b in small_out]
    loss = sm[0]["loss"]
    g_conv = lax.dynamic_slice_in_dim(sm[0]["conv"], me * ncv, ncv, axis=2)
    conv_out = None
    for l in reversed(range(NL)):
        conv_out, _ = adamw(gdn_conv, m_gdn_conv, v_gdn_conv, l, f"adamw_conv_l{l}", g=g_conv[l], prev=conv_out)

    def leaf(i, k):
        if k == "w_in":
            return jnp.transpose(big_out[k][i], (1, 2, 0))
        if k in big_out:
            return big_out[k][i]
        if k == "gdn_conv":
            return conv_out[i]
        return sm[i][k]

    order = ("attn_norm", "w_in", "pool_w", "pool_scale", "gdn_conv", "gdn_a_log", "gdn_dt_bias", "gdn_norm", "w_pool_up",
             "w_sb_up", "w_gdn_up", "w_out", "mlp_norm", "w_ff1", "w_ff2", "final_norm")
    return (loss, dh[None]) + tuple(leaf(i, k) for i in range(4) for k in order)
```

```python
import functools
import math

import jax
import jax.numpy as jnp
from jax import lax
from jax.experimental import pallas as pl
from jax.experimental.pallas import tpu as pltpu
from jax.experimental.pallas import tpu_sc as plsc

f32, bf16 = jnp.float32, jnp.bfloat16

D = 2048
NDEV = 8
NL = 2
HD = 128
NH = 6
WH = NH * HD
W_POOL = 512
EPS = 1e-6
N_IN = 12044
NP = 12288
OFF_SB, OFF_GQ, OFF_Z, OFF_P, OFF_AB, OFF_G = 0, 2304, 4608, 5376, 6016, 6144
AB_LANE = HD - 2 * NH
POOL_WINDOWS = (2, 4, 8, 16)
CH = 128
TQ = 256
VMEM_LIMIT = 56 * 1024 * 1024
ADAM_LR, ADAM_B1, ADAM_B2, ADAM_EPS, ADAM_WD, ADAM_STEP = 0.001, 0.9, 0.999, 1e-08, 0.01, 10
MESH = pl.DeviceIdType.MESH


def _cp(sem=None):
    return pltpu.CompilerParams(dimension_semantics=sem, vmem_limit_bytes=VMEM_LIMIT)


def _sds(shape, dtype):
    return jax.ShapeDtypeStruct(tuple(shape), dtype)


def matmul(a, b, *, name, ta=False, tb=False, out_dtype=f32, tm=1024, tn=1024, tk=2048, epilogue=None, extras=(), after=()):
    m, k = (a.shape[1], a.shape[0]) if ta else a.shape
    n = b.shape[0] if tb else b.shape[1]
    assert k == (b.shape[1] if tb else b.shape[0]) and a.dtype == bf16 and b.dtype == bf16
    tm, tn, tk = min(tm, m), min(tn, n), min(tk, k)
    assert m % tm == 0 and n % tn == 0 and k % tk == 0, (m, n, k, tm, tn, tk)
    nk = k // tk
    a_spec = pl.BlockSpec((tk, tm), lambda i, j, q: (q, i)) if ta else pl.BlockSpec((tm, tk), lambda i, j, q: (i, q))
    b_spec = pl.BlockSpec((tn, tk), lambda i, j, q: (j, q)) if tb else pl.BlockSpec((tk, tn), lambda i, j, q: (q, j))
    e_specs = [pl.BlockSpec((tm, tn), lambda i, j, q: (i, j)) for _ in extras]
    dn = (((0 if ta else 1,), (1 if tb else 0,)), ((), ()))
    ne = len(extras)

    def body(*refs):
        a_ref, b_ref, *rest = refs[len(after):]
        e_refs, o_ref = rest[:ne], rest[ne]
        part = lax.dot_general(a_ref[...], b_ref[...], dn, preferred_element_type=f32)

        def finish(acc):
            if epilogue is not None:
                acc = epilogue(acc, *[e[...] for e in e_refs])
            o_ref[...] = acc.astype(out_dtype)

        if nk == 1:
            finish(part)
        else:
            acc_ref = rest[ne + 1]
            q = pl.program_id(2)

            @pl.when(q == 0)
            def _():
                acc_ref[...] = part

            @pl.when(jnp.logical_and(q > 0, q < nk - 1))
            def _():
                acc_ref[...] += part

            @pl.when(q == nk - 1)
            def _():
                finish(acc_ref[...] + part)

    return pl.pallas_call(
        body, name=name, grid=(m // tm, n // tn, nk),
        in_specs=[pl.BlockSpec(memory_space=pl.ANY)] * len(after) + [a_spec, b_spec] + e_specs,
        out_specs=pl.BlockSpec((tm, tn), lambda i, j, q: (i, j)), out_shape=_sds((m, n), out_dtype),
        scratch_shapes=[pltpu.VMEM((tm, tn), f32)] if nk > 1 else [],
        compiler_params=_cp(("parallel", "parallel", "arbitrary")),
    )(*after, a, b, *extras)


def rowwise(name, fn, rows, params, outs, sums=(), tr=256, after=()):
    s = rows[0][0].shape[0]
    tr = min(tr, s)
    nin, nout = len(rows) + len(params), len(outs)
    in_specs = [pl.BlockSpec((tr, w), functools.partial(lambda i, c: (i, c), c=c)) for (_, w, c) in rows]
    in_specs += [pl.BlockSpec(p.shape, lambda i: (0, 0)) for p in params]
    out_specs = [pl.BlockSpec((tr, w), lambda i: (i, 0)) for (w, _) in outs]
    out_specs += [pl.BlockSpec(sh, lambda i: (0, 0)) for sh in sums]
    out_shape = [_sds((s, w), dt) for (w, dt) in outs] + [_sds(sh, f32) for sh in sums]

    def body(*refs):
        refs = refs[len(after):]
        res = fn(*[r[...] for r in refs[:nin]])
        for r, v in zip(refs[nin:nin + nout], res[:nout]):
            r[...] = v.astype(r.dtype)
        i = pl.program_id(0)
        for r, v in zip(refs[nin + nout:], res[nout:]):
            @pl.when(i == 0)
            def _(r=r, v=v):
                r[...] = v

            @pl.when(i > 0)
            def _(r=r, v=v):
                r[...] += v

    res = pl.pallas_call(
        body, name=name, grid=(s // tr,), in_specs=[pl.BlockSpec(memory_space=pl.ANY)] * len(after) + in_specs,
        out_specs=out_specs, out_shape=out_shape, compiler_params=_cp(("arbitrary",)),
    )(*after, *[r[0] for r in rows], *params)
    return res


def _rms(x, g):
    return x * lax.rsqrt(jnp.mean(x * x, axis=-1, keepdims=True) + EPS) * g


def rms_fwd(x, g, name):
    return rowwise(name, lambda xb, gb: (_rms(xb, gb),), [(x, D, 0)], [g], [(D, bf16)])[0]


def rms_bwd(x, du, dres, g, name, after=()):
    def fn(xb, dub, drb, gb):
        _, vjp = jax.vjp(_rms, xb, gb)
        dx, dg = vjp(dub.astype(f32))
        return drb + dx, drb + dx, dg

    return rowwise(name, fn, [(x, D, 0), (du, D, 0), (dres, D, 0)], [g], [(D, f32), (D, bf16)], [(1, D)], after=after)


def _merge(gates, up_p, up_s, up_g):
    sg = jax.nn.sigmoid(gates)
    return sg[:, :D] * up_p + sg[:, D:2 * D] * up_s + sg[:, 2 * D:] * up_g


def merge_fwd(gates, ups, name):
    return rowwise(name, lambda g, a, b, c: (_merge(g, a, b, c),),
                   [(gates, 3 * D, 0)] + [(u, D, 0) for u in ups], [], [(D, bf16)], tr=128)[0]


def merge_bwd(gates, ups, dmerged, name):
    def fn(g, a, b, c, dm):
        _, vjp = jax.vjp(_merge, g, a, b, c)
        return vjp(dm.astype(f32))

    return rowwise(name, fn, [(gates, 3 * D, 0)] + [(u, D, 0) for u in ups] + [(dmerged, D, 0)], [],
                   [(3 * D, bf16), (D, bf16), (D, bf16), (D, bf16)], tr=128)


def _gdn_out(o, z, g):
    ys = []
    for h in range(NH):
        sl = slice(h * HD, (h + 1) * HD)
        ys.append(_rms(o[:, sl], g) * jax.nn.silu(z[:, sl]))
    return jnp.concatenate(ys, axis=1)


def gdn_out_fwd(o, proj, g, name):
    return rowwise(name, lambda ob, zb, gb: (_gdn_out(ob, zb, gb),), [(o, WH, 0), (proj, WH, OFF_Z // WH)], [g],
                   [(WH, bf16)])[0]


def gdn_out_bwd(o, proj, g, dy, name):
    def fn(ob, zb, dyb, gb):
        _, vjp = jax.vjp(_gdn_out, ob, zb, gb)
        return vjp(dyb.astype(f32))

    return rowwise(name, fn, [(o, WH, 0), (proj, WH, OFF_Z // WH), (dy, WH, 0)], [g], [(WH, f32), (WH, bf16)],
                   [(1, HD)])


def loss_head(x, target, g, name):
    def loss_fn(xb, gb, tb):
        err = _rms(xb, gb) - tb
        return (0.5 / D) * jnp.sum(jnp.sum(err * err, axis=1, keepdims=True), axis=0, keepdims=True)

    def fn(xb, tb, gb):
        val, vjp = jax.vjp(functools.partial(loss_fn, tb=tb), xb, gb)
        dx, dg = vjp(jnp.ones((1, 1), f32))
        return dx, dx, jnp.broadcast_to(val, (1, HD)), dg

    return rowwise(name, fn, [(x, D, 0), (target, D, 0)], [g], [(D, f32), (D, bf16)], [(1, HD), (1, D)])


PB = 256


def _split(v):
    hi = v.astype(bf16)
    return hi, (v - hi.astype(f32)).astype(bf16)


def _band_dot(make_band, v, s, forward):
    hi, lo = _split(v)
    nb = s // PB
    outs = []
    for r in range(nb):
        lo_r = max(r - 1, 0) if forward else r
        hi_r = r + 1 if forward else min(r + 2, nb)
        band = make_band(r * PB, lo_r * PB, (hi_r - lo_r) * PB)
        sl = slice(lo_r * PB, hi_r * PB)
        outs.append(jnp.dot(band, hi[sl], preferred_element_type=f32) + jnp.dot(band, lo[sl], preferred_element_type=f32))
    return jnp.concatenate(outs, axis=0)


def _pool_common(p, win, s):
    def band(row0, col0, ncol):
        t = row0 + lax.broadcasted_iota(jnp.int32, (PB, ncol), 0)
        u = col0 + lax.broadcasted_iota(jnp.int32, (PB, ncol), 1)
        return jnp.logical_and(u <= t, t < u + win).astype(bf16)

    def band_t(row0, col0, ncol):
        u = row0 + lax.broadcasted_iota(jnp.int32, (PB, ncol), 0)
        t = col0 + lax.broadcasted_iota(jnp.int32, (PB, ncol), 1)
        return jnp.logical_and(u <= t, t < u + win).astype(bf16)

    t = lax.broadcasted_iota(jnp.int32, (s, 1), 0)
    inv_n = 1.0 / jnp.minimum(t + 1, win).astype(f32)
    d = _band_dot(band, p, s, True) * inv_n - p
    return d, inv_n, band_t


def pool_fwd(proj, pool_w, pool_scale, name):
    s = proj.shape[0]

    def body(p_ref, w_ref, sc_ref, y_ref):
        win = jnp.left_shift(2, pl.program_id(0))
        d, _, _ = _pool_common(p_ref[...], win, s)
        y = jnp.dot(d.astype(bf16), w_ref[...].astype(bf16), preferred_element_type=f32) * sc_ref[...]
        y_ref[...] = y.astype(bf16)

    return pl.pallas_call(
        body, name=name, grid=(4,),
        in_specs=[pl.BlockSpec((s, HD), lambda g: (0, OFF_P // HD + g)), pl.BlockSpec((None, HD, HD), lambda g: (g, 0, 0)),
                  pl.BlockSpec((1, HD), lambda g: (0, g))],
        out_specs=pl.BlockSpec((s, HD), lambda g: (0, g)), out_shape=_sds((s, W_POOL), bf16),
        compiler_params=_cp(("arbitrary",)),
    )(proj, pool_w, pool_scale)


def pool_bwd(proj, pool_w, pool_scale, dy, name):
    s = proj.shape[0]

    def body(p_ref, w_ref, sc_ref, dy_ref, dp_ref, dw_ref, dsc_ref):
        win = jnp.left_shift(2, pl.program_id(0))
        d, inv_n, band_t = _pool_common(p_ref[...], win, s)
        w = w_ref[...].astype(bf16)
        dyf = dy_ref[...].astype(f32)
        dsc_ref[...] = jnp.sum(dyf * jnp.dot(d.astype(bf16), w, preferred_element_type=f32), axis=0, keepdims=True)
        dys = (dyf * sc_ref[...]).astype(bf16)
        dd = lax.dot_general(dys, w, (((1,), (1,)), ((), ())), preferred_element_type=f32)
        dw_ref[...] = lax.dot_general(d.astype(bf16), dys, (((0,), (0,)), ((), ())), preferred_element_type=f32)
        dp_ref[...] = (_band_dot(band_t, dd * inv_n, s, False) - dd).astype(bf16)

    return pl.pallas_call(
        body, name=name, grid=(4,),
        in_specs=[pl.BlockSpec((s, HD), lambda g: (0, OFF_P // HD + g)), pl.BlockSpec((None, HD, HD), lambda g: (g, 0, 0)),
                  pl.BlockSpec((1, HD), lambda g: (0, g)), pl.BlockSpec((s, HD), lambda g: (0, g))],
        out_specs=[pl.BlockSpec((s, HD), lambda g: (0, g)), pl.BlockSpec((None, HD, HD), lambda g: (g, 0, 0)),
                   pl.BlockSpec((1, HD), lambda g: (0, g))],
        out_shape=[_sds((s, W_POOL), bf16), _sds((4, HD, HD), f32), _sds((1, W_POOL), f32)],
        compiler_params=_cp(("arbitrary",)),
    )(proj, pool_w, pool_scale, dy)


HB = 3
WB = HB * HD


def _heads_of(v):
    return _stack([v[:, h * HD:(h + 1) * HD] for h in range(HB)])


def _bd(a, b, dn):
    return lax.dot_general(a, b, dn, preferred_element_type=f32)


def _run_sum(v, tri):
    hi, lo = _split(v.reshape(HB * TQ, TQ))
    return (jnp.dot(hi, tri, preferred_element_type=f32) + jnp.dot(lo, tri, preferred_element_type=f32)).reshape(HB, TQ, TQ)


def _tri(later):
    j = lax.broadcasted_iota(jnp.int32, (TQ, TQ), 0)
    u = lax.broadcasted_iota(jnp.int32, (TQ, TQ), 1)
    return (j > u if later else j < u).astype(bf16)


def _sb_tile(q, k_ref, kb, carry, diagonal):
    k = _heads_of(k_ref[pl.ds(pl.multiple_of(kb * TQ, TQ), TQ), :].astype(bf16))
    z = _bd(q, k, BNT)
    ls = jax.nn.log_sigmoid(-z)
    if diagonal:
        mask = lax.broadcasted_iota(jnp.int32, (TQ, TQ), 1) < lax.broadcasted_iota(jnp.int32, (TQ, TQ), 0)
        ls = jnp.where(mask, ls, 0.0)
    a = jnp.exp(ls + z + _run_sum(ls, _tri(True)) + carry)
    return z, ls, jnp.where(mask, a, 0.0) if diagonal else a


def sb_fwd(proj, name):
    s = proj.shape[0]
    nq = s // TQ
    scale = HD ** -0.5

    def body(q_ref, k_ref, v_ref, y_ref):
        qi = pl.program_id(1)
        q = _heads_of((q_ref[...] * scale).astype(bf16))

        def step(j, c, diagonal=False):
            acc, carry = c
            kb = qi - j
            _, ls, a = _sb_tile(q, k_ref, kb, carry, diagonal)
            v = _heads_of(v_ref[pl.ds(pl.multiple_of(kb * TQ, TQ), TQ), :].astype(bf16))
            return acc + _bd(a.astype(bf16), v, BNN), carry + jnp.sum(ls, axis=2, keepdims=True)

        first = step(0, (jnp.zeros((HB, TQ, HD), f32), jnp.zeros((HB, TQ, 1), f32)), True)
        acc, _ = lax.fori_loop(1, qi + 1, step, first)
        for h in range(HB):
            y_ref[:, h * HD:(h + 1) * HD] = acc[h].astype(bf16)

    c0, ng = OFF_SB // WB, NH // HB
    return pl.pallas_call(
        body, name=name, grid=(ng, nq),
        in_specs=[pl.BlockSpec((TQ, WB), lambda g, i: (i, c0 + g)), pl.BlockSpec((s, WB), lambda g, i: (0, c0 + ng + g)),
                  pl.BlockSpec((s, WB), lambda g, i: (0, c0 + 2 * ng + g))],
        out_specs=pl.BlockSpec((TQ, WB), lambda g, i: (i, g)), out_shape=_sds((s, WH), bf16),
        compiler_params=_cp(("arbitrary", "arbitrary")),
    )(proj, proj, proj)


def sb_bwd(proj, dy, name):
    s = proj.shape[0]
    nq = s // TQ
    scale = HD ** -0.5

    def body(q_ref, k_ref, v_ref, do_ref, dq_ref, dk_ref, dv_ref, e_scr, z_scr, dk_acc, dv_acc):
        qi = pl.program_id(1)
        q = _heads_of((q_ref[...] * scale).astype(bf16))
        do = _heads_of(do_ref[...])

        @pl.when(qi == 0)
        def _():
            dk_acc[...] = jnp.zeros_like(dk_acc)
            dv_acc[...] = jnp.zeros_like(dv_acc)

        def add_heads(acc_ref, rows, upd):
            for h in range(HB):
                acc_ref[rows, h * HD:(h + 1) * HD] += upd[h]

        def sweep_back(j, carry, diagonal=False):
            kb = qi - j
            rows = pl.ds(pl.multiple_of(kb * TQ, TQ), TQ)
            z, ls, a = _sb_tile(q, k_ref, kb, carry, diagonal)
            v = _heads_of(v_ref[rows, :].astype(bf16))
            e_scr[kb] = _bd(do, v, BNT) * a
            z_scr[kb] = z
            add_heads(dv_acc, rows, _bd(a.astype(bf16), do, BTN))
            return carry + jnp.sum(ls, axis=2, keepdims=True)

        lax.fori_loop(1, qi + 1, sweep_back, sweep_back(0, jnp.zeros((HB, TQ, 1), f32), True))

        def sweep_fwd(kb, c, diagonal=False):
            dq, carry = c
            rows = pl.ds(pl.multiple_of(kb * TQ, TQ), TQ)
            e, z = e_scr[kb], z_scr[kb]
            sig = jax.nn.sigmoid(z)
            dz = e * (1.0 - sig) - (_run_sum(e, _tri(False)) + carry) * sig
            if diagonal:
                dz = jnp.where(lax.broadcasted_iota(jnp.int32, (TQ, TQ), 1) < lax.broadcasted_iota(jnp.int32, (TQ, TQ), 0), dz, 0.0)
            dz = dz.astype(bf16)
            k = _heads_of(k_ref[rows, :].astype(bf16))
            add_heads(dk_acc, rows, _bd(dz, q, BTN))
            return dq + _bd(dz, k, BNN), carry + jnp.sum(e, axis=2, keepdims=True)

        dq, _ = sweep_fwd(qi, lax.fori_loop(0, qi, sweep_fwd, (jnp.zeros((HB, TQ, HD), f32), jnp.zeros((HB, TQ, 1), f32))), True)
        for h in range(HB):
            dq_ref[:, h * HD:(h + 1) * HD] = (dq[h] * scale).astype(bf16)

        @pl.when(qi == nq - 1)
        def _():
            dk_ref[...] = dk_acc[...].astype(bf16)
            dv_ref[...] = dv_acc[...].astype(bf16)

    c0, ng = OFF_SB // WB, NH // HB
    return pl.pallas_call(
        body, name=name, grid=(ng, nq),
        in_specs=[pl.BlockSpec((TQ, WB), lambda g, i: (i, c0 + g)), pl.BlockSpec((s, WB), lambda g, i: (0, c0 + ng + g)),
                  pl.BlockSpec((s, WB), lambda g, i: (0, c0 + 2 * ng + g)), pl.BlockSpec((TQ, WB), lambda g, i: (i, g))],
        out_specs=[pl.BlockSpec((TQ, WB), lambda g, i: (i, g)), pl.BlockSpec((s, WB), lambda g, i: (0, g)),
                   pl.BlockSpec((s, WB), lambda g, i: (0, g))],
        out_shape=[_sds((s, WH), bf16)] * 3,
        scratch_shapes=[pltpu.VMEM((nq, HB, TQ, TQ), f32), pltpu.VMEM((nq, HB, TQ, TQ), f32), pltpu.VMEM((s, WB), f32),
                        pltpu.VMEM((s, WB), f32)],
        compiler_params=_cp(("arbitrary", "arbitrary")),
    )(proj, proj, proj, dy)


CB = 256


def _shift_down(v, k, s):
    if k == 0:
        return v
    row = lax.broadcasted_iota(jnp.int32, v.shape, 0)
    return jnp.where(row < k, 0.0, pltpu.roll(v, k, axis=0))


def _shift_up(v, k, s):
    if k == 0:
        return v
    row = lax.broadcasted_iota(jnp.int32, v.shape, 0)
    return jnp.where(row >= s - k, 0.0, pltpu.roll(v, s - k, axis=0))


def conv_fwd(proj, w, name):
    s = proj.shape[0]

    def body(x_ref, w_ref, y_ref):
        x, wv = x_ref[...], w_ref[...]
        y = sum(wv[3 - k:4 - k, :] * _shift_down(x, k, s) for k in range(4))
        y_ref[...] = jax.nn.silu(y)

    return pl.pallas_call(
        body, name=name, grid=(3 * WH // CB,),
        in_specs=[pl.BlockSpec((s, CB), lambda j: (0, OFF_GQ // CB + j)), pl.BlockSpec((4, CB), lambda j: (0, j))],
        out_specs=pl.BlockSpec((s, CB), lambda j: (0, j)), out_shape=_sds((s, 3 * WH), f32),
        compiler_params=_cp(("parallel",)),
    )(proj, w)


def conv_bwd(proj, w, dc, name, after=()):
    s = proj.shape[0]

    def body(*refs):
        x_ref, w_ref, dc_ref, dx_ref, dw_ref = refs[len(after):]
        x, wv = x_ref[...], w_ref[...]
        xs = [_shift_down(x, k, s) for k in range(4)]
        y = sum(wv[3 - k:4 - k, :] * xs[k] for k in range(4))
        sig = jax.nn.sigmoid(y)
        dy = dc_ref[...] * (sig * (1.0 + y * (1.0 - sig)))
        dx_ref[...] = sum(wv[3 - k:4 - k, :] * _shift_up(dy, k, s) for k in range(4)).astype(bf16)
        dw_ref[...] = jnp.concatenate([jnp.sum(dy * xs[3 - i], axis=0, keepdims=True) for i in range(4)], axis=0)

    return pl.pallas_call(
        body, name=name, grid=(3 * WH // CB,),
        in_specs=[pl.BlockSpec(memory_space=pl.ANY)] * len(after)
        + [pl.BlockSpec((s, CB), lambda j: (0, OFF_GQ // CB + j)), pl.BlockSpec((4, CB), lambda j: (0, j)),
           pl.BlockSpec((s, CB), lambda j: (0, j))],
        out_specs=[pl.BlockSpec((s, CB), lambda j: (0, j)), pl.BlockSpec((4, CB), lambda j: (0, j))],
        out_shape=[_sds((s, 3 * WH), bf16), _sds((4, 3 * WH), f32)],
        compiler_params=_cp(("parallel",)),
    )(*after, proj, w, dc)


SOLVE_PASSES = 3


def _pdot_impl(a, b, dn, passes):
    ah, al = _split(a)
    bh, bl = _split(b)
    dot = lambda p, q: lax.dot_general(p, q, dn, preferred_element_type=f32)
    if passes == 1:
        return dot(ah, bh)
    if passes == 2:
        return dot(ah, bh) + dot(ah, bl)
    return dot(ah, bh) + (dot(ah, bl) + dot(al, bh))


BNN, BNT, BTN = (((2,), (1,)), ((0,), (0,))), (((2,), (2,)), ((0,), (0,))), (((1,), (1,)), ((0,), (0,)))


@functools.partial(jax.custom_vjp, nondiff_argnums=(2,))
def _bdot(a, b, passes):
    return _pdot_impl(a, b, BNN, passes)


def _bdot_fwd(a, b, passes):
    return _pdot_impl(a, b, BNN, passes), (a, b)


def _bdot_bwd(passes, res, ct):
    a, b = res
    return _pdot_impl(ct, b, BNT, passes), _pdot_impl(a, ct, BTN, passes)


_bdot.defvjp(_bdot_fwd, _bdot_bwd)


@functools.partial(jax.custom_vjp, nondiff_argnums=(2,))
def _bdot_nt(a, b, passes):
    return _pdot_impl(a, b, BNT, passes)


def _bdot_nt_fwd(a, b, passes):
    return _pdot_impl(a, b, BNT, passes), (a, b)


def _bdot_nt_bwd(passes, res, ct):
    a, b = res
    return _pdot_impl(ct, b, BNN, passes), _pdot_impl(ct, a, BTN, passes)


_bdot_nt.defvjp(_bdot_nt_fwd, _bdot_nt_bwd)


def _lane_pick(v, h):
    lane = lax.broadcasted_iota(jnp.int32, v.shape, v.ndim - 1)
    return jnp.sum(jnp.where(lane == h, v, 0.0), axis=-1, keepdims=True)


def _stack(parts):
    return jnp.concatenate([p[None] for p in parts], axis=0)


def _heads(v, first):
    return _stack([_lane_pick(v, first + h) for h in range(NH)])


def _l2n(v):
    return v * lax.rsqrt(jnp.sum(v * v, axis=-1, keepdims=True) + EPS)


def _dot_nt(a, b):
    return lax.dot_general(a, b, (((1,), (1,)), ((), ())), preferred_element_type=f32)


def _inverse_impl(lower):
    i = lax.broadcasted_iota(jnp.int32, (CH, CH), 0)
    j = lax.broadcasted_iota(jnp.int32, (CH, CH), 1)
    inv = (i == j).astype(f32) - lower
    pw = _pdot_impl(lower, lower, BNN, SOLVE_PASSES)
    for m in range(1, int(math.log2(CH))):
        inv = inv + _pdot_impl(inv, pw, BNN, SOLVE_PASSES)
        if m < int(math.log2(CH)) - 1:
            pw = _pdot_impl(pw, pw, BNN, SOLVE_PASSES)
    return inv


@jax.custom_vjp
def _unit_lower_inverse(lower):
    return _inverse_impl(lower)


def _unit_lower_inverse_fwd(lower):
    inv = _inverse_impl(lower)
    return inv, inv


def _unit_lower_inverse_bwd(inv, ct):
    return (-_pdot_impl(_pdot_impl(inv, ct, BTN, SOLVE_PASSES), inv, BNT, SOLVE_PASSES),)


_unit_lower_inverse.defvjp(_unit_lower_inverse_fwd, _unit_lower_inverse_bwd)


def _gdn_chunk(cq, ck, cv, ab, alog, dtb):
    ones = jnp.ones((NH, CH, HD), f32)
    q = _l2n(cq) * (HD ** -0.5)
    k = _l2n(ck)
    la = -jnp.exp(_heads(alog, 0)) * jax.nn.softplus(_heads(ab, AB_LANE) + _heads(dtb, 0))
    beta = jax.nn.sigmoid(_heads(ab, AB_LANE + NH)) * ones
    i = lax.broadcasted_iota(jnp.int32, (CH, CH), 0)
    j = lax.broadcasted_iota(jnp.int32, (CH, CH), 1)
    incl, strict = j <= i, j < i
    g = _bdot(jnp.broadcast_to(incl.astype(f32), (NH, CH, CH)), la * ones, 2)
    g_row = _stack([g[h].T for h in range(NH)])
    gamma = jnp.where(incl, jnp.exp(jnp.where(incl, g - g_row, 0.0)), 0.0)
    lower = jnp.where(strict, beta * _bdot_nt(k, k, 1) * gamma, 0.0)
    inv = _unit_lower_inverse(lower)
    eg = jnp.exp(g)
    u = _bdot(inv, cv * beta, SOLVE_PASSES)
    w = _bdot(inv, k * (beta * eg), SOLVE_PASSES)
    qk = _bdot_nt(q, k, 1) * gamma
    g_last = g[:, CH - 1:CH, :]
    return u, w, q * eg, k * jnp.exp(g_last - g), qk, jnp.exp(g_last)


def _by_head(ref, t=0):
    return _stack([ref[:, t * WH + h * HD:t * WH + (h + 1) * HD] for h in range(NH)])


def gdn_a_fwd(c, proj, alog, dtb, name):
    s = c.shape[0]
    nc = s // CH

    def body(c_ref, ab_ref, al_ref, dt_ref, u_ref, w_ref, qd_ref, kd_ref, qk_ref, dec_ref):
        res = _gdn_chunk(_by_head(c_ref, 0), _by_head(c_ref, 1), _by_head(c_ref, 2), ab_ref[...], al_ref[...], dt_ref[...])
        for h in range(NH):
            sl = slice(h * HD, (h + 1) * HD)
            for r, v in zip((u_ref, w_ref, qd_ref, kd_ref, qk_ref), res[:5]):
                r[:, sl] = v[h]
            dec_ref[:, sl] = jnp.broadcast_to(res[5][h], (8, HD))

    row = pl.BlockSpec((CH, WH), lambda n: (n, 0))
    par = pl.BlockSpec((1, HD), lambda n: (0, 0))
    return pl.pallas_call(
        body, name=name, grid=(nc,),
        in_specs=[pl.BlockSpec((CH, 3 * WH), lambda n: (n, 0)), pl.BlockSpec((CH, HD), lambda n: (n, OFF_AB // HD)), par, par],
        out_specs=[row] * 5 + [pl.BlockSpec((None, 8, WH), lambda n: (n, 0, 0))],
        out_shape=[_sds((s, WH), f32)] * 5 + [_sds((nc, 8, WH), f32)],
        compiler_params=_cp(("parallel",)),
    )(c, proj, alog, dtb)


def gdn_a_bwd(c, proj, alog, dtb, cots, ddec, name, swap=()):
    s = c.shape[0]
    nc = s // CH
    gs, axes = swap if swap else ((), ())
    ns = len(gs)
    shards = [tuple(d // (NDEV if a == ax else 1) for a, d in enumerate(g.shape)) for g, ax in zip(gs, axes)]

    def body(c_ref, ab_ref, al_ref, dt_ref, du_ref, dw_ref, dqd_ref, dkd_ref, dqk_ref, ddec_ref, *rest):
        g_refs, (dc_ref, dab_ref, dal_ref, ddt_ref), p_refs = rest[:ns], rest[ns:ns + 4], rest[ns + 4:2 * ns + 4]
        n = pl.program_id(0)

        def copies():
            send, recv = rest[2 * ns + 4:]
            x, y, core = lax.axis_index("x"), lax.axis_index("y"), lax.axis_index("c")
            return [pltpu.make_async_remote_copy(
                src_ref=_shard_of(g_refs[t], axes[t], shards[t][axes[t]], 2 * j + (1 - core)), dst_ref=p_refs[t].at[j],
                send_sem=send.at[t, j], recv_sem=recv.at[t, j], device_id=(x, y, 1 - core), device_id_type=MESH)
                for t in range(ns) for j in range(4)]

        if ns:
            @pl.when(n == 0)
            def _():
                for cp in copies():
                    cp.start()

        _, vjp = jax.vjp(_gdn_chunk, _by_head(c_ref, 0), _by_head(c_ref, 1), _by_head(c_ref, 2), ab_ref[...], al_ref[...],
                         dt_ref[...])
        lane = lax.broadcasted_iota(jnp.int32, (1, HD), 1)
        dd = _stack([jnp.where(lane == 0, ddec_ref[0:1, h * HD:(h + 1) * HD], 0.0) for h in range(NH)])
        dcq, dck, dcv, dab, dal, ddt = vjp(tuple(_by_head(r) for r in (du_ref, dw_ref, dqd_ref, dkd_ref, dqk_ref)) + (dd,))
        for h in range(NH):
            for t, v in enumerate((dcq, dck, dcv)):
                dc_ref[:, t * WH + h * HD:t * WH + (h + 1) * HD] = v[h]
        dab_ref[...] = dab.astype(bf16)

        @pl.when(n == 0)
        def _():
            dal_ref[...] = dal
            ddt_ref[...] = ddt

        @pl.when(n > 0)
        def _():
            dal_ref[...] += dal
            ddt_ref[...] += ddt

        if ns:
            @pl.when(n == nc - 1)
            def _():
                cps = copies()
                for cp in cps:
                    cp.wait_recv()
                for cp in cps:
                    cp.wait_send()

    row = pl.BlockSpec((CH, WH), lambda n: (n, 0))
    wide = pl.BlockSpec((CH, 3 * WH), lambda n: (n, 0))
    par = pl.BlockSpec((1, HD), lambda n: (0, 0))
    res = pl.pallas_call(
        body, name=name, grid=(nc,),
        in_specs=[wide, pl.BlockSpec((CH, HD), lambda n: (n, OFF_AB // HD)), par, par] + [row] * 5
        + [pl.BlockSpec((None, 8, WH), lambda n: (n, 0, 0))] + [ANY] * ns,
        out_specs=[wide, pl.BlockSpec((CH, HD), lambda n: (n, 0)), par, par] + [ANY] * ns,
        out_shape=[_sds((s, 3 * WH), f32), _sds((s, HD), bf16), _sds((1, HD), f32), _sds((1, HD), f32)]
        + [_sds((4,) + sh, g.dtype) for sh, g in zip(shards, gs)],
        scratch_shapes=[pltpu.SemaphoreType.DMA((ns, 4)), pltpu.SemaphoreType.DMA((ns, 4))] if ns else [],
        compiler_params=_cp(("arbitrary",)),
    )(c, proj, alog, dtb, *cots, ddec, *gs)
    return tuple(res[:4]) + (list(res[4:]),)


def gdn_b_fwd(u, w, qd, kd, qk, dec, name):
    s = u.shape[0]
    nc = s // CH

    def body(u_ref, w_ref, qd_ref, kd_ref, qk_ref, dec_ref, o_ref, st_ref, state):
        n = pl.program_id(0)

        @pl.when(n == 0)
        def _():
            state[...] = jnp.zeros_like(state)

        for h in range(NH):
            sl = slice(h * HD, (h + 1) * HD)
            st = state[sl, :]
            st_ref[sl, :] = st
            sb = st.astype(bf16)
            vn = u_ref[:, sl] - jnp.dot(w_ref[:, sl].astype(bf16), sb, preferred_element_type=f32)
            vb = vn.astype(bf16)
            o_ref[:, sl] = (jnp.dot(qd_ref[:, sl].astype(bf16), sb, preferred_element_type=f32)
                            + jnp.dot(qk_ref[:, sl].astype(bf16), vb, preferred_element_type=f32))
            state[sl, :] = st * dec_ref[0:1, sl] + lax.dot_general(
                kd_ref[:, sl].astype(bf16), vb, (((0,), (0,)), ((), ())), preferred_element_type=f32)

    row = pl.BlockSpec((CH, WH), lambda n: (n, 0))
    return pl.pallas_call(
        body, name=name, grid=(nc,),
        in_specs=[row] * 5 + [pl.BlockSpec((None, 8, WH), lambda n: (n, 0, 0))],
        out_specs=[row, pl.BlockSpec((None, WH, HD), lambda n: (n, 0, 0))],
        out_shape=[_sds((s, WH), f32), _sds((nc, WH, HD), f32)],
        scratch_shapes=[pltpu.VMEM((WH, HD), f32)],
        compiler_params=_cp(("arbitrary",)),
    )(u, w, qd, kd, qk, dec)


def gdn_b_bwd(u, w, qd, kd, qk, dec, states, do, name):
    s = u.shape[0]
    nc = s // CH

    def body(u_ref, w_ref, qd_ref, kd_ref, qk_ref, dec_ref, st_ref, do_ref,
             du_ref, dw_ref, dqd_ref, dkd_ref, dqk_ref, ddec_ref, dstate):
        n = pl.program_id(0)

        @pl.when(n == 0)
        def _():
            dstate[...] = jnp.zeros_like(dstate)

        for h in range(NH):
            sl = slice(h * HD, (h + 1) * HD)
            st, ds = st_ref[sl, :], dstate[sl, :]
            sb, dsb = st.astype(bf16), ds.astype(bf16)
            wb, qdb, kdb, qkb = (r[:, sl].astype(bf16) for r in (w_ref, qd_ref, kd_ref, qk_ref))
            dob = do_ref[:, sl].astype(bf16)
            vn = u_ref[:, sl] - jnp.dot(wb, sb, preferred_element_type=f32)
            vb = vn.astype(bf16)
            dvn = (lax.dot_general(qkb, dob, (((0,), (0,)), ((), ())), preferred_element_type=f32)
                   + jnp.dot(kdb, dsb, preferred_element_type=f32))
            dvb = dvn.astype(bf16)
            du_ref[:, sl] = dvn
            dw_ref[:, sl] = -_dot_nt(dvb, sb)
            dqd_ref[:, sl] = _dot_nt(dob, sb)
            dkd_ref[:, sl] = _dot_nt(vb, dsb)
            dqk_ref[:, sl] = _dot_nt(dob, vb)
            tot = jnp.sum(jnp.sum(ds * st, axis=1, keepdims=True), axis=0, keepdims=True)
            ddec_ref[:, sl] = jnp.broadcast_to(tot, (8, HD))
            dstate[sl, :] = (ds * dec_ref[0:1, sl]
                             + lax.dot_general(qdb, dob, (((0,), (0,)), ((), ())), preferred_element_type=f32)
                             - lax.dot_general(wb, dvb, (((0,), (0,)), ((), ())), preferred_element_type=f32))

    row = pl.BlockSpec((CH, WH), lambda n: (nc - 1 - n, 0))
    small = pl.BlockSpec((None, 8, WH), lambda n: (nc - 1 - n, 0, 0))
    return pl.pallas_call(
        body, name=name, grid=(nc,),
        in_specs=[row] * 5 + [small, pl.BlockSpec((None, WH, HD), lambda n: (nc - 1 - n, 0, 0)), row],
        out_specs=[row] * 5 + [small],
        out_shape=[_sds((s, WH), f32)] * 5 + [_sds((nc, 8, WH), f32)],
        scratch_shapes=[pltpu.VMEM((WH, HD), f32)],
        compiler_params=_cp(("arbitrary",)),
    )(u, w, qd, kd, qk, dec, states, do)


ANY = pl.BlockSpec(memory_space=pl.ANY)


def _dev_index(p):
    return 4 * p[0] + 2 * p[1] + p[2]


def _shard_of(ref, axis, size, idx):
    return ref.at[pl.ds(idx * size, size), :] if axis == 0 else ref.at[:, pl.ds(idx * size, size)]


def _peer(x, y, c, r):
    return (1 - x if r & 4 else x, 1 - y if r & 2 else y, 1 - c if r & 1 else c)


ALL_PEERS, SIBLING, OTHER_CHIPS = tuple(range(1, NDEV)), (1,), (4, 2, 6)


def _launch(body, ins, out_sds, name, sequencer_id, relations=ALL_PEERS, kinds=7):
    n = len(ins)
    sems = [pltpu.SemaphoreType.DMA((n, kinds)), pltpu.SemaphoreType.DMA((n, kinds)), pltpu.SemaphoreType.DMA((n,))]
    if sequencer_id is None:
        return pl.pallas_call(
            lambda *refs: body(refs[:n], refs[n:2 * n], *refs[2 * n:]), name=name, in_specs=[ANY] * n, out_specs=[ANY] * n,
            out_shape=out_sds, scratch_shapes=sems, compiler_params=pltpu.CompilerParams(has_side_effects=True),
        )(*ins)
    in_refs = [jax.new_ref(a, memory_space=pltpu.MemorySpace.HBM) for a in ins]
    out_refs = [jax.empty_ref(sd, memory_space=pltpu.MemorySpace.HBM) for sd in out_sds]

    @pl.kernel(mesh=plsc.ScalarSubcoreMesh(axis_name="sequencer", num_cores=1), name=name, scratch_types=sems,
               compiler_params=pltpu.CompilerParams(collective_id=sequencer_id))
    def launch(send, recv, loc):
        x, y, c = lax.axis_index("x"), lax.axis_index("y"), lax.axis_index("c")
        barrier = pltpu.get_barrier_semaphore()
        for r in relations:
            pl.semaphore_signal(barrier, inc=1, device_id=_peer(x, y, c, r), device_id_type=MESH)
        pl.semaphore_wait(barrier, len(relations))
        body(in_refs, out_refs, send, recv, loc)

    launch()
    return [r[...] for r in out_refs]


def all_gather(xs, axes, name, sequencer_id=None):
    n = len(xs)
    fulls = [tuple(d * (NDEV if a == ax else 1) for a, d in enumerate(x.shape)) for x, ax in zip(xs, axes)]
    halved = [x.shape[0] % 32 == 0 for x in xs]

    def body(x_refs, o_refs, send, recv, loc):
        x, y, c = lax.axis_index("x"), lax.axis_index("y"), lax.axis_index("c")
        me, sib = (x, y, c), (x, y, 1 - c)
        xn, yn, dg = (1 - x, y), (x, 1 - y), (1 - x, 1 - y)

        def part(t, p, half=None):
            ref = _shard_of(o_refs[t], axes[t], xs[t].shape[axes[t]], _dev_index(p))
            rows = xs[t].shape[0] // 2
            return ref if half is None else ref.at[pl.ds(half * rows, rows), :]

        def copy(t, k, block, to, half=None, src=None):
            return pltpu.make_async_remote_copy(
                src_ref=part(t, block, half) if src is None else src, dst_ref=part(t, block, half),
                send_sem=send.at[t, k], recv_sem=recv.at[t, k], device_id=to, device_id_type=MESH)

        mine = [pltpu.make_async_copy(x_refs[t], part(t, me), loc.at[t]) for t in range(n)]
        for cp in mine:
            cp.start()
        sends = []
        for t in range(n):
            sends += [copy(t, 0, me, sib, src=x_refs[t]), copy(t, 1, me, (*xn, c), src=x_refs[t]),
                      copy(t, 2, me, (*yn, c), src=x_refs[t])]
            if not halved[t]:
                sends.append(copy(t, 3, me, (*dg, c), src=x_refs[t]))
        for cp in sends:
            cp.start()

        def pass_on(cp):
            cp.start()
            sends.append(cp)

        for t in range(n):
            h0, h1 = (0, 1) if halved[t] else (None, None)
            copy(t, 1, (*xn, c), me).wait_recv()
            if halved[t]:
                pass_on(copy(t, 3, (*xn, c), (*yn, c), 0))
            pass_on(copy(t, 5, (*xn, c), sib))
            copy(t, 2, (*yn, c), me).wait_recv()
            if halved[t]:
                pass_on(copy(t, 4, (*yn, c), (*xn, c), 1))
            pass_on(copy(t, 6, (*yn, c), sib))
            copy(t, 3, (*dg, c), me, h0).wait_recv()
            pass_on(copy(t, 7, (*dg, c), sib, h0))
            if halved[t]:
                copy(t, 4, (*dg, c), me, h1).wait_recv()
                pass_on(copy(t, 8, (*dg, c), sib, h1))
        for t in range(n):
            h0, h1 = (0, 1) if halved[t] else (None, None)
            copy(t, 0, sib, me).wait_recv()
            copy(t, 5, (*xn, 1 - c), me).wait_recv()
            copy(t, 6, (*yn, 1 - c), me).wait_recv()
            copy(t, 7, (*dg, 1 - c), me, h0).wait_recv()
            if halved[t]:
                copy(t, 8, (*dg, 1 - c), me, h1).wait_recv()
        for cp in sends:
            cp.wait_send()
        for cp in mine:
            cp.wait()

    return _launch(body, xs, [_sds(f, x.dtype) for f, x in zip(fulls, xs)], name, sequencer_id, kinds=9)


def pair_swap(gs, axes, name, sequencer_id):
    n = len(gs)
    shards = [tuple(d // (NDEV if a == ax else 1) for a, d in enumerate(g.shape)) for g, ax in zip(gs, axes)]

    def body(g_refs, p_refs, send, recv, loc):
        x, y, c = lax.axis_index("x"), lax.axis_index("y"), lax.axis_index("c")

        def copy(t, j):
            return pltpu.make_async_remote_copy(
                src_ref=_shard_of(g_refs[t], axes[t], shards[t][axes[t]], 2 * j + (1 - c)), dst_ref=p_refs[t].at[j],
                send_sem=send.at[t, j], recv_sem=recv.at[t, j], device_id=(x, y, 1 - c), device_id_type=MESH)

        copies = [copy(t, j) for t in range(n) for j in range(4)]
        for cp in copies:
            cp.start()
        for cp in copies:
            cp.wait_recv()
        for cp in copies:
            cp.wait_send()

    return _launch(body, gs, [_sds((4,) + sh, g.dtype) for sh, g in zip(shards, gs)], name, sequencer_id, SIBLING)


def pair_add(g, p, axis, name, after=()):
    _, rows, cols = p.shape
    tr = _row_tile(rows, cols, 1 << 20)
    nb = rows // tr
    if axis == 0:
        g_spec = pl.BlockSpec((tr, cols), lambda j, i, c: ((2 * j + c[0]) * nb + i, 0))
    else:
        g_spec = pl.BlockSpec((tr, cols), lambda j, i, c: (i, 2 * j + c[0]))
    blk = pl.BlockSpec((None, tr, cols), lambda j, i, c: (j, i, 0))

    na = len(after)

    def body(c_ref, *refs):
        g_ref, p_ref, q_ref = refs[na:]
        q_ref[...] = (g_ref[...].astype(f32) + p_ref[...].astype(f32)).astype(bf16)

    return pl.pallas_call(
        body, name=name, out_shape=_sds(p.shape, bf16),
        grid_spec=pltpu.PrefetchScalarGridSpec(num_scalar_prefetch=1, grid=(4, nb), in_specs=[ANY] * na + [g_spec, blk],
                                               out_specs=blk),
        compiler_params=_cp(("parallel", "parallel")),
    )(lax.axis_index("c").astype(jnp.int32).reshape(1), *after, g, p)


def chip_exchange(qs, name, sequencer_id):
    n = len(qs)

    def body(q_refs, r_refs, send, recv, loc):
        x, y, c = lax.axis_index("x"), lax.axis_index("y"), lax.axis_index("c")
        my_chip = 2 * x + y
        peers = [_peer(x, y, c, r) for r in OTHER_CHIPS]
        mine = [pltpu.make_async_copy(q_refs[t].at[my_chip], r_refs[t].at[my_chip], loc.at[t]) for t in range(n)]
        for cp in mine:
            cp.start()

        def copy(t, k, slot):
            p = peers[k]
            return pltpu.make_async_remote_copy(
                src_ref=q_refs[t].at[2 * p[0] + p[1]], dst_ref=r_refs[t].at[slot], send_sem=send.at[t, k], recv_sem=recv.at[t, k],
                device_id=p, device_id_type=MESH)

        sends = [copy(t, k, my_chip) for t in range(n) for k in range(3)]
        for cp in sends:
            cp.start()
        for t in range(n):
            for k in range(3):
                copy(t, k, 2 * peers[k][0] + peers[k][1]).wait_recv()
        for cp in sends:
            cp.wait_send()
        for cp in mine:
            cp.wait()

    return _launch(body, qs, [_sds(q.shape, q.dtype) for q in qs], name, sequencer_id, OTHER_CHIPS)


def _adamw(w, g, m, v):
    m = ADAM_B1 * m + (1.0 - ADAM_B1) * g
    v = ADAM_B2 * v + (1.0 - ADAM_B2) * jnp.square(g)
    m_hat = m / (1.0 - ADAM_B1 ** ADAM_STEP)
    v_hat = v / (1.0 - ADAM_B2 ** ADAM_STEP)
    return -ADAM_LR * (m_hat / (jnp.sqrt(v_hat) + ADAM_EPS) + ADAM_WD * w), m, v


def _sum8(r_ref):
    g = r_ref[0].astype(f32)
    for j in range(1, r_ref.shape[0]):
        g = g + r_ref[j].astype(f32)
    return g


def _row_tile(rows, cols, elems=1 << 18):
    tr = min(rows, max(8, 1 << int(math.log2(elems / cols))))
    while rows % tr:
        tr //= 2
    assert tr % 8 == 0 or tr == rows, (rows, cols)
    return tr


def sum_partials(r, name, after=()):
    _, rows, cols = r.shape
    tr = _row_tile(rows, cols)
    na = len(after)

    def body(*refs):
        refs[na + 1][...] = _sum8(refs[na])

    return pl.pallas_call(
        body, name=name, grid=(rows // tr,),
        in_specs=[ANY] * na + [pl.BlockSpec((r.shape[0], tr, cols), lambda i: (0, i, 0))],
        out_specs=pl.BlockSpec((tr, cols), lambda i: (i, 0)), out_shape=_sds((rows, cols), f32),
        compiler_params=_cp(("parallel",)),
    )(*after, r)


def adamw_t(w, m, v, grads, name):
    rows, nl, cols = w.shape
    tr = min(rows, (1 << 16) // cols)
    blk = pl.BlockSpec((tr, nl, cols), lambda i: (i, 0, 0))
    flat = pl.BlockSpec((tr, cols), lambda i: (i, 0))

    def body(w_ref, m_ref, v_ref, *rest):
        g_refs, (g_ref, d_ref, nm_ref, nv_ref) = rest[:nl], rest[nl:]
        for l in range(nl):
            grad = g_refs[l][...]
            g_ref[:, l, :] = grad
            d_ref[:, l, :], nm_ref[:, l, :], nv_ref[:, l, :] = _adamw(w_ref[:, l, :], grad, m_ref[:, l, :], v_ref[:, l, :])

    return tuple(pl.pallas_call(
        body, name=name, grid=(pl.cdiv(rows, tr),), in_specs=[blk] * 3 + [flat] * nl, out_specs=[blk] * 4,
        out_shape=[_sds(w.shape, f32)] * 4, compiler_params=_cp(("parallel",)),
    )(w, m, v, *grads))


def adamw(w, m, v, layer, name, r=None, g=None, prev=None):
    _, rows, cols = w.shape
    tr = _row_tile(rows, cols)
    blk = pl.BlockSpec((None, tr, cols), lambda i: (layer, i, 0))
    nprev = 0 if prev is None else 4

    def body(w_ref, m_ref, v_ref, src_ref, *rest):
        g_ref, d_ref, nm_ref, nv_ref, token_ref = rest[nprev:]
        grad = _sum8(src_ref) if g is None else src_ref[...]
        g_ref[...] = grad
        d_ref[...], nm_ref[...], nv_ref[...] = _adamw(w_ref[...], grad, m_ref[...], v_ref[...])
        token_ref[...] = jnp.zeros_like(token_ref)

    src, src_spec = (r, pl.BlockSpec((r.shape[0], tr, cols), lambda i: (0, i, 0))) if g is None else (g, pl.BlockSpec((tr, cols), lambda i: (i, 0)))
    *outs, token = pl.pallas_call(
        body, name=name, grid=(rows // tr,), in_specs=[blk] * 3 + [src_spec] + [ANY] * nprev,
        out_specs=[blk] * 4 + [pl.BlockSpec((8, HD), lambda i: (0, 0))],
        out_shape=[_sds(w.shape, f32)] * 4 + [_sds((8, HD), f32)], input_output_aliases={4 + k: k for k in range(nprev)},
        compiler_params=_cp(("arbitrary",)),
    )(w, m, v, src, *(prev or ()))
    return tuple(outs), token


def small_adamw(parts, w, m, v, name, after=()):
    def body(*refs):
        p_ref, w_ref, m_ref, v_ref, g_ref, d_ref, nm_ref, nv_ref = refs[len(after):]
        g = _sum8(p_ref)
        g_ref[...] = g
        d_ref[...], nm_ref[...], nv_ref[...] = _adamw(w_ref[...], g, m_ref[...], v_ref[...])

    vmem = pl.BlockSpec(memory_space=pltpu.VMEM)
    return pl.pallas_call(body, name=name, out_shape=[_sds(w.shape, f32)] * 4, in_specs=[ANY] * len(after) + [vmem] * 4,
                          out_specs=[vmem] * 4, compiler_params=_cp())(*after, parts, w, m, v)


def _pack_rows(wt):
    tail = jnp.pad(wt[5888:N_IN], ((OFF_G - 12 - (OFF_P + 512), 0), (0, 0)))
    return jnp.concatenate([wt[512:2816], wt[2816:5120], wt[5120:5888], wt[0:512], tail], axis=0)


def _unpack_rows(g):
    return jnp.concatenate([g[OFF_P:OFF_P + 512], g[OFF_SB:OFF_SB + 2304], g[OFF_GQ:OFF_GQ + 2304], g[OFF_Z:OFF_Z + 768],
                            g[OFF_G - 12:NP]], axis=0)


def _lanes(v):
    flat = v.reshape(-1)
    n = -(-flat.shape[0] // HD) * HD
    return jnp.pad(flat, (0, n - flat.shape[0])).reshape(n // HD, HD)


def _lanes8(v):
    rows = _lanes(v)
    return jnp.pad(rows, ((0, -rows.shape[0] % 8), (0, 0)))


def _layer_fwd(x, p, l):
    nm = lambda s: f"{s}_l{l}"
    u = rms_fwd(x, p["attn_norm"], nm("rms1"))
    proj = matmul(u, p["w_in"][0], name=nm("inproj"), tb=True)
    y_pool = pool_fwd(proj, p["pool_w"], p["pool_scale"], nm("pool"))
    y_sb = sb_fwd(proj, nm("sb"))
    c = conv_fwd(proj, p["conv"], nm("conv"))
    ga = gdn_a_fwd(c, proj, p["alog"], p["dtb"], nm("gdna"))
    o, states = gdn_b_fwd(*ga, nm("gdnb"))
    y_gdn = gdn_out_fwd(o, proj, p["gdn_norm"], nm("gdno"))
    gates = matmul(u, p["w_in"][1], name=nm("inproj_gates"), tb=True, after=(y_gdn,))
    ups = [matmul(y, p[k], name=nm(k)) for y, k in ((y_pool, "w_pool_up"), (y_sb, "w_sb_up"), (y_gdn, "w_gdn_up"))]
    merged = merge_fwd(gates, ups, nm("merge"))
    x1 = matmul(merged, p["w_out"], name=nm("outproj"), epilogue=lambda acc, r: acc + r, extras=(x,))
    u2 = rms_fwd(x1, p["mlp_norm"], nm("rms2"))
    h2 = matmul(u2, p["w_ff1"], name=nm("ff1"), out_dtype=bf16, epilogue=lambda acc: jnp.square(jnp.maximum(acc, 0.0)))
    x2 = matmul(h2, p["w_ff2"], name=nm("ff2"), epilogue=lambda acc, r: acc + r, extras=(x1,))
    saved = dict(x=x, u=u, proj=proj, gates=gates, y_pool=y_pool, y_sb=y_sb, c=c, ga=ga, o=o, states=states, y_gdn=y_gdn, ups=ups,
                 merged=merged, x1=x1, u2=u2, h2=h2)
    return x2, saved


def _layer_bwd(dx2, dx2b, sv, p, l, swap, finish, swap_in_gdn=True):
    nm = lambda s: f"{s}_l{l}"
    s = dx2.shape[0]
    dh = matmul(dx2b, p["w_ff2"], name=nm("d_ff2_x"), tb=True, out_dtype=bf16,
                epilogue=lambda acc, h2: acc * (2.0 * jnp.sqrt(h2.astype(f32))), extras=(sv["h2"],))
    g_ff2 = matmul(sv["h2"], dx2b, name=nm("d_ff2_w"), ta=True, out_dtype=bf16)
    du2 = matmul(dh, p["w_ff1"], name=nm("d_ff1_x"), tb=True)
    g_ff1 = matmul(sv["u2"], dh, name=nm("d_ff1_w"), ta=True, out_dtype=bf16)
    dx1, dx1b, d_mlp_norm = rms_bwd(sv["x1"], du2, dx2, p["mlp_norm"], nm("d_rms2"))
    dmerged = matmul(dx1b, p["w_out"], name=nm("d_out_x"), tb=True, out_dtype=bf16)
    g_out = matmul(sv["merged"], dx1b, name=nm("d_out_w"), ta=True, out_dtype=bf16)
    dgates, *dups = merge_bwd(sv["gates"], sv["ups"], dmerged, nm("d_merge"))
    dys, g_ups = [], []
    for dup, y, k in zip(dups, (sv["y_pool"], sv["y_sb"], sv["y_gdn"]), ("w_pool_up", "w_sb_up", "w_gdn_up")):
        dys.append(matmul(dup, p[k], name=nm("d_" + k + "_x"), tb=True, out_dtype=bf16))
        g_ups.append(matmul(y, dup, name=nm("d_" + k + "_w"), ta=True, out_dtype=bf16))
    early = g_ups + [g_out, g_ff1, g_ff2]
    do, dz, d_gdn_norm = gdn_out_bwd(sv["o"], sv["proj"], p["gdn_norm"], dys[2], nm("d_gdno"))
    cots = gdn_b_bwd(*sv["ga"], sv["states"], do, nm("d_gdnb"))
    dc, dab, d_alog, d_dtb, swapped = gdn_a_bwd(sv["c"], sv["proj"], p["alog"], p["dtb"], cots[:5], cots[5], nm("d_gdna"),
                                                (early, BIG_AXES[1:]) if swap_in_gdn else ())
    r_early, sent_early = finish(early, swapped, BIG_AXES[1:], BIG_NAMES[1:], "a", (dab,))
    dgq, d_conv = conv_bwd(sv["proj"], p["conv"], dc, nm("d_conv"), sent_early)
    dq, dk, dv = sb_bwd(sv["proj"], dys[1], nm("d_sb"))
    dp, d_pool_w, d_pool_scale = pool_bwd(sv["proj"], p["pool_w"], p["pool_scale"], dys[0], nm("d_pool"))
    dproj = jnp.concatenate([dq, dk, dv, dgq, dz, dp, jnp.zeros((s, OFF_AB - OFF_P - W_POOL), bf16), dab], axis=1)
    g_in = [matmul(dproj, sv["u"], name=nm("d_in_w"), ta=True, out_dtype=bf16),
            matmul(dgates, sv["u"], name=nm("d_in_w_gates"), ta=True, out_dtype=bf16)]
    r_in, sent_in = finish(g_in, swap(g_in, (1, 1), "b"), (1, 1), ("w_in", "w_in_gates"), "b", ())
    du = matmul(dproj, p["w_in"][0], name=nm("d_in_x"), after=sent_in)
    du = matmul(dgates, p["w_in"][1], name=nm("d_in_x_gates"), epilogue=lambda acc, r: acc + r, extras=(du,))
    dx, dxb, d_attn_norm = rms_bwd(sv["x"], du, dx1, p["attn_norm"], nm("d_rms1"))
    recv = r_in + r_early
    small = [d_attn_norm, d_pool_w, d_pool_scale, d_conv, d_alog, d_dtb, d_gdn_norm, d_mlp_norm]
    return dx, dxb, recv, small


BIG_AXES = (1, 1, 1, 1, 0, 1, 0)
GATHER_ID, EXCHANGE_ID = 1, 2
BIG_NAMES = ("w_in", "w_pool_up", "w_sb_up", "w_gdn_up", "w_out", "w_ff1", "w_ff2")


def kernel(x, attn_norm, w_in, pool_w, pool_scale, gdn_conv, gdn_a_log, gdn_dt_bias, gdn_norm, w_pool_up, w_sb_up, w_gdn_up, w_out, mlp_norm, w_ff1, w_ff2, final_norm, loss_target, m_attn_norm, m_w_in, m_pool_w, m_pool_scale, m_gdn_conv, m_gdn_a_log, m_gdn_dt_bias, m_gdn_norm, m_w_pool_up, m_w_sb_up, m_w_gdn_up, m_w_out, m_mlp_norm, m_w_ff1, m_w_ff2, m_final_norm, v_attn_norm, v_w_in, v_pool_w, v_pool_scale, v_gdn_conv, v_gdn_a_log, v_gdn_dt_bias, v_gdn_norm, v_w_pool_up, v_w_sb_up, v_w_gdn_up, v_w_out, v_mlp_norm, v_w_ff1, v_w_ff2, v_final_norm):
    s = x.shape[1]
    me = _dev_index((lax.axis_index("x"), lax.axis_index("y"), lax.axis_index("c")))
    ncv = gdn_conv.shape[2]

    w_in_t, m_w_in_t, v_w_in_t = (jnp.transpose(a, (2, 0, 1)) for a in (w_in, m_w_in, v_w_in))
    full = []
    for l in range(NL):
        packed_in = _pack_rows(w_in_t[:, l]).astype(bf16)
        in_mix, in_gates = packed_in[:OFF_G], packed_in[OFF_G:]
        shards = [w_pool_up[l].astype(bf16), w_sb_up[l].astype(bf16), w_gdn_up[l].astype(bf16), w_out[l].astype(bf16),
                  w_ff1[l].astype(bf16), w_ff2[l].astype(bf16)]
        if l == 0:
            first = all_gather([in_mix, gdn_conv.reshape(NL * 4, ncv)], (1, 0), "gather_first", sequencer_id=GATHER_ID)
            gates = all_gather([in_gates], (1,), "gather_gates_l0", sequencer_id=GATHER_ID)
            full.append([(first[0], gates[0])] + all_gather(shards, BIG_AXES[1:], "gather_rest_l0", sequencer_id=GATHER_ID))
            conv_full = first[1].reshape(NDEV, NL, 4, ncv).transpose(1, 2, 0, 3).reshape(NL, 4, NDEV * ncv)
        else:
            first = all_gather([in_mix, in_gates], (1, 1), f"gather_in_l{l}", sequencer_id=GATHER_ID)
            full.append([tuple(first)] + all_gather(shards, BIG_AXES[1:], f"gather_rest_l{l}", sequencer_id=GATHER_ID))
    params = []
    for l in range(NL):
        p = dict(zip(("w_in", "w_pool_up", "w_sb_up", "w_gdn_up", "w_out", "w_ff1", "w_ff2"), full[l][:7]))
        p.update(attn_norm=attn_norm[l][None], mlp_norm=mlp_norm[l][None], pool_w=pool_w[l], pool_scale=pool_scale[l][None],
                 conv=conv_full[l], alog=_lanes(gdn_a_log[l]), dtb=_lanes(gdn_dt_bias[l]), gdn_norm=gdn_norm[l][None])
        params.append(p)

    h = x[0]
    saved = []
    for l in range(NL):
        h, sv = _layer_fwd(h, params[l], l)
        saved.append(sv)
    dh, dhb, loss_row, d_final = loss_head(h, loss_target[0], final_norm[None], "loss_head")
    recv, smalls = [None] * NL, [None] * NL
    for l in reversed(range(NL)):
        def swap(gs, axes, tag, l=l):
            return pair_swap(gs, axes, f"swap_{tag}_l{l}", None)

        def finish(gs, ps, axes, names, tag, after, l=l):
            qs = [pair_add(g, p, ax, f"pair_add_{k}_l{l}", after) for g, p, ax, k in zip(gs, ps, axes, names)]
            return chip_exchange(qs, f"exchange_{tag}_l{l}", EXCHANGE_ID), tuple(qs)

        dh, dhb, recv[l], smalls[l] = _layer_bwd(dh, dhb, saved[l], params[l], l, swap, finish)

    small_rows = [_lanes8(t) for l in range(NL) for t in smalls[l]] + [_lanes8(d_final), _lanes8(loss_row)]
    packed = jnp.concatenate(small_rows, axis=0)
    parts = all_gather([packed], (0,), "gather_small", sequencer_id=GATHER_ID)[0].reshape(NDEV, packed.shape[0], HD)

    def pack_small(tree):
        rows = []
        for l in range(NL):
            rows += [_lanes8(tree["attn_norm"][l]), _lanes8(tree["pool_w"][l]), _lanes8(tree["pool_scale"][l]),
                     jnp.zeros((4 * NDEV * ncv // HD, HD), f32), _lanes8(tree["gdn_a_log"][l]), _lanes8(tree["gdn_dt_bias"][l]),
                     _lanes8(tree["gdn_norm"][l]), _lanes8(tree["mlp_norm"][l])]
        rows += [_lanes8(tree["final_norm"]), jnp.zeros((8, HD), f32)]
        return jnp.concatenate(rows, axis=0)

    names = ("attn_norm", "pool_w", "pool_scale", "gdn_a_log", "gdn_dt_bias", "gdn_norm", "mlp_norm", "final_norm")
    w_small = pack_small(dict(zip(names, (attn_norm, pool_w, pool_scale, gdn_a_log, gdn_dt_bias, gdn_norm, mlp_norm, final_norm))))
    m_small = pack_small(dict(zip(names, (m_attn_norm, m_pool_w, m_pool_scale, m_gdn_a_log, m_gdn_dt_bias, m_gdn_norm, m_mlp_norm, m_final_norm))))
    v_small = pack_small(dict(zip(names, (v_attn_norm, v_pool_w, v_pool_scale, v_gdn_a_log, v_gdn_dt_bias, v_gdn_norm, v_mlp_norm, v_final_norm))))
    def unpack_small(buf):
        out, conv_g, r = {}, [], 0
        layer_items = (("attn_norm", (D,)), ("pool_w", (4, HD, HD)), ("pool_scale", (W_POOL,)), ("conv", (4, NDEV * ncv)),
                       ("gdn_a_log", (NH,)), ("gdn_dt_bias", (NH,)), ("gdn_norm", (HD,)), ("mlp_norm", (D,)))
        per_layer = {k: [] for k, _ in layer_items}
        for l in range(NL):
            for k, shape in layer_items:
                size = math.prod(shape)
                nrow = -(-size // (8 * HD)) * 8
                per_layer[k].append(buf[r:r + nrow].reshape(-1)[:size].reshape(shape))
                r += nrow
        for k, _ in layer_items:
            out[k] = jnp.stack(per_layer[k])
        out["final_norm"] = buf[r:r + D // HD].reshape(D)
        out["loss"] = buf[r + D // HD, 0]
        return out

    big_out = {}
    big_names = BIG_NAMES
    big_w = dict(zip(big_names, ((w_in_t, m_w_in_t, v_w_in_t), (w_pool_up, m_w_pool_up, v_w_pool_up), (w_sb_up, m_w_sb_up, v_w_sb_up),
                                 (w_gdn_up, m_w_gdn_up, v_w_gdn_up), (w_out, m_w_out, v_w_out), (w_ff1, m_w_ff1, v_w_ff1),
                                 (w_ff2, m_w_ff2, v_w_ff2))))
    tokens = []
    for l, k in [(l, k) for l in reversed(range(NL)) for k in big_names[1:]]:
        w, m, v = big_w[k]
        big_out[k], token = adamw(w, m, v, l, f"adamw_{k}_l{l}", r=recv[l][big_names.index(k) + 1], prev=big_out.get(k))
        tokens.append(token)
    g_in = []
    for l in range(NL):
        halves = [sum_partials(recv[l][h], f"sum_w_in_{h}_l{l}", tuple(tokens) if l == 0 else ()) for h in range(2)]
        g_in.append(_unpack_rows(jnp.concatenate(halves, axis=0)))
    big_out["w_in"] = adamw_t(*big_w["w_in"], g_in, "adamw_w_in")

    small_out = small_adamw(parts, w_small, m_small, v_small, "adamw_small", after=(big_out["w_in"][1],))
    sm = [unpack_small(b) for b in small_out]
    loss = sm[0]["loss"]
    g_conv = lax.dynamic_slice_in_dim(sm[0]["conv"], me * ncv, ncv, axis=2)
    conv_out = None
    for l in reversed(range(NL)):
        conv_out, _ = adamw(gdn_conv, m_gdn_conv, v_gdn_conv, l, f"adamw_conv_l{l}", g=g_conv[l], prev=conv_out)

    def leaf(i, k):
        if k == "w_in":
            return jnp.transpose(big_out[k][i], (1, 2, 0))
        if k in big_out:
            return big_out[k][i]
        if k == "gdn_conv":
            return conv_out[i]
        return sm[i][k]

    order = ("attn_norm", "w_in", "pool_w", "pool_scale", "gdn_conv", "gdn_a_log", "gdn_dt_bias", "gdn_norm", "w_pool_up",
             "w_sb_up", "w_gdn_up", "w_out", "mlp_norm", "w_ff1", "w_ff2", "final_norm")
    return (loss, dh[None]) + tuple(leaf(i, k) for i in range(4) for k in order)
```

```python
import functools
import math

import jax
import jax.numpy as jnp
from jax import lax
from jax.experimental import pallas as pl
from jax.experimental.pallas import tpu as pltpu
from jax.experimental.pallas import tpu_sc as plsc

f32, bf16 = jnp.float32, jnp.bfloat16

D = 2048
NDEV = 8
NL = 2
HD = 128
NH = 6
WH = NH * HD
W_POOL = 512
EPS = 1e-6
N_IN = 12044
NP = 12288
OFF_SB, OFF_GQ, OFF_Z, OFF_P, OFF_AB, OFF_G = 0, 2304, 4608, 5376, 6016, 6144
AB_LANE = HD - 2 * NH
POOL_WINDOWS = (2, 4, 8, 16)
CH = 128
TQ = 256
VMEM_LIMIT = 56 * 1024 * 1024
ADAM_LR, ADAM_B1, ADAM_B2, ADAM_EPS, ADAM_WD, ADAM_STEP = 0.001, 0.9, 0.999, 1e-08, 0.01, 10
MESH = pl.DeviceIdType.MESH


def _cp(sem=None):
    return pltpu.CompilerParams(dimension_semantics=sem, vmem_limit_bytes=VMEM_LIMIT)


def _sds(shape, dtype):
    return jax.ShapeDtypeStruct(tuple(shape), dtype)


def _swap_copies(g_refs, p_refs, send, recv, axes):
    x, y, core = lax.axis_index("x"), lax.axis_index("y"), lax.axis_index("c")
    return [pltpu.make_async_remote_copy(
        src_ref=_shard_of(g, ax, p.shape[1 + ax], 2 * j + (1 - core)), dst_ref=p.at[j], send_sem=send.at[t, j],
        recv_sem=recv.at[t, j], device_id=(x, y, 1 - core), device_id_type=MESH)
        for t, (g, p, ax) in enumerate(zip(g_refs, p_refs, axes)) for j in range(4)]


def matmul(a, b, *, name, ta=False, tb=False, out_dtype=f32, tm=1024, tn=1024, tk=2048, epilogue=None, extras=(), after=(),
           swap=()):
    m, k = (a.shape[1], a.shape[0]) if ta else a.shape
    n = b.shape[0] if tb else b.shape[1]
    assert k == (b.shape[1] if tb else b.shape[0]) and a.dtype == bf16 and b.dtype == bf16
    tm, tn, tk = min(tm, m), min(tn, n), min(tk, k)
    assert m % tm == 0 and n % tn == 0 and k % tk == 0, (m, n, k, tm, tn, tk)
    nk = k // tk
    a_spec = pl.BlockSpec((tk, tm), lambda i, j, q: (q, i)) if ta else pl.BlockSpec((tm, tk), lambda i, j, q: (i, q))
    b_spec = pl.BlockSpec((tn, tk), lambda i, j, q: (j, q)) if tb else pl.BlockSpec((tk, tn), lambda i, j, q: (q, j))
    e_specs = [pl.BlockSpec((tm, tn), lambda i, j, q: (i, j)) for _ in extras]
    dn = (((0 if ta else 1,), (1 if tb else 0,)), ((), ()))
    ne = len(extras)
    gs, axes = swap if swap else ((), ())
    ns = len(gs)
    shards = [tuple(d // (NDEV if ax_ == ax else 1) for ax_, d in enumerate(g.shape)) for g, ax in zip(gs, axes)]
    grid = (m // tm, n // tn, nk)

    def body(*refs):
        g_refs = refs[len(after):len(after) + ns]
        a_ref, b_ref, *rest = refs[len(after) + ns:]
        e_refs, o_ref, p_refs = rest[:ne], rest[ne], rest[ne + 1:ne + 1 + ns]
        if ns:
            at = [pl.program_id(d) for d in range(3)]
            first = functools.reduce(jnp.logical_and, [c == 0 for c in at])
            last = functools.reduce(jnp.logical_and, [c == g - 1 for c, g in zip(at, grid)])

            @pl.when(first)
            def _():
                for cp in _swap_copies(g_refs, p_refs, *rest[-2:], axes):
                    cp.start()

        part = lax.dot_general(a_ref[...], b_ref[...], dn, preferred_element_type=f32)

        def finish(acc):
            if epilogue is not None:
                acc = epilogue(acc, *[e[...] for e in e_refs])
            o_ref[...] = acc.astype(out_dtype)

        if nk == 1:
            finish(part)
        else:
            acc_ref = rest[ne + 1 + ns]
            q = pl.program_id(2)

            @pl.when(q == 0)
            def _():
                acc_ref[...] = part

            @pl.when(jnp.logical_and(q > 0, q < nk - 1))
            def _():
                acc_ref[...] += part

            @pl.when(q == nk - 1)
            def _():
                finish(acc_ref[...] + part)

        if ns:
            @pl.when(last)
            def _():
                cps = _swap_copies(g_refs, p_refs, *rest[-2:], axes)
                for cp in cps:
                    cp.wait_recv()
                for cp in cps:
                    cp.wait_send()

    hbm = pl.BlockSpec(memory_space=pl.ANY)
    res = pl.pallas_call(
        body, name=name, grid=grid,
        in_specs=[hbm] * (len(after) + ns) + [a_spec, b_spec] + e_specs,
        out_specs=[pl.BlockSpec((tm, tn), lambda i, j, q: (i, j))] + [hbm] * ns,
        out_shape=[_sds((m, n), out_dtype)] + [_sds((4,) + sh, g.dtype) for sh, g in zip(shards, gs)],
        scratch_shapes=([pltpu.VMEM((tm, tn), f32)] if nk > 1 else [])
        + ([pltpu.SemaphoreType.DMA((ns, 4)), pltpu.SemaphoreType.DMA((ns, 4))] if ns else []),
        compiler_params=_cp(("arbitrary",) * 3 if ns else ("parallel", "parallel", "arbitrary")),
    )(*after, *gs, a, b, *extras)
    return (res[0], list(res[1:])) if ns else res[0]


def rowwise(name, fn, rows, params, outs, sums=(), tr=256, after=()):
    s = rows[0][0].shape[0]
    tr = min(tr, s)
    nin, nout = len(rows) + len(params), len(outs)
    in_specs = [pl.BlockSpec((tr, w), functools.partial(lambda i, c: (i, c), c=c)) for (_, w, c) in rows]
    in_specs += [pl.BlockSpec(p.shape, lambda i: (0, 0)) for p in params]
    out_specs = [pl.BlockSpec((tr, w), lambda i: (i, 0)) for (w, _) in outs]
    out_specs += [pl.BlockSpec(sh, lambda i: (0, 0)) for sh in sums]
    out_shape = [_sds((s, w), dt) for (w, dt) in outs] + [_sds(sh, f32) for sh in sums]

    def body(*refs):
        refs = refs[len(after):]
        res = fn(*[r[...] for r in refs[:nin]])
        for r, v in zip(refs[nin:nin + nout], res[:nout]):
            r[...] = v.astype(r.dtype)
        i = pl.program_id(0)
        for r, v in zip(refs[nin + nout:], res[nout:]):
            @pl.when(i == 0)
            def _(r=r, v=v):
                r[...] = v

            @pl.when(i > 0)
            def _(r=r, v=v):
                r[...] += v

    res = pl.pallas_call(
        body, name=name, grid=(s // tr,), in_specs=[pl.BlockSpec(memory_space=pl.ANY)] * len(after) + in_specs,
        out_specs=out_specs, out_shape=out_shape, compiler_params=_cp(("arbitrary",)),
    )(*after, *[r[0] for r in rows], *params)
    return res


def _rms(x, g):
    return x * lax.rsqrt(jnp.mean(x * x, axis=-1, keepdims=True) + EPS) * g


def rms_fwd(x, g, name):
    return rowwise(name, lambda xb, gb: (_rms(xb, gb),), [(x, D, 0)], [g], [(D, bf16)])[0]


def rms_bwd(x, du, dres, g, name, after=()):
    def fn(xb, dub, drb, gb):
        _, vjp = jax.vjp(_rms, xb, gb)
        dx, dg = vjp(dub.astype(f32))
        return drb + dx, drb + dx, dg

    return rowwise(name, fn, [(x, D, 0), (du, D, 0), (dres, D, 0)], [g], [(D, f32), (D, bf16)], [(1, D)], after=after)


def _merge(gates, up_p, up_s, up_g):
    sg = jax.nn.sigmoid(gates)
    return sg[:, :D] * up_p + sg[:, D:2 * D] * up_s + sg[:, 2 * D:] * up_g


def merge_fwd(gates, ups, name):
    return rowwise(name, lambda g, a, b, c: (_merge(g, a, b, c),),
                   [(gates, 3 * D, 0)] + [(u, D, 0) for u in ups], [], [(D, bf16)], tr=128)[0]


def merge_bwd(gates, ups, dmerged, name):
    def fn(g, a, b, c, dm):
        _, vjp = jax.vjp(_merge, g, a, b, c)
        return vjp(dm.astype(f32))

    return rowwise(name, fn, [(gates, 3 * D, 0)] + [(u, D, 0) for u in ups] + [(dmerged, D, 0)], [],
                   [(3 * D, bf16), (D, bf16), (D, bf16), (D, bf16)], tr=128)


def _gdn_out(o, z, g):
    ys = []
    for h in range(NH):
        sl = slice(h * HD, (h + 1) * HD)
        ys.append(_rms(o[:, sl], g) * jax.nn.silu(z[:, sl]))
    return jnp.concatenate(ys, axis=1)


def gdn_out_fwd(o, proj, g, name):
    return rowwise(name, lambda ob, zb, gb: (_gdn_out(ob, zb, gb),), [(o, WH, 0), (proj, WH, OFF_Z // WH)], [g],
                   [(WH, bf16)])[0]


def gdn_out_bwd(o, proj, g, dy, name):
    def fn(ob, zb, dyb, gb):
        _, vjp = jax.vjp(_gdn_out, ob, zb, gb)
        return vjp(dyb.astype(f32))

    return rowwise(name, fn, [(o, WH, 0), (proj, WH, OFF_Z // WH), (dy, WH, 0)], [g], [(WH, f32), (WH, bf16)],
                   [(1, HD)])


def loss_head(x, target, g, name):
    def loss_fn(xb, gb, tb):
        err = _rms(xb, gb) - tb
        return (0.5 / D) * jnp.sum(jnp.sum(err * err, axis=1, keepdims=True), axis=0, keepdims=True)

    def fn(xb, tb, gb):
        val, vjp = jax.vjp(functools.partial(loss_fn, tb=tb), xb, gb)
        dx, dg = vjp(jnp.ones((1, 1), f32))
        return dx, dx, jnp.broadcast_to(val, (1, HD)), dg

    return rowwise(name, fn, [(x, D, 0), (target, D, 0)], [g], [(D, f32), (D, bf16)], [(1, HD), (1, D)])


PB = 256


def _split(v):
    hi = v.astype(bf16)
    return hi, (v - hi.astype(f32)).astype(bf16)


def _band_dot(make_band, v, s, forward):
    hi, lo = _split(v)
    nb = s // PB
    outs = []
    for r in range(nb):
        lo_r = max(r - 1, 0) if forward else r
        hi_r = r + 1 if forward else min(r + 2, nb)
        band = make_band(r * PB, lo_r * PB, (hi_r - lo_r) * PB)
        sl = slice(lo_r * PB, hi_r * PB)
        outs.append(jnp.dot(band, hi[sl], preferred_element_type=f32) + jnp.dot(band, lo[sl], preferred_element_type=f32))
    return jnp.concatenate(outs, axis=0)


def _pool_common(p, win, s):
    def band(row0, col0, ncol):
        t = row0 + lax.broadcasted_iota(jnp.int32, (PB, ncol), 0)
        u = col0 + lax.broadcasted_iota(jnp.int32, (PB, ncol), 1)
        return jnp.logical_and(u <= t, t < u + win).astype(bf16)

    def band_t(row0, col0, ncol):
        u = row0 + lax.broadcasted_iota(jnp.int32, (PB, ncol), 0)
        t = col0 + lax.broadcasted_iota(jnp.int32, (PB, ncol), 1)
        return jnp.logical_and(u <= t, t < u + win).astype(bf16)

    t = lax.broadcasted_iota(jnp.int32, (s, 1), 0)
    inv_n = 1.0 / jnp.minimum(t + 1, win).astype(f32)
    d = _band_dot(band, p, s, True) * inv_n - p
    return d, inv_n, band_t


def pool_fwd(proj, pool_w, pool_scale, name):
    s = proj.shape[0]

    def body(p_ref, w_ref, sc_ref, y_ref):
        win = jnp.left_shift(2, pl.program_id(0))
        d, _, _ = _pool_common(p_ref[...], win, s)
        y = jnp.dot(d.astype(bf16), w_ref[...].astype(bf16), preferred_element_type=f32) * sc_ref[...]
        y_ref[...] = y.astype(bf16)

    return pl.pallas_call(
        body, name=name, grid=(4,),
        in_specs=[pl.BlockSpec((s, HD), lambda g: (0, OFF_P // HD + g)), pl.BlockSpec((None, HD, HD), lambda g: (g, 0, 0)),
                  pl.BlockSpec((1, HD), lambda g: (0, g))],
        out_specs=pl.BlockSpec((s, HD), lambda g: (0, g)), out_shape=_sds((s, W_POOL), bf16),
        compiler_params=_cp(("arbitrary",)),
    )(proj, pool_w, pool_scale)


def pool_bwd(proj, pool_w, pool_scale, dy, name):
    s = proj.shape[0]

    def body(p_ref, w_ref, sc_ref, dy_ref, dp_ref, dw_ref, dsc_ref):
        win = jnp.left_shift(2, pl.program_id(0))
        d, inv_n, band_t = _pool_common(p_ref[...], win, s)
        w = w_ref[...].astype(bf16)
        dyf = dy_ref[...].astype(f32)
        dsc_ref[...] = jnp.sum(dyf * jnp.dot(d.astype(bf16), w, preferred_element_type=f32), axis=0, keepdims=True)
        dys = (dyf * sc_ref[...]).astype(bf16)
        dd = lax.dot_general(dys, w, (((1,), (1,)), ((), ())), preferred_element_type=f32)
        dw_ref[...] = lax.dot_general(d.astype(bf16), dys, (((0,), (0,)), ((), ())), preferred_element_type=f32)
        dp_ref[...] = (_band_dot(band_t, dd * inv_n, s, False) - dd).astype(bf16)

    return pl.pallas_call(
        body, name=name, grid=(4,),
        in_specs=[pl.BlockSpec((s, HD), lambda g: (0, OFF_P // HD + g)), pl.BlockSpec((None, HD, HD), lambda g: (g, 0, 0)),
                  pl.BlockSpec((1, HD), lambda g: (0, g)), pl.BlockSpec((s, HD), lambda g: (0, g))],
        out_specs=[pl.BlockSpec((s, HD), lambda g: (0, g)), pl.BlockSpec((None, HD, HD), lambda g: (g, 0, 0)),
                   pl.BlockSpec((1, HD), lambda g: (0, g))],
        out_shape=[_sds((s, W_POOL), bf16), _sds((4, HD, HD), f32), _sds((1, W_POOL), f32)],
        compiler_params=_cp(("arbitrary",)),
    )(proj, pool_w, pool_scale, dy)


HB = 3
WB = HB * HD


def _heads_of(v):
    return _stack([v[:, h * HD:(h + 1) * HD] for h in range(HB)])


def _bd(a, b, dn):
    return lax.dot_general(a, b, dn, preferred_element_type=f32)


def _run_sum(v, tri):
    hi, lo = _split(v.reshape(HB * TQ, TQ))
    return (jnp.dot(hi, tri, preferred_element_type=f32) + jnp.dot(lo, tri, preferred_element_type=f32)).reshape(HB, TQ, TQ)


def _tri(later):
    j = lax.broadcasted_iota(jnp.int32, (TQ, TQ), 0)
    u = lax.broadcasted_iota(jnp.int32, (TQ, TQ), 1)
    return (j > u if later else j < u).astype(bf16)


def _sb_tile(q, k_ref, kb, carry, diagonal):
    k = _heads_of(k_ref[pl.ds(pl.multiple_of(kb * TQ, TQ), TQ), :].astype(bf16))
    z = _bd(q, k, BNT)
    ls = jax.nn.log_sigmoid(-z)
    if diagonal:
        mask = lax.broadcasted_iota(jnp.int32, (TQ, TQ), 1) < lax.broadcasted_iota(jnp.int32, (TQ, TQ), 0)
        ls = jnp.where(mask, ls, 0.0)
    a = jnp.exp(ls + z + _run_sum(ls, _tri(True)) + carry)
    return z, ls, jnp.where(mask, a, 0.0) if diagonal else a


def sb_fwd(proj, name):
    s = proj.shape[0]
    nq = s // TQ
    scale = HD ** -0.5

    def body(q_ref, k_ref, v_ref, y_ref):
        qi = pl.program_id(1)
        q = _heads_of((q_ref[...] * scale).astype(bf16))

        def step(j, c, diagonal=False):
            acc, carry = c
            kb = qi - j
            _, ls, a = _sb_tile(q, k_ref, kb, carry, diagonal)
            v = _heads_of(v_ref[pl.ds(pl.multiple_of(kb * TQ, TQ), TQ), :].astype(bf16))
            return acc + _bd(a.astype(bf16), v, BNN), carry + jnp.sum(ls, axis=2, keepdims=True)

        first = step(0, (jnp.zeros((HB, TQ, HD), f32), jnp.zeros((HB, TQ, 1), f32)), True)
        acc, _ = lax.fori_loop(1, qi + 1, step, first)
        for h in range(HB):
            y_ref[:, h * HD:(h + 1) * HD] = acc[h].astype(bf16)

    c0, ng = OFF_SB // WB, NH // HB
    return pl.pallas_call(
        body, name=name, grid=(ng, nq),
        in_specs=[pl.BlockSpec((TQ, WB), lambda g, i: (i, c0 + g)), pl.BlockSpec((s, WB), lambda g, i: (0, c0 + ng + g)),
                  pl.BlockSpec((s, WB), lambda g, i: (0, c0 + 2 * ng + g))],
        out_specs=pl.BlockSpec((TQ, WB), lambda g, i: (i, g)), out_shape=_sds((s, WH), bf16),
        compiler_params=_cp(("arbitrary", "arbitrary")),
    )(proj, proj, proj)


def sb_bwd(proj, dy, name):
    s = proj.shape[0]
    nq = s // TQ
    scale = HD ** -0.5

    def body(q_ref, k_ref, v_ref, do_ref, dq_ref, dk_ref, dv_ref, e_scr, z_scr, dk_acc, dv_acc):
        qi = pl.program_id(1)
        q = _heads_of((q_ref[...] * scale).astype(bf16))
        do = _heads_of(do_ref[...])

        @pl.when(qi == 0)
        def _():
            dk_acc[...] = jnp.zeros_like(dk_acc)
            dv_acc[...] = jnp.zeros_like(dv_acc)

        def add_heads(acc_ref, rows, upd):
            for h in range(HB):
                acc_ref[rows, h * HD:(h + 1) * HD] += upd[h]

        def sweep_back(j, carry, diagonal=False):
            kb = qi - j
            rows = pl.ds(pl.multiple_of(kb * TQ, TQ), TQ)
            z, ls, a = _sb_tile(q, k_ref, kb, carry, diagonal)
            v = _heads_of(v_ref[rows, :].astype(bf16))
            e_scr[kb] = _bd(do, v, BNT) * a
            z_scr[kb] = z
            add_heads(dv_acc, rows, _bd(a.astype(bf16), do, BTN))
            return carry + jnp.sum(ls, axis=2, keepdims=True)

        lax.fori_loop(1, qi + 1, sweep_back, sweep_back(0, jnp.zeros((HB, TQ, 1), f32), True))

        def sweep_fwd(kb, c, diagonal=False):
            dq, carry = c
            rows = pl.ds(pl.multiple_of(kb * TQ, TQ), TQ)
            e, z = e_scr[kb], z_scr[kb]
            sig = jax.nn.sigmoid(z)
            dz = e * (1.0 - sig) - (_run_sum(e, _tri(False)) + carry) * sig
            if diagonal:
                dz = jnp.where(lax.broadcasted_iota(jnp.int32, (TQ, TQ), 1) < lax.broadcasted_iota(jnp.int32, (TQ, TQ), 0), dz, 0.0)
            dz = dz.astype(bf16)
            k = _heads_of(k_ref[rows, :].astype(bf16))
            add_heads(dk_acc, rows, _bd(dz, q, BTN))
            return dq + _bd(dz, k, BNN), carry + jnp.sum(e, axis=2, keepdims=True)

        dq, _ = sweep_fwd(qi, lax.fori_loop(0, qi, sweep_fwd, (jnp.zeros((HB, TQ, HD), f32), jnp.zeros((HB, TQ, 1), f32))), True)
        for h in range(HB):
            dq_ref[:, h * HD:(h + 1) * HD] = (dq[h] * scale).astype(bf16)

        @pl.when(qi == nq - 1)
        def _():
            dk_ref[...] = dk_acc[...].astype(bf16)
            dv_ref[...] = dv_acc[...].astype(bf16)

    c0, ng = OFF_SB // WB, NH // HB
    return pl.pallas_call(
        body, name=name, grid=(ng, nq),
        in_specs=[pl.BlockSpec((TQ, WB), lambda g, i: (i, c0 + g)), pl.BlockSpec((s, WB), lambda g, i: (0, c0 + ng + g)),
                  pl.BlockSpec((s, WB), lambda g, i: (0, c0 + 2 * ng + g)), pl.BlockSpec((TQ, WB), lambda g, i: (i, g))],
        out_specs=[pl.BlockSpec((TQ, WB), lambda g, i: (i, g)), pl.BlockSpec((s, WB), lambda g, i: (0, g)),
                   pl.BlockSpec((s, WB), lambda g, i: (0, g))],
        out_shape=[_sds((s, WH), bf16)] * 3,
        scratch_shapes=[pltpu.VMEM((nq, HB, TQ, TQ), f32), pltpu.VMEM((nq, HB, TQ, TQ), f32), pltpu.VMEM((s, WB), f32),
                        pltpu.VMEM((s, WB), f32)],
        compiler_params=_cp(("arbitrary", "arbitrary")),
    )(proj, proj, proj, dy)


CB = 256


def _shift_down(v, k, s):
    if k == 0:
        return v
    row = lax.broadcasted_iota(jnp.int32, v.shape, 0)
    return jnp.where(row < k, 0.0, pltpu.roll(v, k, axis=0))


def _shift_up(v, k, s):
    if k == 0:
        return v
    row = lax.broadcasted_iota(jnp.int32, v.shape, 0)
    return jnp.where(row >= s - k, 0.0, pltpu.roll(v, s - k, axis=0))


def conv_fwd(proj, w, name):
    s = proj.shape[0]

    def body(x_ref, w_ref, y_ref):
        x, wv = x_ref[...], w_ref[...]
        y = sum(wv[3 - k:4 - k, :] * _shift_down(x, k, s) for k in range(4))
        y_ref[...] = jax.nn.silu(y)

    return pl.pallas_call(
        body, name=name, grid=(3 * WH // CB,),
        in_specs=[pl.BlockSpec((s, CB), lambda j: (0, OFF_GQ // CB + j)), pl.BlockSpec((4, CB), lambda j: (0, j))],
        out_specs=pl.BlockSpec((s, CB), lambda j: (0, j)), out_shape=_sds((s, 3 * WH), f32),
        compiler_params=_cp(("parallel",)),
    )(proj, w)


def conv_bwd(proj, w, dc, name, after=()):
    s = proj.shape[0]

    def body(*refs):
        x_ref, w_ref, dc_ref, dx_ref, dw_ref = refs[len(after):]
        x, wv = x_ref[...], w_ref[...]
        xs = [_shift_down(x, k, s) for k in range(4)]
        y = sum(wv[3 - k:4 - k, :] * xs[k] for k in range(4))
        sig = jax.nn.sigmoid(y)
        dy = dc_ref[...] * (sig * (1.0 + y * (1.0 - sig)))
        dx_ref[...] = sum(wv[3 - k:4 - k, :] * _shift_up(dy, k, s) for k in range(4)).astype(bf16)
        dw_ref[...] = jnp.concatenate([jnp.sum(dy * xs[3 - i], axis=0, keepdims=True) for i in range(4)], axis=0)

    return pl.pallas_call(
        body, name=name, grid=(3 * WH // CB,),
        in_specs=[pl.BlockSpec(memory_space=pl.ANY)] * len(after)
        + [pl.BlockSpec((s, CB), lambda j: (0, OFF_GQ // CB + j)), pl.BlockSpec((4, CB), lambda j: (0, j)),
           pl.BlockSpec((s, CB), lambda j: (0, j))],
        out_specs=[pl.BlockSpec((s, CB), lambda j: (0, j)), pl.BlockSpec((4, CB), lambda j: (0, j))],
        out_shape=[_sds((s, 3 * WH), bf16), _sds((4, 3 * WH), f32)],
        compiler_params=_cp(("parallel",)),
    )(*after, proj, w, dc)


SOLVE_PASSES = 3


def _pdot_impl(a, b, dn, passes):
    ah, al = _split(a)
    bh, bl = _split(b)
    dot = lambda p, q: lax.dot_general(p, q, dn, preferred_element_type=f32)
    if passes == 1:
        return dot(ah, bh)
    if passes == 2:
        return dot(ah, bh) + dot(ah, bl)
    return dot(ah, bh) + (dot(ah, bl) + dot(al, bh))


BNN, BNT, BTN = (((2,), (1,)), ((0,), (0,))), (((2,), (2,)), ((0,), (0,))), (((1,), (1,)), ((0,), (0,)))


@functools.partial(jax.custom_vjp, nondiff_argnums=(2,))
def _bdot(a, b, passes):
    return _pdot_impl(a, b, BNN, passes)


def _bdot_fwd(a, b, passes):
    return _pdot_impl(a, b, BNN, passes), (a, b)


def _bdot_bwd(passes, res, ct):
    a, b = res
    return _pdot_impl(ct, b, BNT, passes), _pdot_impl(a, ct, BTN, passes)


_bdot.defvjp(_bdot_fwd, _bdot_bwd)


@functools.partial(jax.custom_vjp, nondiff_argnums=(2,))
def _bdot_nt(a, b, passes):
    return _pdot_impl(a, b, BNT, passes)


def _bdot_nt_fwd(a, b, passes):
    return _pdot_impl(a, b, BNT, passes), (a, b)


def _bdot_nt_bwd(passes, res, ct):
    a, b = res
    return _pdot_impl(ct, b, BNN, passes), _pdot_impl(ct, a, BTN, passes)


_bdot_nt.defvjp(_bdot_nt_fwd, _bdot_nt_bwd)


def _lane_pick(v, h):
    lane = lax.broadcasted_iota(jnp.int32, v.shape, v.ndim - 1)
    return jnp.sum(jnp.where(lane == h, v, 0.0), axis=-1, keepdims=True)


def _stack(parts):
    return jnp.concatenate([p[None] for p in parts], axis=0)


def _heads(v, first):
    return _stack([_lane_pick(v, first + h) for h in range(NH)])


def _l2n(v):
    return v * lax.rsqrt(jnp.sum(v * v, axis=-1, keepdims=True) + EPS)


def _dot_nt(a, b):
    return lax.dot_general(a, b, (((1,), (1,)), ((), ())), preferred_element_type=f32)


def _inverse_impl(lower):
    i = lax.broadcasted_iota(jnp.int32, (CH, CH), 0)
    j = lax.broadcasted_iota(jnp.int32, (CH, CH), 1)
    inv = (i == j).astype(f32) - lower
    pw = _pdot_impl(lower, lower, BNN, SOLVE_PASSES)
    for m in range(1, int(math.log2(CH))):
        inv = inv + _pdot_impl(inv, pw, BNN, SOLVE_PASSES)
        if m < int(math.log2(CH)) - 1:
            pw = _pdot_impl(pw, pw, BNN, SOLVE_PASSES)
    return inv


@jax.custom_vjp
def _unit_lower_inverse(lower):
    return _inverse_impl(lower)


def _unit_lower_inverse_fwd(lower):
    inv = _inverse_impl(lower)
    return inv, inv


def _unit_lower_inverse_bwd(inv, ct):
    return (-_pdot_impl(_pdot_impl(inv, ct, BTN, SOLVE_PASSES), inv, BNT, SOLVE_PASSES),)


_unit_lower_inverse.defvjp(_unit_lower_inverse_fwd, _unit_lower_inverse_bwd)


def _gdn_chunk(cq, ck, cv, ab, alog, dtb):
    ones = jnp.ones((NH, CH, HD), f32)
    q = _l2n(cq) * (HD ** -0.5)
    k = _l2n(ck)
    la = -jnp.exp(_heads(alog, 0)) * jax.nn.softplus(_heads(ab, AB_LANE) + _heads(dtb, 0))
    beta = jax.nn.sigmoid(_heads(ab, AB_LANE + NH)) * ones
    i = lax.broadcasted_iota(jnp.int32, (CH, CH), 0)
    j = lax.broadcasted_iota(jnp.int32, (CH, CH), 1)
    incl, strict = j <= i, j < i
    g = _bdot(jnp.broadcast_to(incl.astype(f32), (NH, CH, CH)), la * ones, 2)
    g_row = _stack([g[h].T for h in range(NH)])
    gamma = jnp.where(incl, jnp.exp(jnp.where(incl, g - g_row, 0.0)), 0.0)
    lower = jnp.where(strict, beta * _bdot_nt(k, k, 1) * gamma, 0.0)
    inv = _unit_lower_inverse(lower)
    eg = jnp.exp(g)
    u = _bdot(inv, cv * beta, SOLVE_PASSES)
    w = _bdot(inv, k * (beta * eg), SOLVE_PASSES)
    qk = _bdot_nt(q, k, 1) * gamma
    g_last = g[:, CH - 1:CH, :]
    return u, w, q * eg, k * jnp.exp(g_last - g), qk, jnp.exp(g_last)


def _by_head(ref, t=0):
    return _stack([ref[:, t * WH + h * HD:t * WH + (h + 1) * HD] for h in range(NH)])


def gdn_a_fwd(c, proj, alog, dtb, name):
    s = c.shape[0]
    nc = s // CH

    def body(c_ref, ab_ref, al_ref, dt_ref, u_ref, w_ref, qd_ref, kd_ref, qk_ref, dec_ref):
        res = _gdn_chunk(_by_head(c_ref, 0), _by_head(c_ref, 1), _by_head(c_ref, 2), ab_ref[...], al_ref[...], dt_ref[...])
        for h in range(NH):
            sl = slice(h * HD, (h + 1) * HD)
            for r, v in zip((u_ref, w_ref, qd_ref, kd_ref, qk_ref), res[:5]):
                r[:, sl] = v[h]
            dec_ref[:, sl] = jnp.broadcast_to(res[5][h], (8, HD))

    row = pl.BlockSpec((CH, WH), lambda n: (n, 0))
    par = pl.BlockSpec((1, HD), lambda n: (0, 0))
    return pl.pallas_call(
        body, name=name, grid=(nc,),
        in_specs=[pl.BlockSpec((CH, 3 * WH), lambda n: (n, 0)), pl.BlockSpec((CH, HD), lambda n: (n, OFF_AB // HD)), par, par],
        out_specs=[row] * 5 + [pl.BlockSpec((None, 8, WH), lambda n: (n, 0, 0))],
        out_shape=[_sds((s, WH), f32)] * 5 + [_sds((nc, 8, WH), f32)],
        compiler_params=_cp(("parallel",)),
    )(c, proj, alog, dtb)


def gdn_a_bwd(c, proj, alog, dtb, cots, ddec, name, swap=()):
    s = c.shape[0]
    nc = s // CH
    gs, axes = swap if swap else ((), ())
    ns = len(gs)
    shards = [tuple(d // (NDEV if a == ax else 1) for a, d in enumerate(g.shape)) for g, ax in zip(gs, axes)]

    def body(c_ref, ab_ref, al_ref, dt_ref, du_ref, dw_ref, dqd_ref, dkd_ref, dqk_ref, ddec_ref, *rest):
        g_refs, (dc_ref, dab_ref, dal_ref, ddt_ref), p_refs = rest[:ns], rest[ns:ns + 4], rest[ns + 4:2 * ns + 4]
        n = pl.program_id(0)

        def copies():
            return _swap_copies(g_refs, p_refs, *rest[2 * ns + 4:], axes)

        if ns:
            @pl.when(n == 0)
            def _():
                for cp in copies():
                    cp.start()

        _, vjp = jax.vjp(_gdn_chunk, _by_head(c_ref, 0), _by_head(c_ref, 1), _by_head(c_ref, 2), ab_ref[...], al_ref[...],
                         dt_ref[...])
        lane = lax.broadcasted_iota(jnp.int32, (1, HD), 1)
        dd = _stack([jnp.where(lane == 0, ddec_ref[0:1, h * HD:(h + 1) * HD], 0.0) for h in range(NH)])
        dcq, dck, dcv, dab, dal, ddt = vjp(tuple(_by_head(r) for r in (du_ref, dw_ref, dqd_ref, dkd_ref, dqk_ref)) + (dd,))
        for h in range(NH):
            for t, v in enumerate((dcq, dck, dcv)):
                dc_ref[:, t * WH + h * HD:t * WH + (h + 1) * HD] = v[h]
        dab_ref[...] = dab.astype(bf16)

        @pl.when(n == 0)
        def _():
            dal_ref[...] = dal
            ddt_ref[...] = ddt

        @pl.when(n > 0)
        def _():
            dal_ref[...] += dal
            ddt_ref[...] += ddt

        if ns:
            @pl.when(n == nc - 1)
            def _():
                cps = copies()
                for cp in cps:
                    cp.wait_recv()
                for cp in cps:
                    cp.wait_send()

    row = pl.BlockSpec((CH, WH), lambda n: (n, 0))
    wide = pl.BlockSpec((CH, 3 * WH), lambda n: (n, 0))
    par = pl.BlockSpec((1, HD), lambda n: (0, 0))
    res = pl.pallas_call(
        body, name=name, grid=(nc,),
        in_specs=[wide, pl.BlockSpec((CH, HD), lambda n: (n, OFF_AB // HD)), par, par] + [row] * 5
        + [pl.BlockSpec((None, 8, WH), lambda n: (n, 0, 0))] + [ANY] * ns,
        out_specs=[wide, pl.BlockSpec((CH, HD), lambda n: (n, 0)), par, par] + [ANY] * ns,
        out_shape=[_sds((s, 3 * WH), f32), _sds((s, HD), bf16), _sds((1, HD), f32), _sds((1, HD), f32)]
        + [_sds((4,) + sh, g.dtype) for sh, g in zip(shards, gs)],
        scratch_shapes=[pltpu.SemaphoreType.DMA((ns, 4)), pltpu.SemaphoreType.DMA((ns, 4))] if ns else [],
        compiler_params=_cp(("arbitrary",)),
    )(c, proj, alog, dtb, *cots, ddec, *gs)
    return tuple(res[:4]) + (list(res[4:]),)


def gdn_b_fwd(u, w, qd, kd, qk, dec, name):
    s = u.shape[0]
    nc = s // CH

    def body(u_ref, w_ref, qd_ref, kd_ref, qk_ref, dec_ref, o_ref, st_ref, state):
        n = pl.program_id(0)

        @pl.when(n == 0)
        def _():
            state[...] = jnp.zeros_like(state)

        for h in range(NH):
            sl = slice(h * HD, (h + 1) * HD)
            st = state[sl, :]
            st_ref[sl, :] = st
            sb = st.astype(bf16)
            vn = u_ref[:, sl] - jnp.dot(w_ref[:, sl].astype(bf16), sb, preferred_element_type=f32)
            vb = vn.astype(bf16)
            o_ref[:, sl] = (jnp.dot(qd_ref[:, sl].astype(bf16), sb, preferred_element_type=f32)
                            + jnp.dot(qk_ref[:, sl].astype(bf16), vb, preferred_element_type=f32))
            state[sl, :] = st * dec_ref[0:1, sl] + lax.dot_general(
                kd_ref[:, sl].astype(bf16), vb, (((0,), (0,)), ((), ())), preferred_element_type=f32)

    row = pl.BlockSpec((CH, WH), lambda n: (n, 0))
    return pl.pallas_call(
        body, name=name, grid=(nc,),
        in_specs=[row] * 5 + [pl.BlockSpec((None, 8, WH), lambda n: (n, 0, 0))],
        out_specs=[row, pl.BlockSpec((None, WH, HD), lambda n: (n, 0, 0))],
        out_shape=[_sds((s, WH), f32), _sds((nc, WH, HD), f32)],
        scratch_shapes=[pltpu.VMEM((WH, HD), f32)],
        compiler_params=_cp(("arbitrary",)),
    )(u, w, qd, kd, qk, dec)


def gdn_b_bwd(u, w, qd, kd, qk, dec, states, do, name):
    s = u.shape[0]
    nc = s // CH

    def body(u_ref, w_ref, qd_ref, kd_ref, qk_ref, dec_ref, st_ref, do_ref,
             du_ref, dw_ref, dqd_ref, dkd_ref, dqk_ref, ddec_ref, dstate):
        n = pl.program_id(0)

        @pl.when(n == 0)
        def _():
            dstate[...] = jnp.zeros_like(dstate)

        for h in range(NH):
            sl = slice(h * HD, (h + 1) * HD)
            st, ds = st_ref[sl, :], dstate[sl, :]
            sb, dsb = st.astype(bf16), ds.astype(bf16)
            wb, qdb, kdb, qkb = (r[:, sl].astype(bf16) for r in (w_ref, qd_ref, kd_ref, qk_ref))
            dob = do_ref[:, sl].astype(bf16)
            vn = u_ref[:, sl] - jnp.dot(wb, sb, preferred_element_type=f32)
            vb = vn.astype(bf16)
            dvn = (lax.dot_general(qkb, dob, (((0,), (0,)), ((), ())), preferred_element_type=f32)
                   + jnp.dot(kdb, dsb, preferred_element_type=f32))
            dvb = dvn.astype(bf16)
            du_ref[:, sl] = dvn
            dw_ref[:, sl] = -_dot_nt(dvb, sb)
            dqd_ref[:, sl] = _dot_nt(dob, sb)
            dkd_ref[:, sl] = _dot_nt(vb, dsb)
            dqk_ref[:, sl] = _dot_nt(dob, vb)
            tot = jnp.sum(jnp.sum(ds * st, axis=1, keepdims=True), axis=0, keepdims=True)
            ddec_ref[:, sl] = jnp.broadcast_to(tot, (8, HD))
            dstate[sl, :] = (ds * dec_ref[0:1, sl]
                             + lax.dot_general(qdb, dob, (((0,), (0,)), ((), ())), preferred_element_type=f32)
                             - lax.dot_general(wb, dvb, (((0,), (0,)), ((), ())), preferred_element_type=f32))

    row = pl.BlockSpec((CH, WH), lambda n: (nc - 1 - n, 0))
    small = pl.BlockSpec((None, 8, WH), lambda n: (nc - 1 - n, 0, 0))
    return pl.pallas_call(
        body, name=name, grid=(nc,),
        in_specs=[row] * 5 + [small, pl.BlockSpec((None, WH, HD), lambda n: (nc - 1 - n, 0, 0)), row],
        out_specs=[row] * 5 + [small],
        out_shape=[_sds((s, WH), f32)] * 5 + [_sds((nc, 8, WH), f32)],
        scratch_shapes=[pltpu.VMEM((WH, HD), f32)],
        compiler_params=_cp(("arbitrary",)),
    )(u, w, qd, kd, qk, dec, states, do)


ANY = pl.BlockSpec(memory_space=pl.ANY)


def _dev_index(p):
    return 4 * p[0] + 2 * p[1] + p[2]


def _shard_of(ref, axis, size, idx):
    return ref.at[pl.ds(idx * size, size), :] if axis == 0 else ref.at[:, pl.ds(idx * size, size)]


def _peer(x, y, c, r):
    return (1 - x if r & 4 else x, 1 - y if r & 2 else y, 1 - c if r & 1 else c)


ALL_PEERS, OTHER_CHIPS = tuple(range(1, NDEV)), (4, 2, 6)


def _launch(body, ins, out_sds, name, sequencer_id, relations=ALL_PEERS, kinds=7):
    n = len(ins)
    sems = [pltpu.SemaphoreType.DMA((n, kinds)), pltpu.SemaphoreType.DMA((n, kinds)), pltpu.SemaphoreType.DMA((n,))]
    if sequencer_id is None:
        return pl.pallas_call(
            lambda *refs: body(refs[:n], refs[n:2 * n], *refs[2 * n:]), name=name, in_specs=[ANY] * n, out_specs=[ANY] * n,
            out_shape=out_sds, scratch_shapes=sems, compiler_params=pltpu.CompilerParams(has_side_effects=True),
        )(*ins)
    in_refs = [jax.new_ref(a, memory_space=pltpu.MemorySpace.HBM) for a in ins]
    out_refs = [jax.empty_ref(sd, memory_space=pltpu.MemorySpace.HBM) for sd in out_sds]

    @pl.kernel(mesh=plsc.ScalarSubcoreMesh(axis_name="sequencer", num_cores=1), name=name, scratch_types=sems,
               compiler_params=pltpu.CompilerParams(collective_id=sequencer_id))
    def launch(send, recv, loc):
        x, y, c = lax.axis_index("x"), lax.axis_index("y"), lax.axis_index("c")
        barrier = pltpu.get_barrier_semaphore()
        for r in relations:
            pl.semaphore_signal(barrier, inc=1, device_id=_peer(x, y, c, r), device_id_type=MESH)
        pl.semaphore_wait(barrier, len(relations))
        body(in_refs, out_refs, send, recv, loc)

    launch()
    return [r[...] for r in out_refs]


def all_gather(xs, axes, name, sequencer_id=None):
    n = len(xs)
    fulls = [tuple(d * (NDEV if a == ax else 1) for a, d in enumerate(x.shape)) for x, ax in zip(xs, axes)]
    halved = [x.shape[0] % 32 == 0 for x in xs]

    def body(x_refs, o_refs, send, recv, loc):
        x, y, c = lax.axis_index("x"), lax.axis_index("y"), lax.axis_index("c")
        me, sib = (x, y, c), (x, y, 1 - c)
        xn, yn, dg = (1 - x, y), (x, 1 - y), (1 - x, 1 - y)

        def part(t, p, half=None):
            ref = _shard_of(o_refs[t], axes[t], xs[t].shape[axes[t]], _dev_index(p))
            rows = xs[t].shape[0] // 2
            return ref if half is None else ref.at[pl.ds(half * rows, rows), :]

        def copy(t, k, block, to, half=None, src=None):
            return pltpu.make_async_remote_copy(
                src_ref=part(t, block, half) if src is None else src, dst_ref=part(t, block, half),
                send_sem=send.at[t, k], recv_sem=recv.at[t, k], device_id=to, device_id_type=MESH)

        mine = [pltpu.make_async_copy(x_refs[t], part(t, me), loc.at[t]) for t in range(n)]
        for cp in mine:
            cp.start()
        sends = []
        for t in range(n):
            sends += [copy(t, 0, me, sib, src=x_refs[t]), copy(t, 1, me, (*xn, c), src=x_refs[t]),
                      copy(t, 2, me, (*yn, c), src=x_refs[t])]
            if not halved[t]:
                sends.append(copy(t, 3, me, (*dg, c), src=x_refs[t]))
        for cp in sends:
            cp.start()

        def pass_on(cp):
            cp.start()
            sends.append(cp)

        for t in range(n):
            h0, h1 = (0, 1) if halved[t] else (None, None)
            copy(t, 1, (*xn, c), me).wait_recv()
            if halved[t]:
                pass_on(copy(t, 3, (*xn, c), (*yn, c), 0))
            pass_on(copy(t, 5, (*xn, c), sib))
            copy(t, 2, (*yn, c), me).wait_recv()
            if halved[t]:
                pass_on(copy(t, 4, (*yn, c), (*xn, c), 1))
            pass_on(copy(t, 6, (*yn, c), sib))
            copy(t, 3, (*dg, c), me, h0).wait_recv()
            pass_on(copy(t, 7, (*dg, c), sib, h0))
            if halved[t]:
                copy(t, 4, (*dg, c), me, h1).wait_recv()
                pass_on(copy(t, 8, (*dg, c), sib, h1))
        for t in range(n):
            h0, h1 = (0, 1) if halved[t] else (None, None)
            copy(t, 0, sib, me).wait_recv()
            copy(t, 5, (*xn, 1 - c), me).wait_recv()
            copy(t, 6, (*yn, 1 - c), me).wait_recv()
            copy(t, 7, (*dg, 1 - c), me, h0).wait_recv()
            if halved[t]:
                copy(t, 8, (*dg, 1 - c), me, h1).wait_recv()
        for cp in sends:
            cp.wait_send()
        for cp in mine:
            cp.wait()

    return _launch(body, xs, [_sds(f, x.dtype) for f, x in zip(fulls, xs)], name, sequencer_id, kinds=9)


def pair_add(g, p, axis, name, after=()):
    _, rows, cols = p.shape
    tr = _row_tile(rows, cols, 1 << 20)
    nb = rows // tr
    if axis == 0:
        g_spec = pl.BlockSpec((tr, cols), lambda j, i, c: ((2 * j + c[0]) * nb + i, 0))
    else:
        g_spec = pl.BlockSpec((tr, cols), lambda j, i, c: (i, 2 * j + c[0]))
    blk = pl.BlockSpec((None, tr, cols), lambda j, i, c: (j, i, 0))

    na = len(after)

    def body(c_ref, *refs):
        g_ref, p_ref, q_ref = refs[na:]
        q_ref[...] = (g_ref[...].astype(f32) + p_ref[...].astype(f32)).astype(bf16)

    return pl.pallas_call(
        body, name=name, out_shape=_sds(p.shape, bf16),
        grid_spec=pltpu.PrefetchScalarGridSpec(num_scalar_prefetch=1, grid=(4, nb), in_specs=[ANY] * na + [g_spec, blk],
                                               out_specs=blk),
        compiler_params=_cp(("parallel", "parallel")),
    )(lax.axis_index("c").astype(jnp.int32).reshape(1), *after, g, p)


def chip_exchange(qs, name, sequencer_id):
    n = len(qs)

    def body(q_refs, r_refs, send, recv, loc):
        x, y, c = lax.axis_index("x"), lax.axis_index("y"), lax.axis_index("c")
        my_chip = 2 * x + y
        peers = [_peer(x, y, c, r) for r in OTHER_CHIPS]
        mine = [pltpu.make_async_copy(q_refs[t].at[my_chip], r_refs[t].at[my_chip], loc.at[t]) for t in range(n)]
        for cp in mine:
            cp.start()

        def copy(t, k, slot):
            p = peers[k]
            return pltpu.make_async_remote_copy(
                src_ref=q_refs[t].at[2 * p[0] + p[1]], dst_ref=r_refs[t].at[slot], send_sem=send.at[t, k], recv_sem=recv.at[t, k],
                device_id=p, device_id_type=MESH)

        sends = [copy(t, k, my_chip) for t in range(n) for k in range(3)]
        for cp in sends:
            cp.start()
        for t in range(n):
            for k in range(3):
                copy(t, k, 2 * peers[k][0] + peers[k][1]).wait_recv()
        for cp in sends:
            cp.wait_send()
        for cp in mine:
            cp.wait()

    return _launch(body, qs, [_sds(q.shape, q.dtype) for q in qs], name, sequencer_id, OTHER_CHIPS)


def _adamw(w, g, m, v):
    m = ADAM_B1 * m + (1.0 - ADAM_B1) * g
    v = ADAM_B2 * v + (1.0 - ADAM_B2) * jnp.square(g)
    m_hat = m / (1.0 - ADAM_B1 ** ADAM_STEP)
    v_hat = v / (1.0 - ADAM_B2 ** ADAM_STEP)
    return -ADAM_LR * (m_hat / (jnp.sqrt(v_hat) + ADAM_EPS) + ADAM_WD * w), m, v


def _sum8(r_ref):
    g = r_ref[0].astype(f32)
    for j in range(1, r_ref.shape[0]):
        g = g + r_ref[j].astype(f32)
    return g


def _row_tile(rows, cols, elems=1 << 18):
    tr = min(rows, max(8, 1 << int(math.log2(elems / cols))))
    while rows % tr:
        tr //= 2
    assert tr % 8 == 0 or tr == rows, (rows, cols)
    return tr


def sum_partials(r, name, after=()):
    _, rows, cols = r.shape
    tr = _row_tile(rows, cols)
    na = len(after)

    def body(*refs):
        refs[na + 1][...] = _sum8(refs[na])

    return pl.pallas_call(
        body, name=name, grid=(rows // tr,),
        in_specs=[ANY] * na + [pl.BlockSpec((r.shape[0], tr, cols), lambda i: (0, i, 0))],
        out_specs=pl.BlockSpec((tr, cols), lambda i: (i, 0)), out_shape=_sds((rows, cols), f32),
        compiler_params=_cp(("parallel",)),
    )(*after, r)


def adamw_t(w, m, v, grads, name):
    rows, nl, cols = w.shape
    tr = min(rows, (1 << 16) // cols)
    blk = pl.BlockSpec((tr, nl, cols), lambda i: (i, 0, 0))
    flat = pl.BlockSpec((tr, cols), lambda i: (i, 0))

    def body(w_ref, m_ref, v_ref, *rest):
        g_refs, (g_ref, d_ref, nm_ref, nv_ref) = rest[:nl], rest[nl:]
        for l in range(nl):
            grad = g_refs[l][...]
            g_ref[:, l, :] = grad
            d_ref[:, l, :], nm_ref[:, l, :], nv_ref[:, l, :] = _adamw(w_ref[:, l, :], grad, m_ref[:, l, :], v_ref[:, l, :])

    return tuple(pl.pallas_call(
        body, name=name, grid=(pl.cdiv(rows, tr),), in_specs=[blk] * 3 + [flat] * nl, out_specs=[blk] * 4,
        out_shape=[_sds(w.shape, f32)] * 4, compiler_params=_cp(("parallel",)),
    )(w, m, v, *grads))


def adamw(w, m, v, layer, name, r=None, g=None, prev=None):
    _, rows, cols = w.shape
    tr = _row_tile(rows, cols)
    blk = pl.BlockSpec((None, tr, cols), lambda i: (layer, i, 0))
    nprev = 0 if prev is None else 4

    def body(w_ref, m_ref, v_ref, src_ref, *rest):
        g_ref, d_ref, nm_ref, nv_ref, token_ref = rest[nprev:]
        grad = _sum8(src_ref) if g is None else src_ref[...]
        g_ref[...] = grad
        d_ref[...], nm_ref[...], nv_ref[...] = _adamw(w_ref[...], grad, m_ref[...], v_ref[...])
        token_ref[...] = jnp.zeros_like(token_ref)

    src, src_spec = (r, pl.BlockSpec((r.shape[0], tr, cols), lambda i: (0, i, 0))) if g is None else (g, pl.BlockSpec((tr, cols), lambda i: (i, 0)))
    *outs, token = pl.pallas_call(
        body, name=name, grid=(rows // tr,), in_specs=[blk] * 3 + [src_spec] + [ANY] * nprev,
        out_specs=[blk] * 4 + [pl.BlockSpec((8, HD), lambda i: (0, 0))],
        out_shape=[_sds(w.shape, f32)] * 4 + [_sds((8, HD), f32)], input_output_aliases={4 + k: k for k in range(nprev)},
        compiler_params=_cp(("arbitrary",)),
    )(w, m, v, src, *(prev or ()))
    return tuple(outs), token


def small_adamw(parts, w, m, v, name, after=()):
    def body(*refs):
        p_ref, w_ref, m_ref, v_ref, g_ref, d_ref, nm_ref, nv_ref = refs[len(after):]
        g = _sum8(p_ref)
        g_ref[...] = g
        d_ref[...], nm_ref[...], nv_ref[...] = _adamw(w_ref[...], g, m_ref[...], v_ref[...])

    vmem = pl.BlockSpec(memory_space=pltpu.VMEM)
    return pl.pallas_call(body, name=name, out_shape=[_sds(w.shape, f32)] * 4, in_specs=[ANY] * len(after) + [vmem] * 4,
                          out_specs=[vmem] * 4, compiler_params=_cp())(*after, parts, w, m, v)


def _pack_rows(wt):
    tail = jnp.pad(wt[5888:N_IN], ((OFF_G - 12 - (OFF_P + 512), 0), (0, 0)))
    return jnp.concatenate([wt[512:2816], wt[2816:5120], wt[5120:5888], wt[0:512], tail], axis=0)


def _unpack_rows(g):
    return jnp.concatenate([g[OFF_P:OFF_P + 512], g[OFF_SB:OFF_SB + 2304], g[OFF_GQ:OFF_GQ + 2304], g[OFF_Z:OFF_Z + 768],
                            g[OFF_G - 12:NP]], axis=0)


def _lanes(v):
    flat = v.reshape(-1)
    n = -(-flat.shape[0] // HD) * HD
    return jnp.pad(flat, (0, n - flat.shape[0])).reshape(n // HD, HD)


def _lanes8(v):
    rows = _lanes(v)
    return jnp.pad(rows, ((0, -rows.shape[0] % 8), (0, 0)))


def _layer_fwd(x, p, l):
    nm = lambda s: f"{s}_l{l}"
    u = rms_fwd(x, p["attn_norm"], nm("rms1"))
    proj = matmul(u, p["w_in"][0], name=nm("inproj"), tb=True)
    y_pool = pool_fwd(proj, p["pool_w"], p["pool_scale"], nm("pool"))
    y_sb = sb_fwd(proj, nm("sb"))
    c = conv_fwd(proj, p["conv"], nm("conv"))
    ga = gdn_a_fwd(c, proj, p["alog"], p["dtb"], nm("gdna"))
    o, states = gdn_b_fwd(*ga, nm("gdnb"))
    y_gdn = gdn_out_fwd(o, proj, p["gdn_norm"], nm("gdno"))
    gates = matmul(u, p["w_in"][1], name=nm("inproj_gates"), tb=True, after=(y_gdn,))
    ups = [matmul(y, p[k], name=nm(k)) for y, k in ((y_pool, "w_pool_up"), (y_sb, "w_sb_up"), (y_gdn, "w_gdn_up"))]
    merged = merge_fwd(gates, ups, nm("merge"))
    x1 = matmul(merged, p["w_out"], name=nm("outproj"), epilogue=lambda acc, r: acc + r, extras=(x,))
    u2 = rms_fwd(x1, p["mlp_norm"], nm("rms2"))
    h2 = matmul(u2, p["w_ff1"], name=nm("ff1"), out_dtype=bf16, epilogue=lambda acc: jnp.square(jnp.maximum(acc, 0.0)))
    x2 = matmul(h2, p["w_ff2"], name=nm("ff2"), epilogue=lambda acc, r: acc + r, extras=(x1,))
    saved = dict(x=x, u=u, proj=proj, gates=gates, y_pool=y_pool, y_sb=y_sb, c=c, ga=ga, o=o, states=states, y_gdn=y_gdn, ups=ups,
                 merged=merged, x1=x1, u2=u2, h2=h2)
    return x2, saved


def _layer_bwd(dx2, dx2b, sv, p, l, finish):
    nm = lambda s: f"{s}_l{l}"
    s = dx2.shape[0]
    dh = matmul(dx2b, p["w_ff2"], name=nm("d_ff2_x"), tb=True, out_dtype=bf16,
                epilogue=lambda acc, h2: acc * (2.0 * jnp.sqrt(h2.astype(f32))), extras=(sv["h2"],))
    g_ff2 = matmul(sv["h2"], dx2b, name=nm("d_ff2_w"), ta=True, out_dtype=bf16)
    du2 = matmul(dh, p["w_ff1"], name=nm("d_ff1_x"), tb=True)
    g_ff1 = matmul(sv["u2"], dh, name=nm("d_ff1_w"), ta=True, out_dtype=bf16)
    dx1, dx1b, d_mlp_norm = rms_bwd(sv["x1"], du2, dx2, p["mlp_norm"], nm("d_rms2"))
    dmerged = matmul(dx1b, p["w_out"], name=nm("d_out_x"), tb=True, out_dtype=bf16)
    g_out = matmul(sv["merged"], dx1b, name=nm("d_out_w"), ta=True, out_dtype=bf16)
    dgates, *dups = merge_bwd(sv["gates"], sv["ups"], dmerged, nm("d_merge"))
    dys, g_ups = [], []
    for dup, y, k in zip(dups, (sv["y_pool"], sv["y_sb"], sv["y_gdn"]), ("w_pool_up", "w_sb_up", "w_gdn_up")):
        dys.append(matmul(dup, p[k], name=nm("d_" + k + "_x"), tb=True, out_dtype=bf16))
        g_ups.append(matmul(y, dup, name=nm("d_" + k + "_w"), ta=True, out_dtype=bf16))
    early = g_ups + [g_out, g_ff1, g_ff2]
    do, dz, d_gdn_norm = gdn_out_bwd(sv["o"], sv["proj"], p["gdn_norm"], dys[2], nm("d_gdno"))
    cots = gdn_b_bwd(*sv["ga"], sv["states"], do, nm("d_gdnb"))
    dc, dab, d_alog, d_dtb, swapped = gdn_a_bwd(sv["c"], sv["proj"], p["alog"], p["dtb"], cots[:5], cots[5], nm("d_gdna"),
                                                (early, BIG_AXES[1:]))
    r_early, sent_early = finish(early, swapped, BIG_AXES[1:], BIG_NAMES[1:], "a", (dab,))
    dgq, d_conv = conv_bwd(sv["proj"], p["conv"], dc, nm("d_conv"), sent_early)
    dq, dk, dv = sb_bwd(sv["proj"], dys[1], nm("d_sb"))
    dp, d_pool_w, d_pool_scale = pool_bwd(sv["proj"], p["pool_w"], p["pool_scale"], dys[0], nm("d_pool"))
    dproj = jnp.concatenate([dq, dk, dv, dgq, dz, dp, jnp.zeros((s, OFF_AB - OFF_P - W_POOL), bf16), dab], axis=1)
    g_in = [matmul(dproj, sv["u"], name=nm("d_in_w"), ta=True, out_dtype=bf16),
            matmul(dgates, sv["u"], name=nm("d_in_w_gates"), ta=True, out_dtype=bf16)]
    du, swapped = matmul(dproj, p["w_in"][0], name=nm("d_in_x"), swap=(g_in, (1, 1)))
    r_in, sent_in = finish(g_in, swapped, (1, 1), ("w_in", "w_in_gates"), "b", ())
    du = matmul(dgates, p["w_in"][1], name=nm("d_in_x_gates"), epilogue=lambda acc, r: acc + r, extras=(du,), after=sent_in)
    dx, dxb, d_attn_norm = rms_bwd(sv["x"], du, dx1, p["attn_norm"], nm("d_rms1"))
    recv = r_in + r_early
    small = [d_attn_norm, d_pool_w, d_pool_scale, d_conv, d_alog, d_dtb, d_gdn_norm, d_mlp_norm]
    return dx, dxb, recv, small


BIG_AXES = (1, 1, 1, 1, 0, 1, 0)
GATHER_ID, EXCHANGE_ID = 1, 2
BIG_NAMES = ("w_in", "w_pool_up", "w_sb_up", "w_gdn_up", "w_out", "w_ff1", "w_ff2")


def kernel(x, attn_norm, w_in, pool_w, pool_scale, gdn_conv, gdn_a_log, gdn_dt_bias, gdn_norm, w_pool_up, w_sb_up, w_gdn_up, w_out, mlp_norm, w_ff1, w_ff2, final_norm, loss_target, m_attn_norm, m_w_in, m_pool_w, m_pool_scale, m_gdn_conv, m_gdn_a_log, m_gdn_dt_bias, m_gdn_norm, m_w_pool_up, m_w_sb_up, m_w_gdn_up, m_w_out, m_mlp_norm, m_w_ff1, m_w_ff2, m_final_norm, v_attn_norm, v_w_in, v_pool_w, v_pool_scale, v_gdn_conv, v_gdn_a_log, v_gdn_dt_bias, v_gdn_norm, v_w_pool_up, v_w_sb_up, v_w_gdn_up, v_w_out, v_mlp_norm, v_w_ff1, v_w_ff2, v_final_norm):
    s = x.shape[1]
    me = _dev_index((lax.axis_index("x"), lax.axis_index("y"), lax.axis_index("c")))
    ncv = gdn_conv.shape[2]

    w_in_t, m_w_in_t, v_w_in_t = (jnp.transpose(a, (2, 0, 1)) for a in (w_in, m_w_in, v_w_in))
    full = []
    for l in range(NL):
        packed_in = _pack_rows(w_in_t[:, l]).astype(bf16)
        in_mix, in_gates = packed_in[:OFF_G], packed_in[OFF_G:]
        shards = [w_pool_up[l].astype(bf16), w_sb_up[l].astype(bf16), w_gdn_up[l].astype(bf16), w_out[l].astype(bf16),
                  w_ff1[l].astype(bf16), w_ff2[l].astype(bf16)]
        if l == 0:
            first = all_gather([in_mix, gdn_conv.reshape(NL * 4, ncv)], (1, 0), "gather_first", sequencer_id=GATHER_ID)
            gates = all_gather([in_gates], (1,), "gather_gates_l0", sequencer_id=GATHER_ID)
            full.append([(first[0], gates[0])] + all_gather(shards, BIG_AXES[1:], "gather_rest_l0", sequencer_id=GATHER_ID))
            conv_full = first[1].reshape(NDEV, NL, 4, ncv).transpose(1, 2, 0, 3).reshape(NL, 4, NDEV * ncv)
        else:
            first = all_gather([in_mix, in_gates], (1, 1), f"gather_in_l{l}", sequencer_id=GATHER_ID)
            full.append([tuple(first)] + all_gather(shards, BIG_AXES[1:], f"gather_rest_l{l}", sequencer_id=GATHER_ID))
    params = []
    for l in range(NL):
        p = dict(zip(("w_in", "w_pool_up", "w_sb_up", "w_gdn_up", "w_out", "w_ff1", "w_ff2"), full[l][:7]))
        p.update(attn_norm=attn_norm[l][None], mlp_norm=mlp_norm[l][None], pool_w=pool_w[l], pool_scale=pool_scale[l][None],
                 conv=conv_full[l], alog=_lanes(gdn_a_log[l]), dtb=_lanes(gdn_dt_bias[l]), gdn_norm=gdn_norm[l][None])
        params.append(p)

    h = x[0]
    saved = []
    for l in range(NL):
        h, sv = _layer_fwd(h, params[l], l)
        saved.append(sv)
    dh, dhb, loss_row, d_final = loss_head(h, loss_target[0], final_norm[None], "loss_head")
    recv, smalls = [None] * NL, [None] * NL
    for l in reversed(range(NL)):
        def finish(gs, ps, axes, names, tag, after, l=l):
            qs = [pair_add(g, p, ax, f"pair_add_{k}_l{l}", after) for g, p, ax, k in zip(gs, ps, axes, names)]
            return chip_exchange(qs, f"exchange_{tag}_l{l}", EXCHANGE_ID), tuple(qs)

        dh, dhb, recv[l], smalls[l] = _layer_bwd(dh, dhb, saved[l], params[l], l, finish)

    small_rows = [_lanes8(t) for l in range(NL) for t in smalls[l]] + [_lanes8(d_final), _lanes8(loss_row)]
    packed = jnp.concatenate(small_rows, axis=0)
    parts = all_gather([packed], (0,), "gather_small", sequencer_id=GATHER_ID)[0].reshape(NDEV, packed.shape[0], HD)

    def pack_small(tree):
        rows = []
        for l in range(NL):
            rows += [_lanes8(tree["attn_norm"][l]), _lanes8(tree["pool_w"][l]), _lanes8(tree["pool_scale"][l]),
                     jnp.zeros((4 * NDEV * ncv // HD, HD), f32), _lanes8(tree["gdn_a_log"][l]), _lanes8(tree["gdn_dt_bias"][l]),
                     _lanes8(tree["gdn_norm"][l]), _lanes8(tree["mlp_norm"][l])]
        rows += [_lanes8(tree["final_norm"]), jnp.zeros((8, HD), f32)]
        return jnp.concatenate(rows, axis=0)

    names = ("attn_norm", "pool_w", "pool_scale", "gdn_a_log", "gdn_dt_bias", "gdn_norm", "mlp_norm", "final_norm")
    w_small = pack_small(dict(zip(names, (attn_norm, pool_w, pool_scale, gdn_a_log, gdn_dt_bias, gdn_norm, mlp_norm, final_norm))))
    m_small = pack_small(dict(zip(names, (m_attn_norm, m_pool_w, m_pool_scale, m_gdn_a_log, m_gdn_dt_bias, m_gdn_norm, m_mlp_norm, m_final_norm))))
    v_small = pack_small(dict(zip(names, (v_attn_norm, v_pool_w, v_pool_scale, v_gdn_a_log, v_gdn_dt_bias, v_gdn_norm, v_mlp_norm, v_final_norm))))
    def unpack_small(buf):
        out, conv_g, r = {}, [], 0
        layer_items = (("attn_norm", (D,)), ("pool_w", (4, HD, HD)), ("pool_scale", (W_POOL,)), ("conv", (4, NDEV * ncv)),
                       ("gdn_a_log", (NH,)), ("gdn_dt_bias", (NH,)), ("gdn_norm", (HD,)), ("mlp_norm", (D,)))
        per_layer = {k: [] for k, _ in layer_items}
        for l in range(NL):
            for k, shape in layer_items:
                size = math.prod(shape)
                nrow = -(-size // (8 * HD)) * 8
                per_layer[k].append(buf[r:r + nrow].reshape(-1)[:size].reshape(shape))
                r += nrow
        for k, _ in layer_items:
            out[k] = jnp.stack(per_layer[k])
        out["final_norm"] = buf[r:r + D // HD].reshape(D)
        out["loss"] = buf[r + D // HD, 0]
        return out

    big_out = {}
    big_names = BIG_NAMES
    big_w = dict(zip(big_names, ((w_in_t, m_w_in_t, v_w_in_t), (w_pool_up, m_w_pool_up, v_w_pool_up), (w_sb_up, m_w_sb_up, v_w_sb_up),
                                 (w_gdn_up, m_w_gdn_up, v_w_gdn_up), (w_out, m_w_out, v_w_out), (w_ff1, m_w_ff1, v_w_ff1),
                                 (w_ff2, m_w_ff2, v_w_ff2))))
    tokens = []
    for l, k in [(l, k) for l in reversed(range(NL)) for k in big_names[1:]]:
        w, m, v = big_w[k]
        big_out[k], token = adamw(w, m, v, l, f"adamw_{k}_l{l}", r=recv[l][big_names.index(k) + 1], prev=big_out.get(k))
        tokens.append(token)
    g_in = []
    for l in range(NL):
        halves = [sum_partials(recv[l][h], f"sum_w_in_{h}_l{l}", tuple(tokens) if l == 0 else ()) for h in range(2)]
        g_in.append(_unpack_rows(jnp.concatenate(halves, axis=0)))
    big_out["w_in"] = adamw_t(*big_w["w_in"], g_in, "adamw_w_in")

    small_out = small_adamw(parts, w_small, m_small, v_small, "adamw_small", after=(big_out["w_in"][1],))
    sm = [unpack_small(b) for b in small_out]
    loss = sm[0]["loss"]
    g_conv = lax.dynamic_slice_in_dim(sm[0]["conv"], me * ncv, ncv, axis=2)
    conv_out = None
    for l in reversed(range(NL)):
        conv_out, _ = adamw(gdn_conv, m_gdn_conv, v_gdn_conv, l, f"adamw_conv_l{l}", g=g_conv[l], prev=conv_out)

    def leaf(i, k):
        if k == "w_in":
            return jnp.transpose(big_out[k][i], (1, 2, 0))
        if k in big_out:
            return big_out[k][i]
        if k == "gdn_conv":
            return conv_out[i]
        return sm[i][k]

    order = ("attn_norm", "w_in", "pool_w", "pool_scale", "gdn_conv", "gdn_a_log", "gdn_dt_bias", "gdn_norm", "w_pool_up",
             "w_sb_up", "w_gdn_up", "w_out", "mlp_norm", "w_ff1", "w_ff2", "final_norm")
    return (loss, dh[None]) + tuple(leaf(i, k) for i in range(4) for k in order)
```

```python
import functools
import math

import jax
import jax.numpy as jnp
from jax import lax
from jax.experimental import pallas as pl
from jax.experimental.pallas import tpu as pltpu
from jax.experimental.pallas import tpu_sc as plsc

f32, bf16 = jnp.float32, jnp.bfloat16

D = 2048
NDEV = 8
NL = 2
HD = 128
NH = 6
WH = NH * HD
W_POOL = 512
EPS = 1e-6
N_IN = 12044
NP = 12288
OFF_SB, OFF_GQ, OFF_Z, OFF_P, OFF_AB, OFF_G = 0, 2304, 4608, 5376, 6016, 6144
AB_LANE = HD - 2 * NH
POOL_WINDOWS = (2, 4, 8, 16)
CH = 128
TQ = 256
VMEM_LIMIT = 56 * 1024 * 1024
ADAM_LR, ADAM_B1, ADAM_B2, ADAM_EPS, ADAM_WD, ADAM_STEP = 0.001, 0.9, 0.999, 1e-08, 0.01, 10
MESH = pl.DeviceIdType.MESH


def _cp(sem=None):
    return pltpu.CompilerParams(dimension_semantics=sem, vmem_limit_bytes=VMEM_LIMIT)


def _sds(shape, dtype):
    return jax.ShapeDtypeStruct(tuple(shape), dtype)


def _swap_copies(g_refs, p_refs, send, recv, axes):
    x, y, core = lax.axis_index("x"), lax.axis_index("y"), lax.axis_index("c")
    return [pltpu.make_async_remote_copy(
        src_ref=_shard_of(g, ax, p.shape[1 + ax], 2 * j + (1 - core)), dst_ref=p.at[j], send_sem=send.at[t, j],
        recv_sem=recv.at[t, j], device_id=(x, y, 1 - core), device_id_type=MESH)
        for t, (g, p, ax) in enumerate(zip(g_refs, p_refs, axes)) for j in range(4)]


def matmul(a, b, *, name, ta=False, tb=False, out_dtype=f32, tm=1024, tn=1024, tk=2048, epilogue=None, extras=(), after=(),
           swap=()):
    m, k = (a.shape[1], a.shape[0]) if ta else a.shape
    n = b.shape[0] if tb else b.shape[1]
    assert k == (b.shape[1] if tb else b.shape[0]) and a.dtype == bf16 and b.dtype == bf16
    tm, tn, tk = min(tm, m), min(tn, n), min(tk, k)
    assert m % tm == 0 and n % tn == 0 and k % tk == 0, (m, n, k, tm, tn, tk)
    nk = k // tk
    a_spec = pl.BlockSpec((tk, tm), lambda i, j, q: (q, i)) if ta else pl.BlockSpec((tm, tk), lambda i, j, q: (i, q))
    b_spec = pl.BlockSpec((tn, tk), lambda i, j, q: (j, q)) if tb else pl.BlockSpec((tk, tn), lambda i, j, q: (q, j))
    e_specs = [pl.BlockSpec((tm, tn), lambda i, j, q: (i, j)) for _ in extras]
    dn = (((0 if ta else 1,), (1 if tb else 0,)), ((), ()))
    ne = len(extras)
    gs, axes = swap if swap else ((), ())
    ns = len(gs)
    shards = [tuple(d // (NDEV if ax_ == ax else 1) for ax_, d in enumerate(g.shape)) for g, ax in zip(gs, axes)]
    grid = (m // tm, n // tn, nk)

    def body(*refs):
        g_refs = refs[len(after):len(after) + ns]
        a_ref, b_ref, *rest = refs[len(after) + ns:]
        e_refs, o_ref, p_refs = rest[:ne], rest[ne], rest[ne + 1:ne + 1 + ns]
        if ns:
            at = [pl.program_id(d) for d in range(3)]
            first = functools.reduce(jnp.logical_and, [c == 0 for c in at])
            last = functools.reduce(jnp.logical_and, [c == g - 1 for c, g in zip(at, grid)])

            @pl.when(first)
            def _():
                for cp in _swap_copies(g_refs, p_refs, *rest[-2:], axes):
                    cp.start()

        part = lax.dot_general(a_ref[...], b_ref[...], dn, preferred_element_type=f32)

        def finish(acc):
            if epilogue is not None:
                acc = epilogue(acc, *[e[...] for e in e_refs])
            o_ref[...] = acc.astype(out_dtype)

        if nk == 1:
            finish(part)
        else:
            acc_ref = rest[ne + 1 + ns]
            q = pl.program_id(2)

            @pl.when(q == 0)
            def _():
                acc_ref[...] = part

            @pl.when(jnp.logical_and(q > 0, q < nk - 1))
            def _():
                acc_ref[...] += part

            @pl.when(q == nk - 1)
            def _():
                finish(acc_ref[...] + part)

        if ns:
            @pl.when(last)
            def _():
                cps = _swap_copies(g_refs, p_refs, *rest[-2:], axes)
                for cp in cps:
                    cp.wait_recv()
                for cp in cps:
                    cp.wait_send()

    hbm = pl.BlockSpec(memory_space=pl.ANY)
    res = pl.pallas_call(
        body, name=name, grid=grid,
        in_specs=[hbm] * (len(after) + ns) + [a_spec, b_spec] + e_specs,
        out_specs=[pl.BlockSpec((tm, tn), lambda i, j, q: (i, j))] + [hbm] * ns,
        out_shape=[_sds((m, n), out_dtype)] + [_sds((4,) + sh, g.dtype) for sh, g in zip(shards, gs)],
        scratch_shapes=([pltpu.VMEM((tm, tn), f32)] if nk > 1 else [])
        + ([pltpu.SemaphoreType.DMA((ns, 4)), pltpu.SemaphoreType.DMA((ns, 4))] if ns else []),
        compiler_params=_cp(("arbitrary",) * 3 if ns else ("parallel", "parallel", "arbitrary")),
    )(*after, *gs, a, b, *extras)
    return (res[0], list(res[1:])) if ns else res[0]


def rowwise(name, fn, rows, params, outs, sums=(), tr=256, after=()):
    s = rows[0][0].shape[0]
    tr = min(tr, s)
    nin, nout = len(rows) + len(params), len(outs)
    in_specs = [pl.BlockSpec((tr, w), functools.partial(lambda i, c: (i, c), c=c)) for (_, w, c) in rows]
    in_specs += [pl.BlockSpec(p.shape, lambda i: (0, 0)) for p in params]
    out_specs = [pl.BlockSpec((tr, w), lambda i: (i, 0)) for (w, _) in outs]
    out_specs += [pl.BlockSpec(sh, lambda i: (0, 0)) for sh in sums]
    out_shape = [_sds((s, w), dt) for (w, dt) in outs] + [_sds(sh, f32) for sh in sums]

    def body(*refs):
        refs = refs[len(after):]
        res = fn(*[r[...] for r in refs[:nin]])
        for r, v in zip(refs[nin:nin + nout], res[:nout]):
            r[...] = v.astype(r.dtype)
        i = pl.program_id(0)
        for r, v in zip(refs[nin + nout:], res[nout:]):
            @pl.when(i == 0)
            def _(r=r, v=v):
                r[...] = v

            @pl.when(i > 0)
            def _(r=r, v=v):
                r[...] += v

    res = pl.pallas_call(
        body, name=name, grid=(s // tr,), in_specs=[pl.BlockSpec(memory_space=pl.ANY)] * len(after) + in_specs,
        out_specs=out_specs, out_shape=out_shape, compiler_params=_cp(("arbitrary",)),
    )(*after, *[r[0] for r in rows], *params)
    return res


def _rms(x, g):
    return x * lax.rsqrt(jnp.mean(x * x, axis=-1, keepdims=True) + EPS) * g


def rms_fwd(x, g, name):
    return rowwise(name, lambda xb, gb: (_rms(xb, gb),), [(x, D, 0)], [g], [(D, bf16)])[0]


def rms_bwd(x, du, dres, g, name, after=()):
    def fn(xb, dub, drb, gb):
        _, vjp = jax.vjp(_rms, xb, gb)
        dx, dg = vjp(dub.astype(f32))
        return drb + dx, drb + dx, dg

    return rowwise(name, fn, [(x, D, 0), (du, D, 0), (dres, D, 0)], [g], [(D, f32), (D, bf16)], [(1, D)], after=after)


def _merge(gates, up_p, up_s, up_g):
    sg = jax.nn.sigmoid(gates)
    return sg[:, :D] * up_p + sg[:, D:2 * D] * up_s + sg[:, 2 * D:] * up_g


def merge_fwd(gates, ups, name):
    return rowwise(name, lambda g, a, b, c: (_merge(g, a, b, c),),
                   [(gates, 3 * D, 0)] + [(u, D, 0) for u in ups], [], [(D, bf16)], tr=128)[0]


def merge_bwd(gates, ups, dmerged, name):
    def fn(g, a, b, c, dm):
        _, vjp = jax.vjp(_merge, g, a, b, c)
        return vjp(dm.astype(f32))

    return rowwise(name, fn, [(gates, 3 * D, 0)] + [(u, D, 0) for u in ups] + [(dmerged, D, 0)], [],
                   [(3 * D, bf16), (D, bf16), (D, bf16), (D, bf16)], tr=128)


def _gdn_out(o, z, g):
    ys = []
    for h in range(NH):
        sl = slice(h * HD, (h + 1) * HD)
        ys.append(_rms(o[:, sl], g) * jax.nn.silu(z[:, sl]))
    return jnp.concatenate(ys, axis=1)


def gdn_out_fwd(o, proj, g, name):
    return rowwise(name, lambda ob, zb, gb: (_gdn_out(ob, zb, gb),), [(o, WH, 0), (proj, WH, OFF_Z // WH)], [g],
                   [(WH, bf16)])[0]


def gdn_out_bwd(o, proj, g, dy, name):
    def fn(ob, zb, dyb, gb):
        _, vjp = jax.vjp(_gdn_out, ob, zb, gb)
        return vjp(dyb.astype(f32))

    return rowwise(name, fn, [(o, WH, 0), (proj, WH, OFF_Z // WH), (dy, WH, 0)], [g], [(WH, f32), (WH, bf16)],
                   [(1, HD)])


def loss_head(x, target, g, name):
    def loss_fn(xb, gb, tb):
        err = _rms(xb, gb) - tb
        return (0.5 / D) * jnp.sum(jnp.sum(err * err, axis=1, keepdims=True), axis=0, keepdims=True)

    def fn(xb, tb, gb):
        val, vjp = jax.vjp(functools.partial(loss_fn, tb=tb), xb, gb)
        dx, dg = vjp(jnp.ones((1, 1), f32))
        return dx, dx, jnp.broadcast_to(val, (1, HD)), dg

    return rowwise(name, fn, [(x, D, 0), (target, D, 0)], [g], [(D, f32), (D, bf16)], [(1, HD), (1, D)])


PB = 256


def _split(v):
    hi = v.astype(bf16)
    return hi, (v - hi.astype(f32)).astype(bf16)


def _band_dot(make_band, v, s, forward):
    hi, lo = _split(v)
    nb = s // PB
    outs = []
    for r in range(nb):
        lo_r = max(r - 1, 0) if forward else r
        hi_r = r + 1 if forward else min(r + 2, nb)
        band = make_band(r * PB, lo_r * PB, (hi_r - lo_r) * PB)
        sl = slice(lo_r * PB, hi_r * PB)
        outs.append(jnp.dot(band, hi[sl], preferred_element_type=f32) + jnp.dot(band, lo[sl], preferred_element_type=f32))
    return jnp.concatenate(outs, axis=0)


def _pool_common(p, win, s):
    def band(row0, col0, ncol):
        t = row0 + lax.broadcasted_iota(jnp.int32, (PB, ncol), 0)
        u = col0 + lax.broadcasted_iota(jnp.int32, (PB, ncol), 1)
        return jnp.logical_and(u <= t, t < u + win).astype(bf16)

    def band_t(row0, col0, ncol):
        u = row0 + lax.broadcasted_iota(jnp.int32, (PB, ncol), 0)
        t = col0 + lax.broadcasted_iota(jnp.int32, (PB, ncol), 1)
        return jnp.logical_and(u <= t, t < u + win).astype(bf16)

    t = lax.broadcasted_iota(jnp.int32, (s, 1), 0)
    inv_n = 1.0 / jnp.minimum(t + 1, win).astype(f32)
    d = _band_dot(band, p, s, True) * inv_n - p
    return d, inv_n, band_t


def pool_fwd(proj, pool_w, pool_scale, name):
    s = proj.shape[0]

    def body(p_ref, w_ref, sc_ref, y_ref):
        win = jnp.left_shift(2, pl.program_id(0))
        d, _, _ = _pool_common(p_ref[...], win, s)
        y = jnp.dot(d.astype(bf16), w_ref[...].astype(bf16), preferred_element_type=f32) * sc_ref[...]
        y_ref[...] = y.astype(bf16)

    return pl.pallas_call(
        body, name=name, grid=(4,),
        in_specs=[pl.BlockSpec((s, HD), lambda g: (0, OFF_P // HD + g)), pl.BlockSpec((None, HD, HD), lambda g: (g, 0, 0)),
                  pl.BlockSpec((1, HD), lambda g: (0, g))],
        out_specs=pl.BlockSpec((s, HD), lambda g: (0, g)), out_shape=_sds((s, W_POOL), bf16),
        compiler_params=_cp(("arbitrary",)),
    )(proj, pool_w, pool_scale)


def pool_bwd(proj, pool_w, pool_scale, dy, name):
    s = proj.shape[0]

    def body(p_ref, w_ref, sc_ref, dy_ref, dp_ref, dw_ref, dsc_ref):
        win = jnp.left_shift(2, pl.program_id(0))
        d, inv_n, band_t = _pool_common(p_ref[...], win, s)
        w = w_ref[...].astype(bf16)
        dyf = dy_ref[...].astype(f32)
        dsc_ref[...] = jnp.sum(dyf * jnp.dot(d.astype(bf16), w, preferred_element_type=f32), axis=0, keepdims=True)
        dys = (dyf * sc_ref[...]).astype(bf16)
        dd = lax.dot_general(dys, w, (((1,), (1,)), ((), ())), preferred_element_type=f32)
        dw_ref[...] = lax.dot_general(d.astype(bf16), dys, (((0,), (0,)), ((), ())), preferred_element_type=f32)
        dp_ref[...] = (_band_dot(band_t, dd * inv_n, s, False) - dd).astype(bf16)

    return pl.pallas_call(
        body, name=name, grid=(4,),
        in_specs=[pl.BlockSpec((s, HD), lambda g: (0, OFF_P // HD + g)), pl.BlockSpec((None, HD, HD), lambda g: (g, 0, 0)),
                  pl.BlockSpec((1, HD), lambda g: (0, g)), pl.BlockSpec((s, HD), lambda g: (0, g))],
        out_specs=[pl.BlockSpec((s, HD), lambda g: (0, g)), pl.BlockSpec((None, HD, HD), lambda g: (g, 0, 0)),
                   pl.BlockSpec((1, HD), lambda g: (0, g))],
        out_shape=[_sds((s, W_POOL), bf16), _sds((4, HD, HD), f32), _sds((1, W_POOL), f32)],
        compiler_params=_cp(("arbitrary",)),
    )(proj, pool_w, pool_scale, dy)


HB = 3
WB = HB * HD


def _heads_of(v):
    return _stack([v[:, h * HD:(h + 1) * HD] for h in range(HB)])


def _bd(a, b, dn):
    return lax.dot_general(a, b, dn, preferred_element_type=f32)


def _run_sum(v, tri):
    hi, lo = _split(v.reshape(HB * TQ, TQ))
    return (jnp.dot(hi, tri, preferred_element_type=f32) + jnp.dot(lo, tri, preferred_element_type=f32)).reshape(HB, TQ, TQ)


def _tri(later):
    j = lax.broadcasted_iota(jnp.int32, (TQ, TQ), 0)
    u = lax.broadcasted_iota(jnp.int32, (TQ, TQ), 1)
    return (j > u if later else j < u).astype(bf16)


def _sb_tile(q, k_ref, kb, carry, diagonal):
    k = _heads_of(k_ref[pl.ds(pl.multiple_of(kb * TQ, TQ), TQ), :].astype(bf16))
    z = _bd(q, k, BNT)
    ls = jax.nn.log_sigmoid(-z)
    if diagonal:
        mask = lax.broadcasted_iota(jnp.int32, (TQ, TQ), 1) < lax.broadcasted_iota(jnp.int32, (TQ, TQ), 0)
        ls = jnp.where(mask, ls, 0.0)
    a = jnp.exp(ls + z + _run_sum(ls, _tri(True)) + carry)
    return z, ls, jnp.where(mask, a, 0.0) if diagonal else a


def sb_fwd(proj, name):
    s = proj.shape[0]
    nq = s // TQ
    scale = HD ** -0.5

    def body(q_ref, k_ref, v_ref, y_ref):
        qi = pl.program_id(1)
        q = _heads_of((q_ref[...] * scale).astype(bf16))

        def step(j, c, diagonal=False):
            acc, carry = c
            kb = qi - j
            _, ls, a = _sb_tile(q, k_ref, kb, carry, diagonal)
            v = _heads_of(v_ref[pl.ds(pl.multiple_of(kb * TQ, TQ), TQ), :].astype(bf16))
            return acc + _bd(a.astype(bf16), v, BNN), carry + jnp.sum(ls, axis=2, keepdims=True)

        first = step(0, (jnp.zeros((HB, TQ, HD), f32), jnp.zeros((HB, TQ, 1), f32)), True)
        acc, _ = lax.fori_loop(1, qi + 1, step, first)
        for h in range(HB):
            y_ref[:, h * HD:(h + 1) * HD] = acc[h].astype(bf16)

    c0, ng = OFF_SB // WB, NH // HB
    return pl.pallas_call(
        body, name=name, grid=(ng, nq),
        in_specs=[pl.BlockSpec((TQ, WB), lambda g, i: (i, c0 + g)), pl.BlockSpec((s, WB), lambda g, i: (0, c0 + ng + g)),
                  pl.BlockSpec((s, WB), lambda g, i: (0, c0 + 2 * ng + g))],
        out_specs=pl.BlockSpec((TQ, WB), lambda g, i: (i, g)), out_shape=_sds((s, WH), bf16),
        compiler_params=_cp(("arbitrary", "arbitrary")),
    )(proj, proj, proj)


def sb_bwd(proj, dy, name):
    s = proj.shape[0]
    nq = s // TQ
    scale = HD ** -0.5

    def body(q_ref, k_ref, v_ref, do_ref, dq_ref, dk_ref, dv_ref, e_scr, z_scr, dk_acc, dv_acc):
        qi = pl.program_id(1)
        q = _heads_of((q_ref[...] * scale).astype(bf16))
        do = _heads_of(do_ref[...])

        @pl.when(qi == 0)
        def _():
            dk_acc[...] = jnp.zeros_like(dk_acc)
            dv_acc[...] = jnp.zeros_like(dv_acc)

        def add_heads(acc_ref, rows, upd):
            for h in range(HB):
                acc_ref[rows, h * HD:(h + 1) * HD] += upd[h]

        def sweep_back(j, carry, diagonal=False):
            kb = qi - j
            rows = pl.ds(pl.multiple_of(kb * TQ, TQ), TQ)
            z, ls, a = _sb_tile(q, k_ref, kb, carry, diagonal)
            v = _heads_of(v_ref[rows, :].astype(bf16))
            e_scr[kb] = _bd(do, v, BNT) * a
            z_scr[kb] = z
            add_heads(dv_acc, rows, _bd(a.astype(bf16), do, BTN))
            return carry + jnp.sum(ls, axis=2, keepdims=True)

        lax.fori_loop(1, qi + 1, sweep_back, sweep_back(0, jnp.zeros((HB, TQ, 1), f32), True))

        def sweep_fwd(kb, c, diagonal=False):
            dq, carry = c
            rows = pl.ds(pl.multiple_of(kb * TQ, TQ), TQ)
            e, z = e_scr[kb], z_scr[kb]
            sig = jax.nn.sigmoid(z)
            dz = e * (1.0 - sig) - (_run_sum(e, _tri(False)) + carry) * sig
            if diagonal:
                dz = jnp.where(lax.broadcasted_iota(jnp.int32, (TQ, TQ), 1) < lax.broadcasted_iota(jnp.int32, (TQ, TQ), 0), dz, 0.0)
            dz = dz.astype(bf16)
            k = _heads_of(k_ref[rows, :].astype(bf16))
            add_heads(dk_acc, rows, _bd(dz, q, BTN))
            return dq + _bd(dz, k, BNN), carry + jnp.sum(e, axis=2, keepdims=True)

        dq, _ = sweep_fwd(qi, lax.fori_loop(0, qi, sweep_fwd, (jnp.zeros((HB, TQ, HD), f32), jnp.zeros((HB, TQ, 1), f32))), True)
        for h in range(HB):
            dq_ref[:, h * HD:(h + 1) * HD] = (dq[h] * scale).astype(bf16)

        @pl.when(qi == nq - 1)
        def _():
            dk_ref[...] = dk_acc[...].astype(bf16)
            dv_ref[...] = dv_acc[...].astype(bf16)

    c0, ng = OFF_SB // WB, NH // HB
    return pl.pallas_call(
        body, name=name, grid=(ng, nq),
        in_specs=[pl.BlockSpec((TQ, WB), lambda g, i: (i, c0 + g)), pl.BlockSpec((s, WB), lambda g, i: (0, c0 + ng + g)),
                  pl.BlockSpec((s, WB), lambda g, i: (0, c0 + 2 * ng + g)), pl.BlockSpec((TQ, WB), lambda g, i: (i, g))],
        out_specs=[pl.BlockSpec((TQ, WB), lambda g, i: (i, g)), pl.BlockSpec((s, WB), lambda g, i: (0, g)),
                   pl.BlockSpec((s, WB), lambda g, i: (0, g))],
        out_shape=[_sds((s, WH), bf16)] * 3,
        scratch_shapes=[pltpu.VMEM((nq, HB, TQ, TQ), f32), pltpu.VMEM((nq, HB, TQ, TQ), f32), pltpu.VMEM((s, WB), f32),
                        pltpu.VMEM((s, WB), f32)],
        compiler_params=_cp(("arbitrary", "arbitrary")),
    )(proj, proj, proj, dy)


CB = 256


def _shift_down(v, k, s):
    if k == 0:
        return v
    row = lax.broadcasted_iota(jnp.int32, v.shape, 0)
    return jnp.where(row < k, 0.0, pltpu.roll(v, k, axis=0))


def _shift_up(v, k, s):
    if k == 0:
        return v
    row = lax.broadcasted_iota(jnp.int32, v.shape, 0)
    return jnp.where(row >= s - k, 0.0, pltpu.roll(v, s - k, axis=0))


def conv_fwd(proj, w, name):
    s = proj.shape[0]

    def body(x_ref, w_ref, y_ref):
        x, wv = x_ref[...], w_ref[...]
        y = sum(wv[3 - k:4 - k, :] * _shift_down(x, k, s) for k in range(4))
        y_ref[...] = jax.nn.silu(y)

    return pl.pallas_call(
        body, name=name, grid=(3 * WH // CB,),
        in_specs=[pl.BlockSpec((s, CB), lambda j: (0, OFF_GQ // CB + j)), pl.BlockSpec((4, CB), lambda j: (0, j))],
        out_specs=pl.BlockSpec((s, CB), lambda j: (0, j)), out_shape=_sds((s, 3 * WH), f32),
        compiler_params=_cp(("parallel",)),
    )(proj, w)


def conv_bwd(proj, w, dc, name, after=()):
    s = proj.shape[0]

    def body(*refs):
        x_ref, w_ref, dc_ref, dx_ref, dw_ref = refs[len(after):]
        x, wv = x_ref[...], w_ref[...]
        xs = [_shift_down(x, k, s) for k in range(4)]
        y = sum(wv[3 - k:4 - k, :] * xs[k] for k in range(4))
        sig = jax.nn.sigmoid(y)
        dy = dc_ref[...] * (sig * (1.0 + y * (1.0 - sig)))
        dx_ref[...] = sum(wv[3 - k:4 - k, :] * _shift_up(dy, k, s) for k in range(4)).astype(bf16)
        dw_ref[...] = jnp.concatenate([jnp.sum(dy * xs[3 - i], axis=0, keepdims=True) for i in range(4)], axis=0)

    return pl.pallas_call(
        body, name=name, grid=(3 * WH // CB,),
        in_specs=[pl.BlockSpec(memory_space=pl.ANY)] * len(after)
        + [pl.BlockSpec((s, CB), lambda j: (0, OFF_GQ // CB + j)), pl.BlockSpec((4, CB), lambda j: (0, j)),
           pl.BlockSpec((s, CB), lambda j: (0, j))],
        out_specs=[pl.BlockSpec((s, CB), lambda j: (0, j)), pl.BlockSpec((4, CB), lambda j: (0, j))],
        out_shape=[_sds((s, 3 * WH), bf16), _sds((4, 3 * WH), f32)],
        compiler_params=_cp(("parallel",)),
    )(*after, proj, w, dc)


SOLVE_PASSES = 3


def _pdot_impl(a, b, dn, passes):
    ah, al = _split(a)
    bh, bl = _split(b)
    dot = lambda p, q: lax.dot_general(p, q, dn, preferred_element_type=f32)
    if passes == 1:
        return dot(ah, bh)
    if passes == 2:
        return dot(ah, bh) + dot(ah, bl)
    return dot(ah, bh) + (dot(ah, bl) + dot(al, bh))


BNN, BNT, BTN = (((2,), (1,)), ((0,), (0,))), (((2,), (2,)), ((0,), (0,))), (((1,), (1,)), ((0,), (0,)))


@functools.partial(jax.custom_vjp, nondiff_argnums=(2,))
def _bdot(a, b, passes):
    return _pdot_impl(a, b, BNN, passes)


def _bdot_fwd(a, b, passes):
    return _pdot_impl(a, b, BNN, passes), (a, b)


def _bdot_bwd(passes, res, ct):
    a, b = res
    return _pdot_impl(ct, b, BNT, passes), _pdot_impl(a, ct, BTN, passes)


_bdot.defvjp(_bdot_fwd, _bdot_bwd)


@functools.partial(jax.custom_vjp, nondiff_argnums=(2,))
def _bdot_nt(a, b, passes):
    return _pdot_impl(a, b, BNT, passes)


def _bdot_nt_fwd(a, b, passes):
    return _pdot_impl(a, b, BNT, passes), (a, b)


def _bdot_nt_bwd(passes, res, ct):
    a, b = res
    return _pdot_impl(ct, b, BNN, passes), _pdot_impl(ct, a, BTN, passes)


_bdot_nt.defvjp(_bdot_nt_fwd, _bdot_nt_bwd)


def _lane_pick(v, h):
    lane = lax.broadcasted_iota(jnp.int32, v.shape, v.ndim - 1)
    return jnp.sum(jnp.where(lane == h, v, 0.0), axis=-1, keepdims=True)


def _stack(parts):
    return jnp.concatenate([p[None] for p in parts], axis=0)


def _heads(v, first):
    return _stack([_lane_pick(v, first + h) for h in range(NH)])


def _l2n(v):
    return v * lax.rsqrt(jnp.sum(v * v, axis=-1, keepdims=True) + EPS)


def _dot_nt(a, b):
    return lax.dot_general(a, b, (((1,), (1,)), ((), ())), preferred_element_type=f32)


def _inverse_impl(lower):
    i = lax.broadcasted_iota(jnp.int32, (CH, CH), 0)
    j = lax.broadcasted_iota(jnp.int32, (CH, CH), 1)
    inv = (i == j).astype(f32) - lower
    pw = _pdot_impl(lower, lower, BNN, SOLVE_PASSES)
    for m in range(1, int(math.log2(CH))):
        inv = inv + _pdot_impl(inv, pw, BNN, SOLVE_PASSES)
        if m < int(math.log2(CH)) - 1:
            pw = _pdot_impl(pw, pw, BNN, SOLVE_PASSES)
    return inv


@jax.custom_vjp
def _unit_lower_inverse(lower):
    return _inverse_impl(lower)


def _unit_lower_inverse_fwd(lower):
    inv = _inverse_impl(lower)
    return inv, inv


def _unit_lower_inverse_bwd(inv, ct):
    return (-_pdot_impl(_pdot_impl(inv, ct, BTN, SOLVE_PASSES), inv, BNT, SOLVE_PASSES),)


_unit_lower_inverse.defvjp(_unit_lower_inverse_fwd, _unit_lower_inverse_bwd)


def _gdn_chunk(cq, ck, cv, ab, alog, dtb):
    ones = jnp.ones((NH, CH, HD), f32)
    q = _l2n(cq) * (HD ** -0.5)
    k = _l2n(ck)
    la = -jnp.exp(_heads(alog, 0)) * jax.nn.softplus(_heads(ab, AB_LANE) + _heads(dtb, 0))
    beta = jax.nn.sigmoid(_heads(ab, AB_LANE + NH)) * ones
    i = lax.broadcasted_iota(jnp.int32, (CH, CH), 0)
    j = lax.broadcasted_iota(jnp.int32, (CH, CH), 1)
    incl, strict = j <= i, j < i
    g = _bdot(jnp.broadcast_to(incl.astype(f32), (NH, CH, CH)), la * ones, 2)
    g_row = _stack([g[h].T for h in range(NH)])
    gamma = jnp.where(incl, jnp.exp(jnp.where(incl, g - g_row, 0.0)), 0.0)
    lower = jnp.where(strict, beta * _bdot_nt(k, k, 1) * gamma, 0.0)
    inv = _unit_lower_inverse(lower)
    eg = jnp.exp(g)
    u = _bdot(inv, cv * beta, SOLVE_PASSES)
    w = _bdot(inv, k * (beta * eg), SOLVE_PASSES)
    qk = _bdot_nt(q, k, 1) * gamma
    g_last = g[:, CH - 1:CH, :]
    return u, w, q * eg, k * jnp.exp(g_last - g), qk, jnp.exp(g_last)


def _by_head(ref, t=0):
    return _stack([ref[:, t * WH + h * HD:t * WH + (h + 1) * HD] for h in range(NH)])


def gdn_a_fwd(c, proj, alog, dtb, name):
    s = c.shape[0]
    nc = s // CH

    def body(c_ref, ab_ref, al_ref, dt_ref, u_ref, w_ref, qd_ref, kd_ref, qk_ref, dec_ref):
        res = _gdn_chunk(_by_head(c_ref, 0), _by_head(c_ref, 1), _by_head(c_ref, 2), ab_ref[...], al_ref[...], dt_ref[...])
        for h in range(NH):
            sl = slice(h * HD, (h + 1) * HD)
            for r, v in zip((u_ref, w_ref, qd_ref, kd_ref, qk_ref), res[:5]):
                r[:, sl] = v[h]
            dec_ref[:, sl] = jnp.broadcast_to(res[5][h], (8, HD))

    row = pl.BlockSpec((CH, WH), lambda n: (n, 0))
    par = pl.BlockSpec((1, HD), lambda n: (0, 0))
    return pl.pallas_call(
        body, name=name, grid=(nc,),
        in_specs=[pl.BlockSpec((CH, 3 * WH), lambda n: (n, 0)), pl.BlockSpec((CH, HD), lambda n: (n, OFF_AB // HD)), par, par],
        out_specs=[row] * 5 + [pl.BlockSpec((None, 8, WH), lambda n: (n, 0, 0))],
        out_shape=[_sds((s, WH), f32)] * 5 + [_sds((nc, 8, WH), f32)],
        compiler_params=_cp(("parallel",)),
    )(c, proj, alog, dtb)


def gdn_a_bwd(c, proj, alog, dtb, cots, ddec, name, swap=()):
    s = c.shape[0]
    nc = s // CH
    gs, axes = swap if swap else ((), ())
    ns = len(gs)
    shards = [tuple(d // (NDEV if a == ax else 1) for a, d in enumerate(g.shape)) for g, ax in zip(gs, axes)]

    def body(c_ref, ab_ref, al_ref, dt_ref, du_ref, dw_ref, dqd_ref, dkd_ref, dqk_ref, ddec_ref, *rest):
        g_refs, (dc_ref, dab_ref, dal_ref, ddt_ref), p_refs = rest[:ns], rest[ns:ns + 4], rest[ns + 4:2 * ns + 4]
        n = pl.program_id(0)

        def copies():
            return _swap_copies(g_refs, p_refs, *rest[2 * ns + 4:], axes)

        if ns:
            @pl.when(n == 0)
            def _():
                for cp in copies():
                    cp.start()

        _, vjp = jax.vjp(_gdn_chunk, _by_head(c_ref, 0), _by_head(c_ref, 1), _by_head(c_ref, 2), ab_ref[...], al_ref[...],
                         dt_ref[...])
        lane = lax.broadcasted_iota(jnp.int32, (1, HD), 1)
        dd = _stack([jnp.where(lane == 0, ddec_ref[0:1, h * HD:(h + 1) * HD], 0.0) for h in range(NH)])
        dcq, dck, dcv, dab, dal, ddt = vjp(tuple(_by_head(r) for r in (du_ref, dw_ref, dqd_ref, dkd_ref, dqk_ref)) + (dd,))
        for h in range(NH):
            for t, v in enumerate((dcq, dck, dcv)):
                dc_ref[:, t * WH + h * HD:t * WH + (h + 1) * HD] = v[h]
        dab_ref[...] = dab.astype(bf16)

        @pl.when(n == 0)
        def _():
            dal_ref[...] = dal
            ddt_ref[...] = ddt

        @pl.when(n > 0)
        def _():
            dal_ref[...] += dal
            ddt_ref[...] += ddt

        if ns:
            @pl.when(n == nc - 1)
            def _():
                cps = copies()
                for cp in cps:
                    cp.wait_recv()
                for cp in cps:
                    cp.wait_send()

    row = pl.BlockSpec((CH, WH), lambda n: (n, 0))
    wide = pl.BlockSpec((CH, 3 * WH), lambda n: (n, 0))
    par = pl.BlockSpec((1, HD), lambda n: (0, 0))
    res = pl.pallas_call(
        body, name=name, grid=(nc,),
        in_specs=[wide, pl.BlockSpec((CH, HD), lambda n: (n, OFF_AB // HD)), par, par] + [row] * 5
        + [pl.BlockSpec((None, 8, WH), lambda n: (n, 0, 0))] + [ANY] * ns,
        out_specs=[wide, pl.BlockSpec((CH, HD), lambda n: (n, 0)), par, par] + [ANY] * ns,
        out_shape=[_sds((s, 3 * WH), f32), _sds((s, HD), bf16), _sds((1, HD), f32), _sds((1, HD), f32)]
        + [_sds((4,) + sh, g.dtype) for sh, g in zip(shards, gs)],
        scratch_shapes=[pltpu.SemaphoreType.DMA((ns, 4)), pltpu.SemaphoreType.DMA((ns, 4))] if ns else [],
        compiler_params=_cp(("arbitrary",)),
    )(c, proj, alog, dtb, *cots, ddec, *gs)
    return tuple(res[:4]) + (list(res[4:]),)


def gdn_b_fwd(u, w, qd, kd, qk, dec, name):
    s = u.shape[0]
    nc = s // CH

    def body(u_ref, w_ref, qd_ref, kd_ref, qk_ref, dec_ref, o_ref, st_ref, state):
        n = pl.program_id(0)

        @pl.when(n == 0)
        def _():
            state[...] = jnp.zeros_like(state)

        for h in range(NH):
            sl = slice(h * HD, (h + 1) * HD)
            st = state[sl, :]
            st_ref[sl, :] = st
            sb = st.astype(bf16)
            vn = u_ref[:, sl] - jnp.dot(w_ref[:, sl].astype(bf16), sb, preferred_element_type=f32)
            vb = vn.astype(bf16)
            o_ref[:, sl] = (jnp.dot(qd_ref[:, sl].astype(bf16), sb, preferred_element_type=f32)
                            + jnp.dot(qk_ref[:, sl].astype(bf16), vb, preferred_element_type=f32))
            state[sl, :] = st * dec_ref[0:1, sl] + lax.dot_general(
                kd_ref[:, sl].astype(bf16), vb, (((0,), (0,)), ((), ())), preferred_element_type=f32)

    row = pl.BlockSpec((CH, WH), lambda n: (n, 0))
    return pl.pallas_call(
        body, name=name, grid=(nc,),
        in_specs=[row] * 5 + [pl.BlockSpec((None, 8, WH), lambda n: (n, 0, 0))],
        out_specs=[row, pl.BlockSpec((None, WH, HD), lambda n: (n, 0, 0))],
        out_shape=[_sds((s, WH), f32), _sds((nc, WH, HD), f32)],
        scratch_shapes=[pltpu.VMEM((WH, HD), f32)],
        compiler_params=_cp(("arbitrary",)),
    )(u, w, qd, kd, qk, dec)


def gdn_b_bwd(u, w, qd, kd, qk, dec, states, do, name):
    s = u.shape[0]
    nc = s // CH

    def body(u_ref, w_ref, qd_ref, kd_ref, qk_ref, dec_ref, st_ref, do_ref,
             du_ref, dw_ref, dqd_ref, dkd_ref, dqk_ref, ddec_ref, dstate):
        n = pl.program_id(0)

        @pl.when(n == 0)
        def _():
            dstate[...] = jnp.zeros_like(dstate)

        for h in range(NH):
            sl = slice(h * HD, (h + 1) * HD)
            st, ds = st_ref[sl, :], dstate[sl, :]
            sb, dsb = st.astype(bf16), ds.astype(bf16)
            wb, qdb, kdb, qkb = (r[:, sl].astype(bf16) for r in (w_ref, qd_ref, kd_ref, qk_ref))
            dob = do_ref[:, sl].astype(bf16)
            vn = u_ref[:, sl] - jnp.dot(wb, sb, preferred_element_type=f32)
            vb = vn.astype(bf16)
            dvn = (lax.dot_general(qkb, dob, (((0,), (0,)), ((), ())), preferred_element_type=f32)
                   + jnp.dot(kdb, dsb, preferred_element_type=f32))
            dvb = dvn.astype(bf16)
            du_ref[:, sl] = dvn
            dw_ref[:, sl] = -_dot_nt(dvb, sb)
            dqd_ref[:, sl] = _dot_nt(dob, sb)
            dkd_ref[:, sl] = _dot_nt(vb, dsb)
            dqk_ref[:, sl] = _dot_nt(dob, vb)
            tot = jnp.sum(jnp.sum(ds * st, axis=1, keepdims=True), axis=0, keepdims=True)
            ddec_ref[:, sl] = jnp.broadcast_to(tot, (8, HD))
            dstate[sl, :] = (ds * dec_ref[0:1, sl]
                             + lax.dot_general(qdb, dob, (((0,), (0,)), ((), ())), preferred_element_type=f32)
                             - lax.dot_general(wb, dvb, (((0,), (0,)), ((), ())), preferred_element_type=f32))

    row = pl.BlockSpec((CH, WH), lambda n: (nc - 1 - n, 0))
    small = pl.BlockSpec((None, 8, WH), lambda n: (nc - 1 - n, 0, 0))
    return pl.pallas_call(
        body, name=name, grid=(nc,),
        in_specs=[row] * 5 + [small, pl.BlockSpec((None, WH, HD), lambda n: (nc - 1 - n, 0, 0)), row],
        out_specs=[row] * 5 + [small],
        out_shape=[_sds((s, WH), f32)] * 5 + [_sds((nc, 8, WH), f32)],
        scratch_shapes=[pltpu.VMEM((WH, HD), f32)],
        compiler_params=_cp(("arbitrary",)),
    )(u, w, qd, kd, qk, dec, states, do)


ANY = pl.BlockSpec(memory_space=pl.ANY)


def _dev_index(p):
    return 4 * p[0] + 2 * p[1] + p[2]


def _shard_of(ref, axis, size, idx):
    return ref.at[pl.ds(idx * size, size), :] if axis == 0 else ref.at[:, pl.ds(idx * size, size)]


def _peer(x, y, c, r):
    return (1 - x if r & 4 else x, 1 - y if r & 2 else y, 1 - c if r & 1 else c)


ALL_PEERS, OTHER_CHIPS = tuple(range(1, NDEV)), (4, 2, 6)


def _launch(body, ins, out_sds, name, sequencer_id, relations=ALL_PEERS, kinds=7):
    n = len(ins)
    sems = [pltpu.SemaphoreType.DMA((n, kinds)), pltpu.SemaphoreType.DMA((n, kinds)), pltpu.SemaphoreType.DMA((n,))]
    if sequencer_id is None:
        return pl.pallas_call(
            lambda *refs: body(refs[:n], refs[n:2 * n], *refs[2 * n:]), name=name, in_specs=[ANY] * n, out_specs=[ANY] * n,
            out_shape=out_sds, scratch_shapes=sems, compiler_params=pltpu.CompilerParams(has_side_effects=True),
        )(*ins)
    in_refs = [jax.new_ref(a, memory_space=pltpu.MemorySpace.HBM) for a in ins]
    out_refs = [jax.empty_ref(sd, memory_space=pltpu.MemorySpace.HBM) for sd in out_sds]

    @pl.kernel(mesh=plsc.ScalarSubcoreMesh(axis_name="sequencer", num_cores=1), name=name, scratch_types=sems,
               compiler_params=pltpu.CompilerParams(collective_id=sequencer_id))
    def launch(send, recv, loc):
        x, y, c = lax.axis_index("x"), lax.axis_index("y"), lax.axis_index("c")
        barrier = pltpu.get_barrier_semaphore()
        for r in relations:
            pl.semaphore_signal(barrier, inc=1, device_id=_peer(x, y, c, r), device_id_type=MESH)
        pl.semaphore_wait(barrier, len(relations))
        body(in_refs, out_refs, send, recv, loc)

    launch()
    return [r[...] for r in out_refs]


def all_gather(xs, axes, name, sequencer_id=None):
    n = len(xs)
    fulls = [tuple(d * (NDEV if a == ax else 1) for a, d in enumerate(x.shape)) for x, ax in zip(xs, axes)]
    halved = [x.shape[0] % 32 == 0 for x in xs]

    def body(x_refs, o_refs, send, recv, loc):
        x, y, c = lax.axis_index("x"), lax.axis_index("y"), lax.axis_index("c")
        me, sib = (x, y, c), (x, y, 1 - c)
        xn, yn, dg = (1 - x, y), (x, 1 - y), (1 - x, 1 - y)

        def part(t, p, half=None):
            ref = _shard_of(o_refs[t], axes[t], xs[t].shape[axes[t]], _dev_index(p))
            rows = xs[t].shape[0] // 2
            return ref if half is None else ref.at[pl.ds(half * rows, rows), :]

        def copy(t, k, block, to, half=None, src=None):
            return pltpu.make_async_remote_copy(
                src_ref=part(t, block, half) if src is None else src, dst_ref=part(t, block, half),
                send_sem=send.at[t, k], recv_sem=recv.at[t, k], device_id=to, device_id_type=MESH)

        mine = [pltpu.make_async_copy(x_refs[t], part(t, me), loc.at[t]) for t in range(n)]
        for cp in mine:
            cp.start()
        sends = []
        for t in range(n):
            sends += [copy(t, 0, me, sib, src=x_refs[t]), copy(t, 1, me, (*xn, c), src=x_refs[t]),
                      copy(t, 2, me, (*yn, c), src=x_refs[t])]
            if not halved[t]:
                sends.append(copy(t, 3, me, (*dg, c), src=x_refs[t]))
        for cp in sends:
            cp.start()

        def pass_on(cp):
            cp.start()
            sends.append(cp)

        for t in range(n):
            h0, h1 = (0, 1) if halved[t] else (None, None)
            copy(t, 1, (*xn, c), me).wait_recv()
            if halved[t]:
                pass_on(copy(t, 3, (*xn, c), (*yn, c), 0))
            pass_on(copy(t, 5, (*xn, c), sib))
            copy(t, 2, (*yn, c), me).wait_recv()
            if halved[t]:
                pass_on(copy(t, 4, (*yn, c), (*xn, c), 1))
            pass_on(copy(t, 6, (*yn, c), sib))
            copy(t, 3, (*dg, c), me, h0).wait_recv()
            pass_on(copy(t, 7, (*dg, c), sib, h0))
            if halved[t]:
                copy(t, 4, (*dg, c), me, h1).wait_recv()
                pass_on(copy(t, 8, (*dg, c), sib, h1))
        for t in range(n):
            h0, h1 = (0, 1) if halved[t] else (None, None)
            copy(t, 0, sib, me).wait_recv()
            copy(t, 5, (*xn, 1 - c), me).wait_recv()
            copy(t, 6, (*yn, 1 - c), me).wait_recv()
            copy(t, 7, (*dg, 1 - c), me, h0).wait_recv()
            if halved[t]:
                copy(t, 8, (*dg, 1 - c), me, h1).wait_recv()
        for cp in sends:
            cp.wait_send()
        for cp in mine:
            cp.wait()

    return _launch(body, xs, [_sds(f, x.dtype) for f, x in zip(fulls, xs)], name, sequencer_id, kinds=9)


def pair_add(g, p, axis, name, after=()):
    _, rows, cols = p.shape
    tr = _row_tile(rows, cols, 1 << 20)
    nb = rows // tr
    if axis == 0:
        g_spec = pl.BlockSpec((tr, cols), lambda j, i, c: ((2 * j + c[0]) * nb + i, 0))
    else:
        g_spec = pl.BlockSpec((tr, cols), lambda j, i, c: (i, 2 * j + c[0]))
    blk = pl.BlockSpec((None, tr, cols), lambda j, i, c: (j, i, 0))

    na = len(after)

    def body(c_ref, *refs):
        g_ref, p_ref, q_ref = refs[na:]
        q_ref[...] = (g_ref[...].astype(f32) + p_ref[...].astype(f32)).astype(bf16)

    return pl.pallas_call(
        body, name=name, out_shape=_sds(p.shape, bf16),
        grid_spec=pltpu.PrefetchScalarGridSpec(num_scalar_prefetch=1, grid=(4, nb), in_specs=[ANY] * na + [g_spec, blk],
                                               out_specs=blk),
        compiler_params=_cp(("parallel", "parallel")),
    )(lax.axis_index("c").astype(jnp.int32).reshape(1), *after, g, p)


def chip_exchange(qs, name, sequencer_id):
    n = len(qs)

    def body(q_refs, r_refs, send, recv, loc):
        x, y, c = lax.axis_index("x"), lax.axis_index("y"), lax.axis_index("c")
        my_chip = 2 * x + y
        peers = [_peer(x, y, c, r) for r in OTHER_CHIPS]
        mine = [pltpu.make_async_copy(q_refs[t].at[my_chip], r_refs[t].at[my_chip], loc.at[t]) for t in range(n)]
        for cp in mine:
            cp.start()

        def copy(t, k, slot):
            p = peers[k]
            return pltpu.make_async_remote_copy(
                src_ref=q_refs[t].at[2 * p[0] + p[1]], dst_ref=r_refs[t].at[slot], send_sem=send.at[t, k], recv_sem=recv.at[t, k],
                device_id=p, device_id_type=MESH)

        sends = [copy(t, k, my_chip) for t in range(n) for k in range(3)]
        for cp in sends:
            cp.start()
        for t in range(n):
            for k in range(3):
                copy(t, k, 2 * peers[k][0] + peers[k][1]).wait_recv()
        for cp in sends:
            cp.wait_send()
        for cp in mine:
            cp.wait()

    return _launch(body, qs, [_sds(q.shape, q.dtype) for q in qs], name, sequencer_id, OTHER_CHIPS)


def _adamw(w, g, m, v):
    m = ADAM_B1 * m + (1.0 - ADAM_B1) * g
    v = ADAM_B2 * v + (1.0 - ADAM_B2) * jnp.square(g)
    m_hat = m / (1.0 - ADAM_B1 ** ADAM_STEP)
    v_hat = v / (1.0 - ADAM_B2 ** ADAM_STEP)
    return -ADAM_LR * (m_hat / (jnp.sqrt(v_hat) + ADAM_EPS) + ADAM_WD * w), m, v


def _sum8(r_ref):
    g = r_ref[0].astype(f32)
    for j in range(1, r_ref.shape[0]):
        g = g + r_ref[j].astype(f32)
    return g


def _row_tile(rows, cols, elems=1 << 18):
    tr = min(rows, max(8, 1 << int(math.log2(elems / cols))))
    while rows % tr:
        tr //= 2
    assert tr % 8 == 0 or tr == rows, (rows, cols)
    return tr


def sum_partials(r, name, after=()):
    _, rows, cols = r.shape
    tr = _row_tile(rows, cols)
    na = len(after)

    def body(*refs):
        refs[na + 1][...] = _sum8(refs[na])

    return pl.pallas_call(
        body, name=name, grid=(rows // tr,),
        in_specs=[ANY] * na + [pl.BlockSpec((r.shape[0], tr, cols), lambda i: (0, i, 0))],
        out_specs=pl.BlockSpec((tr, cols), lambda i: (i, 0)), out_shape=_sds((rows, cols), f32),
        compiler_params=_cp(("parallel",)),
    )(*after, r)


def adamw_t(w, m, v, grads, name):
    rows, nl, cols = w.shape
    tr = min(rows, (1 << 16) // cols)
    blk = pl.BlockSpec((tr, nl, cols), lambda i: (i, 0, 0))
    flat = pl.BlockSpec((tr, cols), lambda i: (i, 0))

    def body(w_ref, m_ref, v_ref, *rest):
        g_refs, (g_ref, d_ref, nm_ref, nv_ref) = rest[:nl], rest[nl:]
        for l in range(nl):
            grad = g_refs[l][...]
            g_ref[:, l, :] = grad
            d_ref[:, l, :], nm_ref[:, l, :], nv_ref[:, l, :] = _adamw(w_ref[:, l, :], grad, m_ref[:, l, :], v_ref[:, l, :])

    return tuple(pl.pallas_call(
        body, name=name, grid=(pl.cdiv(rows, tr),), in_specs=[blk] * 3 + [flat] * nl, out_specs=[blk] * 4,
        out_shape=[_sds(w.shape, f32)] * 4, compiler_params=_cp(("parallel",)),
    )(w, m, v, *grads))


def adamw(w, m, v, layer, name, r=None, g=None, prev=None):
    _, rows, cols = w.shape
    tr = _row_tile(rows, cols)
    blk = pl.BlockSpec((None, tr, cols), lambda i: (layer, i, 0))
    nprev = 0 if prev is None else 4

    def body(w_ref, m_ref, v_ref, src_ref, *rest):
        g_ref, d_ref, nm_ref, nv_ref, token_ref = rest[nprev:]
        grad = _sum8(src_ref) if g is None else src_ref[...]
        g_ref[...] = grad
        d_ref[...], nm_ref[...], nv_ref[...] = _adamw(w_ref[...], grad, m_ref[...], v_ref[...])
        token_ref[...] = jnp.zeros_like(token_ref)

    src, src_spec = (r, pl.BlockSpec((r.shape[0], tr, cols), lambda i: (0, i, 0))) if g is None else (g, pl.BlockSpec((tr, cols), lambda i: (i, 0)))
    *outs, token = pl.pallas_call(
        body, name=name, grid=(rows // tr,), in_specs=[blk] * 3 + [src_spec] + [ANY] * nprev,
        out_specs=[blk] * 4 + [pl.BlockSpec((8, HD), lambda i: (0, 0))],
        out_shape=[_sds(w.shape, f32)] * 4 + [_sds((8, HD), f32)], input_output_aliases={4 + k: k for k in range(nprev)},
        compiler_params=_cp(("arbitrary",)),
    )(w, m, v, src, *(prev or ()))
    return tuple(outs), token


def small_adamw(parts, w, m, v, name, after=()):
    def body(*refs):
        p_ref, w_ref, m_ref, v_ref, g_ref, d_ref, nm_ref, nv_ref = refs[len(after):]
        g = _sum8(p_ref)
        g_ref[...] = g
        d_ref[...], nm_ref[...], nv_ref[...] = _adamw(w_ref[...], g, m_ref[...], v_ref[...])

    vmem = pl.BlockSpec(memory_space=pltpu.VMEM)
    return pl.pallas_call(body, name=name, out_shape=[_sds(w.shape, f32)] * 4, in_specs=[ANY] * len(after) + [vmem] * 4,
                          out_specs=[vmem] * 4, compiler_params=_cp())(*after, parts, w, m, v)


def _pack_rows(wt):
    tail = jnp.pad(wt[5888:N_IN], ((OFF_G - 12 - (OFF_P + 512), 0), (0, 0)))
    return jnp.concatenate([wt[512:2816], wt[2816:5120], wt[5120:5888], wt[0:512], tail], axis=0)


def _unpack_rows(g):
    return jnp.concatenate([g[OFF_P:OFF_P + 512], g[OFF_SB:OFF_SB + 2304], g[OFF_GQ:OFF_GQ + 2304], g[OFF_Z:OFF_Z + 768],
                            g[OFF_G - 12:NP]], axis=0)


def _lanes(v):
    flat = v.reshape(-1)
    n = -(-flat.shape[0] // HD) * HD
    return jnp.pad(flat, (0, n - flat.shape[0])).reshape(n // HD, HD)


def _lanes8(v):
    rows = _lanes(v)
    return jnp.pad(rows, ((0, -rows.shape[0] % 8), (0, 0)))


def _layer_fwd(x, p, l):
    nm = lambda s: f"{s}_l{l}"
    u = rms_fwd(x, p["attn_norm"], nm("rms1"))
    proj = matmul(u, p["w_in"][0], name=nm("inproj"), tb=True)
    y_pool = pool_fwd(proj, p["pool_w"], p["pool_scale"], nm("pool"))
    y_sb = sb_fwd(proj, nm("sb"))
    c = conv_fwd(proj, p["conv"], nm("conv"))
    ga = gdn_a_fwd(c, proj, p["alog"], p["dtb"], nm("gdna"))
    o, states = gdn_b_fwd(*ga, nm("gdnb"))
    y_gdn = gdn_out_fwd(o, proj, p["gdn_norm"], nm("gdno"))
    gates = matmul(u, p["w_in"][1], name=nm("inproj_gates"), tb=True, after=(y_gdn,))
    ups = [matmul(y, p[k], name=nm(k)) for y, k in ((y_pool, "w_pool_up"), (y_sb, "w_sb_up"), (y_gdn, "w_gdn_up"))]
    merged = merge_fwd(gates, ups, nm("merge"))
    x1 = matmul(merged, p["w_out"], name=nm("outproj"), epilogue=lambda acc, r: acc + r, extras=(x,))
    u2 = rms_fwd(x1, p["mlp_norm"], nm("rms2"))
    h2 = matmul(u2, p["w_ff1"], name=nm("ff1"), out_dtype=bf16, epilogue=lambda acc: jnp.square(jnp.maximum(acc, 0.0)))
    x2 = matmul(h2, p["w_ff2"], name=nm("ff2"), epilogue=lambda acc, r: acc + r, extras=(x1,))
    saved = dict(x=x, u=u, proj=proj, gates=gates, y_pool=y_pool, y_sb=y_sb, c=c, ga=ga, o=o, states=states, y_gdn=y_gdn, ups=ups,
                 merged=merged, x1=x1, u2=u2, h2=h2)
    return x2, saved


def _layer_bwd(dx2, dx2b, sv, p, l, finish):
    nm = lambda s: f"{s}_l{l}"
    s = dx2.shape[0]
    dh = matmul(dx2b, p["w_ff2"], name=nm("d_ff2_x"), tb=True, out_dtype=bf16,
                epilogue=lambda acc, h2: acc * (2.0 * jnp.sqrt(h2.astype(f32))), extras=(sv["h2"],))
    g_ff2 = matmul(sv["h2"], dx2b, name=nm("d_ff2_w"), ta=True, out_dtype=bf16)
    du2 = matmul(dh, p["w_ff1"], name=nm("d_ff1_x"), tb=True)
    g_ff1 = matmul(sv["u2"], dh, name=nm("d_ff1_w"), ta=True, out_dtype=bf16)
    dx1, dx1b, d_mlp_norm = rms_bwd(sv["x1"], du2, dx2, p["mlp_norm"], nm("d_rms2"))
    dmerged = matmul(dx1b, p["w_out"], name=nm("d_out_x"), tb=True, out_dtype=bf16)
    g_out = matmul(sv["merged"], dx1b, name=nm("d_out_w"), ta=True, out_dtype=bf16)
    dgates, *dups = merge_bwd(sv["gates"], sv["ups"], dmerged, nm("d_merge"))
    dys, g_ups = [], []
    for dup, y, k in zip(dups, (sv["y_pool"], sv["y_sb"], sv["y_gdn"]), ("w_pool_up", "w_sb_up", "w_gdn_up")):
        dys.append(matmul(dup, p[k], name=nm("d_" + k + "_x"), tb=True, out_dtype=bf16))
        g_ups.append(matmul(y, dup, name=nm("d_" + k + "_w"), ta=True, out_dtype=bf16))
    early = g_ups + [g_out, g_ff1, g_ff2]
    do, dz, d_gdn_norm = gdn_out_bwd(sv["o"], sv["proj"], p["gdn_norm"], dys[2], nm("d_gdno"))
    cots = gdn_b_bwd(*sv["ga"], sv["states"], do, nm("d_gdnb"))
    dc, dab, d_alog, d_dtb, swapped = gdn_a_bwd(sv["c"], sv["proj"], p["alog"], p["dtb"], cots[:5], cots[5], nm("d_gdna"),
                                                (early, BIG_AXES[1:]))
    r_early, sent_early = finish(early, swapped, BIG_AXES[1:], BIG_NAMES[1:], "a", (dab,))
    dgq, d_conv = conv_bwd(sv["proj"], p["conv"], dc, nm("d_conv"), sent_early)
    dq, dk, dv = sb_bwd(sv["proj"], dys[1], nm("d_sb"))
    dp, d_pool_w, d_pool_scale = pool_bwd(sv["proj"], p["pool_w"], p["pool_scale"], dys[0], nm("d_pool"))
    dproj = jnp.concatenate([dq, dk, dv, dgq, dz, dp, jnp.zeros((s, OFF_AB - OFF_P - W_POOL), bf16), dab], axis=1)
    g_in = [matmul(dproj, sv["u"], name=nm("d_in_w"), ta=True, out_dtype=bf16),
            matmul(dgates, sv["u"], name=nm("d_in_w_gates"), ta=True, out_dtype=bf16)]
    du, swapped = matmul(dproj, p["w_in"][0], name=nm("d_in_x"), swap=(g_in, (1, 1)))
    r_in, sent_in = finish(g_in, swapped, (1, 1), ("w_in", "w_in_gates"), "b", ())
    du = matmul(dgates, p["w_in"][1], name=nm("d_in_x_gates"), epilogue=lambda acc, r: acc + r, extras=(du,), after=sent_in)
    dx, dxb, d_attn_norm = rms_bwd(sv["x"], du, dx1, p["attn_norm"], nm("d_rms1"))
    recv = r_in + r_early
    small = [d_attn_norm, d_pool_w, d_pool_scale, d_conv, d_alog, d_dtb, d_gdn_norm, d_mlp_norm]
    return dx, dxb, recv, small


BIG_AXES = (1, 1, 1, 1, 0, 1, 0)
GATHER_ID, EXCHANGE_ID = 1, 2
BIG_NAMES = ("w_in", "w_pool_up", "w_sb_up", "w_gdn_up", "w_out", "w_ff1", "w_ff2")


def kernel(x, attn_norm, w_in, pool_w, pool_scale, gdn_conv, gdn_a_log, gdn_dt_bias, gdn_norm, w_pool_up, w_sb_up, w_gdn_up, w_out, mlp_norm, w_ff1, w_ff2, final_norm, loss_target, m_attn_norm, m_w_in, m_pool_w, m_pool_scale, m_gdn_conv, m_gdn_a_log, m_gdn_dt_bias, m_gdn_norm, m_w_pool_up, m_w_sb_up, m_w_gdn_up, m_w_out, m_mlp_norm, m_w_ff1, m_w_ff2, m_final_norm, v_attn_norm, v_w_in, v_pool_w, v_pool_scale, v_gdn_conv, v_gdn_a_log, v_gdn_dt_bias, v_gdn_norm, v_w_pool_up, v_w_sb_up, v_w_gdn_up, v_w_out, v_mlp_norm, v_w_ff1, v_w_ff2, v_final_norm):
    s = x.shape[1]
    me = _dev_index((lax.axis_index("x"), lax.axis_index("y"), lax.axis_index("c")))
    ncv = gdn_conv.shape[2]

    w_in_t, m_w_in_t, v_w_in_t = (jnp.transpose(a, (2, 0, 1)) for a in (w_in, m_w_in, v_w_in))
    full, prev = [], None
    for l in range(NL):
        packed_in = _pack_rows(w_in_t[:, l]).astype(bf16)
        groups = [[packed_in[:OFF_G]] + ([gdn_conv.reshape(NL * 4, ncv)] if l == 0 else []),
                  [packed_in[OFF_G:]],
                  [w_pool_up[l].astype(bf16), w_sb_up[l].astype(bf16), w_gdn_up[l].astype(bf16), w_out[l].astype(bf16)],
                  [w_ff1[l].astype(bf16), w_ff2[l].astype(bf16)]]
        axes = [(1, 0)[:len(groups[0])], (1,), BIG_AXES[1:5], BIG_AXES[5:]]
        got = []
        for k, (group, ax) in enumerate(zip(groups, axes)):
            if prev is not None:
                group, _ = lax.optimization_barrier((group, prev))
            prev = group[0]
            got.append(all_gather(group, ax, f"gather_{k}_l{l}", sequencer_id=GATHER_ID))
        full.append([(got[0][0], got[1][0])] + got[2] + got[3])
        if l == 0:
            conv_full = got[0][1].reshape(NDEV, NL, 4, ncv).transpose(1, 2, 0, 3).reshape(NL, 4, NDEV * ncv)
    params = []
    for l in range(NL):
        p = dict(zip(("w_in", "w_pool_up", "w_sb_up", "w_gdn_up", "w_out", "w_ff1", "w_ff2"), full[l][:7]))
        p.update(attn_norm=attn_norm[l][None], mlp_norm=mlp_norm[l][None], pool_w=pool_w[l], pool_scale=pool_scale[l][None],
                 conv=conv_full[l], alog=_lanes(gdn_a_log[l]), dtb=_lanes(gdn_dt_bias[l]), gdn_norm=gdn_norm[l][None])
        params.append(p)

    h = x[0]
    saved = []
    for l in range(NL):
        h, sv = _layer_fwd(h, params[l], l)
        saved.append(sv)
    dh, dhb, loss_row, d_final = loss_head(h, loss_target[0], final_norm[None], "loss_head")
    recv, smalls = [None] * NL, [None] * NL
    for l in reversed(range(NL)):
        def finish(gs, ps, axes, names, tag, after, l=l):
            qs = [pair_add(g, p, ax, f"pair_add_{k}_l{l}", after) for g, p, ax, k in zip(gs, ps, axes, names)]
            return chip_exchange(qs, f"exchange_{tag}_l{l}", EXCHANGE_ID), tuple(qs)

        dh, dhb, recv[l], smalls[l] = _layer_bwd(dh, dhb, saved[l], params[l], l, finish)

    small_rows = [_lanes8(t) for l in range(NL) for t in smalls[l]] + [_lanes8(d_final), _lanes8(loss_row)]
    packed = jnp.concatenate(small_rows, axis=0)
    parts = all_gather([packed], (0,), "gather_small", sequencer_id=GATHER_ID)[0].reshape(NDEV, packed.shape[0], HD)

    def pack_small(tree):
        rows = []
        for l in range(NL):
            rows += [_lanes8(tree["attn_norm"][l]), _lanes8(tree["pool_w"][l]), _lanes8(tree["pool_scale"][l]),
                     jnp.zeros((4 * NDEV * ncv // HD, HD), f32), _lanes8(tree["gdn_a_log"][l]), _lanes8(tree["gdn_dt_bias"][l]),
                     _lanes8(tree["gdn_norm"][l]), _lanes8(tree["mlp_norm"][l])]
        rows += [_lanes8(tree["final_norm"]), jnp.zeros((8, HD), f32)]
        return jnp.concatenate(rows, axis=0)

    names = ("attn_norm", "pool_w", "pool_scale", "gdn_a_log", "gdn_dt_bias", "gdn_norm", "mlp_norm", "final_norm")
    w_small = pack_small(dict(zip(names, (attn_norm, pool_w, pool_scale, gdn_a_log, gdn_dt_bias, gdn_norm, mlp_norm, final_norm))))
    m_small = pack_small(dict(zip(names, (m_attn_norm, m_pool_w, m_pool_scale, m_gdn_a_log, m_gdn_dt_bias, m_gdn_norm, m_mlp_norm, m_final_norm))))
    v_small = pack_small(dict(zip(names, (v_attn_norm, v_pool_w, v_pool_scale, v_gdn_a_log, v_gdn_dt_bias, v_gdn_norm, v_mlp_norm, v_final_norm))))
    def unpack_small(buf):
        out, conv_g, r = {}, [], 0
        layer_items = (("attn_norm", (D,)), ("pool_w", (4, HD, HD)), ("pool_scale", (W_POOL,)), ("conv", (4, NDEV * ncv)),
                       ("gdn_a_log", (NH,)), ("gdn_dt_bias", (NH,)), ("gdn_norm", (HD,)), ("mlp_norm", (D,)))
        per_layer = {k: [] for k, _ in layer_items}
        for l in range(NL):
            for k, shape in layer_items:
                size = math.prod(shape)
                nrow = -(-size // (8 * HD)) * 8
                per_layer[k].append(buf[r:r + nrow].reshape(-1)[:size].reshape(shape))
                r += nrow
        for k, _ in layer_items:
            out[k] = jnp.stack(per_layer[k])
        out["final_norm"] = buf[r:r + D // HD].reshape(D)
        out["loss"] = buf[r + D // HD, 0]
        return out

    big_out = {}
    big_names = BIG_NAMES
    big_w = dict(zip(big_names, ((w_in_t, m_w_in_t, v_w_in_t), (w_pool_up, m_w_pool_up, v_w_pool_up), (w_sb_up, m_w_sb_up, v_w_sb_up),
                                 (w_gdn_up, m_w_gdn_up, v_w_gdn_up), (w_out, m_w_out, v_w_out), (w_ff1, m_w_ff1, v_w_ff1),
                                 (w_ff2, m_w_ff2, v_w_ff2))))
    tokens = []
    for l, k in [(l, k) for l in reversed(range(NL)) for k in big_names[1:]]:
        w, m, v = big_w[k]
        big_out[k], token = adamw(w, m, v, l, f"adamw_{k}_l{l}", r=recv[l][big_names.index(k) + 1], prev=big_out.get(k))
        tokens.append(token)
    g_in = []
    for l in range(NL):
        halves = [sum_partials(recv[l][h], f"sum_w_in_{h}_l{l}", tuple(tokens) if l == 0 else ()) for h in range(2)]
        g_in.append(_unpack_rows(jnp.concatenate(halves, axis=0)))
    big_out["w_in"] = adamw_t(*big_w["w_in"], g_in, "adamw_w_in")

    small_out = small_adamw(parts, w_small, m_small, v_small, "adamw_small", after=(big_out["w_in"][1],))
    sm = [unpack_small(b) for b in small_out]
    loss = sm[0]["loss"]
    g_conv = lax.dynamic_slice_in_dim(sm[0]["conv"], me * ncv, ncv, axis=2)
    conv_out = None
    for l in reversed(range(NL)):
        conv_out, _ = adamw(gdn_conv, m_gdn_conv, v_gdn_conv, l, f"adamw_conv_l{l}", g=g_conv[l], prev=conv_out)

    def leaf(i, k):
        if k == "w_in":
            return jnp.transpose(big_out[k][i], (1, 2, 0))
        if k in big_out:
            return big_out[k][i]
        if k == "gdn_conv":
            return conv_out[i]
        return sm[i][k]

    order = ("attn_norm", "w_in", "pool_w", "pool_scale", "gdn_conv", "gdn_a_log", "gdn_dt_bias", "gdn_norm", "w_pool_up",
             "w_sb_up", "w_gdn_up", "w_out", "mlp_norm", "w_ff1", "w_ff2", "final_norm")
    return (loss, dh[None]) + tuple(leaf(i, k) for i in range(4) for k in order)
```

```python
import functools
import math

import jax
import jax.numpy as jnp
from jax import lax
from jax.experimental import pallas as pl
from jax.experimental.pallas import tpu as pltpu
from jax.experimental.pallas import tpu_sc as plsc

f32, bf16 = jnp.float32, jnp.bfloat16

D = 2048
NDEV = 8
NL = 2
HD = 128
NH = 6
WH = NH * HD
W_POOL = 512
EPS = 1e-6
N_IN = 12044
NP = 12288
OFF_SB, OFF_GQ, OFF_Z, OFF_P, OFF_AB, OFF_G = 0, 2304, 4608, 5376, 6016, 6144
AB_LANE = HD - 2 * NH
POOL_WINDOWS = (2, 4, 8, 16)
CH = 128
TQ = 256
VMEM_LIMIT = 56 * 1024 * 1024
ADAM_LR, ADAM_B1, ADAM_B2, ADAM_EPS, ADAM_WD, ADAM_STEP = 0.001, 0.9, 0.999, 1e-08, 0.01, 10
MESH = pl.DeviceIdType.MESH


def _cp(sem=None):
    return pltpu.CompilerParams(dimension_semantics=sem, vmem_limit_bytes=VMEM_LIMIT)


def _sds(shape, dtype):
    return jax.ShapeDtypeStruct(tuple(shape), dtype)


def _swap_copies(g_refs, p_refs, send, recv, axes):
    x, y, core = lax.axis_index("x"), lax.axis_index("y"), lax.axis_index("c")
    return [pltpu.make_async_remote_copy(
        src_ref=_shard_of(g, ax, p.shape[1 + ax], 2 * j + (1 - core)), dst_ref=p.at[j], send_sem=send.at[t, j],
        recv_sem=recv.at[t, j], device_id=(x, y, 1 - core), device_id_type=MESH)
        for t, (g, p, ax) in enumerate(zip(g_refs, p_refs, axes)) for j in range(4)]


def matmul(a, b, *, name, ta=False, tb=False, out_dtype=f32, tm=1024, tn=1024, tk=2048, epilogue=None, extras=(), after=(),
           swap=()):
    m, k = (a.shape[1], a.shape[0]) if ta else a.shape
    n = b.shape[0] if tb else b.shape[1]
    assert k == (b.shape[1] if tb else b.shape[0]) and a.dtype == bf16 and b.dtype == bf16
    if k >= 4 * tk:
        tm, tk = tm // 2, 2 * tk
    tm, tn, tk = min(tm, m), min(tn, n), min(tk, k)
    assert m % tm == 0 and n % tn == 0 and k % tk == 0, (m, n, k, tm, tn, tk)
    nk = k // tk
    a_spec = pl.BlockSpec((tk, tm), lambda i, j, q: (q, i)) if ta else pl.BlockSpec((tm, tk), lambda i, j, q: (i, q))
    b_spec = pl.BlockSpec((tn, tk), lambda i, j, q: (j, q)) if tb else pl.BlockSpec((tk, tn), lambda i, j, q: (q, j))
    e_specs = [pl.BlockSpec((tm, tn), lambda i, j, q: (i, j)) for _ in extras]
    dn = (((0 if ta else 1,), (1 if tb else 0,)), ((), ()))
    ne = len(extras)
    gs, axes = swap if swap else ((), ())
    ns = len(gs)
    shards = [tuple(d // (NDEV if ax_ == ax else 1) for ax_, d in enumerate(g.shape)) for g, ax in zip(gs, axes)]
    grid = (m // tm, n // tn, nk)

    def body(*refs):
        g_refs = refs[len(after):len(after) + ns]
        a_ref, b_ref, *rest = refs[len(after) + ns:]
        e_refs, o_ref, p_refs = rest[:ne], rest[ne], rest[ne + 1:ne + 1 + ns]
        if ns:
            at = [pl.program_id(d) for d in range(3)]
            first = functools.reduce(jnp.logical_and, [c == 0 for c in at])
            last = functools.reduce(jnp.logical_and, [c == g - 1 for c, g in zip(at, grid)])

            @pl.when(first)
            def _():
                for cp in _swap_copies(g_refs, p_refs, *rest[-2:], axes):
                    cp.start()

        part = lax.dot_general(a_ref[...], b_ref[...], dn, preferred_element_type=f32)

        def finish(acc):
            if epilogue is not None:
                acc = epilogue(acc, *[e[...] for e in e_refs])
            o_ref[...] = acc.astype(out_dtype)

        if nk == 1:
            finish(part)
        else:
            acc_ref = rest[ne + 1 + ns]
            q = pl.program_id(2)

            @pl.when(q == 0)
            def _():
                acc_ref[...] = part

            @pl.when(jnp.logical_and(q > 0, q < nk - 1))
            def _():
                acc_ref[...] += part

            @pl.when(q == nk - 1)
            def _():
                finish(acc_ref[...] + part)

        if ns:
            @pl.when(last)
            def _():
                cps = _swap_copies(g_refs, p_refs, *rest[-2:], axes)
                for cp in cps:
                    cp.wait_recv()
                for cp in cps:
                    cp.wait_send()

    hbm = pl.BlockSpec(memory_space=pl.ANY)
    res = pl.pallas_call(
        body, name=name, grid=grid,
        in_specs=[hbm] * (len(after) + ns) + [a_spec, b_spec] + e_specs,
        out_specs=[pl.BlockSpec((tm, tn), lambda i, j, q: (i, j))] + [hbm] * ns,
        out_shape=[_sds((m, n), out_dtype)] + [_sds((4,) + sh, g.dtype) for sh, g in zip(shards, gs)],
        scratch_shapes=([pltpu.VMEM((tm, tn), f32)] if nk > 1 else [])
        + ([pltpu.SemaphoreType.DMA((ns, 4)), pltpu.SemaphoreType.DMA((ns, 4))] if ns else []),
        compiler_params=_cp(("arbitrary",) * 3 if ns else ("parallel", "parallel", "arbitrary")),
    )(*after, *gs, a, b, *extras)
    return (res[0], list(res[1:])) if ns else res[0]


def rowwise(name, fn, rows, params, outs, sums=(), tr=256, after=()):
    s = rows[0][0].shape[0]
    tr = min(tr, s)
    nin, nout = len(rows) + len(params), len(outs)
    in_specs = [pl.BlockSpec((tr, w), functools.partial(lambda i, c: (i, c), c=c)) for (_, w, c) in rows]
    in_specs += [pl.BlockSpec(p.shape, lambda i: (0, 0)) for p in params]
    out_specs = [pl.BlockSpec((tr, w), lambda i: (i, 0)) for (w, _) in outs]
    out_specs += [pl.BlockSpec(sh, lambda i: (0, 0)) for sh in sums]
    out_shape = [_sds((s, w), dt) for (w, dt) in outs] + [_sds(sh, f32) for sh in sums]

    def body(*refs):
        refs = refs[len(after):]
        res = fn(*[r[...] for r in refs[:nin]])
        for r, v in zip(refs[nin:nin + nout], res[:nout]):
            r[...] = v.astype(r.dtype)
        i = pl.program_id(0)
        for r, v in zip(refs[nin + nout:], res[nout:]):
            @pl.when(i == 0)
            def _(r=r, v=v):
                r[...] = v

            @pl.when(i > 0)
            def _(r=r, v=v):
                r[...] += v

    res = pl.pallas_call(
        body, name=name, grid=(s // tr,), in_specs=[pl.BlockSpec(memory_space=pl.ANY)] * len(after) + in_specs,
        out_specs=out_specs, out_shape=out_shape, compiler_params=_cp(("arbitrary",)),
    )(*after, *[r[0] for r in rows], *params)
    return res


def _rms(x, g):
    return x * lax.rsqrt(jnp.mean(x * x, axis=-1, keepdims=True) + EPS) * g


def rms_fwd(x, g, name):
    return rowwise(name, lambda xb, gb: (_rms(xb, gb),), [(x, D, 0)], [g], [(D, bf16)])[0]


def rms_bwd(x, du, dres, g, name, after=()):
    def fn(xb, dub, drb, gb):
        _, vjp = jax.vjp(_rms, xb, gb)
        dx, dg = vjp(dub.astype(f32))
        return drb + dx, drb + dx, dg

    return rowwise(name, fn, [(x, D, 0), (du, D, 0), (dres, D, 0)], [g], [(D, f32), (D, bf16)], [(1, D)], after=after)


def _merge(gates, up_p, up_s, up_g):
    sg = jax.nn.sigmoid(gates)
    return sg[:, :D] * up_p + sg[:, D:2 * D] * up_s + sg[:, 2 * D:] * up_g


def merge_fwd(gates, ups, name):
    return rowwise(name, lambda g, a, b, c: (_merge(g, a, b, c),),
                   [(gates, 3 * D, 0)] + [(u, D, 0) for u in ups], [], [(D, bf16)], tr=128)[0]


def merge_bwd(gates, ups, dmerged, name):
    def fn(g, a, b, c, dm):
        _, vjp = jax.vjp(_merge, g, a, b, c)
        return vjp(dm.astype(f32))

    return rowwise(name, fn, [(gates, 3 * D, 0)] + [(u, D, 0) for u in ups] + [(dmerged, D, 0)], [],
                   [(3 * D, bf16), (D, bf16), (D, bf16), (D, bf16)], tr=128)


def _gdn_out(o, z, g):
    ys = []
    for h in range(NH):
        sl = slice(h * HD, (h + 1) * HD)
        ys.append(_rms(o[:, sl], g) * jax.nn.silu(z[:, sl]))
    return jnp.concatenate(ys, axis=1)


def gdn_out_fwd(o, proj, g, name):
    return rowwise(name, lambda ob, zb, gb: (_gdn_out(ob, zb, gb),), [(o, WH, 0), (proj, WH, OFF_Z // WH)], [g],
                   [(WH, bf16)])[0]


def gdn_out_bwd(o, proj, g, dy, name):
    def fn(ob, zb, dyb, gb):
        _, vjp = jax.vjp(_gdn_out, ob, zb, gb)
        return vjp(dyb.astype(f32))

    return rowwise(name, fn, [(o, WH, 0), (proj, WH, OFF_Z // WH), (dy, WH, 0)], [g], [(WH, f32), (WH, bf16)],
                   [(1, HD)])


def loss_head(x, target, g, name):
    def loss_fn(xb, gb, tb):
        err = _rms(xb, gb) - tb
        return (0.5 / D) * jnp.sum(jnp.sum(err * err, axis=1, keepdims=True), axis=0, keepdims=True)

    def fn(xb, tb, gb):
        val, vjp = jax.vjp(functools.partial(loss_fn, tb=tb), xb, gb)
        dx, dg = vjp(jnp.ones((1, 1), f32))
        return dx, dx, jnp.broadcast_to(val, (1, HD)), dg

    return rowwise(name, fn, [(x, D, 0), (target, D, 0)], [g], [(D, f32), (D, bf16)], [(1, HD), (1, D)])


PB = 256


def _split(v):
    hi = v.astype(bf16)
    return hi, (v - hi.astype(f32)).astype(bf16)


def _band_dot(make_band, v, s, forward):
    hi, lo = _split(v)
    nb = s // PB
    outs = []
    for r in range(nb):
        lo_r = max(r - 1, 0) if forward else r
        hi_r = r + 1 if forward else min(r + 2, nb)
        band = make_band(r * PB, lo_r * PB, (hi_r - lo_r) * PB)
        sl = slice(lo_r * PB, hi_r * PB)
        outs.append(jnp.dot(band, hi[sl], preferred_element_type=f32) + jnp.dot(band, lo[sl], preferred_element_type=f32))
    return jnp.concatenate(outs, axis=0)


def _pool_common(p, win, s):
    def band(row0, col0, ncol):
        t = row0 + lax.broadcasted_iota(jnp.int32, (PB, ncol), 0)
        u = col0 + lax.broadcasted_iota(jnp.int32, (PB, ncol), 1)
        return jnp.logical_and(u <= t, t < u + win).astype(bf16)

    def band_t(row0, col0, ncol):
        u = row0 + lax.broadcasted_iota(jnp.int32, (PB, ncol), 0)
        t = col0 + lax.broadcasted_iota(jnp.int32, (PB, ncol), 1)
        return jnp.logical_and(u <= t, t < u + win).astype(bf16)

    t = lax.broadcasted_iota(jnp.int32, (s, 1), 0)
    inv_n = 1.0 / jnp.minimum(t + 1, win).astype(f32)
    d = _band_dot(band, p, s, True) * inv_n - p
    return d, inv_n, band_t


def pool_fwd(proj, pool_w, pool_scale, name):
    s = proj.shape[0]

    def body(p_ref, w_ref, sc_ref, y_ref):
        win = jnp.left_shift(2, pl.program_id(0))
        d, _, _ = _pool_common(p_ref[...], win, s)
        y = jnp.dot(d.astype(bf16), w_ref[...].astype(bf16), preferred_element_type=f32) * sc_ref[...]
        y_ref[...] = y.astype(bf16)

    return pl.pallas_call(
        body, name=name, grid=(4,),
        in_specs=[pl.BlockSpec((s, HD), lambda g: (0, OFF_P // HD + g)), pl.BlockSpec((None, HD, HD), lambda g: (g, 0, 0)),
                  pl.BlockSpec((1, HD), lambda g: (0, g))],
        out_specs=pl.BlockSpec((s, HD), lambda g: (0, g)), out_shape=_sds((s, W_POOL), bf16),
        compiler_params=_cp(("arbitrary",)),
    )(proj, pool_w, pool_scale)


def pool_bwd(proj, pool_w, pool_scale, dy, name):
    s = proj.shape[0]

    def body(p_ref, w_ref, sc_ref, dy_ref, dp_ref, dw_ref, dsc_ref):
        win = jnp.left_shift(2, pl.program_id(0))
        d, inv_n, band_t = _pool_common(p_ref[...], win, s)
        w = w_ref[...].astype(bf16)
        dyf = dy_ref[...].astype(f32)
        dsc_ref[...] = jnp.sum(dyf * jnp.dot(d.astype(bf16), w, preferred_element_type=f32), axis=0, keepdims=True)
        dys = (dyf * sc_ref[...]).astype(bf16)
        dd = lax.dot_general(dys, w, (((1,), (1,)), ((), ())), preferred_element_type=f32)
        dw_ref[...] = lax.dot_general(d.astype(bf16), dys, (((0,), (0,)), ((), ())), preferred_element_type=f32)
        dp_ref[...] = (_band_dot(band_t, dd * inv_n, s, False) - dd).astype(bf16)

    return pl.pallas_call(
        body, name=name, grid=(4,),
        in_specs=[pl.BlockSpec((s, HD), lambda g: (0, OFF_P // HD + g)), pl.BlockSpec((None, HD, HD), lambda g: (g, 0, 0)),
                  pl.BlockSpec((1, HD), lambda g: (0, g)), pl.BlockSpec((s, HD), lambda g: (0, g))],
        out_specs=[pl.BlockSpec((s, HD), lambda g: (0, g)), pl.BlockSpec((None, HD, HD), lambda g: (g, 0, 0)),
                   pl.BlockSpec((1, HD), lambda g: (0, g))],
        out_shape=[_sds((s, W_POOL), bf16), _sds((4, HD, HD), f32), _sds((1, W_POOL), f32)],
        compiler_params=_cp(("arbitrary",)),
    )(proj, pool_w, pool_scale, dy)


HB = 3
WB = HB * HD


def _heads_of(v):
    return _stack([v[:, h * HD:(h + 1) * HD] for h in range(HB)])


def _bd(a, b, dn):
    return lax.dot_general(a, b, dn, preferred_element_type=f32)


def _run_sum(v, tri):
    hi, lo = _split(v.reshape(HB * TQ, TQ))
    return (jnp.dot(hi, tri, preferred_element_type=f32) + jnp.dot(lo, tri, preferred_element_type=f32)).reshape(HB, TQ, TQ)


def _tri(later):
    j = lax.broadcasted_iota(jnp.int32, (TQ, TQ), 0)
    u = lax.broadcasted_iota(jnp.int32, (TQ, TQ), 1)
    return (j > u if later else j < u).astype(bf16)


def _sb_tile(q, k_ref, kb, carry, diagonal):
    k = _heads_of(k_ref[pl.ds(pl.multiple_of(kb * TQ, TQ), TQ), :].astype(bf16))
    z = _bd(q, k, BNT)
    ls = jax.nn.log_sigmoid(-z)
    if diagonal:
        mask = lax.broadcasted_iota(jnp.int32, (TQ, TQ), 1) < lax.broadcasted_iota(jnp.int32, (TQ, TQ), 0)
        ls = jnp.where(mask, ls, 0.0)
    a = jnp.exp(ls + z + _run_sum(ls, _tri(True)) + carry)
    return z, ls, jnp.where(mask, a, 0.0) if diagonal else a


def sb_fwd(proj, name):
    s = proj.shape[0]
    nq = s // TQ
    scale = HD ** -0.5

    def body(q_ref, k_ref, v_ref, y_ref):
        qi = pl.program_id(1)
        q = _heads_of((q_ref[...] * scale).astype(bf16))

        def step(j, c, diagonal=False):
            acc, carry = c
            kb = qi - j
            _, ls, a = _sb_tile(q, k_ref, kb, carry, diagonal)
            v = _heads_of(v_ref[pl.ds(pl.multiple_of(kb * TQ, TQ), TQ), :].astype(bf16))
            return acc + _bd(a.astype(bf16), v, BNN), carry + jnp.sum(ls, axis=2, keepdims=True)

        first = step(0, (jnp.zeros((HB, TQ, HD), f32), jnp.zeros((HB, TQ, 1), f32)), True)
        acc, _ = lax.fori_loop(1, qi + 1, step, first)
        for h in range(HB):
            y_ref[:, h * HD:(h + 1) * HD] = acc[h].astype(bf16)

    c0, ng = OFF_SB // WB, NH // HB
    return pl.pallas_call(
        body, name=name, grid=(ng, nq),
        in_specs=[pl.BlockSpec((TQ, WB), lambda g, i: (i, c0 + g)), pl.BlockSpec((s, WB), lambda g, i: (0, c0 + ng + g)),
                  pl.BlockSpec((s, WB), lambda g, i: (0, c0 + 2 * ng + g))],
        out_specs=pl.BlockSpec((TQ, WB), lambda g, i: (i, g)), out_shape=_sds((s, WH), bf16),
        compiler_params=_cp(("arbitrary", "arbitrary")),
    )(proj, proj, proj)


def sb_bwd(proj, dy, name):
    s = proj.shape[0]
    nq = s // TQ
    scale = HD ** -0.5

    def body(q_ref, k_ref, v_ref, do_ref, dq_ref, dk_ref, dv_ref, e_scr, z_scr, dk_acc, dv_acc):
        qi = pl.program_id(1)
        q = _heads_of((q_ref[...] * scale).astype(bf16))
        do = _heads_of(do_ref[...])

        @pl.when(qi == 0)
        def _():
            dk_acc[...] = jnp.zeros_like(dk_acc)
            dv_acc[...] = jnp.zeros_like(dv_acc)

        def add_heads(acc_ref, rows, upd):
            for h in range(HB):
                acc_ref[rows, h * HD:(h + 1) * HD] += upd[h]

        def sweep_back(j, carry, diagonal=False):
            kb = qi - j
            rows = pl.ds(pl.multiple_of(kb * TQ, TQ), TQ)
            z, ls, a = _sb_tile(q, k_ref, kb, carry, diagonal)
            v = _heads_of(v_ref[rows, :].astype(bf16))
            e_scr[kb] = _bd(do, v, BNT) * a
            z_scr[kb] = z
            add_heads(dv_acc, rows, _bd(a.astype(bf16), do, BTN))
            return carry + jnp.sum(ls, axis=2, keepdims=True)

        lax.fori_loop(1, qi + 1, sweep_back, sweep_back(0, jnp.zeros((HB, TQ, 1), f32), True))

        def sweep_fwd(kb, c, diagonal=False):
            dq, carry = c
            rows = pl.ds(pl.multiple_of(kb * TQ, TQ), TQ)
            e, z = e_scr[kb], z_scr[kb]
            sig = jax.nn.sigmoid(z)
            dz = e * (1.0 - sig) - (_run_sum(e, _tri(False)) + carry) * sig
            if diagonal:
                dz = jnp.where(lax.broadcasted_iota(jnp.int32, (TQ, TQ), 1) < lax.broadcasted_iota(jnp.int32, (TQ, TQ), 0), dz, 0.0)
            dz = dz.astype(bf16)
            k = _heads_of(k_ref[rows, :].astype(bf16))
            add_heads(dk_acc, rows, _bd(dz, q, BTN))
            return dq + _bd(dz, k, BNN), carry + jnp.sum(e, axis=2, keepdims=True)

        dq, _ = sweep_fwd(qi, lax.fori_loop(0, qi, sweep_fwd, (jnp.zeros((HB, TQ, HD), f32), jnp.zeros((HB, TQ, 1), f32))), True)
        for h in range(HB):
            dq_ref[:, h * HD:(h + 1) * HD] = (dq[h] * scale).astype(bf16)

        @pl.when(qi == nq - 1)
        def _():
            dk_ref[...] = dk_acc[...].astype(bf16)
            dv_ref[...] = dv_acc[...].astype(bf16)

    c0, ng = OFF_SB // WB, NH // HB
    return pl.pallas_call(
        body, name=name, grid=(ng, nq),
        in_specs=[pl.BlockSpec((TQ, WB), lambda g, i: (i, c0 + g)), pl.BlockSpec((s, WB), lambda g, i: (0, c0 + ng + g)),
                  pl.BlockSpec((s, WB), lambda g, i: (0, c0 + 2 * ng + g)), pl.BlockSpec((TQ, WB), lambda g, i: (i, g))],
        out_specs=[pl.BlockSpec((TQ, WB), lambda g, i: (i, g)), pl.BlockSpec((s, WB), lambda g, i: (0, g)),
                   pl.BlockSpec((s, WB), lambda g, i: (0, g))],
        out_shape=[_sds((s, WH), bf16)] * 3,
        scratch_shapes=[pltpu.VMEM((nq, HB, TQ, TQ), f32), pltpu.VMEM((nq, HB, TQ, TQ), f32), pltpu.VMEM((s, WB), f32),
                        pltpu.VMEM((s, WB), f32)],
        compiler_params=_cp(("arbitrary", "arbitrary")),
    )(proj, proj, proj, dy)


CB = 256


def _shift_down(v, k, s):
    if k == 0:
        return v
    row = lax.broadcasted_iota(jnp.int32, v.shape, 0)
    return jnp.where(row < k, 0.0, pltpu.roll(v, k, axis=0))


def _shift_up(v, k, s):
    if k == 0:
        return v
    row = lax.broadcasted_iota(jnp.int32, v.shape, 0)
    return jnp.where(row >= s - k, 0.0, pltpu.roll(v, s - k, axis=0))


def conv_fwd(proj, w, name):
    s = proj.shape[0]

    def body(x_ref, w_ref, y_ref):
        x, wv = x_ref[...], w_ref[...]
        y = sum(wv[3 - k:4 - k, :] * _shift_down(x, k, s) for k in range(4))
        y_ref[...] = jax.nn.silu(y)

    return pl.pallas_call(
        body, name=name, grid=(3 * WH // CB,),
        in_specs=[pl.BlockSpec((s, CB), lambda j: (0, OFF_GQ // CB + j)), pl.BlockSpec((4, CB), lambda j: (0, j))],
        out_specs=pl.BlockSpec((s, CB), lambda j: (0, j)), out_shape=_sds((s, 3 * WH), f32),
        compiler_params=_cp(("parallel",)),
    )(proj, w)


def conv_bwd(proj, w, dc, name, after=()):
    s = proj.shape[0]

    def body(*refs):
        x_ref, w_ref, dc_ref, dx_ref, dw_ref = refs[len(after):]
        x, wv = x_ref[...], w_ref[...]
        xs = [_shift_down(x, k, s) for k in range(4)]
        y = sum(wv[3 - k:4 - k, :] * xs[k] for k in range(4))
        sig = jax.nn.sigmoid(y)
        dy = dc_ref[...] * (sig * (1.0 + y * (1.0 - sig)))
        dx_ref[...] = sum(wv[3 - k:4 - k, :] * _shift_up(dy, k, s) for k in range(4)).astype(bf16)
        dw_ref[...] = jnp.concatenate([jnp.sum(dy * xs[3 - i], axis=0, keepdims=True) for i in range(4)], axis=0)

    return pl.pallas_call(
        body, name=name, grid=(3 * WH // CB,),
        in_specs=[pl.BlockSpec(memory_space=pl.ANY)] * len(after)
        + [pl.BlockSpec((s, CB), lambda j: (0, OFF_GQ // CB + j)), pl.BlockSpec((4, CB), lambda j: (0, j)),
           pl.BlockSpec((s, CB), lambda j: (0, j))],
        out_specs=[pl.BlockSpec((s, CB), lambda j: (0, j)), pl.BlockSpec((4, CB), lambda j: (0, j))],
        out_shape=[_sds((s, 3 * WH), bf16), _sds((4, 3 * WH), f32)],
        compiler_params=_cp(("parallel",)),
    )(*after, proj, w, dc)


SOLVE_PASSES = 3


def _pdot_impl(a, b, dn, passes):
    ah, al = _split(a)
    bh, bl = _split(b)
    dot = lambda p, q: lax.dot_general(p, q, dn, preferred_element_type=f32)
    if passes == 1:
        return dot(ah, bh)
    if passes == 2:
        return dot(ah, bh) + dot(ah, bl)
    return dot(ah, bh) + (dot(ah, bl) + dot(al, bh))


BNN, BNT, BTN = (((2,), (1,)), ((0,), (0,))), (((2,), (2,)), ((0,), (0,))), (((1,), (1,)), ((0,), (0,)))


@functools.partial(jax.custom_vjp, nondiff_argnums=(2,))
def _bdot(a, b, passes):
    return _pdot_impl(a, b, BNN, passes)


def _bdot_fwd(a, b, passes):
    return _pdot_impl(a, b, BNN, passes), (a, b)


def _bdot_bwd(passes, res, ct):
    a, b = res
    return _pdot_impl(ct, b, BNT, passes), _pdot_impl(a, ct, BTN, passes)


_bdot.defvjp(_bdot_fwd, _bdot_bwd)


@functools.partial(jax.custom_vjp, nondiff_argnums=(2,))
def _bdot_nt(a, b, passes):
    return _pdot_impl(a, b, BNT, passes)


def _bdot_nt_fwd(a, b, passes):
    return _pdot_impl(a, b, BNT, passes), (a, b)


def _bdot_nt_bwd(passes, res, ct):
    a, b = res
    return _pdot_impl(ct, b, BNN, passes), _pdot_impl(ct, a, BTN, passes)


_bdot_nt.defvjp(_bdot_nt_fwd, _bdot_nt_bwd)


def _lane_pick(v, h):
    lane = lax.broadcasted_iota(jnp.int32, v.shape, v.ndim - 1)
    return jnp.sum(jnp.where(lane == h, v, 0.0), axis=-1, keepdims=True)


def _stack(parts):
    return jnp.concatenate([p[None] for p in parts], axis=0)


def _heads(v, first):
    return _stack([_lane_pick(v, first + h) for h in range(NH)])


def _l2n(v):
    return v * lax.rsqrt(jnp.sum(v * v, axis=-1, keepdims=True) + EPS)


def _dot_nt(a, b):
    return lax.dot_general(a, b, (((1,), (1,)), ((), ())), preferred_element_type=f32)


def _inverse_impl(lower):
    i = lax.broadcasted_iota(jnp.int32, (CH, CH), 0)
    j = lax.broadcasted_iota(jnp.int32, (CH, CH), 1)
    inv = (i == j).astype(f32) - lower
    pw = _pdot_impl(lower, lower, BNN, SOLVE_PASSES)
    for m in range(1, int(math.log2(CH))):
        inv = inv + _pdot_impl(inv, pw, BNN, SOLVE_PASSES)
        if m < int(math.log2(CH)) - 1:
            pw = _pdot_impl(pw, pw, BNN, SOLVE_PASSES)
    return inv


@jax.custom_vjp
def _unit_lower_inverse(lower):
    return _inverse_impl(lower)


def _unit_lower_inverse_fwd(lower):
    inv = _inverse_impl(lower)
    return inv, inv


def _unit_lower_inverse_bwd(inv, ct):
    return (-_pdot_impl(_pdot_impl(inv, ct, BTN, SOLVE_PASSES), inv, BNT, SOLVE_PASSES),)


_unit_lower_inverse.defvjp(_unit_lower_inverse_fwd, _unit_lower_inverse_bwd)


def _gdn_chunk(cq, ck, cv, ab, alog, dtb):
    ones = jnp.ones((NH, CH, HD), f32)
    q = _l2n(cq) * (HD ** -0.5)
    k = _l2n(ck)
    la = -jnp.exp(_heads(alog, 0)) * jax.nn.softplus(_heads(ab, AB_LANE) + _heads(dtb, 0))
    beta = jax.nn.sigmoid(_heads(ab, AB_LANE + NH)) * ones
    i = lax.broadcasted_iota(jnp.int32, (CH, CH), 0)
    j = lax.broadcasted_iota(jnp.int32, (CH, CH), 1)
    incl, strict = j <= i, j < i
    g = _bdot(jnp.broadcast_to(incl.astype(f32), (NH, CH, CH)), la * ones, 2)
    g_row = _stack([g[h].T for h in range(NH)])
    gamma = jnp.where(incl, jnp.exp(jnp.where(incl, g - g_row, 0.0)), 0.0)
    lower = jnp.where(strict, beta * _bdot_nt(k, k, 1) * gamma, 0.0)
    inv = _unit_lower_inverse(lower)
    eg = jnp.exp(g)
    u = _bdot(inv, cv * beta, SOLVE_PASSES)
    w = _bdot(inv, k * (beta * eg), SOLVE_PASSES)
    qk = _bdot_nt(q, k, 1) * gamma
    g_last = g[:, CH - 1:CH, :]
    return u, w, q * eg, k * jnp.exp(g_last - g), qk, jnp.exp(g_last)


def _by_head(ref, t=0):
    return _stack([ref[:, t * WH + h * HD:t * WH + (h + 1) * HD] for h in range(NH)])


def gdn_a_fwd(c, proj, alog, dtb, name):
    s = c.shape[0]
    nc = s // CH

    def body(c_ref, ab_ref, al_ref, dt_ref, u_ref, w_ref, qd_ref, kd_ref, qk_ref, dec_ref):
        res = _gdn_chunk(_by_head(c_ref, 0), _by_head(c_ref, 1), _by_head(c_ref, 2), ab_ref[...], al_ref[...], dt_ref[...])
        for h in range(NH):
            sl = slice(h * HD, (h + 1) * HD)
            for r, v in zip((u_ref, w_ref, qd_ref, kd_ref, qk_ref), res[:5]):
                r[:, sl] = v[h]
            dec_ref[:, sl] = jnp.broadcast_to(res[5][h], (8, HD))

    row = pl.BlockSpec((CH, WH), lambda n: (n, 0))
    par = pl.BlockSpec((1, HD), lambda n: (0, 0))
    return pl.pallas_call(
        body, name=name, grid=(nc,),
        in_specs=[pl.BlockSpec((CH, 3 * WH), lambda n: (n, 0)), pl.BlockSpec((CH, HD), lambda n: (n, OFF_AB // HD)), par, par],
        out_specs=[row] * 5 + [pl.BlockSpec((None, 8, WH), lambda n: (n, 0, 0))],
        out_shape=[_sds((s, WH), f32)] * 5 + [_sds((nc, 8, WH), f32)],
        compiler_params=_cp(("parallel",)),
    )(c, proj, alog, dtb)


def gdn_a_bwd(c, proj, alog, dtb, cots, ddec, name, swap=()):
    s = c.shape[0]
    nc = s // CH
    gs, axes = swap if swap else ((), ())
    ns = len(gs)
    shards = [tuple(d // (NDEV if a == ax else 1) for a, d in enumerate(g.shape)) for g, ax in zip(gs, axes)]

    def body(c_ref, ab_ref, al_ref, dt_ref, du_ref, dw_ref, dqd_ref, dkd_ref, dqk_ref, ddec_ref, *rest):
        g_refs, (dc_ref, dab_ref, dal_ref, ddt_ref), p_refs = rest[:ns], rest[ns:ns + 4], rest[ns + 4:2 * ns + 4]
        n = pl.program_id(0)

        def copies():
            return _swap_copies(g_refs, p_refs, *rest[2 * ns + 4:], axes)

        if ns:
            @pl.when(n == 0)
            def _():
                for cp in copies():
                    cp.start()

        _, vjp = jax.vjp(_gdn_chunk, _by_head(c_ref, 0), _by_head(c_ref, 1), _by_head(c_ref, 2), ab_ref[...], al_ref[...],
                         dt_ref[...])
        lane = lax.broadcasted_iota(jnp.int32, (1, HD), 1)
        dd = _stack([jnp.where(lane == 0, ddec_ref[0:1, h * HD:(h + 1) * HD], 0.0) for h in range(NH)])
        dcq, dck, dcv, dab, dal, ddt = vjp(tuple(_by_head(r) for r in (du_ref, dw_ref, dqd_ref, dkd_ref, dqk_ref)) + (dd,))
        for h in range(NH):
            for t, v in enumerate((dcq, dck, dcv)):
                dc_ref[:, t * WH + h * HD:t * WH + (h + 1) * HD] = v[h]
        dab_ref[...] = dab.astype(bf16)

        @pl.when(n == 0)
        def _():
            dal_ref[...] = dal
            ddt_ref[...] = ddt

        @pl.when(n > 0)
        def _():
            dal_ref[...] += dal
            ddt_ref[...] += ddt

        if ns:
            @pl.when(n == nc - 1)
            def _():
                cps = copies()
                for cp in cps:
                    cp.wait_recv()
                for cp in cps:
                    cp.wait_send()

    row = pl.BlockSpec((CH, WH), lambda n: (n, 0))
    wide = pl.BlockSpec((CH, 3 * WH), lambda n: (n, 0))
    par = pl.BlockSpec((1, HD), lambda n: (0, 0))
    res = pl.pallas_call(
        body, name=name, grid=(nc,),
        in_specs=[wide, pl.BlockSpec((CH, HD), lambda n: (n, OFF_AB // HD)), par, par] + [row] * 5
        + [pl.BlockSpec((None, 8, WH), lambda n: (n, 0, 0))] + [ANY] * ns,
        out_specs=[wide, pl.BlockSpec((CH, HD), lambda n: (n, 0)), par, par] + [ANY] * ns,
        out_shape=[_sds((s, 3 * WH), f32), _sds((s, HD), bf16), _sds((1, HD), f32), _sds((1, HD), f32)]
        + [_sds((4,) + sh, g.dtype) for sh, g in zip(shards, gs)],
        scratch_shapes=[pltpu.SemaphoreType.DMA((ns, 4)), pltpu.SemaphoreType.DMA((ns, 4))] if ns else [],
        compiler_params=_cp(("arbitrary",)),
    )(c, proj, alog, dtb, *cots, ddec, *gs)
    return tuple(res[:4]) + (list(res[4:]),)


def gdn_b_fwd(u, w, qd, kd, qk, dec, name):
    s = u.shape[0]
    nc = s // CH

    def body(u_ref, w_ref, qd_ref, kd_ref, qk_ref, dec_ref, o_ref, st_ref, state):
        n = pl.program_id(0)

        @pl.when(n == 0)
        def _():
            state[...] = jnp.zeros_like(state)

        for h in range(NH):
            sl = slice(h * HD, (h + 1) * HD)
            st = state[sl, :]
            st_ref[sl, :] = st
            sb = st.astype(bf16)
            vn = u_ref[:, sl] - jnp.dot(w_ref[:, sl].astype(bf16), sb, preferred_element_type=f32)
            vb = vn.astype(bf16)
            o_ref[:, sl] = (jnp.dot(qd_ref[:, sl].astype(bf16), sb, preferred_element_type=f32)
                            + jnp.dot(qk_ref[:, sl].astype(bf16), vb, preferred_element_type=f32))
            state[sl, :] = st * dec_ref[0:1, sl] + lax.dot_general(
                kd_ref[:, sl].astype(bf16), vb, (((0,), (0,)), ((), ())), preferred_element_type=f32)

    row = pl.BlockSpec((CH, WH), lambda n: (n, 0))
    return pl.pallas_call(
        body, name=name, grid=(nc,),
        in_specs=[row] * 5 + [pl.BlockSpec((None, 8, WH), lambda n: (n, 0, 0))],
        out_specs=[row, pl.BlockSpec((None, WH, HD), lambda n: (n, 0, 0))],
        out_shape=[_sds((s, WH), f32), _sds((nc, WH, HD), f32)],
        scratch_shapes=[pltpu.VMEM((WH, HD), f32)],
        compiler_params=_cp(("arbitrary",)),
    )(u, w, qd, kd, qk, dec)


def gdn_b_bwd(u, w, qd, kd, qk, dec, states, do, name):
    s = u.shape[0]
    nc = s // CH

    def body(u_ref, w_ref, qd_ref, kd_ref, qk_ref, dec_ref, st_ref, do_ref,
             du_ref, dw_ref, dqd_ref, dkd_ref, dqk_ref, ddec_ref, dstate):
        n = pl.program_id(0)

        @pl.when(n == 0)
        def _():
            dstate[...] = jnp.zeros_like(dstate)

        for h in range(NH):
            sl = slice(h * HD, (h + 1) * HD)
            st, ds = st_ref[sl, :], dstate[sl, :]
            sb, dsb = st.astype(bf16), ds.astype(bf16)
            wb, qdb, kdb, qkb = (r[:, sl].astype(bf16) for r in (w_ref, qd_ref, kd_ref, qk_ref))
            dob = do_ref[:, sl].astype(bf16)
            vn = u_ref[:, sl] - jnp.dot(wb, sb, preferred_element_type=f32)
            vb = vn.astype(bf16)
            dvn = (lax.dot_general(qkb, dob, (((0,), (0,)), ((), ())), preferred_element_type=f32)
                   + jnp.dot(kdb, dsb, preferred_element_type=f32))
            dvb = dvn.astype(bf16)
            du_ref[:, sl] = dvn
            dw_ref[:, sl] = -_dot_nt(dvb, sb)
            dqd_ref[:, sl] = _dot_nt(dob, sb)
            dkd_ref[:, sl] = _dot_nt(vb, dsb)
            dqk_ref[:, sl] = _dot_nt(dob, vb)
            tot = jnp.sum(jnp.sum(ds * st, axis=1, keepdims=True), axis=0, keepdims=True)
            ddec_ref[:, sl] = jnp.broadcast_to(tot, (8, HD))
            dstate[sl, :] = (ds * dec_ref[0:1, sl]
                             + lax.dot_general(qdb, dob, (((0,), (0,)), ((), ())), preferred_element_type=f32)
                             - lax.dot_general(wb, dvb, (((0,), (0,)), ((), ())), preferred_element_type=f32))

    row = pl.BlockSpec((CH, WH), lambda n: (nc - 1 - n, 0))
    small = pl.BlockSpec((None, 8, WH), lambda n: (nc - 1 - n, 0, 0))
    return pl.pallas_call(
        body, name=name, grid=(nc,),
        in_specs=[row] * 5 + [small, pl.BlockSpec((None, WH, HD), lambda n: (nc - 1 - n, 0, 0)), row],
        out_specs=[row] * 5 + [small],
        out_shape=[_sds((s, WH), f32)] * 5 + [_sds((nc, 8, WH), f32)],
        scratch_shapes=[pltpu.VMEM((WH, HD), f32)],
        compiler_params=_cp(("arbitrary",)),
    )(u, w, qd, kd, qk, dec, states, do)


ANY = pl.BlockSpec(memory_space=pl.ANY)


def _dev_index(p):
    return 4 * p[0] + 2 * p[1] + p[2]


def _shard_of(ref, axis, size, idx):
    return ref.at[pl.ds(idx * size, size), :] if axis == 0 else ref.at[:, pl.ds(idx * size, size)]


def _peer(x, y, c, r):
    return (1 - x if r & 4 else x, 1 - y if r & 2 else y, 1 - c if r & 1 else c)


ALL_PEERS, OTHER_CHIPS = tuple(range(1, NDEV)), (4, 2, 6)


def _launch(body, ins, out_sds, name, sequencer_id, relations=ALL_PEERS, kinds=7):
    n = len(ins)
    sems = [pltpu.SemaphoreType.DMA((n, kinds)), pltpu.SemaphoreType.DMA((n, kinds)), pltpu.SemaphoreType.DMA((n,))]
    if sequencer_id is None:
        return pl.pallas_call(
            lambda *refs: body(refs[:n], refs[n:2 * n], *refs[2 * n:]), name=name, in_specs=[ANY] * n, out_specs=[ANY] * n,
            out_shape=out_sds, scratch_shapes=sems, compiler_params=pltpu.CompilerParams(has_side_effects=True),
        )(*ins)
    in_refs = [jax.new_ref(a, memory_space=pltpu.MemorySpace.HBM) for a in ins]
    out_refs = [jax.empty_ref(sd, memory_space=pltpu.MemorySpace.HBM) for sd in out_sds]

    @pl.kernel(mesh=plsc.ScalarSubcoreMesh(axis_name="sequencer", num_cores=1), name=name, scratch_types=sems,
               compiler_params=pltpu.CompilerParams(collective_id=sequencer_id))
    def launch(send, recv, loc):
        x, y, c = lax.axis_index("x"), lax.axis_index("y"), lax.axis_index("c")
        barrier = pltpu.get_barrier_semaphore()
        for r in relations:
            pl.semaphore_signal(barrier, inc=1, device_id=_peer(x, y, c, r), device_id_type=MESH)
        pl.semaphore_wait(barrier, len(relations))
        body(in_refs, out_refs, send, recv, loc)

    launch()
    return [r[...] for r in out_refs]


def all_gather(xs, axes, name, sequencer_id=None):
    n = len(xs)
    fulls = [tuple(d * (NDEV if a == ax else 1) for a, d in enumerate(x.shape)) for x, ax in zip(xs, axes)]
    halved = [x.shape[0] % 32 == 0 for x in xs]

    def body(x_refs, o_refs, send, recv, loc):
        x, y, c = lax.axis_index("x"), lax.axis_index("y"), lax.axis_index("c")
        me, sib = (x, y, c), (x, y, 1 - c)
        xn, yn, dg = (1 - x, y), (x, 1 - y), (1 - x, 1 - y)

        def part(t, p, half=None):
            ref = _shard_of(o_refs[t], axes[t], xs[t].shape[axes[t]], _dev_index(p))
            rows = xs[t].shape[0] // 2
            return ref if half is None else ref.at[pl.ds(half * rows, rows), :]

        def copy(t, k, block, to, half=None, src=None):
            return pltpu.make_async_remote_copy(
                src_ref=part(t, block, half) if src is None else src, dst_ref=part(t, block, half),
                send_sem=send.at[t, k], recv_sem=recv.at[t, k], device_id=to, device_id_type=MESH)

        mine = [pltpu.make_async_copy(x_refs[t], part(t, me), loc.at[t]) for t in range(n)]
        for cp in mine:
            cp.start()
        sends = []
        for t in range(n):
            sends += [copy(t, 0, me, sib, src=x_refs[t]), copy(t, 1, me, (*xn, c), src=x_refs[t]),
                      copy(t, 2, me, (*yn, c), src=x_refs[t])]
            if not halved[t]:
                sends.append(copy(t, 3, me, (*dg, c), src=x_refs[t]))
        for cp in sends:
            cp.start()

        def pass_on(cp):
            cp.start()
            sends.append(cp)

        for t in range(n):
            h0, h1 = (0, 1) if halved[t] else (None, None)
            copy(t, 1, (*xn, c), me).wait_recv()
            if halved[t]:
                pass_on(copy(t, 3, (*xn, c), (*yn, c), 0))
            pass_on(copy(t, 5, (*xn, c), sib))
            copy(t, 2, (*yn, c), me).wait_recv()
            if halved[t]:
                pass_on(copy(t, 4, (*yn, c), (*xn, c), 1))
            pass_on(copy(t, 6, (*yn, c), sib))
            copy(t, 3, (*dg, c), me, h0).wait_recv()
            pass_on(copy(t, 7, (*dg, c), sib, h0))
            if halved[t]:
                copy(t, 4, (*dg, c), me, h1).wait_recv()
                pass_on(copy(t, 8, (*dg, c), sib, h1))
        for t in range(n):
            h0, h1 = (0, 1) if halved[t] else (None, None)
            copy(t, 0, sib, me).wait_recv()
            copy(t, 5, (*xn, 1 - c), me).wait_recv()
            copy(t, 6, (*yn, 1 - c), me).wait_recv()
            copy(t, 7, (*dg, 1 - c), me, h0).wait_recv()
            if halved[t]:
                copy(t, 8, (*dg, 1 - c), me, h1).wait_recv()
        for cp in sends:
            cp.wait_send()
        for cp in mine:
            cp.wait()

    return _launch(body, xs, [_sds(f, x.dtype) for f, x in zip(fulls, xs)], name, sequencer_id, kinds=9)


def pair_add(g, p, axis, name, after=()):
    _, rows, cols = p.shape
    tr = _row_tile(rows, cols, 1 << 20)
    nb = rows // tr
    if axis == 0:
        g_spec = pl.BlockSpec((tr, cols), lambda j, i, c: ((2 * j + c[0]) * nb + i, 0))
    else:
        g_spec = pl.BlockSpec((tr, cols), lambda j, i, c: (i, 2 * j + c[0]))
    blk = pl.BlockSpec((None, tr, cols), lambda j, i, c: (j, i, 0))

    na = len(after)

    def body(c_ref, *refs):
        g_ref, p_ref, q_ref = refs[na:]
        q_ref[...] = (g_ref[...].astype(f32) + p_ref[...].astype(f32)).astype(bf16)

    return pl.pallas_call(
        body, name=name, out_shape=_sds(p.shape, bf16),
        grid_spec=pltpu.PrefetchScalarGridSpec(num_scalar_prefetch=1, grid=(4, nb), in_specs=[ANY] * na + [g_spec, blk],
                                               out_specs=blk),
        compiler_params=_cp(("parallel", "parallel")),
    )(lax.axis_index("c").astype(jnp.int32).reshape(1), *after, g, p)


def chip_exchange(qs, name, sequencer_id):
    n = len(qs)

    def body(q_refs, r_refs, send, recv, loc):
        x, y, c = lax.axis_index("x"), lax.axis_index("y"), lax.axis_index("c")
        my_chip = 2 * x + y
        peers = [_peer(x, y, c, r) for r in OTHER_CHIPS]
        mine = [pltpu.make_async_copy(q_refs[t].at[my_chip], r_refs[t].at[my_chip], loc.at[t]) for t in range(n)]
        for cp in mine:
            cp.start()

        def copy(t, k, slot):
            p = peers[k]
            return pltpu.make_async_remote_copy(
                src_ref=q_refs[t].at[2 * p[0] + p[1]], dst_ref=r_refs[t].at[slot], send_sem=send.at[t, k], recv_sem=recv.at[t, k],
                device_id=p, device_id_type=MESH)

        sends = [copy(t, k, my_chip) for t in range(n) for k in range(3)]
        for cp in sends:
            cp.start()
        for t in range(n):
            for k in range(3):
                copy(t, k, 2 * peers[k][0] + peers[k][1]).wait_recv()
        for cp in sends:
            cp.wait_send()
        for cp in mine:
            cp.wait()

    return _launch(body, qs, [_sds(q.shape, q.dtype) for q in qs], name, sequencer_id, OTHER_CHIPS)


def _adamw(w, g, m, v):
    m = ADAM_B1 * m + (1.0 - ADAM_B1) * g
    v = ADAM_B2 * v + (1.0 - ADAM_B2) * jnp.square(g)
    m_hat = m / (1.0 - ADAM_B1 ** ADAM_STEP)
    v_hat = v / (1.0 - ADAM_B2 ** ADAM_STEP)
    return -ADAM_LR * (m_hat / (jnp.sqrt(v_hat) + ADAM_EPS) + ADAM_WD * w), m, v


def _sum8(r_ref):
    g = r_ref[0].astype(f32)
    for j in range(1, r_ref.shape[0]):
        g = g + r_ref[j].astype(f32)
    return g


def _row_tile(rows, cols, elems=1 << 18):
    tr = min(rows, max(8, 1 << int(math.log2(elems / cols))))
    while rows % tr:
        tr //= 2
    assert tr % 8 == 0 or tr == rows, (rows, cols)
    return tr


def sum_partials(r, name, after=()):
    _, rows, cols = r.shape
    tr = _row_tile(rows, cols)
    na = len(after)

    def body(*refs):
        refs[na + 1][...] = _sum8(refs[na])

    return pl.pallas_call(
        body, name=name, grid=(rows // tr,),
        in_specs=[ANY] * na + [pl.BlockSpec((r.shape[0], tr, cols), lambda i: (0, i, 0))],
        out_specs=pl.BlockSpec((tr, cols), lambda i: (i, 0)), out_shape=_sds((rows, cols), f32),
        compiler_params=_cp(("parallel",)),
    )(*after, r)


def adamw_t(w, m, v, grads, name):
    rows, nl, cols = w.shape
    tr = min(rows, (1 << 16) // cols)
    blk = pl.BlockSpec((tr, nl, cols), lambda i: (i, 0, 0))
    flat = pl.BlockSpec((tr, cols), lambda i: (i, 0))

    def body(w_ref, m_ref, v_ref, *rest):
        g_refs, (g_ref, d_ref, nm_ref, nv_ref) = rest[:nl], rest[nl:]
        for l in range(nl):
            grad = g_refs[l][...]
            g_ref[:, l, :] = grad
            d_ref[:, l, :], nm_ref[:, l, :], nv_ref[:, l, :] = _adamw(w_ref[:, l, :], grad, m_ref[:, l, :], v_ref[:, l, :])

    return tuple(pl.pallas_call(
        body, name=name, grid=(pl.cdiv(rows, tr),), in_specs=[blk] * 3 + [flat] * nl, out_specs=[blk] * 4,
        out_shape=[_sds(w.shape, f32)] * 4, compiler_params=_cp(("parallel",)),
    )(w, m, v, *grads))


def adamw(w, m, v, layer, name, r=None, g=None, prev=None):
    _, rows, cols = w.shape
    tr = _row_tile(rows, cols)
    blk = pl.BlockSpec((None, tr, cols), lambda i: (layer, i, 0))
    nprev = 0 if prev is None else 4

    def body(w_ref, m_ref, v_ref, src_ref, *rest):
        g_ref, d_ref, nm_ref, nv_ref, token_ref = rest[nprev:]
        grad = _sum8(src_ref) if g is None else src_ref[...]
        g_ref[...] = grad
        d_ref[...], nm_ref[...], nv_ref[...] = _adamw(w_ref[...], grad, m_ref[...], v_ref[...])
        token_ref[...] = jnp.zeros_like(token_ref)

    src, src_spec = (r, pl.BlockSpec((r.shape[0], tr, cols), lambda i: (0, i, 0))) if g is None else (g, pl.BlockSpec((tr, cols), lambda i: (i, 0)))
    *outs, token = pl.pallas_call(
        body, name=name, grid=(rows // tr,), in_specs=[blk] * 3 + [src_spec] + [ANY] * nprev,
        out_specs=[blk] * 4 + [pl.BlockSpec((8, HD), lambda i: (0, 0))],
        out_shape=[_sds(w.shape, f32)] * 4 + [_sds((8, HD), f32)], input_output_aliases={4 + k: k for k in range(nprev)},
        compiler_params=_cp(("arbitrary",)),
    )(w, m, v, src, *(prev or ()))
    return tuple(outs), token


def small_adamw(parts, w, m, v, name, after=()):
    def body(*refs):
        p_ref, w_ref, m_ref, v_ref, g_ref, d_ref, nm_ref, nv_ref = refs[len(after):]
        g = _sum8(p_ref)
        g_ref[...] = g
        d_ref[...], nm_ref[...], nv_ref[...] = _adamw(w_ref[...], g, m_ref[...], v_ref[...])

    vmem = pl.BlockSpec(memory_space=pltpu.VMEM)
    return pl.pallas_call(body, name=name, out_shape=[_sds(w.shape, f32)] * 4, in_specs=[ANY] * len(after) + [vmem] * 4,
                          out_specs=[vmem] * 4, compiler_params=_cp())(*after, parts, w, m, v)


def _pack_rows(wt):
    tail = jnp.pad(wt[5888:N_IN], ((OFF_G - 12 - (OFF_P + 512), 0), (0, 0)))
    return jnp.concatenate([wt[512:2816], wt[2816:5120], wt[5120:5888], wt[0:512], tail], axis=0)


def _unpack_rows(g):
    return jnp.concatenate([g[OFF_P:OFF_P + 512], g[OFF_SB:OFF_SB + 2304], g[OFF_GQ:OFF_GQ + 2304], g[OFF_Z:OFF_Z + 768],
                            g[OFF_G - 12:NP]], axis=0)


def _lanes(v):
    flat = v.reshape(-1)
    n = -(-flat.shape[0] // HD) * HD
    return jnp.pad(flat, (0, n - flat.shape[0])).reshape(n // HD, HD)


def _lanes8(v):
    rows = _lanes(v)
    return jnp.pad(rows, ((0, -rows.shape[0] % 8), (0, 0)))


def _layer_fwd(x, p, l):
    nm = lambda s: f"{s}_l{l}"
    u = rms_fwd(x, p["attn_norm"], nm("rms1"))
    proj = matmul(u, p["w_in"][0], name=nm("inproj"), tb=True)
    y_pool = pool_fwd(proj, p["pool_w"], p["pool_scale"], nm("pool"))
    y_sb = sb_fwd(proj, nm("sb"))
    c = conv_fwd(proj, p["conv"], nm("conv"))
    ga = gdn_a_fwd(c, proj, p["alog"], p["dtb"], nm("gdna"))
    o, states = gdn_b_fwd(*ga, nm("gdnb"))
    y_gdn = gdn_out_fwd(o, proj, p["gdn_norm"], nm("gdno"))
    gates = matmul(u, p["w_in"][1], name=nm("inproj_gates"), tb=True, after=(y_gdn,))
    ups = [matmul(y, p[k], name=nm(k)) for y, k in ((y_pool, "w_pool_up"), (y_sb, "w_sb_up"), (y_gdn, "w_gdn_up"))]
    merged = merge_fwd(gates, ups, nm("merge"))
    x1 = matmul(merged, p["w_out"], name=nm("outproj"), epilogue=lambda acc, r: acc + r, extras=(x,))
    u2 = rms_fwd(x1, p["mlp_norm"], nm("rms2"))
    h2 = matmul(u2, p["w_ff1"], name=nm("ff1"), out_dtype=bf16, epilogue=lambda acc: jnp.square(jnp.maximum(acc, 0.0)))
    x2 = matmul(h2, p["w_ff2"], name=nm("ff2"), epilogue=lambda acc, r: acc + r, extras=(x1,))
    saved = dict(x=x, u=u, proj=proj, gates=gates, y_pool=y_pool, y_sb=y_sb, c=c, ga=ga, o=o, states=states, y_gdn=y_gdn, ups=ups,
                 merged=merged, x1=x1, u2=u2, h2=h2)
    return x2, saved


def _layer_bwd(dx2, dx2b, sv, p, l, finish):
    nm = lambda s: f"{s}_l{l}"
    s = dx2.shape[0]
    dh = matmul(dx2b, p["w_ff2"], name=nm("d_ff2_x"), tb=True, out_dtype=bf16,
                epilogue=lambda acc, h2: acc * (2.0 * jnp.sqrt(h2.astype(f32))), extras=(sv["h2"],))
    g_ff2 = matmul(sv["h2"], dx2b, name=nm("d_ff2_w"), ta=True, out_dtype=bf16)
    du2 = matmul(dh, p["w_ff1"], name=nm("d_ff1_x"), tb=True)
    g_ff1 = matmul(sv["u2"], dh, name=nm("d_ff1_w"), ta=True, out_dtype=bf16)
    dx1, dx1b, d_mlp_norm = rms_bwd(sv["x1"], du2, dx2, p["mlp_norm"], nm("d_rms2"))
    dmerged = matmul(dx1b, p["w_out"], name=nm("d_out_x"), tb=True, out_dtype=bf16)
    g_out = matmul(sv["merged"], dx1b, name=nm("d_out_w"), ta=True, out_dtype=bf16)
    dgates, *dups = merge_bwd(sv["gates"], sv["ups"], dmerged, nm("d_merge"))
    dys, g_ups = [], []
    for dup, y, k in zip(dups, (sv["y_pool"], sv["y_sb"], sv["y_gdn"]), ("w_pool_up", "w_sb_up", "w_gdn_up")):
        dys.append(matmul(dup, p[k], name=nm("d_" + k + "_x"), tb=True, out_dtype=bf16))
        g_ups.append(matmul(y, dup, name=nm("d_" + k + "_w"), ta=True, out_dtype=bf16))
    early = g_ups + [g_out, g_ff1, g_ff2]
    do, dz, d_gdn_norm = gdn_out_bwd(sv["o"], sv["proj"], p["gdn_norm"], dys[2], nm("d_gdno"))
    cots = gdn_b_bwd(*sv["ga"], sv["states"], do, nm("d_gdnb"))
    dc, dab, d_alog, d_dtb, swapped = gdn_a_bwd(sv["c"], sv["proj"], p["alog"], p["dtb"], cots[:5], cots[5], nm("d_gdna"),
                                                (early, BIG_AXES[1:]))
    r_early, sent_early = finish(early, swapped, BIG_AXES[1:], BIG_NAMES[1:], "a", (dab,))
    dgq, d_conv = conv_bwd(sv["proj"], p["conv"], dc, nm("d_conv"), sent_early)
    dq, dk, dv = sb_bwd(sv["proj"], dys[1], nm("d_sb"))
    dp, d_pool_w, d_pool_scale = pool_bwd(sv["proj"], p["pool_w"], p["pool_scale"], dys[0], nm("d_pool"))
    dproj = jnp.concatenate([dq, dk, dv, dgq, dz, dp, jnp.zeros((s, OFF_AB - OFF_P - W_POOL), bf16), dab], axis=1)
    g_in = [matmul(dproj, sv["u"], name=nm("d_in_w"), ta=True, out_dtype=bf16),
            matmul(dgates, sv["u"], name=nm("d_in_w_gates"), ta=True, out_dtype=bf16)]
    du, swapped = matmul(dproj, p["w_in"][0], name=nm("d_in_x"), swap=(g_in, (1, 1)))
    r_in, sent_in = finish(g_in, swapped, (1, 1), ("w_in", "w_in_gates"), "b", ())
    du = matmul(dgates, p["w_in"][1], name=nm("d_in_x_gates"), epilogue=lambda acc, r: acc + r, extras=(du,), after=sent_in)
    dx, dxb, d_attn_norm = rms_bwd(sv["x"], du, dx1, p["attn_norm"], nm("d_rms1"))
    recv = r_in + r_early
    small = [d_attn_norm, d_pool_w, d_pool_scale, d_conv, d_alog, d_dtb, d_gdn_norm, d_mlp_norm]
    return dx, dxb, recv, small


BIG_AXES = (1, 1, 1, 1, 0, 1, 0)
GATHER_ID, EXCHANGE_ID = 1, 2
BIG_NAMES = ("w_in", "w_pool_up", "w_sb_up", "w_gdn_up", "w_out", "w_ff1", "w_ff2")


def kernel(x, attn_norm, w_in, pool_w, pool_scale, gdn_conv, gdn_a_log, gdn_dt_bias, gdn_norm, w_pool_up, w_sb_up, w_gdn_up, w_out, mlp_norm, w_ff1, w_ff2, final_norm, loss_target, m_attn_norm, m_w_in, m_pool_w, m_pool_scale, m_gdn_conv, m_gdn_a_log, m_gdn_dt_bias, m_gdn_norm, m_w_pool_up, m_w_sb_up, m_w_gdn_up, m_w_out, m_mlp_norm, m_w_ff1, m_w_ff2, m_final_norm, v_attn_norm, v_w_in, v_pool_w, v_pool_scale, v_gdn_conv, v_gdn_a_log, v_gdn_dt_bias, v_gdn_norm, v_w_pool_up, v_w_sb_up, v_w_gdn_up, v_w_out, v_mlp_norm, v_w_ff1, v_w_ff2, v_final_norm):
    s = x.shape[1]
    me = _dev_index((lax.axis_index("x"), lax.axis_index("y"), lax.axis_index("c")))
    ncv = gdn_conv.shape[2]

    w_in_t, m_w_in_t, v_w_in_t = (jnp.transpose(a, (2, 0, 1)) for a in (w_in, m_w_in, v_w_in))
    full, prev = [], None
    for l in range(NL):
        packed_in = _pack_rows(w_in_t[:, l]).astype(bf16)
        groups = [[packed_in[:OFF_G]] + ([gdn_conv.reshape(NL * 4, ncv)] if l == 0 else []),
                  [packed_in[OFF_G:]],
                  [w_pool_up[l].astype(bf16), w_sb_up[l].astype(bf16), w_gdn_up[l].astype(bf16), w_out[l].astype(bf16)],
                  [w_ff1[l].astype(bf16), w_ff2[l].astype(bf16)]]
        axes = [(1, 0)[:len(groups[0])], (1,), BIG_AXES[1:5], BIG_AXES[5:]]
        got = []
        for k, (group, ax) in enumerate(zip(groups, axes)):
            if prev is not None:
                group, _ = lax.optimization_barrier((group, prev))
            prev = group[0]
            got.append(all_gather(group, ax, f"gather_{k}_l{l}", sequencer_id=GATHER_ID))
        full.append([(got[0][0], got[1][0])] + got[2] + got[3])
        if l == 0:
            conv_full = got[0][1].reshape(NDEV, NL, 4, ncv).transpose(1, 2, 0, 3).reshape(NL, 4, NDEV * ncv)
    params = []
    for l in range(NL):
        p = dict(zip(("w_in", "w_pool_up", "w_sb_up", "w_gdn_up", "w_out", "w_ff1", "w_ff2"), full[l][:7]))
        p.update(attn_norm=attn_norm[l][None], mlp_norm=mlp_norm[l][None], pool_w=pool_w[l], pool_scale=pool_scale[l][None],
                 conv=conv_full[l], alog=_lanes(gdn_a_log[l]), dtb=_lanes(gdn_dt_bias[l]), gdn_norm=gdn_norm[l][None])
        params.append(p)

    h = x[0]
    saved = []
    for l in range(NL):
        h, sv = _layer_fwd(h, params[l], l)
        saved.append(sv)
    dh, dhb, loss_row, d_final = loss_head(h, loss_target[0], final_norm[None], "loss_head")
    recv, smalls = [None] * NL, [None] * NL
    for l in reversed(range(NL)):
        def finish(gs, ps, axes, names, tag, after, l=l):
            qs = [pair_add(g, p, ax, f"pair_add_{k}_l{l}", after) for g, p, ax, k in zip(gs, ps, axes, names)]
            return chip_exchange(qs, f"exchange_{tag}_l{l}", EXCHANGE_ID), tuple(qs)

        dh, dhb, recv[l], smalls[l] = _layer_bwd(dh, dhb, saved[l], params[l], l, finish)

    small_rows = [_lanes8(t) for l in range(NL) for t in smalls[l]] + [_lanes8(d_final), _lanes8(loss_row)]
    packed = jnp.concatenate(small_rows, axis=0)
    parts = all_gather([packed], (0,), "gather_small", sequencer_id=GATHER_ID)[0].reshape(NDEV, packed.shape[0], HD)

    def pack_small(tree):
        rows = []
        for l in range(NL):
            rows += [_lanes8(tree["attn_norm"][l]), _lanes8(tree["pool_w"][l]), _lanes8(tree["pool_scale"][l]),
                     jnp.zeros((4 * NDEV * ncv // HD, HD), f32), _lanes8(tree["gdn_a_log"][l]), _lanes8(tree["gdn_dt_bias"][l]),
                     _lanes8(tree["gdn_norm"][l]), _lanes8(tree["mlp_norm"][l])]
        rows += [_lanes8(tree["final_norm"]), jnp.zeros((8, HD), f32)]
        return jnp.concatenate(rows, axis=0)

    names = ("attn_norm", "pool_w", "pool_scale", "gdn_a_log", "gdn_dt_bias", "gdn_norm", "mlp_norm", "final_norm")
    w_small = pack_small(dict(zip(names, (attn_norm, pool_w, pool_scale, gdn_a_log, gdn_dt_bias, gdn_norm, mlp_norm, final_norm))))
    m_small = pack_small(dict(zip(names, (m_attn_norm, m_pool_w, m_pool_scale, m_gdn_a_log, m_gdn_dt_bias, m_gdn_norm, m_mlp_norm, m_final_norm))))
    v_small = pack_small(dict(zip(names, (v_attn_norm, v_pool_w, v_pool_scale, v_gdn_a_log, v_gdn_dt_bias, v_gdn_norm, v_mlp_norm, v_final_norm))))
    def unpack_small(buf):
        out, conv_g, r = {}, [], 0
        layer_items = (("attn_norm", (D,)), ("pool_w", (4, HD, HD)), ("pool_scale", (W_POOL,)), ("conv", (4, NDEV * ncv)),
                       ("gdn_a_log", (NH,)), ("gdn_dt_bias", (NH,)), ("gdn_norm", (HD,)), ("mlp_norm", (D,)))
        per_layer = {k: [] for k, _ in layer_items}
        for l in range(NL):
            for k, shape in layer_items:
                size = math.prod(shape)
                nrow = -(-size // (8 * HD)) * 8
                per_layer[k].append(buf[r:r + nrow].reshape(-1)[:size].reshape(shape))
                r += nrow
        for k, _ in layer_items:
            out[k] = jnp.stack(per_layer[k])
        out["final_norm"] = buf[r:r + D // HD].reshape(D)
        out["loss"] = buf[r + D // HD, 0]
        return out

    big_out = {}
    big_names = BIG_NAMES
    big_w = dict(zip(big_names, ((w_in_t, m_w_in_t, v_w_in_t), (w_pool_up, m_w_pool_up, v_w_pool_up), (w_sb_up, m_w_sb_up, v_w_sb_up),
                                 (w_gdn_up, m_w_gdn_up, v_w_gdn_up), (w_out, m_w_out, v_w_out), (w_ff1, m_w_ff1, v_w_ff1),
                                 (w_ff2, m_w_ff2, v_w_ff2))))
    tokens = []
    for l, k in [(l, k) for l in reversed(range(NL)) for k in big_names[1:]]:
        w, m, v = big_w[k]
        big_out[k], token = adamw(w, m, v, l, f"adamw_{k}_l{l}", r=recv[l][big_names.index(k) + 1], prev=big_out.get(k))
        tokens.append(token)
    g_in = []
    for l in range(NL):
        halves = [sum_partials(recv[l][h], f"sum_w_in_{h}_l{l}", tuple(tokens) if l == 0 else ()) for h in range(2)]
        g_in.append(_unpack_rows(jnp.concatenate(halves, axis=0)))
    big_out["w_in"] = adamw_t(*big_w["w_in"], g_in, "adamw_w_in")

    small_out = small_adamw(parts, w_small, m_small, v_small, "adamw_small", after=(big_out["w_in"][1],))
    sm = [unpack_small(b) for b in small_out]
    loss = sm[0]["loss"]
    g_conv = lax.dynamic_slice_in_dim(sm[0]["conv"], me * ncv, ncv, axis=2)
    conv_out = None
    for l in reversed(range(NL)):
        conv_out, _ = adamw(gdn_conv, m_gdn_conv, v_gdn_conv, l, f"adamw_conv_l{l}", g=g_conv[l], prev=conv_out)

    def leaf(i, k):
        if k == "w_in":
            return jnp.transpose(big_out[k][i], (1, 2, 0))
        if k in big_out:
            return big_out[k][i]
        if k == "gdn_conv":
            return conv_out[i]
        return sm[i][k]

    order = ("attn_norm", "w_in", "pool_w", "pool_scale", "gdn_conv", "gdn_a_log", "gdn_dt_bias", "gdn_norm", "w_pool_up",
             "w_sb_up", "w_gdn_up", "w_out", "mlp_norm", "w_ff1", "w_ff2", "final_norm")
    return (loss, dh[None]) + tuple(leaf(i, k) for i in range(4) for k in order)
```

```python
import functools
import math

import jax
import jax.numpy as jnp
from jax import lax
from jax.experimental import pallas as pl
from jax.experimental.pallas import tpu as pltpu
from jax.experimental.pallas import tpu_sc as plsc

f32, bf16 = jnp.float32, jnp.bfloat16

D = 2048
NDEV = 8
NL = 2
HD = 128
NH = 6
WH = NH * HD
W_POOL = 512
EPS = 1e-6
N_IN = 12044
NP = 12288
OFF_SB, OFF_GQ, OFF_Z, OFF_P, OFF_AB, OFF_G = 0, 2304, 4608, 5376, 6016, 6144
AB_LANE = HD - 2 * NH
POOL_WINDOWS = (2, 4, 8, 16)
CH = 128
TQ = 256
VMEM_LIMIT = 56 * 1024 * 1024
ADAM_LR, ADAM_B1, ADAM_B2, ADAM_EPS, ADAM_WD, ADAM_STEP = 0.001, 0.9, 0.999, 1e-08, 0.01, 10
MESH = pl.DeviceIdType.MESH


def _cp(sem=None):
    return pltpu.CompilerParams(dimension_semantics=sem, vmem_limit_bytes=VMEM_LIMIT)


def _sds(shape, dtype):
    return jax.ShapeDtypeStruct(tuple(shape), dtype)


def _swap_copies(g_refs, p_refs, send, recv, axes):
    x, y, core = lax.axis_index("x"), lax.axis_index("y"), lax.axis_index("c")
    return [pltpu.make_async_remote_copy(
        src_ref=_shard_of(g, ax, p.shape[1 + ax], 2 * j + (1 - core)), dst_ref=p.at[j], send_sem=send.at[t, j],
        recv_sem=recv.at[t, j], device_id=(x, y, 1 - core), device_id_type=MESH)
        for t, (g, p, ax) in enumerate(zip(g_refs, p_refs, axes)) for j in range(4)]


def matmul(a, b, *, name, ta=False, tb=False, out_dtype=f32, tm=1024, tn=1024, tk=2048, epilogue=None, extras=(), after=(),
           swap=()):
    m, k = (a.shape[1], a.shape[0]) if ta else a.shape
    n = b.shape[0] if tb else b.shape[1]
    assert k == (b.shape[1] if tb else b.shape[0]) and a.dtype == bf16 and b.dtype == bf16
    tm, tn, tk = min(tm, m), min(tn, n), min(tk, k)
    assert m % tm == 0 and n % tn == 0 and k % tk == 0, (m, n, k, tm, tn, tk)
    nk = k // tk
    a_spec = pl.BlockSpec((tk, tm), lambda i, j, q: (q, i)) if ta else pl.BlockSpec((tm, tk), lambda i, j, q: (i, q))
    b_spec = pl.BlockSpec((tn, tk), lambda i, j, q: (j, q)) if tb else pl.BlockSpec((tk, tn), lambda i, j, q: (q, j))
    e_specs = [pl.BlockSpec((tm, tn), lambda i, j, q: (i, j)) for _ in extras]
    dn = (((0 if ta else 1,), (1 if tb else 0,)), ((), ()))
    ne = len(extras)
    gs, axes = swap if swap else ((), ())
    ns = len(gs)
    shards = [tuple(d // (NDEV if ax_ == ax else 1) for ax_, d in enumerate(g.shape)) for g, ax in zip(gs, axes)]
    grid = (m // tm, n // tn, nk)

    def body(*refs):
        g_refs = refs[len(after):len(after) + ns]
        a_ref, b_ref, *rest = refs[len(after) + ns:]
        e_refs, o_ref, p_refs = rest[:ne], rest[ne], rest[ne + 1:ne + 1 + ns]
        if ns:
            at = [pl.program_id(d) for d in range(3)]
            first = functools.reduce(jnp.logical_and, [c == 0 for c in at])
            last = functools.reduce(jnp.logical_and, [c == g - 1 for c, g in zip(at, grid)])

            @pl.when(first)
            def _():
                for cp in _swap_copies(g_refs, p_refs, *rest[-2:], axes):
                    cp.start()

        part = lax.dot_general(a_ref[...], b_ref[...], dn, preferred_element_type=f32)

        def finish(acc):
            if epilogue is not None:
                acc = epilogue(acc, *[e[...] for e in e_refs])
            o_ref[...] = acc.astype(out_dtype)

        if nk == 1:
            finish(part)
        else:
            acc_ref = rest[ne + 1 + ns]
            q = pl.program_id(2)

            @pl.when(q == 0)
            def _():
                acc_ref[...] = part

            @pl.when(jnp.logical_and(q > 0, q < nk - 1))
            def _():
                acc_ref[...] += part

            @pl.when(q == nk - 1)
            def _():
                finish(acc_ref[...] + part)

        if ns:
            @pl.when(last)
            def _():
                cps = _swap_copies(g_refs, p_refs, *rest[-2:], axes)
                for cp in cps:
                    cp.wait_recv()
                for cp in cps:
                    cp.wait_send()

    hbm = pl.BlockSpec(memory_space=pl.ANY)
    res = pl.pallas_call(
        body, name=name, grid=grid,
        in_specs=[hbm] * (len(after) + ns) + [a_spec, b_spec] + e_specs,
        out_specs=[pl.BlockSpec((tm, tn), lambda i, j, q: (i, j))] + [hbm] * ns,
        out_shape=[_sds((m, n), out_dtype)] + [_sds((4,) + sh, g.dtype) for sh, g in zip(shards, gs)],
        scratch_shapes=([pltpu.VMEM((tm, tn), f32)] if nk > 1 else [])
        + ([pltpu.SemaphoreType.DMA((ns, 4)), pltpu.SemaphoreType.DMA((ns, 4))] if ns else []),
        compiler_params=_cp(("arbitrary",) * 3 if ns else ("parallel", "parallel", "arbitrary")),
    )(*after, *gs, a, b, *extras)
    return (res[0], list(res[1:])) if ns else res[0]


def rowwise(name, fn, rows, params, outs, sums=(), tr=256, after=()):
    s = rows[0][0].shape[0]
    tr = min(tr, s)
    nin, nout = len(rows) + len(params), len(outs)
    in_specs = [pl.BlockSpec((tr, w), functools.partial(lambda i, c: (i, c), c=c)) for (_, w, c) in rows]
    in_specs += [pl.BlockSpec(p.shape, lambda i: (0, 0)) for p in params]
    out_specs = [pl.BlockSpec((tr, w), lambda i: (i, 0)) for (w, _) in outs]
    out_specs += [pl.BlockSpec(sh, lambda i: (0, 0)) for sh in sums]
    out_shape = [_sds((s, w), dt) for (w, dt) in outs] + [_sds(sh, f32) for sh in sums]

    def body(*refs):
        refs = refs[len(after):]
        res = fn(*[r[...] for r in refs[:nin]])
        for r, v in zip(refs[nin:nin + nout], res[:nout]):
            r[...] = v.astype(r.dtype)
        i = pl.program_id(0)
        for r, v in zip(refs[nin + nout:], res[nout:]):
            @pl.when(i == 0)
            def _(r=r, v=v):
                r[...] = v

            @pl.when(i > 0)
            def _(r=r, v=v):
                r[...] += v

    res = pl.pallas_call(
        body, name=name, grid=(s // tr,), in_specs=[pl.BlockSpec(memory_space=pl.ANY)] * len(after) + in_specs,
        out_specs=out_specs, out_shape=out_shape, compiler_params=_cp(("arbitrary",)),
    )(*after, *[r[0] for r in rows], *params)
    return res


def _rms(x, g):
    return x * lax.rsqrt(jnp.mean(x * x, axis=-1, keepdims=True) + EPS) * g


def rms_fwd(x, g, name):
    return rowwise(name, lambda xb, gb: (_rms(xb, gb),), [(x, D, 0)], [g], [(D, bf16)])[0]


def rms_bwd(x, du, dres, g, name, after=()):
    def fn(xb, dub, drb, gb):
        _, vjp = jax.vjp(_rms, xb, gb)
        dx, dg = vjp(dub.astype(f32))
        return drb + dx, drb + dx, dg

    return rowwise(name, fn, [(x, D, 0), (du, D, 0), (dres, D, 0)], [g], [(D, f32), (D, bf16)], [(1, D)], after=after)


def _merge(gates, up_p, up_s, up_g):
    sg = jax.nn.sigmoid(gates)
    return sg[:, :D] * up_p + sg[:, D:2 * D] * up_s + sg[:, 2 * D:] * up_g


def merge_fwd(gates, ups, name):
    return rowwise(name, lambda *blocks: (_merge(*[b.astype(f32) for b in blocks]),),
                   [(gates, 3 * D, 0)] + [(u, D, 0) for u in ups], [], [(D, bf16)], tr=128)[0]


def merge_bwd(gates, ups, dmerged, name):
    def fn(g, a, b, c, dm):
        _, vjp = jax.vjp(_merge, *[t.astype(f32) for t in (g, a, b, c)])
        return vjp(dm.astype(f32))

    return rowwise(name, fn, [(gates, 3 * D, 0)] + [(u, D, 0) for u in ups] + [(dmerged, D, 0)], [],
                   [(3 * D, bf16), (D, bf16), (D, bf16), (D, bf16)], tr=128)


def _gdn_out(o, z, g):
    ys = []
    for h in range(NH):
        sl = slice(h * HD, (h + 1) * HD)
        ys.append(_rms(o[:, sl], g) * jax.nn.silu(z[:, sl]))
    return jnp.concatenate(ys, axis=1)


def gdn_out_fwd(o, proj, g, name):
    return rowwise(name, lambda ob, zb, gb: (_gdn_out(ob, zb, gb),), [(o, WH, 0), (proj, WH, OFF_Z // WH)], [g],
                   [(WH, bf16)])[0]


def gdn_out_bwd(o, proj, g, dy, name):
    def fn(ob, zb, dyb, gb):
        _, vjp = jax.vjp(_gdn_out, ob, zb, gb)
        return vjp(dyb.astype(f32))

    return rowwise(name, fn, [(o, WH, 0), (proj, WH, OFF_Z // WH), (dy, WH, 0)], [g], [(WH, f32), (WH, bf16)],
                   [(1, HD)])


def loss_head(x, target, g, name):
    def loss_fn(xb, gb, tb):
        err = _rms(xb, gb) - tb
        return (0.5 / D) * jnp.sum(jnp.sum(err * err, axis=1, keepdims=True), axis=0, keepdims=True)

    def fn(xb, tb, gb):
        val, vjp = jax.vjp(functools.partial(loss_fn, tb=tb), xb, gb)
        dx, dg = vjp(jnp.ones((1, 1), f32))
        return dx, dx, jnp.broadcast_to(val, (1, HD)), dg

    return rowwise(name, fn, [(x, D, 0), (target, D, 0)], [g], [(D, f32), (D, bf16)], [(1, HD), (1, D)])


PB = 256


def _split(v):
    hi = v.astype(bf16)
    return hi, (v - hi.astype(f32)).astype(bf16)


def _band_dot(make_band, v, s, forward):
    hi, lo = _split(v)
    nb = s // PB
    outs = []
    for r in range(nb):
        lo_r = max(r - 1, 0) if forward else r
        hi_r = r + 1 if forward else min(r + 2, nb)
        band = make_band(r * PB, lo_r * PB, (hi_r - lo_r) * PB)
        sl = slice(lo_r * PB, hi_r * PB)
        outs.append(jnp.dot(band, hi[sl], preferred_element_type=f32) + jnp.dot(band, lo[sl], preferred_element_type=f32))
    return jnp.concatenate(outs, axis=0)


def _pool_common(p, win, s):
    def band(row0, col0, ncol):
        t = row0 + lax.broadcasted_iota(jnp.int32, (PB, ncol), 0)
        u = col0 + lax.broadcasted_iota(jnp.int32, (PB, ncol), 1)
        return jnp.logical_and(u <= t, t < u + win).astype(bf16)

    def band_t(row0, col0, ncol):
        u = row0 + lax.broadcasted_iota(jnp.int32, (PB, ncol), 0)
        t = col0 + lax.broadcasted_iota(jnp.int32, (PB, ncol), 1)
        return jnp.logical_and(u <= t, t < u + win).astype(bf16)

    t = lax.broadcasted_iota(jnp.int32, (s, 1), 0)
    inv_n = 1.0 / jnp.minimum(t + 1, win).astype(f32)
    d = _band_dot(band, p, s, True) * inv_n - p
    return d, inv_n, band_t


def pool_fwd(proj, pool_w, pool_scale, name):
    s = proj.shape[0]

    def body(p_ref, w_ref, sc_ref, y_ref):
        win = jnp.left_shift(2, pl.program_id(0))
        d, _, _ = _pool_common(p_ref[...], win, s)
        y = jnp.dot(d.astype(bf16), w_ref[...].astype(bf16), preferred_element_type=f32) * sc_ref[...]
        y_ref[...] = y.astype(bf16)

    return pl.pallas_call(
        body, name=name, grid=(4,),
        in_specs=[pl.BlockSpec((s, HD), lambda g: (0, OFF_P // HD + g)), pl.BlockSpec((None, HD, HD), lambda g: (g, 0, 0)),
                  pl.BlockSpec((1, HD), lambda g: (0, g))],
        out_specs=pl.BlockSpec((s, HD), lambda g: (0, g)), out_shape=_sds((s, W_POOL), bf16),
        compiler_params=_cp(("arbitrary",)),
    )(proj, pool_w, pool_scale)


def pool_bwd(proj, pool_w, pool_scale, dy, name):
    s = proj.shape[0]

    def body(p_ref, w_ref, sc_ref, dy_ref, dp_ref, dw_ref, dsc_ref):
        win = jnp.left_shift(2, pl.program_id(0))
        d, inv_n, band_t = _pool_common(p_ref[...], win, s)
        w = w_ref[...].astype(bf16)
        dyf = dy_ref[...].astype(f32)
        dsc_ref[...] = jnp.sum(dyf * jnp.dot(d.astype(bf16), w, preferred_element_type=f32), axis=0, keepdims=True)
        dys = (dyf * sc_ref[...]).astype(bf16)
        dd = lax.dot_general(dys, w, (((1,), (1,)), ((), ())), preferred_element_type=f32)
        dw_ref[...] = lax.dot_general(d.astype(bf16), dys, (((0,), (0,)), ((), ())), preferred_element_type=f32)
        dp_ref[...] = (_band_dot(band_t, dd * inv_n, s, False) - dd).astype(bf16)

    return pl.pallas_call(
        body, name=name, grid=(4,),
        in_specs=[pl.BlockSpec((s, HD), lambda g: (0, OFF_P // HD + g)), pl.BlockSpec((None, HD, HD), lambda g: (g, 0, 0)),
                  pl.BlockSpec((1, HD), lambda g: (0, g)), pl.BlockSpec((s, HD), lambda g: (0, g))],
        out_specs=[pl.BlockSpec((s, HD), lambda g: (0, g)), pl.BlockSpec((None, HD, HD), lambda g: (g, 0, 0)),
                   pl.BlockSpec((1, HD), lambda g: (0, g))],
        out_shape=[_sds((s, W_POOL), bf16), _sds((4, HD, HD), f32), _sds((1, W_POOL), f32)],
        compiler_params=_cp(("arbitrary",)),
    )(proj, pool_w, pool_scale, dy)


HB = 3
WB = HB * HD


def _heads_of(v):
    return _stack([v[:, h * HD:(h + 1) * HD] for h in range(HB)])


def _bd(a, b, dn):
    return lax.dot_general(a, b, dn, preferred_element_type=f32)


def _run_sum(v, tri):
    hi, lo = _split(v.reshape(HB * TQ, TQ))
    return (jnp.dot(hi, tri, preferred_element_type=f32) + jnp.dot(lo, tri, preferred_element_type=f32)).reshape(HB, TQ, TQ)


def _tri(later):
    j = lax.broadcasted_iota(jnp.int32, (TQ, TQ), 0)
    u = lax.broadcasted_iota(jnp.int32, (TQ, TQ), 1)
    return (j > u if later else j < u).astype(bf16)


def _sb_tile(q, k_ref, kb, carry, diagonal):
    k = _heads_of(k_ref[pl.ds(pl.multiple_of(kb * TQ, TQ), TQ), :].astype(bf16))
    z = _bd(q, k, BNT)
    ls = jax.nn.log_sigmoid(-z)
    if diagonal:
        mask = lax.broadcasted_iota(jnp.int32, (TQ, TQ), 1) < lax.broadcasted_iota(jnp.int32, (TQ, TQ), 0)
        ls = jnp.where(mask, ls, 0.0)
    a = jnp.exp(ls + z + _run_sum(ls, _tri(True)) + carry)
    return z, ls, jnp.where(mask, a, 0.0) if diagonal else a


def sb_fwd(proj, name):
    s = proj.shape[0]
    nq = s // TQ
    scale = HD ** -0.5

    def body(q_ref, k_ref, v_ref, y_ref):
        qi = pl.program_id(1)
        q = _heads_of((q_ref[...] * scale).astype(bf16))

        def step(j, c, diagonal=False):
            acc, carry = c
            kb = qi - j
            _, ls, a = _sb_tile(q, k_ref, kb, carry, diagonal)
            v = _heads_of(v_ref[pl.ds(pl.multiple_of(kb * TQ, TQ), TQ), :].astype(bf16))
            return acc + _bd(a.astype(bf16), v, BNN), carry + jnp.sum(ls, axis=2, keepdims=True)

        first = step(0, (jnp.zeros((HB, TQ, HD), f32), jnp.zeros((HB, TQ, 1), f32)), True)
        acc, _ = lax.fori_loop(1, qi + 1, step, first)
        for h in range(HB):
            y_ref[:, h * HD:(h + 1) * HD] = acc[h].astype(bf16)

    c0, ng = OFF_SB // WB, NH // HB
    return pl.pallas_call(
        body, name=name, grid=(ng, nq),
        in_specs=[pl.BlockSpec((TQ, WB), lambda g, i: (i, c0 + g)), pl.BlockSpec((s, WB), lambda g, i: (0, c0 + ng + g)),
                  pl.BlockSpec((s, WB), lambda g, i: (0, c0 + 2 * ng + g))],
        out_specs=pl.BlockSpec((TQ, WB), lambda g, i: (i, g)), out_shape=_sds((s, WH), bf16),
        compiler_params=_cp(("arbitrary", "arbitrary")),
    )(proj, proj, proj)


def sb_bwd(proj, dy, name):
    s = proj.shape[0]
    nq = s // TQ
    scale = HD ** -0.5

    def body(q_ref, k_ref, v_ref, do_ref, dq_ref, dk_ref, dv_ref, e_scr, z_scr, dk_acc, dv_acc):
        qi = pl.program_id(1)
        q = _heads_of((q_ref[...] * scale).astype(bf16))
        do = _heads_of(do_ref[...])

        @pl.when(qi == 0)
        def _():
            dk_acc[...] = jnp.zeros_like(dk_acc)
            dv_acc[...] = jnp.zeros_like(dv_acc)

        def add_heads(acc_ref, rows, upd):
            for h in range(HB):
                acc_ref[rows, h * HD:(h + 1) * HD] += upd[h]

        def sweep_back(j, carry, diagonal=False):
            kb = qi - j
            rows = pl.ds(pl.multiple_of(kb * TQ, TQ), TQ)
            z, ls, a = _sb_tile(q, k_ref, kb, carry, diagonal)
            v = _heads_of(v_ref[rows, :].astype(bf16))
            e_scr[kb] = _bd(do, v, BNT) * a
            z_scr[kb] = z
            add_heads(dv_acc, rows, _bd(a.astype(bf16), do, BTN))
            return carry + jnp.sum(ls, axis=2, keepdims=True)

        lax.fori_loop(1, qi + 1, sweep_back, sweep_back(0, jnp.zeros((HB, TQ, 1), f32), True))

        def sweep_fwd(kb, c, diagonal=False):
            dq, carry = c
            rows = pl.ds(pl.multiple_of(kb * TQ, TQ), TQ)
            e, z = e_scr[kb], z_scr[kb]
            sig = jax.nn.sigmoid(z)
            dz = e * (1.0 - sig) - (_run_sum(e, _tri(False)) + carry) * sig
            if diagonal:
                dz = jnp.where(lax.broadcasted_iota(jnp.int32, (TQ, TQ), 1) < lax.broadcasted_iota(jnp.int32, (TQ, TQ), 0), dz, 0.0)
            dz = dz.astype(bf16)
            k = _heads_of(k_ref[rows, :].astype(bf16))
            add_heads(dk_acc, rows, _bd(dz, q, BTN))
            return dq + _bd(dz, k, BNN), carry + jnp.sum(e, axis=2, keepdims=True)

        dq, _ = sweep_fwd(qi, lax.fori_loop(0, qi, sweep_fwd, (jnp.zeros((HB, TQ, HD), f32), jnp.zeros((HB, TQ, 1), f32))), True)
        for h in range(HB):
            dq_ref[:, h * HD:(h + 1) * HD] = (dq[h] * scale).astype(bf16)

        @pl.when(qi == nq - 1)
        def _():
            dk_ref[...] = dk_acc[...].astype(bf16)
            dv_ref[...] = dv_acc[...].astype(bf16)

    c0, ng = OFF_SB // WB, NH // HB
    return pl.pallas_call(
        body, name=name, grid=(ng, nq),
        in_specs=[pl.BlockSpec((TQ, WB), lambda g, i: (i, c0 + g)), pl.BlockSpec((s, WB), lambda g, i: (0, c0 + ng + g)),
                  pl.BlockSpec((s, WB), lambda g, i: (0, c0 + 2 * ng + g)), pl.BlockSpec((TQ, WB), lambda g, i: (i, g))],
        out_specs=[pl.BlockSpec((TQ, WB), lambda g, i: (i, g)), pl.BlockSpec((s, WB), lambda g, i: (0, g)),
                   pl.BlockSpec((s, WB), lambda g, i: (0, g))],
        out_shape=[_sds((s, WH), bf16)] * 3,
        scratch_shapes=[pltpu.VMEM((nq, HB, TQ, TQ), f32), pltpu.VMEM((nq, HB, TQ, TQ), f32), pltpu.VMEM((s, WB), f32),
                        pltpu.VMEM((s, WB), f32)],
        compiler_params=_cp(("arbitrary", "arbitrary")),
    )(proj, proj, proj, dy)


CB = 256


def _shift_down(v, k, s):
    if k == 0:
        return v
    row = lax.broadcasted_iota(jnp.int32, v.shape, 0)
    return jnp.where(row < k, 0.0, pltpu.roll(v, k, axis=0))


def _shift_up(v, k, s):
    if k == 0:
        return v
    row = lax.broadcasted_iota(jnp.int32, v.shape, 0)
    return jnp.where(row >= s - k, 0.0, pltpu.roll(v, s - k, axis=0))


def conv_fwd(proj, w, name):
    s = proj.shape[0]

    def body(x_ref, w_ref, y_ref):
        x, wv = x_ref[...], w_ref[...]
        y = sum(wv[3 - k:4 - k, :] * _shift_down(x, k, s) for k in range(4))
        y_ref[...] = jax.nn.silu(y)

    return pl.pallas_call(
        body, name=name, grid=(3 * WH // CB,),
        in_specs=[pl.BlockSpec((s, CB), lambda j: (0, OFF_GQ // CB + j)), pl.BlockSpec((4, CB), lambda j: (0, j))],
        out_specs=pl.BlockSpec((s, CB), lambda j: (0, j)), out_shape=_sds((s, 3 * WH), f32),
        compiler_params=_cp(("parallel",)),
    )(proj, w)


def conv_bwd(proj, w, dc, name, after=()):
    s = proj.shape[0]

    def body(*refs):
        x_ref, w_ref, dc_ref, dx_ref, dw_ref = refs[len(after):]
        x, wv = x_ref[...], w_ref[...]
        xs = [_shift_down(x, k, s) for k in range(4)]
        y = sum(wv[3 - k:4 - k, :] * xs[k] for k in range(4))
        sig = jax.nn.sigmoid(y)
        dy = dc_ref[...] * (sig * (1.0 + y * (1.0 - sig)))
        dx_ref[...] = sum(wv[3 - k:4 - k, :] * _shift_up(dy, k, s) for k in range(4)).astype(bf16)
        dw_ref[...] = jnp.concatenate([jnp.sum(dy * xs[3 - i], axis=0, keepdims=True) for i in range(4)], axis=0)

    return pl.pallas_call(
        body, name=name, grid=(3 * WH // CB,),
        in_specs=[pl.BlockSpec(memory_space=pl.ANY)] * len(after)
        + [pl.BlockSpec((s, CB), lambda j: (0, OFF_GQ // CB + j)), pl.BlockSpec((4, CB), lambda j: (0, j)),
           pl.BlockSpec((s, CB), lambda j: (0, j))],
        out_specs=[pl.BlockSpec((s, CB), lambda j: (0, j)), pl.BlockSpec((4, CB), lambda j: (0, j))],
        out_shape=[_sds((s, 3 * WH), bf16), _sds((4, 3 * WH), f32)],
        compiler_params=_cp(("parallel",)),
    )(*after, proj, w, dc)


SOLVE_PASSES = 3


def _pdot_impl(a, b, dn, passes):
    ah, al = _split(a)
    bh, bl = _split(b)
    dot = lambda p, q: lax.dot_general(p, q, dn, preferred_element_type=f32)
    if passes == 1:
        return dot(ah, bh)
    if passes == 2:
        return dot(ah, bh) + dot(ah, bl)
    return dot(ah, bh) + (dot(ah, bl) + dot(al, bh))


BNN, BNT, BTN = (((2,), (1,)), ((0,), (0,))), (((2,), (2,)), ((0,), (0,))), (((1,), (1,)), ((0,), (0,)))


@functools.partial(jax.custom_vjp, nondiff_argnums=(2,))
def _bdot(a, b, passes):
    return _pdot_impl(a, b, BNN, passes)


def _bdot_fwd(a, b, passes):
    return _pdot_impl(a, b, BNN, passes), (a, b)


def _bdot_bwd(passes, res, ct):
    a, b = res
    return _pdot_impl(ct, b, BNT, passes), _pdot_impl(a, ct, BTN, passes)


_bdot.defvjp(_bdot_fwd, _bdot_bwd)


@functools.partial(jax.custom_vjp, nondiff_argnums=(2,))
def _bdot_nt(a, b, passes):
    return _pdot_impl(a, b, BNT, passes)


def _bdot_nt_fwd(a, b, passes):
    return _pdot_impl(a, b, BNT, passes), (a, b)


def _bdot_nt_bwd(passes, res, ct):
    a, b = res
    return _pdot_impl(ct, b, BNN, passes), _pdot_impl(ct, a, BTN, passes)


_bdot_nt.defvjp(_bdot_nt_fwd, _bdot_nt_bwd)


def _lane_pick(v, h):
    lane = lax.broadcasted_iota(jnp.int32, v.shape, v.ndim - 1)
    return jnp.sum(jnp.where(lane == h, v, 0.0), axis=-1, keepdims=True)


def _stack(parts):
    return jnp.concatenate([p[None] for p in parts], axis=0)


def _heads(v, first):
    return _stack([_lane_pick(v, first + h) for h in range(NH)])


def _l2n(v):
    return v * lax.rsqrt(jnp.sum(v * v, axis=-1, keepdims=True) + EPS)


def _dot_nt(a, b):
    return lax.dot_general(a, b, (((1,), (1,)), ((), ())), preferred_element_type=f32)


def _inverse_impl(lower):
    i = lax.broadcasted_iota(jnp.int32, (CH, CH), 0)
    j = lax.broadcasted_iota(jnp.int32, (CH, CH), 1)
    inv = (i == j).astype(f32) - lower
    pw = _pdot_impl(lower, lower, BNN, SOLVE_PASSES)
    for m in range(1, int(math.log2(CH))):
        inv = inv + _pdot_impl(inv, pw, BNN, SOLVE_PASSES)
        if m < int(math.log2(CH)) - 1:
            pw = _pdot_impl(pw, pw, BNN, SOLVE_PASSES)
    return inv


@jax.custom_vjp
def _unit_lower_inverse(lower):
    return _inverse_impl(lower)


def _unit_lower_inverse_fwd(lower):
    inv = _inverse_impl(lower)
    return inv, inv


def _unit_lower_inverse_bwd(inv, ct):
    return (-_pdot_impl(_pdot_impl(inv, ct, BTN, SOLVE_PASSES), inv, BNT, SOLVE_PASSES),)


_unit_lower_inverse.defvjp(_unit_lower_inverse_fwd, _unit_lower_inverse_bwd)


def _gdn_chunk(cq, ck, cv, ab, alog, dtb):
    ones = jnp.ones((NH, CH, HD), f32)
    q = _l2n(cq) * (HD ** -0.5)
    k = _l2n(ck)
    la = -jnp.exp(_heads(alog, 0)) * jax.nn.softplus(_heads(ab, AB_LANE) + _heads(dtb, 0))
    beta = jax.nn.sigmoid(_heads(ab, AB_LANE + NH)) * ones
    i = lax.broadcasted_iota(jnp.int32, (CH, CH), 0)
    j = lax.broadcasted_iota(jnp.int32, (CH, CH), 1)
    incl, strict = j <= i, j < i
    g = _bdot(jnp.broadcast_to(incl.astype(f32), (NH, CH, CH)), la * ones, 2)
    g_row = _stack([g[h].T for h in range(NH)])
    gamma = jnp.where(incl, jnp.exp(jnp.where(incl, g - g_row, 0.0)), 0.0)
    lower = jnp.where(strict, beta * _bdot_nt(k, k, 1) * gamma, 0.0)
    inv = _unit_lower_inverse(lower)
    eg = jnp.exp(g)
    u = _bdot(inv, cv * beta, SOLVE_PASSES)
    w = _bdot(inv, k * (beta * eg), SOLVE_PASSES)
    qk = _bdot_nt(q, k, 1) * gamma
    g_last = g[:, CH - 1:CH, :]
    return u, w, q * eg, k * jnp.exp(g_last - g), qk, jnp.exp(g_last)


def _by_head(ref, t=0):
    return _stack([ref[:, t * WH + h * HD:t * WH + (h + 1) * HD] for h in range(NH)])


def gdn_a_fwd(c, proj, alog, dtb, name):
    s = c.shape[0]
    nc = s // CH

    def body(c_ref, ab_ref, al_ref, dt_ref, u_ref, w_ref, qd_ref, kd_ref, qk_ref, dec_ref):
        res = _gdn_chunk(_by_head(c_ref, 0), _by_head(c_ref, 1), _by_head(c_ref, 2), ab_ref[...], al_ref[...], dt_ref[...])
        for h in range(NH):
            sl = slice(h * HD, (h + 1) * HD)
            for r, v in zip((u_ref, w_ref, qd_ref, kd_ref, qk_ref), res[:5]):
                r[:, sl] = v[h]
            dec_ref[:, sl] = jnp.broadcast_to(res[5][h], (8, HD))

    row = pl.BlockSpec((CH, WH), lambda n: (n, 0))
    par = pl.BlockSpec((1, HD), lambda n: (0, 0))
    return pl.pallas_call(
        body, name=name, grid=(nc,),
        in_specs=[pl.BlockSpec((CH, 3 * WH), lambda n: (n, 0)), pl.BlockSpec((CH, HD), lambda n: (n, OFF_AB // HD)), par, par],
        out_specs=[row] * 5 + [pl.BlockSpec((None, 8, WH), lambda n: (n, 0, 0))],
        out_shape=[_sds((s, WH), f32)] * 5 + [_sds((nc, 8, WH), f32)],
        compiler_params=_cp(("parallel",)),
    )(c, proj, alog, dtb)


def gdn_a_bwd(c, proj, alog, dtb, cots, ddec, name, swap=()):
    s = c.shape[0]
    nc = s // CH
    gs, axes = swap if swap else ((), ())
    ns = len(gs)
    shards = [tuple(d // (NDEV if a == ax else 1) for a, d in enumerate(g.shape)) for g, ax in zip(gs, axes)]

    def body(c_ref, ab_ref, al_ref, dt_ref, du_ref, dw_ref, dqd_ref, dkd_ref, dqk_ref, ddec_ref, *rest):
        g_refs, (dc_ref, dab_ref, dal_ref, ddt_ref), p_refs = rest[:ns], rest[ns:ns + 4], rest[ns + 4:2 * ns + 4]
        n = pl.program_id(0)

        def copies():
            return _swap_copies(g_refs, p_refs, *rest[2 * ns + 4:], axes)

        if ns:
            @pl.when(n == 0)
            def _():
                for cp in copies():
                    cp.start()

        _, vjp = jax.vjp(_gdn_chunk, _by_head(c_ref, 0), _by_head(c_ref, 1), _by_head(c_ref, 2), ab_ref[...], al_ref[...],
                         dt_ref[...])
        lane = lax.broadcasted_iota(jnp.int32, (1, HD), 1)
        dd = _stack([jnp.where(lane == 0, ddec_ref[0:1, h * HD:(h + 1) * HD], 0.0) for h in range(NH)])
        dcq, dck, dcv, dab, dal, ddt = vjp(tuple(_by_head(r) for r in (du_ref, dw_ref, dqd_ref, dkd_ref, dqk_ref)) + (dd,))
        for h in range(NH):
            for t, v in enumerate((dcq, dck, dcv)):
                dc_ref[:, t * WH + h * HD:t * WH + (h + 1) * HD] = v[h]
        dab_ref[...] = dab.astype(bf16)

        @pl.when(n == 0)
        def _():
            dal_ref[...] = dal
            ddt_ref[...] = ddt

        @pl.when(n > 0)
        def _():
            dal_ref[...] += dal
            ddt_ref[...] += ddt

        if ns:
            @pl.when(n == nc - 1)
            def _():
                cps = copies()
                for cp in cps:
                    cp.wait_recv()
                for cp in cps:
                    cp.wait_send()

    row = pl.BlockSpec((CH, WH), lambda n: (n, 0))
    wide = pl.BlockSpec((CH, 3 * WH), lambda n: (n, 0))
    par = pl.BlockSpec((1, HD), lambda n: (0, 0))
    res = pl.pallas_call(
        body, name=name, grid=(nc,),
        in_specs=[wide, pl.BlockSpec((CH, HD), lambda n: (n, OFF_AB // HD)), par, par] + [row] * 5
        + [pl.BlockSpec((None, 8, WH), lambda n: (n, 0, 0))] + [ANY] * ns,
        out_specs=[wide, pl.BlockSpec((CH, HD), lambda n: (n, 0)), par, par] + [ANY] * ns,
        out_shape=[_sds((s, 3 * WH), f32), _sds((s, HD), bf16), _sds((1, HD), f32), _sds((1, HD), f32)]
        + [_sds((4,) + sh, g.dtype) for sh, g in zip(shards, gs)],
        scratch_shapes=[pltpu.SemaphoreType.DMA((ns, 4)), pltpu.SemaphoreType.DMA((ns, 4))] if ns else [],
        compiler_params=_cp(("arbitrary",)),
    )(c, proj, alog, dtb, *cots, ddec, *gs)
    return tuple(res[:4]) + (list(res[4:]),)


def gdn_b_fwd(u, w, qd, kd, qk, dec, name):
    s = u.shape[0]
    nc = s // CH

    def body(u_ref, w_ref, qd_ref, kd_ref, qk_ref, dec_ref, o_ref, st_ref, state):
        n = pl.program_id(0)

        @pl.when(n == 0)
        def _():
            state[...] = jnp.zeros_like(state)

        for h in range(NH):
            sl = slice(h * HD, (h + 1) * HD)
            st = state[sl, :]
            st_ref[sl, :] = st
            sb = st.astype(bf16)
            vn = u_ref[:, sl] - jnp.dot(w_ref[:, sl].astype(bf16), sb, preferred_element_type=f32)
            vb = vn.astype(bf16)
            o_ref[:, sl] = (jnp.dot(qd_ref[:, sl].astype(bf16), sb, preferred_element_type=f32)
                            + jnp.dot(qk_ref[:, sl].astype(bf16), vb, preferred_element_type=f32))
            state[sl, :] = st * dec_ref[0:1, sl] + lax.dot_general(
                kd_ref[:, sl].astype(bf16), vb, (((0,), (0,)), ((), ())), preferred_element_type=f32)

    row = pl.BlockSpec((CH, WH), lambda n: (n, 0))
    return pl.pallas_call(
        body, name=name, grid=(nc,),
        in_specs=[row] * 5 + [pl.BlockSpec((None, 8, WH), lambda n: (n, 0, 0))],
        out_specs=[row, pl.BlockSpec((None, WH, HD), lambda n: (n, 0, 0))],
        out_shape=[_sds((s, WH), f32), _sds((nc, WH, HD), f32)],
        scratch_shapes=[pltpu.VMEM((WH, HD), f32)],
        compiler_params=_cp(("arbitrary",)),
    )(u, w, qd, kd, qk, dec)


def gdn_b_bwd(u, w, qd, kd, qk, dec, states, do, name):
    s = u.shape[0]
    nc = s // CH

    def body(u_ref, w_ref, qd_ref, kd_ref, qk_ref, dec_ref, st_ref, do_ref,
             du_ref, dw_ref, dqd_ref, dkd_ref, dqk_ref, ddec_ref, dstate):
        n = pl.program_id(0)

        @pl.when(n == 0)
        def _():
            dstate[...] = jnp.zeros_like(dstate)

        for h in range(NH):
            sl = slice(h * HD, (h + 1) * HD)
            st, ds = st_ref[sl, :], dstate[sl, :]
            sb, dsb = st.astype(bf16), ds.astype(bf16)
            wb, qdb, kdb, qkb = (r[:, sl].astype(bf16) for r in (w_ref, qd_ref, kd_ref, qk_ref))
            dob = do_ref[:, sl].astype(bf16)
            vn = u_ref[:, sl] - jnp.dot(wb, sb, preferred_element_type=f32)
            vb = vn.astype(bf16)
            dvn = (lax.dot_general(qkb, dob, (((0,), (0,)), ((), ())), preferred_element_type=f32)
                   + jnp.dot(kdb, dsb, preferred_element_type=f32))
            dvb = dvn.astype(bf16)
            du_ref[:, sl] = dvn
            dw_ref[:, sl] = -_dot_nt(dvb, sb)
            dqd_ref[:, sl] = _dot_nt(dob, sb)
            dkd_ref[:, sl] = _dot_nt(vb, dsb)
            dqk_ref[:, sl] = _dot_nt(dob, vb)
            tot = jnp.sum(jnp.sum(ds * st, axis=1, keepdims=True), axis=0, keepdims=True)
            ddec_ref[:, sl] = jnp.broadcast_to(tot, (8, HD))
            dstate[sl, :] = (ds * dec_ref[0:1, sl]
                             + lax.dot_general(qdb, dob, (((0,), (0,)), ((), ())), preferred_element_type=f32)
                             - lax.dot_general(wb, dvb, (((0,), (0,)), ((), ())), preferred_element_type=f32))

    row = pl.BlockSpec((CH, WH), lambda n: (nc - 1 - n, 0))
    small = pl.BlockSpec((None, 8, WH), lambda n: (nc - 1 - n, 0, 0))
    return pl.pallas_call(
        body, name=name, grid=(nc,),
        in_specs=[row] * 5 + [small, pl.BlockSpec((None, WH, HD), lambda n: (nc - 1 - n, 0, 0)), row],
        out_specs=[row] * 5 + [small],
        out_shape=[_sds((s, WH), f32)] * 5 + [_sds((nc, 8, WH), f32)],
        scratch_shapes=[pltpu.VMEM((WH, HD), f32)],
        compiler_params=_cp(("arbitrary",)),
    )(u, w, qd, kd, qk, dec, states, do)


ANY = pl.BlockSpec(memory_space=pl.ANY)


def _dev_index(p):
    return 4 * p[0] + 2 * p[1] + p[2]


def _shard_of(ref, axis, size, idx):
    return ref.at[pl.ds(idx * size, size), :] if axis == 0 else ref.at[:, pl.ds(idx * size, size)]


def _peer(x, y, c, r):
    return (1 - x if r & 4 else x, 1 - y if r & 2 else y, 1 - c if r & 1 else c)


ALL_PEERS, OTHER_CHIPS = tuple(range(1, NDEV)), (4, 2, 6)


def _launch(body, ins, out_sds, name, sequencer_id, relations=ALL_PEERS, kinds=7):
    n = len(ins)
    sems = [pltpu.SemaphoreType.DMA((n, kinds)), pltpu.SemaphoreType.DMA((n, kinds)), pltpu.SemaphoreType.DMA((n,))]
    if sequencer_id is None:
        return pl.pallas_call(
            lambda *refs: body(refs[:n], refs[n:2 * n], *refs[2 * n:]), name=name, in_specs=[ANY] * n, out_specs=[ANY] * n,
            out_shape=out_sds, scratch_shapes=sems, compiler_params=pltpu.CompilerParams(has_side_effects=True),
        )(*ins)
    in_refs = [jax.new_ref(a, memory_space=pltpu.MemorySpace.HBM) for a in ins]
    out_refs = [jax.empty_ref(sd, memory_space=pltpu.MemorySpace.HBM) for sd in out_sds]

    @pl.kernel(mesh=plsc.ScalarSubcoreMesh(axis_name="sequencer", num_cores=1), name=name, scratch_types=sems,
               compiler_params=pltpu.CompilerParams(collective_id=sequencer_id))
    def launch(send, recv, loc):
        x, y, c = lax.axis_index("x"), lax.axis_index("y"), lax.axis_index("c")
        barrier = pltpu.get_barrier_semaphore()
        for r in relations:
            pl.semaphore_signal(barrier, inc=1, device_id=_peer(x, y, c, r), device_id_type=MESH)
        pl.semaphore_wait(barrier, len(relations))
        body(in_refs, out_refs, send, recv, loc)

    launch()
    return [r[...] for r in out_refs]


def all_gather(xs, axes, name, sequencer_id=None):
    n = len(xs)
    fulls = [tuple(d * (NDEV if a == ax else 1) for a, d in enumerate(x.shape)) for x, ax in zip(xs, axes)]
    halved = [x.shape[0] % 32 == 0 for x in xs]

    def body(x_refs, o_refs, send, recv, loc):
        x, y, c = lax.axis_index("x"), lax.axis_index("y"), lax.axis_index("c")
        me, sib = (x, y, c), (x, y, 1 - c)
        xn, yn, dg = (1 - x, y), (x, 1 - y), (1 - x, 1 - y)

        def part(t, p, half=None):
            ref = _shard_of(o_refs[t], axes[t], xs[t].shape[axes[t]], _dev_index(p))
            rows = xs[t].shape[0] // 2
            return ref if half is None else ref.at[pl.ds(half * rows, rows), :]

        def copy(t, k, block, to, half=None, src=None):
            return pltpu.make_async_remote_copy(
                src_ref=part(t, block, half) if src is None else src, dst_ref=part(t, block, half),
                send_sem=send.at[t, k], recv_sem=recv.at[t, k], device_id=to, device_id_type=MESH)

        mine = [pltpu.make_async_copy(x_refs[t], part(t, me), loc.at[t]) for t in range(n)]
        for cp in mine:
            cp.start()
        sends = []
        for t in range(n):
            sends += [copy(t, 0, me, sib, src=x_refs[t]), copy(t, 1, me, (*xn, c), src=x_refs[t]),
                      copy(t, 2, me, (*yn, c), src=x_refs[t])]
            if not halved[t]:
                sends.append(copy(t, 3, me, (*dg, c), src=x_refs[t]))
        for cp in sends:
            cp.start()

        def pass_on(cp):
            cp.start()
            sends.append(cp)

        for t in range(n):
            h0, h1 = (0, 1) if halved[t] else (None, None)
            copy(t, 1, (*xn, c), me).wait_recv()
            if halved[t]:
                pass_on(copy(t, 3, (*xn, c), (*yn, c), 0))
            pass_on(copy(t, 5, (*xn, c), sib))
            copy(t, 2, (*yn, c), me).wait_recv()
            if halved[t]:
                pass_on(copy(t, 4, (*yn, c), (*xn, c), 1))
            pass_on(copy(t, 6, (*yn, c), sib))
            copy(t, 3, (*dg, c), me, h0).wait_recv()
            pass_on(copy(t, 7, (*dg, c), sib, h0))
            if halved[t]:
                copy(t, 4, (*dg, c), me, h1).wait_recv()
                pass_on(copy(t, 8, (*dg, c), sib, h1))
        for t in range(n):
            h0, h1 = (0, 1) if halved[t] else (None, None)
            copy(t, 0, sib, me).wait_recv()
            copy(t, 5, (*xn, 1 - c), me).wait_recv()
            copy(t, 6, (*yn, 1 - c), me).wait_recv()
            copy(t, 7, (*dg, 1 - c), me, h0).wait_recv()
            if halved[t]:
                copy(t, 8, (*dg, 1 - c), me, h1).wait_recv()
        for cp in sends:
            cp.wait_send()
        for cp in mine:
            cp.wait()

    return _launch(body, xs, [_sds(f, x.dtype) for f, x in zip(fulls, xs)], name, sequencer_id, kinds=9)


def pair_add(g, p, axis, name, after=()):
    _, rows, cols = p.shape
    tr = _row_tile(rows, cols, 1 << 20)
    nb = rows // tr
    if axis == 0:
        g_spec = pl.BlockSpec((tr, cols), lambda j, i, c: ((2 * j + c[0]) * nb + i, 0))
    else:
        g_spec = pl.BlockSpec((tr, cols), lambda j, i, c: (i, 2 * j + c[0]))
    blk = pl.BlockSpec((None, tr, cols), lambda j, i, c: (j, i, 0))

    na = len(after)

    def body(c_ref, *refs):
        g_ref, p_ref, q_ref = refs[na:]
        q_ref[...] = (g_ref[...].astype(f32) + p_ref[...].astype(f32)).astype(bf16)

    return pl.pallas_call(
        body, name=name, out_shape=_sds(p.shape, bf16),
        grid_spec=pltpu.PrefetchScalarGridSpec(num_scalar_prefetch=1, grid=(4, nb), in_specs=[ANY] * na + [g_spec, blk],
                                               out_specs=blk),
        compiler_params=_cp(("parallel", "parallel")),
    )(lax.axis_index("c").astype(jnp.int32).reshape(1), *after, g, p)


def chip_exchange(qs, name, sequencer_id):
    n = len(qs)

    def body(q_refs, r_refs, send, recv, loc):
        x, y, c = lax.axis_index("x"), lax.axis_index("y"), lax.axis_index("c")
        my_chip = 2 * x + y
        peers = [_peer(x, y, c, r) for r in OTHER_CHIPS]
        mine = [pltpu.make_async_copy(q_refs[t].at[my_chip], r_refs[t].at[my_chip], loc.at[t]) for t in range(n)]
        for cp in mine:
            cp.start()

        def copy(t, k, slot):
            p = peers[k]
            return pltpu.make_async_remote_copy(
                src_ref=q_refs[t].at[2 * p[0] + p[1]], dst_ref=r_refs[t].at[slot], send_sem=send.at[t, k], recv_sem=recv.at[t, k],
                device_id=p, device_id_type=MESH)

        sends = [copy(t, k, my_chip) for t in range(n) for k in range(3)]
        for cp in sends:
            cp.start()
        for t in range(n):
            for k in range(3):
                copy(t, k, 2 * peers[k][0] + peers[k][1]).wait_recv()
        for cp in sends:
            cp.wait_send()
        for cp in mine:
            cp.wait()

    return _launch(body, qs, [_sds(q.shape, q.dtype) for q in qs], name, sequencer_id, OTHER_CHIPS)


def _adamw(w, g, m, v):
    m = ADAM_B1 * m + (1.0 - ADAM_B1) * g
    v = ADAM_B2 * v + (1.0 - ADAM_B2) * jnp.square(g)
    m_hat = m / (1.0 - ADAM_B1 ** ADAM_STEP)
    v_hat = v / (1.0 - ADAM_B2 ** ADAM_STEP)
    return -ADAM_LR * (m_hat / (jnp.sqrt(v_hat) + ADAM_EPS) + ADAM_WD * w), m, v


def _sum8(r_ref):
    g = r_ref[0].astype(f32)
    for j in range(1, r_ref.shape[0]):
        g = g + r_ref[j].astype(f32)
    return g


def _row_tile(rows, cols, elems=1 << 18):
    tr = min(rows, max(8, 1 << int(math.log2(elems / cols))))
    while rows % tr:
        tr //= 2
    assert tr % 8 == 0 or tr == rows, (rows, cols)
    return tr


def sum_partials(r, name, after=()):
    _, rows, cols = r.shape
    tr = _row_tile(rows, cols)
    na = len(after)

    def body(*refs):
        refs[na + 1][...] = _sum8(refs[na])

    return pl.pallas_call(
        body, name=name, grid=(rows // tr,),
        in_specs=[ANY] * na + [pl.BlockSpec((r.shape[0], tr, cols), lambda i: (0, i, 0))],
        out_specs=pl.BlockSpec((tr, cols), lambda i: (i, 0)), out_shape=_sds((rows, cols), f32),
        compiler_params=_cp(("parallel",)),
    )(*after, r)


def adamw_t(w, m, v, grads, name):
    rows, nl, cols = w.shape
    tr = min(rows, (1 << 16) // cols)
    blk = pl.BlockSpec((tr, nl, cols), lambda i: (i, 0, 0))
    flat = pl.BlockSpec((tr, cols), lambda i: (i, 0))

    def body(w_ref, m_ref, v_ref, *rest):
        g_refs, (g_ref, d_ref, nm_ref, nv_ref) = rest[:nl], rest[nl:]
        for l in range(nl):
            grad = g_refs[l][...]
            g_ref[:, l, :] = grad
            d_ref[:, l, :], nm_ref[:, l, :], nv_ref[:, l, :] = _adamw(w_ref[:, l, :], grad, m_ref[:, l, :], v_ref[:, l, :])

    return tuple(pl.pallas_call(
        body, name=name, grid=(pl.cdiv(rows, tr),), in_specs=[blk] * 3 + [flat] * nl, out_specs=[blk] * 4,
        out_shape=[_sds(w.shape, f32)] * 4, compiler_params=_cp(("parallel",)),
    )(w, m, v, *grads))


def adamw(w, m, v, layer, name, r=None, g=None, prev=None):
    _, rows, cols = w.shape
    tr = _row_tile(rows, cols)
    blk = pl.BlockSpec((None, tr, cols), lambda i: (layer, i, 0))
    nprev = 0 if prev is None else 4

    def body(w_ref, m_ref, v_ref, src_ref, *rest):
        g_ref, d_ref, nm_ref, nv_ref, token_ref = rest[nprev:]
        grad = _sum8(src_ref) if g is None else src_ref[...]
        g_ref[...] = grad
        d_ref[...], nm_ref[...], nv_ref[...] = _adamw(w_ref[...], grad, m_ref[...], v_ref[...])
        token_ref[...] = jnp.zeros_like(token_ref)

    src, src_spec = (r, pl.BlockSpec((r.shape[0], tr, cols), lambda i: (0, i, 0))) if g is None else (g, pl.BlockSpec((tr, cols), lambda i: (i, 0)))
    *outs, token = pl.pallas_call(
        body, name=name, grid=(rows // tr,), in_specs=[blk] * 3 + [src_spec] + [ANY] * nprev,
        out_specs=[blk] * 4 + [pl.BlockSpec((8, HD), lambda i: (0, 0))],
        out_shape=[_sds(w.shape, f32)] * 4 + [_sds((8, HD), f32)], input_output_aliases={4 + k: k for k in range(nprev)},
        compiler_params=_cp(("arbitrary",)),
    )(w, m, v, src, *(prev or ()))
    return tuple(outs), token


def small_adamw(parts, w, m, v, name, after=()):
    def body(*refs):
        p_ref, w_ref, m_ref, v_ref, g_ref, d_ref, nm_ref, nv_ref = refs[len(after):]
        g = _sum8(p_ref)
        g_ref[...] = g
        d_ref[...], nm_ref[...], nv_ref[...] = _adamw(w_ref[...], g, m_ref[...], v_ref[...])

    vmem = pl.BlockSpec(memory_space=pltpu.VMEM)
    return pl.pallas_call(body, name=name, out_shape=[_sds(w.shape, f32)] * 4, in_specs=[ANY] * len(after) + [vmem] * 4,
                          out_specs=[vmem] * 4, compiler_params=_cp())(*after, parts, w, m, v)


def _pack_rows(wt):
    tail = jnp.pad(wt[5888:N_IN], ((OFF_G - 12 - (OFF_P + 512), 0), (0, 0)))
    return jnp.concatenate([wt[512:2816], wt[2816:5120], wt[5120:5888], wt[0:512], tail], axis=0)


def _unpack_rows(g):
    return jnp.concatenate([g[OFF_P:OFF_P + 512], g[OFF_SB:OFF_SB + 2304], g[OFF_GQ:OFF_GQ + 2304], g[OFF_Z:OFF_Z + 768],
                            g[OFF_G - 12:NP]], axis=0)


def _lanes(v):
    flat = v.reshape(-1)
    n = -(-flat.shape[0] // HD) * HD
    return jnp.pad(flat, (0, n - flat.shape[0])).reshape(n // HD, HD)


def _lanes8(v):
    rows = _lanes(v)
    return jnp.pad(rows, ((0, -rows.shape[0] % 8), (0, 0)))


def _layer_fwd(x, p, l):
    nm = lambda s: f"{s}_l{l}"
    u = rms_fwd(x, p["attn_norm"], nm("rms1"))
    proj = matmul(u, p["w_in"][0], name=nm("inproj"), tb=True)
    y_pool = pool_fwd(proj, p["pool_w"], p["pool_scale"], nm("pool"))
    y_sb = sb_fwd(proj, nm("sb"))
    c = conv_fwd(proj, p["conv"], nm("conv"))
    ga = gdn_a_fwd(c, proj, p["alog"], p["dtb"], nm("gdna"))
    o, states = gdn_b_fwd(*ga, nm("gdnb"))
    y_gdn = gdn_out_fwd(o, proj, p["gdn_norm"], nm("gdno"))
    gates = matmul(u, p["w_in"][1], name=nm("inproj_gates"), tb=True, out_dtype=bf16, after=(y_gdn,))
    ups = [matmul(y, p[k], name=nm(k), out_dtype=bf16)
           for y, k in ((y_pool, "w_pool_up"), (y_sb, "w_sb_up"), (y_gdn, "w_gdn_up"))]
    merged = merge_fwd(gates, ups, nm("merge"))
    x1 = matmul(merged, p["w_out"], name=nm("outproj"), epilogue=lambda acc, r: acc + r, extras=(x,))
    u2 = rms_fwd(x1, p["mlp_norm"], nm("rms2"))
    h2 = matmul(u2, p["w_ff1"], name=nm("ff1"), out_dtype=bf16, epilogue=lambda acc: jnp.square(jnp.maximum(acc, 0.0)))
    x2 = matmul(h2, p["w_ff2"], name=nm("ff2"), epilogue=lambda acc, r: acc + r, extras=(x1,))
    saved = dict(x=x, u=u, proj=proj, gates=gates, y_pool=y_pool, y_sb=y_sb, c=c, ga=ga, o=o, states=states, y_gdn=y_gdn, ups=ups,
                 merged=merged, x1=x1, u2=u2, h2=h2)
    return x2, saved


def _layer_bwd(dx2, dx2b, sv, p, l, finish):
    nm = lambda s: f"{s}_l{l}"
    s = dx2.shape[0]
    dh = matmul(dx2b, p["w_ff2"], name=nm("d_ff2_x"), tb=True, out_dtype=bf16,
                epilogue=lambda acc, h2: acc * (2.0 * jnp.sqrt(h2.astype(f32))), extras=(sv["h2"],))
    g_ff2 = matmul(sv["h2"], dx2b, name=nm("d_ff2_w"), ta=True, out_dtype=bf16)
    du2 = matmul(dh, p["w_ff1"], name=nm("d_ff1_x"), tb=True)
    g_ff1 = matmul(sv["u2"], dh, name=nm("d_ff1_w"), ta=True, out_dtype=bf16)
    dx1, dx1b, d_mlp_norm = rms_bwd(sv["x1"], du2, dx2, p["mlp_norm"], nm("d_rms2"))
    dmerged = matmul(dx1b, p["w_out"], name=nm("d_out_x"), tb=True, out_dtype=bf16)
    g_out = matmul(sv["merged"], dx1b, name=nm("d_out_w"), ta=True, out_dtype=bf16)
    dgates, *dups = merge_bwd(sv["gates"], sv["ups"], dmerged, nm("d_merge"))
    dys, g_ups = [], []
    for dup, y, k in zip(dups, (sv["y_pool"], sv["y_sb"], sv["y_gdn"]), ("w_pool_up", "w_sb_up", "w_gdn_up")):
        dys.append(matmul(dup, p[k], name=nm("d_" + k + "_x"), tb=True, out_dtype=bf16))
        g_ups.append(matmul(y, dup, name=nm("d_" + k + "_w"), ta=True, out_dtype=bf16))
    early = g_ups + [g_out, g_ff1, g_ff2]
    do, dz, d_gdn_norm = gdn_out_bwd(sv["o"], sv["proj"], p["gdn_norm"], dys[2], nm("d_gdno"))
    cots = gdn_b_bwd(*sv["ga"], sv["states"], do, nm("d_gdnb"))
    dc, dab, d_alog, d_dtb, swapped = gdn_a_bwd(sv["c"], sv["proj"], p["alog"], p["dtb"], cots[:5], cots[5], nm("d_gdna"),
                                                (early, BIG_AXES[1:]))
    r_early, sent_early = finish(early, swapped, BIG_AXES[1:], BIG_NAMES[1:], "a", (dab,))
    dgq, d_conv = conv_bwd(sv["proj"], p["conv"], dc, nm("d_conv"), sent_early)
    dq, dk, dv = sb_bwd(sv["proj"], dys[1], nm("d_sb"))
    dp, d_pool_w, d_pool_scale = pool_bwd(sv["proj"], p["pool_w"], p["pool_scale"], dys[0], nm("d_pool"))
    dproj = jnp.concatenate([dq, dk, dv, dgq, dz, dp, jnp.zeros((s, OFF_AB - OFF_P - W_POOL), bf16), dab], axis=1)
    g_in = [matmul(dproj, sv["u"], name=nm("d_in_w"), ta=True, out_dtype=bf16),
            matmul(dgates, sv["u"], name=nm("d_in_w_gates"), ta=True, out_dtype=bf16)]
    du, swapped = matmul(dproj, p["w_in"][0], name=nm("d_in_x"), swap=(g_in, (1, 1)))
    r_in, sent_in = finish(g_in, swapped, (1, 1), ("w_in", "w_in_gates"), "b", ())
    du = matmul(dgates, p["w_in"][1], name=nm("d_in_x_gates"), epilogue=lambda acc, r: acc + r, extras=(du,), after=sent_in)
    dx, dxb, d_attn_norm = rms_bwd(sv["x"], du, dx1, p["attn_norm"], nm("d_rms1"))
    recv = r_in + r_early
    small = [d_attn_norm, d_pool_w, d_pool_scale, d_conv, d_alog, d_dtb, d_gdn_norm, d_mlp_norm]
    return dx, dxb, recv, small


BIG_AXES = (1, 1, 1, 1, 0, 1, 0)
GATHER_ID, EXCHANGE_ID = 1, 2
BIG_NAMES = ("w_in", "w_pool_up", "w_sb_up", "w_gdn_up", "w_out", "w_ff1", "w_ff2")


def kernel(x, attn_norm, w_in, pool_w, pool_scale, gdn_conv, gdn_a_log, gdn_dt_bias, gdn_norm, w_pool_up, w_sb_up, w_gdn_up, w_out, mlp_norm, w_ff1, w_ff2, final_norm, loss_target, m_attn_norm, m_w_in, m_pool_w, m_pool_scale, m_gdn_conv, m_gdn_a_log, m_gdn_dt_bias, m_gdn_norm, m_w_pool_up, m_w_sb_up, m_w_gdn_up, m_w_out, m_mlp_norm, m_w_ff1, m_w_ff2, m_final_norm, v_attn_norm, v_w_in, v_pool_w, v_pool_scale, v_gdn_conv, v_gdn_a_log, v_gdn_dt_bias, v_gdn_norm, v_w_pool_up, v_w_sb_up, v_w_gdn_up, v_w_out, v_mlp_norm, v_w_ff1, v_w_ff2, v_final_norm):
    s = x.shape[1]
    me = _dev_index((lax.axis_index("x"), lax.axis_index("y"), lax.axis_index("c")))
    ncv = gdn_conv.shape[2]

    w_in_t, m_w_in_t, v_w_in_t = (jnp.transpose(a, (2, 0, 1)) for a in (w_in, m_w_in, v_w_in))
    full, prev = [], None
    for l in range(NL):
        packed_in = _pack_rows(w_in_t[:, l]).astype(bf16)
        groups = [[packed_in[:OFF_G]] + ([gdn_conv.reshape(NL * 4, ncv)] if l == 0 else []),
                  [packed_in[OFF_G:]],
                  [w_pool_up[l].astype(bf16), w_sb_up[l].astype(bf16), w_gdn_up[l].astype(bf16), w_out[l].astype(bf16)],
                  [w_ff1[l].astype(bf16), w_ff2[l].astype(bf16)]]
        axes = [(1, 0)[:len(groups[0])], (1,), BIG_AXES[1:5], BIG_AXES[5:]]
        got = []
        for k, (group, ax) in enumerate(zip(groups, axes)):
            if prev is not None:
                group, _ = lax.optimization_barrier((group, prev))
            prev = group[0]
            got.append(all_gather(group, ax, f"gather_{k}_l{l}", sequencer_id=GATHER_ID))
        full.append([(got[0][0], got[1][0])] + got[2] + got[3])
        if l == 0:
            conv_full = got[0][1].reshape(NDEV, NL, 4, ncv).transpose(1, 2, 0, 3).reshape(NL, 4, NDEV * ncv)
    params = []
    for l in range(NL):
        p = dict(zip(("w_in", "w_pool_up", "w_sb_up", "w_gdn_up", "w_out", "w_ff1", "w_ff2"), full[l][:7]))
        p.update(attn_norm=attn_norm[l][None], mlp_norm=mlp_norm[l][None], pool_w=pool_w[l], pool_scale=pool_scale[l][None],
                 conv=conv_full[l], alog=_lanes(gdn_a_log[l]), dtb=_lanes(gdn_dt_bias[l]), gdn_norm=gdn_norm[l][None])
        params.append(p)

    h = x[0]
    saved = []
    for l in range(NL):
        h, sv = _layer_fwd(h, params[l], l)
        saved.append(sv)
    dh, dhb, loss_row, d_final = loss_head(h, loss_target[0], final_norm[None], "loss_head")
    recv, smalls = [None] * NL, [None] * NL
    for l in reversed(range(NL)):
        def finish(gs, ps, axes, names, tag, after, l=l):
            qs = [pair_add(g, p, ax, f"pair_add_{k}_l{l}", after) for g, p, ax, k in zip(gs, ps, axes, names)]
            return chip_exchange(qs, f"exchange_{tag}_l{l}", EXCHANGE_ID), tuple(qs)

        dh, dhb, recv[l], smalls[l] = _layer_bwd(dh, dhb, saved[l], params[l], l, finish)

    small_rows = [_lanes8(t) for l in range(NL) for t in smalls[l]] + [_lanes8(d_final), _lanes8(loss_row)]
    packed = jnp.concatenate(small_rows, axis=0)
    parts = all_gather([packed], (0,), "gather_small", sequencer_id=GATHER_ID)[0].reshape(NDEV, packed.shape[0], HD)

    def pack_small(tree):
        rows = []
        for l in range(NL):
            rows += [_lanes8(tree["attn_norm"][l]), _lanes8(tree["pool_w"][l]), _lanes8(tree["pool_scale"][l]),
                     jnp.zeros((4 * NDEV * ncv // HD, HD), f32), _lanes8(tree["gdn_a_log"][l]), _lanes8(tree["gdn_dt_bias"][l]),
                     _lanes8(tree["gdn_norm"][l]), _lanes8(tree["mlp_norm"][l])]
        rows += [_lanes8(tree["final_norm"]), jnp.zeros((8, HD), f32)]
        return jnp.concatenate(rows, axis=0)

    names = ("attn_norm", "pool_w", "pool_scale", "gdn_a_log", "gdn_dt_bias", "gdn_norm", "mlp_norm", "final_norm")
    w_small = pack_small(dict(zip(names, (attn_norm, pool_w, pool_scale, gdn_a_log, gdn_dt_bias, gdn_norm, mlp_norm, final_norm))))
    m_small = pack_small(dict(zip(names, (m_attn_norm, m_pool_w, m_pool_scale, m_gdn_a_log, m_gdn_dt_bias, m_gdn_norm, m_mlp_norm, m_final_norm))))
    v_small = pack_small(dict(zip(names, (v_attn_norm, v_pool_w, v_pool_scale, v_gdn_a_log, v_gdn_dt_bias, v_gdn_norm, v_mlp_norm, v_final_norm))))
    def unpack_small(buf):
        out, conv_g, r = {}, [], 0
        layer_items = (("attn_norm", (D,)), ("pool_w", (4, HD, HD)), ("pool_scale", (W_POOL,)), ("conv", (4, NDEV * ncv)),
                       ("gdn_a_log", (NH,)), ("gdn_dt_bias", (NH,)), ("gdn_norm", (HD,)), ("mlp_norm", (D,)))
        per_layer = {k: [] for k, _ in layer_items}
        for l in range(NL):
            for k, shape in layer_items:
                size = math.prod(shape)
                nrow = -(-size // (8 * HD)) * 8
                per_layer[k].append(buf[r:r + nrow].reshape(-1)[:size].reshape(shape))
                r += nrow
        for k, _ in layer_items:
            out[k] = jnp.stack(per_layer[k])
        out["final_norm"] = buf[r:r + D // HD].reshape(D)
        out["loss"] = buf[r + D // HD, 0]
        return out

    big_out = {}
    big_names = BIG_NAMES
    big_w = dict(zip(big_names, ((w_in_t, m_w_in_t, v_w_in_t), (w_pool_up, m_w_pool_up, v_w_pool_up), (w_sb_up, m_w_sb_up, v_w_sb_up),
                                 (w_gdn_up, m_w_gdn_up, v_w_gdn_up), (w_out, m_w_out, v_w_out), (w_ff1, m_w_ff1, v_w_ff1),
                                 (w_ff2, m_w_ff2, v_w_ff2))))
    tokens = []
    for l, k in [(l, k) for l in reversed(range(NL)) for k in big_names[1:]]:
        w, m, v = big_w[k]
        big_out[k], token = adamw(w, m, v, l, f"adamw_{k}_l{l}", r=recv[l][big_names.index(k) + 1], prev=big_out.get(k))
        tokens.append(token)
    g_in = []
    for l in range(NL):
        halves = [sum_partials(recv[l][h], f"sum_w_in_{h}_l{l}", tuple(tokens) if l == 0 else ()) for h in range(2)]
        g_in.append(_unpack_rows(jnp.concatenate(halves, axis=0)))
    big_out["w_in"] = adamw_t(*big_w["w_in"], g_in, "adamw_w_in")

    small_out = small_adamw(parts, w_small, m_small, v_small, "adamw_small", after=(big_out["w_in"][1],))
    sm = [unpack_small(b) for b in small_out]
    loss = sm[0]["loss"]
    g_conv = lax.dynamic_slice_in_dim(sm[0]["conv"], me * ncv, ncv, axis=2)
    conv_out = None
    for l in reversed(range(NL)):
        conv_out, _ = adamw(gdn_conv, m_gdn_conv, v_gdn_conv, l, f"adamw_conv_l{l}", g=g_conv[l], prev=conv_out)

    def leaf(i, k):
        if k == "w_in":
            return jnp.transpose(big_out[k][i], (1, 2, 0))
        if k in big_out:
            return big_out[k][i]
        if k == "gdn_conv":
            return conv_out[i]
        return sm[i][k]

    order = ("attn_norm", "w_in", "pool_w", "pool_scale", "gdn_conv", "gdn_a_log", "gdn_dt_bias", "gdn_norm", "w_pool_up",
             "w_sb_up", "w_gdn_up", "w_out", "mlp_norm", "w_ff1", "w_ff2", "final_norm")
    return (loss, dh[None]) + tuple(leaf(i, k) for i in range(4) for k in order)
```

```python
import functools
import math

import jax
import jax.numpy as jnp
from jax import lax
from jax.experimental import pallas as pl
from jax.experimental.pallas import tpu as pltpu
from jax.experimental.pallas import tpu_sc as plsc

f32, bf16 = jnp.float32, jnp.bfloat16

D = 2048
NDEV = 8
NL = 2
HD = 128
NH = 6
WH = NH * HD
W_POOL = 512
EPS = 1e-6
N_IN = 12044
NP = 12288
OFF_SB, OFF_GQ, OFF_Z, OFF_P, OFF_AB, OFF_G = 0, 2304, 4608, 5376, 6016, 6144
AB_LANE = HD - 2 * NH
POOL_WINDOWS = (2, 4, 8, 16)
CH = 128
TQ = 256
VMEM_LIMIT = 56 * 1024 * 1024
ADAM_LR, ADAM_B1, ADAM_B2, ADAM_EPS, ADAM_WD, ADAM_STEP = 0.001, 0.9, 0.999, 1e-08, 0.01, 10
MESH = pl.DeviceIdType.MESH


def _cp(sem=None):
    return pltpu.CompilerParams(dimension_semantics=sem, vmem_limit_bytes=VMEM_LIMIT)


def _sds(shape, dtype):
    return jax.ShapeDtypeStruct(tuple(shape), dtype)


def _swap_copies(g_refs, p_refs, send, recv, axes):
    x, y, core = lax.axis_index("x"), lax.axis_index("y"), lax.axis_index("c")
    return [pltpu.make_async_remote_copy(
        src_ref=_shard_of(g, ax, p.shape[1 + ax], 2 * j + (1 - core)), dst_ref=p.at[j], send_sem=send.at[t, j],
        recv_sem=recv.at[t, j], device_id=(x, y, 1 - core), device_id_type=MESH)
        for t, (g, p, ax) in enumerate(zip(g_refs, p_refs, axes)) for j in range(4)]


def matmul(a, b, *, name, ta=False, tb=False, out_dtype=f32, tm=1024, tn=1024, tk=2048, epilogue=None, extras=(), after=(),
           swap=()):
    m, k = (a.shape[1], a.shape[0]) if ta else a.shape
    n = b.shape[0] if tb else b.shape[1]
    assert k == (b.shape[1] if tb else b.shape[0]) and a.dtype == bf16 and b.dtype == bf16
    tm, tn, tk = min(tm, m), min(tn, n), min(tk, k)
    assert m % tm == 0 and n % tn == 0 and k % tk == 0, (m, n, k, tm, tn, tk)
    nk = k // tk
    a_spec = pl.BlockSpec((tk, tm), lambda i, j, q: (q, i)) if ta else pl.BlockSpec((tm, tk), lambda i, j, q: (i, q))
    b_spec = pl.BlockSpec((tn, tk), lambda i, j, q: (j, q)) if tb else pl.BlockSpec((tk, tn), lambda i, j, q: (q, j))
    e_specs = [pl.BlockSpec((tm, tn), lambda i, j, q: (i, j)) for _ in extras]
    dn = (((0 if ta else 1,), (1 if tb else 0,)), ((), ()))
    ne = len(extras)
    gs, axes = swap if swap else ((), ())
    ns = len(gs)
    shards = [tuple(d // (NDEV if ax_ == ax else 1) for ax_, d in enumerate(g.shape)) for g, ax in zip(gs, axes)]
    grid = (m // tm, n // tn, nk)

    def body(*refs):
        g_refs = refs[len(after):len(after) + ns]
        a_ref, b_ref, *rest = refs[len(after) + ns:]
        e_refs, o_ref, p_refs = rest[:ne], rest[ne], rest[ne + 1:ne + 1 + ns]
        if ns:
            at = [pl.program_id(d) for d in range(3)]
            first = functools.reduce(jnp.logical_and, [c == 0 for c in at])
            last = functools.reduce(jnp.logical_and, [c == g - 1 for c, g in zip(at, grid)])

            @pl.when(first)
            def _():
                for cp in _swap_copies(g_refs, p_refs, *rest[-2:], axes):
                    cp.start()

        part = lax.dot_general(a_ref[...], b_ref[...], dn, preferred_element_type=f32)

        def finish(acc):
            if epilogue is not None:
                acc = epilogue(acc, *[e[...] for e in e_refs])
            o_ref[...] = acc.astype(out_dtype)

        if nk == 1:
            finish(part)
        else:
            acc_ref = rest[ne + 1 + ns]
            q = pl.program_id(2)

            @pl.when(q == 0)
            def _():
                acc_ref[...] = part

            @pl.when(jnp.logical_and(q > 0, q < nk - 1))
            def _():
                acc_ref[...] += part

            @pl.when(q == nk - 1)
            def _():
                finish(acc_ref[...] + part)

        if ns:
            @pl.when(last)
            def _():
                cps = _swap_copies(g_refs, p_refs, *rest[-2:], axes)
                for cp in cps:
                    cp.wait_recv()
                for cp in cps:
                    cp.wait_send()

    hbm = pl.BlockSpec(memory_space=pl.ANY)
    res = pl.pallas_call(
        body, name=name, grid=grid,
        in_specs=[hbm] * (len(after) + ns) + [a_spec, b_spec] + e_specs,
        out_specs=[pl.BlockSpec((tm, tn), lambda i, j, q: (i, j))] + [hbm] * ns,
        out_shape=[_sds((m, n), out_dtype)] + [_sds((4,) + sh, g.dtype) for sh, g in zip(shards, gs)],
        scratch_shapes=([pltpu.VMEM((tm, tn), f32)] if nk > 1 else [])
        + ([pltpu.SemaphoreType.DMA((ns, 4)), pltpu.SemaphoreType.DMA((ns, 4))] if ns else []),
        compiler_params=_cp(("arbitrary",) * 3 if ns else ("parallel", "parallel", "arbitrary")),
    )(*after, *gs, a, b, *extras)
    return (res[0], list(res[1:])) if ns else res[0]


def rowwise(name, fn, rows, params, outs, sums=(), tr=256, after=()):
    s = rows[0][0].shape[0]
    tr = min(tr, s)
    nin, nout = len(rows) + len(params), len(outs)
    in_specs = [pl.BlockSpec((tr, w), functools.partial(lambda i, c: (i, c), c=c)) for (_, w, c) in rows]
    in_specs += [pl.BlockSpec(p.shape, lambda i: (0, 0)) for p in params]
    out_specs = [pl.BlockSpec((tr, w), lambda i: (i, 0)) for (w, _) in outs]
    out_specs += [pl.BlockSpec(sh, lambda i: (0, 0)) for sh in sums]
    out_shape = [_sds((s, w), dt) for (w, dt) in outs] + [_sds(sh, f32) for sh in sums]

    def body(*refs):
        refs = refs[len(after):]
        res = fn(*[r[...] for r in refs[:nin]])
        for r, v in zip(refs[nin:nin + nout], res[:nout]):
            r[...] = v.astype(r.dtype)
        i = pl.program_id(0)
        for r, v in zip(refs[nin + nout:], res[nout:]):
            @pl.when(i == 0)
            def _(r=r, v=v):
                r[...] = v

            @pl.when(i > 0)
            def _(r=r, v=v):
                r[...] += v

    res = pl.pallas_call(
        body, name=name, grid=(s // tr,), in_specs=[pl.BlockSpec(memory_space=pl.ANY)] * len(after) + in_specs,
        out_specs=out_specs, out_shape=out_shape, compiler_params=_cp(("arbitrary",)),
    )(*after, *[r[0] for r in rows], *params)
    return res


def _rms(x, g):
    return x * lax.rsqrt(jnp.mean(x * x, axis=-1, keepdims=True) + EPS) * g


def rms_fwd(x, g, name):
    return rowwise(name, lambda xb, gb: (_rms(xb, gb),), [(x, D, 0)], [g], [(D, bf16)])[0]


def rms_bwd(x, du, dres, g, name, after=()):
    def fn(xb, dub, drb, gb):
        _, vjp = jax.vjp(_rms, xb, gb)
        dx, dg = vjp(dub.astype(f32))
        return drb + dx, drb + dx, dg

    return rowwise(name, fn, [(x, D, 0), (du, D, 0), (dres, D, 0)], [g], [(D, f32), (D, bf16)], [(1, D)], after=after)


def _merge(gates, up_p, up_s, up_g):
    sg = jax.nn.sigmoid(gates)
    return sg[:, :D] * up_p + sg[:, D:2 * D] * up_s + sg[:, 2 * D:] * up_g


def merge_fwd(gates, ups, name):
    return rowwise(name, lambda *blocks: (_merge(*[b.astype(f32) for b in blocks]),),
                   [(gates, 3 * D, 0)] + [(u, D, 0) for u in ups], [], [(D, bf16)], tr=128)[0]


def merge_bwd(gates, ups, dmerged, name):
    def fn(g, a, b, c, dm):
        _, vjp = jax.vjp(_merge, *[t.astype(f32) for t in (g, a, b, c)])
        return vjp(dm.astype(f32))

    return rowwise(name, fn, [(gates, 3 * D, 0)] + [(u, D, 0) for u in ups] + [(dmerged, D, 0)], [],
                   [(3 * D, bf16), (D, bf16), (D, bf16), (D, bf16)], tr=128)


def _gdn_out(o, z, g):
    ys = []
    for h in range(NH):
        sl = slice(h * HD, (h + 1) * HD)
        ys.append(_rms(o[:, sl], g) * jax.nn.silu(z[:, sl]))
    return jnp.concatenate(ys, axis=1)


def gdn_out_fwd(o, proj, g, name):
    return rowwise(name, lambda ob, zb, gb: (_gdn_out(ob, zb, gb),), [(o, WH, 0), (proj, WH, OFF_Z // WH)], [g],
                   [(WH, bf16)])[0]


def gdn_out_bwd(o, proj, g, dy, name):
    def fn(ob, zb, dyb, gb):
        _, vjp = jax.vjp(_gdn_out, ob, zb, gb)
        return vjp(dyb.astype(f32))

    return rowwise(name, fn, [(o, WH, 0), (proj, WH, OFF_Z // WH), (dy, WH, 0)], [g], [(WH, f32), (WH, bf16)],
                   [(1, HD)])


def loss_head(x, target, g, name):
    def loss_fn(xb, gb, tb):
        err = _rms(xb, gb) - tb
        return (0.5 / D) * jnp.sum(jnp.sum(err * err, axis=1, keepdims=True), axis=0, keepdims=True)

    def fn(xb, tb, gb):
        val, vjp = jax.vjp(functools.partial(loss_fn, tb=tb), xb, gb)
        dx, dg = vjp(jnp.ones((1, 1), f32))
        return dx, dx, jnp.broadcast_to(val, (1, HD)), dg

    return rowwise(name, fn, [(x, D, 0), (target, D, 0)], [g], [(D, f32), (D, bf16)], [(1, HD), (1, D)])


PB = 256


def _split(v):
    hi = v.astype(bf16)
    return hi, (v - hi.astype(f32)).astype(bf16)


def _band_dot(make_band, v, s, forward):
    hi, lo = _split(v)
    nb = s // PB
    outs = []
    for r in range(nb):
        lo_r = max(r - 1, 0) if forward else r
        hi_r = r + 1 if forward else min(r + 2, nb)
        band = make_band(r * PB, lo_r * PB, (hi_r - lo_r) * PB)
        sl = slice(lo_r * PB, hi_r * PB)
        outs.append(jnp.dot(band, hi[sl], preferred_element_type=f32) + jnp.dot(band, lo[sl], preferred_element_type=f32))
    return jnp.concatenate(outs, axis=0)


def _pool_common(p, win, s):
    def band(row0, col0, ncol):
        t = row0 + lax.broadcasted_iota(jnp.int32, (PB, ncol), 0)
        u = col0 + lax.broadcasted_iota(jnp.int32, (PB, ncol), 1)
        return jnp.logical_and(u <= t, t < u + win).astype(bf16)

    def band_t(row0, col0, ncol):
        u = row0 + lax.broadcasted_iota(jnp.int32, (PB, ncol), 0)
        t = col0 + lax.broadcasted_iota(jnp.int32, (PB, ncol), 1)
        return jnp.logical_and(u <= t, t < u + win).astype(bf16)

    t = lax.broadcasted_iota(jnp.int32, (s, 1), 0)
    inv_n = 1.0 / jnp.minimum(t + 1, win).astype(f32)
    d = _band_dot(band, p, s, True) * inv_n - p
    return d, inv_n, band_t


def pool_fwd(proj, pool_w, pool_scale, name):
    s = proj.shape[0]

    def body(p_ref, w_ref, sc_ref, y_ref):
        win = jnp.left_shift(2, pl.program_id(0))
        d, _, _ = _pool_common(p_ref[...], win, s)
        y = jnp.dot(d.astype(bf16), w_ref[...].astype(bf16), preferred_element_type=f32) * sc_ref[...]
        y_ref[...] = y.astype(bf16)

    return pl.pallas_call(
        body, name=name, grid=(4,),
        in_specs=[pl.BlockSpec((s, HD), lambda g: (0, OFF_P // HD + g)), pl.BlockSpec((None, HD, HD), lambda g: (g, 0, 0)),
                  pl.BlockSpec((1, HD), lambda g: (0, g))],
        out_specs=pl.BlockSpec((s, HD), lambda g: (0, g)), out_shape=_sds((s, W_POOL), bf16),
        compiler_params=_cp(("arbitrary",)),
    )(proj, pool_w, pool_scale)


def pool_bwd(proj, pool_w, pool_scale, dy, name):
    s = proj.shape[0]

    def body(p_ref, w_ref, sc_ref, dy_ref, dp_ref, dw_ref, dsc_ref):
        win = jnp.left_shift(2, pl.program_id(0))
        d, inv_n, band_t = _pool_common(p_ref[...], win, s)
        w = w_ref[...].astype(bf16)
        dyf = dy_ref[...].astype(f32)
        dsc_ref[...] = jnp.sum(dyf * jnp.dot(d.astype(bf16), w, preferred_element_type=f32), axis=0, keepdims=True)
        dys = (dyf * sc_ref[...]).astype(bf16)
        dd = lax.dot_general(dys, w, (((1,), (1,)), ((), ())), preferred_element_type=f32)
        dw_ref[...] = lax.dot_general(d.astype(bf16), dys, (((0,), (0,)), ((), ())), preferred_element_type=f32)
        dp_ref[...] = (_band_dot(band_t, dd * inv_n, s, False) - dd).astype(bf16)

    return pl.pallas_call(
        body, name=name, grid=(4,),
        in_specs=[pl.BlockSpec((s, HD), lambda g: (0, OFF_P // HD + g)), pl.BlockSpec((None, HD, HD), lambda g: (g, 0, 0)),
                  pl.BlockSpec((1, HD), lambda g: (0, g)), pl.BlockSpec((s, HD), lambda g: (0, g))],
        out_specs=[pl.BlockSpec((s, HD), lambda g: (0, g)), pl.BlockSpec((None, HD, HD), lambda g: (g, 0, 0)),
                   pl.BlockSpec((1, HD), lambda g: (0, g))],
        out_shape=[_sds((s, W_POOL), bf16), _sds((4, HD, HD), f32), _sds((1, W_POOL), f32)],
        compiler_params=_cp(("arbitrary",)),
    )(proj, pool_w, pool_scale, dy)


HB = 3
WB = HB * HD


def _heads_of(v):
    return _stack([v[:, h * HD:(h + 1) * HD] for h in range(HB)])


def _bd(a, b, dn):
    return lax.dot_general(a, b, dn, preferred_element_type=f32)


def _run_sum(v, tri):
    hi, lo = _split(v.reshape(HB * TQ, TQ))
    return (jnp.dot(hi, tri, preferred_element_type=f32) + jnp.dot(lo, tri, preferred_element_type=f32)).reshape(HB, TQ, TQ)


def _tri(later):
    j = lax.broadcasted_iota(jnp.int32, (TQ, TQ), 0)
    u = lax.broadcasted_iota(jnp.int32, (TQ, TQ), 1)
    return (j > u if later else j < u).astype(bf16)


def _sb_tile(q, k_ref, kb, carry, diagonal):
    k = _heads_of(k_ref[pl.ds(pl.multiple_of(kb * TQ, TQ), TQ), :].astype(bf16))
    z = _bd(q, k, BNT)
    ls = jax.nn.log_sigmoid(-z)
    if diagonal:
        mask = lax.broadcasted_iota(jnp.int32, (TQ, TQ), 1) < lax.broadcasted_iota(jnp.int32, (TQ, TQ), 0)
        ls = jnp.where(mask, ls, 0.0)
    a = jnp.exp(ls + z + _run_sum(ls, _tri(True)) + carry)
    return z, ls, jnp.where(mask, a, 0.0) if diagonal else a


def sb_fwd(proj, name):
    s = proj.shape[0]
    nq = s // TQ
    scale = HD ** -0.5

    def body(q_ref, k_ref, v_ref, y_ref):
        qi = pl.program_id(1)
        q = _heads_of((q_ref[...] * scale).astype(bf16))

        def step(j, c, diagonal=False):
            acc, carry = c
            kb = qi - j
            _, ls, a = _sb_tile(q, k_ref, kb, carry, diagonal)
            v = _heads_of(v_ref[pl.ds(pl.multiple_of(kb * TQ, TQ), TQ), :].astype(bf16))
            return acc + _bd(a.astype(bf16), v, BNN), carry + jnp.sum(ls, axis=2, keepdims=True)

        first = step(0, (jnp.zeros((HB, TQ, HD), f32), jnp.zeros((HB, TQ, 1), f32)), True)
        acc, _ = lax.fori_loop(1, qi + 1, step, first)
        for h in range(HB):
            y_ref[:, h * HD:(h + 1) * HD] = acc[h].astype(bf16)

    c0, ng = OFF_SB // WB, NH // HB
    return pl.pallas_call(
        body, name=name, grid=(ng, nq),
        in_specs=[pl.BlockSpec((TQ, WB), lambda g, i: (i, c0 + g)), pl.BlockSpec((s, WB), lambda g, i: (0, c0 + ng + g)),
                  pl.BlockSpec((s, WB), lambda g, i: (0, c0 + 2 * ng + g))],
        out_specs=pl.BlockSpec((TQ, WB), lambda g, i: (i, g)), out_shape=_sds((s, WH), bf16),
        compiler_params=_cp(("arbitrary", "arbitrary")),
    )(proj, proj, proj)


def sb_bwd(proj, dy, name):
    s = proj.shape[0]
    nq = s // TQ
    scale = HD ** -0.5

    def body(q_ref, k_ref, v_ref, do_ref, dq_ref, dk_ref, dv_ref, e_scr, z_scr, dk_acc, dv_acc):
        qi = pl.program_id(1)
        q = _heads_of((q_ref[...] * scale).astype(bf16))
        do = _heads_of(do_ref[...])

        @pl.when(qi == 0)
        def _():
            dk_acc[...] = jnp.zeros_like(dk_acc)
            dv_acc[...] = jnp.zeros_like(dv_acc)

        def add_heads(acc_ref, rows, upd):
            for h in range(HB):
                acc_ref[rows, h * HD:(h + 1) * HD] += upd[h]

        def sweep_back(j, carry, diagonal=False):
            kb = qi - j
            rows = pl.ds(pl.multiple_of(kb * TQ, TQ), TQ)
            z, ls, a = _sb_tile(q, k_ref, kb, carry, diagonal)
            v = _heads_of(v_ref[rows, :].astype(bf16))
            e_scr[kb] = _bd(do, v, BNT) * a
            z_scr[kb] = z
            add_heads(dv_acc, rows, _bd(a.astype(bf16), do, BTN))
            return carry + jnp.sum(ls, axis=2, keepdims=True)

        lax.fori_loop(1, qi + 1, sweep_back, sweep_back(0, jnp.zeros((HB, TQ, 1), f32), True))

        def sweep_fwd(kb, c, diagonal=False):
            dq, carry = c
            rows = pl.ds(pl.multiple_of(kb * TQ, TQ), TQ)
            e, z = e_scr[kb], z_scr[kb]
            sig = jax.nn.sigmoid(z)
            dz = e * (1.0 - sig) - (_run_sum(e, _tri(False)) + carry) * sig
            if diagonal:
                dz = jnp.where(lax.broadcasted_iota(jnp.int32, (TQ, TQ), 1) < lax.broadcasted_iota(jnp.int32, (TQ, TQ), 0), dz, 0.0)
            dz = dz.astype(bf16)
            k = _heads_of(k_ref[rows, :].astype(bf16))
            add_heads(dk_acc, rows, _bd(dz, q, BTN))
            return dq + _bd(dz, k, BNN), carry + jnp.sum(e, axis=2, keepdims=True)

        dq, _ = sweep_fwd(qi, lax.fori_loop(0, qi, sweep_fwd, (jnp.zeros((HB, TQ, HD), f32), jnp.zeros((HB, TQ, 1), f32))), True)
        for h in range(HB):
            dq_ref[:, h * HD:(h + 1) * HD] = (dq[h] * scale).astype(bf16)

        @pl.when(qi == nq - 1)
        def _():
            dk_ref[...] = dk_acc[...].astype(bf16)
            dv_ref[...] = dv_acc[...].astype(bf16)

    c0, ng = OFF_SB // WB, NH // HB
    return pl.pallas_call(
        body, name=name, grid=(ng, nq),
        in_specs=[pl.BlockSpec((TQ, WB), lambda g, i: (i, c0 + g)), pl.BlockSpec((s, WB), lambda g, i: (0, c0 + ng + g)),
                  pl.BlockSpec((s, WB), lambda g, i: (0, c0 + 2 * ng + g)), pl.BlockSpec((TQ, WB), lambda g, i: (i, g))],
        out_specs=[pl.BlockSpec((TQ, WB), lambda g, i: (i, g)), pl.BlockSpec((s, WB), lambda g, i: (0, g)),
                   pl.BlockSpec((s, WB), lambda g, i: (0, g))],
        out_shape=[_sds((s, WH), bf16)] * 3,
        scratch_shapes=[pltpu.VMEM((nq, HB, TQ, TQ), f32), pltpu.VMEM((nq, HB, TQ, TQ), f32), pltpu.VMEM((s, WB), f32),
                        pltpu.VMEM((s, WB), f32)],
        compiler_params=_cp(("arbitrary", "arbitrary")),
    )(proj, proj, proj, dy)


CB = 256


def _shift_down(v, k, s):
    if k == 0:
        return v
    row = lax.broadcasted_iota(jnp.int32, v.shape, 0)
    return jnp.where(row < k, 0.0, pltpu.roll(v, k, axis=0))


def _shift_up(v, k, s):
    if k == 0:
        return v
    row = lax.broadcasted_iota(jnp.int32, v.shape, 0)
    return jnp.where(row >= s - k, 0.0, pltpu.roll(v, s - k, axis=0))


def conv_fwd(proj, w, name):
    s = proj.shape[0]

    def body(x_ref, w_ref, y_ref):
        x, wv = x_ref[...], w_ref[...]
        y = sum(wv[3 - k:4 - k, :] * _shift_down(x, k, s) for k in range(4))
        y_ref[...] = jax.nn.silu(y)

    return pl.pallas_call(
        body, name=name, grid=(3 * WH // CB,),
        in_specs=[pl.BlockSpec((s, CB), lambda j: (0, OFF_GQ // CB + j)), pl.BlockSpec((4, CB), lambda j: (0, j))],
        out_specs=pl.BlockSpec((s, CB), lambda j: (0, j)), out_shape=_sds((s, 3 * WH), f32),
        compiler_params=_cp(("parallel",)),
    )(proj, w)


def conv_bwd(proj, w, dc, name, after=()):
    s = proj.shape[0]

    def body(*refs):
        x_ref, w_ref, dc_ref, dx_ref, dw_ref = refs[len(after):]
        x, wv = x_ref[...], w_ref[...]
        xs = [_shift_down(x, k, s) for k in range(4)]
        y = sum(wv[3 - k:4 - k, :] * xs[k] for k in range(4))
        sig = jax.nn.sigmoid(y)
        dy = dc_ref[...] * (sig * (1.0 + y * (1.0 - sig)))
        dx_ref[...] = sum(wv[3 - k:4 - k, :] * _shift_up(dy, k, s) for k in range(4)).astype(bf16)
        dw_ref[...] = jnp.concatenate([jnp.sum(dy * xs[3 - i], axis=0, keepdims=True) for i in range(4)], axis=0)

    return pl.pallas_call(
        body, name=name, grid=(3 * WH // CB,),
        in_specs=[pl.BlockSpec(memory_space=pl.ANY)] * len(after)
        + [pl.BlockSpec((s, CB), lambda j: (0, OFF_GQ // CB + j)), pl.BlockSpec((4, CB), lambda j: (0, j)),
           pl.BlockSpec((s, CB), lambda j: (0, j))],
        out_specs=[pl.BlockSpec((s, CB), lambda j: (0, j)), pl.BlockSpec((4, CB), lambda j: (0, j))],
        out_shape=[_sds((s, 3 * WH), bf16), _sds((4, 3 * WH), f32)],
        compiler_params=_cp(("parallel",)),
    )(*after, proj, w, dc)


SOLVE_PASSES = 3


def _pdot_impl(a, b, dn, passes):
    ah, al = _split(a)
    bh, bl = _split(b)
    dot = lambda p, q: lax.dot_general(p, q, dn, preferred_element_type=f32)
    if passes == 1:
        return dot(ah, bh)
    if passes == 2:
        return dot(ah, bh) + dot(ah, bl)
    return dot(ah, bh) + (dot(ah, bl) + dot(al, bh))


BNN, BNT, BTN = (((2,), (1,)), ((0,), (0,))), (((2,), (2,)), ((0,), (0,))), (((1,), (1,)), ((0,), (0,)))


@functools.partial(jax.custom_vjp, nondiff_argnums=(2,))
def _bdot(a, b, passes):
    return _pdot_impl(a, b, BNN, passes)


def _bdot_fwd(a, b, passes):
    return _pdot_impl(a, b, BNN, passes), (a, b)


def _bdot_bwd(passes, res, ct):
    a, b = res
    return _pdot_impl(ct, b, BNT, passes), _pdot_impl(a, ct, BTN, passes)


_bdot.defvjp(_bdot_fwd, _bdot_bwd)


@functools.partial(jax.custom_vjp, nondiff_argnums=(2,))
def _bdot_nt(a, b, passes):
    return _pdot_impl(a, b, BNT, passes)


def _bdot_nt_fwd(a, b, passes):
    return _pdot_impl(a, b, BNT, passes), (a, b)


def _bdot_nt_bwd(passes, res, ct):
    a, b = res
    return _pdot_impl(ct, b, BNN, passes), _pdot_impl(ct, a, BTN, passes)


_bdot_nt.defvjp(_bdot_nt_fwd, _bdot_nt_bwd)


def _lane_pick(v, h):
    lane = lax.broadcasted_iota(jnp.int32, v.shape, v.ndim - 1)
    return jnp.sum(jnp.where(lane == h, v, 0.0), axis=-1, keepdims=True)


def _stack(parts):
    return jnp.concatenate([p[None] for p in parts], axis=0)


def _heads(v, first):
    return _stack([_lane_pick(v, first + h) for h in range(NH)])


def _l2n(v):
    return v * lax.rsqrt(jnp.sum(v * v, axis=-1, keepdims=True) + EPS)


def _dot_nt(a, b):
    return lax.dot_general(a, b, (((1,), (1,)), ((), ())), preferred_element_type=f32)


def _inverse_impl(lower):
    i = lax.broadcasted_iota(jnp.int32, (CH, CH), 0)
    j = lax.broadcasted_iota(jnp.int32, (CH, CH), 1)
    inv = (i == j).astype(f32) - lower
    pw = _pdot_impl(lower, lower, BNN, SOLVE_PASSES)
    for m in range(1, int(math.log2(CH))):
        inv = inv + _pdot_impl(inv, pw, BNN, SOLVE_PASSES)
        if m < int(math.log2(CH)) - 1:
            pw = _pdot_impl(pw, pw, BNN, SOLVE_PASSES)
    return inv


@jax.custom_vjp
def _unit_lower_inverse(lower):
    return _inverse_impl(lower)


def _unit_lower_inverse_fwd(lower):
    inv = _inverse_impl(lower)
    return inv, inv


def _unit_lower_inverse_bwd(inv, ct):
    return (-_pdot_impl(_pdot_impl(inv, ct, BTN, SOLVE_PASSES), inv, BNT, SOLVE_PASSES),)


_unit_lower_inverse.defvjp(_unit_lower_inverse_fwd, _unit_lower_inverse_bwd)


def _gdn_chunk(cq, ck, cv, ab, alog, dtb):
    ones = jnp.ones((NH, CH, HD), f32)
    q = _l2n(cq) * (HD ** -0.5)
    k = _l2n(ck)
    la = -jnp.exp(_heads(alog, 0)) * jax.nn.softplus(_heads(ab, AB_LANE) + _heads(dtb, 0))
    beta = jax.nn.sigmoid(_heads(ab, AB_LANE + NH)) * ones
    i = lax.broadcasted_iota(jnp.int32, (CH, CH), 0)
    j = lax.broadcasted_iota(jnp.int32, (CH, CH), 1)
    incl, strict = j <= i, j < i
    g = _bdot(jnp.broadcast_to(incl.astype(f32), (NH, CH, CH)), la * ones, 2)
    g_row = _stack([g[h].T for h in range(NH)])
    gamma = jnp.where(incl, jnp.exp(jnp.where(incl, g - g_row, 0.0)), 0.0)
    lower = jnp.where(strict, beta * _bdot_nt(k, k, 1) * gamma, 0.0)
    inv = _unit_lower_inverse(lower)
    eg = jnp.exp(g)
    u = _bdot(inv, cv * beta, SOLVE_PASSES)
    w = _bdot(inv, k * (beta * eg), SOLVE_PASSES)
    qk = _bdot_nt(q, k, 1) * gamma
    g_last = g[:, CH - 1:CH, :]
    return u, w, q * eg, k * jnp.exp(g_last - g), qk, jnp.exp(g_last)


def _by_head(ref, t=0):
    return _stack([ref[:, t * WH + h * HD:t * WH + (h + 1) * HD] for h in range(NH)])


def gdn_a_fwd(c, proj, alog, dtb, name):
    s = c.shape[0]
    nc = s // CH

    def body(c_ref, ab_ref, al_ref, dt_ref, u_ref, w_ref, qd_ref, kd_ref, qk_ref, dec_ref):
        res = _gdn_chunk(_by_head(c_ref, 0), _by_head(c_ref, 1), _by_head(c_ref, 2), ab_ref[...], al_ref[...], dt_ref[...])
        for h in range(NH):
            sl = slice(h * HD, (h + 1) * HD)
            for r, v in zip((u_ref, w_ref, qd_ref, kd_ref, qk_ref), res[:5]):
                r[:, sl] = v[h].astype(r.dtype)
            dec_ref[:, sl] = jnp.broadcast_to(res[5][h], (8, HD))

    row = pl.BlockSpec((CH, WH), lambda n: (n, 0))
    par = pl.BlockSpec((1, HD), lambda n: (0, 0))
    return pl.pallas_call(
        body, name=name, grid=(nc,),
        in_specs=[pl.BlockSpec((CH, 3 * WH), lambda n: (n, 0)), pl.BlockSpec((CH, HD), lambda n: (n, OFF_AB // HD)), par, par],
        out_specs=[row] * 5 + [pl.BlockSpec((None, 8, WH), lambda n: (n, 0, 0))],
        out_shape=[_sds((s, WH), f32)] + [_sds((s, WH), bf16)] * 4 + [_sds((nc, 8, WH), f32)],
        compiler_params=_cp(("parallel",)),
    )(c, proj, alog, dtb)


def gdn_a_bwd(c, proj, alog, dtb, cots, ddec, name, swap=()):
    s = c.shape[0]
    nc = s // CH
    gs, axes = swap if swap else ((), ())
    ns = len(gs)
    shards = [tuple(d // (NDEV if a == ax else 1) for a, d in enumerate(g.shape)) for g, ax in zip(gs, axes)]

    def body(c_ref, ab_ref, al_ref, dt_ref, du_ref, dw_ref, dqd_ref, dkd_ref, dqk_ref, ddec_ref, *rest):
        g_refs, (dc_ref, dab_ref, dal_ref, ddt_ref), p_refs = rest[:ns], rest[ns:ns + 4], rest[ns + 4:2 * ns + 4]
        n = pl.program_id(0)

        def copies():
            return _swap_copies(g_refs, p_refs, *rest[2 * ns + 4:], axes)

        if ns:
            @pl.when(n == 0)
            def _():
                for cp in copies():
                    cp.start()

        _, vjp = jax.vjp(_gdn_chunk, _by_head(c_ref, 0), _by_head(c_ref, 1), _by_head(c_ref, 2), ab_ref[...], al_ref[...],
                         dt_ref[...])
        lane = lax.broadcasted_iota(jnp.int32, (1, HD), 1)
        dd = _stack([jnp.where(lane == 0, ddec_ref[0:1, h * HD:(h + 1) * HD], 0.0) for h in range(NH)])
        dcq, dck, dcv, dab, dal, ddt = vjp(tuple(_by_head(r) for r in (du_ref, dw_ref, dqd_ref, dkd_ref, dqk_ref)) + (dd,))
        for h in range(NH):
            for t, v in enumerate((dcq, dck, dcv)):
                dc_ref[:, t * WH + h * HD:t * WH + (h + 1) * HD] = v[h]
        dab_ref[...] = dab.astype(bf16)

        @pl.when(n == 0)
        def _():
            dal_ref[...] = dal
            ddt_ref[...] = ddt

        @pl.when(n > 0)
        def _():
            dal_ref[...] += dal
            ddt_ref[...] += ddt

        if ns:
            @pl.when(n == nc - 1)
            def _():
                cps = copies()
                for cp in cps:
                    cp.wait_recv()
                for cp in cps:
                    cp.wait_send()

    row = pl.BlockSpec((CH, WH), lambda n: (n, 0))
    wide = pl.BlockSpec((CH, 3 * WH), lambda n: (n, 0))
    par = pl.BlockSpec((1, HD), lambda n: (0, 0))
    res = pl.pallas_call(
        body, name=name, grid=(nc,),
        in_specs=[wide, pl.BlockSpec((CH, HD), lambda n: (n, OFF_AB // HD)), par, par] + [row] * 5
        + [pl.BlockSpec((None, 8, WH), lambda n: (n, 0, 0))] + [ANY] * ns,
        out_specs=[wide, pl.BlockSpec((CH, HD), lambda n: (n, 0)), par, par] + [ANY] * ns,
        out_shape=[_sds((s, 3 * WH), f32), _sds((s, HD), bf16), _sds((1, HD), f32), _sds((1, HD), f32)]
        + [_sds((4,) + sh, g.dtype) for sh, g in zip(shards, gs)],
        scratch_shapes=[pltpu.SemaphoreType.DMA((ns, 4)), pltpu.SemaphoreType.DMA((ns, 4))] if ns else [],
        compiler_params=_cp(("arbitrary",)),
    )(c, proj, alog, dtb, *cots, ddec, *gs)
    return tuple(res[:4]) + (list(res[4:]),)


def gdn_b_fwd(u, w, qd, kd, qk, dec, name):
    s = u.shape[0]
    nc = s // CH

    def body(u_ref, w_ref, qd_ref, kd_ref, qk_ref, dec_ref, o_ref, st_ref, state):
        n = pl.program_id(0)

        @pl.when(n == 0)
        def _():
            state[...] = jnp.zeros_like(state)

        for h in range(NH):
            sl = slice(h * HD, (h + 1) * HD)
            st = state[sl, :]
            st_ref[sl, :] = st
            sb = st.astype(bf16)
            vn = u_ref[:, sl] - jnp.dot(w_ref[:, sl].astype(bf16), sb, preferred_element_type=f32)
            vb = vn.astype(bf16)
            o_ref[:, sl] = (jnp.dot(qd_ref[:, sl].astype(bf16), sb, preferred_element_type=f32)
                            + jnp.dot(qk_ref[:, sl].astype(bf16), vb, preferred_element_type=f32))
            state[sl, :] = st * dec_ref[0:1, sl] + lax.dot_general(
                kd_ref[:, sl].astype(bf16), vb, (((0,), (0,)), ((), ())), preferred_element_type=f32)

    row = pl.BlockSpec((CH, WH), lambda n: (n, 0))
    return pl.pallas_call(
        body, name=name, grid=(nc,),
        in_specs=[row] * 5 + [pl.BlockSpec((None, 8, WH), lambda n: (n, 0, 0))],
        out_specs=[row, pl.BlockSpec((None, WH, HD), lambda n: (n, 0, 0))],
        out_shape=[_sds((s, WH), f32), _sds((nc, WH, HD), f32)],
        scratch_shapes=[pltpu.VMEM((WH, HD), f32)],
        compiler_params=_cp(("arbitrary",)),
    )(u, w, qd, kd, qk, dec)


def gdn_b_bwd(u, w, qd, kd, qk, dec, states, do, name):
    s = u.shape[0]
    nc = s // CH

    def body(u_ref, w_ref, qd_ref, kd_ref, qk_ref, dec_ref, st_ref, do_ref,
             du_ref, dw_ref, dqd_ref, dkd_ref, dqk_ref, ddec_ref, dstate):
        n = pl.program_id(0)

        @pl.when(n == 0)
        def _():
            dstate[...] = jnp.zeros_like(dstate)

        for h in range(NH):
            sl = slice(h * HD, (h + 1) * HD)
            st, ds = st_ref[sl, :], dstate[sl, :]
            sb, dsb = st.astype(bf16), ds.astype(bf16)
            wb, qdb, kdb, qkb = (r[:, sl].astype(bf16) for r in (w_ref, qd_ref, kd_ref, qk_ref))
            dob = do_ref[:, sl].astype(bf16)
            vn = u_ref[:, sl] - jnp.dot(wb, sb, preferred_element_type=f32)
            vb = vn.astype(bf16)
            dvn = (lax.dot_general(qkb, dob, (((0,), (0,)), ((), ())), preferred_element_type=f32)
                   + jnp.dot(kdb, dsb, preferred_element_type=f32))
            dvb = dvn.astype(bf16)
            du_ref[:, sl] = dvn
            dw_ref[:, sl] = -_dot_nt(dvb, sb)
            dqd_ref[:, sl] = _dot_nt(dob, sb)
            dkd_ref[:, sl] = _dot_nt(vb, dsb)
            dqk_ref[:, sl] = _dot_nt(dob, vb)
            tot = jnp.sum(jnp.sum(ds * st, axis=1, keepdims=True), axis=0, keepdims=True)
            ddec_ref[:, sl] = jnp.broadcast_to(tot, (8, HD))
            dstate[sl, :] = (ds * dec_ref[0:1, sl]
                             + lax.dot_general(qdb, dob, (((0,), (0,)), ((), ())), preferred_element_type=f32)
                             - lax.dot_general(wb, dvb, (((0,), (0,)), ((), ())), preferred_element_type=f32))

    row = pl.BlockSpec((CH, WH), lambda n: (nc - 1 - n, 0))
    small = pl.BlockSpec((None, 8, WH), lambda n: (nc - 1 - n, 0, 0))
    return pl.pallas_call(
        body, name=name, grid=(nc,),
        in_specs=[row] * 5 + [small, pl.BlockSpec((None, WH, HD), lambda n: (nc - 1 - n, 0, 0)), row],
        out_specs=[row] * 5 + [small],
        out_shape=[_sds((s, WH), f32)] * 5 + [_sds((nc, 8, WH), f32)],
        scratch_shapes=[pltpu.VMEM((WH, HD), f32)],
        compiler_params=_cp(("arbitrary",)),
    )(u, w, qd, kd, qk, dec, states, do)


ANY = pl.BlockSpec(memory_space=pl.ANY)


def _dev_index(p):
    return 4 * p[0] + 2 * p[1] + p[2]


def _shard_of(ref, axis, size, idx):
    return ref.at[pl.ds(idx * size, size), :] if axis == 0 else ref.at[:, pl.ds(idx * size, size)]


def _peer(x, y, c, r):
    return (1 - x if r & 4 else x, 1 - y if r & 2 else y, 1 - c if r & 1 else c)


ALL_PEERS, OTHER_CHIPS = tuple(range(1, NDEV)), (4, 2, 6)


def _launch(body, ins, out_sds, name, sequencer_id, relations=ALL_PEERS, kinds=7):
    n = len(ins)
    sems = [pltpu.SemaphoreType.DMA((n, kinds)), pltpu.SemaphoreType.DMA((n, kinds)), pltpu.SemaphoreType.DMA((n,))]
    if sequencer_id is None:
        return pl.pallas_call(
            lambda *refs: body(refs[:n], refs[n:2 * n], *refs[2 * n:]), name=name, in_specs=[ANY] * n, out_specs=[ANY] * n,
            out_shape=out_sds, scratch_shapes=sems, compiler_params=pltpu.CompilerParams(has_side_effects=True),
        )(*ins)
    in_refs = [jax.new_ref(a, memory_space=pltpu.MemorySpace.HBM) for a in ins]
    out_refs = [jax.empty_ref(sd, memory_space=pltpu.MemorySpace.HBM) for sd in out_sds]

    @pl.kernel(mesh=plsc.ScalarSubcoreMesh(axis_name="sequencer", num_cores=1), name=name, scratch_types=sems,
               compiler_params=pltpu.CompilerParams(collective_id=sequencer_id))
    def launch(send, recv, loc):
        x, y, c = lax.axis_index("x"), lax.axis_index("y"), lax.axis_index("c")
        barrier = pltpu.get_barrier_semaphore()
        for r in relations:
            pl.semaphore_signal(barrier, inc=1, device_id=_peer(x, y, c, r), device_id_type=MESH)
        pl.semaphore_wait(barrier, len(relations))
        body(in_refs, out_refs, send, recv, loc)

    launch()
    return [r[...] for r in out_refs]


def all_gather(xs, axes, name, sequencer_id=None):
    n = len(xs)
    fulls = [tuple(d * (NDEV if a == ax else 1) for a, d in enumerate(x.shape)) for x, ax in zip(xs, axes)]
    halved = [x.shape[0] % 32 == 0 for x in xs]

    def body(x_refs, o_refs, send, recv, loc):
        x, y, c = lax.axis_index("x"), lax.axis_index("y"), lax.axis_index("c")
        me, sib = (x, y, c), (x, y, 1 - c)
        xn, yn, dg = (1 - x, y), (x, 1 - y), (1 - x, 1 - y)

        def part(t, p, half=None):
            ref = _shard_of(o_refs[t], axes[t], xs[t].shape[axes[t]], _dev_index(p))
            rows = xs[t].shape[0] // 2
            return ref if half is None else ref.at[pl.ds(half * rows, rows), :]

        def copy(t, k, block, to, half=None, src=None):
            return pltpu.make_async_remote_copy(
                src_ref=part(t, block, half) if src is None else src, dst_ref=part(t, block, half),
                send_sem=send.at[t, k], recv_sem=recv.at[t, k], device_id=to, device_id_type=MESH)

        mine = [pltpu.make_async_copy(x_refs[t], part(t, me), loc.at[t]) for t in range(n)]
        for cp in mine:
            cp.start()
        sends = []
        for t in range(n):
            sends += [copy(t, 0, me, sib, src=x_refs[t]), copy(t, 1, me, (*xn, c), src=x_refs[t]),
                      copy(t, 2, me, (*yn, c), src=x_refs[t])]
            if not halved[t]:
                sends.append(copy(t, 3, me, (*dg, c), src=x_refs[t]))
        for cp in sends:
            cp.start()

        def pass_on(cp):
            cp.start()
            sends.append(cp)

        for t in range(n):
            h0, h1 = (0, 1) if halved[t] else (None, None)
            copy(t, 1, (*xn, c), me).wait_recv()
            if halved[t]:
                pass_on(copy(t, 3, (*xn, c), (*yn, c), 0))
            pass_on(copy(t, 5, (*xn, c), sib))
            copy(t, 2, (*yn, c), me).wait_recv()
            if halved[t]:
                pass_on(copy(t, 4, (*yn, c), (*xn, c), 1))
            pass_on(copy(t, 6, (*yn, c), sib))
            copy(t, 3, (*dg, c), me, h0).wait_recv()
            pass_on(copy(t, 7, (*dg, c), sib, h0))
            if halved[t]:
                copy(t, 4, (*dg, c), me, h1).wait_recv()
                pass_on(copy(t, 8, (*dg, c), sib, h1))
        for t in range(n):
            h0, h1 = (0, 1) if halved[t] else (None, None)
            copy(t, 0, sib, me).wait_recv()
            copy(t, 5, (*xn, 1 - c), me).wait_recv()
            copy(t, 6, (*yn, 1 - c), me).wait_recv()
            copy(t, 7, (*dg, 1 - c), me, h0).wait_recv()
            if halved[t]:
                copy(t, 8, (*dg, 1 - c), me, h1).wait_recv()
        for cp in sends:
            cp.wait_send()
        for cp in mine:
            cp.wait()

    return _launch(body, xs, [_sds(f, x.dtype) for f, x in zip(fulls, xs)], name, sequencer_id, kinds=9)


def pair_add(g, p, axis, name, after=()):
    _, rows, cols = p.shape
    tr = _row_tile(rows, cols, 1 << 20)
    nb = rows // tr
    if axis == 0:
        g_spec = pl.BlockSpec((tr, cols), lambda j, i, c: ((2 * j + c[0]) * nb + i, 0))
    else:
        g_spec = pl.BlockSpec((tr, cols), lambda j, i, c: (i, 2 * j + c[0]))
    blk = pl.BlockSpec((None, tr, cols), lambda j, i, c: (j, i, 0))

    na = len(after)

    def body(c_ref, *refs):
        g_ref, p_ref, q_ref = refs[na:]
        q_ref[...] = (g_ref[...].astype(f32) + p_ref[...].astype(f32)).astype(bf16)

    return pl.pallas_call(
        body, name=name, out_shape=_sds(p.shape, bf16),
        grid_spec=pltpu.PrefetchScalarGridSpec(num_scalar_prefetch=1, grid=(4, nb), in_specs=[ANY] * na + [g_spec, blk],
                                               out_specs=blk),
        compiler_params=_cp(("parallel", "parallel")),
    )(lax.axis_index("c").astype(jnp.int32).reshape(1), *after, g, p)


def chip_exchange(qs, name, sequencer_id):
    n = len(qs)

    def body(q_refs, r_refs, send, recv, loc):
        x, y, c = lax.axis_index("x"), lax.axis_index("y"), lax.axis_index("c")
        my_chip = 2 * x + y
        peers = [_peer(x, y, c, r) for r in OTHER_CHIPS]
        mine = [pltpu.make_async_copy(q_refs[t].at[my_chip], r_refs[t].at[my_chip], loc.at[t]) for t in range(n)]
        for cp in mine:
            cp.start()

        def copy(t, k, slot):
            p = peers[k]
            return pltpu.make_async_remote_copy(
                src_ref=q_refs[t].at[2 * p[0] + p[1]], dst_ref=r_refs[t].at[slot], send_sem=send.at[t, k], recv_sem=recv.at[t, k],
                device_id=p, device_id_type=MESH)

        sends = [copy(t, k, my_chip) for t in range(n) for k in range(3)]
        for cp in sends:
            cp.start()
        for t in range(n):
            for k in range(3):
                copy(t, k, 2 * peers[k][0] + peers[k][1]).wait_recv()
        for cp in sends:
            cp.wait_send()
        for cp in mine:
            cp.wait()

    return _launch(body, qs, [_sds(q.shape, q.dtype) for q in qs], name, sequencer_id, OTHER_CHIPS)


def _adamw(w, g, m, v):
    m = ADAM_B1 * m + (1.0 - ADAM_B1) * g
    v = ADAM_B2 * v + (1.0 - ADAM_B2) * jnp.square(g)
    m_hat = m / (1.0 - ADAM_B1 ** ADAM_STEP)
    v_hat = v / (1.0 - ADAM_B2 ** ADAM_STEP)
    return -ADAM_LR * (m_hat / (jnp.sqrt(v_hat) + ADAM_EPS) + ADAM_WD * w), m, v


def _sum8(r_ref):
    g = r_ref[0].astype(f32)
    for j in range(1, r_ref.shape[0]):
        g = g + r_ref[j].astype(f32)
    return g


def _row_tile(rows, cols, elems=1 << 18):
    tr = min(rows, max(8, 1 << int(math.log2(elems / cols))))
    while rows % tr:
        tr //= 2
    assert tr % 8 == 0 or tr == rows, (rows, cols)
    return tr


def sum_partials(r, name, after=()):
    _, rows, cols = r.shape
    tr = _row_tile(rows, cols)
    na = len(after)

    def body(*refs):
        refs[na + 1][...] = _sum8(refs[na])

    return pl.pallas_call(
        body, name=name, grid=(rows // tr,),
        in_specs=[ANY] * na + [pl.BlockSpec((r.shape[0], tr, cols), lambda i: (0, i, 0))],
        out_specs=pl.BlockSpec((tr, cols), lambda i: (i, 0)), out_shape=_sds((rows, cols), f32),
        compiler_params=_cp(("parallel",)),
    )(*after, r)


def adamw_t(w, m, v, grads, name):
    rows, nl, cols = w.shape
    tr = min(rows, (1 << 16) // cols)
    blk = pl.BlockSpec((tr, nl, cols), lambda i: (i, 0, 0))
    flat = pl.BlockSpec((tr, cols), lambda i: (i, 0))

    def body(w_ref, m_ref, v_ref, *rest):
        g_refs, (g_ref, d_ref, nm_ref, nv_ref) = rest[:nl], rest[nl:]
        for l in range(nl):
            grad = g_refs[l][...]
            g_ref[:, l, :] = grad
            d_ref[:, l, :], nm_ref[:, l, :], nv_ref[:, l, :] = _adamw(w_ref[:, l, :], grad, m_ref[:, l, :], v_ref[:, l, :])

    return tuple(pl.pallas_call(
        body, name=name, grid=(pl.cdiv(rows, tr),), in_specs=[blk] * 3 + [flat] * nl, out_specs=[blk] * 4,
        out_shape=[_sds(w.shape, f32)] * 4, compiler_params=_cp(("parallel",)),
    )(w, m, v, *grads))


def adamw(w, m, v, layer, name, r=None, g=None, prev=None):
    _, rows, cols = w.shape
    tr = _row_tile(rows, cols)
    blk = pl.BlockSpec((None, tr, cols), lambda i: (layer, i, 0))
    nprev = 0 if prev is None else 4

    def body(w_ref, m_ref, v_ref, src_ref, *rest):
        g_ref, d_ref, nm_ref, nv_ref, token_ref = rest[nprev:]
        grad = _sum8(src_ref) if g is None else src_ref[...]
        g_ref[...] = grad
        d_ref[...], nm_ref[...], nv_ref[...] = _adamw(w_ref[...], grad, m_ref[...], v_ref[...])
        token_ref[...] = jnp.zeros_like(token_ref)

    src, src_spec = (r, pl.BlockSpec((r.shape[0], tr, cols), lambda i: (0, i, 0))) if g is None else (g, pl.BlockSpec((tr, cols), lambda i: (i, 0)))
    *outs, token = pl.pallas_call(
        body, name=name, grid=(rows // tr,), in_specs=[blk] * 3 + [src_spec] + [ANY] * nprev,
        out_specs=[blk] * 4 + [pl.BlockSpec((8, HD), lambda i: (0, 0))],
        out_shape=[_sds(w.shape, f32)] * 4 + [_sds((8, HD), f32)], input_output_aliases={4 + k: k for k in range(nprev)},
        compiler_params=_cp(("arbitrary",)),
    )(w, m, v, src, *(prev or ()))
    return tuple(outs), token


def small_adamw(parts, w, m, v, name, after=()):
    def body(*refs):
        p_ref, w_ref, m_ref, v_ref, g_ref, d_ref, nm_ref, nv_ref = refs[len(after):]
        g = _sum8(p_ref)
        g_ref[...] = g
        d_ref[...], nm_ref[...], nv_ref[...] = _adamw(w_ref[...], g, m_ref[...], v_ref[...])

    vmem = pl.BlockSpec(memory_space=pltpu.VMEM)
    return pl.pallas_call(body, name=name, out_shape=[_sds(w.shape, f32)] * 4, in_specs=[ANY] * len(after) + [vmem] * 4,
                          out_specs=[vmem] * 4, compiler_params=_cp())(*after, parts, w, m, v)


def _pack_rows(wt):
    tail = jnp.pad(wt[5888:N_IN], ((OFF_G - 12 - (OFF_P + 512), 0), (0, 0)))
    return jnp.concatenate([wt[512:2816], wt[2816:5120], wt[5120:5888], wt[0:512], tail], axis=0)


def _unpack_rows(g):
    return jnp.concatenate([g[OFF_P:OFF_P + 512], g[OFF_SB:OFF_SB + 2304], g[OFF_GQ:OFF_GQ + 2304], g[OFF_Z:OFF_Z + 768],
                            g[OFF_G - 12:NP]], axis=0)


def _lanes(v):
    flat = v.reshape(-1)
    n = -(-flat.shape[0] // HD) * HD
    return jnp.pad(flat, (0, n - flat.shape[0])).reshape(n // HD, HD)


def _lanes8(v):
    rows = _lanes(v)
    return jnp.pad(rows, ((0, -rows.shape[0] % 8), (0, 0)))


def _layer_fwd(x, p, l):
    nm = lambda s: f"{s}_l{l}"
    u = rms_fwd(x, p["attn_norm"], nm("rms1"))
    proj = matmul(u, p["w_in"][0], name=nm("inproj"), tb=True)
    y_pool = pool_fwd(proj, p["pool_w"], p["pool_scale"], nm("pool"))
    y_sb = sb_fwd(proj, nm("sb"))
    c = conv_fwd(proj, p["conv"], nm("conv"))
    ga = gdn_a_fwd(c, proj, p["alog"], p["dtb"], nm("gdna"))
    o, states = gdn_b_fwd(*ga, nm("gdnb"))
    y_gdn = gdn_out_fwd(o, proj, p["gdn_norm"], nm("gdno"))
    gates = matmul(u, p["w_in"][1], name=nm("inproj_gates"), tb=True, out_dtype=bf16, after=(y_gdn,))
    ups = [matmul(y, p[k], name=nm(k), out_dtype=bf16)
           for y, k in ((y_pool, "w_pool_up"), (y_sb, "w_sb_up"), (y_gdn, "w_gdn_up"))]
    merged = merge_fwd(gates, ups, nm("merge"))
    x1 = matmul(merged, p["w_out"], name=nm("outproj"), epilogue=lambda acc, r: acc + r, extras=(x,))
    u2 = rms_fwd(x1, p["mlp_norm"], nm("rms2"))
    h2 = matmul(u2, p["w_ff1"], name=nm("ff1"), out_dtype=bf16, epilogue=lambda acc: jnp.square(jnp.maximum(acc, 0.0)))
    x2 = matmul(h2, p["w_ff2"], name=nm("ff2"), epilogue=lambda acc, r: acc + r, extras=(x1,))
    saved = dict(x=x, u=u, proj=proj, gates=gates, y_pool=y_pool, y_sb=y_sb, c=c, ga=ga, o=o, states=states, y_gdn=y_gdn, ups=ups,
                 merged=merged, x1=x1, u2=u2, h2=h2)
    return x2, saved


def _layer_bwd(dx2, dx2b, sv, p, l, finish):
    nm = lambda s: f"{s}_l{l}"
    s = dx2.shape[0]
    dh = matmul(dx2b, p["w_ff2"], name=nm("d_ff2_x"), tb=True, out_dtype=bf16,
                epilogue=lambda acc, h2: acc * (2.0 * jnp.sqrt(h2.astype(f32))), extras=(sv["h2"],))
    g_ff2 = matmul(sv["h2"], dx2b, name=nm("d_ff2_w"), ta=True, out_dtype=bf16)
    du2 = matmul(dh, p["w_ff1"], name=nm("d_ff1_x"), tb=True)
    g_ff1 = matmul(sv["u2"], dh, name=nm("d_ff1_w"), ta=True, out_dtype=bf16)
    dx1, dx1b, d_mlp_norm = rms_bwd(sv["x1"], du2, dx2, p["mlp_norm"], nm("d_rms2"))
    dmerged = matmul(dx1b, p["w_out"], name=nm("d_out_x"), tb=True, out_dtype=bf16)
    g_out = matmul(sv["merged"], dx1b, name=nm("d_out_w"), ta=True, out_dtype=bf16)
    dgates, *dups = merge_bwd(sv["gates"], sv["ups"], dmerged, nm("d_merge"))
    dys, g_ups = [], []
    for dup, y, k in zip(dups, (sv["y_pool"], sv["y_sb"], sv["y_gdn"]), ("w_pool_up", "w_sb_up", "w_gdn_up")):
        dys.append(matmul(dup, p[k], name=nm("d_" + k + "_x"), tb=True, out_dtype=bf16))
        g_ups.append(matmul(y, dup, name=nm("d_" + k + "_w"), ta=True, out_dtype=bf16))
    early = g_ups + [g_out, g_ff1, g_ff2]
    do, dz, d_gdn_norm = gdn_out_bwd(sv["o"], sv["proj"], p["gdn_norm"], dys[2], nm("d_gdno"))
    cots = gdn_b_bwd(*sv["ga"], sv["states"], do, nm("d_gdnb"))
    dc, dab, d_alog, d_dtb, swapped = gdn_a_bwd(sv["c"], sv["proj"], p["alog"], p["dtb"], cots[:5], cots[5], nm("d_gdna"),
                                                (early, BIG_AXES[1:]))
    r_early, sent_early = finish(early, swapped, BIG_AXES[1:], BIG_NAMES[1:], "a", (dab,))
    dgq, d_conv = conv_bwd(sv["proj"], p["conv"], dc, nm("d_conv"), sent_early)
    dq, dk, dv = sb_bwd(sv["proj"], dys[1], nm("d_sb"))
    dp, d_pool_w, d_pool_scale = pool_bwd(sv["proj"], p["pool_w"], p["pool_scale"], dys[0], nm("d_pool"))
    dproj = jnp.concatenate([dq, dk, dv, dgq, dz, dp, jnp.zeros((s, OFF_AB - OFF_P - W_POOL), bf16), dab], axis=1)
    g_in = [matmul(dproj, sv["u"], name=nm("d_in_w"), ta=True, out_dtype=bf16),
            matmul(dgates, sv["u"], name=nm("d_in_w_gates"), ta=True, out_dtype=bf16)]
    du, swapped = matmul(dproj, p["w_in"][0], name=nm("d_in_x"), swap=(g_in, (1, 1)))
    r_in, sent_in = finish(g_in, swapped, (1, 1), ("w_in", "w_in_gates"), "b", ())
    du = matmul(dgates, p["w_in"][1], name=nm("d_in_x_gates"), epilogue=lambda acc, r: acc + r, extras=(du,), after=sent_in)
    dx, dxb, d_attn_norm = rms_bwd(sv["x"], du, dx1, p["attn_norm"], nm("d_rms1"))
    recv = r_in + r_early
    small = [d_attn_norm, d_pool_w, d_pool_scale, d_conv, d_alog, d_dtb, d_gdn_norm, d_mlp_norm]
    return dx, dxb, recv, small


BIG_AXES = (1, 1, 1, 1, 0, 1, 0)
GATHER_ID, EXCHANGE_ID = 1, 2
BIG_NAMES = ("w_in", "w_pool_up", "w_sb_up", "w_gdn_up", "w_out", "w_ff1", "w_ff2")


def kernel(x, attn_norm, w_in, pool_w, pool_scale, gdn_conv, gdn_a_log, gdn_dt_bias, gdn_norm, w_pool_up, w_sb_up, w_gdn_up, w_out, mlp_norm, w_ff1, w_ff2, final_norm, loss_target, m_attn_norm, m_w_in, m_pool_w, m_pool_scale, m_gdn_conv, m_gdn_a_log, m_gdn_dt_bias, m_gdn_norm, m_w_pool_up, m_w_sb_up, m_w_gdn_up, m_w_out, m_mlp_norm, m_w_ff1, m_w_ff2, m_final_norm, v_attn_norm, v_w_in, v_pool_w, v_pool_scale, v_gdn_conv, v_gdn_a_log, v_gdn_dt_bias, v_gdn_norm, v_w_pool_up, v_w_sb_up, v_w_gdn_up, v_w_out, v_mlp_norm, v_w_ff1, v_w_ff2, v_final_norm):
    s = x.shape[1]
    me = _dev_index((lax.axis_index("x"), lax.axis_index("y"), lax.axis_index("c")))
    ncv = gdn_conv.shape[2]

    w_in_t, m_w_in_t, v_w_in_t = (jnp.transpose(a, (2, 0, 1)) for a in (w_in, m_w_in, v_w_in))
    full, prev = [], None
    for l in range(NL):
        packed_in = _pack_rows(w_in_t[:, l]).astype(bf16)
        groups = [[packed_in[:OFF_G]] + ([gdn_conv.reshape(NL * 4, ncv)] if l == 0 else []),
                  [packed_in[OFF_G:]],
                  [w_pool_up[l].astype(bf16), w_sb_up[l].astype(bf16), w_gdn_up[l].astype(bf16), w_out[l].astype(bf16)],
                  [w_ff1[l].astype(bf16), w_ff2[l].astype(bf16)]]
        axes = [(1, 0)[:len(groups[0])], (1,), BIG_AXES[1:5], BIG_AXES[5:]]
        got = []
        for k, (group, ax) in enumerate(zip(groups, axes)):
            if prev is not None:
                group, _ = lax.optimization_barrier((group, prev))
            prev = group[0]
            got.append(all_gather(group, ax, f"gather_{k}_l{l}", sequencer_id=GATHER_ID))
        full.append([(got[0][0], got[1][0])] + got[2] + got[3])
        if l == 0:
            conv_full = got[0][1].reshape(NDEV, NL, 4, ncv).transpose(1, 2, 0, 3).reshape(NL, 4, NDEV * ncv)
    params = []
    for l in range(NL):
        p = dict(zip(("w_in", "w_pool_up", "w_sb_up", "w_gdn_up", "w_out", "w_ff1", "w_ff2"), full[l][:7]))
        p.update(attn_norm=attn_norm[l][None], mlp_norm=mlp_norm[l][None], pool_w=pool_w[l], pool_scale=pool_scale[l][None],
                 conv=conv_full[l], alog=_lanes(gdn_a_log[l]), dtb=_lanes(gdn_dt_bias[l]), gdn_norm=gdn_norm[l][None])
        params.append(p)

    h = x[0]
    saved = []
    for l in range(NL):
        h, sv = _layer_fwd(h, params[l], l)
        saved.append(sv)
    dh, dhb, loss_row, d_final = loss_head(h, loss_target[0], final_norm[None], "loss_head")
    recv, smalls = [None] * NL, [None] * NL
    for l in reversed(range(NL)):
        def finish(gs, ps, axes, names, tag, after, l=l):
            qs = [pair_add(g, p, ax, f"pair_add_{k}_l{l}", after) for g, p, ax, k in zip(gs, ps, axes, names)]
            return chip_exchange(qs, f"exchange_{tag}_l{l}", EXCHANGE_ID), tuple(qs)

        dh, dhb, recv[l], smalls[l] = _layer_bwd(dh, dhb, saved[l], params[l], l, finish)

    small_rows = [_lanes8(t) for l in range(NL) for t in smalls[l]] + [_lanes8(d_final), _lanes8(loss_row)]
    packed = jnp.concatenate(small_rows, axis=0)
    parts = all_gather([packed], (0,), "gather_small", sequencer_id=GATHER_ID)[0].reshape(NDEV, packed.shape[0], HD)

    def pack_small(tree):
        rows = []
        for l in range(NL):
            rows += [_lanes8(tree["attn_norm"][l]), _lanes8(tree["pool_w"][l]), _lanes8(tree["pool_scale"][l]),
                     jnp.zeros((4 * NDEV * ncv // HD, HD), f32), _lanes8(tree["gdn_a_log"][l]), _lanes8(tree["gdn_dt_bias"][l]),
                     _lanes8(tree["gdn_norm"][l]), _lanes8(tree["mlp_norm"][l])]
        rows += [_lanes8(tree["final_norm"]), jnp.zeros((8, HD), f32)]
        return jnp.concatenate(rows, axis=0)

    names = ("attn_norm", "pool_w", "pool_scale", "gdn_a_log", "gdn_dt_bias", "gdn_norm", "mlp_norm", "final_norm")
    w_small = pack_small(dict(zip(names, (attn_norm, pool_w, pool_scale, gdn_a_log, gdn_dt_bias, gdn_norm, mlp_norm, final_norm))))
    m_small = pack_small(dict(zip(names, (m_attn_norm, m_pool_w, m_pool_scale, m_gdn_a_log, m_gdn_dt_bias, m_gdn_norm, m_mlp_norm, m_final_norm))))
    v_small = pack_small(dict(zip(names, (v_attn_norm, v_pool_w, v_pool_scale, v_gdn_a_log, v_gdn_dt_bias, v_gdn_norm, v_mlp_norm, v_final_norm))))
    def unpack_small(buf):
        out, conv_g, r = {}, [], 0
        layer_items = (("attn_norm", (D,)), ("pool_w", (4, HD, HD)), ("pool_scale", (W_POOL,)), ("conv", (4, NDEV * ncv)),
                       ("gdn_a_log", (NH,)), ("gdn_dt_bias", (NH,)), ("gdn_norm", (HD,)), ("mlp_norm", (D,)))
        per_layer = {k: [] for k, _ in layer_items}
        for l in range(NL):
            for k, shape in layer_items:
                size = math.prod(shape)
                nrow = -(-size // (8 * HD)) * 8
                per_layer[k].append(buf[r:r + nrow].reshape(-1)[:size].reshape(shape))
                r += nrow
        for k, _ in layer_items:
            out[k] = jnp.stack(per_layer[k])
        out["final_norm"] = buf[r:r + D // HD].reshape(D)
        out["loss"] = buf[r + D // HD, 0]
        return out

    big_out = {}
    big_names = BIG_NAMES
    big_w = dict(zip(big_names, ((w_in_t, m_w_in_t, v_w_in_t), (w_pool_up, m_w_pool_up, v_w_pool_up), (w_sb_up, m_w_sb_up, v_w_sb_up),
                                 (w_gdn_up, m_w_gdn_up, v_w_gdn_up), (w_out, m_w_out, v_w_out), (w_ff1, m_w_ff1, v_w_ff1),
                                 (w_ff2, m_w_ff2, v_w_ff2))))
    tokens = []
    for l, k in [(l, k) for l in reversed(range(NL)) for k in big_names[1:]]:
        w, m, v = big_w[k]
        big_out[k], token = adamw(w, m, v, l, f"adamw_{k}_l{l}", r=recv[l][big_names.index(k) + 1], prev=big_out.get(k))
        tokens.append(token)
    g_in = []
    for l in range(NL):
        halves = [sum_partials(recv[l][h], f"sum_w_in_{h}_l{l}", tuple(tokens) if l == 0 else ()) for h in range(2)]
        g_in.append(_unpack_rows(jnp.concatenate(halves, axis=0)))
    big_out["w_in"] = adamw_t(*big_w["w_in"], g_in, "adamw_w_in")

    small_out = small_adamw(parts, w_small, m_small, v_small, "adamw_small", after=(big_out["w_in"][1],))
    sm = [unpack_small(b) for b in small_out]
    loss = sm[0]["loss"]
    g_conv = lax.dynamic_slice_in_dim(sm[0]["conv"], me * ncv, ncv, axis=2)
    conv_out = None
    for l in reversed(range(NL)):
        conv_out, _ = adamw(gdn_conv, m_gdn_conv, v_gdn_conv, l, f"adamw_conv_l{l}", g=g_conv[l], prev=conv_out)

    def leaf(i, k):
        if k == "w_in":
            return jnp.transpose(big_out[k][i], (1, 2, 0))
        if k in big_out:
            return big_out[k][i]
        if k == "gdn_conv":
            return conv_out[i]
        return sm[i][k]

    order = ("attn_norm", "w_in", "pool_w", "pool_scale", "gdn_conv", "gdn_a_log", "gdn_dt_bias", "gdn_norm", "w_pool_up",
             "w_sb_up", "w_gdn_up", "w_out", "mlp_norm", "w_ff1", "w_ff2", "final_norm")
    return (loss, dh[None]) + tuple(leaf(i, k) for i in range(4) for k in order)
```
